```python
import math
import jax, jax.numpy as jnp
from jax import lax
import numpy as np

D_MODEL = 1024
BATCH = 4
SEQ = 4096
DEPTH = 2
DEC_BATCH = 128
DEC_SEQ = 1
PAST_LEN = 8192
PAGE_SIZE = 128

N_AB_LAYERS = (DEPTH + 1) // 2
N_C_LAYERS = DEPTH // 2
A_HEADS = 8
A_KV_HEADS = 2
A_HEAD_DIM = 64
A_GROUP = A_HEADS // A_KV_HEADS
WINDOW = 128
ATTN_SCALE = A_HEAD_DIM ** -0.5
NUM_BUCKETS = 32
MAX_DISTANCE = 128
A_Q = A_HEADS * A_HEAD_DIM
A_KV = A_KV_HEADS * A_HEAD_DIM
B_GROUPS = 8
B_GROUP_DIM = 64
B_WIDTH = B_GROUPS * B_GROUP_DIM
B_CHUNK = 128
AB_IN = A_Q + 2 * A_KV + 2 * B_WIDTH
AB_MIX = A_Q + B_WIDTH
C_HEADS = 8
C_KEY_DIM = 128
C_VAL_DIM = D_MODEL // C_HEADS
C_F = C_HEADS * C_KEY_DIM
C_V = C_HEADS * C_VAL_DIM
C_IN = 2 * C_F + 2 * C_V
C_CHUNK = 32
D_FF = ((8 * D_MODEL + 767) // 768) * 256
EPS = 1e-6

kernel_name = 'hybrid_swa_gmlp_hgrn2_step'


def _rms(x, g):
    xf = x.astype(jnp.float32)
    y = xf * lax.rsqrt(jnp.mean(xf * xf, axis=-1, keepdims=True) + EPS)
    return (y * g.astype(jnp.float32)).astype(x.dtype)


def _layernorm(x, g, b):
    xf = x.astype(jnp.float32)
    xc = xf - jnp.mean(xf, axis=-1, keepdims=True)
    y = xc * lax.rsqrt(jnp.mean(xc * xc, axis=-1, keepdims=True) + EPS)
    return (y * g.astype(jnp.float32) + b.astype(jnp.float32)).astype(x.dtype)


def _t5_bucket(dist):
    max_exact = NUM_BUCKETS // 2
    d = jnp.maximum(dist, 0)
    dl = jnp.maximum(d, 1).astype(jnp.float32)
    large = max_exact + (jnp.log(dl / max_exact) / math.log(MAX_DISTANCE / max_exact)
                         * (NUM_BUCKETS - max_exact)).astype(jnp.int32)
    large = jnp.minimum(large, NUM_BUCKETS - 1)
    return jnp.where(d < max_exact, d, large)


def _band_bias(dist, ok, rel_bias):
    ok = ok & (dist >= 0) & (dist < WINDOW)
    tab = rel_bias.astype(jnp.float32)[_t5_bucket(dist)]
    tab = jnp.moveaxis(tab, -1, -3)
    b = jnp.where(ok[..., None, :, :], tab, -jnp.inf)
    return b.reshape(b.shape[:-3] + (A_KV_HEADS, A_GROUP) + b.shape[-2:])


def _sink_attend(q, k, v, bias, sink):
    s = jnp.einsum('...qhgd,...khd->...hgqk', q, k, preferred_element_type=jnp.float32) * ATTN_SCALE + bias
    sk = jnp.broadcast_to(sink.astype(jnp.float32).reshape(A_KV_HEADS, A_GROUP, 1, 1), s.shape[:-1] + (1,))
    p = jax.nn.softmax(jnp.concatenate([s, sk], axis=-1), axis=-1)[..., :-1]
    return jnp.einsum('...hgqk,...khd->...qhgd', p.astype(v.dtype), v)


def _ab_project(h, w_in, q_norm, k_norm):
    bn, L = h.shape[:2]
    z = h @ w_in
    q, k, v, zu, zv = jnp.split(z, [A_Q, A_Q + A_KV, A_Q + 2 * A_KV, A_Q + 2 * A_KV + B_WIDTH], axis=-1)
    q = _rms(q.reshape(bn, L, A_HEADS, A_HEAD_DIM), q_norm)
    k = _rms(k.reshape(bn, L, A_KV_HEADS, A_HEAD_DIM), k_norm)
    v = v.reshape(bn, L, A_KV_HEADS, A_HEAD_DIM)
    return q, k, v, zu, zv


def _swa_prompt(q, k, v, rel_bias, sink):
    bn, L = q.shape[:2]
    nb = L // WINDOW
    qb = q.reshape(bn, nb, WINDOW, A_KV_HEADS, A_GROUP, A_HEAD_DIM)

    def band(x):
        xp = jnp.pad(x, ((0, 0), (WINDOW, 0), (0, 0), (0, 0))).reshape(bn, nb + 1, WINDOW, A_KV_HEADS, A_HEAD_DIM)
        return jnp.concatenate([xp[:, :-1], xp[:, 1:]], axis=2)

    qi = jnp.arange(WINDOW)
    kj = jnp.arange(2 * WINDOW)
    dist = qi[:, None] + WINDOW - kj[None, :]
    kpos = jnp.arange(nb)[:, None] * WINDOW - WINDOW + kj[None, :]
    bias = _band_bias(dist, (kpos >= 0)[:, None, :], rel_bias)
    o = _sink_attend(qb, band(k), band(v), bias, sink)
    return o.reshape(bn, L, A_Q)


def _swa_step(q, k_new, v_new, cache_k, cache_v, rel_bias, sink):
    bn, L = q.shape[:2]
    wb = cache_k.shape[1]
    k_all = jnp.concatenate([cache_k.astype(k_new.dtype), k_new], axis=1)
    v_all = jnp.concatenate([cache_v.astype(v_new.dtype), v_new], axis=1)
    qpos = PAST_LEN + jnp.arange(L)
    kpos = PAST_LEN - wb + jnp.arange(wb + L)
    bias = _band_bias(qpos[:, None] - kpos[None, :], (kpos >= 0)[None, :], rel_bias)
    o = _sink_attend(q.reshape(bn, L, A_KV_HEADS, A_GROUP, A_HEAD_DIM), k_all, v_all, bias, sink)
    return o.reshape(bn, L, A_Q), k_all[:, L:], v_all[:, L:]


def _chunk_gmlp(zu, zv, ln_g, ln_b, w_s, b_s):
    bn, L = zu.shape[:2]
    u = jax.nn.gelu(zu, approximate=False)
    v = _layernorm(jax.nn.gelu(zv, approximate=False), ln_g, ln_b)
    nc = -(-L // B_CHUNK)
    pad = nc * B_CHUNK - L
    vp = jnp.pad(v, ((0, 0), (0, pad), (0, 0))).reshape(bn, nc, B_CHUNK, B_GROUPS, B_GROUP_DIM)
    w = jnp.where(jnp.tril(jnp.ones((B_CHUNK, B_CHUNK), bool)), w_s, 0.0).astype(v.dtype)
    s = jnp.einsum('gts,bcsgd->bctgd', w, vp) + b_s.T.astype(v.dtype)[:, :, None]
    s = s.reshape(bn, nc * B_CHUNK, B_WIDTH)[:, :L]
    open_start = ((L - 1) // B_CHUNK) * B_CHUNK
    return u * s, v[:, open_start:]


def _hgrn2_scan(q, fl, i, lb, s0):
    bn, L = q.shape[:2]
    f = lb + (1.0 - lb) * jax.nn.sigmoid(fl.astype(jnp.float32))
    g = jnp.log(f)
    kk = 1.0 - f
    c = min(C_CHUNK, L)
    nc = -(-L // c)
    pad = nc * c - L

    def chunks(x):
        x = jnp.pad(x.astype(jnp.float32), ((0, 0), (0, pad), (0, 0), (0, 0)))
        return x.reshape(bn, nc, c, x.shape[2], x.shape[3]).transpose(1, 0, 3, 2, 4)

    mask = jnp.tril(jnp.ones((c, c), bool))[:, :, None]

    def step(S, xs):
        qc, kc, vc, gc = xs
        b = jnp.cumsum(gc, axis=2)
        o = jnp.einsum('bhtd,bhde->bhte', qc * jnp.exp(b), S)
        decay = jnp.exp(jnp.where(mask, b[:, :, :, None, :] - b[:, :, None, :, :], -jnp.inf))
        att = jnp.einsum('bhtd,bhsd,bhtsd->bhts', qc, kc, decay)
        o = o + jnp.einsum('bhts,bhse->bhte', att, vc)
        b_last = b[:, :, -1:, :]
        S = jnp.exp(b_last[:, :, 0, :])[..., None] * S + jnp.einsum('bhsd,bhse->bhde', kc * jnp.exp(b_last - b), vc)
        return S, o

    S, o = lax.scan(step, s0.astype(jnp.float32), (chunks(q), chunks(kk), chunks(i), chunks(g)))
    o = o.transpose(1, 0, 3, 2, 4).reshape(bn, nc * c, q.shape[2], i.shape[3])[:, :L]
    return o, S


def _hgrn_mixer(h, w_in, lb_l, out_norm, w_out, s0):
    bn, L = h.shape[:2]
    z = h @ w_in
    q, fl, i, gt = jnp.split(z, [C_F, 2 * C_F, 2 * C_F + C_V], axis=-1)
    o, S = _hgrn2_scan(q.reshape(bn, L, C_HEADS, C_KEY_DIM), fl.reshape(bn, L, C_HEADS, C_KEY_DIM),
                       i.reshape(bn, L, C_HEADS, C_VAL_DIM), lb_l.reshape(C_HEADS, C_KEY_DIM), s0)
    o = _rms(o.astype(h.dtype), out_norm) * jax.nn.sigmoid(gt).reshape(bn, L, C_HEADS, C_VAL_DIM)
    return o.reshape(bn, L, C_V) @ w_out, S


def _ffn(x, g, w_gate, w_up, w_down):
    h = _rms(x, g)
    return x + (jax.nn.silu(h @ w_gate) * (h @ w_up)) @ w_down


def setup_inputs(seed: int = 0) -> dict:
    key = jax.random.key(seed)
    ks = jax.random.split(key, 24)

    def nrm(k, shape, scale):
        return scale * jax.random.normal(k, shape, jnp.float32)

    wbuf = min(WINDOW, PAST_LEN)
    return {
        'x_prompt': nrm(ks[0], (BATCH, SEQ, D_MODEL), 1.0),
        'x_sample': nrm(ks[1], (DEC_BATCH, DEC_SEQ, D_MODEL), 1.0),
        'cache_k': nrm(ks[2], (N_AB_LAYERS, DEC_BATCH, wbuf, A_KV_HEADS, A_HEAD_DIM), 1.0),
        'cache_v': nrm(ks[3], (N_AB_LAYERS, DEC_BATCH, wbuf, A_KV_HEADS, A_HEAD_DIM), 1.0),
        'state_hgrn': nrm(ks[4], (N_C_LAYERS, DEC_BATCH, C_HEADS, C_KEY_DIM, C_VAL_DIM), 0.5),
        'norm_mix': 1.0 + nrm(ks[5], (DEPTH, D_MODEL), 0.05),
        'norm_ffn': 1.0 + nrm(ks[6], (DEPTH, D_MODEL), 0.05),
        'w_in_ab': nrm(ks[7], (N_AB_LAYERS, D_MODEL, AB_IN), D_MODEL ** -0.5),
        'w_out_ab': nrm(ks[8], (N_AB_LAYERS, AB_MIX, D_MODEL), AB_MIX ** -0.5),
        'q_norm': 1.0 + nrm(ks[9], (N_AB_LAYERS, A_HEAD_DIM), 0.05),
        'k_norm': 1.0 + nrm(ks[10], (N_AB_LAYERS, A_HEAD_DIM), 0.05),
        'attn_sink': nrm(ks[11], (N_AB_LAYERS, A_HEADS), 0.5),
        'rel_bias': nrm(ks[12], (NUM_BUCKETS, A_HEADS), 0.5),
        'gmlp_ln_g': 1.0 + nrm(ks[13], (N_AB_LAYERS, B_WIDTH), 0.05),
        'gmlp_ln_b': nrm(ks[14], (N_AB_LAYERS, B_WIDTH), 0.02),
        'gmlp_w_s': nrm(ks[15], (N_AB_LAYERS, B_GROUPS, B_CHUNK, B_CHUNK), B_CHUNK ** -0.5),
        'gmlp_b_s': 1.0 + nrm(ks[16], (N_AB_LAYERS, B_GROUPS, B_CHUNK), 0.1),
        'w_in_c': nrm(ks[17], (N_C_LAYERS, D_MODEL, C_IN), D_MODEL ** -0.5),
        'c_lower_bounds': nrm(ks[18], (DEPTH, C_F), 0.1),
        'c_out_norm': 1.0 + nrm(ks[19], (N_C_LAYERS, C_VAL_DIM), 0.05),
        'w_out_c': nrm(ks[20], (N_C_LAYERS, C_V, D_MODEL), C_V ** -0.5),
        'w_gate': nrm(ks[21], (DEPTH, D_MODEL, D_FF), D_MODEL ** -0.5),
        'w_up': nrm(ks[22], (DEPTH, D_MODEL, D_FF), D_MODEL ** -0.5),
        'w_down': nrm(ks[23], (DEPTH, D_FF, D_MODEL), D_FF ** -0.5),
    }


def reference(x_prompt, x_sample, cache_k, cache_v, state_hgrn,
              norm_mix, norm_ffn, w_in_ab, w_out_ab, q_norm, k_norm, attn_sink, rel_bias,
              gmlp_ln_g, gmlp_ln_b, gmlp_w_s, gmlp_b_s,
              w_in_c, c_lower_bounds, c_out_norm, w_out_c,
              w_gate, w_up, w_down):
    lb = jax.nn.softmax(c_lower_bounds.astype(jnp.float32), axis=0)
    lb = jnp.cumsum(lb, axis=0) - lb[0:1]

    xp, xs = x_prompt, x_sample
    kp_l, vp_l, ks_l, vs_l = [], [], [], []
    gvp_l, gvs_l, sp_l, ss_l = [], [], [], []
    for l in range(DEPTH):
        j = l // 2
        hp = _rms(xp, norm_mix[l])
        hs = _rms(xs, norm_mix[l])
        if l % 2 == 0:
            q, k, v, zu, zv = _ab_project(hp, w_in_ab[j], q_norm[j], k_norm[j])
            a = _swa_prompt(q, k, v, rel_bias, attn_sink[j])
            b, gv = _chunk_gmlp(zu, zv, gmlp_ln_g[j], gmlp_ln_b[j], gmlp_w_s[j], gmlp_b_s[j])
            xp = xp + jnp.concatenate([a, b], axis=-1) @ w_out_ab[j]
            nwin = min(WINDOW, k.shape[1])
            kp_l.append(k[:, -nwin:])
            vp_l.append(v[:, -nwin:])
            gvp_l.append(gv)
            q, k, v, zu, zv = _ab_project(hs, w_in_ab[j], q_norm[j], k_norm[j])
            a, kw, vw = _swa_step(q, k, v, cache_k[j], cache_v[j], rel_bias, attn_sink[j])
            b, gv = _chunk_gmlp(zu, zv, gmlp_ln_g[j], gmlp_ln_b[j], gmlp_w_s[j], gmlp_b_s[j])
            xs = xs + jnp.concatenate([a, b], axis=-1) @ w_out_ab[j]
            ks_l.append(kw)
            vs_l.append(vw)
            gvs_l.append(gv)
        else:
            s0 = jnp.zeros((xp.shape[0], C_HEADS, C_KEY_DIM, C_VAL_DIM), jnp.float32)
            o, s = _hgrn_mixer(hp, w_in_c[j], lb[l], c_out_norm[j], w_out_c[j], s0)
            xp = xp + o
            sp_l.append(s.astype(xp.dtype))
            o, s = _hgrn_mixer(hs, w_in_c[j], lb[l], c_out_norm[j], w_out_c[j], state_hgrn[j])
            xs = xs + o
            ss_l.append(s.astype(state_hgrn.dtype))
        xp = _ffn(xp, norm_ffn[l], w_gate[l], w_up[l], w_down[l])
        xs = _ffn(xs, norm_ffn[l], w_gate[l], w_up[l], w_down[l])

    new_k_prompt = jnp.stack(kp_l)
    new_v_prompt = jnp.stack(vp_l)
    new_k_sample = jnp.stack(ks_l)
    new_v_sample = jnp.stack(vs_l)
    gmlp_v_prompt = jnp.stack(gvp_l)
    gmlp_v_sample = jnp.stack(gvs_l)
    state_hgrn_prompt = jnp.stack(sp_l)
    state_hgrn_sample = jnp.stack(ss_l)
    return (xp, xs, new_k_prompt, new_v_prompt, new_k_sample, new_v_sample,
            gmlp_v_prompt, gmlp_v_sample, state_hgrn_prompt, state_hgrn_sample)
```

```python
import functools
import math

import jax
import jax.numpy as jnp
from jax import lax
from jax.experimental import pallas as pl
from jax.experimental.pallas import tpu as pltpu

F32 = jnp.float32
BF16 = jnp.bfloat16

D_MODEL = 1024
DEPTH = 2
A_HEADS = 8
A_KV_HEADS = 2
A_GROUP = A_HEADS // A_KV_HEADS
A_HEAD_DIM = 64
WINDOW = 128
ATTN_SCALE = A_HEAD_DIM ** -0.5
NUM_BUCKETS = 32
MAX_DISTANCE = 128
A_Q = A_HEADS * A_HEAD_DIM
A_KV = A_KV_HEADS * A_HEAD_DIM
B_GROUPS = 8
B_GROUP_DIM = 64
B_WIDTH = B_GROUPS * B_GROUP_DIM
B_CHUNK = 128
AB_IN = A_Q + 2 * A_KV + 2 * B_WIDTH
AB_MIX = A_Q + B_WIDTH
C_HEADS = 8
C_KEY_DIM = 128
C_VAL_DIM = 128
C_F = C_HEADS * C_KEY_DIM
C_V = C_HEADS * C_VAL_DIM
C_IN = 2 * C_F + 2 * C_V
D_FF = 2816
EPS = 1e-6

NEG = -1e30

VMEM_LIMIT = 56 * 1024 * 1024

CHUNK = 128
TQ = 512
TM = 512
SB = 8

_NT = (((1,), (1,)), ((), ()))
_TN = (((0,), (0,)), ((), ()))


def _rms(x, g):
    return x * lax.rsqrt(jnp.mean(x * x, axis=-1, keepdims=True) + EPS) * g


def _gelu(x):
    return 0.5 * x * (1.0 + lax.erf(x * math.sqrt(0.5)))


def _layernorm(x, g, b):
    xc = x - jnp.mean(x, axis=-1, keepdims=True)
    return xc * lax.rsqrt(jnp.mean(xc * xc, axis=-1, keepdims=True) + EPS) * g + b


def _dot(a, b):
    return jnp.dot(a, b, preferred_element_type=F32)


def _full(shape):
    n = len(shape)
    return pl.BlockSpec(shape, lambda *_: (0,) * n)


def _resident(shape):
    n = len(shape)
    return pl.BlockSpec(shape, lambda *_: (0,) * n, pipeline_mode=pl.Buffered(1))


_SMEM = pl.BlockSpec(memory_space=pltpu.SMEM)


def _bias_table_kernel(rel_ref, tab_ref):
    qi = lax.broadcasted_iota(jnp.int32, (WINDOW, 2 * WINDOW), 0)
    kj = lax.broadcasted_iota(jnp.int32, (WINDOW, 2 * WINDOW), 1)
    dist = qi + WINDOW - kj
    ok = (dist >= 0) & (dist < WINDOW)
    max_exact = NUM_BUCKETS // 2
    d = jnp.maximum(dist, 0)
    dl = jnp.maximum(d, 1).astype(F32)
    large = max_exact + (jnp.log(dl / max_exact) / math.log(MAX_DISTANCE / max_exact)
                         * (NUM_BUCKETS - max_exact)).astype(jnp.int32)
    large = jnp.minimum(large, NUM_BUCKETS - 1)
    bucket = jnp.where(d < max_exact, d, large)
    for h in range(A_HEADS):
        acc = jnp.zeros((WINDOW, 2 * WINDOW), F32)
        for b in range(NUM_BUCKETS):
            acc = jnp.where(bucket == b, rel_ref[b, h], acc)
        tab_ref[h] = jnp.where(ok, acc, NEG)


def _bias_table(rel_bias):
    return pl.pallas_call(
        _bias_table_kernel,
        out_shape=jax.ShapeDtypeStruct((A_HEADS, WINDOW, 2 * WINDOW), F32),
        in_specs=[_SMEM],
        name="bias_table",
    )(rel_bias)


def _ab_prompt_kernel(sink_ref, x_ref, nm_ref, win_ref, qn_ref, kn_ref, tab_ref, lng_ref, lnb_ref,
                      ws_ref, bst_ref, wout_ref,
                      y_ref, knew_ref, vnew_ref, gv_ref,
                      z_ref, kprev_ref, vprev_ref, wtril_ref):
    j = pl.program_id(1)
    last_j = pl.num_programs(1) - 1
    n_chunks = TQ // CHUNK

    @pl.when(j == 0)
    def _():
        kprev_ref[...] = jnp.zeros_like(kprev_ref)
        vprev_ref[...] = jnp.zeros_like(vprev_ref)
        row = lax.broadcasted_iota(jnp.int32, (B_CHUNK, B_CHUNK), 0)
        col = lax.broadcasted_iota(jnp.int32, (B_CHUNK, B_CHUNK), 1)
        for g in range(B_GROUPS):
            wtril_ref[g] = jnp.where(row >= col, ws_ref[g], 0.0).astype(BF16)

    h = _rms(x_ref[0], nm_ref[...]).astype(BF16)
    z_ref[...] = _dot(h, win_ref[...])

    prev_half = lax.broadcasted_iota(jnp.int32, (1, 2 * WINDOW), 1) < WINDOW

    def chunk(c, carry):
        r0 = pl.multiple_of(c * CHUNK, CHUNK)
        rows = pl.ds(r0, CHUNK)
        first = jnp.logical_and(j == 0, c == 0)
        maskrow = jnp.where(prev_half, jnp.where(first, NEG, 0.0), 0.0)

        kparts = []
        for g in range(A_KV_HEADS):
            kg = z_ref[rows, A_Q + g * A_HEAD_DIM:A_Q + (g + 1) * A_HEAD_DIM]
            kparts.append(_rms(kg, kn_ref[...]))
        kn = jnp.concatenate(kparts, axis=-1)
        v = z_ref[rows, A_Q + A_KV:A_Q + 2 * A_KV]
        kb = kn.astype(BF16)
        vb = v.astype(BF16)
        k2 = jnp.concatenate([kprev_ref[...], kb], axis=0)
        v2 = jnp.concatenate([vprev_ref[...], vb], axis=0)
        outs = []
        for hh in range(A_HEADS):
            g = hh // A_GROUP
            gs = slice(g * A_HEAD_DIM, (g + 1) * A_HEAD_DIM)
            qh = z_ref[rows, hh * A_HEAD_DIM:(hh + 1) * A_HEAD_DIM]
            qh = _rms(qh, qn_ref[...]) * ATTN_SCALE
            s = lax.dot_general(qh.astype(BF16), k2[:, gs], _NT, preferred_element_type=F32)
            s = s + tab_ref[hh] + maskrow
            sk = sink_ref[hh]
            m = jnp.maximum(jnp.max(s, axis=-1, keepdims=True), sk)
            e = jnp.exp(s - m)
            den = jnp.sum(e, axis=-1, keepdims=True) + jnp.exp(sk - m)
            p = e * (1.0 / den)
            outs.append(_dot(p.astype(BF16), v2[:, gs]))
        a = jnp.concatenate(outs, axis=-1)

        zu = z_ref[rows, A_Q + 2 * A_KV:A_Q + 2 * A_KV + B_WIDTH]
        zv = z_ref[rows, A_Q + 2 * A_KV + B_WIDTH:AB_IN]
        u = _gelu(zu)
        vln = _layernorm(_gelu(zv), lng_ref[...], lnb_ref[...])
        vlb = vln.astype(BF16)
        sparts = []
        for g in range(B_GROUPS):
            gs = slice(g * B_GROUP_DIM, (g + 1) * B_GROUP_DIM)
            sparts.append(_dot(wtril_ref[g], vlb[:, gs]) + bst_ref[:, g:g + 1])
        bm = u * jnp.concatenate(sparts, axis=-1)

        mix = jnp.concatenate([a, bm], axis=-1).astype(BF16)
        y_ref[0, rows, :] = x_ref[0, rows, :] + _dot(mix, wout_ref[...])

        kprev_ref[...] = kb
        vprev_ref[...] = vb

        @pl.when(jnp.logical_and(j == last_j, c == n_chunks - 1))
        def _():
            knew_ref[0] = kn
            vnew_ref[0] = v
            gv_ref[0] = vln

        return carry

    lax.fori_loop(0, n_chunks, chunk, 0)


def _ab_prompt(x, nm, w_in, qn, kn, sink, tab, lng, lnb, w_s, bst, w_out):
    nb, seq, _ = x.shape
    grid = (nb, seq // TQ)
    blk = lambda b, j: (b, j, 0)
    per_b = lambda b, j: (b, 0, 0)
    return pl.pallas_call(
        _ab_prompt_kernel,
        grid=grid,
        in_specs=[
            _SMEM,
            pl.BlockSpec((1, TQ, D_MODEL), blk),
            _full((1, D_MODEL)),
            _resident((D_MODEL, AB_IN)),
            _full((1, A_HEAD_DIM)),
            _full((1, A_HEAD_DIM)),
            _resident((A_HEADS, WINDOW, 2 * WINDOW)),
            _full((1, B_WIDTH)),
            _full((1, B_WIDTH)),
            _resident((B_GROUPS, B_CHUNK, B_CHUNK)),
            _full((B_CHUNK, B_GROUPS)),
            _resident((AB_MIX, D_MODEL)),
        ],
        out_specs=[
            pl.BlockSpec((1, TQ, D_MODEL), blk),
            pl.BlockSpec((1, WINDOW, A_KV), per_b),
            pl.BlockSpec((1, WINDOW, A_KV), per_b),
            pl.BlockSpec((1, B_CHUNK, B_WIDTH), per_b),
        ],
        out_shape=[
            jax.ShapeDtypeStruct((nb, seq, D_MODEL), F32),
            jax.ShapeDtypeStruct((nb, WINDOW, A_KV), F32),
            jax.ShapeDtypeStruct((nb, WINDOW, A_KV), F32),
            jax.ShapeDtypeStruct((nb, B_CHUNK, B_WIDTH), F32),
        ],
        scratch_shapes=[
            pltpu.VMEM((TQ, AB_IN), F32),
            pltpu.VMEM((WINDOW, A_KV), BF16),
            pltpu.VMEM((WINDOW, A_KV), BF16),
            pltpu.VMEM((B_GROUPS, B_CHUNK, B_CHUNK), BF16),
        ],
        compiler_params=pltpu.CompilerParams(
            dimension_semantics=("arbitrary", "arbitrary"), vmem_limit_bytes=VMEM_LIMIT),
        name="ab_prompt",
    )(sink, x, nm, w_in, qn, kn, tab, lng, lnb, w_s, bst, w_out)


def _ffn_kernel(x_ref, g_ref, wg_ref, wu_ref, wd_ref, o_ref):
    x = x_ref[...]
    h = _rms(x, g_ref[...]).astype(BF16)
    gate = _dot(h, wg_ref[...])
    up = _dot(h, wu_ref[...])
    a = (gate * jax.nn.sigmoid(gate) * up).astype(BF16)
    o_ref[...] = x + _dot(a, wd_ref[...])


def _ffn(x, g, wg, wu, wd):
    rows = x.shape[0]
    tm = min(TM, rows)
    blk = lambda i: (i, 0)
    return pl.pallas_call(
        _ffn_kernel,
        grid=(rows // tm,),
        in_specs=[
            pl.BlockSpec((tm, D_MODEL), blk),
            _full((1, D_MODEL)),
            _resident((D_MODEL, D_FF)),
            _resident((D_MODEL, D_FF)),
            _resident((D_FF, D_MODEL)),
        ],
        out_specs=pl.BlockSpec((tm, D_MODEL), blk),
        out_shape=jax.ShapeDtypeStruct((rows, D_MODEL), F32),
        compiler_params=pltpu.CompilerParams(
            dimension_semantics=("arbitrary",), vmem_limit_bytes=VMEM_LIMIT),
        name="ffn",
    )(x, g, wg, wu, wd)


def _lower_bound(clb):
    m = jnp.max(clb, axis=0, keepdims=True)
    e = jnp.exp(clb - m)
    sm = e / jnp.sum(e, axis=0, keepdims=True)
    return (sm[0:1] + sm[1:2]) - sm[0:1]


def _cumsum_rows(x):
    row = lax.broadcasted_iota(jnp.int32, x.shape, 0)
    s = 1
    while s < x.shape[0]:
        x = x + jnp.where(row >= s, pltpu.roll(x, s, 0), 0.0)
        s *= 2
    return x


def _block_row(b, m):
    n = b.shape[0] // (2 * m)
    r = b.reshape(n, 2 * m, b.shape[1])[:, m - 1:m, :]
    return jnp.broadcast_to(r, (n, 2 * m, b.shape[1])).reshape(b.shape)


def _hgrn_prompt_kernel(x_ref, nm_ref, win_ref, clb_ref, on_ref, wout_ref,
                        y_ref, st_ref,
                        z_ref, o_ref, stt_ref):
    j = pl.program_id(1)
    last_j = pl.num_programs(1) - 1
    n_chunks = TQ // CHUNK

    @pl.when(j == 0)
    def _():
        stt_ref[...] = jnp.zeros_like(stt_ref)

    h = _rms(x_ref[0], nm_ref[...]).astype(BF16)
    z_ref[...] = _dot(h, win_ref[...])
    lb = _lower_bound(clb_ref[...])

    row = lax.broadcasted_iota(jnp.int32, (CHUNK, CHUNK), 0)
    col = lax.broadcasted_iota(jnp.int32, (CHUNK, CHUNK), 1)
    levels = [2 ** p for p in range(int(math.log2(CHUNK)))]

    def chunk(c, carry):
        r0 = pl.multiple_of(c * CHUNK, CHUNK)
        rows = pl.ds(r0, CHUNK)
        for hd in range(C_HEADS):
            ks = slice(hd * C_KEY_DIM, (hd + 1) * C_KEY_DIM)
            q = z_ref[rows, hd * C_KEY_DIM:(hd + 1) * C_KEY_DIM]
            fl = z_ref[rows, C_F + hd * C_KEY_DIM:C_F + (hd + 1) * C_KEY_DIM]
            iv = z_ref[rows, 2 * C_F + hd * C_VAL_DIM:2 * C_F + (hd + 1) * C_VAL_DIM]
            gt = z_ref[rows, 2 * C_F + C_V + hd * C_VAL_DIM:2 * C_F + C_V + (hd + 1) * C_VAL_DIM]
            lbh = lb[:, ks]
            f = lbh + (1.0 - lbh) * jax.nn.sigmoid(fl)
            kk = 1.0 - f
            b = _cumsum_rows(jnp.log(f))

            att = jnp.where(row == col, jnp.sum(q * kk, axis=-1, keepdims=True), 0.0)
            for p, m in enumerate(levels):
                upper = ((row >> p) & 1) == 1
                be = _block_row(b, m)
                dec = jnp.exp(jnp.where(upper, b - be, be - b))
                qt = jnp.where(upper, q * dec, 0.0).astype(BF16)
                kt = jnp.where(upper, 0.0, kk * dec).astype(BF16)
                pm = lax.dot_general(qt, kt, _NT, preferred_element_type=F32)
                att = att + jnp.where((row >> (p + 1)) == (col >> (p + 1)), pm, 0.0)

            ivb = iv.astype(BF16)
            stt = stt_ref[hd]
            o = lax.dot_general((q * jnp.exp(b)).astype(BF16), stt.astype(BF16), _NT,
                                preferred_element_type=F32)
            o = o + _dot(att.astype(BF16), ivb)
            b_last = b[CHUNK - 1:CHUNK, :]
            kd = (kk * jnp.exp(b_last - b)).astype(BF16)
            stt_ref[hd] = stt * jnp.exp(b_last) + lax.dot_general(
                ivb, kd, _TN, preferred_element_type=F32)
            o = _rms(o, on_ref[...]) * jax.nn.sigmoid(gt)
            o_ref[rows, hd * C_VAL_DIM:(hd + 1) * C_VAL_DIM] = o.astype(BF16)
        return carry

    lax.fori_loop(0, n_chunks, chunk, 0)
    y_ref[0] = x_ref[0] + _dot(o_ref[...], wout_ref[...])

    @pl.when(j == last_j)
    def _():
        for hd in range(C_HEADS):
            st_ref[0, hd] = stt_ref[hd].T


def _hgrn_prompt(x, nm, w_in, clb, on, w_out):
    nb, seq, _ = x.shape
    grid = (nb, seq // TQ)
    blk = lambda b, j: (b, j, 0)
    return pl.pallas_call(
        _hgrn_prompt_kernel,
        grid=grid,
        in_specs=[
            pl.BlockSpec((1, TQ, D_MODEL), blk),
            _full((1, D_MODEL)),
            _resident((D_MODEL, C_IN)),
            _full((DEPTH, C_F)),
            _full((1, C_VAL_DIM)),
            _resident((C_V, D_MODEL)),
        ],
        out_specs=[
            pl.BlockSpec((1, TQ, D_MODEL), blk),
            pl.BlockSpec((1, C_HEADS, C_KEY_DIM, C_VAL_DIM), lambda b, j: (b, 0, 0, 0)),
        ],
        out_shape=[
            jax.ShapeDtypeStruct((nb, seq, D_MODEL), F32),
            jax.ShapeDtypeStruct((nb, C_HEADS, C_KEY_DIM, C_VAL_DIM), F32),
        ],
        scratch_shapes=[
            pltpu.VMEM((TQ, C_IN), F32),
            pltpu.VMEM((TQ, C_V), BF16),
            pltpu.VMEM((C_HEADS, C_VAL_DIM, C_KEY_DIM), F32),
        ],
        compiler_params=pltpu.CompilerParams(
            dimension_semantics=("arbitrary", "arbitrary"), vmem_limit_bytes=VMEM_LIMIT),
        name="hgrn_prompt",
    )(x, nm, w_in, clb, on, w_out)


def _ab_sample_proj_kernel(w00_ref, b0_ref, x_ref, nm_ref, win_ref, qn_ref, kn_ref, lng_ref, lnb_ref,
                           qx_ref, knew_ref, vnew_ref, bm_ref, gv_ref):
    n = x_ref.shape[0]
    h = _rms(x_ref[...], nm_ref[...]).astype(BF16)
    z = _dot(h, win_ref[...])
    zeros = jnp.zeros((n, A_HEAD_DIM), F32)
    for hh in range(A_HEADS):
        qh = _rms(z[:, hh * A_HEAD_DIM:(hh + 1) * A_HEAD_DIM], qn_ref[...]) * ATTN_SCALE
        qx_ref[hh] = jnp.concatenate([qh, zeros] if hh // A_GROUP == 0 else [zeros, qh], axis=-1)
    kparts = []
    for g in range(A_KV_HEADS):
        kparts.append(_rms(z[:, A_Q + g * A_HEAD_DIM:A_Q + (g + 1) * A_HEAD_DIM], kn_ref[...]))
    knew_ref[...] = jnp.concatenate(kparts, axis=-1)
    vnew_ref[...] = z[:, A_Q + A_KV:A_Q + 2 * A_KV]

    u = _gelu(z[:, A_Q + 2 * A_KV:A_Q + 2 * A_KV + B_WIDTH])
    vln = _layernorm(_gelu(z[:, A_Q + 2 * A_KV + B_WIDTH:AB_IN]), lng_ref[...], lnb_ref[...])
    grp = lax.broadcasted_iota(jnp.int32, (1, B_WIDTH), 1) // B_GROUP_DIM
    srow = jnp.zeros((1, B_WIDTH), F32)
    brow = jnp.zeros((1, B_WIDTH), F32)
    for g in range(B_GROUPS):
        srow = jnp.where(grp == g, w00_ref[g], srow)
        brow = jnp.where(grp == g, b0_ref[g], brow)
    bm_ref[...] = u * (vln * srow + brow)
    gv_ref[...] = vln


def _ab_sample_proj(x, nm, w_in, qn, kn, lng, lnb, w00, b0):
    n = x.shape[0]
    return pl.pallas_call(
        _ab_sample_proj_kernel,
        in_specs=[_SMEM, _SMEM] + [pl.BlockSpec(memory_space=pltpu.VMEM)] * 7,
        out_shape=[
            jax.ShapeDtypeStruct((A_HEADS, n, A_KV), F32),
            jax.ShapeDtypeStruct((n, A_KV), F32),
            jax.ShapeDtypeStruct((n, A_KV), F32),
            jax.ShapeDtypeStruct((n, B_WIDTH), F32),
            jax.ShapeDtypeStruct((n, B_WIDTH), F32),
        ],
        compiler_params=pltpu.CompilerParams(vmem_limit_bytes=VMEM_LIMIT),
        name="ab_sample_proj",
    )(w00, b0, x, nm, w_in, qn, kn, lng, lnb)


def _ab_sample_attn_kernel(ck_ref, cv_ref, qx_ref, kn_ref, vn_ref, sb_ref, sink_ref,
                           nk_ref, nv_ref, om_ref):
    wb = ck_ref.shape[1]
    head = lax.broadcasted_iota(jnp.int32, (A_HEADS, A_KV), 0)
    lane = lax.broadcasted_iota(jnp.int32, (A_HEADS, A_KV), 1)
    own_group = (head // A_GROUP) == (lane // A_HEAD_DIM)
    sink = sink_ref[...]
    for i in range(SB):
        kc = ck_ref[i]
        vc = cv_ref[i]
        kn = kn_ref[i:i + 1, :]
        vn = vn_ref[i:i + 1, :]
        nk_ref[i, 0:wb - 1, :] = kc[1:wb, :]
        nk_ref[i, wb - 1:wb, :] = kn
        nv_ref[i, 0:wb - 1, :] = vc[1:wb, :]
        nv_ref[i, wb - 1:wb, :] = vn
        q = qx_ref[i]
        s = lax.dot_general(q.astype(BF16), kc.astype(BF16), _NT, preferred_element_type=F32)
        s = s + sb_ref[:, 0:wb]
        sn = jnp.sum(q * kn, axis=-1, keepdims=True) + sb_ref[:, wb:wb + 1]
        m = jnp.maximum(jnp.maximum(jnp.max(s, axis=-1, keepdims=True), sn), sink)
        e = jnp.exp(s - m)
        en = jnp.exp(sn - m)
        r = 1.0 / (jnp.sum(e, axis=-1, keepdims=True) + en + jnp.exp(sink - m))
        o = _dot((e * r).astype(BF16), vc.astype(BF16)) + (en * r) * vn
        om_ref[i] = jnp.where(own_group, o, 0.0)


def _ab_sample_attn(ck, cv, qx, kn, vn, sb, sink):
    n, wb, _ = ck.shape
    blk3 = lambda i: (i, 0, 0)
    blk2 = lambda i: (i, 0)
    return pl.pallas_call(
        _ab_sample_attn_kernel,
        grid=(n // SB,),
        in_specs=[
            pl.BlockSpec((SB, wb, A_KV), blk3),
            pl.BlockSpec((SB, wb, A_KV), blk3),
            pl.BlockSpec((SB, A_HEADS, A_KV), blk3),
            pl.BlockSpec((SB, A_KV), blk2),
            pl.BlockSpec((SB, A_KV), blk2),
            _full((A_HEADS, 2 * WINDOW)),
            _full((A_HEADS, 1)),
        ],
        out_specs=[
            pl.BlockSpec((SB, wb, A_KV), blk3),
            pl.BlockSpec((SB, wb, A_KV), blk3),
            pl.BlockSpec((SB, A_HEADS, A_KV), blk3),
        ],
        out_shape=[
            jax.ShapeDtypeStruct((n, wb, A_KV), F32),
            jax.ShapeDtypeStruct((n, wb, A_KV), F32),
            jax.ShapeDtypeStruct((n, A_HEADS, A_KV), F32),
        ],
        compiler_params=pltpu.CompilerParams(dimension_semantics=("arbitrary",)),
        name="ab_sample_attn",
    )(ck, cv, qx, kn, vn, sb, sink)


def _residual_proj_kernel(x_ref, m_ref, w_ref, y_ref):
    y_ref[...] = x_ref[...] + _dot(m_ref[...].astype(BF16), w_ref[...])


def _residual_proj(x, mix, w):
    return pl.pallas_call(
        _residual_proj_kernel,
        out_shape=jax.ShapeDtypeStruct(x.shape, F32),
        name="residual_proj",
    )(x, mix, w)


def _hgrn_sample_proj_kernel(x_ref, nm_ref, win_ref, clb_ref, q_ref, f_ref, k_ref, i_ref, sg_ref):
    h = _rms(x_ref[...], nm_ref[...]).astype(BF16)
    z = _dot(h, win_ref[...])
    lb = _lower_bound(clb_ref[...])
    f = lb + (1.0 - lb) * jax.nn.sigmoid(z[:, C_F:2 * C_F])
    q_ref[...] = z[:, 0:C_F]
    f_ref[...] = f
    k_ref[...] = 1.0 - f
    i_ref[...] = z[:, 2 * C_F:2 * C_F + C_V]
    sg_ref[...] = jax.nn.sigmoid(z[:, 2 * C_F + C_V:C_IN])


def _hgrn_sample_proj(x, nm, w_in, clb):
    n = x.shape[0]
    return pl.pallas_call(
        _hgrn_sample_proj_kernel,
        out_shape=[jax.ShapeDtypeStruct((n, C_F), F32)] * 3 + [jax.ShapeDtypeStruct((n, C_V), F32)] * 2,
        compiler_params=pltpu.CompilerParams(vmem_limit_bytes=VMEM_LIMIT),
        name="hgrn_sample_proj",
    )(x, nm, w_in, clb)


def _hgrn_sample_state_kernel(s_ref, qc_ref, fc_ref, kc_ref, i_ref, sn_ref, o_ref):
    out_rows = []
    for s in range(SB):
        parts = []
        for hd in range(C_HEADS):
            st = s_ref[s, hd]
            irow = i_ref[s:s + 1, hd * C_VAL_DIM:(hd + 1) * C_VAL_DIM]
            sn = fc_ref[0, hd, :, s:s + 1] * st + kc_ref[0, hd, :, s:s + 1] * irow
            sn_ref[s, hd] = sn
            parts.append(jnp.sum(qc_ref[0, hd, :, s:s + 1] * sn, axis=0, keepdims=True))
        out_rows.append(jnp.concatenate(parts, axis=-1))
    o_ref[...] = jnp.concatenate(out_rows, axis=0)


def _hgrn_sample_state(state, qc, fc, kc, iv):
    n = state.shape[0]
    sblk = pl.BlockSpec((SB, C_HEADS, C_KEY_DIM, C_VAL_DIM), lambda i: (i, 0, 0, 0))
    cblk = pl.BlockSpec((1, C_HEADS, C_KEY_DIM, SB), lambda i: (i, 0, 0, 0))
    return pl.pallas_call(
        _hgrn_sample_state_kernel,
        grid=(n // SB,),
        in_specs=[sblk, cblk, cblk, cblk, pl.BlockSpec((SB, C_V), lambda i: (i, 0))],
        out_specs=[sblk, pl.BlockSpec((SB, C_V), lambda i: (i, 0))],
        out_shape=[
            jax.ShapeDtypeStruct(state.shape, F32),
            jax.ShapeDtypeStruct((n, C_V), F32),
        ],
        compiler_params=pltpu.CompilerParams(
            dimension_semantics=("arbitrary",), vmem_limit_bytes=VMEM_LIMIT),
        name="hgrn_sample_state",
    )(state, qc, fc, kc, iv)


def _hgrn_sample_out_kernel(o_ref, sg_ref, on_ref, w_ref, x_ref, y_ref):
    parts = []
    for hd in range(C_HEADS):
        parts.append(_rms(o_ref[:, hd * C_VAL_DIM:(hd + 1) * C_VAL_DIM], on_ref[...]))
    on = jnp.concatenate(parts, axis=-1) * sg_ref[...]
    y_ref[...] = x_ref[...] + _dot(on.astype(BF16), w_ref[...])


def _hgrn_sample_out(o, sg, on, w_out, x):
    return pl.pallas_call(
        _hgrn_sample_out_kernel,
        out_shape=jax.ShapeDtypeStruct(x.shape, F32),
        name="hgrn_sample_out",
    )(o, sg, on, w_out, x)


def _to_columns(a):
    n = a.shape[0]
    return a.reshape(n // SB, SB, C_HEADS, C_KEY_DIM).transpose(0, 2, 3, 1)


def kernel(x_prompt, x_sample, cache_k, cache_v, state_hgrn, norm_mix, norm_ffn, w_in_ab, w_out_ab,
           q_norm, k_norm, attn_sink, rel_bias, gmlp_ln_g, gmlp_ln_b, gmlp_w_s, gmlp_b_s,
           w_in_c, c_lower_bounds, c_out_norm, w_out_c, w_gate, w_up, w_down):
    assert norm_mix.shape[0] == DEPTH == 2 and w_in_ab.shape[0] == 1 and w_in_c.shape[0] == 1
    nb, seq, _ = x_prompt.shape
    ns = x_sample.shape[0]
    assert x_sample.shape[1] == 1 and cache_k.shape[2] == WINDOW

    row = lambda v: v.reshape(1, -1)
    bf = lambda w: w.astype(BF16)
    w_in_ab0, w_out_ab0 = bf(w_in_ab[0]), bf(w_out_ab[0])
    w_in_c0, w_out_c0 = bf(w_in_c[0]), bf(w_out_c[0])
    wg, wu, wd = bf(w_gate), bf(w_up), bf(w_down)
    nm, nf = norm_mix, norm_ffn
    qn, kn = row(q_norm[0]), row(k_norm[0])
    lng, lnb = row(gmlp_ln_g[0]), row(gmlp_ln_b[0])
    sink = attn_sink[0]

    tab = _bias_table(rel_bias)

    xp, knew_p, vnew_p, gv_p = _ab_prompt(
        x_prompt, row(nm[0]), w_in_ab0, qn, kn, sink, tab, lng, lnb,
        gmlp_w_s[0], gmlp_b_s[0].T, w_out_ab0)
    xp = _ffn(xp.reshape(nb * seq, D_MODEL), row(nf[0]), wg[0], wu[0], wd[0]).reshape(nb, seq, D_MODEL)
    xp, st_p = _hgrn_prompt(xp, row(nm[1]), w_in_c0, c_lower_bounds, row(c_out_norm[0]), w_out_c0)
    xp = _ffn(xp.reshape(nb * seq, D_MODEL), row(nf[1]), wg[1], wu[1], wd[1]).reshape(nb, seq, D_MODEL)

    xs = x_sample.reshape(ns, D_MODEL)
    qx, knew_s, vnew_s, bm_s, gv_s = _ab_sample_proj(
        xs, row(nm[0]), w_in_ab0, qn, kn, lng, lnb, gmlp_w_s[0, :, 0, 0], gmlp_b_s[0, :, 0])
    sb = jnp.pad(tab[:, WINDOW - 1, WINDOW - 1:], ((0, 0), (0, WINDOW - 1)))
    nk_s, nv_s, om = _ab_sample_attn(
        cache_k[0].reshape(ns, WINDOW, A_KV), cache_v[0].reshape(ns, WINDOW, A_KV),
        qx.transpose(1, 0, 2), knew_s, vnew_s, sb, sink.reshape(A_HEADS, 1))
    om = om.reshape(ns, A_KV_HEADS, A_GROUP, A_KV_HEADS, A_HEAD_DIM)
    a_s = jnp.stack([om[:, g, :, g, :] for g in range(A_KV_HEADS)], axis=1).reshape(ns, A_Q)
    xs = _residual_proj(xs, jnp.concatenate([a_s, bm_s], axis=-1), w_out_ab0)
    xs = _ffn(xs, row(nf[0]), wg[0], wu[0], wd[0])

    q_s, f_s, k_s, i_s, sg_s = _hgrn_sample_proj(xs, row(nm[1]), w_in_c0, c_lower_bounds)
    st_s, o_s = _hgrn_sample_state(state_hgrn[0], _to_columns(q_s), _to_columns(f_s), _to_columns(k_s), i_s)
    xs = _hgrn_sample_out(o_s, sg_s, row(c_out_norm[0]), w_out_c0, xs)
    xs = _ffn(xs, row(nf[1]), wg[1], wu[1], wd[1])

    kv5 = lambda a: a.reshape(1, a.shape[0], WINDOW, A_KV_HEADS, A_HEAD_DIM)
    return (xp, xs.reshape(ns, 1, D_MODEL),
            kv5(knew_p), kv5(vnew_p), kv5(nk_s), kv5(nv_s),
            gv_p[None], gv_s.reshape(1, ns, 1, B_WIDTH),
            st_p[None], st_s[None])
```

```python
import math

import jax
import jax.numpy as jnp
import numpy as np
from jax import lax
from jax.experimental import pallas as pl
from jax.experimental.pallas import tpu as pltpu

F32 = jnp.float32
BF16 = jnp.bfloat16

D_MODEL = 1024
DEPTH = 2
A_HEADS = 8
A_KV_HEADS = 2
A_GROUP = A_HEADS // A_KV_HEADS
A_HEAD_DIM = 64
WINDOW = 128
ATTN_SCALE = A_HEAD_DIM ** -0.5
NUM_BUCKETS = 32
MAX_DISTANCE = 128
A_Q = A_HEADS * A_HEAD_DIM
A_KV = A_KV_HEADS * A_HEAD_DIM
B_GROUPS = 8
B_GROUP_DIM = 64
B_WIDTH = B_GROUPS * B_GROUP_DIM
B_CHUNK = 128
AB_IN = A_Q + 2 * A_KV + 2 * B_WIDTH
AB_MIX = A_Q + B_WIDTH
C_HEADS = 8
C_KEY_DIM = 128
C_VAL_DIM = 128
C_F = C_HEADS * C_KEY_DIM
C_V = C_HEADS * C_VAL_DIM
C_IN = 2 * C_F + 2 * C_V
D_FF = 2816
EPS = 1e-6

NEG = -1e30

VMEM_LIMIT = 56 * 1024 * 1024
VREG_ROWS = 8

CHUNK = 128
TQ = 512
TM = 512
SB = 8
HEAD_SKEW = 2

_NT = (((1,), (1,)), ((), ()))
_TN = (((0,), (0,)), ((), ()))


def _rms(x, g):
    return x * lax.rsqrt(jnp.mean(x * x, axis=-1, keepdims=True) + EPS) * g


def _gelu(x):
    return 0.5 * x * (1.0 + lax.erf(x * math.sqrt(0.5)))


def _layernorm(x, g, b):
    xc = x - jnp.mean(x, axis=-1, keepdims=True)
    return xc * lax.rsqrt(jnp.mean(xc * xc, axis=-1, keepdims=True) + EPS) * g + b


def _dot(a, b):
    return jnp.dot(a, b, preferred_element_type=F32)


def _full(shape):
    n = len(shape)
    return pl.BlockSpec(shape, lambda *_: (0,) * n)


def _resident(shape):
    n = len(shape)
    return pl.BlockSpec(shape, lambda *_: (0,) * n, pipeline_mode=pl.Buffered(1))


_SMEM = pl.BlockSpec(memory_space=pltpu.SMEM)


def _bias_table_kernel(rel_ref, tab_ref, tabp_ref):
    qi = lax.broadcasted_iota(jnp.int32, (WINDOW, 2 * WINDOW), 0)
    kj = lax.broadcasted_iota(jnp.int32, (WINDOW, 2 * WINDOW), 1)
    dist = qi + WINDOW - kj
    ok = (dist >= 0) & (dist < WINDOW)
    max_exact = NUM_BUCKETS // 2
    d = jnp.maximum(dist, 0)
    dl = jnp.maximum(d, 1).astype(F32)
    large = max_exact + (jnp.log(dl / max_exact) / math.log(MAX_DISTANCE / max_exact)
                         * (NUM_BUCKETS - max_exact)).astype(jnp.int32)
    large = jnp.minimum(large, NUM_BUCKETS - 1)
    bucket = jnp.where(d < max_exact, d, large)
    for h in range(A_HEADS):
        acc = jnp.zeros((WINDOW, 2 * WINDOW), F32)
        for b in range(NUM_BUCKETS):
            acc = jnp.where(bucket == b, rel_ref[b, h], acc)
        t = jnp.where(ok, acc, NEG)
        tab_ref[h] = t
        cols = slice((h % 2) * 2 * WINDOW, (h % 2 + 1) * 2 * WINDOW)
        tabp_ref[0, h // 2, :, cols] = t
        tabp_ref[1, h // 2, :, cols] = jnp.where(kj < WINDOW, NEG, t)


def _bias_table(rel_bias):
    return pl.pallas_call(
        _bias_table_kernel,
        out_shape=[
            jax.ShapeDtypeStruct((A_HEADS, WINDOW, 2 * WINDOW), F32),
            jax.ShapeDtypeStruct((2, A_HEADS // 2, WINDOW, 4 * WINDOW), F32),
        ],
        in_specs=[_SMEM],
        name="bias_table",
    )(rel_bias)


PAIR = 2 * A_HEAD_DIM
N_PAIRS = A_HEADS // 2


def _ab_prompt_kernel(sink_ref, x_ref, nm_ref, win_ref, qg_ref, kg_ref, tabp_ref, lng_ref, lnb_ref,
                      ws_ref, bsp_ref, wout_ref,
                      y_ref, knew_ref, vnew_ref, gv_ref,
                      z_ref, mix_ref, kp_ref, kpr_ref, vp_ref, vpr_ref, wpair_ref):
    j = pl.program_id(1)
    last_j = pl.num_programs(1) - 1
    n_chunks = TQ // CHUNK

    @pl.when(j == 0)
    def _():
        for ref in (kp_ref, kpr_ref, vp_ref, vpr_ref):
            ref[...] = jnp.zeros_like(ref)
        row = lax.broadcasted_iota(jnp.int32, (B_CHUNK, B_CHUNK), 0)
        col = lax.broadcasted_iota(jnp.int32, (B_CHUNK, B_CHUNK), 1)
        for g in range(B_GROUPS):
            wpair_ref[g // 2, :, (g % 2) * B_CHUNK:(g % 2 + 1) * B_CHUNK] = jnp.where(
                row >= col, ws_ref[g], 0.0).astype(BF16)

    h = _rms(x_ref[0], nm_ref[...]).astype(BF16)
    z_ref[...] = _dot(h, win_ref[...])

    lo_half = lax.broadcasted_iota(jnp.int32, (1, PAIR), 1) < A_HEAD_DIM
    r_i = lax.broadcasted_iota(jnp.int32, (PAIR, PAIR), 0) // A_HEAD_DIM
    c_i = lax.broadcasted_iota(jnp.int32, (PAIR, PAIR), 1) // A_HEAD_DIM
    half_mean = jnp.where(r_i == c_i, 1.0 / A_HEAD_DIM, 0.0).astype(BF16)

    def mean_sq_halves(x):
        x2 = x * x
        hi = x2.astype(BF16)
        lo = (x2 - hi.astype(F32)).astype(BF16)
        return _dot(hi, half_mean) + _dot(lo, half_mean)

    def block_diag(top, bot):
        zero = jnp.zeros_like(top)
        return jnp.concatenate([jnp.where(lo_half, top, zero), jnp.where(lo_half, zero, bot)], axis=0)

    def chunk(c, carry):
        r0 = pl.multiple_of(c * CHUNK, CHUNK)
        rows = pl.ds(r0, CHUNK)
        first = jnp.where(jnp.logical_and(j == 0, c == 0), 1, 0)

        kraw = z_ref[rows, A_Q:A_Q + A_KV]
        v = z_ref[rows, A_Q + A_KV:A_Q + 2 * A_KV]
        kn = kraw * lax.rsqrt(mean_sq_halves(kraw) + EPS) * kg_ref[...]
        kb, kbr = kn.astype(BF16), pltpu.roll(kn, A_HEAD_DIM, 1).astype(BF16)
        vb, vbr = v.astype(BF16), pltpu.roll(v, A_HEAD_DIM, 1).astype(BF16)
        k2 = jnp.concatenate([kp_ref[...], kb], axis=0)
        k2r = jnp.concatenate([kpr_ref[...], kbr], axis=0)
        v2 = jnp.concatenate([vp_ref[...], vb], axis=0)
        v2r = jnp.concatenate([vpr_ref[...], vbr], axis=0)
        kbd = [block_diag(k2, k2r), block_diag(k2r, k2)]
        vbd = [block_diag(v2, v2r), block_diag(v2r, v2)]

        scores = []
        for i in range(N_PAIRS):
            ps = slice(i * PAIR, (i + 1) * PAIR)
            qraw = z_ref[rows, i * PAIR:(i + 1) * PAIR]
            qn = qraw * lax.rsqrt(mean_sq_halves(qraw) + EPS) * (qg_ref[:, ps] * ATTN_SCALE)
            s = lax.dot_general(qn.astype(BF16), kbd[i // (A_GROUP // 2)], _NT,
                                preferred_element_type=F32)
            scores.append(s + tabp_ref[first, i])
        outs = []
        for i in range(N_PAIRS):
            es, rs = [], []
            for hh in range(2):
                sh = scores[i][:, hh * 2 * WINDOW:(hh + 1) * 2 * WINDOW]
                sk = sink_ref[2 * i + hh]
                m = jnp.maximum(jnp.max(sh, axis=-1, keepdims=True), sk)
                e = jnp.exp(sh - m)
                rs.append(1.0 / (jnp.sum(e, axis=-1, keepdims=True) + jnp.exp(sk - m)))
                es.append(e.astype(BF16))
            o = _dot(jnp.concatenate(es, axis=-1), vbd[i // (A_GROUP // 2)])
            outs.append(o * jnp.where(lo_half, rs[0], rs[1]))
        mix_ref[rows, 0:A_Q] = jnp.concatenate(outs, axis=-1).astype(BF16)

        zu = z_ref[rows, A_Q + 2 * A_KV:A_Q + 2 * A_KV + B_WIDTH]
        zv = z_ref[rows, A_Q + 2 * A_KV + B_WIDTH:AB_IN]
        u = _gelu(zu)
        vln = _layernorm(_gelu(zv), lng_ref[...], lnb_ref[...])
        vlb = vln.astype(BF16)
        sparts = []
        for i in range(B_GROUPS // 2):
            vpair = vlb[:, i * PAIR:(i + 1) * PAIR]
            sparts.append(_dot(wpair_ref[i], block_diag(vpair, vpair)))
        bm = u * (jnp.concatenate(sparts, axis=-1) + bsp_ref[...])
        mix_ref[rows, A_Q:AB_MIX] = bm.astype(BF16)

        kp_ref[...] = kb
        kpr_ref[...] = kbr
        vp_ref[...] = vb
        vpr_ref[...] = vbr

        @pl.when(jnp.logical_and(j == last_j, c == n_chunks - 1))
        def _():
            knew_ref[0] = kn
            vnew_ref[0] = v
            gv_ref[0] = vln

        return carry

    lax.fori_loop(0, n_chunks, chunk, 0)
    y_ref[0] = x_ref[0] + _dot(mix_ref[...], wout_ref[...])


def _ab_prompt(x, nm, w_in, qg, kg, sink, tabp, lng, lnb, w_s, bsp, w_out):
    nb, seq, _ = x.shape
    grid = (nb, seq // TQ)
    blk = lambda b, j: (b, j, 0)
    per_b = lambda b, j: (b, 0, 0)
    return pl.pallas_call(
        _ab_prompt_kernel,
        grid=grid,
        in_specs=[
            _SMEM,
            pl.BlockSpec((1, TQ, D_MODEL), blk),
            _full((1, D_MODEL)),
            _resident((D_MODEL, AB_IN)),
            _full((1, A_Q)),
            _full((1, A_KV)),
            _resident((2, N_PAIRS, WINDOW, 4 * WINDOW)),
            _full((1, B_WIDTH)),
            _full((1, B_WIDTH)),
            _resident((B_GROUPS, B_CHUNK, B_CHUNK)),
            _resident((B_CHUNK, B_WIDTH)),
            _resident((AB_MIX, D_MODEL)),
        ],
        out_specs=[
            pl.BlockSpec((1, TQ, D_MODEL), blk),
            pl.BlockSpec((1, WINDOW, A_KV), per_b),
            pl.BlockSpec((1, WINDOW, A_KV), per_b),
            pl.BlockSpec((1, B_CHUNK, B_WIDTH), per_b),
        ],
        out_shape=[
            jax.ShapeDtypeStruct((nb, seq, D_MODEL), F32),
            jax.ShapeDtypeStruct((nb, WINDOW, A_KV), F32),
            jax.ShapeDtypeStruct((nb, WINDOW, A_KV), F32),
            jax.ShapeDtypeStruct((nb, B_CHUNK, B_WIDTH), F32),
        ],
        scratch_shapes=[
            pltpu.VMEM((TQ, AB_IN), F32),
            pltpu.VMEM((TQ, AB_MIX), BF16),
            pltpu.VMEM((WINDOW, A_KV), BF16),
            pltpu.VMEM((WINDOW, A_KV), BF16),
            pltpu.VMEM((WINDOW, A_KV), BF16),
            pltpu.VMEM((WINDOW, A_KV), BF16),
            pltpu.VMEM((B_GROUPS // 2, B_CHUNK, 2 * B_CHUNK), BF16),
        ],
        compiler_params=pltpu.CompilerParams(
            dimension_semantics=("arbitrary", "arbitrary"), vmem_limit_bytes=VMEM_LIMIT),
        name="ab_prompt",
    )(sink, x, nm, w_in, qg, kg, tabp, lng, lnb, w_s, bsp, w_out)


def _ffn_kernel(x_ref, g_ref, wg_ref, wu_ref, wd_ref, o_ref):
    x = x_ref[...]
    h = _rms(x, g_ref[...]).astype(BF16)
    gate = _dot(h, wg_ref[...])
    up = _dot(h, wu_ref[...])
    a = (gate * jax.nn.sigmoid(gate) * up).astype(BF16)
    o_ref[...] = x + _dot(a, wd_ref[...])


def _ffn(x, g, wg, wu, wd):
    rows = x.shape[0]
    tm = min(TM, rows)
    blk = lambda i: (i, 0)
    return pl.pallas_call(
        _ffn_kernel,
        grid=(rows // tm,),
        in_specs=[
            pl.BlockSpec((tm, D_MODEL), blk),
            _full((1, D_MODEL)),
            _resident((D_MODEL, D_FF)),
            _resident((D_MODEL, D_FF)),
            _resident((D_FF, D_MODEL)),
        ],
        out_specs=pl.BlockSpec((tm, D_MODEL), blk),
        out_shape=jax.ShapeDtypeStruct((rows, D_MODEL), F32),
        compiler_params=pltpu.CompilerParams(
            dimension_semantics=("arbitrary",), vmem_limit_bytes=VMEM_LIMIT),
        name="ffn",
    )(x, g, wg, wu, wd)


def _lower_bound(clb):
    m = jnp.max(clb, axis=0, keepdims=True)
    e = jnp.exp(clb - m)
    sm = e / jnp.sum(e, axis=0, keepdims=True)
    return (sm[0:1] + sm[1:2]) - sm[0:1]


def _split3(x):
    hi = x.astype(BF16)
    r = x - hi.astype(F32)
    mid = r.astype(BF16)
    lo = (r - mid.astype(F32)).astype(BF16)
    return hi, mid, lo


def _neg_abs(x):
    return lax.bitcast_convert_type(
        lax.bitcast_convert_type(x, jnp.uint32) | jnp.uint32(0x80000000), F32)


def _pair_level_table():
    t = np.arange(CHUNK)[:, None]
    s = np.arange(CHUNK)[None, :]
    lev = np.floor(np.log2(np.maximum(t ^ s, 1))).astype(np.int32)
    lev = np.where(t == s, -1, lev)
    return np.where(s > t, -2, lev).astype(np.int32)


def _level_operand(p, q, kk, f, b2):
    m = 2 ** p
    if m < VREG_ROWS:
        shape3 = (CHUNK // VREG_ROWS, VREG_ROWS, q.shape[1])
        sub = lax.broadcasted_iota(jnp.int32, (1, VREG_ROWS, q.shape[1]), 1)
        upper = ((sub >> p) & 1) == 1
        q3, k3 = q.reshape(shape3), kk.reshape(shape3)
        if p == 0:
            y = jnp.where(upper, q3 * f.reshape(shape3), k3)
        else:
            b3 = b2.reshape(shape3)
            be = b3[:, m - 1:m, :]
            for k in range(1, VREG_ROWS // (2 * m)):
                be = jnp.where(sub >= 2 * m * k, b3[:, 2 * m * k + m - 1:2 * m * k + m, :], be)
            y = jnp.where(upper, q3, k3) * jnp.exp2(_neg_abs(b3 - be))
        return y.reshape(q.shape).astype(BF16)
    parts = []
    for k in range(CHUNK // (2 * m)):
        lo = slice(2 * m * k, 2 * m * k + m)
        up = slice(2 * m * k + m, 2 * m * (k + 1))
        be = b2[2 * m * k + m - 1:2 * m * k + m, :]
        parts.append(kk[lo] * jnp.exp2(be - b2[lo]))
        parts.append(q[up] * jnp.exp2(b2[up] - be))
    return jnp.concatenate(parts, axis=0).astype(BF16)


def _merge_level(p, att, pm, lev):
    m = 2 ** p
    if m < VREG_ROWS:
        return jnp.where(lev == p, pm, att)
    col = lax.broadcasted_iota(jnp.int32, (1, CHUNK), 1)
    parts = []
    for k in range(CHUNK // (2 * m)):
        lo = slice(2 * m * k, 2 * m * k + m)
        up = slice(2 * m * k + m, 2 * m * (k + 1))
        parts.append(att[lo])
        parts.append(jnp.where((col >= 2 * m * k) & (col < 2 * m * k + m), pm[up], att[up]))
    return jnp.concatenate(parts, axis=0)


def _hgrn_prompt_kernel(x_ref, nm_ref, win_ref, clb_ref, on_ref, wout_ref, lev_ref,
                        y_ref, st_ref,
                        z_ref, o_ref, stt_ref, f_ref, k_ref, b_ref):
    j = pl.program_id(1)
    last_j = pl.num_programs(1) - 1
    n_chunks = TQ // CHUNK
    n_levels = int(math.log2(CHUNK))

    @pl.when(j == 0)
    def _():
        stt_ref[...] = jnp.zeros_like(stt_ref)

    h = _rms(x_ref[0], nm_ref[...]).astype(BF16)
    z_ref[...] = _dot(h, win_ref[...])
    lb = _lower_bound(clb_ref[...])

    row = lax.broadcasted_iota(jnp.int32, (CHUNK, CHUNK), 0)
    col = lax.broadcasted_iota(jnp.int32, (CHUNK, CHUNK), 1)
    ltri = (row >= col).astype(BF16)

    def chunk(c, carry):
        r0 = pl.multiple_of(c * CHUNK, CHUNK)
        rows = pl.ds(r0, CHUNK)
        f_all = lb + (1.0 - lb) * jax.nn.sigmoid(z_ref[rows, C_F:2 * C_F])
        f_ref[...] = f_all
        k_ref[...] = 1.0 - f_all
        hi, mid, lo = _split3(jnp.log2(f_all))
        b_ref[...] = (_dot(ltri, hi) + _dot(ltri, mid)) + _dot(ltri, lo)

        lev = lev_ref[...]

        def products(hd):
            ks = slice(hd * C_KEY_DIM, (hd + 1) * C_KEY_DIM)
            q = z_ref[rows, hd * C_KEY_DIM:(hd + 1) * C_KEY_DIM]
            iv = z_ref[rows, 2 * C_F + hd * C_VAL_DIM:2 * C_F + (hd + 1) * C_VAL_DIM]
            f = f_ref[:, ks]
            kk = k_ref[:, ks]
            b2 = b_ref[:, ks]
            diag = jnp.sum(q * kk, axis=-1, keepdims=True)
            pms = []
            for p in range(n_levels):
                y = _level_operand(p, q, kk, f, b2)
                pms.append(lax.dot_general(y, y, _NT, preferred_element_type=F32))
            ivb = iv.astype(BF16)
            stt = stt_ref[hd]
            o_prev = lax.dot_general((q * jnp.exp2(b2)).astype(BF16), stt.astype(BF16), _NT,
                                     preferred_element_type=F32)
            b_last = b2[CHUNK - 1:CHUNK, :]
            kd = (kk * jnp.exp2(b_last - b2)).astype(BF16)
            stt_ref[hd] = stt * jnp.exp2(b_last) + lax.dot_general(
                ivb, kd, _TN, preferred_element_type=F32)
            return diag, pms, o_prev, ivb

        def finish(hd, diag, pms, o_prev, ivb):
            gt = z_ref[rows, 2 * C_F + C_V + hd * C_VAL_DIM:2 * C_F + C_V + (hd + 1) * C_VAL_DIM]
            att = jnp.where(lev == -1, diag, 0.0)
            for p in range(n_levels):
                att = _merge_level(p, att, pms[p], lev)
            o = o_prev + _dot(att.astype(BF16), ivb)
            o = _rms(o, on_ref[...]) * jax.nn.sigmoid(gt)
            o_ref[rows, hd * C_VAL_DIM:(hd + 1) * C_VAL_DIM] = o.astype(BF16)

        pending = [products(hd) for hd in range(HEAD_SKEW)]
        for hd in range(C_HEADS):
            if hd + HEAD_SKEW < C_HEADS:
                pending.append(products(hd + HEAD_SKEW))
            finish(hd, *pending.pop(0))
        return carry

    lax.fori_loop(0, n_chunks, chunk, 0)
    y_ref[0] = x_ref[0] + _dot(o_ref[...], wout_ref[...])

    @pl.when(j == last_j)
    def _():
        for hd in range(C_HEADS):
            st_ref[0, hd] = stt_ref[hd].T


def _hgrn_prompt(x, nm, w_in, clb, on, w_out):
    nb, seq, _ = x.shape
    grid = (nb, seq // TQ)
    blk = lambda b, j: (b, j, 0)
    return pl.pallas_call(
        _hgrn_prompt_kernel,
        grid=grid,
        in_specs=[
            pl.BlockSpec((1, TQ, D_MODEL), blk),
            _full((1, D_MODEL)),
            _resident((D_MODEL, C_IN)),
            _full((DEPTH, C_F)),
            _full((1, C_VAL_DIM)),
            _resident((C_V, D_MODEL)),
            _full((CHUNK, CHUNK)),
        ],
        out_specs=[
            pl.BlockSpec((1, TQ, D_MODEL), blk),
            pl.BlockSpec((1, C_HEADS, C_KEY_DIM, C_VAL_DIM), lambda b, j: (b, 0, 0, 0)),
        ],
        out_shape=[
            jax.ShapeDtypeStruct((nb, seq, D_MODEL), F32),
            jax.ShapeDtypeStruct((nb, C_HEADS, C_KEY_DIM, C_VAL_DIM), F32),
        ],
        scratch_shapes=[
            pltpu.VMEM((TQ, C_IN), F32),
            pltpu.VMEM((TQ, C_V), BF16),
            pltpu.VMEM((C_HEADS, C_VAL_DIM, C_KEY_DIM), F32),
            pltpu.VMEM((CHUNK, C_F), F32),
            pltpu.VMEM((CHUNK, C_F), F32),
            pltpu.VMEM((CHUNK, C_F), F32),
        ],
        compiler_params=pltpu.CompilerParams(
            dimension_semantics=("arbitrary", "arbitrary"), vmem_limit_bytes=VMEM_LIMIT),
        name="hgrn_prompt",
    )(x, nm, w_in, clb, on, w_out, jnp.asarray(_pair_level_table()))


def _ab_sample_proj_kernel(w00_ref, b0_ref, x_ref, nm_ref, win_ref, qn_ref, kn_ref, lng_ref, lnb_ref,
                           qx_ref, knew_ref, vnew_ref, bm_ref, gv_ref):
    n = x_ref.shape[0]
    h = _rms(x_ref[...], nm_ref[...]).astype(BF16)
    z = _dot(h, win_ref[...])
    zeros = jnp.zeros((n, A_HEAD_DIM), F32)
    for hh in range(A_HEADS):
        qh = _rms(z[:, hh * A_HEAD_DIM:(hh + 1) * A_HEAD_DIM], qn_ref[...]) * ATTN_SCALE
        qx_ref[hh] = jnp.concatenate([qh, zeros] if hh // A_GROUP == 0 else [zeros, qh], axis=-1)
    kparts = []
    for g in range(A_KV_HEADS):
        kparts.append(_rms(z[:, A_Q + g * A_HEAD_DIM:A_Q + (g + 1) * A_HEAD_DIM], kn_ref[...]))
    knew_ref[...] = jnp.concatenate(kparts, axis=-1)
    vnew_ref[...] = z[:, A_Q + A_KV:A_Q + 2 * A_KV]

    u = _gelu(z[:, A_Q + 2 * A_KV:A_Q + 2 * A_KV + B_WIDTH])
    vln = _layernorm(_gelu(z[:, A_Q + 2 * A_KV + B_WIDTH:AB_IN]), lng_ref[...], lnb_ref[...])
    grp = lax.broadcasted_iota(jnp.int32, (1, B_WIDTH), 1) // B_GROUP_DIM
    srow = jnp.zeros((1, B_WIDTH), F32)
    brow = jnp.zeros((1, B_WIDTH), F32)
    for g in range(B_GROUPS):
        srow = jnp.where(grp == g, w00_ref[g], srow)
        brow = jnp.where(grp == g, b0_ref[g], brow)
    bm_ref[...] = u * (vln * srow + brow)
    gv_ref[...] = vln


def _ab_sample_proj(x, nm, w_in, qn, kn, lng, lnb, w00, b0):
    n = x.shape[0]
    return pl.pallas_call(
        _ab_sample_proj_kernel,
        in_specs=[_SMEM, _SMEM] + [pl.BlockSpec(memory_space=pltpu.VMEM)] * 7,
        out_shape=[
            jax.ShapeDtypeStruct((A_HEADS, n, A_KV), F32),
            jax.ShapeDtypeStruct((n, A_KV), F32),
            jax.ShapeDtypeStruct((n, A_KV), F32),
            jax.ShapeDtypeStruct((n, B_WIDTH), F32),
            jax.ShapeDtypeStruct((n, B_WIDTH), F32),
        ],
        compiler_params=pltpu.CompilerParams(vmem_limit_bytes=VMEM_LIMIT),
        name="ab_sample_proj",
    )(w00, b0, x, nm, w_in, qn, kn, lng, lnb)


def _ab_sample_attn_kernel(ck_ref, cv_ref, qx_ref, kn_ref, vn_ref, sb_ref, sink_ref,
                           nk_ref, nv_ref, om_ref):
    wb = ck_ref.shape[1]
    head = lax.broadcasted_iota(jnp.int32, (A_HEADS, A_KV), 0)
    lane = lax.broadcasted_iota(jnp.int32, (A_HEADS, A_KV), 1)
    own_group = (head // A_GROUP) == (lane // A_HEAD_DIM)
    sink = sink_ref[...]
    for i in range(SB):
        kc = ck_ref[i]
        vc = cv_ref[i]
        kn = kn_ref[i:i + 1, :]
        vn = vn_ref[i:i + 1, :]
        nk_ref[i, 0:wb - 1, :] = kc[1:wb, :]
        nk_ref[i, wb - 1:wb, :] = kn
        nv_ref[i, 0:wb - 1, :] = vc[1:wb, :]
        nv_ref[i, wb - 1:wb, :] = vn
        q = qx_ref[i]
        s = lax.dot_general(q.astype(BF16), kc.astype(BF16), _NT, preferred_element_type=F32)
        s = s + sb_ref[:, 0:wb]
        sn = jnp.sum(q * kn, axis=-1, keepdims=True) + sb_ref[:, wb:wb + 1]
        m = jnp.maximum(jnp.maximum(jnp.max(s, axis=-1, keepdims=True), sn), sink)
        e = jnp.exp(s - m)
        en = jnp.exp(sn - m)
        r = 1.0 / (jnp.sum(e, axis=-1, keepdims=True) + en + jnp.exp(sink - m))
        o = _dot((e * r).astype(BF16), vc.astype(BF16)) + (en * r) * vn
        om_ref[i] = jnp.where(own_group, o, 0.0)


def _ab_sample_attn(ck, cv, qx, kn, vn, sb, sink):
    n, wb, _ = ck.shape
    blk3 = lambda i: (i, 0, 0)
    blk2 = lambda i: (i, 0)
    return pl.pallas_call(
        _ab_sample_attn_kernel,
        grid=(n // SB,),
        in_specs=[
            pl.BlockSpec((SB, wb, A_KV), blk3),
            pl.BlockSpec((SB, wb, A_KV), blk3),
            pl.BlockSpec((SB, A_HEADS, A_KV), blk3),
            pl.BlockSpec((SB, A_KV), blk2),
            pl.BlockSpec((SB, A_KV), blk2),
            _full((A_HEADS, 2 * WINDOW)),
            _full((A_HEADS, 1)),
        ],
        out_specs=[
            pl.BlockSpec((SB, wb, A_KV), blk3),
            pl.BlockSpec((SB, wb, A_KV), blk3),
            pl.BlockSpec((SB, A_HEADS, A_KV), blk3),
        ],
        out_shape=[
            jax.ShapeDtypeStruct((n, wb, A_KV), F32),
            jax.ShapeDtypeStruct((n, wb, A_KV), F32),
            jax.ShapeDtypeStruct((n, A_HEADS, A_KV), F32),
        ],
        compiler_params=pltpu.CompilerParams(dimension_semantics=("arbitrary",)),
        name="ab_sample_attn",
    )(ck, cv, qx, kn, vn, sb, sink)


def _residual_proj_kernel(x_ref, m_ref, w_ref, y_ref):
    y_ref[...] = x_ref[...] + _dot(m_ref[...].astype(BF16), w_ref[...])


def _residual_proj(x, mix, w):
    return pl.pallas_call(
        _residual_proj_kernel,
        out_shape=jax.ShapeDtypeStruct(x.shape, F32),
        name="residual_proj",
    )(x, mix, w)


def _hgrn_sample_proj_kernel(x_ref, nm_ref, win_ref, clb_ref, q_ref, f_ref, k_ref, i_ref, sg_ref):
    h = _rms(x_ref[...], nm_ref[...]).astype(BF16)
    z = _dot(h, win_ref[...])
    lb = _lower_bound(clb_ref[...])
    f = lb + (1.0 - lb) * jax.nn.sigmoid(z[:, C_F:2 * C_F])
    q_ref[...] = z[:, 0:C_F]
    f_ref[...] = f
    k_ref[...] = 1.0 - f
    i_ref[...] = z[:, 2 * C_F:2 * C_F + C_V]
    sg_ref[...] = jax.nn.sigmoid(z[:, 2 * C_F + C_V:C_IN])


def _hgrn_sample_proj(x, nm, w_in, clb):
    n = x.shape[0]
    return pl.pallas_call(
        _hgrn_sample_proj_kernel,
        out_shape=[jax.ShapeDtypeStruct((n, C_F), F32)] * 3 + [jax.ShapeDtypeStruct((n, C_V), F32)] * 2,
        compiler_params=pltpu.CompilerParams(vmem_limit_bytes=VMEM_LIMIT),
        name="hgrn_sample_proj",
    )(x, nm, w_in, clb)


def _hgrn_sample_state_kernel(s_ref, qc_ref, fc_ref, kc_ref, i_ref, sn_ref, o_ref):
    out_rows = []
    for s in range(SB):
        parts = []
        for hd in range(C_HEADS):
            st = s_ref[s, hd]
            irow = i_ref[s:s + 1, hd * C_VAL_DIM:(hd + 1) * C_VAL_DIM]
            sn = fc_ref[0, hd, :, s:s + 1] * st + kc_ref[0, hd, :, s:s + 1] * irow
            sn_ref[s, hd] = sn
            parts.append(jnp.sum(qc_ref[0, hd, :, s:s + 1] * sn, axis=0, keepdims=True))
        out_rows.append(jnp.concatenate(parts, axis=-1))
    o_ref[...] = jnp.concatenate(out_rows, axis=0)


def _hgrn_sample_state(state, qc, fc, kc, iv):
    n = state.shape[0]
    sblk = pl.BlockSpec((SB, C_HEADS, C_KEY_DIM, C_VAL_DIM), lambda i: (i, 0, 0, 0))
    cblk = pl.BlockSpec((1, C_HEADS, C_KEY_DIM, SB), lambda i: (i, 0, 0, 0))
    return pl.pallas_call(
        _hgrn_sample_state_kernel,
        grid=(n // SB,),
        in_specs=[sblk, cblk, cblk, cblk, pl.BlockSpec((SB, C_V), lambda i: (i, 0))],
        out_specs=[sblk, pl.BlockSpec((SB, C_V), lambda i: (i, 0))],
        out_shape=[
            jax.ShapeDtypeStruct(state.shape, F32),
            jax.ShapeDtypeStruct((n, C_V), F32),
        ],
        compiler_params=pltpu.CompilerParams(
            dimension_semantics=("arbitrary",), vmem_limit_bytes=VMEM_LIMIT),
        name="hgrn_sample_state",
    )(state, qc, fc, kc, iv)


def _hgrn_sample_out_kernel(o_ref, sg_ref, on_ref, w_ref, x_ref, y_ref):
    parts = []
    for hd in range(C_HEADS):
        parts.append(_rms(o_ref[:, hd * C_VAL_DIM:(hd + 1) * C_VAL_DIM], on_ref[...]))
    on = jnp.concatenate(parts, axis=-1) * sg_ref[...]
    y_ref[...] = x_ref[...] + _dot(on.astype(BF16), w_ref[...])


def _hgrn_sample_out(o, sg, on, w_out, x):
    return pl.pallas_call(
        _hgrn_sample_out_kernel,
        out_shape=jax.ShapeDtypeStruct(x.shape, F32),
        name="hgrn_sample_out",
    )(o, sg, on, w_out, x)


def _to_columns(a):
    n = a.shape[0]
    return a.reshape(n // SB, SB, C_HEADS, C_KEY_DIM).transpose(0, 2, 3, 1)


def kernel(x_prompt, x_sample, cache_k, cache_v, state_hgrn, norm_mix, norm_ffn, w_in_ab, w_out_ab,
           q_norm, k_norm, attn_sink, rel_bias, gmlp_ln_g, gmlp_ln_b, gmlp_w_s, gmlp_b_s,
           w_in_c, c_lower_bounds, c_out_norm, w_out_c, w_gate, w_up, w_down):
    assert norm_mix.shape[0] == DEPTH == 2 and w_in_ab.shape[0] == 1 and w_in_c.shape[0] == 1
    nb, seq, _ = x_prompt.shape
    ns = x_sample.shape[0]
    assert x_sample.shape[1] == 1 and cache_k.shape[2] == WINDOW

    row = lambda v: v.reshape(1, -1)
    bf = lambda w: w.astype(BF16)
    w_in_ab0, w_out_ab0 = bf(w_in_ab[0]), bf(w_out_ab[0])
    w_in_c0, w_out_c0 = bf(w_in_c[0]), bf(w_out_c[0])
    wg, wu, wd = bf(w_gate), bf(w_up), bf(w_down)
    nm, nf = norm_mix, norm_ffn
    qn, kn = row(q_norm[0]), row(k_norm[0])
    lng, lnb = row(gmlp_ln_g[0]), row(gmlp_ln_b[0])
    sink = attn_sink[0]

    tab, tabp = _bias_table(rel_bias)

    xp, knew_p, vnew_p, gv_p = _ab_prompt(
        x_prompt, row(nm[0]), w_in_ab0, jnp.tile(qn, (1, A_HEADS)), jnp.tile(kn, (1, A_KV_HEADS)),
        sink, tabp, lng, lnb, gmlp_w_s[0], jnp.repeat(gmlp_b_s[0].T, B_GROUP_DIM, axis=1), w_out_ab0)
    xp = _ffn(xp.reshape(nb * seq, D_MODEL), row(nf[0]), wg[0], wu[0], wd[0]).reshape(nb, seq, D_MODEL)
    xp, st_p = _hgrn_prompt(xp, row(nm[1]), w_in_c0, c_lower_bounds, row(c_out_norm[0]), w_out_c0)
    xp = _ffn(xp.reshape(nb * seq, D_MODEL), row(nf[1]), wg[1], wu[1], wd[1]).reshape(nb, seq, D_MODEL)

    xs = x_sample.reshape(ns, D_MODEL)
    qx, knew_s, vnew_s, bm_s, gv_s = _ab_sample_proj(
        xs, row(nm[0]), w_in_ab0, qn, kn, lng, lnb, gmlp_w_s[0, :, 0, 0], gmlp_b_s[0, :, 0])
    sb = jnp.pad(tab[:, WINDOW - 1, WINDOW - 1:], ((0, 0), (0, WINDOW - 1)))
    nk_s, nv_s, om = _ab_sample_attn(
        cache_k[0].reshape(ns, WINDOW, A_KV), cache_v[0].reshape(ns, WINDOW, A_KV),
        qx.transpose(1, 0, 2), knew_s, vnew_s, sb, sink.reshape(A_HEADS, 1))
    om = om.reshape(ns, A_KV_HEADS, A_GROUP, A_KV_HEADS, A_HEAD_DIM)
    a_s = jnp.stack([om[:, g, :, g, :] for g in range(A_KV_HEADS)], axis=1).reshape(ns, A_Q)
    xs = _residual_proj(xs, jnp.concatenate([a_s, bm_s], axis=-1), w_out_ab0)
    xs = _ffn(xs, row(nf[0]), wg[0], wu[0], wd[0])

    q_s, f_s, k_s, i_s, sg_s = _hgrn_sample_proj(xs, row(nm[1]), w_in_c0, c_lower_bounds)
    st_s, o_s = _hgrn_sample_state(state_hgrn[0], _to_columns(q_s), _to_columns(f_s), _to_columns(k_s), i_s)
    xs = _hgrn_sample_out(o_s, sg_s, row(c_out_norm[0]), w_out_c0, xs)
    xs = _ffn(xs, row(nf[1]), wg[1], wu[1], wd[1])

    kv5 = lambda a: a.reshape(1, a.shape[0], WINDOW, A_KV_HEADS, A_HEAD_DIM)
    return (xp, xs.reshape(ns, 1, D_MODEL),
            kv5(knew_p), kv5(vnew_p), kv5(nk_s), kv5(nv_s),
            gv_p[None], gv_s.reshape(1, ns, 1, B_WIDTH),
            st_p[None], st_s[None])
```

```python
import math

import jax
import jax.numpy as jnp
import numpy as np
from jax import lax
from jax.experimental import pallas as pl
from jax.experimental.pallas import tpu as pltpu

F32 = jnp.float32
BF16 = jnp.bfloat16

D_MODEL = 1024
DEPTH = 2
A_HEADS = 8
A_KV_HEADS = 2
A_GROUP = A_HEADS // A_KV_HEADS
A_HEAD_DIM = 64
WINDOW = 128
ATTN_SCALE = A_HEAD_DIM ** -0.5
NUM_BUCKETS = 32
MAX_DISTANCE = 128
A_Q = A_HEADS * A_HEAD_DIM
A_KV = A_KV_HEADS * A_HEAD_DIM
B_GROUPS = 8
B_GROUP_DIM = 64
B_WIDTH = B_GROUPS * B_GROUP_DIM
B_CHUNK = 128
AB_IN = A_Q + 2 * A_KV + 2 * B_WIDTH
AB_MIX = A_Q + B_WIDTH
C_HEADS = 8
C_KEY_DIM = 128
C_VAL_DIM = 128
C_F = C_HEADS * C_KEY_DIM
C_V = C_HEADS * C_VAL_DIM
C_IN = 2 * C_F + 2 * C_V
D_FF = 2816
EPS = 1e-6

NEG = -1e30

VMEM_LIMIT = 56 * 1024 * 1024
VREG_ROWS = 8

CHUNK = 128
TQ = 512
TM = 512
SB = 8
HEAD_SKEW = 2

_NT = (((1,), (1,)), ((), ()))
_TN = (((0,), (0,)), ((), ()))


def _rms(x, g):
    return x * lax.rsqrt(jnp.mean(x * x, axis=-1, keepdims=True) + EPS) * g


def _gelu(x):
    return 0.5 * x * (1.0 + lax.erf(x * math.sqrt(0.5)))


def _layernorm(x, g, b):
    xc = x - jnp.mean(x, axis=-1, keepdims=True)
    return xc * lax.rsqrt(jnp.mean(xc * xc, axis=-1, keepdims=True) + EPS) * g + b


def _dot(a, b):
    return jnp.dot(a, b, preferred_element_type=F32)


def _full(shape):
    n = len(shape)
    return pl.BlockSpec(shape, lambda *_: (0,) * n)


def _resident(shape):
    n = len(shape)
    return pl.BlockSpec(shape, lambda *_: (0,) * n, pipeline_mode=pl.Buffered(1))


_SMEM = pl.BlockSpec(memory_space=pltpu.SMEM)


def _bias_table_kernel(rel_ref, tab_ref, tabp_ref):
    qi = lax.broadcasted_iota(jnp.int32, (WINDOW, 2 * WINDOW), 0)
    kj = lax.broadcasted_iota(jnp.int32, (WINDOW, 2 * WINDOW), 1)
    dist = qi + WINDOW - kj
    ok = (dist >= 0) & (dist < WINDOW)
    max_exact = NUM_BUCKETS // 2
    d = jnp.maximum(dist, 0)
    dl = jnp.maximum(d, 1).astype(F32)
    large = max_exact + (jnp.log(dl / max_exact) / math.log(MAX_DISTANCE / max_exact)
                         * (NUM_BUCKETS - max_exact)).astype(jnp.int32)
    large = jnp.minimum(large, NUM_BUCKETS - 1)
    bucket = jnp.where(d < max_exact, d, large)
    for h in range(A_HEADS):
        acc = jnp.zeros((WINDOW, 2 * WINDOW), F32)
        for b in range(NUM_BUCKETS):
            acc = jnp.where(bucket == b, rel_ref[b, h], acc)
        t = jnp.where(ok, acc, NEG)
        tab_ref[h] = t
        cols = slice((h % 2) * 2 * WINDOW, (h % 2 + 1) * 2 * WINDOW)
        tabp_ref[0, h // 2, :, cols] = t
        tabp_ref[1, h // 2, :, cols] = jnp.where(kj < WINDOW, NEG, t)


def _bias_table(rel_bias):
    return pl.pallas_call(
        _bias_table_kernel,
        out_shape=[
            jax.ShapeDtypeStruct((A_HEADS, WINDOW, 2 * WINDOW), F32),
            jax.ShapeDtypeStruct((2, A_HEADS // 2, WINDOW, 4 * WINDOW), F32),
        ],
        in_specs=[_SMEM],
        name="bias_table",
    )(rel_bias)


PAIR = 2 * A_HEAD_DIM
N_PAIRS = A_HEADS // 2


def _ab_prompt_kernel(sink_ref, x_ref, nm_ref, win_ref, qg_ref, kg_ref, tabp_ref, lng_ref, lnb_ref,
                      ws_ref, bsp_ref, wout_ref,
                      y_ref, knew_ref, vnew_ref, gv_ref,
                      z_ref, mix_ref, kp_ref, kpr_ref, vp_ref, vpr_ref, wpair_ref):
    j = pl.program_id(1)
    last_j = pl.num_programs(1) - 1
    n_chunks = TQ // CHUNK

    @pl.when(j == 0)
    def _():
        for ref in (kp_ref, kpr_ref, vp_ref, vpr_ref):
            ref[...] = jnp.zeros_like(ref)
        row = lax.broadcasted_iota(jnp.int32, (B_CHUNK, B_CHUNK), 0)
        col = lax.broadcasted_iota(jnp.int32, (B_CHUNK, B_CHUNK), 1)
        for g in range(B_GROUPS):
            wpair_ref[g // 2, :, (g % 2) * B_CHUNK:(g % 2 + 1) * B_CHUNK] = jnp.where(
                row >= col, ws_ref[g], 0.0).astype(BF16)

    h = _rms(x_ref[0], nm_ref[...]).astype(BF16)
    z_ref[...] = _dot(h, win_ref[...])

    lo_half = lax.broadcasted_iota(jnp.int32, (1, PAIR), 1) < A_HEAD_DIM
    r_i = lax.broadcasted_iota(jnp.int32, (PAIR, PAIR), 0) // A_HEAD_DIM
    c_i = lax.broadcasted_iota(jnp.int32, (PAIR, PAIR), 1) // A_HEAD_DIM
    half_mean = jnp.where(r_i == c_i, 1.0 / A_HEAD_DIM, 0.0).astype(BF16)

    def mean_sq_halves(x):
        x2 = x * x
        hi = x2.astype(BF16)
        lo = (x2 - hi.astype(F32)).astype(BF16)
        return _dot(hi, half_mean) + _dot(lo, half_mean)

    def block_diag(top, bot):
        zero = jnp.zeros_like(top)
        return jnp.concatenate([jnp.where(lo_half, top, zero), jnp.where(lo_half, zero, bot)], axis=0)

    def chunk(c, carry):
        r0 = pl.multiple_of(c * CHUNK, CHUNK)
        rows = pl.ds(r0, CHUNK)
        first = jnp.where(jnp.logical_and(j == 0, c == 0), 1, 0)

        kraw = z_ref[rows, A_Q:A_Q + A_KV]
        v = z_ref[rows, A_Q + A_KV:A_Q + 2 * A_KV]
        kn = kraw * lax.rsqrt(mean_sq_halves(kraw) + EPS) * kg_ref[...]
        kb, kbr = kn.astype(BF16), pltpu.roll(kn, A_HEAD_DIM, 1).astype(BF16)
        vb, vbr = v.astype(BF16), pltpu.roll(v, A_HEAD_DIM, 1).astype(BF16)
        k2 = jnp.concatenate([kp_ref[...], kb], axis=0)
        k2r = jnp.concatenate([kpr_ref[...], kbr], axis=0)
        v2 = jnp.concatenate([vp_ref[...], vb], axis=0)
        v2r = jnp.concatenate([vpr_ref[...], vbr], axis=0)
        kbd = [block_diag(k2, k2r), block_diag(k2r, k2)]
        vbd = [block_diag(v2, v2r), block_diag(v2r, v2)]

        scores = []
        for i in range(N_PAIRS):
            ps = slice(i * PAIR, (i + 1) * PAIR)
            qraw = z_ref[rows, i * PAIR:(i + 1) * PAIR]
            qn = qraw * lax.rsqrt(mean_sq_halves(qraw) + EPS) * (qg_ref[:, ps] * ATTN_SCALE)
            s = lax.dot_general(qn.astype(BF16), kbd[i // (A_GROUP // 2)], _NT,
                                preferred_element_type=F32)
            scores.append(s + tabp_ref[first, i])
        outs = []
        for i in range(N_PAIRS):
            es, rs = [], []
            for hh in range(2):
                sh = scores[i][:, hh * 2 * WINDOW:(hh + 1) * 2 * WINDOW]
                sk = sink_ref[2 * i + hh]
                m = jnp.maximum(jnp.max(sh, axis=-1, keepdims=True), sk)
                e = jnp.exp(sh - m)
                rs.append(1.0 / (jnp.sum(e, axis=-1, keepdims=True) + jnp.exp(sk - m)))
                es.append(e.astype(BF16))
            o = _dot(jnp.concatenate(es, axis=-1), vbd[i // (A_GROUP // 2)])
            outs.append(o * jnp.where(lo_half, rs[0], rs[1]))
        mix_ref[rows, 0:A_Q] = jnp.concatenate(outs, axis=-1).astype(BF16)

        zu = z_ref[rows, A_Q + 2 * A_KV:A_Q + 2 * A_KV + B_WIDTH]
        zv = z_ref[rows, A_Q + 2 * A_KV + B_WIDTH:AB_IN]
        u = _gelu(zu)
        vln = _layernorm(_gelu(zv), lng_ref[...], lnb_ref[...])
        vlb = vln.astype(BF16)
        sparts = []
        for i in range(B_GROUPS // 2):
            vpair = vlb[:, i * PAIR:(i + 1) * PAIR]
            sparts.append(_dot(wpair_ref[i], block_diag(vpair, vpair)))
        bm = u * (jnp.concatenate(sparts, axis=-1) + bsp_ref[...])
        mix_ref[rows, A_Q:AB_MIX] = bm.astype(BF16)

        kp_ref[...] = kb
        kpr_ref[...] = kbr
        vp_ref[...] = vb
        vpr_ref[...] = vbr

        @pl.when(jnp.logical_and(j == last_j, c == n_chunks - 1))
        def _():
            knew_ref[0] = kn
            vnew_ref[0] = v
            gv_ref[0] = vln

        return carry

    lax.fori_loop(0, n_chunks, chunk, 0)
    y_ref[0] = x_ref[0] + _dot(mix_ref[...], wout_ref[...])


def _ab_prompt(x, nm, w_in, qg, kg, sink, tabp, lng, lnb, w_s, bsp, w_out):
    nb, seq, _ = x.shape
    grid = (nb, seq // TQ)
    blk = lambda b, j: (b, j, 0)
    per_b = lambda b, j: (b, 0, 0)
    return pl.pallas_call(
        _ab_prompt_kernel,
        grid=grid,
        in_specs=[
            _SMEM,
            pl.BlockSpec((1, TQ, D_MODEL), blk),
            _full((1, D_MODEL)),
            _resident((D_MODEL, AB_IN)),
            _full((1, A_Q)),
            _full((1, A_KV)),
            _resident((2, N_PAIRS, WINDOW, 4 * WINDOW)),
            _full((1, B_WIDTH)),
            _full((1, B_WIDTH)),
            _resident((B_GROUPS, B_CHUNK, B_CHUNK)),
            _resident((B_CHUNK, B_WIDTH)),
            _resident((AB_MIX, D_MODEL)),
        ],
        out_specs=[
            pl.BlockSpec((1, TQ, D_MODEL), blk),
            pl.BlockSpec((1, WINDOW, A_KV), per_b),
            pl.BlockSpec((1, WINDOW, A_KV), per_b),
            pl.BlockSpec((1, B_CHUNK, B_WIDTH), per_b),
        ],
        out_shape=[
            jax.ShapeDtypeStruct((nb, seq, D_MODEL), F32),
            jax.ShapeDtypeStruct((nb, WINDOW, A_KV), F32),
            jax.ShapeDtypeStruct((nb, WINDOW, A_KV), F32),
            jax.ShapeDtypeStruct((nb, B_CHUNK, B_WIDTH), F32),
        ],
        scratch_shapes=[
            pltpu.VMEM((TQ, AB_IN), F32),
            pltpu.VMEM((TQ, AB_MIX), BF16),
            pltpu.VMEM((WINDOW, A_KV), BF16),
            pltpu.VMEM((WINDOW, A_KV), BF16),
            pltpu.VMEM((WINDOW, A_KV), BF16),
            pltpu.VMEM((WINDOW, A_KV), BF16),
            pltpu.VMEM((B_GROUPS // 2, B_CHUNK, 2 * B_CHUNK), BF16),
        ],
        compiler_params=pltpu.CompilerParams(
            dimension_semantics=("arbitrary", "arbitrary"), vmem_limit_bytes=VMEM_LIMIT),
        name="ab_prompt",
    )(sink, x, nm, w_in, qg, kg, tabp, lng, lnb, w_s, bsp, w_out)


def _ffn_kernel(x_ref, g_ref, wg_ref, wu_ref, wd_ref, o_ref):
    x = x_ref[...]
    h = _rms(x, g_ref[...]).astype(BF16)
    gate = _dot(h, wg_ref[...])
    up = _dot(h, wu_ref[...])
    a = (gate * jax.nn.sigmoid(gate) * up).astype(BF16)
    o_ref[...] = x + _dot(a, wd_ref[...])


def _ffn(x, g, wg, wu, wd):
    rows = x.shape[0]
    tm = min(TM, rows)
    blk = lambda i: (i, 0)
    return pl.pallas_call(
        _ffn_kernel,
        grid=(rows // tm,),
        in_specs=[
            pl.BlockSpec((tm, D_MODEL), blk),
            _full((1, D_MODEL)),
            _resident((D_MODEL, D_FF)),
            _resident((D_MODEL, D_FF)),
            _resident((D_FF, D_MODEL)),
        ],
        out_specs=pl.BlockSpec((tm, D_MODEL), blk),
        out_shape=jax.ShapeDtypeStruct((rows, D_MODEL), F32),
        compiler_params=pltpu.CompilerParams(
            dimension_semantics=("arbitrary",), vmem_limit_bytes=VMEM_LIMIT),
        name="ffn",
    )(x, g, wg, wu, wd)


def _lower_bound(clb):
    m = jnp.max(clb, axis=0, keepdims=True)
    e = jnp.exp(clb - m)
    sm = e / jnp.sum(e, axis=0, keepdims=True)
    return (sm[0:1] + sm[1:2]) - sm[0:1]


def _split3(x):
    hi = x.astype(BF16)
    r = x - hi.astype(F32)
    mid = r.astype(BF16)
    lo = (r - mid.astype(F32)).astype(BF16)
    return hi, mid, lo


def _neg_abs(x):
    return lax.bitcast_convert_type(
        lax.bitcast_convert_type(x, jnp.uint32) | jnp.uint32(0x80000000), F32)


def _pair_level_table():
    t = np.arange(CHUNK)[:, None]
    s = np.arange(CHUNK)[None, :]
    lev = np.floor(np.log2(np.maximum(t ^ s, 1))).astype(np.int32)
    lev = np.where(t == s, -1, lev)
    return np.where(s > t, -2, lev).astype(np.int32)


def _level_operand(p, q, kk, f, b2):
    m = 2 ** p
    if m < VREG_ROWS:
        shape3 = (CHUNK // VREG_ROWS, VREG_ROWS, q.shape[1])
        sub = lax.broadcasted_iota(jnp.int32, (1, VREG_ROWS, q.shape[1]), 1)
        upper = ((sub >> p) & 1) == 1
        q3, k3 = q.reshape(shape3), kk.reshape(shape3)
        if p == 0:
            y = jnp.where(upper, q3 * f.reshape(shape3), k3)
        else:
            b3 = b2.reshape(shape3)
            be = b3[:, m - 1:m, :]
            for k in range(1, VREG_ROWS // (2 * m)):
                be = jnp.where(sub >= 2 * m * k, b3[:, 2 * m * k + m - 1:2 * m * k + m, :], be)
            y = jnp.where(upper, q3, k3) * jnp.exp2(_neg_abs(b3 - be))
        return y.reshape(q.shape).astype(BF16)
    parts = []
    for k in range(CHUNK // (2 * m)):
        lo = slice(2 * m * k, 2 * m * k + m)
        up = slice(2 * m * k + m, 2 * m * (k + 1))
        be = b2[2 * m * k + m - 1:2 * m * k + m, :]
        parts.append(kk[lo] * jnp.exp2(be - b2[lo]))
        parts.append(q[up] * jnp.exp2(b2[up] - be))
    return jnp.concatenate(parts, axis=0).astype(BF16)


def _merge_level(p, att, pm, lev):
    m = 2 ** p
    if m < VREG_ROWS:
        return jnp.where(lev == p, pm, att)
    col = lax.broadcasted_iota(jnp.int32, (1, CHUNK), 1)
    parts = []
    for k in range(CHUNK // (2 * m)):
        lo = slice(2 * m * k, 2 * m * k + m)
        up = slice(2 * m * k + m, 2 * m * (k + 1))
        parts.append(att[lo])
        parts.append(jnp.where((col >= 2 * m * k) & (col < 2 * m * k + m), pm[up], att[up]))
    return jnp.concatenate(parts, axis=0)


def _hgrn_prompt_kernel(x_ref, nm_ref, win_ref, clb_ref, on_ref, wout_ref, lev_ref,
                        y_ref, st_ref,
                        z_ref, o_ref, stt_ref, f_ref, k_ref, b_ref):
    j = pl.program_id(1)
    last_j = pl.num_programs(1) - 1
    n_chunks = TQ // CHUNK
    n_levels = int(math.log2(CHUNK))

    @pl.when(j == 0)
    def _():
        stt_ref[...] = jnp.zeros_like(stt_ref)

    h = _rms(x_ref[0], nm_ref[...]).astype(BF16)
    z_ref[...] = _dot(h, win_ref[...])
    lb = _lower_bound(clb_ref[...])

    row = lax.broadcasted_iota(jnp.int32, (CHUNK, CHUNK), 0)
    col = lax.broadcasted_iota(jnp.int32, (CHUNK, CHUNK), 1)
    ltri = (row >= col).astype(BF16)

    def z_cols(rows, start, width):
        return z_ref[rows, start:start + width]

    def chunk(c, carry):
        r0 = pl.multiple_of(c * CHUNK, CHUNK)
        rows = pl.ds(r0, CHUNK)
        f_all = lb + (1.0 - lb) * jax.nn.sigmoid(z_cols(rows, C_F, C_F))
        f_ref[...] = f_all
        k_ref[...] = 1.0 - f_all
        hi, mid, lo = _split3(jnp.log2(f_all))
        b_ref[...] = (_dot(ltri, hi) + _dot(ltri, mid)) + _dot(ltri, lo)

        lev = lev_ref[...]

        def products(hd):
            ks = slice(hd * C_KEY_DIM, (hd + 1) * C_KEY_DIM)
            q = z_cols(rows, hd * C_KEY_DIM, C_KEY_DIM)
            iv = z_cols(rows, 2 * C_F + hd * C_VAL_DIM, C_VAL_DIM)
            f = f_ref[:, ks]
            kk = k_ref[:, ks]
            b2 = b_ref[:, ks]
            diag = jnp.sum(q * kk, axis=-1, keepdims=True)
            pms = []
            for p in range(n_levels):
                y = _level_operand(p, q, kk, f, b2)
                pms.append(lax.dot_general(y, y, _NT, preferred_element_type=F32))
            ivb = iv.astype(BF16)
            stt = stt_ref[hd]
            o_prev = lax.dot_general((q * jnp.exp2(b2)).astype(BF16), stt.astype(BF16), _NT,
                                     preferred_element_type=F32)
            b_last = b2[CHUNK - 1:CHUNK, :]
            kd = (kk * jnp.exp2(b_last - b2)).astype(BF16)
            stt_ref[hd] = stt * jnp.exp2(b_last) + lax.dot_general(
                ivb, kd, _TN, preferred_element_type=F32)
            return diag, pms, o_prev, ivb

        def finish(hd, diag, pms, o_prev, ivb):
            gt = z_cols(rows, 2 * C_F + C_V + hd * C_VAL_DIM, C_VAL_DIM)
            att = jnp.where(lev == -1, diag, 0.0)
            for p in range(n_levels):
                att = _merge_level(p, att, pms[p], lev)
            o = o_prev + _dot(att.astype(BF16), ivb)
            o = _rms(o, on_ref[...]) * jax.nn.sigmoid(gt)
            o_ref[rows, hd * C_VAL_DIM:(hd + 1) * C_VAL_DIM] = o.astype(BF16)

        pending = [products(hd) for hd in range(HEAD_SKEW)]
        for hd in range(C_HEADS):
            if hd + HEAD_SKEW < C_HEADS:
                pending.append(products(hd + HEAD_SKEW))
            finish(hd, *pending.pop(0))
        return carry

    lax.fori_loop(0, n_chunks, chunk, 0)
    y_ref[0] = x_ref[0] + _dot(o_ref[...], wout_ref[...])

    @pl.when(j == last_j)
    def _():
        for hd in range(C_HEADS):
            st_ref[0, hd] = stt_ref[hd].T


def _hgrn_prompt(x, nm, w_in, clb, on, w_out):
    nb, seq, _ = x.shape
    grid = (nb, seq // TQ)
    blk = lambda b, j: (b, j, 0)
    return pl.pallas_call(
        _hgrn_prompt_kernel,
        grid=grid,
        in_specs=[
            pl.BlockSpec((1, TQ, D_MODEL), blk),
            _full((1, D_MODEL)),
            _resident((D_MODEL, C_IN)),
            _full((DEPTH, C_F)),
            _full((1, C_VAL_DIM)),
            _resident((C_V, D_MODEL)),
            _full((CHUNK, CHUNK)),
        ],
        out_specs=[
            pl.BlockSpec((1, TQ, D_MODEL), blk),
            pl.BlockSpec((1, C_HEADS, C_KEY_DIM, C_VAL_DIM), lambda b, j: (b, 0, 0, 0)),
        ],
        out_shape=[
            jax.ShapeDtypeStruct((nb, seq, D_MODEL), F32),
            jax.ShapeDtypeStruct((nb, C_HEADS, C_KEY_DIM, C_VAL_DIM), F32),
        ],
        scratch_shapes=[
            pltpu.VMEM((TQ, C_IN), F32),
            pltpu.VMEM((TQ, C_V), BF16),
            pltpu.VMEM((C_HEADS, C_VAL_DIM, C_KEY_DIM), F32),
            pltpu.VMEM((CHUNK, C_F), F32),
            pltpu.VMEM((CHUNK, C_F), F32),
            pltpu.VMEM((CHUNK, C_F), F32),
        ],
        compiler_params=pltpu.CompilerParams(
            dimension_semantics=("arbitrary", "arbitrary"), vmem_limit_bytes=VMEM_LIMIT),
        name="hgrn_prompt",
    )(x, nm, w_in, clb, on, w_out, jnp.asarray(_pair_level_table()))


def _ab_sample_proj_kernel(w00_ref, b0_ref, x_ref, nm_ref, win_ref, qn_ref, kn_ref, lng_ref, lnb_ref,
                           qx_ref, knew_ref, vnew_ref, bm_ref, gv_ref):
    n = x_ref.shape[0]
    h = _rms(x_ref[...], nm_ref[...]).astype(BF16)
    z = _dot(h, win_ref[...])
    zeros = jnp.zeros((n, A_HEAD_DIM), F32)
    for hh in range(A_HEADS):
        qh = _rms(z[:, hh * A_HEAD_DIM:(hh + 1) * A_HEAD_DIM], qn_ref[...]) * ATTN_SCALE
        qx_ref[hh] = jnp.concatenate([qh, zeros] if hh // A_GROUP == 0 else [zeros, qh], axis=-1)
    kparts = []
    for g in range(A_KV_HEADS):
        kparts.append(_rms(z[:, A_Q + g * A_HEAD_DIM:A_Q + (g + 1) * A_HEAD_DIM], kn_ref[...]))
    knew_ref[...] = jnp.concatenate(kparts, axis=-1)
    vnew_ref[...] = z[:, A_Q + A_KV:A_Q + 2 * A_KV]

    u = _gelu(z[:, A_Q + 2 * A_KV:A_Q + 2 * A_KV + B_WIDTH])
    vln = _layernorm(_gelu(z[:, A_Q + 2 * A_KV + B_WIDTH:AB_IN]), lng_ref[...], lnb_ref[...])
    grp = lax.broadcasted_iota(jnp.int32, (1, B_WIDTH), 1) // B_GROUP_DIM
    srow = jnp.zeros((1, B_WIDTH), F32)
    brow = jnp.zeros((1, B_WIDTH), F32)
    for g in range(B_GROUPS):
        srow = jnp.where(grp == g, w00_ref[g], srow)
        brow = jnp.where(grp == g, b0_ref[g], brow)
    bm_ref[...] = u * (vln * srow + brow)
    gv_ref[...] = vln


def _ab_sample_proj(x, nm, w_in, qn, kn, lng, lnb, w00, b0):
    n = x.shape[0]
    return pl.pallas_call(
        _ab_sample_proj_kernel,
        in_specs=[_SMEM, _SMEM] + [pl.BlockSpec(memory_space=pltpu.VMEM)] * 7,
        out_shape=[
            jax.ShapeDtypeStruct((A_HEADS, n, A_KV), F32),
            jax.ShapeDtypeStruct((n, A_KV), F32),
            jax.ShapeDtypeStruct((n, A_KV), F32),
            jax.ShapeDtypeStruct((n, B_WIDTH), F32),
            jax.ShapeDtypeStruct((n, B_WIDTH), F32),
        ],
        compiler_params=pltpu.CompilerParams(vmem_limit_bytes=VMEM_LIMIT),
        name="ab_sample_proj",
    )(w00, b0, x, nm, w_in, qn, kn, lng, lnb)


def _ab_sample_attn_kernel(ck_ref, cv_ref, qx_ref, kn_ref, vn_ref, sb_ref, sink_ref,
                           nk_ref, nv_ref, om_ref):
    wb = ck_ref.shape[1]
    head = lax.broadcasted_iota(jnp.int32, (1, A_HEADS, A_KV), 1)
    lane = lax.broadcasted_iota(jnp.int32, (1, A_HEADS, A_KV), 2)
    own_group = (head // A_GROUP) == (lane // A_HEAD_DIM)
    sink = sink_ref[...][None]
    kc, vc = ck_ref[...], cv_ref[...]
    kn, vn = kn_ref[...], vn_ref[...]
    nk_ref[:, 0:wb - 1, :] = kc[:, 1:wb, :]
    nk_ref[:, wb - 1:wb, :] = kn
    nv_ref[:, 0:wb - 1, :] = vc[:, 1:wb, :]
    nv_ref[:, wb - 1:wb, :] = vn
    q = qx_ref[...]
    s = jnp.einsum('bhd,bkd->bhk', q.astype(BF16), kc.astype(BF16), preferred_element_type=F32)
    s = s + sb_ref[:, 0:wb][None]
    sn = jnp.sum(q * kn, axis=-1, keepdims=True) + sb_ref[:, wb:wb + 1][None]
    m = jnp.maximum(jnp.maximum(jnp.max(s, axis=-1, keepdims=True), sn), sink)
    e = jnp.exp(s - m)
    en = jnp.exp(sn - m)
    r = 1.0 / (jnp.sum(e, axis=-1, keepdims=True) + en + jnp.exp(sink - m))
    o = jnp.einsum('bhk,bkd->bhd', (e * r).astype(BF16), vc.astype(BF16),
                   preferred_element_type=F32) + (en * r) * vn
    om_ref[...] = jnp.where(own_group, o, 0.0)


def _ab_sample_attn(ck, cv, qx, kn, vn, sb, sink):
    n, wb, _ = ck.shape
    blk3 = lambda i: (i, 0, 0)
    return pl.pallas_call(
        _ab_sample_attn_kernel,
        grid=(n // SB,),
        in_specs=[
            pl.BlockSpec((SB, wb, A_KV), blk3),
            pl.BlockSpec((SB, wb, A_KV), blk3),
            pl.BlockSpec((SB, A_HEADS, A_KV), blk3),
            pl.BlockSpec((SB, 1, A_KV), blk3),
            pl.BlockSpec((SB, 1, A_KV), blk3),
            _full((A_HEADS, 2 * WINDOW)),
            _full((A_HEADS, 1)),
        ],
        out_specs=[
            pl.BlockSpec((SB, wb, A_KV), blk3),
            pl.BlockSpec((SB, wb, A_KV), blk3),
            pl.BlockSpec((SB, A_HEADS, A_KV), blk3),
        ],
        out_shape=[
            jax.ShapeDtypeStruct((n, wb, A_KV), F32),
            jax.ShapeDtypeStruct((n, wb, A_KV), F32),
            jax.ShapeDtypeStruct((n, A_HEADS, A_KV), F32),
        ],
        compiler_params=pltpu.CompilerParams(dimension_semantics=("arbitrary",)),
        name="ab_sample_attn",
    )(ck, cv, qx, kn, vn, sb, sink)


def _residual_proj_kernel(x_ref, m_ref, w_ref, y_ref):
    y_ref[...] = x_ref[...] + _dot(m_ref[...].astype(BF16), w_ref[...])


def _residual_proj(x, mix, w):
    return pl.pallas_call(
        _residual_proj_kernel,
        out_shape=jax.ShapeDtypeStruct(x.shape, F32),
        name="residual_proj",
    )(x, mix, w)


def _hgrn_sample_proj_kernel(x_ref, nm_ref, win_ref, clb_ref, q_ref, f_ref, i_ref, sg_ref):
    h = _rms(x_ref[...], nm_ref[...]).astype(BF16)
    z = _dot(h, win_ref[...])
    lb = _lower_bound(clb_ref[...])
    q_ref[...] = z[:, 0:C_F]
    f_ref[...] = lb + (1.0 - lb) * jax.nn.sigmoid(z[:, C_F:2 * C_F])
    i_ref[...] = z[:, 2 * C_F:2 * C_F + C_V]
    sg_ref[...] = jax.nn.sigmoid(z[:, 2 * C_F + C_V:C_IN])


def _hgrn_sample_proj(x, nm, w_in, clb):
    n = x.shape[0]
    return pl.pallas_call(
        _hgrn_sample_proj_kernel,
        out_shape=[jax.ShapeDtypeStruct((n, C_F), F32)] * 2 + [jax.ShapeDtypeStruct((n, C_V), F32)] * 2,
        compiler_params=pltpu.CompilerParams(vmem_limit_bytes=VMEM_LIMIT),
        name="hgrn_sample_proj",
    )(x, nm, w_in, clb)


def _hgrn_sample_state_kernel(s_ref, fc_ref, q_ref, i_ref, sn_ref, o_ref):
    out_rows = []
    for s in range(SB):
        parts = []
        for hd in range(C_HEADS):
            hs = slice(hd * C_VAL_DIM, (hd + 1) * C_VAL_DIM)
            fb = jnp.broadcast_to(fc_ref[0, hd, :, s:s + 1], (C_KEY_DIM, C_VAL_DIM))
            sn = fb * s_ref[s, hd] + (1.0 - fb) * i_ref[s:s + 1, hs]
            sn_ref[s, hd] = sn
            parts.append(_dot(q_ref[s:s + 1, hs].astype(BF16), sn.astype(BF16)))
        out_rows.append(jnp.concatenate(parts, axis=-1))
    o_ref[...] = jnp.concatenate(out_rows, axis=0)


def _hgrn_sample_state(state, fc, q, iv):
    n = state.shape[0]
    sblk = pl.BlockSpec((SB, C_HEADS, C_KEY_DIM, C_VAL_DIM), lambda i: (i, 0, 0, 0))
    cblk = pl.BlockSpec((1, C_HEADS, C_KEY_DIM, SB), lambda i: (i, 0, 0, 0))
    rblk = pl.BlockSpec((SB, C_V), lambda i: (i, 0))
    return pl.pallas_call(
        _hgrn_sample_state_kernel,
        grid=(n // SB,),
        in_specs=[sblk, cblk, rblk, rblk],
        out_specs=[sblk, pl.BlockSpec((SB, C_V), lambda i: (i, 0))],
        out_shape=[
            jax.ShapeDtypeStruct(state.shape, F32),
            jax.ShapeDtypeStruct((n, C_V), F32),
        ],
        compiler_params=pltpu.CompilerParams(
            dimension_semantics=("arbitrary",), vmem_limit_bytes=VMEM_LIMIT),
        name="hgrn_sample_state",
    )(state, fc, q, iv)


def _hgrn_sample_out_kernel(o_ref, sg_ref, on_ref, w_ref, x_ref, y_ref):
    parts = []
    for hd in range(C_HEADS):
        parts.append(_rms(o_ref[:, hd * C_VAL_DIM:(hd + 1) * C_VAL_DIM], on_ref[...]))
    on = jnp.concatenate(parts, axis=-1) * sg_ref[...]
    y_ref[...] = x_ref[...] + _dot(on.astype(BF16), w_ref[...])


def _hgrn_sample_out(o, sg, on, w_out, x):
    return pl.pallas_call(
        _hgrn_sample_out_kernel,
        out_shape=jax.ShapeDtypeStruct(x.shape, F32),
        name="hgrn_sample_out",
    )(o, sg, on, w_out, x)


def _to_columns(a):
    n = a.shape[0]
    return a.reshape(n // SB, SB, C_HEADS, C_KEY_DIM).transpose(0, 2, 3, 1)


def kernel(x_prompt, x_sample, cache_k, cache_v, state_hgrn, norm_mix, norm_ffn, w_in_ab, w_out_ab,
           q_norm, k_norm, attn_sink, rel_bias, gmlp_ln_g, gmlp_ln_b, gmlp_w_s, gmlp_b_s,
           w_in_c, c_lower_bounds, c_out_norm, w_out_c, w_gate, w_up, w_down):
    assert norm_mix.shape[0] == DEPTH == 2 and w_in_ab.shape[0] == 1 and w_in_c.shape[0] == 1
    nb, seq, _ = x_prompt.shape
    ns = x_sample.shape[0]
    assert x_sample.shape[1] == 1 and cache_k.shape[2] == WINDOW

    row = lambda v: v.reshape(1, -1)
    bf = lambda w: w.astype(BF16)
    w_in_ab0, w_out_ab0 = bf(w_in_ab[0]), bf(w_out_ab[0])
    w_in_c0, w_out_c0 = bf(w_in_c[0]), bf(w_out_c[0])
    wg, wu, wd = bf(w_gate), bf(w_up), bf(w_down)
    nm, nf = norm_mix, norm_ffn
    qn, kn = row(q_norm[0]), row(k_norm[0])
    lng, lnb = row(gmlp_ln_g[0]), row(gmlp_ln_b[0])
    sink = attn_sink[0]

    tab, tabp = _bias_table(rel_bias)

    xp, knew_p, vnew_p, gv_p = _ab_prompt(
        x_prompt, row(nm[0]), w_in_ab0, jnp.tile(qn, (1, A_HEADS)), jnp.tile(kn, (1, A_KV_HEADS)),
        sink, tabp, lng, lnb, gmlp_w_s[0], jnp.repeat(gmlp_b_s[0].T, B_GROUP_DIM, axis=1), w_out_ab0)
    xp = _ffn(xp.reshape(nb * seq, D_MODEL), row(nf[0]), wg[0], wu[0], wd[0]).reshape(nb, seq, D_MODEL)
    xp, st_p = _hgrn_prompt(xp, row(nm[1]), w_in_c0, c_lower_bounds, row(c_out_norm[0]), w_out_c0)
    xp = _ffn(xp.reshape(nb * seq, D_MODEL), row(nf[1]), wg[1], wu[1], wd[1]).reshape(nb, seq, D_MODEL)

    xs = x_sample.reshape(ns, D_MODEL)
    qx, knew_s, vnew_s, bm_s, gv_s = _ab_sample_proj(
        xs, row(nm[0]), w_in_ab0, qn, kn, lng, lnb, gmlp_w_s[0, :, 0, 0], gmlp_b_s[0, :, 0])
    sb = jnp.pad(tab[:, WINDOW - 1, WINDOW - 1:], ((0, 0), (0, WINDOW - 1)))
    nk_s, nv_s, om = _ab_sample_attn(
        cache_k[0].reshape(ns, WINDOW, A_KV), cache_v[0].reshape(ns, WINDOW, A_KV),
        qx.transpose(1, 0, 2), knew_s[:, None, :], vnew_s[:, None, :], sb, sink.reshape(A_HEADS, 1))
    om = om.reshape(ns, A_KV_HEADS, A_GROUP, A_KV_HEADS, A_HEAD_DIM)
    a_s = jnp.stack([om[:, g, :, g, :] for g in range(A_KV_HEADS)], axis=1).reshape(ns, A_Q)
    xs = _residual_proj(xs, jnp.concatenate([a_s, bm_s], axis=-1), w_out_ab0)
    xs = _ffn(xs, row(nf[0]), wg[0], wu[0], wd[0])

    q_s, f_s, i_s, sg_s = _hgrn_sample_proj(xs, row(nm[1]), w_in_c0, c_lower_bounds)
    st_s, o_s = _hgrn_sample_state(state_hgrn[0], _to_columns(f_s), q_s, i_s)
    xs = _hgrn_sample_out(o_s, sg_s, row(c_out_norm[0]), w_out_c0, xs)
    xs = _ffn(xs, row(nf[1]), wg[1], wu[1], wd[1])

    kv5 = lambda a: a.reshape(1, a.shape[0], WINDOW, A_KV_HEADS, A_HEAD_DIM)
    return (xp, xs.reshape(ns, 1, D_MODEL),
            kv5(knew_p), kv5(vnew_p), kv5(nk_s), kv5(nv_s),
            gv_p[None], gv_s.reshape(1, ns, 1, B_WIDTH),
            st_p[None], st_s[None])
```

```python
import math

import jax
import jax.numpy as jnp
import numpy as np
from jax import lax
from jax.experimental import pallas as pl
from jax.experimental.pallas import tpu as pltpu

F32 = jnp.float32
BF16 = jnp.bfloat16

D_MODEL = 1024
DEPTH = 2
A_HEADS = 8
A_KV_HEADS = 2
A_GROUP = A_HEADS // A_KV_HEADS
A_HEAD_DIM = 64
WINDOW = 128
ATTN_SCALE = A_HEAD_DIM ** -0.5
NUM_BUCKETS = 32
MAX_DISTANCE = 128
A_Q = A_HEADS * A_HEAD_DIM
A_KV = A_KV_HEADS * A_HEAD_DIM
B_GROUPS = 8
B_GROUP_DIM = 64
B_WIDTH = B_GROUPS * B_GROUP_DIM
B_CHUNK = 128
AB_IN = A_Q + 2 * A_KV + 2 * B_WIDTH
AB_MIX = A_Q + B_WIDTH
C_HEADS = 8
C_KEY_DIM = 128
C_VAL_DIM = 128
C_F = C_HEADS * C_KEY_DIM
C_V = C_HEADS * C_VAL_DIM
C_IN = 2 * C_F + 2 * C_V
D_FF = 2816
EPS = 1e-6

NEG = -1e30

VMEM_LIMIT = 56 * 1024 * 1024
VREG_ROWS = 8

CHUNK = 128
TQ = 1024
TM = 512
SB = 8
HEAD_SKEW = 2
SAFE_LOG2_RANGE = 64.0

_NT = (((1,), (1,)), ((), ()))
_TN = (((0,), (0,)), ((), ()))


def _rms(x, g):
    return x * lax.rsqrt(jnp.mean(x * x, axis=-1, keepdims=True) + EPS) * g


def _gelu(x):
    return 0.5 * x * (1.0 + lax.erf(x * math.sqrt(0.5)))


def _layernorm(x, g, b):
    xc = x - jnp.mean(x, axis=-1, keepdims=True)
    return xc * lax.rsqrt(jnp.mean(xc * xc, axis=-1, keepdims=True) + EPS) * g + b


def _dot(a, b):
    return jnp.dot(a, b, preferred_element_type=F32)


def _full(shape):
    n = len(shape)
    return pl.BlockSpec(shape, lambda *_: (0,) * n)


def _resident(shape):
    n = len(shape)
    return pl.BlockSpec(shape, lambda *_: (0,) * n, pipeline_mode=pl.Buffered(1))


_SMEM = pl.BlockSpec(memory_space=pltpu.SMEM)


def _bias_table_kernel(rel_ref, tab_ref, tabp_ref):
    qi = lax.broadcasted_iota(jnp.int32, (WINDOW, 2 * WINDOW), 0)
    kj = lax.broadcasted_iota(jnp.int32, (WINDOW, 2 * WINDOW), 1)
    dist = qi + WINDOW - kj
    ok = (dist >= 0) & (dist < WINDOW)
    max_exact = NUM_BUCKETS // 2
    d = jnp.maximum(dist, 0)
    dl = jnp.maximum(d, 1).astype(F32)
    large = max_exact + (jnp.log(dl / max_exact) / math.log(MAX_DISTANCE / max_exact)
                         * (NUM_BUCKETS - max_exact)).astype(jnp.int32)
    large = jnp.minimum(large, NUM_BUCKETS - 1)
    bucket = jnp.where(d < max_exact, d, large)
    for h in range(A_HEADS):
        acc = jnp.zeros((WINDOW, 2 * WINDOW), F32)
        for b in range(NUM_BUCKETS):
            acc = jnp.where(bucket == b, rel_ref[b, h], acc)
        t = jnp.where(ok, acc, NEG)
        tab_ref[h] = t
        cols = slice((h % 2) * 2 * WINDOW, (h % 2 + 1) * 2 * WINDOW)
        tabp_ref[0, h // 2, :, cols] = t
        tabp_ref[1, h // 2, :, cols] = jnp.where(kj < WINDOW, NEG, t)


def _bias_table(rel_bias):
    return pl.pallas_call(
        _bias_table_kernel,
        out_shape=[
            jax.ShapeDtypeStruct((A_HEADS, WINDOW, 2 * WINDOW), F32),
            jax.ShapeDtypeStruct((2, A_HEADS // 2, WINDOW, 4 * WINDOW), F32),
        ],
        in_specs=[_SMEM],
        name="bias_table",
    )(rel_bias)


PAIR = 2 * A_HEAD_DIM
N_PAIRS = A_HEADS // 2


def _ab_prompt_kernel(sink_ref, x_ref, nm_ref, win_ref, qg_ref, kg_ref, tabp_ref, lng_ref, lnb_ref,
                      ws_ref, bsp_ref, wout_ref,
                      y_ref, knew_ref, vnew_ref, gv_ref,
                      z_ref, mix_ref, kp_ref, kpr_ref, vp_ref, vpr_ref, wpair_ref):
    j = pl.program_id(1)
    last_j = pl.num_programs(1) - 1
    n_chunks = TQ // CHUNK

    @pl.when(j == 0)
    def _():
        for ref in (kp_ref, kpr_ref, vp_ref, vpr_ref):
            ref[...] = jnp.zeros_like(ref)
        row = lax.broadcasted_iota(jnp.int32, (B_CHUNK, B_CHUNK), 0)
        col = lax.broadcasted_iota(jnp.int32, (B_CHUNK, B_CHUNK), 1)
        for g in range(B_GROUPS):
            wpair_ref[g // 2, :, (g % 2) * B_CHUNK:(g % 2 + 1) * B_CHUNK] = jnp.where(
                row >= col, ws_ref[g], 0.0).astype(BF16)

    h = _rms(x_ref[0], nm_ref[...]).astype(BF16)
    z_ref[...] = _dot(h, win_ref[...])

    lo_half = lax.broadcasted_iota(jnp.int32, (1, PAIR), 1) < A_HEAD_DIM
    r_i = lax.broadcasted_iota(jnp.int32, (PAIR, PAIR), 0) // A_HEAD_DIM
    c_i = lax.broadcasted_iota(jnp.int32, (PAIR, PAIR), 1) // A_HEAD_DIM
    half_mean = jnp.where(r_i == c_i, 1.0 / A_HEAD_DIM, 0.0).astype(BF16)

    def mean_sq_halves(x):
        x2 = x * x
        hi = x2.astype(BF16)
        lo = (x2 - hi.astype(F32)).astype(BF16)
        return _dot(hi, half_mean) + _dot(lo, half_mean)

    def block_diag(top, bot):
        zero = jnp.zeros_like(top)
        return jnp.concatenate([jnp.where(lo_half, top, zero), jnp.where(lo_half, zero, bot)], axis=0)

    def chunk(c, carry):
        r0 = pl.multiple_of(c * CHUNK, CHUNK)
        rows = pl.ds(r0, CHUNK)
        first = jnp.where(jnp.logical_and(j == 0, c == 0), 1, 0)

        kraw = z_ref[rows, A_Q:A_Q + A_KV]
        v = z_ref[rows, A_Q + A_KV:A_Q + 2 * A_KV]
        kn = kraw * lax.rsqrt(mean_sq_halves(kraw) + EPS) * kg_ref[...]
        kb, kbr = kn.astype(BF16), pltpu.roll(kn, A_HEAD_DIM, 1).astype(BF16)
        vb, vbr = v.astype(BF16), pltpu.roll(v, A_HEAD_DIM, 1).astype(BF16)
        k2 = jnp.concatenate([kp_ref[...], kb], axis=0)
        k2r = jnp.concatenate([kpr_ref[...], kbr], axis=0)
        v2 = jnp.concatenate([vp_ref[...], vb], axis=0)
        v2r = jnp.concatenate([vpr_ref[...], vbr], axis=0)
        kbd = [block_diag(k2, k2r), block_diag(k2r, k2)]
        vbd = [block_diag(v2, v2r), block_diag(v2r, v2)]

        scores = []
        for i in range(N_PAIRS):
            ps = slice(i * PAIR, (i + 1) * PAIR)
            qraw = z_ref[rows, i * PAIR:(i + 1) * PAIR]
            qn = qraw * lax.rsqrt(mean_sq_halves(qraw) + EPS) * (qg_ref[:, ps] * ATTN_SCALE)
            s = lax.dot_general(qn.astype(BF16), kbd[i // (A_GROUP // 2)], _NT,
                                preferred_element_type=F32)
            scores.append(s + tabp_ref[first, i])
        outs = []
        for i in range(N_PAIRS):
            es, rs = [], []
            for hh in range(2):
                sh = scores[i][:, hh * 2 * WINDOW:(hh + 1) * 2 * WINDOW]
                sk = sink_ref[2 * i + hh]
                m = jnp.maximum(jnp.max(sh, axis=-1, keepdims=True), sk)
                e = jnp.exp(sh - m)
                rs.append(1.0 / (jnp.sum(e, axis=-1, keepdims=True) + jnp.exp(sk - m)))
                es.append(e.astype(BF16))
            o = _dot(jnp.concatenate(es, axis=-1), vbd[i // (A_GROUP // 2)])
            outs.append(o * jnp.where(lo_half, rs[0], rs[1]))
        mix_ref[rows, 0:A_Q] = jnp.concatenate(outs, axis=-1).astype(BF16)

        zu = z_ref[rows, A_Q + 2 * A_KV:A_Q + 2 * A_KV + B_WIDTH]
        zv = z_ref[rows, A_Q + 2 * A_KV + B_WIDTH:AB_IN]
        u = _gelu(zu)
        vln = _layernorm(_gelu(zv), lng_ref[...], lnb_ref[...])
        vlb = vln.astype(BF16)
        sparts = []
        for i in range(B_GROUPS // 2):
            vpair = vlb[:, i * PAIR:(i + 1) * PAIR]
            sparts.append(_dot(wpair_ref[i], block_diag(vpair, vpair)))
        bm = u * (jnp.concatenate(sparts, axis=-1) + bsp_ref[...])
        mix_ref[rows, A_Q:AB_MIX] = bm.astype(BF16)

        kp_ref[...] = kb
        kpr_ref[...] = kbr
        vp_ref[...] = vb
        vpr_ref[...] = vbr

        @pl.when(jnp.logical_and(j == last_j, c == n_chunks - 1))
        def _():
            knew_ref[0] = kn
            vnew_ref[0] = v
            gv_ref[0] = vln

        return carry

    lax.fori_loop(0, n_chunks, chunk, 0)
    y_ref[0] = x_ref[0] + _dot(mix_ref[...], wout_ref[...])


def _ab_prompt(x, nm, w_in, qg, kg, sink, tabp, lng, lnb, w_s, bsp, w_out):
    nb, seq, _ = x.shape
    grid = (nb, seq // TQ)
    blk = lambda b, j: (b, j, 0)
    per_b = lambda b, j: (b, 0, 0)
    return pl.pallas_call(
        _ab_prompt_kernel,
        grid=grid,
        in_specs=[
            _SMEM,
            pl.BlockSpec((1, TQ, D_MODEL), blk),
            _full((1, D_MODEL)),
            _resident((D_MODEL, AB_IN)),
            _full((1, A_Q)),
            _full((1, A_KV)),
            _resident((2, N_PAIRS, WINDOW, 4 * WINDOW)),
            _full((1, B_WIDTH)),
            _full((1, B_WIDTH)),
            _resident((B_GROUPS, B_CHUNK, B_CHUNK)),
            _resident((B_CHUNK, B_WIDTH)),
            _resident((AB_MIX, D_MODEL)),
        ],
        out_specs=[
            pl.BlockSpec((1, TQ, D_MODEL), blk),
            pl.BlockSpec((1, WINDOW, A_KV), per_b),
            pl.BlockSpec((1, WINDOW, A_KV), per_b),
            pl.BlockSpec((1, B_CHUNK, B_WIDTH), per_b),
        ],
        out_shape=[
            jax.ShapeDtypeStruct((nb, seq, D_MODEL), F32),
            jax.ShapeDtypeStruct((nb, WINDOW, A_KV), F32),
            jax.ShapeDtypeStruct((nb, WINDOW, A_KV), F32),
            jax.ShapeDtypeStruct((nb, B_CHUNK, B_WIDTH), F32),
        ],
        scratch_shapes=[
            pltpu.VMEM((TQ, AB_IN), F32),
            pltpu.VMEM((TQ, AB_MIX), BF16),
            pltpu.VMEM((WINDOW, A_KV), BF16),
            pltpu.VMEM((WINDOW, A_KV), BF16),
            pltpu.VMEM((WINDOW, A_KV), BF16),
            pltpu.VMEM((WINDOW, A_KV), BF16),
            pltpu.VMEM((B_GROUPS // 2, B_CHUNK, 2 * B_CHUNK), BF16),
        ],
        compiler_params=pltpu.CompilerParams(
            dimension_semantics=("arbitrary", "arbitrary"), vmem_limit_bytes=VMEM_LIMIT),
        name="ab_prompt",
    )(sink, x, nm, w_in, qg, kg, tabp, lng, lnb, w_s, bsp, w_out)


def _ffn_kernel(x_ref, g_ref, wg_ref, wu_ref, wd_ref, o_ref):
    x = x_ref[...]
    h = _rms(x, g_ref[...]).astype(BF16)
    gate = _dot(h, wg_ref[...])
    up = _dot(h, wu_ref[...])
    a = (gate * jax.nn.sigmoid(gate) * up).astype(BF16)
    o_ref[...] = x + _dot(a, wd_ref[...])


def _ffn(x, g, wg, wu, wd):
    rows = x.shape[0]
    tm = min(TM, rows)
    blk = lambda i: (i, 0)
    return pl.pallas_call(
        _ffn_kernel,
        grid=(rows // tm,),
        in_specs=[
            pl.BlockSpec((tm, D_MODEL), blk),
            _full((1, D_MODEL)),
            _resident((D_MODEL, D_FF)),
            _resident((D_MODEL, D_FF)),
            _resident((D_FF, D_MODEL)),
        ],
        out_specs=pl.BlockSpec((tm, D_MODEL), blk),
        out_shape=jax.ShapeDtypeStruct((rows, D_MODEL), F32),
        compiler_params=pltpu.CompilerParams(
            dimension_semantics=("arbitrary",), vmem_limit_bytes=VMEM_LIMIT),
        name="ffn",
    )(x, g, wg, wu, wd)


def _lower_bound(clb):
    m = jnp.max(clb, axis=0, keepdims=True)
    e = jnp.exp(clb - m)
    sm = e / jnp.sum(e, axis=0, keepdims=True)
    return (sm[0:1] + sm[1:2]) - sm[0:1]


def _split3(x):
    hi = x.astype(BF16)
    r = x - hi.astype(F32)
    mid = r.astype(BF16)
    lo = (r - mid.astype(F32)).astype(BF16)
    return hi, mid, lo


def _neg_abs(x):
    return lax.bitcast_convert_type(
        lax.bitcast_convert_type(x, jnp.uint32) | jnp.uint32(0x80000000), F32)


def _pair_level_table():
    t = np.arange(CHUNK)[:, None]
    s = np.arange(CHUNK)[None, :]
    lev = np.floor(np.log2(np.maximum(t ^ s, 1))).astype(np.int32)
    lev = np.where(t == s, -1, lev)
    return np.where(s > t, -2, lev).astype(np.int32)


def _level_operand(p, q, kk, f, b2):
    m = 2 ** p
    if m < VREG_ROWS:
        shape3 = (CHUNK // VREG_ROWS, VREG_ROWS, q.shape[1])
        sub = lax.broadcasted_iota(jnp.int32, (1, VREG_ROWS, q.shape[1]), 1)
        upper = ((sub >> p) & 1) == 1
        q3, k3 = q.reshape(shape3), kk.reshape(shape3)
        if p == 0:
            y = jnp.where(upper, q3 * f.reshape(shape3), k3)
        else:
            b3 = b2.reshape(shape3)
            be = b3[:, m - 1:m, :]
            for k in range(1, VREG_ROWS // (2 * m)):
                be = jnp.where(sub >= 2 * m * k, b3[:, 2 * m * k + m - 1:2 * m * k + m, :], be)
            y = jnp.where(upper, q3, k3) * jnp.exp2(_neg_abs(b3 - be))
        return y.reshape(q.shape).astype(BF16)
    parts = []
    for k in range(CHUNK // (2 * m)):
        lo = slice(2 * m * k, 2 * m * k + m)
        up = slice(2 * m * k + m, 2 * m * (k + 1))
        be = b2[2 * m * k + m - 1:2 * m * k + m, :]
        parts.append(kk[lo] * jnp.exp2(be - b2[lo]))
        parts.append(q[up] * jnp.exp2(b2[up] - be))
    return jnp.concatenate(parts, axis=0).astype(BF16)


def _merge_level(p, att, pm, lev):
    m = 2 ** p
    if m < VREG_ROWS:
        return jnp.where(lev == p, pm, att)
    col = lax.broadcasted_iota(jnp.int32, (1, CHUNK), 1)
    parts = []
    for k in range(CHUNK // (2 * m)):
        lo = slice(2 * m * k, 2 * m * k + m)
        up = slice(2 * m * k + m, 2 * m * (k + 1))
        parts.append(att[lo])
        parts.append(jnp.where((col >= 2 * m * k) & (col < 2 * m * k + m), pm[up], att[up]))
    return jnp.concatenate(parts, axis=0)


def _hgrn_prompt_kernel(x_ref, nm_ref, win_ref, clb_ref, on_ref, wout_ref, lev_ref,
                        y_ref, st_ref,
                        z_ref, o_ref, stt_ref, f_ref, k_ref, b_ref):
    j = pl.program_id(1)
    last_j = pl.num_programs(1) - 1
    n_chunks = TQ // CHUNK
    n_levels = int(math.log2(CHUNK))

    @pl.when(j == 0)
    def _():
        stt_ref[...] = jnp.zeros_like(stt_ref)

    h = _rms(x_ref[0], nm_ref[...]).astype(BF16)
    z_ref[...] = _dot(h, win_ref[...])
    lb = _lower_bound(clb_ref[...])

    row = lax.broadcasted_iota(jnp.int32, (CHUNK, CHUNK), 0)
    col = lax.broadcasted_iota(jnp.int32, (CHUNK, CHUNK), 1)
    ltri = (row >= col).astype(BF16)

    def z_cols(rows, start, width):
        return z_ref[rows, start:start + width]

    def chunk(c, carry):
        r0 = pl.multiple_of(c * CHUNK, CHUNK)
        rows = pl.ds(r0, CHUNK)
        f_all = lb + (1.0 - lb) * jax.nn.sigmoid(z_cols(rows, C_F, C_F))
        f_ref[...] = f_all
        k_ref[...] = 1.0 - f_all
        hi, mid, lo = _split3(jnp.log2(f_all))
        b_ref[...] = (_dot(ltri, hi) + _dot(ltri, mid)) + _dot(ltri, lo)

        lev = lev_ref[...]

        def products(hd):
            ks = slice(hd * C_KEY_DIM, (hd + 1) * C_KEY_DIM)
            q = z_cols(rows, hd * C_KEY_DIM, C_KEY_DIM)
            iv = z_cols(rows, 2 * C_F + hd * C_VAL_DIM, C_VAL_DIM)
            f = f_ref[:, ks]
            kk = k_ref[:, ks]
            b2 = b_ref[:, ks]
            diag = jnp.sum(q * kk, axis=-1, keepdims=True)
            pms = []
            for p in range(n_levels):
                y = _level_operand(p, q, kk, f, b2)
                pms.append(lax.dot_general(y, y, _NT, preferred_element_type=F32))
            ivb = iv.astype(BF16)
            stt = stt_ref[hd]
            o_prev = lax.dot_general((q * jnp.exp2(b2)).astype(BF16), stt.astype(BF16), _NT,
                                     preferred_element_type=F32)
            b_last = b2[CHUNK - 1:CHUNK, :]
            kd = (kk * jnp.exp2(b_last - b2)).astype(BF16)
            stt_ref[hd] = stt * jnp.exp2(b_last) + lax.dot_general(
                ivb, kd, _TN, preferred_element_type=F32)
            return diag, pms, o_prev, ivb

        def finish(hd, diag, pms, o_prev, ivb):
            gt = z_cols(rows, 2 * C_F + C_V + hd * C_VAL_DIM, C_VAL_DIM)
            att = jnp.where(lev == -1, diag, 0.0)
            for p in range(n_levels):
                att = _merge_level(p, att, pms[p], lev)
            o = o_prev + _dot(att.astype(BF16), ivb)
            o = _rms(o, on_ref[...]) * jax.nn.sigmoid(gt)
            o_ref[rows, hd * C_VAL_DIM:(hd + 1) * C_VAL_DIM] = o.astype(BF16)

        def factored_head(hd):
            ks = slice(hd * C_KEY_DIM, (hd + 1) * C_KEY_DIM)
            q = z_cols(rows, hd * C_KEY_DIM, C_KEY_DIM)
            ivb = z_cols(rows, 2 * C_F + hd * C_VAL_DIM, C_VAL_DIM).astype(BF16)
            gt = z_cols(rows, 2 * C_F + C_V + hd * C_VAL_DIM, C_VAL_DIM)
            b2 = b_ref[:, ks]
            b_mid = b2[CHUNK // 2 - 1:CHUNK // 2, :]
            b_last = b2[CHUNK - 1:CHUNK, :]
            qs = (q * jnp.exp2(b2 - b_mid)).astype(BF16)
            kd = (k_ref[:, ks] * jnp.exp2(b_mid - b2)).astype(BF16)
            att = jnp.where(row >= col, lax.dot_general(qs, kd, _NT, preferred_element_type=F32), 0.0)
            stt = stt_ref[hd]
            o = lax.dot_general(qs, (stt * jnp.exp2(b_mid)).astype(BF16), _NT,
                                preferred_element_type=F32) + _dot(att.astype(BF16), ivb)
            stt_ref[hd] = stt * jnp.exp2(b_last) + jnp.exp2(b_last - b_mid) * lax.dot_general(
                ivb, kd, _TN, preferred_element_type=F32)
            o = _rms(o, on_ref[...]) * jax.nn.sigmoid(gt)
            o_ref[rows, hd * C_VAL_DIM:(hd + 1) * C_VAL_DIM] = o.astype(BF16)

        b_mid_all = b_ref[CHUNK // 2 - 1:CHUNK // 2, :]
        b_last_all = b_ref[CHUNK - 1:CHUNK, :]
        bounded = jnp.max(jnp.maximum(-b_mid_all, b_mid_all - b_last_all)) <= SAFE_LOG2_RANGE

        @pl.when(bounded)
        def _():
            for hd in range(C_HEADS):
                factored_head(hd)

        @pl.when(jnp.logical_not(bounded))
        def _():
            pending = [products(hd) for hd in range(HEAD_SKEW)]
            for hd in range(C_HEADS):
                if hd + HEAD_SKEW < C_HEADS:
                    pending.append(products(hd + HEAD_SKEW))
                finish(hd, *pending.pop(0))

        return carry

    lax.fori_loop(0, n_chunks, chunk, 0)
    y_ref[0] = x_ref[0] + _dot(o_ref[...], wout_ref[...])

    @pl.when(j == last_j)
    def _():
        for hd in range(C_HEADS):
            st_ref[0, hd] = stt_ref[hd].T


def _hgrn_prompt(x, nm, w_in, clb, on, w_out):
    nb, seq, _ = x.shape
    grid = (nb, seq // TQ)
    blk = lambda b, j: (b, j, 0)
    return pl.pallas_call(
        _hgrn_prompt_kernel,
        grid=grid,
        in_specs=[
            pl.BlockSpec((1, TQ, D_MODEL), blk),
            _full((1, D_MODEL)),
            _resident((D_MODEL, C_IN)),
            _full((DEPTH, C_F)),
            _full((1, C_VAL_DIM)),
            _resident((C_V, D_MODEL)),
            _full((CHUNK, CHUNK)),
        ],
        out_specs=[
            pl.BlockSpec((1, TQ, D_MODEL), blk),
            pl.BlockSpec((1, C_HEADS, C_KEY_DIM, C_VAL_DIM), lambda b, j: (b, 0, 0, 0)),
        ],
        out_shape=[
            jax.ShapeDtypeStruct((nb, seq, D_MODEL), F32),
            jax.ShapeDtypeStruct((nb, C_HEADS, C_KEY_DIM, C_VAL_DIM), F32),
        ],
        scratch_shapes=[
            pltpu.VMEM((TQ, C_IN), F32),
            pltpu.VMEM((TQ, C_V), BF16),
            pltpu.VMEM((C_HEADS, C_VAL_DIM, C_KEY_DIM), F32),
            pltpu.VMEM((CHUNK, C_F), F32),
            pltpu.VMEM((CHUNK, C_F), F32),
            pltpu.VMEM((CHUNK, C_F), F32),
        ],
        compiler_params=pltpu.CompilerParams(
            dimension_semantics=("arbitrary", "arbitrary"), vmem_limit_bytes=VMEM_LIMIT),
        name="hgrn_prompt",
    )(x, nm, w_in, clb, on, w_out, jnp.asarray(_pair_level_table()))


def _ab_sample_proj_kernel(w00_ref, b0_ref, x_ref, nm_ref, win_ref, qn_ref, kn_ref, lng_ref, lnb_ref,
                           qx_ref, knew_ref, vnew_ref, bm_ref, gv_ref):
    n = x_ref.shape[0]
    h = _rms(x_ref[...], nm_ref[...]).astype(BF16)
    z = _dot(h, win_ref[...])
    zeros = jnp.zeros((n, A_HEAD_DIM), F32)
    for hh in range(A_HEADS):
        qh = _rms(z[:, hh * A_HEAD_DIM:(hh + 1) * A_HEAD_DIM], qn_ref[...]) * ATTN_SCALE
        qx_ref[hh] = jnp.concatenate([qh, zeros] if hh // A_GROUP == 0 else [zeros, qh], axis=-1)
    kparts = []
    for g in range(A_KV_HEADS):
        kparts.append(_rms(z[:, A_Q + g * A_HEAD_DIM:A_Q + (g + 1) * A_HEAD_DIM], kn_ref[...]))
    knew_ref[...] = jnp.concatenate(kparts, axis=-1)
    vnew_ref[...] = z[:, A_Q + A_KV:A_Q + 2 * A_KV]

    u = _gelu(z[:, A_Q + 2 * A_KV:A_Q + 2 * A_KV + B_WIDTH])
    vln = _layernorm(_gelu(z[:, A_Q + 2 * A_KV + B_WIDTH:AB_IN]), lng_ref[...], lnb_ref[...])
    grp = lax.broadcasted_iota(jnp.int32, (1, B_WIDTH), 1) // B_GROUP_DIM
    srow = jnp.zeros((1, B_WIDTH), F32)
    brow = jnp.zeros((1, B_WIDTH), F32)
    for g in range(B_GROUPS):
        srow = jnp.where(grp == g, w00_ref[g], srow)
        brow = jnp.where(grp == g, b0_ref[g], brow)
    bm_ref[...] = u * (vln * srow + brow)
    gv_ref[...] = vln


def _ab_sample_proj(x, nm, w_in, qn, kn, lng, lnb, w00, b0):
    n = x.shape[0]
    return pl.pallas_call(
        _ab_sample_proj_kernel,
        in_specs=[_SMEM, _SMEM] + [pl.BlockSpec(memory_space=pltpu.VMEM)] * 7,
        out_shape=[
            jax.ShapeDtypeStruct((A_HEADS, n, A_KV), F32),
            jax.ShapeDtypeStruct((n, A_KV), F32),
            jax.ShapeDtypeStruct((n, A_KV), F32),
            jax.ShapeDtypeStruct((n, B_WIDTH), F32),
            jax.ShapeDtypeStruct((n, B_WIDTH), F32),
        ],
        compiler_params=pltpu.CompilerParams(vmem_limit_bytes=VMEM_LIMIT),
        name="ab_sample_proj",
    )(w00, b0, x, nm, w_in, qn, kn, lng, lnb)


def _ab_sample_attn_kernel(ck_ref, cv_ref, qx_ref, kn_ref, vn_ref, sb_ref, sink_ref,
                           nk_ref, nv_ref, om_ref):
    wb = ck_ref.shape[1]
    head = lax.broadcasted_iota(jnp.int32, (1, A_HEADS, A_KV), 1)
    lane = lax.broadcasted_iota(jnp.int32, (1, A_HEADS, A_KV), 2)
    own_group = (head // A_GROUP) == (lane // A_HEAD_DIM)
    sink = sink_ref[...][None]
    kc, vc = ck_ref[...], cv_ref[...]
    kn, vn = kn_ref[...], vn_ref[...]
    nk_ref[:, 0:wb - 1, :] = kc[:, 1:wb, :]
    nk_ref[:, wb - 1:wb, :] = kn
    nv_ref[:, 0:wb - 1, :] = vc[:, 1:wb, :]
    nv_ref[:, wb - 1:wb, :] = vn
    q = qx_ref[...]
    s = jnp.einsum('bhd,bkd->bhk', q.astype(BF16), kc.astype(BF16), preferred_element_type=F32)
    s = s + sb_ref[:, 0:wb][None]
    sn = jnp.sum(q * kn, axis=-1, keepdims=True) + sb_ref[:, wb:wb + 1][None]
    m = jnp.maximum(jnp.maximum(jnp.max(s, axis=-1, keepdims=True), sn), sink)
    e = jnp.exp(s - m)
    en = jnp.exp(sn - m)
    r = 1.0 / (jnp.sum(e, axis=-1, keepdims=True) + en + jnp.exp(sink - m))
    o = jnp.einsum('bhk,bkd->bhd', (e * r).astype(BF16), vc.astype(BF16),
                   preferred_element_type=F32) + (en * r) * vn
    om_ref[...] = jnp.where(own_group, o, 0.0)


def _ab_sample_attn(ck, cv, qx, kn, vn, sb, sink):
    n, wb, _ = ck.shape
    blk3 = lambda i: (i, 0, 0)
    return pl.pallas_call(
        _ab_sample_attn_kernel,
        grid=(n // SB,),
        in_specs=[
            pl.BlockSpec((SB, wb, A_KV), blk3),
            pl.BlockSpec((SB, wb, A_KV), blk3),
            pl.BlockSpec((SB, A_HEADS, A_KV), blk3),
            pl.BlockSpec((SB, 1, A_KV), blk3),
            pl.BlockSpec((SB, 1, A_KV), blk3),
            _full((A_HEADS, 2 * WINDOW)),
            _full((A_HEADS, 1)),
        ],
        out_specs=[
            pl.BlockSpec((SB, wb, A_KV), blk3),
            pl.BlockSpec((SB, wb, A_KV), blk3),
            pl.BlockSpec((SB, A_HEADS, A_KV), blk3),
        ],
        out_shape=[
            jax.ShapeDtypeStruct((n, wb, A_KV), F32),
            jax.ShapeDtypeStruct((n, wb, A_KV), F32),
            jax.ShapeDtypeStruct((n, A_HEADS, A_KV), F32),
        ],
        compiler_params=pltpu.CompilerParams(dimension_semantics=("arbitrary",)),
        name="ab_sample_attn",
    )(ck, cv, qx, kn, vn, sb, sink)


def _residual_proj_kernel(x_ref, m_ref, w_ref, y_ref):
    y_ref[...] = x_ref[...] + _dot(m_ref[...].astype(BF16), w_ref[...])


def _residual_proj(x, mix, w):
    return pl.pallas_call(
        _residual_proj_kernel,
        out_shape=jax.ShapeDtypeStruct(x.shape, F32),
        name="residual_proj",
    )(x, mix, w)


def _hgrn_sample_proj_kernel(x_ref, nm_ref, win_ref, clb_ref, q_ref, f_ref, i_ref, sg_ref):
    h = _rms(x_ref[...], nm_ref[...]).astype(BF16)
    z = _dot(h, win_ref[...])
    lb = _lower_bound(clb_ref[...])
    q_ref[...] = z[:, 0:C_F]
    f_ref[...] = lb + (1.0 - lb) * jax.nn.sigmoid(z[:, C_F:2 * C_F])
    i_ref[...] = z[:, 2 * C_F:2 * C_F + C_V]
    sg_ref[...] = jax.nn.sigmoid(z[:, 2 * C_F + C_V:C_IN])


def _hgrn_sample_proj(x, nm, w_in, clb):
    n = x.shape[0]
    return pl.pallas_call(
        _hgrn_sample_proj_kernel,
        out_shape=[jax.ShapeDtypeStruct((n, C_F), F32)] * 2 + [jax.ShapeDtypeStruct((n, C_V), F32)] * 2,
        compiler_params=pltpu.CompilerParams(vmem_limit_bytes=VMEM_LIMIT),
        name="hgrn_sample_proj",
    )(x, nm, w_in, clb)


def _hgrn_sample_state_kernel(s_ref, fc_ref, q_ref, i_ref, sn_ref, o_ref):
    out_rows = []
    for s in range(SB):
        parts = []
        for hd in range(C_HEADS):
            hs = slice(hd * C_VAL_DIM, (hd + 1) * C_VAL_DIM)
            fb = jnp.broadcast_to(fc_ref[0, hd, :, s:s + 1], (C_KEY_DIM, C_VAL_DIM))
            sn = fb * s_ref[s, hd] + (1.0 - fb) * i_ref[s:s + 1, hs]
            sn_ref[s, hd] = sn
            parts.append(_dot(q_ref[s:s + 1, hs].astype(BF16), sn.astype(BF16)))
        out_rows.append(jnp.concatenate(parts, axis=-1))
    o_ref[...] = jnp.concatenate(out_rows, axis=0)


def _hgrn_sample_state(state, fc, q, iv):
    n = state.shape[0]
    sblk = pl.BlockSpec((SB, C_HEADS, C_KEY_DIM, C_VAL_DIM), lambda i: (i, 0, 0, 0))
    cblk = pl.BlockSpec((1, C_HEADS, C_KEY_DIM, SB), lambda i: (i, 0, 0, 0))
    rblk = pl.BlockSpec((SB, C_V), lambda i: (i, 0))
    return pl.pallas_call(
        _hgrn_sample_state_kernel,
        grid=(n // SB,),
        in_specs=[sblk, cblk, rblk, rblk],
        out_specs=[sblk, pl.BlockSpec((SB, C_V), lambda i: (i, 0))],
        out_shape=[
            jax.ShapeDtypeStruct(state.shape, F32),
            jax.ShapeDtypeStruct((n, C_V), F32),
        ],
        compiler_params=pltpu.CompilerParams(
            dimension_semantics=("arbitrary",), vmem_limit_bytes=VMEM_LIMIT),
        name="hgrn_sample_state",
    )(state, fc, q, iv)


def _hgrn_sample_out_kernel(o_ref, sg_ref, on_ref, w_ref, x_ref, y_ref):
    parts = []
    for hd in range(C_HEADS):
        parts.append(_rms(o_ref[:, hd * C_VAL_DIM:(hd + 1) * C_VAL_DIM], on_ref[...]))
    on = jnp.concatenate(parts, axis=-1) * sg_ref[...]
    y_ref[...] = x_ref[...] + _dot(on.astype(BF16), w_ref[...])


def _hgrn_sample_out(o, sg, on, w_out, x):
    return pl.pallas_call(
        _hgrn_sample_out_kernel,
        out_shape=jax.ShapeDtypeStruct(x.shape, F32),
        name="hgrn_sample_out",
    )(o, sg, on, w_out, x)


def _to_columns(a):
    n = a.shape[0]
    return a.reshape(n // SB, SB, C_HEADS, C_KEY_DIM).transpose(0, 2, 3, 1)


def kernel(x_prompt, x_sample, cache_k, cache_v, state_hgrn, norm_mix, norm_ffn, w_in_ab, w_out_ab,
           q_norm, k_norm, attn_sink, rel_bias, gmlp_ln_g, gmlp_ln_b, gmlp_w_s, gmlp_b_s,
           w_in_c, c_lower_bounds, c_out_norm, w_out_c, w_gate, w_up, w_down):
    assert norm_mix.shape[0] == DEPTH == 2 and w_in_ab.shape[0] == 1 and w_in_c.shape[0] == 1
    nb, seq, _ = x_prompt.shape
    ns = x_sample.shape[0]
    assert x_sample.shape[1] == 1 and cache_k.shape[2] == WINDOW

    row = lambda v: v.reshape(1, -1)
    bf = lambda w: w.astype(BF16)
    w_in_ab0, w_out_ab0 = bf(w_in_ab[0]), bf(w_out_ab[0])
    w_in_c0, w_out_c0 = bf(w_in_c[0]), bf(w_out_c[0])
    wg, wu, wd = bf(w_gate), bf(w_up), bf(w_down)
    nm, nf = norm_mix, norm_ffn
    qn, kn = row(q_norm[0]), row(k_norm[0])
    lng, lnb = row(gmlp_ln_g[0]), row(gmlp_ln_b[0])
    sink = attn_sink[0]

    tab, tabp = _bias_table(rel_bias)

    xp, knew_p, vnew_p, gv_p = _ab_prompt(
        x_prompt, row(nm[0]), w_in_ab0, jnp.tile(qn, (1, A_HEADS)), jnp.tile(kn, (1, A_KV_HEADS)),
        sink, tabp, lng, lnb, gmlp_w_s[0], jnp.repeat(gmlp_b_s[0].T, B_GROUP_DIM, axis=1), w_out_ab0)
    xp = _ffn(xp.reshape(nb * seq, D_MODEL), row(nf[0]), wg[0], wu[0], wd[0]).reshape(nb, seq, D_MODEL)
    xp, st_p = _hgrn_prompt(xp, row(nm[1]), w_in_c0, c_lower_bounds, row(c_out_norm[0]), w_out_c0)
    xp = _ffn(xp.reshape(nb * seq, D_MODEL), row(nf[1]), wg[1], wu[1], wd[1]).reshape(nb, seq, D_MODEL)

    xs = x_sample.reshape(ns, D_MODEL)
    qx, knew_s, vnew_s, bm_s, gv_s = _ab_sample_proj(
        xs, row(nm[0]), w_in_ab0, qn, kn, lng, lnb, gmlp_w_s[0, :, 0, 0], gmlp_b_s[0, :, 0])
    sb = jnp.pad(tab[:, WINDOW - 1, WINDOW - 1:], ((0, 0), (0, WINDOW - 1)))
    nk_s, nv_s, om = _ab_sample_attn(
        cache_k[0].reshape(ns, WINDOW, A_KV), cache_v[0].reshape(ns, WINDOW, A_KV),
        qx.transpose(1, 0, 2), knew_s[:, None, :], vnew_s[:, None, :], sb, sink.reshape(A_HEADS, 1))
    om = om.reshape(ns, A_KV_HEADS, A_GROUP, A_KV_HEADS, A_HEAD_DIM)
    a_s = jnp.stack([om[:, g, :, g, :] for g in range(A_KV_HEADS)], axis=1).reshape(ns, A_Q)
    xs = _residual_proj(xs, jnp.concatenate([a_s, bm_s], axis=-1), w_out_ab0)
    xs = _ffn(xs, row(nf[0]), wg[0], wu[0], wd[0])

    q_s, f_s, i_s, sg_s = _hgrn_sample_proj(xs, row(nm[1]), w_in_c0, c_lower_bounds)
    st_s, o_s = _hgrn_sample_state(state_hgrn[0], _to_columns(f_s), q_s, i_s)
    xs = _hgrn_sample_out(o_s, sg_s, row(c_out_norm[0]), w_out_c0, xs)
    xs = _ffn(xs, row(nf[1]), wg[1], wu[1], wd[1])

    kv5 = lambda a: a.reshape(1, a.shape[0], WINDOW, A_KV_HEADS, A_HEAD_DIM)
    return (xp, xs.reshape(ns, 1, D_MODEL),
            kv5(knew_p), kv5(vnew_p), kv5(nk_s), kv5(nv_s),
            gv_p[None], gv_s.reshape(1, ns, 1, B_WIDTH),
            st_p[None], st_s[None])
```

```python
import math

import jax
import jax.numpy as jnp
import numpy as np
from jax import lax
from jax.experimental import pallas as pl
from jax.experimental.pallas import tpu as pltpu

F32 = jnp.float32
BF16 = jnp.bfloat16

D_MODEL = 1024
DEPTH = 2
A_HEADS = 8
A_KV_HEADS = 2
A_GROUP = A_HEADS // A_KV_HEADS
A_HEAD_DIM = 64
WINDOW = 128
ATTN_SCALE = A_HEAD_DIM ** -0.5
NUM_BUCKETS = 32
MAX_DISTANCE = 128
A_Q = A_HEADS * A_HEAD_DIM
A_KV = A_KV_HEADS * A_HEAD_DIM
B_GROUPS = 8
B_GROUP_DIM = 64
B_WIDTH = B_GROUPS * B_GROUP_DIM
B_CHUNK = 128
AB_IN = A_Q + 2 * A_KV + 2 * B_WIDTH
AB_MIX = A_Q + B_WIDTH
C_HEADS = 8
C_KEY_DIM = 128
C_VAL_DIM = 128
C_F = C_HEADS * C_KEY_DIM
C_V = C_HEADS * C_VAL_DIM
C_IN = 2 * C_F + 2 * C_V
D_FF = 2816
EPS = 1e-6

NEG = -1e30

VMEM_LIMIT = 56 * 1024 * 1024
VREG_ROWS = 8

CHUNK = 128
TQ = 1024
TM = 512
SB = 8
HEAD_SKEW = 2
SAFE_LOG2_RANGE = 64.0
PREFIX_GROUP = 4
FACTORED_UNROLL = 4
AB_UNROLL = 2

_NT = (((1,), (1,)), ((), ()))
_TN = (((0,), (0,)), ((), ()))


def _rms(x, g):
    return x * lax.rsqrt(jnp.mean(x * x, axis=-1, keepdims=True) + EPS) * g


def _gelu(x):
    return 0.5 * x * (1.0 + lax.erf(x * math.sqrt(0.5)))


def _layernorm(x, g, b):
    xc = x - jnp.mean(x, axis=-1, keepdims=True)
    return xc * lax.rsqrt(jnp.mean(xc * xc, axis=-1, keepdims=True) + EPS) * g + b


def _dot(a, b):
    return jnp.dot(a, b, preferred_element_type=F32)


def _full(shape):
    n = len(shape)
    return pl.BlockSpec(shape, lambda *_: (0,) * n)


def _resident(shape):
    n = len(shape)
    return pl.BlockSpec(shape, lambda *_: (0,) * n, pipeline_mode=pl.Buffered(1))


_SMEM = pl.BlockSpec(memory_space=pltpu.SMEM)


def _bias_table_kernel(rel_ref, tab_ref, tabp_ref):
    qi = lax.broadcasted_iota(jnp.int32, (WINDOW, 2 * WINDOW), 0)
    kj = lax.broadcasted_iota(jnp.int32, (WINDOW, 2 * WINDOW), 1)
    dist = qi + WINDOW - kj
    ok = (dist >= 0) & (dist < WINDOW)
    max_exact = NUM_BUCKETS // 2
    d = jnp.maximum(dist, 0)
    dl = jnp.maximum(d, 1).astype(F32)
    large = max_exact + (jnp.log(dl / max_exact) / math.log(MAX_DISTANCE / max_exact)
                         * (NUM_BUCKETS - max_exact)).astype(jnp.int32)
    large = jnp.minimum(large, NUM_BUCKETS - 1)
    bucket = jnp.where(d < max_exact, d, large)
    for h in range(A_HEADS):
        acc = jnp.zeros((WINDOW, 2 * WINDOW), F32)
        for b in range(NUM_BUCKETS):
            acc = jnp.where(bucket == b, rel_ref[b, h], acc)
        t = jnp.where(ok, acc, NEG)
        tab_ref[h] = t
        cols = slice((h % 2) * 2 * WINDOW, (h % 2 + 1) * 2 * WINDOW)
        tabp_ref[0, h // 2, :, cols] = t
        tabp_ref[1, h // 2, :, cols] = jnp.where(kj < WINDOW, NEG, t)


def _bias_table(rel_bias):
    return pl.pallas_call(
        _bias_table_kernel,
        out_shape=[
            jax.ShapeDtypeStruct((A_HEADS, WINDOW, 2 * WINDOW), F32),
            jax.ShapeDtypeStruct((2, A_HEADS // 2, WINDOW, 4 * WINDOW), F32),
        ],
        in_specs=[_SMEM],
        name="bias_table",
    )(rel_bias)


PAIR = 2 * A_HEAD_DIM
N_PAIRS = A_HEADS // 2


def _ab_prompt_kernel(sink_ref, x_ref, nm_ref, win_ref, qg_ref, kg_ref, tabp_ref, lng_ref, lnb_ref,
                      ws_ref, bsp_ref, wout_ref,
                      y_ref, knew_ref, vnew_ref, gv_ref,
                      z_ref, mix_ref, kp_ref, kpr_ref, vp_ref, vpr_ref, wpair_ref, kl_ref, vl_ref, gl_ref):
    j = pl.program_id(1)
    last_j = pl.num_programs(1) - 1
    n_chunks = TQ // CHUNK

    @pl.when(j == 0)
    def _():
        for ref in (kp_ref, kpr_ref, vp_ref, vpr_ref):
            ref[...] = jnp.zeros_like(ref)
        row = lax.broadcasted_iota(jnp.int32, (B_CHUNK, B_CHUNK), 0)
        col = lax.broadcasted_iota(jnp.int32, (B_CHUNK, B_CHUNK), 1)
        for g in range(B_GROUPS):
            wpair_ref[g // 2, :, (g % 2) * B_CHUNK:(g % 2 + 1) * B_CHUNK] = jnp.where(
                row >= col, ws_ref[g], 0.0).astype(BF16)

    h = _rms(x_ref[0], nm_ref[...]).astype(BF16)
    z_ref[...] = _dot(h, win_ref[...])

    lo_half = lax.broadcasted_iota(jnp.int32, (1, PAIR), 1) < A_HEAD_DIM
    r_i = lax.broadcasted_iota(jnp.int32, (PAIR, PAIR), 0) // A_HEAD_DIM
    c_i = lax.broadcasted_iota(jnp.int32, (PAIR, PAIR), 1) // A_HEAD_DIM
    half_mean = jnp.where(r_i == c_i, 1.0 / A_HEAD_DIM, 0.0).astype(BF16)

    def mean_sq_halves(x):
        x2 = x * x
        hi = x2.astype(BF16)
        lo = (x2 - hi.astype(F32)).astype(BF16)
        return _dot(hi, half_mean) + _dot(lo, half_mean)

    def block_diag(top, bot):
        zero = jnp.zeros_like(top)
        return jnp.concatenate([jnp.where(lo_half, top, zero), jnp.where(lo_half, zero, bot)], axis=0)

    def chunk(c, carry):
        r0 = pl.multiple_of(c * CHUNK, CHUNK)
        rows = pl.ds(r0, CHUNK)
        first = jnp.where(jnp.logical_and(j == 0, c == 0), 1, 0)

        kraw = z_ref[rows, A_Q:A_Q + A_KV]
        v = z_ref[rows, A_Q + A_KV:A_Q + 2 * A_KV]
        kn = kraw * lax.rsqrt(mean_sq_halves(kraw) + EPS) * kg_ref[...]
        kb, kbr = kn.astype(BF16), pltpu.roll(kn, A_HEAD_DIM, 1).astype(BF16)
        vb, vbr = v.astype(BF16), pltpu.roll(v, A_HEAD_DIM, 1).astype(BF16)
        k2 = jnp.concatenate([kp_ref[...], kb], axis=0)
        k2r = jnp.concatenate([kpr_ref[...], kbr], axis=0)
        v2 = jnp.concatenate([vp_ref[...], vb], axis=0)
        v2r = jnp.concatenate([vpr_ref[...], vbr], axis=0)
        kbd = [block_diag(k2, k2r), block_diag(k2r, k2)]
        vbd = [block_diag(v2, v2r), block_diag(v2r, v2)]

        scores = []
        for i in range(N_PAIRS):
            ps = slice(i * PAIR, (i + 1) * PAIR)
            qraw = z_ref[rows, i * PAIR:(i + 1) * PAIR]
            qn = qraw * lax.rsqrt(mean_sq_halves(qraw) + EPS) * (qg_ref[:, ps] * ATTN_SCALE)
            s = lax.dot_general(qn.astype(BF16), kbd[i // (A_GROUP // 2)], _NT,
                                preferred_element_type=F32)
            scores.append(s + tabp_ref[first, i])
        outs = []
        for i in range(N_PAIRS):
            es, rs = [], []
            for hh in range(2):
                sh = scores[i][:, hh * 2 * WINDOW:(hh + 1) * 2 * WINDOW]
                sk = sink_ref[2 * i + hh]
                m = jnp.maximum(jnp.max(sh, axis=-1, keepdims=True), sk)
                e = jnp.exp(sh - m)
                rs.append(1.0 / (jnp.sum(e, axis=-1, keepdims=True) + jnp.exp(sk - m)))
                es.append(e.astype(BF16))
            o = _dot(jnp.concatenate(es, axis=-1), vbd[i // (A_GROUP // 2)])
            outs.append(o * jnp.where(lo_half, rs[0], rs[1]))
        mix_ref[rows, 0:A_Q] = jnp.concatenate(outs, axis=-1).astype(BF16)

        zu = z_ref[rows, A_Q + 2 * A_KV:A_Q + 2 * A_KV + B_WIDTH]
        zv = z_ref[rows, A_Q + 2 * A_KV + B_WIDTH:AB_IN]
        u = _gelu(zu)
        vln = _layernorm(_gelu(zv), lng_ref[...], lnb_ref[...])
        vlb = vln.astype(BF16)
        sparts = []
        for i in range(B_GROUPS // 2):
            vpair = vlb[:, i * PAIR:(i + 1) * PAIR]
            sparts.append(_dot(wpair_ref[i], block_diag(vpair, vpair)))
        bm = u * (jnp.concatenate(sparts, axis=-1) + bsp_ref[...])
        mix_ref[rows, A_Q:AB_MIX] = bm.astype(BF16)

        kp_ref[...] = kb
        kpr_ref[...] = kbr
        vp_ref[...] = vb
        vpr_ref[...] = vbr

        kl_ref[...] = kn
        vl_ref[...] = v
        gl_ref[...] = vln
        return carry

    lax.fori_loop(0, n_chunks, chunk, 0, unroll=AB_UNROLL)
    y_ref[0] = x_ref[0] + _dot(mix_ref[...], wout_ref[...])

    @pl.when(j == last_j)
    def _():
        knew_ref[0] = kl_ref[...]
        vnew_ref[0] = vl_ref[...]
        gv_ref[0] = gl_ref[...]


def _ab_prompt(x, nm, w_in, qg, kg, sink, tabp, lng, lnb, w_s, bsp, w_out):
    nb, seq, _ = x.shape
    grid = (nb, seq // TQ)
    blk = lambda b, j: (b, j, 0)
    per_b = lambda b, j: (b, 0, 0)
    return pl.pallas_call(
        _ab_prompt_kernel,
        grid=grid,
        in_specs=[
            _SMEM,
            pl.BlockSpec((1, TQ, D_MODEL), blk),
            _full((1, D_MODEL)),
            _resident((D_MODEL, AB_IN)),
            _full((1, A_Q)),
            _full((1, A_KV)),
            _resident((2, N_PAIRS, WINDOW, 4 * WINDOW)),
            _full((1, B_WIDTH)),
            _full((1, B_WIDTH)),
            _resident((B_GROUPS, B_CHUNK, B_CHUNK)),
            _resident((B_CHUNK, B_WIDTH)),
            _resident((AB_MIX, D_MODEL)),
        ],
        out_specs=[
            pl.BlockSpec((1, TQ, D_MODEL), blk),
            pl.BlockSpec((1, WINDOW, A_KV), per_b),
            pl.BlockSpec((1, WINDOW, A_KV), per_b),
            pl.BlockSpec((1, B_CHUNK, B_WIDTH), per_b),
        ],
        out_shape=[
            jax.ShapeDtypeStruct((nb, seq, D_MODEL), F32),
            jax.ShapeDtypeStruct((nb, WINDOW, A_KV), F32),
            jax.ShapeDtypeStruct((nb, WINDOW, A_KV), F32),
            jax.ShapeDtypeStruct((nb, B_CHUNK, B_WIDTH), F32),
        ],
        scratch_shapes=[
            pltpu.VMEM((TQ, AB_IN), F32),
            pltpu.VMEM((TQ, AB_MIX), BF16),
            pltpu.VMEM((WINDOW, A_KV), BF16),
            pltpu.VMEM((WINDOW, A_KV), BF16),
            pltpu.VMEM((WINDOW, A_KV), BF16),
            pltpu.VMEM((WINDOW, A_KV), BF16),
            pltpu.VMEM((B_GROUPS // 2, B_CHUNK, 2 * B_CHUNK), BF16),
            pltpu.VMEM((WINDOW, A_KV), F32),
            pltpu.VMEM((WINDOW, A_KV), F32),
            pltpu.VMEM((B_CHUNK, B_WIDTH), F32),
        ],
        compiler_params=pltpu.CompilerParams(
            dimension_semantics=("arbitrary", "arbitrary"), vmem_limit_bytes=VMEM_LIMIT),
        name="ab_prompt",
    )(sink, x, nm, w_in, qg, kg, tabp, lng, lnb, w_s, bsp, w_out)


def _ffn_kernel(x_ref, g_ref, wg_ref, wu_ref, wd_ref, o_ref):
    x = x_ref[...]
    h = _rms(x, g_ref[...]).astype(BF16)
    gate = _dot(h, wg_ref[...])
    up = _dot(h, wu_ref[...])
    a = (gate * jax.nn.sigmoid(gate) * up).astype(BF16)
    o_ref[...] = x + _dot(a, wd_ref[...])


def _ffn(x, g, wg, wu, wd):
    rows = x.shape[0]
    tm = min(TM, rows)
    blk = lambda i: (i, 0)
    return pl.pallas_call(
        _ffn_kernel,
        grid=(rows // tm,),
        in_specs=[
            pl.BlockSpec((tm, D_MODEL), blk),
            _full((1, D_MODEL)),
            _resident((D_MODEL, D_FF)),
            _resident((D_MODEL, D_FF)),
            _resident((D_FF, D_MODEL)),
        ],
        out_specs=pl.BlockSpec((tm, D_MODEL), blk),
        out_shape=jax.ShapeDtypeStruct((rows, D_MODEL), F32),
        compiler_params=pltpu.CompilerParams(
            dimension_semantics=("arbitrary",), vmem_limit_bytes=VMEM_LIMIT),
        name="ffn",
    )(x, g, wg, wu, wd)


def _lower_bound(clb):
    m = jnp.max(clb, axis=0, keepdims=True)
    e = jnp.exp(clb - m)
    sm = e / jnp.sum(e, axis=0, keepdims=True)
    return (sm[0:1] + sm[1:2]) - sm[0:1]


def _split3(x):
    hi = x.astype(BF16)
    r = x - hi.astype(F32)
    mid = r.astype(BF16)
    lo = (r - mid.astype(F32)).astype(BF16)
    return hi, mid, lo


def _neg_abs(x):
    return lax.bitcast_convert_type(
        lax.bitcast_convert_type(x, jnp.uint32) | jnp.uint32(0x80000000), F32)


def _pair_level_table():
    t = np.arange(CHUNK)[:, None]
    s = np.arange(CHUNK)[None, :]
    lev = np.floor(np.log2(np.maximum(t ^ s, 1))).astype(np.int32)
    lev = np.where(t == s, -1, lev)
    return np.where(s > t, -2, lev).astype(np.int32)


def _level_operand(p, q, kk, f, b2):
    m = 2 ** p
    if m < VREG_ROWS:
        shape3 = (CHUNK // VREG_ROWS, VREG_ROWS, q.shape[1])
        sub = lax.broadcasted_iota(jnp.int32, (1, VREG_ROWS, q.shape[1]), 1)
        upper = ((sub >> p) & 1) == 1
        q3, k3 = q.reshape(shape3), kk.reshape(shape3)
        if p == 0:
            y = jnp.where(upper, q3 * f.reshape(shape3), k3)
        else:
            b3 = b2.reshape(shape3)
            be = b3[:, m - 1:m, :]
            for k in range(1, VREG_ROWS // (2 * m)):
                be = jnp.where(sub >= 2 * m * k, b3[:, 2 * m * k + m - 1:2 * m * k + m, :], be)
            y = jnp.where(upper, q3, k3) * jnp.exp2(_neg_abs(b3 - be))
        return y.reshape(q.shape).astype(BF16)
    parts = []
    for k in range(CHUNK // (2 * m)):
        lo = slice(2 * m * k, 2 * m * k + m)
        up = slice(2 * m * k + m, 2 * m * (k + 1))
        be = b2[2 * m * k + m - 1:2 * m * k + m, :]
        parts.append(kk[lo] * jnp.exp2(be - b2[lo]))
        parts.append(q[up] * jnp.exp2(b2[up] - be))
    return jnp.concatenate(parts, axis=0).astype(BF16)


def _merge_level(p, att, pm, lev):
    m = 2 ** p
    if m < VREG_ROWS:
        return jnp.where(lev == p, pm, att)
    col = lax.broadcasted_iota(jnp.int32, (1, CHUNK), 1)
    parts = []
    for k in range(CHUNK // (2 * m)):
        lo = slice(2 * m * k, 2 * m * k + m)
        up = slice(2 * m * k + m, 2 * m * (k + 1))
        parts.append(att[lo])
        parts.append(jnp.where((col >= 2 * m * k) & (col < 2 * m * k + m), pm[up], att[up]))
    return jnp.concatenate(parts, axis=0)


def _hgrn_prompt_kernel(x_ref, nm_ref, win_ref, clb_ref, on_ref, wout_ref, lev_ref,
                        y_ref, st_ref,
                        z_ref, o_ref, stt_ref, k_ref):
    j = pl.program_id(1)
    last_j = pl.num_programs(1) - 1
    n_chunks = TQ // CHUNK
    n_levels = int(math.log2(CHUNK))

    @pl.when(j == 0)
    def _():
        stt_ref[...] = jnp.zeros_like(stt_ref)

    h = _rms(x_ref[0], nm_ref[...]).astype(BF16)
    z_ref[...] = _dot(h, win_ref[...])
    lb = _lower_bound(clb_ref[...])

    row = lax.broadcasted_iota(jnp.int32, (CHUNK, CHUNK), 0)
    col = lax.broadcasted_iota(jnp.int32, (CHUNK, CHUNK), 1)
    ltri = (row >= col).astype(BF16)

    def chunk_rows(c):
        return pl.ds(pl.multiple_of(c * CHUNK, CHUNK), CHUNK)

    def prefix(g, worst):
        rows = [chunk_rows(g * PREFIX_GROUP + i) for i in range(PREFIX_GROUP)]
        gates = [z_ref[r, C_F:2 * C_F] for r in rows]
        for r, gate in zip(rows, gates):
            f_all = lb + (1.0 - lb) * jax.nn.sigmoid(gate)
            k_ref[r, :] = 1.0 - f_all
            hi, mid, lo = _split3(jnp.log2(f_all))
            b2 = (_dot(ltri, hi) + _dot(ltri, mid)) + _dot(ltri, lo)
            z_ref[r, C_F:2 * C_F] = b2
            b_mid = b2[CHUNK // 2 - 1:CHUNK // 2, :]
            b_last = b2[CHUNK - 1:CHUNK, :]
            worst = jnp.maximum(worst, jnp.maximum(-b_mid, b_mid - b_last))
        return worst

    worst = lax.fori_loop(0, n_chunks // PREFIX_GROUP, prefix, jnp.zeros((1, C_F), F32))
    bounded = jnp.max(worst) <= SAFE_LOG2_RANGE

    def finish_head(rows, hd, o):
        gt = z_ref[rows, 2 * C_F + C_V + hd * C_VAL_DIM:2 * C_F + C_V + (hd + 1) * C_VAL_DIM]
        o = _rms(o, on_ref[...]) * jax.nn.sigmoid(gt)
        o_ref[rows, hd * C_VAL_DIM:(hd + 1) * C_VAL_DIM] = o.astype(BF16)

    def head_inputs(rows, hd):
        q = z_ref[rows, hd * C_KEY_DIM:(hd + 1) * C_KEY_DIM]
        kk = k_ref[rows, hd * C_KEY_DIM:(hd + 1) * C_KEY_DIM]
        b2 = z_ref[rows, C_F + hd * C_KEY_DIM:C_F + (hd + 1) * C_KEY_DIM]
        ivb = z_ref[rows, 2 * C_F + hd * C_VAL_DIM:2 * C_F + (hd + 1) * C_VAL_DIM].astype(BF16)
        return q, kk, b2, ivb

    def factored_chunk(c, carry):
        rows = chunk_rows(c)
        for hd in range(C_HEADS):
            q, kk, b2, ivb = head_inputs(rows, hd)
            b_mid = b2[CHUNK // 2 - 1:CHUNK // 2, :]
            b_last = b2[CHUNK - 1:CHUNK, :]
            qs = (q * jnp.exp2(b2 - b_mid)).astype(BF16)
            kd = (kk * jnp.exp2(b_mid - b2)).astype(BF16)
            att = jnp.where(row >= col, lax.dot_general(qs, kd, _NT, preferred_element_type=F32), 0.0)
            stt = stt_ref[hd]
            o = lax.dot_general(qs, (stt * jnp.exp2(b_mid)).astype(BF16), _NT,
                                preferred_element_type=F32) + _dot(att.astype(BF16), ivb)
            stt_ref[hd] = stt * jnp.exp2(b_last) + jnp.exp2(b_last - b_mid) * lax.dot_general(
                ivb, kd, _TN, preferred_element_type=F32)
            finish_head(rows, hd, o)
        return carry

    def tree_chunk(c, carry):
        rows = chunk_rows(c)
        lev = lev_ref[...]

        def products(hd):
            q, kk, b2, ivb = head_inputs(rows, hd)
            diag = jnp.sum(q * kk, axis=-1, keepdims=True)
            pms = []
            for p in range(n_levels):
                y = _level_operand(p, q, kk, 1.0 - kk, b2)
                pms.append(lax.dot_general(y, y, _NT, preferred_element_type=F32))
            stt = stt_ref[hd]
            o_prev = lax.dot_general((q * jnp.exp2(b2)).astype(BF16), stt.astype(BF16), _NT,
                                     preferred_element_type=F32)
            b_last = b2[CHUNK - 1:CHUNK, :]
            kd = (kk * jnp.exp2(b_last - b2)).astype(BF16)
            stt_ref[hd] = stt * jnp.exp2(b_last) + lax.dot_general(
                ivb, kd, _TN, preferred_element_type=F32)
            return diag, pms, o_prev, ivb

        def finish(hd, diag, pms, o_prev, ivb):
            att = jnp.where(lev == -1, diag, 0.0)
            for p in range(n_levels):
                att = _merge_level(p, att, pms[p], lev)
            finish_head(rows, hd, o_prev + _dot(att.astype(BF16), ivb))

        pending = [products(hd) for hd in range(HEAD_SKEW)]
        for hd in range(C_HEADS):
            if hd + HEAD_SKEW < C_HEADS:
                pending.append(products(hd + HEAD_SKEW))
            finish(hd, *pending.pop(0))
        return carry

    @pl.when(bounded)
    def _():
        lax.fori_loop(0, n_chunks, factored_chunk, 0, unroll=FACTORED_UNROLL)

    @pl.when(jnp.logical_not(bounded))
    def _():
        lax.fori_loop(0, n_chunks, tree_chunk, 0)

    y_ref[0] = x_ref[0] + _dot(o_ref[...], wout_ref[...])

    @pl.when(j == last_j)
    def _():
        for hd in range(C_HEADS):
            st_ref[0, hd] = stt_ref[hd].T


def _hgrn_prompt(x, nm, w_in, clb, on, w_out):
    nb, seq, _ = x.shape
    grid = (nb, seq // TQ)
    blk = lambda b, j: (b, j, 0)
    return pl.pallas_call(
        _hgrn_prompt_kernel,
        grid=grid,
        in_specs=[
            pl.BlockSpec((1, TQ, D_MODEL), blk),
            _full((1, D_MODEL)),
            _resident((D_MODEL, C_IN)),
            _full((DEPTH, C_F)),
            _full((1, C_VAL_DIM)),
            _resident((C_V, D_MODEL)),
            _full((CHUNK, CHUNK)),
        ],
        out_specs=[
            pl.BlockSpec((1, TQ, D_MODEL), blk),
            pl.BlockSpec((1, C_HEADS, C_KEY_DIM, C_VAL_DIM), lambda b, j: (b, 0, 0, 0)),
        ],
        out_shape=[
            jax.ShapeDtypeStruct((nb, seq, D_MODEL), F32),
            jax.ShapeDtypeStruct((nb, C_HEADS, C_KEY_DIM, C_VAL_DIM), F32),
        ],
        scratch_shapes=[
            pltpu.VMEM((TQ, C_IN), F32),
            pltpu.VMEM((TQ, C_V), BF16),
            pltpu.VMEM((C_HEADS, C_VAL_DIM, C_KEY_DIM), F32),
            pltpu.VMEM((TQ, C_F), F32),
        ],
        compiler_params=pltpu.CompilerParams(
            dimension_semantics=("arbitrary", "arbitrary"), vmem_limit_bytes=VMEM_LIMIT),
        name="hgrn_prompt",
    )(x, nm, w_in, clb, on, w_out, jnp.asarray(_pair_level_table()))


def _ab_sample_proj_kernel(w00_ref, b0_ref, x_ref, nm_ref, win_ref, qn_ref, kn_ref, lng_ref, lnb_ref,
                           qx_ref, knew_ref, vnew_ref, bm_ref, gv_ref):
    n = x_ref.shape[0]
    h = _rms(x_ref[...], nm_ref[...]).astype(BF16)
    z = _dot(h, win_ref[...])
    zeros = jnp.zeros((n, A_HEAD_DIM), F32)
    for hh in range(A_HEADS):
        qh = _rms(z[:, hh * A_HEAD_DIM:(hh + 1) * A_HEAD_DIM], qn_ref[...]) * ATTN_SCALE
        qx_ref[hh] = jnp.concatenate([qh, zeros] if hh // A_GROUP == 0 else [zeros, qh], axis=-1)
    kparts = []
    for g in range(A_KV_HEADS):
        kparts.append(_rms(z[:, A_Q + g * A_HEAD_DIM:A_Q + (g + 1) * A_HEAD_DIM], kn_ref[...]))
    knew_ref[...] = jnp.concatenate(kparts, axis=-1)
    vnew_ref[...] = z[:, A_Q + A_KV:A_Q + 2 * A_KV]

    u = _gelu(z[:, A_Q + 2 * A_KV:A_Q + 2 * A_KV + B_WIDTH])
    vln = _layernorm(_gelu(z[:, A_Q + 2 * A_KV + B_WIDTH:AB_IN]), lng_ref[...], lnb_ref[...])
    grp = lax.broadcasted_iota(jnp.int32, (1, B_WIDTH), 1) // B_GROUP_DIM
    srow = jnp.zeros((1, B_WIDTH), F32)
    brow = jnp.zeros((1, B_WIDTH), F32)
    for g in range(B_GROUPS):
        srow = jnp.where(grp == g, w00_ref[g], srow)
        brow = jnp.where(grp == g, b0_ref[g], brow)
    bm_ref[...] = u * (vln * srow + brow)
    gv_ref[...] = vln


def _ab_sample_proj(x, nm, w_in, qn, kn, lng, lnb, w00, b0):
    n = x.shape[0]
    return pl.pallas_call(
        _ab_sample_proj_kernel,
        in_specs=[_SMEM, _SMEM] + [pl.BlockSpec(memory_space=pltpu.VMEM)] * 7,
        out_shape=[
            jax.ShapeDtypeStruct((A_HEADS, n, A_KV), F32),
            jax.ShapeDtypeStruct((n, A_KV), F32),
            jax.ShapeDtypeStruct((n, A_KV), F32),
            jax.ShapeDtypeStruct((n, B_WIDTH), F32),
            jax.ShapeDtypeStruct((n, B_WIDTH), F32),
        ],
        compiler_params=pltpu.CompilerParams(vmem_limit_bytes=VMEM_LIMIT),
        name="ab_sample_proj",
    )(w00, b0, x, nm, w_in, qn, kn, lng, lnb)


def _ab_sample_attn_kernel(ck_ref, cv_ref, qx_ref, kn_ref, vn_ref, sb_ref, sink_ref,
                           nk_ref, nv_ref, om_ref):
    wb = ck_ref.shape[1]
    head = lax.broadcasted_iota(jnp.int32, (1, A_HEADS, A_KV), 1)
    lane = lax.broadcasted_iota(jnp.int32, (1, A_HEADS, A_KV), 2)
    own_group = (head // A_GROUP) == (lane // A_HEAD_DIM)
    sink = sink_ref[...][None]
    kc, vc = ck_ref[...], cv_ref[...]
    kn, vn = kn_ref[...], vn_ref[...]
    nk_ref[:, 0:wb - 1, :] = kc[:, 1:wb, :]
    nk_ref[:, wb - 1:wb, :] = kn
    nv_ref[:, 0:wb - 1, :] = vc[:, 1:wb, :]
    nv_ref[:, wb - 1:wb, :] = vn
    q = qx_ref[...]
    s = jnp.einsum('bhd,bkd->bhk', q.astype(BF16), kc.astype(BF16), preferred_element_type=F32)
    s = s + sb_ref[:, 0:wb][None]
    sn = jnp.sum(q * kn, axis=-1, keepdims=True) + sb_ref[:, wb:wb + 1][None]
    m = jnp.maximum(jnp.maximum(jnp.max(s, axis=-1, keepdims=True), sn), sink)
    e = jnp.exp(s - m)
    en = jnp.exp(sn - m)
    r = 1.0 / (jnp.sum(e, axis=-1, keepdims=True) + en + jnp.exp(sink - m))
    o = jnp.einsum('bhk,bkd->bhd', (e * r).astype(BF16), vc.astype(BF16),
                   preferred_element_type=F32) + (en * r) * vn
    om_ref[...] = jnp.where(own_group, o, 0.0)


def _ab_sample_attn(ck, cv, qx, kn, vn, sb, sink):
    n, wb, _ = ck.shape
    blk3 = lambda i: (i, 0, 0)
    return pl.pallas_call(
        _ab_sample_attn_kernel,
        grid=(n // SB,),
        in_specs=[
            pl.BlockSpec((SB, wb, A_KV), blk3),
            pl.BlockSpec((SB, wb, A_KV), blk3),
            pl.BlockSpec((SB, A_HEADS, A_KV), blk3),
            pl.BlockSpec((SB, 1, A_KV), blk3),
            pl.BlockSpec((SB, 1, A_KV), blk3),
            _full((A_HEADS, 2 * WINDOW)),
            _full((A_HEADS, 1)),
        ],
        out_specs=[
            pl.BlockSpec((SB, wb, A_KV), blk3),
            pl.BlockSpec((SB, wb, A_KV), blk3),
            pl.BlockSpec((SB, A_HEADS, A_KV), blk3),
        ],
        out_shape=[
            jax.ShapeDtypeStruct((n, wb, A_KV), F32),
            jax.ShapeDtypeStruct((n, wb, A_KV), F32),
            jax.ShapeDtypeStruct((n, A_HEADS, A_KV), F32),
        ],
        compiler_params=pltpu.CompilerParams(dimension_semantics=("arbitrary",)),
        name="ab_sample_attn",
    )(ck, cv, qx, kn, vn, sb, sink)


def _residual_proj_kernel(x_ref, m_ref, w_ref, y_ref):
    y_ref[...] = x_ref[...] + _dot(m_ref[...].astype(BF16), w_ref[...])


def _residual_proj(x, mix, w):
    return pl.pallas_call(
        _residual_proj_kernel,
        out_shape=jax.ShapeDtypeStruct(x.shape, F32),
        name="residual_proj",
    )(x, mix, w)


def _hgrn_sample_proj_kernel(x_ref, nm_ref, win_ref, clb_ref, q_ref, f_ref, i_ref, sg_ref):
    h = _rms(x_ref[...], nm_ref[...]).astype(BF16)
    z = _dot(h, win_ref[...])
    lb = _lower_bound(clb_ref[...])
    q_ref[...] = z[:, 0:C_F]
    f_ref[...] = lb + (1.0 - lb) * jax.nn.sigmoid(z[:, C_F:2 * C_F])
    i_ref[...] = z[:, 2 * C_F:2 * C_F + C_V]
    sg_ref[...] = jax.nn.sigmoid(z[:, 2 * C_F + C_V:C_IN])


def _hgrn_sample_proj(x, nm, w_in, clb):
    n = x.shape[0]
    return pl.pallas_call(
        _hgrn_sample_proj_kernel,
        out_shape=[jax.ShapeDtypeStruct((n, C_F), F32)] * 2 + [jax.ShapeDtypeStruct((n, C_V), F32)] * 2,
        compiler_params=pltpu.CompilerParams(vmem_limit_bytes=VMEM_LIMIT),
        name="hgrn_sample_proj",
    )(x, nm, w_in, clb)


def _hgrn_sample_state_kernel(s_ref, fc_ref, q_ref, i_ref, sn_ref, o_ref):
    out_rows = []
    for s in range(SB):
        parts = []
        for hd in range(C_HEADS):
            hs = slice(hd * C_VAL_DIM, (hd + 1) * C_VAL_DIM)
            fb = jnp.broadcast_to(fc_ref[0, hd, :, s:s + 1], (C_KEY_DIM, C_VAL_DIM))
            sn = fb * s_ref[s, hd] + (1.0 - fb) * i_ref[s:s + 1, hs]
            sn_ref[s, hd] = sn
            parts.append(_dot(q_ref[s:s + 1, hs].astype(BF16), sn.astype(BF16)))
        out_rows.append(jnp.concatenate(parts, axis=-1))
    o_ref[...] = jnp.concatenate(out_rows, axis=0)


def _hgrn_sample_state(state, fc, q, iv):
    n = state.shape[0]
    sblk = pl.BlockSpec((SB, C_HEADS, C_KEY_DIM, C_VAL_DIM), lambda i: (i, 0, 0, 0))
    cblk = pl.BlockSpec((1, C_HEADS, C_KEY_DIM, SB), lambda i: (i, 0, 0, 0))
    rblk = pl.BlockSpec((SB, C_V), lambda i: (i, 0))
    return pl.pallas_call(
        _hgrn_sample_state_kernel,
        grid=(n // SB,),
        in_specs=[sblk, cblk, rblk, rblk],
        out_specs=[sblk, pl.BlockSpec((SB, C_V), lambda i: (i, 0))],
        out_shape=[
            jax.ShapeDtypeStruct(state.shape, F32),
            jax.ShapeDtypeStruct((n, C_V), F32),
        ],
        compiler_params=pltpu.CompilerParams(
            dimension_semantics=("arbitrary",), vmem_limit_bytes=VMEM_LIMIT),
        name="hgrn_sample_state",
    )(state, fc, q, iv)


def _hgrn_sample_out_kernel(o_ref, sg_ref, on_ref, w_ref, x_ref, y_ref):
    parts = []
    for hd in range(C_HEADS):
        parts.append(_rms(o_ref[:, hd * C_VAL_DIM:(hd + 1) * C_VAL_DIM], on_ref[...]))
    on = jnp.concatenate(parts, axis=-1) * sg_ref[...]
    y_ref[...] = x_ref[...] + _dot(on.astype(BF16), w_ref[...])


def _hgrn_sample_out(o, sg, on, w_out, x):
    return pl.pallas_call(
        _hgrn_sample_out_kernel,
        out_shape=jax.ShapeDtypeStruct(x.shape, F32),
        name="hgrn_sample_out",
    )(o, sg, on, w_out, x)


def _to_columns(a):
    n = a.shape[0]
    return a.reshape(n // SB, SB, C_HEADS, C_KEY_DIM).transpose(0, 2, 3, 1)


def kernel(x_prompt, x_sample, cache_k, cache_v, state_hgrn, norm_mix, norm_ffn, w_in_ab, w_out_ab,
           q_norm, k_norm, attn_sink, rel_bias, gmlp_ln_g, gmlp_ln_b, gmlp_w_s, gmlp_b_s,
           w_in_c, c_lower_bounds, c_out_norm, w_out_c, w_gate, w_up, w_down):
    assert norm_mix.shape[0] == DEPTH == 2 and w_in_ab.shape[0] == 1 and w_in_c.shape[0] == 1
    nb, seq, _ = x_prompt.shape
    ns = x_sample.shape[0]
    assert x_sample.shape[1] == 1 and cache_k.shape[2] == WINDOW

    row = lambda v: v.reshape(1, -1)
    bf = lambda w: w.astype(BF16)
    w_in_ab0, w_out_ab0 = bf(w_in_ab[0]), bf(w_out_ab[0])
    w_in_c0, w_out_c0 = bf(w_in_c[0]), bf(w_out_c[0])
    wg, wu, wd = bf(w_gate), bf(w_up), bf(w_down)
    nm, nf = norm_mix, norm_ffn
    qn, kn = row(q_norm[0]), row(k_norm[0])
    lng, lnb = row(gmlp_ln_g[0]), row(gmlp_ln_b[0])
    sink = attn_sink[0]

    tab, tabp = _bias_table(rel_bias)

    xp, knew_p, vnew_p, gv_p = _ab_prompt(
        x_prompt, row(nm[0]), w_in_ab0, jnp.tile(qn, (1, A_HEADS)), jnp.tile(kn, (1, A_KV_HEADS)),
        sink, tabp, lng, lnb, gmlp_w_s[0], jnp.repeat(gmlp_b_s[0].T, B_GROUP_DIM, axis=1), w_out_ab0)
    xp = _ffn(xp.reshape(nb * seq, D_MODEL), row(nf[0]), wg[0], wu[0], wd[0]).reshape(nb, seq, D_MODEL)
    xp, st_p = _hgrn_prompt(xp, row(nm[1]), w_in_c0, c_lower_bounds, row(c_out_norm[0]), w_out_c0)
    xp = _ffn(xp.reshape(nb * seq, D_MODEL), row(nf[1]), wg[1], wu[1], wd[1]).reshape(nb, seq, D_MODEL)

    xs = x_sample.reshape(ns, D_MODEL)
    qx, knew_s, vnew_s, bm_s, gv_s = _ab_sample_proj(
        xs, row(nm[0]), w_in_ab0, qn, kn, lng, lnb, gmlp_w_s[0, :, 0, 0], gmlp_b_s[0, :, 0])
    sb = jnp.pad(tab[:, WINDOW - 1, WINDOW - 1:], ((0, 0), (0, WINDOW - 1)))
    nk_s, nv_s, om = _ab_sample_attn(
        cache_k[0].reshape(ns, WINDOW, A_KV), cache_v[0].reshape(ns, WINDOW, A_KV),
        qx.transpose(1, 0, 2), knew_s[:, None, :], vnew_s[:, None, :], sb, sink.reshape(A_HEADS, 1))
    om = om.reshape(ns, A_KV_HEADS, A_GROUP, A_KV_HEADS, A_HEAD_DIM)
    a_s = jnp.stack([om[:, g, :, g, :] for g in range(A_KV_HEADS)], axis=1).reshape(ns, A_Q)
    xs = _residual_proj(xs, jnp.concatenate([a_s, bm_s], axis=-1), w_out_ab0)
    xs = _ffn(xs, row(nf[0]), wg[0], wu[0], wd[0])

    q_s, f_s, i_s, sg_s = _hgrn_sample_proj(xs, row(nm[1]), w_in_c0, c_lower_bounds)
    st_s, o_s = _hgrn_sample_state(state_hgrn[0], _to_columns(f_s), q_s, i_s)
    xs = _hgrn_sample_out(o_s, sg_s, row(c_out_norm[0]), w_out_c0, xs)
    xs = _ffn(xs, row(nf[1]), wg[1], wu[1], wd[1])

    kv5 = lambda a: a.reshape(1, a.shape[0], WINDOW, A_KV_HEADS, A_HEAD_DIM)
    return (xp, xs.reshape(ns, 1, D_MODEL),
            kv5(knew_p), kv5(vnew_p), kv5(nk_s), kv5(nv_s),
            gv_p[None], gv_s.reshape(1, ns, 1, B_WIDTH),
            st_p[None], st_s[None])
```

```python
import functools
import math

import jax
import jax.numpy as jnp
import numpy as np
from jax import lax
from jax.experimental import pallas as pl
from jax.experimental.pallas import tpu as pltpu

F32 = jnp.float32
BF16 = jnp.bfloat16

D_MODEL = 1024
DEPTH = 2
A_HEADS = 8
A_KV_HEADS = 2
A_GROUP = A_HEADS // A_KV_HEADS
A_HEAD_DIM = 64
WINDOW = 128
ATTN_SCALE = A_HEAD_DIM ** -0.5
NUM_BUCKETS = 32
MAX_DISTANCE = 128
A_Q = A_HEADS * A_HEAD_DIM
A_KV = A_KV_HEADS * A_HEAD_DIM
B_GROUPS = 8
B_GROUP_DIM = 64
B_WIDTH = B_GROUPS * B_GROUP_DIM
B_CHUNK = 128
AB_IN = A_Q + 2 * A_KV + 2 * B_WIDTH
AB_MIX = A_Q + B_WIDTH
C_HEADS = 8
C_KEY_DIM = 128
C_VAL_DIM = 128
C_F = C_HEADS * C_KEY_DIM
C_V = C_HEADS * C_VAL_DIM
C_IN = 2 * C_F + 2 * C_V
D_FF = 2816
EPS = 1e-6

NEG = -1e30

VMEM_LIMIT = 56 * 1024 * 1024
VREG_ROWS = 8

CHUNK = 128
TQ = 1024
TM = 512
SB = 8
HEAD_SKEW = 2
SAFE_LOG2_RANGE = 64.0
PREFIX_GROUP = 4
FACTORED_UNROLL = 4
AB_UNROLL = 2

_NT = (((1,), (1,)), ((), ()))
_TN = (((0,), (0,)), ((), ()))


def _rms(x, g):
    return x * lax.rsqrt(jnp.mean(x * x, axis=-1, keepdims=True) + EPS) * g


def _gelu(x):
    return 0.5 * x * (1.0 + lax.erf(x * math.sqrt(0.5)))


def _layernorm(x, g, b):
    xc = x - jnp.mean(x, axis=-1, keepdims=True)
    return xc * lax.rsqrt(jnp.mean(xc * xc, axis=-1, keepdims=True) + EPS) * g + b


def _dot(a, b):
    return jnp.dot(a, b, preferred_element_type=F32)


def _full(shape):
    n = len(shape)
    return pl.BlockSpec(shape, lambda *_: (0,) * n)


def _resident(shape):
    n = len(shape)
    return pl.BlockSpec(shape, lambda *_: (0,) * n, pipeline_mode=pl.Buffered(1))


_SMEM = pl.BlockSpec(memory_space=pltpu.SMEM)


def _bias_table_kernel(rel_ref, tab_ref, tabp_ref):
    qi = lax.broadcasted_iota(jnp.int32, (WINDOW, 2 * WINDOW), 0)
    kj = lax.broadcasted_iota(jnp.int32, (WINDOW, 2 * WINDOW), 1)
    dist = qi + WINDOW - kj
    ok = (dist >= 0) & (dist < WINDOW)
    max_exact = NUM_BUCKETS // 2
    d = jnp.maximum(dist, 0)
    dl = jnp.maximum(d, 1).astype(F32)
    v = (jnp.log(dl / max_exact) / math.log(MAX_DISTANCE / max_exact) * (NUM_BUCKETS - max_exact))
    far = d >= max_exact
    hits = []
    for b in range(NUM_BUCKETS):
        if b < max_exact:
            hits.append(d == b)
        elif b < NUM_BUCKETS - 1:
            hits.append(far & (v >= b - max_exact) & (v < b - max_exact + 1))
        else:
            hits.append(far & (v >= b - max_exact))
    for h in range(A_HEADS):
        acc = jnp.zeros((WINDOW, 2 * WINDOW), F32)
        for b in range(NUM_BUCKETS):
            acc = jnp.where(hits[b], rel_ref[b, h], acc)
        t = jnp.where(ok, acc, NEG)
        tab_ref[h] = t
        cols = slice((h % 2) * 2 * WINDOW, (h % 2 + 1) * 2 * WINDOW)
        tabp_ref[0, h // 2, :, cols] = t
        tabp_ref[1, h // 2, :, cols] = jnp.where(kj < WINDOW, NEG, t)


def _bias_table(rel_bias):
    return pl.pallas_call(
        _bias_table_kernel,
        out_shape=[
            jax.ShapeDtypeStruct((A_HEADS, WINDOW, 2 * WINDOW), F32),
            jax.ShapeDtypeStruct((2, A_HEADS // 2, WINDOW, 4 * WINDOW), F32),
        ],
        in_specs=[_SMEM],
        name="bias_table",
    )(rel_bias)


PAIR = 2 * A_HEAD_DIM
N_PAIRS = A_HEADS // 2


def _ab_prompt_kernel(sink_ref, x_ref, nm_ref, win_ref, qg_ref, kg_ref, tabp_ref, lng_ref, lnb_ref,
                      ws_ref, bsp_ref, wout_ref,
                      y_ref, knew_ref, vnew_ref, gv_ref,
                      z_ref, mix_ref, kp_ref, kpr_ref, vp_ref, vpr_ref, wpair_ref, kl_ref, vl_ref, gl_ref):
    j = pl.program_id(1)
    last_j = pl.num_programs(1) - 1
    n_chunks = TQ // CHUNK

    @pl.when(j == 0)
    def _():
        for ref in (kp_ref, kpr_ref, vp_ref, vpr_ref):
            ref[...] = jnp.zeros_like(ref)
        row = lax.broadcasted_iota(jnp.int32, (B_CHUNK, B_CHUNK), 0)
        col = lax.broadcasted_iota(jnp.int32, (B_CHUNK, B_CHUNK), 1)
        for g in range(B_GROUPS):
            wpair_ref[g // 2, :, (g % 2) * B_CHUNK:(g % 2 + 1) * B_CHUNK] = jnp.where(
                row >= col, ws_ref[g], 0.0).astype(BF16)

    h = _rms(x_ref[0], nm_ref[...]).astype(BF16)
    z_ref[...] = _dot(h, win_ref[...])

    lo_half = lax.broadcasted_iota(jnp.int32, (1, PAIR), 1) < A_HEAD_DIM
    r_i = lax.broadcasted_iota(jnp.int32, (PAIR, PAIR), 0) // A_HEAD_DIM
    c_i = lax.broadcasted_iota(jnp.int32, (PAIR, PAIR), 1) // A_HEAD_DIM
    half_mean = jnp.where(r_i == c_i, 1.0 / A_HEAD_DIM, 0.0).astype(BF16)

    def mean_sq_halves(x):
        x2 = x * x
        hi = x2.astype(BF16)
        lo = (x2 - hi.astype(F32)).astype(BF16)
        return _dot(hi, half_mean) + _dot(lo, half_mean)

    def block_diag(top, bot):
        zero = jnp.zeros_like(top)
        return jnp.concatenate([jnp.where(lo_half, top, zero), jnp.where(lo_half, zero, bot)], axis=0)

    def chunk(c, carry):
        r0 = pl.multiple_of(c * CHUNK, CHUNK)
        rows = pl.ds(r0, CHUNK)
        first = jnp.where(jnp.logical_and(j == 0, c == 0), 1, 0)

        kraw = z_ref[rows, A_Q:A_Q + A_KV]
        v = z_ref[rows, A_Q + A_KV:A_Q + 2 * A_KV]
        kn = kraw * lax.rsqrt(mean_sq_halves(kraw) + EPS) * kg_ref[...]
        kb, kbr = kn.astype(BF16), pltpu.roll(kn, A_HEAD_DIM, 1).astype(BF16)
        vb, vbr = v.astype(BF16), pltpu.roll(v, A_HEAD_DIM, 1).astype(BF16)
        k2 = jnp.concatenate([kp_ref[...], kb], axis=0)
        k2r = jnp.concatenate([kpr_ref[...], kbr], axis=0)
        v2 = jnp.concatenate([vp_ref[...], vb], axis=0)
        v2r = jnp.concatenate([vpr_ref[...], vbr], axis=0)
        kbd = [block_diag(k2, k2r), block_diag(k2r, k2)]
        vbd = [block_diag(v2, v2r), block_diag(v2r, v2)]

        scores = []
        for i in range(N_PAIRS):
            ps = slice(i * PAIR, (i + 1) * PAIR)
            qraw = z_ref[rows, i * PAIR:(i + 1) * PAIR]
            qn = qraw * lax.rsqrt(mean_sq_halves(qraw) + EPS) * (qg_ref[:, ps] * ATTN_SCALE)
            s = lax.dot_general(qn.astype(BF16), kbd[i // (A_GROUP // 2)], _NT,
                                preferred_element_type=F32)
            scores.append(s + tabp_ref[first, i])
        outs = []
        for i in range(N_PAIRS):
            es, rs = [], []
            for hh in range(2):
                sh = scores[i][:, hh * 2 * WINDOW:(hh + 1) * 2 * WINDOW]
                sk = sink_ref[2 * i + hh]
                m = jnp.maximum(jnp.max(sh, axis=-1, keepdims=True), sk)
                e = jnp.exp(sh - m)
                rs.append(1.0 / (jnp.sum(e, axis=-1, keepdims=True) + jnp.exp(sk - m)))
                es.append(e.astype(BF16))
            o = _dot(jnp.concatenate(es, axis=-1), vbd[i // (A_GROUP // 2)])
            outs.append(o * jnp.where(lo_half, rs[0], rs[1]))
        mix_ref[rows, 0:A_Q] = jnp.concatenate(outs, axis=-1).astype(BF16)

        zu = z_ref[rows, A_Q + 2 * A_KV:A_Q + 2 * A_KV + B_WIDTH]
        zv = z_ref[rows, A_Q + 2 * A_KV + B_WIDTH:AB_IN]
        u = _gelu(zu)
        vln = _layernorm(_gelu(zv), lng_ref[...], lnb_ref[...])
        vlb = vln.astype(BF16)
        sparts = []
        for i in range(B_GROUPS // 2):
            vpair = vlb[:, i * PAIR:(i + 1) * PAIR]
            sparts.append(_dot(wpair_ref[i], block_diag(vpair, vpair)))
        bm = u * (jnp.concatenate(sparts, axis=-1) + bsp_ref[...])
        mix_ref[rows, A_Q:AB_MIX] = bm.astype(BF16)

        kp_ref[...] = kb
        kpr_ref[...] = kbr
        vp_ref[...] = vb
        vpr_ref[...] = vbr

        kl_ref[...] = kn
        vl_ref[...] = v
        gl_ref[...] = vln
        return carry

    lax.fori_loop(0, n_chunks, chunk, 0, unroll=AB_UNROLL)
    y_ref[0] = x_ref[0] + _dot(mix_ref[...], wout_ref[...])

    @pl.when(j == last_j)
    def _():
        knew_ref[0] = kl_ref[...]
        vnew_ref[0] = vl_ref[...]
        gv_ref[0] = gl_ref[...]


def _ab_prompt(x, nm, w_in, qg, kg, sink, tabp, lng, lnb, w_s, bsp, w_out):
    nb, seq, _ = x.shape
    grid = (nb, seq // TQ)
    blk = lambda b, j: (b, j, 0)
    per_b = lambda b, j: (b, 0, 0)
    return pl.pallas_call(
        _ab_prompt_kernel,
        grid=grid,
        in_specs=[
            _SMEM,
            pl.BlockSpec((1, TQ, D_MODEL), blk),
            _full((1, D_MODEL)),
            _resident((D_MODEL, AB_IN)),
            _full((1, A_Q)),
            _full((1, A_KV)),
            _resident((2, N_PAIRS, WINDOW, 4 * WINDOW)),
            _full((1, B_WIDTH)),
            _full((1, B_WIDTH)),
            _resident((B_GROUPS, B_CHUNK, B_CHUNK)),
            _resident((B_CHUNK, B_WIDTH)),
            _resident((AB_MIX, D_MODEL)),
        ],
        out_specs=[
            pl.BlockSpec((1, TQ, D_MODEL), blk),
            pl.BlockSpec((1, WINDOW, A_KV), per_b),
            pl.BlockSpec((1, WINDOW, A_KV), per_b),
            pl.BlockSpec((1, B_CHUNK, B_WIDTH), per_b),
        ],
        out_shape=[
            jax.ShapeDtypeStruct((nb, seq, D_MODEL), F32),
            jax.ShapeDtypeStruct((nb, WINDOW, A_KV), F32),
            jax.ShapeDtypeStruct((nb, WINDOW, A_KV), F32),
            jax.ShapeDtypeStruct((nb, B_CHUNK, B_WIDTH), F32),
        ],
        scratch_shapes=[
            pltpu.VMEM((TQ, AB_IN), F32),
            pltpu.VMEM((TQ, AB_MIX), BF16),
            pltpu.VMEM((WINDOW, A_KV), BF16),
            pltpu.VMEM((WINDOW, A_KV), BF16),
            pltpu.VMEM((WINDOW, A_KV), BF16),
            pltpu.VMEM((WINDOW, A_KV), BF16),
            pltpu.VMEM((B_GROUPS // 2, B_CHUNK, 2 * B_CHUNK), BF16),
            pltpu.VMEM((WINDOW, A_KV), F32),
            pltpu.VMEM((WINDOW, A_KV), F32),
            pltpu.VMEM((B_CHUNK, B_WIDTH), F32),
        ],
        compiler_params=pltpu.CompilerParams(
            dimension_semantics=("arbitrary", "arbitrary"), vmem_limit_bytes=VMEM_LIMIT),
        name="ab_prompt",
    )(sink, x, nm, w_in, qg, kg, tabp, lng, lnb, w_s, bsp, w_out)


FF_TILE = 256


def _ffn_kernel(xp_ref, xs_ref, g_ref, wg_ref, wu_ref, wd_ref, yp_ref, ys_ref,
                wg_s, wu_s, wd_s, *, n_cast, n_prompt):
    s = pl.program_id(0)

    for c in range(n_cast):
        @pl.when(s == c)
        def _(c=c):
            tile = slice(c * FF_TILE, (c + 1) * FF_TILE)
            wg_s[:, tile] = wg_ref[...].astype(BF16)
            wu_s[:, tile] = wu_ref[...].astype(BF16)
            wd_s[tile, :] = wd_ref[...].astype(BF16)

    def swiglu(x):
        h = _rms(x, g_ref[...]).astype(BF16)
        gate = _dot(h, wg_s[...])
        up = _dot(h, wu_s[...])
        a = (gate * jax.nn.sigmoid(gate) * up).astype(BF16)
        return x + _dot(a, wd_s[...])

    @pl.when(jnp.logical_and(s >= n_cast, s < n_cast + n_prompt))
    def _():
        yp_ref[...] = swiglu(xp_ref[...])

    @pl.when(s == n_cast + n_prompt)
    def _():
        ys_ref[...] = swiglu(xs_ref[...])


def _ffn(xp, xs, g, w_gate, w_up, w_down, layer):
    rows, ns = xp.shape[0], xs.shape[0]
    n_cast, n_prompt = D_FF // FF_TILE, rows // TM
    w_tile = lambda s: jnp.minimum(s, n_cast - 1)
    row_blk = lambda s: (jnp.clip(s - n_cast, 0, n_prompt - 1), 0)
    return pl.pallas_call(
        functools.partial(_ffn_kernel, n_cast=n_cast, n_prompt=n_prompt),
        grid=(n_cast + n_prompt + 1,),
        in_specs=[
            pl.BlockSpec((TM, D_MODEL), row_blk),
            _full((ns, D_MODEL)),
            _full((1, D_MODEL)),
            pl.BlockSpec((None, D_MODEL, FF_TILE), lambda s: (layer, 0, w_tile(s))),
            pl.BlockSpec((None, D_MODEL, FF_TILE), lambda s: (layer, 0, w_tile(s))),
            pl.BlockSpec((None, FF_TILE, D_MODEL), lambda s: (layer, w_tile(s), 0)),
        ],
        out_specs=[pl.BlockSpec((TM, D_MODEL), row_blk), _full((ns, D_MODEL))],
        out_shape=[jax.ShapeDtypeStruct((rows, D_MODEL), F32), jax.ShapeDtypeStruct((ns, D_MODEL), F32)],
        scratch_shapes=[
            pltpu.VMEM((D_MODEL, D_FF), BF16),
            pltpu.VMEM((D_MODEL, D_FF), BF16),
            pltpu.VMEM((D_FF, D_MODEL), BF16),
        ],
        compiler_params=pltpu.CompilerParams(
            dimension_semantics=("arbitrary",), vmem_limit_bytes=VMEM_LIMIT),
        name="ffn",
    )(xp, xs, g, w_gate, w_up, w_down)


def _lower_bound(clb):
    m = jnp.max(clb, axis=0, keepdims=True)
    e = jnp.exp(clb - m)
    sm = e / jnp.sum(e, axis=0, keepdims=True)
    return (sm[0:1] + sm[1:2]) - sm[0:1]


def _split3(x):
    hi = x.astype(BF16)
    r = x - hi.astype(F32)
    mid = r.astype(BF16)
    lo = (r - mid.astype(F32)).astype(BF16)
    return hi, mid, lo


def _neg_abs(x):
    return lax.bitcast_convert_type(
        lax.bitcast_convert_type(x, jnp.uint32) | jnp.uint32(0x80000000), F32)


def _pair_level_table():
    t = np.arange(CHUNK)[:, None]
    s = np.arange(CHUNK)[None, :]
    lev = np.floor(np.log2(np.maximum(t ^ s, 1))).astype(np.int32)
    lev = np.where(t == s, -1, lev)
    return np.where(s > t, -2, lev).astype(np.int32)


def _level_operand(p, q, kk, f, b2):
    m = 2 ** p
    if m < VREG_ROWS:
        shape3 = (CHUNK // VREG_ROWS, VREG_ROWS, q.shape[1])
        sub = lax.broadcasted_iota(jnp.int32, (1, VREG_ROWS, q.shape[1]), 1)
        upper = ((sub >> p) & 1) == 1
        q3, k3 = q.reshape(shape3), kk.reshape(shape3)
        if p == 0:
            y = jnp.where(upper, q3 * f.reshape(shape3), k3)
        else:
            b3 = b2.reshape(shape3)
            be = b3[:, m - 1:m, :]
            for k in range(1, VREG_ROWS // (2 * m)):
                be = jnp.where(sub >= 2 * m * k, b3[:, 2 * m * k + m - 1:2 * m * k + m, :], be)
            y = jnp.where(upper, q3, k3) * jnp.exp2(_neg_abs(b3 - be))
        return y.reshape(q.shape).astype(BF16)
    parts = []
    for k in range(CHUNK // (2 * m)):
        lo = slice(2 * m * k, 2 * m * k + m)
        up = slice(2 * m * k + m, 2 * m * (k + 1))
        be = b2[2 * m * k + m - 1:2 * m * k + m, :]
        parts.append(kk[lo] * jnp.exp2(be - b2[lo]))
        parts.append(q[up] * jnp.exp2(b2[up] - be))
    return jnp.concatenate(parts, axis=0).astype(BF16)


def _merge_level(p, att, pm, lev):
    m = 2 ** p
    if m < VREG_ROWS:
        return jnp.where(lev == p, pm, att)
    col = lax.broadcasted_iota(jnp.int32, (1, CHUNK), 1)
    parts = []
    for k in range(CHUNK // (2 * m)):
        lo = slice(2 * m * k, 2 * m * k + m)
        up = slice(2 * m * k + m, 2 * m * (k + 1))
        parts.append(att[lo])
        parts.append(jnp.where((col >= 2 * m * k) & (col < 2 * m * k + m), pm[up], att[up]))
    return jnp.concatenate(parts, axis=0)


def _hgrn_prompt_kernel(x_ref, nm_ref, win_ref, clb_ref, on_ref, wout_ref, lev_ref,
                        y_ref, st_ref,
                        z_ref, o_ref, stt_ref, k_ref):
    j = pl.program_id(1)
    last_j = pl.num_programs(1) - 1
    n_chunks = TQ // CHUNK
    n_levels = int(math.log2(CHUNK))

    @pl.when(j == 0)
    def _():
        stt_ref[...] = jnp.zeros_like(stt_ref)

    h = _rms(x_ref[0], nm_ref[...]).astype(BF16)
    z_ref[...] = _dot(h, win_ref[...])
    lb = _lower_bound(clb_ref[...])

    row = lax.broadcasted_iota(jnp.int32, (CHUNK, CHUNK), 0)
    col = lax.broadcasted_iota(jnp.int32, (CHUNK, CHUNK), 1)
    ltri = (row >= col).astype(BF16)

    def chunk_rows(c):
        return pl.ds(pl.multiple_of(c * CHUNK, CHUNK), CHUNK)

    def prefix(g, worst):
        rows = [chunk_rows(g * PREFIX_GROUP + i) for i in range(PREFIX_GROUP)]
        gates = [z_ref[r, C_F:2 * C_F] for r in rows]
        for r, gate in zip(rows, gates):
            f_all = lb + (1.0 - lb) * jax.nn.sigmoid(gate)
            k_ref[r, :] = 1.0 - f_all
            hi, mid, lo = _split3(jnp.log2(f_all))
            b2 = (_dot(ltri, hi) + _dot(ltri, mid)) + _dot(ltri, lo)
            z_ref[r, C_F:2 * C_F] = b2
            b_mid = b2[CHUNK // 2 - 1:CHUNK // 2, :]
            b_last = b2[CHUNK - 1:CHUNK, :]
            worst = jnp.maximum(worst, jnp.maximum(-b_mid, b_mid - b_last))
        return worst

    worst = lax.fori_loop(0, n_chunks // PREFIX_GROUP, prefix, jnp.zeros((1, C_F), F32))
    bounded = jnp.max(worst) <= SAFE_LOG2_RANGE

    def finish_head(rows, hd, o):
        gt = z_ref[rows, 2 * C_F + C_V + hd * C_VAL_DIM:2 * C_F + C_V + (hd + 1) * C_VAL_DIM]
        o = _rms(o, on_ref[...]) * jax.nn.sigmoid(gt)
        o_ref[rows, hd * C_VAL_DIM:(hd + 1) * C_VAL_DIM] = o.astype(BF16)

    def head_inputs(rows, hd):
        q = z_ref[rows, hd * C_KEY_DIM:(hd + 1) * C_KEY_DIM]
        kk = k_ref[rows, hd * C_KEY_DIM:(hd + 1) * C_KEY_DIM]
        b2 = z_ref[rows, C_F + hd * C_KEY_DIM:C_F + (hd + 1) * C_KEY_DIM]
        ivb = z_ref[rows, 2 * C_F + hd * C_VAL_DIM:2 * C_F + (hd + 1) * C_VAL_DIM].astype(BF16)
        return q, kk, b2, ivb

    def factored_chunk(c, carry):
        rows = chunk_rows(c)
        for hd in range(C_HEADS):
            q, kk, b2, ivb = head_inputs(rows, hd)
            b_mid = b2[CHUNK // 2 - 1:CHUNK // 2, :]
            b_last = b2[CHUNK - 1:CHUNK, :]
            qs = (q * jnp.exp2(b2 - b_mid)).astype(BF16)
            kd = (kk * jnp.exp2(b_mid - b2)).astype(BF16)
            att = jnp.where(row >= col, lax.dot_general(qs, kd, _NT, preferred_element_type=F32), 0.0)
            stt = stt_ref[hd]
            o = lax.dot_general(qs, (stt * jnp.exp2(b_mid)).astype(BF16), _NT,
                                preferred_element_type=F32) + _dot(att.astype(BF16), ivb)
            stt_ref[hd] = stt * jnp.exp2(b_last) + jnp.exp2(b_last - b_mid) * lax.dot_general(
                ivb, kd, _TN, preferred_element_type=F32)
            finish_head(rows, hd, o)
        return carry

    def tree_chunk(c, carry):
        rows = chunk_rows(c)
        lev = lev_ref[...]

        def products(hd):
            q, kk, b2, ivb = head_inputs(rows, hd)
            diag = jnp.sum(q * kk, axis=-1, keepdims=True)
            pms = []
            for p in range(n_levels):
                y = _level_operand(p, q, kk, 1.0 - kk, b2)
                pms.append(lax.dot_general(y, y, _NT, preferred_element_type=F32))
            stt = stt_ref[hd]
            o_prev = lax.dot_general((q * jnp.exp2(b2)).astype(BF16), stt.astype(BF16), _NT,
                                     preferred_element_type=F32)
            b_last = b2[CHUNK - 1:CHUNK, :]
            kd = (kk * jnp.exp2(b_last - b2)).astype(BF16)
            stt_ref[hd] = stt * jnp.exp2(b_last) + lax.dot_general(
                ivb, kd, _TN, preferred_element_type=F32)
            return diag, pms, o_prev, ivb

        def finish(hd, diag, pms, o_prev, ivb):
            att = jnp.where(lev == -1, diag, 0.0)
            for p in range(n_levels):
                att = _merge_level(p, att, pms[p], lev)
            finish_head(rows, hd, o_prev + _dot(att.astype(BF16), ivb))

        pending = [products(hd) for hd in range(HEAD_SKEW)]
        for hd in range(C_HEADS):
            if hd + HEAD_SKEW < C_HEADS:
                pending.append(products(hd + HEAD_SKEW))
            finish(hd, *pending.pop(0))
        return carry

    @pl.when(bounded)
    def _():
        lax.fori_loop(0, n_chunks, factored_chunk, 0, unroll=FACTORED_UNROLL)

    @pl.when(jnp.logical_not(bounded))
    def _():
        lax.fori_loop(0, n_chunks, tree_chunk, 0)

    y_ref[0] = x_ref[0] + _dot(o_ref[...], wout_ref[...])

    @pl.when(j == last_j)
    def _():
        for hd in range(C_HEADS):
            st_ref[0, hd] = stt_ref[hd].T


def _hgrn_prompt(x, nm, w_in, clb, on, w_out):
    nb, seq, _ = x.shape
    grid = (nb, seq // TQ)
    blk = lambda b, j: (b, j, 0)
    return pl.pallas_call(
        _hgrn_prompt_kernel,
        grid=grid,
        in_specs=[
            pl.BlockSpec((1, TQ, D_MODEL), blk),
            _full((1, D_MODEL)),
            _resident((D_MODEL, C_IN)),
            _full((DEPTH, C_F)),
            _full((1, C_VAL_DIM)),
            _resident((C_V, D_MODEL)),
            _full((CHUNK, CHUNK)),
        ],
        out_specs=[
            pl.BlockSpec((1, TQ, D_MODEL), blk),
            pl.BlockSpec((1, C_HEADS, C_KEY_DIM, C_VAL_DIM), lambda b, j: (b, 0, 0, 0)),
        ],
        out_shape=[
            jax.ShapeDtypeStruct((nb, seq, D_MODEL), F32),
            jax.ShapeDtypeStruct((nb, C_HEADS, C_KEY_DIM, C_VAL_DIM), F32),
        ],
        scratch_shapes=[
            pltpu.VMEM((TQ, C_IN), F32),
            pltpu.VMEM((TQ, C_V), BF16),
            pltpu.VMEM((C_HEADS, C_VAL_DIM, C_KEY_DIM), F32),
            pltpu.VMEM((TQ, C_F), F32),
        ],
        compiler_params=pltpu.CompilerParams(
            dimension_semantics=("arbitrary", "arbitrary"), vmem_limit_bytes=VMEM_LIMIT),
        name="hgrn_prompt",
    )(x, nm, w_in, clb, on, w_out, jnp.asarray(_pair_level_table()))


def _ab_sample_proj_kernel(w00_ref, b0_ref, x_ref, nm_ref, win_ref, qn_ref, kn_ref, lng_ref, lnb_ref,
                           qx_ref, knew_ref, vnew_ref, bm_ref, gv_ref):
    n = x_ref.shape[0]
    h = _rms(x_ref[...], nm_ref[...]).astype(BF16)
    z = _dot(h, win_ref[...])
    zeros = jnp.zeros((n, A_HEAD_DIM), F32)
    for hh in range(A_HEADS):
        qh = _rms(z[:, hh * A_HEAD_DIM:(hh + 1) * A_HEAD_DIM], qn_ref[...]) * ATTN_SCALE
        qx_ref[hh] = jnp.concatenate([qh, zeros] if hh // A_GROUP == 0 else [zeros, qh], axis=-1)
    kparts = []
    for g in range(A_KV_HEADS):
        kparts.append(_rms(z[:, A_Q + g * A_HEAD_DIM:A_Q + (g + 1) * A_HEAD_DIM], kn_ref[...]))
    knew_ref[...] = jnp.concatenate(kparts, axis=-1)
    vnew_ref[...] = z[:, A_Q + A_KV:A_Q + 2 * A_KV]

    u = _gelu(z[:, A_Q + 2 * A_KV:A_Q + 2 * A_KV + B_WIDTH])
    vln = _layernorm(_gelu(z[:, A_Q + 2 * A_KV + B_WIDTH:AB_IN]), lng_ref[...], lnb_ref[...])
    grp = lax.broadcasted_iota(jnp.int32, (1, B_WIDTH), 1) // B_GROUP_DIM
    srow = jnp.zeros((1, B_WIDTH), F32)
    brow = jnp.zeros((1, B_WIDTH), F32)
    for g in range(B_GROUPS):
        srow = jnp.where(grp == g, w00_ref[g], srow)
        brow = jnp.where(grp == g, b0_ref[g], brow)
    bm_ref[...] = u * (vln * srow + brow)
    gv_ref[...] = vln


def _ab_sample_proj(x, nm, w_in, qn, kn, lng, lnb, w00, b0):
    n = x.shape[0]
    return pl.pallas_call(
        _ab_sample_proj_kernel,
        in_specs=[_SMEM, _SMEM] + [pl.BlockSpec(memory_space=pltpu.VMEM)] * 7,
        out_shape=[
            jax.ShapeDtypeStruct((A_HEADS, n, A_KV), F32),
            jax.ShapeDtypeStruct((n, A_KV), F32),
            jax.ShapeDtypeStruct((n, A_KV), F32),
            jax.ShapeDtypeStruct((n, B_WIDTH), F32),
            jax.ShapeDtypeStruct((n, B_WIDTH), F32),
        ],
        compiler_params=pltpu.CompilerParams(vmem_limit_bytes=VMEM_LIMIT),
        name="ab_sample_proj",
    )(w00, b0, x, nm, w_in, qn, kn, lng, lnb)


def _ab_sample_attn_kernel(ck_ref, cv_ref, qx_ref, kn_ref, vn_ref, sb_ref, sink_ref,
                           nk_ref, nv_ref, om_ref):
    wb = ck_ref.shape[1]
    head = lax.broadcasted_iota(jnp.int32, (1, A_HEADS, A_KV), 1)
    lane = lax.broadcasted_iota(jnp.int32, (1, A_HEADS, A_KV), 2)
    own_group = (head // A_GROUP) == (lane // A_HEAD_DIM)
    sink = sink_ref[...][None]
    kc, vc = ck_ref[...], cv_ref[...]
    kn, vn = kn_ref[...], vn_ref[...]
    nk_ref[:, 0:wb - 1, :] = kc[:, 1:wb, :]
    nk_ref[:, wb - 1:wb, :] = kn
    nv_ref[:, 0:wb - 1, :] = vc[:, 1:wb, :]
    nv_ref[:, wb - 1:wb, :] = vn
    q = qx_ref[...]
    s = jnp.einsum('bhd,bkd->bhk', q.astype(BF16), kc.astype(BF16), preferred_element_type=F32)
    s = s + sb_ref[:, 0:wb][None]
    sn = jnp.sum(q * kn, axis=-1, keepdims=True) + sb_ref[:, wb:wb + 1][None]
    m = jnp.maximum(jnp.maximum(jnp.max(s, axis=-1, keepdims=True), sn), sink)
    e = jnp.exp(s - m)
    en = jnp.exp(sn - m)
    r = 1.0 / (jnp.sum(e, axis=-1, keepdims=True) + en + jnp.exp(sink - m))
    o = jnp.einsum('bhk,bkd->bhd', (e * r).astype(BF16), vc.astype(BF16),
                   preferred_element_type=F32) + (en * r) * vn
    om_ref[...] = jnp.where(own_group, o, 0.0)


def _ab_sample_attn(ck, cv, qx, kn, vn, sb, sink):
    n, wb, _ = ck.shape
    blk3 = lambda i: (i, 0, 0)
    return pl.pallas_call(
        _ab_sample_attn_kernel,
        grid=(n // SB,),
        in_specs=[
            pl.BlockSpec((SB, wb, A_KV), blk3),
            pl.BlockSpec((SB, wb, A_KV), blk3),
            pl.BlockSpec((SB, A_HEADS, A_KV), blk3),
            pl.BlockSpec((SB, 1, A_KV), blk3),
            pl.BlockSpec((SB, 1, A_KV), blk3),
            _full((A_HEADS, 2 * WINDOW)),
            _full((A_HEADS, 1)),
        ],
        out_specs=[
            pl.BlockSpec((SB, wb, A_KV), blk3),
            pl.BlockSpec((SB, wb, A_KV), blk3),
            pl.BlockSpec((SB, A_HEADS, A_KV), blk3),
        ],
        out_shape=[
            jax.ShapeDtypeStruct((n, wb, A_KV), F32),
            jax.ShapeDtypeStruct((n, wb, A_KV), F32),
            jax.ShapeDtypeStruct((n, A_HEADS, A_KV), F32),
        ],
        compiler_params=pltpu.CompilerParams(dimension_semantics=("arbitrary",)),
        name="ab_sample_attn",
    )(ck, cv, qx, kn, vn, sb, sink)


def _residual_proj_kernel(x_ref, m_ref, w_ref, y_ref):
    y_ref[...] = x_ref[...] + _dot(m_ref[...].astype(BF16), w_ref[...])


def _residual_proj(x, mix, w):
    return pl.pallas_call(
        _residual_proj_kernel,
        out_shape=jax.ShapeDtypeStruct(x.shape, F32),
        name="residual_proj",
    )(x, mix, w)


def _hgrn_sample_proj_kernel(x_ref, nm_ref, win_ref, clb_ref, q_ref, f_ref, i_ref, sg_ref):
    h = _rms(x_ref[...], nm_ref[...]).astype(BF16)
    z = _dot(h, win_ref[...])
    lb = _lower_bound(clb_ref[...])
    q_ref[...] = z[:, 0:C_F]
    f_ref[...] = lb + (1.0 - lb) * jax.nn.sigmoid(z[:, C_F:2 * C_F])
    i_ref[...] = z[:, 2 * C_F:2 * C_F + C_V]
    sg_ref[...] = jax.nn.sigmoid(z[:, 2 * C_F + C_V:C_IN])


def _hgrn_sample_proj(x, nm, w_in, clb):
    n = x.shape[0]
    return pl.pallas_call(
        _hgrn_sample_proj_kernel,
        out_shape=[jax.ShapeDtypeStruct((n, C_F), F32)] * 2 + [jax.ShapeDtypeStruct((n, C_V), F32)] * 2,
        compiler_params=pltpu.CompilerParams(vmem_limit_bytes=VMEM_LIMIT),
        name="hgrn_sample_proj",
    )(x, nm, w_in, clb)


def _hgrn_sample_state_kernel(s_ref, fc_ref, q_ref, i_ref, sn_ref, o_ref):
    out_rows = []
    for s in range(SB):
        parts = []
        for hd in range(C_HEADS):
            hs = slice(hd * C_VAL_DIM, (hd + 1) * C_VAL_DIM)
            fb = jnp.broadcast_to(fc_ref[0, hd, :, s:s + 1], (C_KEY_DIM, C_VAL_DIM))
            sn = fb * s_ref[s, hd] + (1.0 - fb) * i_ref[s:s + 1, hs]
            sn_ref[s, hd] = sn
            parts.append(_dot(q_ref[s:s + 1, hs].astype(BF16), sn.astype(BF16)))
        out_rows.append(jnp.concatenate(parts, axis=-1))
    o_ref[...] = jnp.concatenate(out_rows, axis=0)


def _hgrn_sample_state(state, fc, q, iv):
    n = state.shape[0]
    sblk = pl.BlockSpec((SB, C_HEADS, C_KEY_DIM, C_VAL_DIM), lambda i: (i, 0, 0, 0))
    cblk = pl.BlockSpec((1, C_HEADS, C_KEY_DIM, SB), lambda i: (i, 0, 0, 0))
    rblk = pl.BlockSpec((SB, C_V), lambda i: (i, 0))
    return pl.pallas_call(
        _hgrn_sample_state_kernel,
        grid=(n // SB,),
        in_specs=[sblk, cblk, rblk, rblk],
        out_specs=[sblk, pl.BlockSpec((SB, C_V), lambda i: (i, 0))],
        out_shape=[
            jax.ShapeDtypeStruct(state.shape, F32),
            jax.ShapeDtypeStruct((n, C_V), F32),
        ],
        compiler_params=pltpu.CompilerParams(
            dimension_semantics=("arbitrary",), vmem_limit_bytes=VMEM_LIMIT),
        name="hgrn_sample_state",
    )(state, fc, q, iv)


def _hgrn_sample_out_kernel(o_ref, sg_ref, on_ref, w_ref, x_ref, y_ref):
    parts = []
    for hd in range(C_HEADS):
        parts.append(_rms(o_ref[:, hd * C_VAL_DIM:(hd + 1) * C_VAL_DIM], on_ref[...]))
    on = jnp.concatenate(parts, axis=-1) * sg_ref[...]
    y_ref[...] = x_ref[...] + _dot(on.astype(BF16), w_ref[...])


def _hgrn_sample_out(o, sg, on, w_out, x):
    return pl.pallas_call(
        _hgrn_sample_out_kernel,
        out_shape=jax.ShapeDtypeStruct(x.shape, F32),
        name="hgrn_sample_out",
    )(o, sg, on, w_out, x)


def _to_columns(a):
    n = a.shape[0]
    return a.reshape(n // SB, SB, C_HEADS, C_KEY_DIM).transpose(0, 2, 3, 1)


def kernel(x_prompt, x_sample, cache_k, cache_v, state_hgrn, norm_mix, norm_ffn, w_in_ab, w_out_ab,
           q_norm, k_norm, attn_sink, rel_bias, gmlp_ln_g, gmlp_ln_b, gmlp_w_s, gmlp_b_s,
           w_in_c, c_lower_bounds, c_out_norm, w_out_c, w_gate, w_up, w_down):
    assert norm_mix.shape[0] == DEPTH == 2 and w_in_ab.shape[0] == 1 and w_in_c.shape[0] == 1
    nb, seq, _ = x_prompt.shape
    ns = x_sample.shape[0]
    assert x_sample.shape[1] == 1 and cache_k.shape[2] == WINDOW

    row = lambda v: v.reshape(1, -1)
    bf = lambda w: w.astype(BF16)
    w_in_ab0, w_out_ab0 = bf(w_in_ab[0]), bf(w_out_ab[0])
    w_in_c0, w_out_c0 = bf(w_in_c[0]), bf(w_out_c[0])
    nm, nf = norm_mix, norm_ffn
    qn, kn = row(q_norm[0]), row(k_norm[0])
    lng, lnb = row(gmlp_ln_g[0]), row(gmlp_ln_b[0])
    sink = attn_sink[0]

    tab, tabp = _bias_table(rel_bias)

    xp, knew_p, vnew_p, gv_p = _ab_prompt(
        x_prompt, row(nm[0]), w_in_ab0, jnp.tile(qn, (1, A_HEADS)), jnp.tile(kn, (1, A_KV_HEADS)),
        sink, tabp, lng, lnb, gmlp_w_s[0], jnp.repeat(gmlp_b_s[0].T, B_GROUP_DIM, axis=1), w_out_ab0)
    xs = x_sample.reshape(ns, D_MODEL)
    qx, knew_s, vnew_s, bm_s, gv_s = _ab_sample_proj(
        xs, row(nm[0]), w_in_ab0, qn, kn, lng, lnb, gmlp_w_s[0, :, 0, 0], gmlp_b_s[0, :, 0])
    sb = jnp.pad(tab[:, WINDOW - 1, WINDOW - 1:], ((0, 0), (0, WINDOW - 1)))
    nk_s, nv_s, om = _ab_sample_attn(
        cache_k[0].reshape(ns, WINDOW, A_KV), cache_v[0].reshape(ns, WINDOW, A_KV),
        qx.transpose(1, 0, 2), knew_s[:, None, :], vnew_s[:, None, :], sb, sink.reshape(A_HEADS, 1))
    om = om.reshape(ns, A_KV_HEADS, A_GROUP, A_KV_HEADS, A_HEAD_DIM)
    a_s = jnp.stack([om[:, g, :, g, :] for g in range(A_KV_HEADS)], axis=1).reshape(ns, A_Q)
    xs = _residual_proj(xs, jnp.concatenate([a_s, bm_s], axis=-1), w_out_ab0)
    xp, xs = _ffn(xp.reshape(nb * seq, D_MODEL), xs, row(nf[0]), w_gate, w_up, w_down, 0)

    xp, st_p = _hgrn_prompt(xp.reshape(nb, seq, D_MODEL), row(nm[1]), w_in_c0, c_lower_bounds,
                            row(c_out_norm[0]), w_out_c0)
    q_s, f_s, i_s, sg_s = _hgrn_sample_proj(xs, row(nm[1]), w_in_c0, c_lower_bounds)
    st_s, o_s = _hgrn_sample_state(state_hgrn[0], _to_columns(f_s), q_s, i_s)
    xs = _hgrn_sample_out(o_s, sg_s, row(c_out_norm[0]), w_out_c0, xs)
    xp, xs = _ffn(xp.reshape(nb * seq, D_MODEL), xs, row(nf[1]), w_gate, w_up, w_down, 1)

    kv5 = lambda a: a.reshape(1, a.shape[0], WINDOW, A_KV_HEADS, A_HEAD_DIM)
    return (xp.reshape(nb, seq, D_MODEL), xs.reshape(ns, 1, D_MODEL),
            kv5(knew_p), kv5(vnew_p), kv5(nk_s), kv5(nv_s),
            gv_p[None], gv_s.reshape(1, ns, 1, B_WIDTH),
            st_p[None], st_s[None])
```

```python
import functools
import math

import jax
import jax.numpy as jnp
import numpy as np
from jax import lax
from jax.experimental import pallas as pl
from jax.experimental.pallas import tpu as pltpu

F32 = jnp.float32
BF16 = jnp.bfloat16

D_MODEL = 1024
DEPTH = 2
A_HEADS = 8
A_KV_HEADS = 2
A_GROUP = A_HEADS // A_KV_HEADS
A_HEAD_DIM = 64
WINDOW = 128
ATTN_SCALE = A_HEAD_DIM ** -0.5
NUM_BUCKETS = 32
MAX_DISTANCE = 128
A_Q = A_HEADS * A_HEAD_DIM
A_KV = A_KV_HEADS * A_HEAD_DIM
B_GROUPS = 8
B_GROUP_DIM = 64
B_WIDTH = B_GROUPS * B_GROUP_DIM
B_CHUNK = 128
AB_IN = A_Q + 2 * A_KV + 2 * B_WIDTH
AB_MIX = A_Q + B_WIDTH
C_HEADS = 8
C_KEY_DIM = 128
C_VAL_DIM = 128
C_F = C_HEADS * C_KEY_DIM
C_V = C_HEADS * C_VAL_DIM
C_IN = 2 * C_F + 2 * C_V
D_FF = 2816
EPS = 1e-6

NEG = -1e30

VMEM_LIMIT = 56 * 1024 * 1024
VREG_ROWS = 8

CHUNK = 128
TQ = 1024
TM = 512
SB = 8
HEAD_SKEW = 2
SAFE_LOG2_RANGE = 64.0
PREFIX_GROUP = 4
FACTORED_UNROLL = 4
AB_UNROLL = 2

_NT = (((1,), (1,)), ((), ()))
_TN = (((0,), (0,)), ((), ()))


def _rms(x, g):
    return x * lax.rsqrt(jnp.mean(x * x, axis=-1, keepdims=True) + EPS) * g


def _gelu(x):
    return 0.5 * x * (1.0 + lax.erf(x * math.sqrt(0.5)))


def _layernorm(x, g, b):
    xc = x - jnp.mean(x, axis=-1, keepdims=True)
    return xc * lax.rsqrt(jnp.mean(xc * xc, axis=-1, keepdims=True) + EPS) * g + b


def _dot(a, b):
    return jnp.dot(a, b, preferred_element_type=F32)


def _full(shape):
    n = len(shape)
    return pl.BlockSpec(shape, lambda *_: (0,) * n)


def _resident(shape):
    n = len(shape)
    return pl.BlockSpec(shape, lambda *_: (0,) * n, pipeline_mode=pl.Buffered(1))


_SMEM = pl.BlockSpec(memory_space=pltpu.SMEM)


def _bias_table_kernel(rel_ref, tab_ref, tabp_ref):
    qi = lax.broadcasted_iota(jnp.int32, (WINDOW, 2 * WINDOW), 0)
    kj = lax.broadcasted_iota(jnp.int32, (WINDOW, 2 * WINDOW), 1)
    dist = qi + WINDOW - kj
    ok = (dist >= 0) & (dist < WINDOW)
    max_exact = NUM_BUCKETS // 2
    d = jnp.maximum(dist, 0)
    dl = jnp.maximum(d, 1).astype(F32)
    v = (jnp.log(dl / max_exact) / math.log(MAX_DISTANCE / max_exact) * (NUM_BUCKETS - max_exact))
    far = d >= max_exact
    hits = []
    for b in range(NUM_BUCKETS):
        if b < max_exact:
            hits.append(d == b)
        elif b < NUM_BUCKETS - 1:
            hits.append(far & (v >= b - max_exact) & (v < b - max_exact + 1))
        else:
            hits.append(far & (v >= b - max_exact))
    for h in range(A_HEADS):
        acc = jnp.zeros((WINDOW, 2 * WINDOW), F32)
        for b in range(NUM_BUCKETS):
            acc = jnp.where(hits[b], rel_ref[b, h], acc)
        t = jnp.where(ok, acc, NEG)
        tab_ref[h] = t
        cols = slice((h % 2) * 2 * WINDOW, (h % 2 + 1) * 2 * WINDOW)
        tabp_ref[0, h // 2, :, cols] = t
        tabp_ref[1, h // 2, :, cols] = jnp.where(kj < WINDOW, NEG, t)


def _bias_table(rel_bias):
    return pl.pallas_call(
        _bias_table_kernel,
        out_shape=[
            jax.ShapeDtypeStruct((A_HEADS, WINDOW, 2 * WINDOW), F32),
            jax.ShapeDtypeStruct((2, A_HEADS // 2, WINDOW, 4 * WINDOW), F32),
        ],
        in_specs=[_SMEM],
        name="bias_table",
    )(rel_bias)


PAIR = 2 * A_HEAD_DIM
N_PAIRS = A_HEADS // 2


def _ab_prompt_kernel(sink_ref, x_ref, nm_ref, win_ref, qg_ref, kg_ref, tabp_ref, lng_ref, lnb_ref,
                      ws_ref, bsp_ref, wout_ref,
                      y_ref, knew_ref, vnew_ref, gv_ref,
                      z_ref, mix_ref, kp_ref, kpr_ref, vp_ref, vpr_ref, wpair_ref, kl_ref, vl_ref, gl_ref):
    j = pl.program_id(1)
    last_j = pl.num_programs(1) - 1
    n_chunks = TQ // CHUNK

    @pl.when(j == 0)
    def _():
        for ref in (kp_ref, kpr_ref, vp_ref, vpr_ref):
            ref[...] = jnp.zeros_like(ref)
        row = lax.broadcasted_iota(jnp.int32, (B_CHUNK, B_CHUNK), 0)
        col = lax.broadcasted_iota(jnp.int32, (B_CHUNK, B_CHUNK), 1)
        for g in range(B_GROUPS):
            wpair_ref[g // 2, :, (g % 2) * B_CHUNK:(g % 2 + 1) * B_CHUNK] = jnp.where(
                row >= col, ws_ref[g], 0.0).astype(BF16)

    h = _rms(x_ref[0], nm_ref[...]).astype(BF16)
    z_ref[...] = _dot(h, win_ref[...])

    lo_half = lax.broadcasted_iota(jnp.int32, (1, PAIR), 1) < A_HEAD_DIM
    r_i = lax.broadcasted_iota(jnp.int32, (PAIR, PAIR), 0) // A_HEAD_DIM
    c_i = lax.broadcasted_iota(jnp.int32, (PAIR, PAIR), 1) // A_HEAD_DIM
    half_mean = jnp.where(r_i == c_i, 1.0 / A_HEAD_DIM, 0.0).astype(BF16)

    def mean_sq_halves(x):
        x2 = x * x
        hi = x2.astype(BF16)
        lo = (x2 - hi.astype(F32)).astype(BF16)
        return _dot(hi, half_mean) + _dot(lo, half_mean)

    def block_diag(top, bot):
        zero = jnp.zeros_like(top)
        return jnp.concatenate([jnp.where(lo_half, top, zero), jnp.where(lo_half, zero, bot)], axis=0)

    def chunk(c, carry):
        r0 = pl.multiple_of(c * CHUNK, CHUNK)
        rows = pl.ds(r0, CHUNK)
        first = jnp.where(jnp.logical_and(j == 0, c == 0), 1, 0)

        kraw = z_ref[rows, A_Q:A_Q + A_KV]
        v = z_ref[rows, A_Q + A_KV:A_Q + 2 * A_KV]
        kn = kraw * lax.rsqrt(mean_sq_halves(kraw) + EPS) * kg_ref[...]
        kb, kbr = kn.astype(BF16), pltpu.roll(kn, A_HEAD_DIM, 1).astype(BF16)
        vb, vbr = v.astype(BF16), pltpu.roll(v, A_HEAD_DIM, 1).astype(BF16)
        k2 = jnp.concatenate([kp_ref[...], kb], axis=0)
        k2r = jnp.concatenate([kpr_ref[...], kbr], axis=0)
        v2 = jnp.concatenate([vp_ref[...], vb], axis=0)
        v2r = jnp.concatenate([vpr_ref[...], vbr], axis=0)
        kbd = [block_diag(k2, k2r), block_diag(k2r, k2)]
        vbd = [block_diag(v2, v2r), block_diag(v2r, v2)]

        scores = []
        for i in range(N_PAIRS):
            ps = slice(i * PAIR, (i + 1) * PAIR)
            qraw = z_ref[rows, i * PAIR:(i + 1) * PAIR]
            qn = qraw * lax.rsqrt(mean_sq_halves(qraw) + EPS) * (qg_ref[:, ps] * ATTN_SCALE)
            s = lax.dot_general(qn.astype(BF16), kbd[i // (A_GROUP // 2)], _NT,
                                preferred_element_type=F32)
            scores.append(s + tabp_ref[first, i])
        outs = []
        for i in range(N_PAIRS):
            es, rs = [], []
            for hh in range(2):
                sh = scores[i][:, hh * 2 * WINDOW:(hh + 1) * 2 * WINDOW]
                sk = sink_ref[2 * i + hh]
                m = jnp.maximum(jnp.max(sh, axis=-1, keepdims=True), sk)
                e = jnp.exp(sh - m)
                rs.append(1.0 / (jnp.sum(e, axis=-1, keepdims=True) + jnp.exp(sk - m)))
                es.append(e.astype(BF16))
            o = _dot(jnp.concatenate(es, axis=-1), vbd[i // (A_GROUP // 2)])
            outs.append(o * jnp.where(lo_half, rs[0], rs[1]))
        mix_ref[rows, 0:A_Q] = jnp.concatenate(outs, axis=-1).astype(BF16)

        zu = z_ref[rows, A_Q + 2 * A_KV:A_Q + 2 * A_KV + B_WIDTH]
        zv = z_ref[rows, A_Q + 2 * A_KV + B_WIDTH:AB_IN]
        u = _gelu(zu)
        vln = _layernorm(_gelu(zv), lng_ref[...], lnb_ref[...])
        vlb = vln.astype(BF16)
        sparts = []
        for i in range(B_GROUPS // 2):
            vpair = vlb[:, i * PAIR:(i + 1) * PAIR]
            sparts.append(_dot(wpair_ref[i], block_diag(vpair, vpair)))
        bm = u * (jnp.concatenate(sparts, axis=-1) + bsp_ref[...])
        mix_ref[rows, A_Q:AB_MIX] = bm.astype(BF16)

        kp_ref[...] = kb
        kpr_ref[...] = kbr
        vp_ref[...] = vb
        vpr_ref[...] = vbr

        kl_ref[...] = kn
        vl_ref[...] = v
        gl_ref[...] = vln
        return carry

    lax.fori_loop(0, n_chunks, chunk, 0, unroll=AB_UNROLL)
    y_ref[0] = x_ref[0] + _dot(mix_ref[...], wout_ref[...])

    @pl.when(j == last_j)
    def _():
        knew_ref[0] = kl_ref[...]
        vnew_ref[0] = vl_ref[...]
        gv_ref[0] = gl_ref[...]


def _ab_prompt(x, nm, w_in, qg, kg, sink, tabp, lng, lnb, w_s, bsp, w_out):
    nb, seq, _ = x.shape
    grid = (nb, seq // TQ)
    blk = lambda b, j: (b, j, 0)
    per_b = lambda b, j: (b, 0, 0)
    return pl.pallas_call(
        _ab_prompt_kernel,
        grid=grid,
        in_specs=[
            _SMEM,
            pl.BlockSpec((1, TQ, D_MODEL), blk),
            _full((1, D_MODEL)),
            _resident((D_MODEL, AB_IN)),
            _full((1, A_Q)),
            _full((1, A_KV)),
            _resident((2, N_PAIRS, WINDOW, 4 * WINDOW)),
            _full((1, B_WIDTH)),
            _full((1, B_WIDTH)),
            _resident((B_GROUPS, B_CHUNK, B_CHUNK)),
            _resident((B_CHUNK, B_WIDTH)),
            _resident((AB_MIX, D_MODEL)),
        ],
        out_specs=[
            pl.BlockSpec((1, TQ, D_MODEL), blk),
            pl.BlockSpec((1, WINDOW, A_KV), per_b),
            pl.BlockSpec((1, WINDOW, A_KV), per_b),
            pl.BlockSpec((1, B_CHUNK, B_WIDTH), per_b),
        ],
        out_shape=[
            jax.ShapeDtypeStruct((nb, seq, D_MODEL), F32),
            jax.ShapeDtypeStruct((nb, WINDOW, A_KV), F32),
            jax.ShapeDtypeStruct((nb, WINDOW, A_KV), F32),
            jax.ShapeDtypeStruct((nb, B_CHUNK, B_WIDTH), F32),
        ],
        scratch_shapes=[
            pltpu.VMEM((TQ, AB_IN), F32),
            pltpu.VMEM((TQ, AB_MIX), BF16),
            pltpu.VMEM((WINDOW, A_KV), BF16),
            pltpu.VMEM((WINDOW, A_KV), BF16),
            pltpu.VMEM((WINDOW, A_KV), BF16),
            pltpu.VMEM((WINDOW, A_KV), BF16),
            pltpu.VMEM((B_GROUPS // 2, B_CHUNK, 2 * B_CHUNK), BF16),
            pltpu.VMEM((WINDOW, A_KV), F32),
            pltpu.VMEM((WINDOW, A_KV), F32),
            pltpu.VMEM((B_CHUNK, B_WIDTH), F32),
        ],
        compiler_params=pltpu.CompilerParams(
            dimension_semantics=("arbitrary", "arbitrary"), vmem_limit_bytes=VMEM_LIMIT),
        name="ab_prompt",
    )(sink, x, nm, w_in, qg, kg, tabp, lng, lnb, w_s, bsp, w_out)


FF_TILE = 256


def _ffn_kernel(xp_ref, xs_ref, g_ref, wg_ref, wu_ref, wd_ref, yp_ref, ys_ref,
                wg_s, wu_s, wd_s, h0_s, acc_s, *, n_cast, n_prompt):
    s = pl.program_id(0)

    def gated(h, wg, wu):
        gate = _dot(h, wg)
        return (gate * jax.nn.sigmoid(gate) * _dot(h, wu)).astype(BF16)

    @pl.when(s == 0)
    def _():
        x = xp_ref[...]
        h0_s[...] = _rms(x, g_ref[...]).astype(BF16)
        acc_s[...] = x

    for c in range(n_cast):
        @pl.when(s == c)
        def _(c=c):
            tile = slice(c * FF_TILE, (c + 1) * FF_TILE)
            wg_t, wu_t, wd_t = (r[...].astype(BF16) for r in (wg_ref, wu_ref, wd_ref))
            wg_s[:, tile] = wg_t
            wu_s[:, tile] = wu_t
            wd_s[tile, :] = wd_t
            acc_s[...] += _dot(gated(h0_s[...], wg_t, wu_t), wd_t)

    @pl.when(s == n_cast - 1)
    def _():
        yp_ref[...] = acc_s[...]

    def swiglu(x):
        h = _rms(x, g_ref[...]).astype(BF16)
        return x + _dot(gated(h, wg_s[...], wu_s[...]), wd_s[...])

    @pl.when(jnp.logical_and(s >= n_cast, s < n_cast + n_prompt - 1))
    def _():
        yp_ref[...] = swiglu(xp_ref[...])

    @pl.when(s == n_cast + n_prompt - 1)
    def _():
        ys_ref[...] = swiglu(xs_ref[...])


def _ffn(xp, xs, g, w_gate, w_up, w_down, layer):
    rows, ns = xp.shape[0], xs.shape[0]
    n_cast, n_prompt = D_FF // FF_TILE, rows // TM
    w_tile = lambda s: jnp.minimum(s, n_cast - 1)
    row_blk = lambda s: (jnp.clip(s - (n_cast - 1), 0, n_prompt - 1), 0)
    return pl.pallas_call(
        functools.partial(_ffn_kernel, n_cast=n_cast, n_prompt=n_prompt),
        grid=(n_cast + n_prompt,),
        in_specs=[
            pl.BlockSpec((TM, D_MODEL), row_blk),
            _full((ns, D_MODEL)),
            _full((1, D_MODEL)),
            pl.BlockSpec((None, D_MODEL, FF_TILE), lambda s: (layer, 0, w_tile(s))),
            pl.BlockSpec((None, D_MODEL, FF_TILE), lambda s: (layer, 0, w_tile(s))),
            pl.BlockSpec((None, FF_TILE, D_MODEL), lambda s: (layer, w_tile(s), 0)),
        ],
        out_specs=[pl.BlockSpec((TM, D_MODEL), row_blk), _full((ns, D_MODEL))],
        out_shape=[jax.ShapeDtypeStruct((rows, D_MODEL), F32), jax.ShapeDtypeStruct((ns, D_MODEL), F32)],
        scratch_shapes=[
            pltpu.VMEM((D_MODEL, D_FF), BF16),
            pltpu.VMEM((D_MODEL, D_FF), BF16),
            pltpu.VMEM((D_FF, D_MODEL), BF16),
            pltpu.VMEM((TM, D_MODEL), BF16),
            pltpu.VMEM((TM, D_MODEL), F32),
        ],
        compiler_params=pltpu.CompilerParams(
            dimension_semantics=("arbitrary",), vmem_limit_bytes=VMEM_LIMIT),
        name="ffn",
    )(xp, xs, g, w_gate, w_up, w_down)


def _lower_bound(clb):
    m = jnp.max(clb, axis=0, keepdims=True)
    e = jnp.exp(clb - m)
    sm = e / jnp.sum(e, axis=0, keepdims=True)
    return (sm[0:1] + sm[1:2]) - sm[0:1]


def _split3(x):
    hi = x.astype(BF16)
    r = x - hi.astype(F32)
    mid = r.astype(BF16)
    lo = (r - mid.astype(F32)).astype(BF16)
    return hi, mid, lo


def _neg_abs(x):
    return lax.bitcast_convert_type(
        lax.bitcast_convert_type(x, jnp.uint32) | jnp.uint32(0x80000000), F32)


def _pair_level_table():
    t = np.arange(CHUNK)[:, None]
    s = np.arange(CHUNK)[None, :]
    lev = np.floor(np.log2(np.maximum(t ^ s, 1))).astype(np.int32)
    lev = np.where(t == s, -1, lev)
    return np.where(s > t, -2, lev).astype(np.int32)


def _level_operand(p, q, kk, f, b2):
    m = 2 ** p
    if m < VREG_ROWS:
        shape3 = (CHUNK // VREG_ROWS, VREG_ROWS, q.shape[1])
        sub = lax.broadcasted_iota(jnp.int32, (1, VREG_ROWS, q.shape[1]), 1)
        upper = ((sub >> p) & 1) == 1
        q3, k3 = q.reshape(shape3), kk.reshape(shape3)
        if p == 0:
            y = jnp.where(upper, q3 * f.reshape(shape3), k3)
        else:
            b3 = b2.reshape(shape3)
            be = b3[:, m - 1:m, :]
            for k in range(1, VREG_ROWS // (2 * m)):
                be = jnp.where(sub >= 2 * m * k, b3[:, 2 * m * k + m - 1:2 * m * k + m, :], be)
            y = jnp.where(upper, q3, k3) * jnp.exp2(_neg_abs(b3 - be))
        return y.reshape(q.shape).astype(BF16)
    parts = []
    for k in range(CHUNK // (2 * m)):
        lo = slice(2 * m * k, 2 * m * k + m)
        up = slice(2 * m * k + m, 2 * m * (k + 1))
        be = b2[2 * m * k + m - 1:2 * m * k + m, :]
        parts.append(kk[lo] * jnp.exp2(be - b2[lo]))
        parts.append(q[up] * jnp.exp2(b2[up] - be))
    return jnp.concatenate(parts, axis=0).astype(BF16)


def _merge_level(p, att, pm, lev):
    m = 2 ** p
    if m < VREG_ROWS:
        return jnp.where(lev == p, pm, att)
    col = lax.broadcasted_iota(jnp.int32, (1, CHUNK), 1)
    parts = []
    for k in range(CHUNK // (2 * m)):
        lo = slice(2 * m * k, 2 * m * k + m)
        up = slice(2 * m * k + m, 2 * m * (k + 1))
        parts.append(att[lo])
        parts.append(jnp.where((col >= 2 * m * k) & (col < 2 * m * k + m), pm[up], att[up]))
    return jnp.concatenate(parts, axis=0)


def _hgrn_prompt_kernel(x_ref, nm_ref, win_ref, clb_ref, on_ref, wout_ref, lev_ref,
                        y_ref, st_ref,
                        z_ref, o_ref, stt_ref, k_ref):
    j = pl.program_id(1)
    last_j = pl.num_programs(1) - 1
    n_chunks = TQ // CHUNK
    n_levels = int(math.log2(CHUNK))

    @pl.when(j == 0)
    def _():
        stt_ref[...] = jnp.zeros_like(stt_ref)

    h = _rms(x_ref[0], nm_ref[...]).astype(BF16)
    z_ref[...] = _dot(h, win_ref[...])
    lb = _lower_bound(clb_ref[...])

    row = lax.broadcasted_iota(jnp.int32, (CHUNK, CHUNK), 0)
    col = lax.broadcasted_iota(jnp.int32, (CHUNK, CHUNK), 1)
    ltri = (row >= col).astype(BF16)

    def chunk_rows(c):
        return pl.ds(pl.multiple_of(c * CHUNK, CHUNK), CHUNK)

    def prefix(g, worst):
        rows = [chunk_rows(g * PREFIX_GROUP + i) for i in range(PREFIX_GROUP)]
        gates = [z_ref[r, C_F:2 * C_F] for r in rows]
        for r, gate in zip(rows, gates):
            f_all = lb + (1.0 - lb) * jax.nn.sigmoid(gate)
            k_ref[r, :] = 1.0 - f_all
            hi, mid, lo = _split3(jnp.log2(f_all))
            b2 = (_dot(ltri, hi) + _dot(ltri, mid)) + _dot(ltri, lo)
            z_ref[r, C_F:2 * C_F] = b2
            b_mid = b2[CHUNK // 2 - 1:CHUNK // 2, :]
            b_last = b2[CHUNK - 1:CHUNK, :]
            worst = jnp.maximum(worst, jnp.maximum(-b_mid, b_mid - b_last))
        return worst

    worst = lax.fori_loop(0, n_chunks // PREFIX_GROUP, prefix, jnp.zeros((1, C_F), F32))
    bounded = jnp.max(worst) <= SAFE_LOG2_RANGE

    def finish_head(rows, hd, o):
        gt = z_ref[rows, 2 * C_F + C_V + hd * C_VAL_DIM:2 * C_F + C_V + (hd + 1) * C_VAL_DIM]
        o = _rms(o, on_ref[...]) * jax.nn.sigmoid(gt)
        o_ref[rows, hd * C_VAL_DIM:(hd + 1) * C_VAL_DIM] = o.astype(BF16)

    def head_inputs(rows, hd):
        q = z_ref[rows, hd * C_KEY_DIM:(hd + 1) * C_KEY_DIM]
        kk = k_ref[rows, hd * C_KEY_DIM:(hd + 1) * C_KEY_DIM]
        b2 = z_ref[rows, C_F + hd * C_KEY_DIM:C_F + (hd + 1) * C_KEY_DIM]
        ivb = z_ref[rows, 2 * C_F + hd * C_VAL_DIM:2 * C_F + (hd + 1) * C_VAL_DIM].astype(BF16)
        return q, kk, b2, ivb

    def factored_chunk(c, carry):
        rows = chunk_rows(c)
        for hd in range(C_HEADS):
            q, kk, b2, ivb = head_inputs(rows, hd)
            b_mid = b2[CHUNK // 2 - 1:CHUNK // 2, :]
            b_last = b2[CHUNK - 1:CHUNK, :]
            qs = (q * jnp.exp2(b2 - b_mid)).astype(BF16)
            kd = (kk * jnp.exp2(b_mid - b2)).astype(BF16)
            att = jnp.where(row >= col, lax.dot_general(qs, kd, _NT, preferred_element_type=F32), 0.0)
            stt = stt_ref[hd]
            o = lax.dot_general(qs, (stt * jnp.exp2(b_mid)).astype(BF16), _NT,
                                preferred_element_type=F32) + _dot(att.astype(BF16), ivb)
            stt_ref[hd] = stt * jnp.exp2(b_last) + jnp.exp2(b_last - b_mid) * lax.dot_general(
                ivb, kd, _TN, preferred_element_type=F32)
            finish_head(rows, hd, o)
        return carry

    def tree_chunk(c, carry):
        rows = chunk_rows(c)
        lev = lev_ref[...]

        def products(hd):
            q, kk, b2, ivb = head_inputs(rows, hd)
            diag = jnp.sum(q * kk, axis=-1, keepdims=True)
            pms = []
            for p in range(n_levels):
                y = _level_operand(p, q, kk, 1.0 - kk, b2)
                pms.append(lax.dot_general(y, y, _NT, preferred_element_type=F32))
            stt = stt_ref[hd]
            o_prev = lax.dot_general((q * jnp.exp2(b2)).astype(BF16), stt.astype(BF16), _NT,
                                     preferred_element_type=F32)
            b_last = b2[CHUNK - 1:CHUNK, :]
            kd = (kk * jnp.exp2(b_last - b2)).astype(BF16)
            stt_ref[hd] = stt * jnp.exp2(b_last) + lax.dot_general(
                ivb, kd, _TN, preferred_element_type=F32)
            return diag, pms, o_prev, ivb

        def finish(hd, diag, pms, o_prev, ivb):
            att = jnp.where(lev == -1, diag, 0.0)
            for p in range(n_levels):
                att = _merge_level(p, att, pms[p], lev)
            finish_head(rows, hd, o_prev + _dot(att.astype(BF16), ivb))

        pending = [products(hd) for hd in range(HEAD_SKEW)]
        for hd in range(C_HEADS):
            if hd + HEAD_SKEW < C_HEADS:
                pending.append(products(hd + HEAD_SKEW))
            finish(hd, *pending.pop(0))
        return carry

    @pl.when(bounded)
    def _():
        lax.fori_loop(0, n_chunks, factored_chunk, 0, unroll=FACTORED_UNROLL)

    @pl.when(jnp.logical_not(bounded))
    def _():
        lax.fori_loop(0, n_chunks, tree_chunk, 0)

    y_ref[0] = x_ref[0] + _dot(o_ref[...], wout_ref[...])

    @pl.when(j == last_j)
    def _():
        for hd in range(C_HEADS):
            st_ref[0, hd] = stt_ref[hd].T


def _hgrn_prompt(x, nm, w_in, clb, on, w_out):
    nb, seq, _ = x.shape
    grid = (nb, seq // TQ)
    blk = lambda b, j: (b, j, 0)
    return pl.pallas_call(
        _hgrn_prompt_kernel,
        grid=grid,
        in_specs=[
            pl.BlockSpec((1, TQ, D_MODEL), blk),
            _full((1, D_MODEL)),
            _resident((D_MODEL, C_IN)),
            _full((DEPTH, C_F)),
            _full((1, C_VAL_DIM)),
            _resident((C_V, D_MODEL)),
            _full((CHUNK, CHUNK)),
        ],
        out_specs=[
            pl.BlockSpec((1, TQ, D_MODEL), blk),
            pl.BlockSpec((1, C_HEADS, C_KEY_DIM, C_VAL_DIM), lambda b, j: (b, 0, 0, 0)),
        ],
        out_shape=[
            jax.ShapeDtypeStruct((nb, seq, D_MODEL), F32),
            jax.ShapeDtypeStruct((nb, C_HEADS, C_KEY_DIM, C_VAL_DIM), F32),
        ],
        scratch_shapes=[
            pltpu.VMEM((TQ, C_IN), F32),
            pltpu.VMEM((TQ, C_V), BF16),
            pltpu.VMEM((C_HEADS, C_VAL_DIM, C_KEY_DIM), F32),
            pltpu.VMEM((TQ, C_F), F32),
        ],
        compiler_params=pltpu.CompilerParams(
            dimension_semantics=("arbitrary", "arbitrary"), vmem_limit_bytes=VMEM_LIMIT),
        name="hgrn_prompt",
    )(x, nm, w_in, clb, on, w_out, jnp.asarray(_pair_level_table()))


def _ab_sample_proj_kernel(w00_ref, b0_ref, x_ref, nm_ref, win_ref, qn_ref, kn_ref, lng_ref, lnb_ref,
                           qx_ref, knew_ref, vnew_ref, bm_ref, gv_ref):
    n = x_ref.shape[0]
    h = _rms(x_ref[...], nm_ref[...]).astype(BF16)
    z = _dot(h, win_ref[...])
    zeros = jnp.zeros((n, A_HEAD_DIM), F32)
    for hh in range(A_HEADS):
        qh = _rms(z[:, hh * A_HEAD_DIM:(hh + 1) * A_HEAD_DIM], qn_ref[...]) * ATTN_SCALE
        qx_ref[hh] = jnp.concatenate([qh, zeros] if hh // A_GROUP == 0 else [zeros, qh], axis=-1)
    kparts = []
    for g in range(A_KV_HEADS):
        kparts.append(_rms(z[:, A_Q + g * A_HEAD_DIM:A_Q + (g + 1) * A_HEAD_DIM], kn_ref[...]))
    knew_ref[...] = jnp.concatenate(kparts, axis=-1)
    vnew_ref[...] = z[:, A_Q + A_KV:A_Q + 2 * A_KV]

    u = _gelu(z[:, A_Q + 2 * A_KV:A_Q + 2 * A_KV + B_WIDTH])
    vln = _layernorm(_gelu(z[:, A_Q + 2 * A_KV + B_WIDTH:AB_IN]), lng_ref[...], lnb_ref[...])
    grp = lax.broadcasted_iota(jnp.int32, (1, B_WIDTH), 1) // B_GROUP_DIM
    srow = jnp.zeros((1, B_WIDTH), F32)
    brow = jnp.zeros((1, B_WIDTH), F32)
    for g in range(B_GROUPS):
        srow = jnp.where(grp == g, w00_ref[g], srow)
        brow = jnp.where(grp == g, b0_ref[g], brow)
    bm_ref[...] = u * (vln * srow + brow)
    gv_ref[...] = vln


def _ab_sample_proj(x, nm, w_in, qn, kn, lng, lnb, w00, b0):
    n = x.shape[0]
    return pl.pallas_call(
        _ab_sample_proj_kernel,
        in_specs=[_SMEM, _SMEM] + [pl.BlockSpec(memory_space=pltpu.VMEM)] * 7,
        out_shape=[
            jax.ShapeDtypeStruct((A_HEADS, n, A_KV), F32),
            jax.ShapeDtypeStruct((n, A_KV), F32),
            jax.ShapeDtypeStruct((n, A_KV), F32),
            jax.ShapeDtypeStruct((n, B_WIDTH), F32),
            jax.ShapeDtypeStruct((n, B_WIDTH), F32),
        ],
        compiler_params=pltpu.CompilerParams(vmem_limit_bytes=VMEM_LIMIT),
        name="ab_sample_proj",
    )(w00, b0, x, nm, w_in, qn, kn, lng, lnb)


def _ab_sample_attn_kernel(ck_ref, cv_ref, qx_ref, kn_ref, vn_ref, sb_ref, sink_ref,
                           nk_ref, nv_ref, om_ref):
    wb = ck_ref.shape[1]
    head = lax.broadcasted_iota(jnp.int32, (1, A_HEADS, A_KV), 1)
    lane = lax.broadcasted_iota(jnp.int32, (1, A_HEADS, A_KV), 2)
    own_group = (head // A_GROUP) == (lane // A_HEAD_DIM)
    sink = sink_ref[...][None]
    kc, vc = ck_ref[...], cv_ref[...]
    kn, vn = kn_ref[...], vn_ref[...]
    nk_ref[:, 0:wb - 1, :] = kc[:, 1:wb, :]
    nk_ref[:, wb - 1:wb, :] = kn
    nv_ref[:, 0:wb - 1, :] = vc[:, 1:wb, :]
    nv_ref[:, wb - 1:wb, :] = vn
    q = qx_ref[...]
    s = jnp.einsum('bhd,bkd->bhk', q.astype(BF16), kc.astype(BF16), preferred_element_type=F32)
    s = s + sb_ref[:, 0:wb][None]
    sn = jnp.sum(q * kn, axis=-1, keepdims=True) + sb_ref[:, wb:wb + 1][None]
    m = jnp.maximum(jnp.maximum(jnp.max(s, axis=-1, keepdims=True), sn), sink)
    e = jnp.exp(s - m)
    en = jnp.exp(sn - m)
    r = 1.0 / (jnp.sum(e, axis=-1, keepdims=True) + en + jnp.exp(sink - m))
    o = jnp.einsum('bhk,bkd->bhd', (e * r).astype(BF16), vc.astype(BF16),
                   preferred_element_type=F32) + (en * r) * vn
    om_ref[...] = jnp.where(own_group, o, 0.0)


def _ab_sample_attn(ck, cv, qx, kn, vn, sb, sink):
    n, wb, _ = ck.shape
    blk3 = lambda i: (i, 0, 0)
    return pl.pallas_call(
        _ab_sample_attn_kernel,
        grid=(n // SB,),
        in_specs=[
            pl.BlockSpec((SB, wb, A_KV), blk3),
            pl.BlockSpec((SB, wb, A_KV), blk3),
            pl.BlockSpec((SB, A_HEADS, A_KV), blk3),
            pl.BlockSpec((SB, 1, A_KV), blk3),
            pl.BlockSpec((SB, 1, A_KV), blk3),
            _full((A_HEADS, 2 * WINDOW)),
            _full((A_HEADS, 1)),
        ],
        out_specs=[
            pl.BlockSpec((SB, wb, A_KV), blk3),
            pl.BlockSpec((SB, wb, A_KV), blk3),
            pl.BlockSpec((SB, A_HEADS, A_KV), blk3),
        ],
        out_shape=[
            jax.ShapeDtypeStruct((n, wb, A_KV), F32),
            jax.ShapeDtypeStruct((n, wb, A_KV), F32),
            jax.ShapeDtypeStruct((n, A_HEADS, A_KV), F32),
        ],
        compiler_params=pltpu.CompilerParams(dimension_semantics=("arbitrary",)),
        name="ab_sample_attn",
    )(ck, cv, qx, kn, vn, sb, sink)


def _residual_proj_kernel(x_ref, m_ref, w_ref, y_ref):
    y_ref[...] = x_ref[...] + _dot(m_ref[...].astype(BF16), w_ref[...])


def _residual_proj(x, mix, w):
    return pl.pallas_call(
        _residual_proj_kernel,
        out_shape=jax.ShapeDtypeStruct(x.shape, F32),
        name="residual_proj",
    )(x, mix, w)


def _hgrn_sample_proj_kernel(x_ref, nm_ref, win_ref, clb_ref, q_ref, f_ref, i_ref, sg_ref):
    h = _rms(x_ref[...], nm_ref[...]).astype(BF16)
    z = _dot(h, win_ref[...])
    lb = _lower_bound(clb_ref[...])
    q_ref[...] = z[:, 0:C_F]
    f_ref[...] = lb + (1.0 - lb) * jax.nn.sigmoid(z[:, C_F:2 * C_F])
    i_ref[...] = z[:, 2 * C_F:2 * C_F + C_V]
    sg_ref[...] = jax.nn.sigmoid(z[:, 2 * C_F + C_V:C_IN])


def _hgrn_sample_proj(x, nm, w_in, clb):
    n = x.shape[0]
    return pl.pallas_call(
        _hgrn_sample_proj_kernel,
        out_shape=[jax.ShapeDtypeStruct((n, C_F), F32)] * 2 + [jax.ShapeDtypeStruct((n, C_V), F32)] * 2,
        compiler_params=pltpu.CompilerParams(vmem_limit_bytes=VMEM_LIMIT),
        name="hgrn_sample_proj",
    )(x, nm, w_in, clb)


def _hgrn_sample_state_kernel(s_ref, fc_ref, q_ref, i_ref, sn_ref, o_ref):
    out_rows = []
    for s in range(SB):
        parts = []
        for hd in range(C_HEADS):
            hs = slice(hd * C_VAL_DIM, (hd + 1) * C_VAL_DIM)
            fb = jnp.broadcast_to(fc_ref[0, hd, :, s:s + 1], (C_KEY_DIM, C_VAL_DIM))
            sn = fb * s_ref[s, hd] + (1.0 - fb) * i_ref[s:s + 1, hs]
            sn_ref[s, hd] = sn
            parts.append(_dot(q_ref[s:s + 1, hs].astype(BF16), sn.astype(BF16)))
        out_rows.append(jnp.concatenate(parts, axis=-1))
    o_ref[...] = jnp.concatenate(out_rows, axis=0)


def _hgrn_sample_state(state, fc, q, iv):
    n = state.shape[0]
    sblk = pl.BlockSpec((SB, C_HEADS, C_KEY_DIM, C_VAL_DIM), lambda i: (i, 0, 0, 0))
    cblk = pl.BlockSpec((1, C_HEADS, C_KEY_DIM, SB), lambda i: (i, 0, 0, 0))
    rblk = pl.BlockSpec((SB, C_V), lambda i: (i, 0))
    return pl.pallas_call(
        _hgrn_sample_state_kernel,
        grid=(n // SB,),
        in_specs=[sblk, cblk, rblk, rblk],
        out_specs=[sblk, pl.BlockSpec((SB, C_V), lambda i: (i, 0))],
        out_shape=[
            jax.ShapeDtypeStruct(state.shape, F32),
            jax.ShapeDtypeStruct((n, C_V), F32),
        ],
        compiler_params=pltpu.CompilerParams(
            dimension_semantics=("arbitrary",), vmem_limit_bytes=VMEM_LIMIT),
        name="hgrn_sample_state",
    )(state, fc, q, iv)


def _hgrn_sample_out_kernel(o_ref, sg_ref, on_ref, w_ref, x_ref, y_ref):
    parts = []
    for hd in range(C_HEADS):
        parts.append(_rms(o_ref[:, hd * C_VAL_DIM:(hd + 1) * C_VAL_DIM], on_ref[...]))
    on = jnp.concatenate(parts, axis=-1) * sg_ref[...]
    y_ref[...] = x_ref[...] + _dot(on.astype(BF16), w_ref[...])


def _hgrn_sample_out(o, sg, on, w_out, x):
    return pl.pallas_call(
        _hgrn_sample_out_kernel,
        out_shape=jax.ShapeDtypeStruct(x.shape, F32),
        name="hgrn_sample_out",
    )(o, sg, on, w_out, x)


def _to_columns(a):
    n = a.shape[0]
    return a.reshape(n // SB, SB, C_HEADS, C_KEY_DIM).transpose(0, 2, 3, 1)


def kernel(x_prompt, x_sample, cache_k, cache_v, state_hgrn, norm_mix, norm_ffn, w_in_ab, w_out_ab,
           q_norm, k_norm, attn_sink, rel_bias, gmlp_ln_g, gmlp_ln_b, gmlp_w_s, gmlp_b_s,
           w_in_c, c_lower_bounds, c_out_norm, w_out_c, w_gate, w_up, w_down):
    assert norm_mix.shape[0] == DEPTH == 2 and w_in_ab.shape[0] == 1 and w_in_c.shape[0] == 1
    nb, seq, _ = x_prompt.shape
    ns = x_sample.shape[0]
    assert x_sample.shape[1] == 1 and cache_k.shape[2] == WINDOW

    row = lambda v: v.reshape(1, -1)
    bf = lambda w: w.astype(BF16)
    w_in_ab0, w_out_ab0 = bf(w_in_ab[0]), bf(w_out_ab[0])
    w_in_c0, w_out_c0 = bf(w_in_c[0]), bf(w_out_c[0])
    nm, nf = norm_mix, norm_ffn
    qn, kn = row(q_norm[0]), row(k_norm[0])
    lng, lnb = row(gmlp_ln_g[0]), row(gmlp_ln_b[0])
    sink = attn_sink[0]

    tab, tabp = _bias_table(rel_bias)

    xp, knew_p, vnew_p, gv_p = _ab_prompt(
        x_prompt, row(nm[0]), w_in_ab0, jnp.tile(qn, (1, A_HEADS)), jnp.tile(kn, (1, A_KV_HEADS)),
        sink, tabp, lng, lnb, gmlp_w_s[0], jnp.repeat(gmlp_b_s[0].T, B_GROUP_DIM, axis=1), w_out_ab0)
    xs = x_sample.reshape(ns, D_MODEL)
    qx, knew_s, vnew_s, bm_s, gv_s = _ab_sample_proj(
        xs, row(nm[0]), w_in_ab0, qn, kn, lng, lnb, gmlp_w_s[0, :, 0, 0], gmlp_b_s[0, :, 0])
    sb = jnp.pad(tab[:, WINDOW - 1, WINDOW - 1:], ((0, 0), (0, WINDOW - 1)))
    nk_s, nv_s, om = _ab_sample_attn(
        cache_k[0].reshape(ns, WINDOW, A_KV), cache_v[0].reshape(ns, WINDOW, A_KV),
        qx.transpose(1, 0, 2), knew_s[:, None, :], vnew_s[:, None, :], sb, sink.reshape(A_HEADS, 1))
    om = om.reshape(ns, A_KV_HEADS, A_GROUP, A_KV_HEADS, A_HEAD_DIM)
    a_s = jnp.stack([om[:, g, :, g, :] for g in range(A_KV_HEADS)], axis=1).reshape(ns, A_Q)
    xs = _residual_proj(xs, jnp.concatenate([a_s, bm_s], axis=-1), w_out_ab0)
    xp, xs = _ffn(xp.reshape(nb * seq, D_MODEL), xs, row(nf[0]), w_gate, w_up, w_down, 0)

    xp, st_p = _hgrn_prompt(xp.reshape(nb, seq, D_MODEL), row(nm[1]), w_in_c0, c_lower_bounds,
                            row(c_out_norm[0]), w_out_c0)
    q_s, f_s, i_s, sg_s = _hgrn_sample_proj(xs, row(nm[1]), w_in_c0, c_lower_bounds)
    st_s, o_s = _hgrn_sample_state(state_hgrn[0], _to_columns(f_s), q_s, i_s)
    xs = _hgrn_sample_out(o_s, sg_s, row(c_out_norm[0]), w_out_c0, xs)
    xp, xs = _ffn(xp.reshape(nb * seq, D_MODEL), xs, row(nf[1]), w_gate, w_up, w_down, 1)

    kv5 = lambda a: a.reshape(1, a.shape[0], WINDOW, A_KV_HEADS, A_HEAD_DIM)
    return (xp.reshape(nb, seq, D_MODEL), xs.reshape(ns, 1, D_MODEL),
            kv5(knew_p), kv5(vnew_p), kv5(nk_s), kv5(nv_s),
            gv_p[None], gv_s.reshape(1, ns, 1, B_WIDTH),
            st_p[None], st_s[None])
```

```python
import functools
import math

import jax
import jax.numpy as jnp
import numpy as np
from jax import lax
from jax.experimental import pallas as pl
from jax.experimental.pallas import tpu as pltpu

F32 = jnp.float32
BF16 = jnp.bfloat16

D_MODEL = 1024
DEPTH = 2
A_HEADS = 8
A_KV_HEADS = 2
A_GROUP = A_HEADS // A_KV_HEADS
A_HEAD_DIM = 64
WINDOW = 128
ATTN_SCALE = A_HEAD_DIM ** -0.5
NUM_BUCKETS = 32
MAX_DISTANCE = 128
A_Q = A_HEADS * A_HEAD_DIM
A_KV = A_KV_HEADS * A_HEAD_DIM
B_GROUPS = 8
B_GROUP_DIM = 64
B_WIDTH = B_GROUPS * B_GROUP_DIM
B_CHUNK = 128
AB_IN = A_Q + 2 * A_KV + 2 * B_WIDTH
AB_MIX = A_Q + B_WIDTH
C_HEADS = 8
C_KEY_DIM = 128
C_VAL_DIM = 128
C_F = C_HEADS * C_KEY_DIM
C_V = C_HEADS * C_VAL_DIM
C_IN = 2 * C_F + 2 * C_V
D_FF = 2816
EPS = 1e-6

NEG = -1e30

VMEM_LIMIT = 56 * 1024 * 1024
VREG_ROWS = 8

CHUNK = 128
TQ = 1024
TM = 512
SB = 8
HEAD_SKEW = 2
SAFE_LOG2_RANGE = 64.0
PREFIX_GROUP = 4
FACTORED_UNROLL = 4
AB_UNROLL = 2

_NT = (((1,), (1,)), ((), ()))
_TN = (((0,), (0,)), ((), ()))


def _rms(x, g):
    return x * lax.rsqrt(jnp.mean(x * x, axis=-1, keepdims=True) + EPS) * g


def _gelu(x):
    return 0.5 * x * (1.0 + lax.erf(x * math.sqrt(0.5)))


def _layernorm(x, g, b):
    xc = x - jnp.mean(x, axis=-1, keepdims=True)
    return xc * lax.rsqrt(jnp.mean(xc * xc, axis=-1, keepdims=True) + EPS) * g + b


def _dot(a, b):
    return jnp.dot(a, b, preferred_element_type=F32)


def _full(shape):
    n = len(shape)
    return pl.BlockSpec(shape, lambda *_: (0,) * n)


def _resident(shape):
    n = len(shape)
    return pl.BlockSpec(shape, lambda *_: (0,) * n, pipeline_mode=pl.Buffered(1))


_SMEM = pl.BlockSpec(memory_space=pltpu.SMEM)


def _bias_table_kernel(rel_ref, tab_ref, tabp_ref):
    qi = lax.broadcasted_iota(jnp.int32, (WINDOW, 2 * WINDOW), 0)
    kj = lax.broadcasted_iota(jnp.int32, (WINDOW, 2 * WINDOW), 1)
    dist = qi + WINDOW - kj
    ok = (dist >= 0) & (dist < WINDOW)
    max_exact = NUM_BUCKETS // 2
    d = jnp.maximum(dist, 0)
    dl = jnp.maximum(d, 1).astype(F32)
    v = (jnp.log(dl / max_exact) / math.log(MAX_DISTANCE / max_exact) * (NUM_BUCKETS - max_exact))
    far = d >= max_exact
    hits = []
    for b in range(NUM_BUCKETS):
        if b < max_exact:
            hits.append(d == b)
        elif b < NUM_BUCKETS - 1:
            hits.append(far & (v >= b - max_exact) & (v < b - max_exact + 1))
        else:
            hits.append(far & (v >= b - max_exact))
    for h in range(A_HEADS):
        acc = jnp.zeros((WINDOW, 2 * WINDOW), F32)
        for b in range(NUM_BUCKETS):
            acc = jnp.where(hits[b], rel_ref[b, h], acc)
        t = jnp.where(ok, acc, NEG)
        tab_ref[h] = t
        cols = slice((h % 2) * 2 * WINDOW, (h % 2 + 1) * 2 * WINDOW)
        tabp_ref[0, h // 2, :, cols] = t
        tabp_ref[1, h // 2, :, cols] = jnp.where(kj < WINDOW, NEG, t)


def _bias_table(rel_bias):
    return pl.pallas_call(
        _bias_table_kernel,
        out_shape=[
            jax.ShapeDtypeStruct((A_HEADS, WINDOW, 2 * WINDOW), F32),
            jax.ShapeDtypeStruct((2, A_HEADS // 2, WINDOW, 4 * WINDOW), F32),
        ],
        in_specs=[_SMEM],
        name="bias_table",
    )(rel_bias)


PAIR = 2 * A_HEAD_DIM
N_PAIRS = A_HEADS // 2


def _ab_prompt_kernel(sink_ref, x_ref, nm_ref, win_ref, qg_ref, kg_ref, tabp_ref, lng_ref, lnb_ref,
                      ws_ref, bsp_ref, wout_ref,
                      y_ref, knew_ref, vnew_ref, gv_ref,
                      z_ref, mix_ref, q_ref, k_ref, kr_ref, v_ref, vr_ref, wpair_ref, kl_ref, vl_ref, gl_ref):
    j = pl.program_id(1)
    last_j = pl.num_programs(1) - 1
    n_chunks = TQ // CHUNK

    @pl.when(j == 0)
    def _():
        for ref in (k_ref, kr_ref, v_ref, vr_ref):
            ref[0:CHUNK, :] = jnp.zeros((CHUNK, A_KV), BF16)
        row = lax.broadcasted_iota(jnp.int32, (B_CHUNK, B_CHUNK), 0)
        col = lax.broadcasted_iota(jnp.int32, (B_CHUNK, B_CHUNK), 1)
        for g in range(B_GROUPS):
            wpair_ref[g // 2, :, (g % 2) * B_CHUNK:(g % 2 + 1) * B_CHUNK] = jnp.where(
                row >= col, ws_ref[g], 0.0).astype(BF16)

    h = _rms(x_ref[0], nm_ref[...]).astype(BF16)
    z_ref[...] = _dot(h, win_ref[...])

    lo_half = lax.broadcasted_iota(jnp.int32, (1, PAIR), 1) < A_HEAD_DIM
    r_i = lax.broadcasted_iota(jnp.int32, (PAIR, PAIR), 0) // A_HEAD_DIM
    c_i = lax.broadcasted_iota(jnp.int32, (PAIR, PAIR), 1) // A_HEAD_DIM
    half_mean = jnp.where(r_i == c_i, 1.0 / A_HEAD_DIM, 0.0).astype(BF16)

    def mean_sq_halves(x):
        x2 = x * x
        hi = x2.astype(BF16)
        lo = (x2 - hi.astype(F32)).astype(BF16)
        return _dot(hi, half_mean) + _dot(lo, half_mean)

    def block_diag(top, bot):
        zero = jnp.zeros_like(top)
        return jnp.concatenate([jnp.where(lo_half, top, zero), jnp.where(lo_half, zero, bot)], axis=0)

    kraw = z_ref[:, A_Q:A_Q + A_KV]
    v_all = z_ref[:, A_Q + A_KV:A_Q + 2 * A_KV]
    kn_all = kraw * lax.rsqrt(mean_sq_halves(kraw) + EPS) * kg_ref[...]
    k_ref[CHUNK:, :] = kn_all.astype(BF16)
    kr_ref[CHUNK:, :] = pltpu.roll(kn_all, A_HEAD_DIM, 1).astype(BF16)
    v_ref[CHUNK:, :] = v_all.astype(BF16)
    vr_ref[CHUNK:, :] = pltpu.roll(v_all, A_HEAD_DIM, 1).astype(BF16)
    kl_ref[...] = kn_all[TQ - CHUNK:, :]
    vl_ref[...] = v_all[TQ - CHUNK:, :]
    for i in range(N_PAIRS):
        ps = slice(i * PAIR, (i + 1) * PAIR)
        qraw = z_ref[:, ps]
        qn = qraw * lax.rsqrt(mean_sq_halves(qraw) + EPS) * (qg_ref[:, ps] * ATTN_SCALE)
        q_ref[:, ps] = qn.astype(BF16)

    def chunk(c, carry):
        r0 = pl.multiple_of(c * CHUNK, CHUNK)
        rows = pl.ds(r0, CHUNK)
        first = jnp.where(jnp.logical_and(j == 0, c == 0), 1, 0)

        both = pl.ds(r0, 2 * CHUNK)
        k2, k2r, v2, v2r = k_ref[both, :], kr_ref[both, :], v_ref[both, :], vr_ref[both, :]
        kbd = [block_diag(k2, k2r), block_diag(k2r, k2)]
        vbd = [block_diag(v2, v2r), block_diag(v2r, v2)]

        scores = []
        for i in range(N_PAIRS):
            s = lax.dot_general(q_ref[rows, i * PAIR:(i + 1) * PAIR], kbd[i // (A_GROUP // 2)], _NT,
                                preferred_element_type=F32)
            scores.append(s + tabp_ref[first, i])
        outs = []
        for i in range(N_PAIRS):
            es, rs = [], []
            for hh in range(2):
                sh = scores[i][:, hh * 2 * WINDOW:(hh + 1) * 2 * WINDOW]
                sk = sink_ref[2 * i + hh]
                m = jnp.maximum(jnp.max(sh, axis=-1, keepdims=True), sk)
                e = jnp.exp(sh - m)
                rs.append(1.0 / (jnp.sum(e, axis=-1, keepdims=True) + jnp.exp(sk - m)))
                es.append(e.astype(BF16))
            o = _dot(jnp.concatenate(es, axis=-1), vbd[i // (A_GROUP // 2)])
            outs.append(o * jnp.where(lo_half, rs[0], rs[1]))
        mix_ref[rows, 0:A_Q] = jnp.concatenate(outs, axis=-1).astype(BF16)

        zu = z_ref[rows, A_Q + 2 * A_KV:A_Q + 2 * A_KV + B_WIDTH]
        zv = z_ref[rows, A_Q + 2 * A_KV + B_WIDTH:AB_IN]
        u = _gelu(zu)
        vln = _layernorm(_gelu(zv), lng_ref[...], lnb_ref[...])
        vlb = vln.astype(BF16)
        sparts = []
        for i in range(B_GROUPS // 2):
            vpair = vlb[:, i * PAIR:(i + 1) * PAIR]
            sparts.append(_dot(wpair_ref[i], block_diag(vpair, vpair)))
        bm = u * (jnp.concatenate(sparts, axis=-1) + bsp_ref[...])
        mix_ref[rows, A_Q:AB_MIX] = bm.astype(BF16)

        gl_ref[...] = vln
        return carry

    lax.fori_loop(0, n_chunks, chunk, 0, unroll=AB_UNROLL)
    y_ref[0] = x_ref[0] + _dot(mix_ref[...], wout_ref[...])
    for ref in (k_ref, kr_ref, v_ref, vr_ref):
        ref[0:CHUNK, :] = ref[TQ:TQ + CHUNK, :]

    @pl.when(j == last_j)
    def _():
        knew_ref[0] = kl_ref[...]
        vnew_ref[0] = vl_ref[...]
        gv_ref[0] = gl_ref[...]


def _ab_prompt(x, nm, w_in, qg, kg, sink, tabp, lng, lnb, w_s, bsp, w_out):
    nb, seq, _ = x.shape
    grid = (nb, seq // TQ)
    blk = lambda b, j: (b, j, 0)
    per_b = lambda b, j: (b, 0, 0)
    return pl.pallas_call(
        _ab_prompt_kernel,
        grid=grid,
        in_specs=[
            _SMEM,
            pl.BlockSpec((1, TQ, D_MODEL), blk),
            _full((1, D_MODEL)),
            _resident((D_MODEL, AB_IN)),
            _full((1, A_Q)),
            _full((1, A_KV)),
            _resident((2, N_PAIRS, WINDOW, 4 * WINDOW)),
            _full((1, B_WIDTH)),
            _full((1, B_WIDTH)),
            _resident((B_GROUPS, B_CHUNK, B_CHUNK)),
            _resident((B_CHUNK, B_WIDTH)),
            _resident((AB_MIX, D_MODEL)),
        ],
        out_specs=[
            pl.BlockSpec((1, TQ, D_MODEL), blk),
            pl.BlockSpec((1, WINDOW, A_KV), per_b),
            pl.BlockSpec((1, WINDOW, A_KV), per_b),
            pl.BlockSpec((1, B_CHUNK, B_WIDTH), per_b),
        ],
        out_shape=[
            jax.ShapeDtypeStruct((nb, seq, D_MODEL), F32),
            jax.ShapeDtypeStruct((nb, WINDOW, A_KV), F32),
            jax.ShapeDtypeStruct((nb, WINDOW, A_KV), F32),
            jax.ShapeDtypeStruct((nb, B_CHUNK, B_WIDTH), F32),
        ],
        scratch_shapes=[
            pltpu.VMEM((TQ, AB_IN), F32),
            pltpu.VMEM((TQ, AB_MIX), BF16),
            pltpu.VMEM((TQ, A_Q), BF16),
            pltpu.VMEM((CHUNK + TQ, A_KV), BF16),
            pltpu.VMEM((CHUNK + TQ, A_KV), BF16),
            pltpu.VMEM((CHUNK + TQ, A_KV), BF16),
            pltpu.VMEM((CHUNK + TQ, A_KV), BF16),
            pltpu.VMEM((B_GROUPS // 2, B_CHUNK, 2 * B_CHUNK), BF16),
            pltpu.VMEM((WINDOW, A_KV), F32),
            pltpu.VMEM((WINDOW, A_KV), F32),
            pltpu.VMEM((B_CHUNK, B_WIDTH), F32),
        ],
        compiler_params=pltpu.CompilerParams(
            dimension_semantics=("arbitrary", "arbitrary"), vmem_limit_bytes=VMEM_LIMIT),
        name="ab_prompt",
    )(sink, x, nm, w_in, qg, kg, tabp, lng, lnb, w_s, bsp, w_out)


FF_TILE = 256


def _ffn_kernel(xp_ref, xs_ref, g_ref, wg_ref, wu_ref, wd_ref, yp_ref, ys_ref,
                wg_s, wu_s, wd_s, h0_s, acc_s, *, n_cast, n_prompt):
    s = pl.program_id(0)

    def gated(h, wg, wu):
        gate = _dot(h, wg)
        return (gate * jax.nn.sigmoid(gate) * _dot(h, wu)).astype(BF16)

    @pl.when(s == 0)
    def _():
        x = xp_ref[...]
        h0_s[...] = _rms(x, g_ref[...]).astype(BF16)
        acc_s[...] = x

    for c in range(n_cast):
        @pl.when(s == c)
        def _(c=c):
            tile = slice(c * FF_TILE, (c + 1) * FF_TILE)
            wg_t, wu_t, wd_t = (r[...].astype(BF16) for r in (wg_ref, wu_ref, wd_ref))
            wg_s[:, tile] = wg_t
            wu_s[:, tile] = wu_t
            wd_s[tile, :] = wd_t
            acc_s[...] += _dot(gated(h0_s[...], wg_t, wu_t), wd_t)

    @pl.when(s == n_cast - 1)
    def _():
        yp_ref[...] = acc_s[...]

    def swiglu(x):
        h = _rms(x, g_ref[...]).astype(BF16)
        return x + _dot(gated(h, wg_s[...], wu_s[...]), wd_s[...])

    @pl.when(jnp.logical_and(s >= n_cast, s < n_cast + n_prompt - 1))
    def _():
        yp_ref[...] = swiglu(xp_ref[...])

    @pl.when(s == n_cast + n_prompt - 1)
    def _():
        ys_ref[...] = swiglu(xs_ref[...])


def _ffn(xp, xs, g, w_gate, w_up, w_down, layer):
    rows, ns = xp.shape[0], xs.shape[0]
    n_cast, n_prompt = D_FF // FF_TILE, rows // TM
    w_tile = lambda s: jnp.minimum(s, n_cast - 1)
    row_blk = lambda s: (jnp.clip(s - (n_cast - 1), 0, n_prompt - 1), 0)
    return pl.pallas_call(
        functools.partial(_ffn_kernel, n_cast=n_cast, n_prompt=n_prompt),
        grid=(n_cast + n_prompt,),
        in_specs=[
            pl.BlockSpec((TM, D_MODEL), row_blk),
            _full((ns, D_MODEL)),
            _full((1, D_MODEL)),
            pl.BlockSpec((None, D_MODEL, FF_TILE), lambda s: (layer, 0, w_tile(s))),
            pl.BlockSpec((None, D_MODEL, FF_TILE), lambda s: (layer, 0, w_tile(s))),
            pl.BlockSpec((None, FF_TILE, D_MODEL), lambda s: (layer, w_tile(s), 0)),
        ],
        out_specs=[pl.BlockSpec((TM, D_MODEL), row_blk), _full((ns, D_MODEL))],
        out_shape=[jax.ShapeDtypeStruct((rows, D_MODEL), F32), jax.ShapeDtypeStruct((ns, D_MODEL), F32)],
        scratch_shapes=[
            pltpu.VMEM((D_MODEL, D_FF), BF16),
            pltpu.VMEM((D_MODEL, D_FF), BF16),
            pltpu.VMEM((D_FF, D_MODEL), BF16),
            pltpu.VMEM((TM, D_MODEL), BF16),
            pltpu.VMEM((TM, D_MODEL), F32),
        ],
        compiler_params=pltpu.CompilerParams(
            dimension_semantics=("arbitrary",), vmem_limit_bytes=VMEM_LIMIT),
        name="ffn",
    )(xp, xs, g, w_gate, w_up, w_down)


def _lower_bound(clb):
    m = jnp.max(clb, axis=0, keepdims=True)
    e = jnp.exp(clb - m)
    sm = e / jnp.sum(e, axis=0, keepdims=True)
    return (sm[0:1] + sm[1:2]) - sm[0:1]


def _split3(x):
    hi = x.astype(BF16)
    r = x - hi.astype(F32)
    mid = r.astype(BF16)
    lo = (r - mid.astype(F32)).astype(BF16)
    return hi, mid, lo


def _neg_abs(x):
    return lax.bitcast_convert_type(
        lax.bitcast_convert_type(x, jnp.uint32) | jnp.uint32(0x80000000), F32)


def _pair_level_table():
    t = np.arange(CHUNK)[:, None]
    s = np.arange(CHUNK)[None, :]
    lev = np.floor(np.log2(np.maximum(t ^ s, 1))).astype(np.int32)
    lev = np.where(t == s, -1, lev)
    return np.where(s > t, -2, lev).astype(np.int32)


def _level_operand(p, q, kk, f, b2):
    m = 2 ** p
    if m < VREG_ROWS:
        shape3 = (CHUNK // VREG_ROWS, VREG_ROWS, q.shape[1])
        sub = lax.broadcasted_iota(jnp.int32, (1, VREG_ROWS, q.shape[1]), 1)
        upper = ((sub >> p) & 1) == 1
        q3, k3 = q.reshape(shape3), kk.reshape(shape3)
        if p == 0:
            y = jnp.where(upper, q3 * f.reshape(shape3), k3)
        else:
            b3 = b2.reshape(shape3)
            be = b3[:, m - 1:m, :]
            for k in range(1, VREG_ROWS // (2 * m)):
                be = jnp.where(sub >= 2 * m * k, b3[:, 2 * m * k + m - 1:2 * m * k + m, :], be)
            y = jnp.where(upper, q3, k3) * jnp.exp2(_neg_abs(b3 - be))
        return y.reshape(q.shape).astype(BF16)
    parts = []
    for k in range(CHUNK // (2 * m)):
        lo = slice(2 * m * k, 2 * m * k + m)
        up = slice(2 * m * k + m, 2 * m * (k + 1))
        be = b2[2 * m * k + m - 1:2 * m * k + m, :]
        parts.append(kk[lo] * jnp.exp2(be - b2[lo]))
        parts.append(q[up] * jnp.exp2(b2[up] - be))
    return jnp.concatenate(parts, axis=0).astype(BF16)


def _merge_level(p, att, pm, lev):
    m = 2 ** p
    if m < VREG_ROWS:
        return jnp.where(lev == p, pm, att)
    col = lax.broadcasted_iota(jnp.int32, (1, CHUNK), 1)
    parts = []
    for k in range(CHUNK // (2 * m)):
        lo = slice(2 * m * k, 2 * m * k + m)
        up = slice(2 * m * k + m, 2 * m * (k + 1))
        parts.append(att[lo])
        parts.append(jnp.where((col >= 2 * m * k) & (col < 2 * m * k + m), pm[up], att[up]))
    return jnp.concatenate(parts, axis=0)


def _hgrn_prompt_kernel(x_ref, nm_ref, win_ref, clb_ref, on_ref, wout_ref, lev_ref,
                        y_ref, st_ref,
                        z_ref, o_ref, stt_ref, k_ref):
    j = pl.program_id(1)
    last_j = pl.num_programs(1) - 1
    n_chunks = TQ // CHUNK
    n_levels = int(math.log2(CHUNK))

    @pl.when(j == 0)
    def _():
        stt_ref[...] = jnp.zeros_like(stt_ref)

    h = _rms(x_ref[0], nm_ref[...]).astype(BF16)
    z_ref[...] = _dot(h, win_ref[...])
    lb = _lower_bound(clb_ref[...])

    row = lax.broadcasted_iota(jnp.int32, (CHUNK, CHUNK), 0)
    col = lax.broadcasted_iota(jnp.int32, (CHUNK, CHUNK), 1)
    ltri = (row >= col).astype(BF16)

    def chunk_rows(c):
        return pl.ds(pl.multiple_of(c * CHUNK, CHUNK), CHUNK)

    def prefix(g, worst):
        rows = [chunk_rows(g * PREFIX_GROUP + i) for i in range(PREFIX_GROUP)]
        gates = [z_ref[r, C_F:2 * C_F] for r in rows]
        for r, gate in zip(rows, gates):
            f_all = lb + (1.0 - lb) * jax.nn.sigmoid(gate)
            k_ref[r, :] = 1.0 - f_all
            hi, mid, lo = _split3(jnp.log2(f_all))
            b2 = (_dot(ltri, hi) + _dot(ltri, mid)) + _dot(ltri, lo)
            z_ref[r, C_F:2 * C_F] = b2
            b_mid = b2[CHUNK // 2 - 1:CHUNK // 2, :]
            b_last = b2[CHUNK - 1:CHUNK, :]
            worst = jnp.maximum(worst, jnp.maximum(-b_mid, b_mid - b_last))
        return worst

    worst = lax.fori_loop(0, n_chunks // PREFIX_GROUP, prefix, jnp.zeros((1, C_F), F32))
    bounded = jnp.max(worst) <= SAFE_LOG2_RANGE

    def finish_head(rows, hd, o):
        gt = z_ref[rows, 2 * C_F + C_V + hd * C_VAL_DIM:2 * C_F + C_V + (hd + 1) * C_VAL_DIM]
        o = _rms(o, on_ref[...]) * jax.nn.sigmoid(gt)
        o_ref[rows, hd * C_VAL_DIM:(hd + 1) * C_VAL_DIM] = o.astype(BF16)

    def head_inputs(rows, hd):
        q = z_ref[rows, hd * C_KEY_DIM:(hd + 1) * C_KEY_DIM]
        kk = k_ref[rows, hd * C_KEY_DIM:(hd + 1) * C_KEY_DIM]
        b2 = z_ref[rows, C_F + hd * C_KEY_DIM:C_F + (hd + 1) * C_KEY_DIM]
        ivb = z_ref[rows, 2 * C_F + hd * C_VAL_DIM:2 * C_F + (hd + 1) * C_VAL_DIM].astype(BF16)
        return q, kk, b2, ivb

    def factored_chunk(c, carry):
        rows = chunk_rows(c)
        for hd in range(C_HEADS):
            q, kk, b2, ivb = head_inputs(rows, hd)
            b_mid = b2[CHUNK // 2 - 1:CHUNK // 2, :]
            b_last = b2[CHUNK - 1:CHUNK, :]
            qs = (q * jnp.exp2(b2 - b_mid)).astype(BF16)
            kd = (kk * jnp.exp2(b_mid - b2)).astype(BF16)
            att = jnp.where(row >= col, lax.dot_general(qs, kd, _NT, preferred_element_type=F32), 0.0)
            stt = stt_ref[hd]
            o = lax.dot_general(qs, (stt * jnp.exp2(b_mid)).astype(BF16), _NT,
                                preferred_element_type=F32) + _dot(att.astype(BF16), ivb)
            stt_ref[hd] = stt * jnp.exp2(b_last) + jnp.exp2(b_last - b_mid) * lax.dot_general(
                ivb, kd, _TN, preferred_element_type=F32)
            finish_head(rows, hd, o)
        return carry

    def tree_chunk(c, carry):
        rows = chunk_rows(c)
        lev = lev_ref[...]

        def products(hd):
            q, kk, b2, ivb = head_inputs(rows, hd)
            diag = jnp.sum(q * kk, axis=-1, keepdims=True)
            pms = []
            for p in range(n_levels):
                y = _level_operand(p, q, kk, 1.0 - kk, b2)
                pms.append(lax.dot_general(y, y, _NT, preferred_element_type=F32))
            stt = stt_ref[hd]
            o_prev = lax.dot_general((q * jnp.exp2(b2)).astype(BF16), stt.astype(BF16), _NT,
                                     preferred_element_type=F32)
            b_last = b2[CHUNK - 1:CHUNK, :]
            kd = (kk * jnp.exp2(b_last - b2)).astype(BF16)
            stt_ref[hd] = stt * jnp.exp2(b_last) + lax.dot_general(
                ivb, kd, _TN, preferred_element_type=F32)
            return diag, pms, o_prev, ivb

        def finish(hd, diag, pms, o_prev, ivb):
            att = jnp.where(lev == -1, diag, 0.0)
            for p in range(n_levels):
                att = _merge_level(p, att, pms[p], lev)
            finish_head(rows, hd, o_prev + _dot(att.astype(BF16), ivb))

        pending = [products(hd) for hd in range(HEAD_SKEW)]
        for hd in range(C_HEADS):
            if hd + HEAD_SKEW < C_HEADS:
                pending.append(products(hd + HEAD_SKEW))
            finish(hd, *pending.pop(0))
        return carry

    @pl.when(bounded)
    def _():
        lax.fori_loop(0, n_chunks, factored_chunk, 0, unroll=FACTORED_UNROLL)

    @pl.when(jnp.logical_not(bounded))
    def _():
        lax.fori_loop(0, n_chunks, tree_chunk, 0)

    y_ref[0] = x_ref[0] + _dot(o_ref[...], wout_ref[...])

    @pl.when(j == last_j)
    def _():
        for hd in range(C_HEADS):
            st_ref[0, hd] = stt_ref[hd].T


def _hgrn_prompt(x, nm, w_in, clb, on, w_out):
    nb, seq, _ = x.shape
    grid = (nb, seq // TQ)
    blk = lambda b, j: (b, j, 0)
    return pl.pallas_call(
        _hgrn_prompt_kernel,
        grid=grid,
        in_specs=[
            pl.BlockSpec((1, TQ, D_MODEL), blk),
            _full((1, D_MODEL)),
            _resident((D_MODEL, C_IN)),
            _full((DEPTH, C_F)),
            _full((1, C_VAL_DIM)),
            _resident((C_V, D_MODEL)),
            _full((CHUNK, CHUNK)),
        ],
        out_specs=[
            pl.BlockSpec((1, TQ, D_MODEL), blk),
            pl.BlockSpec((1, C_HEADS, C_KEY_DIM, C_VAL_DIM), lambda b, j: (b, 0, 0, 0)),
        ],
        out_shape=[
            jax.ShapeDtypeStruct((nb, seq, D_MODEL), F32),
            jax.ShapeDtypeStruct((nb, C_HEADS, C_KEY_DIM, C_VAL_DIM), F32),
        ],
        scratch_shapes=[
            pltpu.VMEM((TQ, C_IN), F32),
            pltpu.VMEM((TQ, C_V), BF16),
            pltpu.VMEM((C_HEADS, C_VAL_DIM, C_KEY_DIM), F32),
            pltpu.VMEM((TQ, C_F), F32),
        ],
        compiler_params=pltpu.CompilerParams(
            dimension_semantics=("arbitrary", "arbitrary"), vmem_limit_bytes=VMEM_LIMIT),
        name="hgrn_prompt",
    )(x, nm, w_in, clb, on, w_out, jnp.asarray(_pair_level_table()))


def _ab_sample_proj_kernel(w00_ref, b0_ref, x_ref, nm_ref, win_ref, qn_ref, kn_ref, lng_ref, lnb_ref,
                           qx_ref, knew_ref, vnew_ref, bm_ref, gv_ref):
    n = x_ref.shape[0]
    h = _rms(x_ref[...], nm_ref[...]).astype(BF16)
    z = _dot(h, win_ref[...])
    zeros = jnp.zeros((n, A_HEAD_DIM), F32)
    for hh in range(A_HEADS):
        qh = _rms(z[:, hh * A_HEAD_DIM:(hh + 1) * A_HEAD_DIM], qn_ref[...]) * ATTN_SCALE
        qx_ref[hh] = jnp.concatenate([qh, zeros] if hh // A_GROUP == 0 else [zeros, qh], axis=-1)
    kparts = []
    for g in range(A_KV_HEADS):
        kparts.append(_rms(z[:, A_Q + g * A_HEAD_DIM:A_Q + (g + 1) * A_HEAD_DIM], kn_ref[...]))
    knew_ref[...] = jnp.concatenate(kparts, axis=-1)
    vnew_ref[...] = z[:, A_Q + A_KV:A_Q + 2 * A_KV]

    u = _gelu(z[:, A_Q + 2 * A_KV:A_Q + 2 * A_KV + B_WIDTH])
    vln = _layernorm(_gelu(z[:, A_Q + 2 * A_KV + B_WIDTH:AB_IN]), lng_ref[...], lnb_ref[...])
    grp = lax.broadcasted_iota(jnp.int32, (1, B_WIDTH), 1) // B_GROUP_DIM
    srow = jnp.zeros((1, B_WIDTH), F32)
    brow = jnp.zeros((1, B_WIDTH), F32)
    for g in range(B_GROUPS):
        srow = jnp.where(grp == g, w00_ref[g], srow)
        brow = jnp.where(grp == g, b0_ref[g], brow)
    bm_ref[...] = u * (vln * srow + brow)
    gv_ref[...] = vln


def _ab_sample_proj(x, nm, w_in, qn, kn, lng, lnb, w00, b0):
    n = x.shape[0]
    return pl.pallas_call(
        _ab_sample_proj_kernel,
        in_specs=[_SMEM, _SMEM] + [pl.BlockSpec(memory_space=pltpu.VMEM)] * 7,
        out_shape=[
            jax.ShapeDtypeStruct((A_HEADS, n, A_KV), F32),
            jax.ShapeDtypeStruct((n, A_KV), F32),
            jax.ShapeDtypeStruct((n, A_KV), F32),
            jax.ShapeDtypeStruct((n, B_WIDTH), F32),
            jax.ShapeDtypeStruct((n, B_WIDTH), F32),
        ],
        compiler_params=pltpu.CompilerParams(vmem_limit_bytes=VMEM_LIMIT),
        name="ab_sample_proj",
    )(w00, b0, x, nm, w_in, qn, kn, lng, lnb)


def _ab_sample_attn_kernel(ck_ref, cv_ref, qx_ref, kn_ref, vn_ref, sb_ref, sink_ref,
                           nk_ref, nv_ref, om_ref):
    wb = ck_ref.shape[1]
    head = lax.broadcasted_iota(jnp.int32, (1, A_HEADS, A_KV), 1)
    lane = lax.broadcasted_iota(jnp.int32, (1, A_HEADS, A_KV), 2)
    own_group = (head // A_GROUP) == (lane // A_HEAD_DIM)
    sink = sink_ref[...][None]
    kc, vc = ck_ref[...], cv_ref[...]
    kn, vn = kn_ref[...], vn_ref[...]
    nk_ref[:, 0:wb - 1, :] = kc[:, 1:wb, :]
    nk_ref[:, wb - 1:wb, :] = kn
    nv_ref[:, 0:wb - 1, :] = vc[:, 1:wb, :]
    nv_ref[:, wb - 1:wb, :] = vn
    q = qx_ref[...]
    s = jnp.einsum('bhd,bkd->bhk', q.astype(BF16), kc.astype(BF16), preferred_element_type=F32)
    s = s + sb_ref[:, 0:wb][None]
    sn = jnp.sum(q * kn, axis=-1, keepdims=True) + sb_ref[:, wb:wb + 1][None]
    m = jnp.maximum(jnp.maximum(jnp.max(s, axis=-1, keepdims=True), sn), sink)
    e = jnp.exp(s - m)
    en = jnp.exp(sn - m)
    r = 1.0 / (jnp.sum(e, axis=-1, keepdims=True) + en + jnp.exp(sink - m))
    o = jnp.einsum('bhk,bkd->bhd', (e * r).astype(BF16), vc.astype(BF16),
                   preferred_element_type=F32) + (en * r) * vn
    om_ref[...] = jnp.where(own_group, o, 0.0)


def _ab_sample_attn(ck, cv, qx, kn, vn, sb, sink):
    n, wb, _ = ck.shape
    blk3 = lambda i: (i, 0, 0)
    return pl.pallas_call(
        _ab_sample_attn_kernel,
        grid=(n // SB,),
        in_specs=[
            pl.BlockSpec((SB, wb, A_KV), blk3),
            pl.BlockSpec((SB, wb, A_KV), blk3),
            pl.BlockSpec((SB, A_HEADS, A_KV), blk3),
            pl.BlockSpec((SB, 1, A_KV), blk3),
            pl.BlockSpec((SB, 1, A_KV), blk3),
            _full((A_HEADS, 2 * WINDOW)),
            _full((A_HEADS, 1)),
        ],
        out_specs=[
            pl.BlockSpec((SB, wb, A_KV), blk3),
            pl.BlockSpec((SB, wb, A_KV), blk3),
            pl.BlockSpec((SB, A_HEADS, A_KV), blk3),
        ],
        out_shape=[
            jax.ShapeDtypeStruct((n, wb, A_KV), F32),
            jax.ShapeDtypeStruct((n, wb, A_KV), F32),
            jax.ShapeDtypeStruct((n, A_HEADS, A_KV), F32),
        ],
        compiler_params=pltpu.CompilerParams(dimension_semantics=("arbitrary",)),
        name="ab_sample_attn",
    )(ck, cv, qx, kn, vn, sb, sink)


def _residual_proj_kernel(x_ref, m_ref, w_ref, y_ref):
    y_ref[...] = x_ref[...] + _dot(m_ref[...].astype(BF16), w_ref[...])


def _residual_proj(x, mix, w):
    return pl.pallas_call(
        _residual_proj_kernel,
        out_shape=jax.ShapeDtypeStruct(x.shape, F32),
        name="residual_proj",
    )(x, mix, w)


def _hgrn_sample_proj_kernel(x_ref, nm_ref, win_ref, clb_ref, q_ref, f_ref, i_ref, sg_ref):
    h = _rms(x_ref[...], nm_ref[...]).astype(BF16)
    z = _dot(h, win_ref[...])
    lb = _lower_bound(clb_ref[...])
    q_ref[...] = z[:, 0:C_F]
    f_ref[...] = lb + (1.0 - lb) * jax.nn.sigmoid(z[:, C_F:2 * C_F])
    i_ref[...] = z[:, 2 * C_F:2 * C_F + C_V]
    sg_ref[...] = jax.nn.sigmoid(z[:, 2 * C_F + C_V:C_IN])


def _hgrn_sample_proj(x, nm, w_in, clb):
    n = x.shape[0]
    return pl.pallas_call(
        _hgrn_sample_proj_kernel,
        out_shape=[jax.ShapeDtypeStruct((n, C_F), F32)] * 2 + [jax.ShapeDtypeStruct((n, C_V), F32)] * 2,
        compiler_params=pltpu.CompilerParams(vmem_limit_bytes=VMEM_LIMIT),
        name="hgrn_sample_proj",
    )(x, nm, w_in, clb)


def _hgrn_sample_state_kernel(s_ref, fc_ref, q_ref, i_ref, sn_ref, o_ref):
    out_rows = []
    for s in range(SB):
        parts = []
        for hd in range(C_HEADS):
            hs = slice(hd * C_VAL_DIM, (hd + 1) * C_VAL_DIM)
            fb = jnp.broadcast_to(fc_ref[0, hd, :, s:s + 1], (C_KEY_DIM, C_VAL_DIM))
            sn = fb * s_ref[s, hd] + (1.0 - fb) * i_ref[s:s + 1, hs]
            sn_ref[s, hd] = sn
            parts.append(_dot(q_ref[s:s + 1, hs].astype(BF16), sn.astype(BF16)))
        out_rows.append(jnp.concatenate(parts, axis=-1))
    o_ref[...] = jnp.concatenate(out_rows, axis=0)


def _hgrn_sample_state(state, fc, q, iv):
    n = state.shape[0]
    sblk = pl.BlockSpec((SB, C_HEADS, C_KEY_DIM, C_VAL_DIM), lambda i: (i, 0, 0, 0))
    cblk = pl.BlockSpec((1, C_HEADS, C_KEY_DIM, SB), lambda i: (i, 0, 0, 0))
    rblk = pl.BlockSpec((SB, C_V), lambda i: (i, 0))
    return pl.pallas_call(
        _hgrn_sample_state_kernel,
        grid=(n // SB,),
        in_specs=[sblk, cblk, rblk, rblk],
        out_specs=[sblk, pl.BlockSpec((SB, C_V), lambda i: (i, 0))],
        out_shape=[
            jax.ShapeDtypeStruct(state.shape, F32),
            jax.ShapeDtypeStruct((n, C_V), F32),
        ],
        compiler_params=pltpu.CompilerParams(
            dimension_semantics=("arbitrary",), vmem_limit_bytes=VMEM_LIMIT),
        name="hgrn_sample_state",
    )(state, fc, q, iv)


def _hgrn_sample_out_kernel(o_ref, sg_ref, on_ref, w_ref, x_ref, y_ref):
    parts = []
    for hd in range(C_HEADS):
        parts.append(_rms(o_ref[:, hd * C_VAL_DIM:(hd + 1) * C_VAL_DIM], on_ref[...]))
    on = jnp.concatenate(parts, axis=-1) * sg_ref[...]
    y_ref[...] = x_ref[...] + _dot(on.astype(BF16), w_ref[...])


def _hgrn_sample_out(o, sg, on, w_out, x):
    return pl.pallas_call(
        _hgrn_sample_out_kernel,
        out_shape=jax.ShapeDtypeStruct(x.shape, F32),
        name="hgrn_sample_out",
    )(o, sg, on, w_out, x)


def _to_columns(a):
    n = a.shape[0]
    return a.reshape(n // SB, SB, C_HEADS, C_KEY_DIM).transpose(0, 2, 3, 1)


def kernel(x_prompt, x_sample, cache_k, cache_v, state_hgrn, norm_mix, norm_ffn, w_in_ab, w_out_ab,
           q_norm, k_norm, attn_sink, rel_bias, gmlp_ln_g, gmlp_ln_b, gmlp_w_s, gmlp_b_s,
           w_in_c, c_lower_bounds, c_out_norm, w_out_c, w_gate, w_up, w_down):
    assert norm_mix.shape[0] == DEPTH == 2 and w_in_ab.shape[0] == 1 and w_in_c.shape[0] == 1
    nb, seq, _ = x_prompt.shape
    ns = x_sample.shape[0]
    assert x_sample.shape[1] == 1 and cache_k.shape[2] == WINDOW

    row = lambda v: v.reshape(1, -1)
    bf = lambda w: w.astype(BF16)
    w_in_ab0, w_out_ab0 = bf(w_in_ab[0]), bf(w_out_ab[0])
    w_in_c0, w_out_c0 = bf(w_in_c[0]), bf(w_out_c[0])
    nm, nf = norm_mix, norm_ffn
    qn, kn = row(q_norm[0]), row(k_norm[0])
    lng, lnb = row(gmlp_ln_g[0]), row(gmlp_ln_b[0])
    sink = attn_sink[0]

    tab, tabp = _bias_table(rel_bias)

    xp, knew_p, vnew_p, gv_p = _ab_prompt(
        x_prompt, row(nm[0]), w_in_ab0, jnp.tile(qn, (1, A_HEADS)), jnp.tile(kn, (1, A_KV_HEADS)),
        sink, tabp, lng, lnb, gmlp_w_s[0], jnp.repeat(gmlp_b_s[0].T, B_GROUP_DIM, axis=1), w_out_ab0)
    xs = x_sample.reshape(ns, D_MODEL)
    qx, knew_s, vnew_s, bm_s, gv_s = _ab_sample_proj(
        xs, row(nm[0]), w_in_ab0, qn, kn, lng, lnb, gmlp_w_s[0, :, 0, 0], gmlp_b_s[0, :, 0])
    sb = jnp.pad(tab[:, WINDOW - 1, WINDOW - 1:], ((0, 0), (0, WINDOW - 1)))
    nk_s, nv_s, om = _ab_sample_attn(
        cache_k[0].reshape(ns, WINDOW, A_KV), cache_v[0].reshape(ns, WINDOW, A_KV),
        qx.transpose(1, 0, 2), knew_s[:, None, :], vnew_s[:, None, :], sb, sink.reshape(A_HEADS, 1))
    om = om.reshape(ns, A_KV_HEADS, A_GROUP, A_KV_HEADS, A_HEAD_DIM)
    a_s = jnp.stack([om[:, g, :, g, :] for g in range(A_KV_HEADS)], axis=1).reshape(ns, A_Q)
    xs = _residual_proj(xs, jnp.concatenate([a_s, bm_s], axis=-1), w_out_ab0)
    xp, xs = _ffn(xp.reshape(nb * seq, D_MODEL), xs, row(nf[0]), w_gate, w_up, w_down, 0)

    xp, st_p = _hgrn_prompt(xp.reshape(nb, seq, D_MODEL), row(nm[1]), w_in_c0, c_lower_bounds,
                            row(c_out_norm[0]), w_out_c0)
    q_s, f_s, i_s, sg_s = _hgrn_sample_proj(xs, row(nm[1]), w_in_c0, c_lower_bounds)
    st_s, o_s = _hgrn_sample_state(state_hgrn[0], _to_columns(f_s), q_s, i_s)
    xs = _hgrn_sample_out(o_s, sg_s, row(c_out_norm[0]), w_out_c0, xs)
    xp, xs = _ffn(xp.reshape(nb * seq, D_MODEL), xs, row(nf[1]), w_gate, w_up, w_down, 1)

    kv5 = lambda a: a.reshape(1, a.shape[0], WINDOW, A_KV_HEADS, A_HEAD_DIM)
    return (xp.reshape(nb, seq, D_MODEL), xs.reshape(ns, 1, D_MODEL),
            kv5(knew_p), kv5(vnew_p), kv5(nk_s), kv5(nv_s),
            gv_p[None], gv_s.reshape(1, ns, 1, B_WIDTH),
            st_p[None], st_s[None])
```

```python
import functools
import math

import jax
import jax.numpy as jnp
import numpy as np
from jax import lax
from jax.experimental import pallas as pl
from jax.experimental.pallas import tpu as pltpu

F32 = jnp.float32
BF16 = jnp.bfloat16

D_MODEL = 1024
DEPTH = 2
A_HEADS = 8
A_KV_HEADS = 2
A_GROUP = A_HEADS // A_KV_HEADS
A_HEAD_DIM = 64
WINDOW = 128
ATTN_SCALE = A_HEAD_DIM ** -0.5
NUM_BUCKETS = 32
MAX_DISTANCE = 128
A_Q = A_HEADS * A_HEAD_DIM
A_KV = A_KV_HEADS * A_HEAD_DIM
B_GROUPS = 8
B_GROUP_DIM = 64
B_WIDTH = B_GROUPS * B_GROUP_DIM
B_CHUNK = 128
AB_IN = A_Q + 2 * A_KV + 2 * B_WIDTH
AB_MIX = A_Q + B_WIDTH
C_HEADS = 8
C_KEY_DIM = 128
C_VAL_DIM = 128
C_F = C_HEADS * C_KEY_DIM
C_V = C_HEADS * C_VAL_DIM
C_IN = 2 * C_F + 2 * C_V
D_FF = 2816
EPS = 1e-6

NEG = -1e30

VMEM_LIMIT = 56 * 1024 * 1024
VREG_ROWS = 8

CHUNK = 128
TQ = 1024
TQH = 512
TM = 512
SB = 8
HEAD_SKEW = 2
SAFE_LOG2_RANGE = 64.0
PREFIX_GROUP = 4
FACTORED_UNROLL = 4
AB_UNROLL = 2

_NT = (((1,), (1,)), ((), ()))
_TN = (((0,), (0,)), ((), ()))


def _rms(x, g):
    return x * lax.rsqrt(jnp.mean(x * x, axis=-1, keepdims=True) + EPS) * g


def _gelu(x):
    return 0.5 * x * (1.0 + lax.erf(x * math.sqrt(0.5)))


def _layernorm(x, g, b):
    xc = x - jnp.mean(x, axis=-1, keepdims=True)
    return xc * lax.rsqrt(jnp.mean(xc * xc, axis=-1, keepdims=True) + EPS) * g + b


def _dot(a, b):
    return jnp.dot(a, b, preferred_element_type=F32)


def _full(shape):
    n = len(shape)
    return pl.BlockSpec(shape, lambda *_: (0,) * n)


def _resident(shape):
    n = len(shape)
    return pl.BlockSpec(shape, lambda *_: (0,) * n, pipeline_mode=pl.Buffered(1))


_SMEM = pl.BlockSpec(memory_space=pltpu.SMEM)


def _bias_table_kernel(rel_ref, tab_ref, tabp_ref):
    qi = lax.broadcasted_iota(jnp.int32, (WINDOW, 2 * WINDOW), 0)
    kj = lax.broadcasted_iota(jnp.int32, (WINDOW, 2 * WINDOW), 1)
    dist = qi + WINDOW - kj
    ok = (dist >= 0) & (dist < WINDOW)
    max_exact = NUM_BUCKETS // 2
    d = jnp.maximum(dist, 0)
    dl = jnp.maximum(d, 1).astype(F32)
    v = (jnp.log(dl / max_exact) / math.log(MAX_DISTANCE / max_exact) * (NUM_BUCKETS - max_exact))
    far = d >= max_exact
    hits = []
    for b in range(NUM_BUCKETS):
        if b < max_exact:
            hits.append(d == b)
        elif b < NUM_BUCKETS - 1:
            hits.append(far & (v >= b - max_exact) & (v < b - max_exact + 1))
        else:
            hits.append(far & (v >= b - max_exact))
    for h in range(A_HEADS):
        acc = jnp.zeros((WINDOW, 2 * WINDOW), F32)
        for b in range(NUM_BUCKETS):
            acc = jnp.where(hits[b], rel_ref[b, h], acc)
        t = jnp.where(ok, acc, NEG)
        tab_ref[h] = t
        cols = slice((h % 2) * 2 * WINDOW, (h % 2 + 1) * 2 * WINDOW)
        tabp_ref[0, h // 2, :, cols] = t
        tabp_ref[1, h // 2, :, cols] = jnp.where(kj < WINDOW, NEG, t)


def _bias_table(rel_bias):
    return pl.pallas_call(
        _bias_table_kernel,
        out_shape=[
            jax.ShapeDtypeStruct((A_HEADS, WINDOW, 2 * WINDOW), F32),
            jax.ShapeDtypeStruct((2, A_HEADS // 2, WINDOW, 4 * WINDOW), F32),
        ],
        in_specs=[_SMEM],
        name="bias_table",
    )(rel_bias)


PAIR = 2 * A_HEAD_DIM
N_PAIRS = A_HEADS // 2


def _ab_prompt_kernel(sink_ref, x_ref, nm_ref, win_ref, qg_ref, kg_ref, tabp_ref, lng_ref, lnb_ref,
                      ws_ref, bsp_ref, wout_ref,
                      y_ref, knew_ref, vnew_ref, gv_ref,
                      z_ref, mix_ref, q_ref, k_ref, kr_ref, v_ref, vr_ref, wpair_ref, kl_ref, vl_ref, gl_ref):
    j = pl.program_id(1)
    last_j = pl.num_programs(1) - 1
    n_chunks = TQ // CHUNK

    @pl.when(j == 0)
    def _():
        for ref in (k_ref, kr_ref, v_ref, vr_ref):
            ref[0:CHUNK, :] = jnp.zeros((CHUNK, A_KV), BF16)
        row = lax.broadcasted_iota(jnp.int32, (B_CHUNK, B_CHUNK), 0)
        col = lax.broadcasted_iota(jnp.int32, (B_CHUNK, B_CHUNK), 1)
        for g in range(B_GROUPS):
            wpair_ref[g // 2, :, (g % 2) * B_CHUNK:(g % 2 + 1) * B_CHUNK] = jnp.where(
                row >= col, ws_ref[g], 0.0).astype(BF16)

    h = _rms(x_ref[0], nm_ref[...]).astype(BF16)
    z_ref[...] = _dot(h, win_ref[...])

    lo_half = lax.broadcasted_iota(jnp.int32, (1, PAIR), 1) < A_HEAD_DIM
    r_i = lax.broadcasted_iota(jnp.int32, (PAIR, PAIR), 0) // A_HEAD_DIM
    c_i = lax.broadcasted_iota(jnp.int32, (PAIR, PAIR), 1) // A_HEAD_DIM
    half_mean = jnp.where(r_i == c_i, 1.0 / A_HEAD_DIM, 0.0).astype(BF16)

    def mean_sq_halves(x):
        x2 = x * x
        hi = x2.astype(BF16)
        lo = (x2 - hi.astype(F32)).astype(BF16)
        return _dot(hi, half_mean) + _dot(lo, half_mean)

    def block_diag(top, bot):
        zero = jnp.zeros_like(top)
        return jnp.concatenate([jnp.where(lo_half, top, zero), jnp.where(lo_half, zero, bot)], axis=0)

    kraw = z_ref[:, A_Q:A_Q + A_KV]
    v_all = z_ref[:, A_Q + A_KV:A_Q + 2 * A_KV]
    kn_all = kraw * lax.rsqrt(mean_sq_halves(kraw) + EPS) * kg_ref[...]
    k_ref[CHUNK:, :] = kn_all.astype(BF16)
    kr_ref[CHUNK:, :] = pltpu.roll(kn_all, A_HEAD_DIM, 1).astype(BF16)
    v_ref[CHUNK:, :] = v_all.astype(BF16)
    vr_ref[CHUNK:, :] = pltpu.roll(v_all, A_HEAD_DIM, 1).astype(BF16)
    kl_ref[...] = kn_all[TQ - CHUNK:, :]
    vl_ref[...] = v_all[TQ - CHUNK:, :]
    for i in range(N_PAIRS):
        ps = slice(i * PAIR, (i + 1) * PAIR)
        qraw = z_ref[:, ps]
        qn = qraw * lax.rsqrt(mean_sq_halves(qraw) + EPS) * (qg_ref[:, ps] * ATTN_SCALE)
        q_ref[:, ps] = qn.astype(BF16)

    def chunk(c, carry):
        r0 = pl.multiple_of(c * CHUNK, CHUNK)
        rows = pl.ds(r0, CHUNK)
        first = jnp.where(jnp.logical_and(j == 0, c == 0), 1, 0)

        both = pl.ds(r0, 2 * CHUNK)
        k2, k2r, v2, v2r = k_ref[both, :], kr_ref[both, :], v_ref[both, :], vr_ref[both, :]
        kbd = [block_diag(k2, k2r), block_diag(k2r, k2)]
        vbd = [block_diag(v2, v2r), block_diag(v2r, v2)]

        scores = []
        for i in range(N_PAIRS):
            s = lax.dot_general(q_ref[rows, i * PAIR:(i + 1) * PAIR], kbd[i // (A_GROUP // 2)], _NT,
                                preferred_element_type=F32)
            scores.append(s + tabp_ref[first, i])
        outs = []
        for i in range(N_PAIRS):
            es, rs = [], []
            for hh in range(2):
                sh = scores[i][:, hh * 2 * WINDOW:(hh + 1) * 2 * WINDOW]
                sk = sink_ref[2 * i + hh]
                m = jnp.maximum(jnp.max(sh, axis=-1, keepdims=True), sk)
                e = jnp.exp(sh - m)
                rs.append(1.0 / (jnp.sum(e, axis=-1, keepdims=True) + jnp.exp(sk - m)))
                es.append(e.astype(BF16))
            o = _dot(jnp.concatenate(es, axis=-1), vbd[i // (A_GROUP // 2)])
            outs.append(o * jnp.where(lo_half, rs[0], rs[1]))
        mix_ref[rows, 0:A_Q] = jnp.concatenate(outs, axis=-1).astype(BF16)

        zu = z_ref[rows, A_Q + 2 * A_KV:A_Q + 2 * A_KV + B_WIDTH]
        zv = z_ref[rows, A_Q + 2 * A_KV + B_WIDTH:AB_IN]
        u = _gelu(zu)
        vln = _layernorm(_gelu(zv), lng_ref[...], lnb_ref[...])
        vlb = vln.astype(BF16)
        sparts = []
        for i in range(B_GROUPS // 2):
            vpair = vlb[:, i * PAIR:(i + 1) * PAIR]
            sparts.append(_dot(wpair_ref[i], block_diag(vpair, vpair)))
        bm = u * (jnp.concatenate(sparts, axis=-1) + bsp_ref[...])
        mix_ref[rows, A_Q:AB_MIX] = bm.astype(BF16)

        gl_ref[...] = vln
        return carry

    lax.fori_loop(0, n_chunks, chunk, 0, unroll=AB_UNROLL)
    y_ref[0] = x_ref[0] + _dot(mix_ref[...], wout_ref[...])
    for ref in (k_ref, kr_ref, v_ref, vr_ref):
        ref[0:CHUNK, :] = ref[TQ:TQ + CHUNK, :]

    @pl.when(j == last_j)
    def _():
        knew_ref[0] = kl_ref[...]
        vnew_ref[0] = vl_ref[...]
        gv_ref[0] = gl_ref[...]


def _ab_prompt(x, nm, w_in, qg, kg, sink, tabp, lng, lnb, w_s, bsp, w_out):
    nb, seq, _ = x.shape
    grid = (nb, seq // TQ)
    blk = lambda b, j: (b, j, 0)
    per_b = lambda b, j: (b, 0, 0)
    return pl.pallas_call(
        _ab_prompt_kernel,
        grid=grid,
        in_specs=[
            _SMEM,
            pl.BlockSpec((1, TQ, D_MODEL), blk),
            _full((1, D_MODEL)),
            _resident((D_MODEL, AB_IN)),
            _full((1, A_Q)),
            _full((1, A_KV)),
            _resident((2, N_PAIRS, WINDOW, 4 * WINDOW)),
            _full((1, B_WIDTH)),
            _full((1, B_WIDTH)),
            _resident((B_GROUPS, B_CHUNK, B_CHUNK)),
            _resident((B_CHUNK, B_WIDTH)),
            _resident((AB_MIX, D_MODEL)),
        ],
        out_specs=[
            pl.BlockSpec((1, TQ, D_MODEL), blk),
            pl.BlockSpec((1, WINDOW, A_KV), per_b),
            pl.BlockSpec((1, WINDOW, A_KV), per_b),
            pl.BlockSpec((1, B_CHUNK, B_WIDTH), per_b),
        ],
        out_shape=[
            jax.ShapeDtypeStruct((nb, seq, D_MODEL), F32),
            jax.ShapeDtypeStruct((nb, WINDOW, A_KV), F32),
            jax.ShapeDtypeStruct((nb, WINDOW, A_KV), F32),
            jax.ShapeDtypeStruct((nb, B_CHUNK, B_WIDTH), F32),
        ],
        scratch_shapes=[
            pltpu.VMEM((TQ, AB_IN), F32),
            pltpu.VMEM((TQ, AB_MIX), BF16),
            pltpu.VMEM((TQ, A_Q), BF16),
            pltpu.VMEM((CHUNK + TQ, A_KV), BF16),
            pltpu.VMEM((CHUNK + TQ, A_KV), BF16),
            pltpu.VMEM((CHUNK + TQ, A_KV), BF16),
            pltpu.VMEM((CHUNK + TQ, A_KV), BF16),
            pltpu.VMEM((B_GROUPS // 2, B_CHUNK, 2 * B_CHUNK), BF16),
            pltpu.VMEM((WINDOW, A_KV), F32),
            pltpu.VMEM((WINDOW, A_KV), F32),
            pltpu.VMEM((B_CHUNK, B_WIDTH), F32),
        ],
        compiler_params=pltpu.CompilerParams(
            dimension_semantics=("arbitrary", "arbitrary"), vmem_limit_bytes=VMEM_LIMIT),
        name="ab_prompt",
    )(sink, x, nm, w_in, qg, kg, tabp, lng, lnb, w_s, bsp, w_out)


FF_TILE = 256


def _ffn_kernel(xp_ref, xs_ref, g_ref, wg_ref, wu_ref, wd_ref, yp_ref, ys_ref,
                wg_s, wu_s, wd_s, h0_s, acc_s, *, n_cast, n_prompt):
    s = pl.program_id(0)

    def gated(h, wg, wu):
        gate = _dot(h, wg)
        return (gate * jax.nn.sigmoid(gate) * _dot(h, wu)).astype(BF16)

    @pl.when(s == 0)
    def _():
        x = xp_ref[...]
        h0_s[...] = _rms(x, g_ref[...]).astype(BF16)
        acc_s[...] = x

    for c in range(n_cast):
        @pl.when(s == c)
        def _(c=c):
            tile = slice(c * FF_TILE, (c + 1) * FF_TILE)
            wg_t, wu_t, wd_t = (r[...].astype(BF16) for r in (wg_ref, wu_ref, wd_ref))
            wg_s[:, tile] = wg_t
            wu_s[:, tile] = wu_t
            wd_s[tile, :] = wd_t
            acc_s[...] += _dot(gated(h0_s[...], wg_t, wu_t), wd_t)

    @pl.when(s == n_cast - 1)
    def _():
        yp_ref[...] = acc_s[...]

    def swiglu(x):
        h = _rms(x, g_ref[...]).astype(BF16)
        return x + _dot(gated(h, wg_s[...], wu_s[...]), wd_s[...])

    @pl.when(jnp.logical_and(s >= n_cast, s < n_cast + n_prompt - 1))
    def _():
        yp_ref[...] = swiglu(xp_ref[...])

    @pl.when(s == n_cast + n_prompt - 1)
    def _():
        ys_ref[...] = swiglu(xs_ref[...])


def _ffn(xp, xs, g, w_gate, w_up, w_down, layer):
    rows, ns = xp.shape[0], xs.shape[0]
    n_cast, n_prompt = D_FF // FF_TILE, rows // TM
    w_tile = lambda s: jnp.minimum(s, n_cast - 1)
    row_blk = lambda s: (jnp.clip(s - (n_cast - 1), 0, n_prompt - 1), 0)
    return pl.pallas_call(
        functools.partial(_ffn_kernel, n_cast=n_cast, n_prompt=n_prompt),
        grid=(n_cast + n_prompt,),
        in_specs=[
            pl.BlockSpec((TM, D_MODEL), row_blk),
            _full((ns, D_MODEL)),
            _full((1, D_MODEL)),
            pl.BlockSpec((None, D_MODEL, FF_TILE), lambda s: (layer, 0, w_tile(s))),
            pl.BlockSpec((None, D_MODEL, FF_TILE), lambda s: (layer, 0, w_tile(s))),
            pl.BlockSpec((None, FF_TILE, D_MODEL), lambda s: (layer, w_tile(s), 0)),
        ],
        out_specs=[pl.BlockSpec((TM, D_MODEL), row_blk), _full((ns, D_MODEL))],
        out_shape=[jax.ShapeDtypeStruct((rows, D_MODEL), F32), jax.ShapeDtypeStruct((ns, D_MODEL), F32)],
        scratch_shapes=[
            pltpu.VMEM((D_MODEL, D_FF), BF16),
            pltpu.VMEM((D_MODEL, D_FF), BF16),
            pltpu.VMEM((D_FF, D_MODEL), BF16),
            pltpu.VMEM((TM, D_MODEL), BF16),
            pltpu.VMEM((TM, D_MODEL), F32),
        ],
        compiler_params=pltpu.CompilerParams(
            dimension_semantics=("arbitrary",), vmem_limit_bytes=VMEM_LIMIT),
        name="ffn",
    )(xp, xs, g, w_gate, w_up, w_down)


def _lower_bound(clb):
    m = jnp.max(clb, axis=0, keepdims=True)
    e = jnp.exp(clb - m)
    sm = e / jnp.sum(e, axis=0, keepdims=True)
    return (sm[0:1] + sm[1:2]) - sm[0:1]


def _split3(x):
    hi = x.astype(BF16)
    r = x - hi.astype(F32)
    mid = r.astype(BF16)
    lo = (r - mid.astype(F32)).astype(BF16)
    return hi, mid, lo


def _neg_abs(x):
    return lax.bitcast_convert_type(
        lax.bitcast_convert_type(x, jnp.uint32) | jnp.uint32(0x80000000), F32)


def _pair_level_table():
    t = np.arange(CHUNK)[:, None]
    s = np.arange(CHUNK)[None, :]
    lev = np.floor(np.log2(np.maximum(t ^ s, 1))).astype(np.int32)
    lev = np.where(t == s, -1, lev)
    return np.where(s > t, -2, lev).astype(np.int32)


def _level_operand(p, q, kk, f, b2):
    m = 2 ** p
    if m < VREG_ROWS:
        shape3 = (CHUNK // VREG_ROWS, VREG_ROWS, q.shape[1])
        sub = lax.broadcasted_iota(jnp.int32, (1, VREG_ROWS, q.shape[1]), 1)
        upper = ((sub >> p) & 1) == 1
        q3, k3 = q.reshape(shape3), kk.reshape(shape3)
        if p == 0:
            y = jnp.where(upper, q3 * f.reshape(shape3), k3)
        else:
            b3 = b2.reshape(shape3)
            be = b3[:, m - 1:m, :]
            for k in range(1, VREG_ROWS // (2 * m)):
                be = jnp.where(sub >= 2 * m * k, b3[:, 2 * m * k + m - 1:2 * m * k + m, :], be)
            y = jnp.where(upper, q3, k3) * jnp.exp2(_neg_abs(b3 - be))
        return y.reshape(q.shape).astype(BF16)
    parts = []
    for k in range(CHUNK // (2 * m)):
        lo = slice(2 * m * k, 2 * m * k + m)
        up = slice(2 * m * k + m, 2 * m * (k + 1))
        be = b2[2 * m * k + m - 1:2 * m * k + m, :]
        parts.append(kk[lo] * jnp.exp2(be - b2[lo]))
        parts.append(q[up] * jnp.exp2(b2[up] - be))
    return jnp.concatenate(parts, axis=0).astype(BF16)


def _merge_level(p, att, pm, lev):
    m = 2 ** p
    if m < VREG_ROWS:
        return jnp.where(lev == p, pm, att)
    col = lax.broadcasted_iota(jnp.int32, (1, CHUNK), 1)
    parts = []
    for k in range(CHUNK // (2 * m)):
        lo = slice(2 * m * k, 2 * m * k + m)
        up = slice(2 * m * k + m, 2 * m * (k + 1))
        parts.append(att[lo])
        parts.append(jnp.where((col >= 2 * m * k) & (col < 2 * m * k + m), pm[up], att[up]))
    return jnp.concatenate(parts, axis=0)


def _hgrn_prompt_kernel(x_ref, nm_ref, win_ref, clb_ref, on_ref, wout_ref, lev_ref,
                        ss_ref, fs_ref, qs_ref, is_ref,
                        y_ref, st_ref, sso_ref, os_ref,
                        z_ref, o_ref, stt_ref, k_ref):
    j = pl.program_id(1)
    last_j = pl.num_programs(1) - 1
    n_chunks = TQH // CHUNK
    n_levels = int(math.log2(CHUNK))

    @pl.when(j == 0)
    def _():
        stt_ref[...] = jnp.zeros_like(stt_ref)

    h = _rms(x_ref[0], nm_ref[...]).astype(BF16)
    z_ref[...] = _dot(h, win_ref[...])

    out_rows = []
    for smp in range(ss_ref.shape[0]):
        parts = []
        for hd in range(C_HEADS):
            hs = slice(hd * C_VAL_DIM, (hd + 1) * C_VAL_DIM)
            fb = jnp.broadcast_to(fs_ref[hd, :, smp:smp + 1], (C_KEY_DIM, C_VAL_DIM))
            sn = fb * ss_ref[smp, hd] + (1.0 - fb) * is_ref[smp:smp + 1, hs]
            sso_ref[smp, hd] = sn
            parts.append(jnp.sum(qs_ref[hd, :, smp:smp + 1] * sn, axis=0, keepdims=True))
        out_rows.append(jnp.concatenate(parts, axis=-1))
    os_ref[...] = jnp.concatenate(out_rows, axis=0)

    lb = _lower_bound(clb_ref[...])

    row = lax.broadcasted_iota(jnp.int32, (CHUNK, CHUNK), 0)
    col = lax.broadcasted_iota(jnp.int32, (CHUNK, CHUNK), 1)
    ltri = (row >= col).astype(BF16)

    def chunk_rows(c):
        return pl.ds(pl.multiple_of(c * CHUNK, CHUNK), CHUNK)

    def prefix(g, worst):
        rows = [chunk_rows(g * PREFIX_GROUP + i) for i in range(PREFIX_GROUP)]
        gates = [z_ref[r, C_F:2 * C_F] for r in rows]
        for r, gate in zip(rows, gates):
            f_all = lb + (1.0 - lb) * jax.nn.sigmoid(gate)
            k_ref[r, :] = 1.0 - f_all
            hi, mid, lo = _split3(jnp.log2(f_all))
            b2 = (_dot(ltri, hi) + _dot(ltri, mid)) + _dot(ltri, lo)
            z_ref[r, C_F:2 * C_F] = b2
            b_mid = b2[CHUNK // 2 - 1:CHUNK // 2, :]
            b_last = b2[CHUNK - 1:CHUNK, :]
            worst = jnp.maximum(worst, jnp.maximum(-b_mid, b_mid - b_last))
        return worst

    worst = lax.fori_loop(0, n_chunks // PREFIX_GROUP, prefix, jnp.zeros((1, C_F), F32))
    bounded = jnp.max(worst) <= SAFE_LOG2_RANGE

    def finish_head(rows, hd, o):
        gt = z_ref[rows, 2 * C_F + C_V + hd * C_VAL_DIM:2 * C_F + C_V + (hd + 1) * C_VAL_DIM]
        o = _rms(o, on_ref[...]) * jax.nn.sigmoid(gt)
        o_ref[rows, hd * C_VAL_DIM:(hd + 1) * C_VAL_DIM] = o.astype(BF16)

    def head_inputs(rows, hd):
        q = z_ref[rows, hd * C_KEY_DIM:(hd + 1) * C_KEY_DIM]
        kk = k_ref[rows, hd * C_KEY_DIM:(hd + 1) * C_KEY_DIM]
        b2 = z_ref[rows, C_F + hd * C_KEY_DIM:C_F + (hd + 1) * C_KEY_DIM]
        ivb = z_ref[rows, 2 * C_F + hd * C_VAL_DIM:2 * C_F + (hd + 1) * C_VAL_DIM].astype(BF16)
        return q, kk, b2, ivb

    def factored_chunk(c, carry):
        rows = chunk_rows(c)
        for hd in range(C_HEADS):
            q, kk, b2, ivb = head_inputs(rows, hd)
            b_mid = b2[CHUNK // 2 - 1:CHUNK // 2, :]
            b_last = b2[CHUNK - 1:CHUNK, :]
            qs = (q * jnp.exp2(b2 - b_mid)).astype(BF16)
            kd = (kk * jnp.exp2(b_mid - b2)).astype(BF16)
            att = jnp.where(row >= col, lax.dot_general(qs, kd, _NT, preferred_element_type=F32), 0.0)
            stt = stt_ref[hd]
            o = lax.dot_general(qs, (stt * jnp.exp2(b_mid)).astype(BF16), _NT,
                                preferred_element_type=F32) + _dot(att.astype(BF16), ivb)
            stt_ref[hd] = stt * jnp.exp2(b_last) + jnp.exp2(b_last - b_mid) * lax.dot_general(
                ivb, kd, _TN, preferred_element_type=F32)
            finish_head(rows, hd, o)
        return carry

    def tree_chunk(c, carry):
        rows = chunk_rows(c)
        lev = lev_ref[...]

        def products(hd):
            q, kk, b2, ivb = head_inputs(rows, hd)
            diag = jnp.sum(q * kk, axis=-1, keepdims=True)
            pms = []
            for p in range(n_levels):
                y = _level_operand(p, q, kk, 1.0 - kk, b2)
                pms.append(lax.dot_general(y, y, _NT, preferred_element_type=F32))
            stt = stt_ref[hd]
            o_prev = lax.dot_general((q * jnp.exp2(b2)).astype(BF16), stt.astype(BF16), _NT,
                                     preferred_element_type=F32)
            b_last = b2[CHUNK - 1:CHUNK, :]
            kd = (kk * jnp.exp2(b_last - b2)).astype(BF16)
            stt_ref[hd] = stt * jnp.exp2(b_last) + lax.dot_general(
                ivb, kd, _TN, preferred_element_type=F32)
            return diag, pms, o_prev, ivb

        def finish(hd, diag, pms, o_prev, ivb):
            att = jnp.where(lev == -1, diag, 0.0)
            for p in range(n_levels):
                att = _merge_level(p, att, pms[p], lev)
            finish_head(rows, hd, o_prev + _dot(att.astype(BF16), ivb))

        pending = [products(hd) for hd in range(HEAD_SKEW)]
        for hd in range(C_HEADS):
            if hd + HEAD_SKEW < C_HEADS:
                pending.append(products(hd + HEAD_SKEW))
            finish(hd, *pending.pop(0))
        return carry

    @pl.when(bounded)
    def _():
        lax.fori_loop(0, n_chunks, factored_chunk, 0, unroll=FACTORED_UNROLL)

    @pl.when(jnp.logical_not(bounded))
    def _():
        lax.fori_loop(0, n_chunks, tree_chunk, 0)

    y_ref[0] = x_ref[0] + _dot(o_ref[...], wout_ref[...])

    @pl.when(j == last_j)
    def _():
        for hd in range(C_HEADS):
            st_ref[0, hd] = stt_ref[hd].T


def _hgrn_prompt(x, nm, w_in, clb, on, w_out, state_s, f_s, q_s, i_s):
    nb, seq, _ = x.shape
    nj = seq // TQH
    ns = state_s.shape[0]
    sbh = ns // (nb * nj)
    assert sbh * nb * nj == ns
    blk = lambda b, j: (b, j, 0)
    step = lambda b, j: (b * nj + j, 0, 0)
    step4 = lambda b, j: (b * nj + j, 0, 0, 0)
    rows3 = lambda a: a.reshape(ns // sbh, sbh, a.shape[1])
    cols4 = lambda a: a.reshape(ns // sbh, sbh, C_HEADS, C_KEY_DIM).transpose(0, 2, 3, 1)
    y, st, st_s, o_s = pl.pallas_call(
        _hgrn_prompt_kernel,
        grid=(nb, nj),
        in_specs=[
            pl.BlockSpec((1, TQH, D_MODEL), blk),
            _full((1, D_MODEL)),
            _resident((D_MODEL, C_IN)),
            _full((DEPTH, C_F)),
            _full((1, C_VAL_DIM)),
            _resident((C_V, D_MODEL)),
            _full((CHUNK, CHUNK)),
            pl.BlockSpec((sbh, C_HEADS, C_KEY_DIM, C_VAL_DIM), step4),
            pl.BlockSpec((None, C_HEADS, C_KEY_DIM, sbh), step4),
            pl.BlockSpec((None, C_HEADS, C_KEY_DIM, sbh), step4),
            pl.BlockSpec((None, sbh, C_V), step),
        ],
        out_specs=[
            pl.BlockSpec((1, TQH, D_MODEL), blk),
            pl.BlockSpec((1, C_HEADS, C_KEY_DIM, C_VAL_DIM), lambda b, j: (b, 0, 0, 0)),
            pl.BlockSpec((sbh, C_HEADS, C_KEY_DIM, C_VAL_DIM), step4),
            pl.BlockSpec((None, sbh, C_V), step),
        ],
        out_shape=[
            jax.ShapeDtypeStruct((nb, seq, D_MODEL), F32),
            jax.ShapeDtypeStruct((nb, C_HEADS, C_KEY_DIM, C_VAL_DIM), F32),
            jax.ShapeDtypeStruct(state_s.shape, F32),
            jax.ShapeDtypeStruct((ns // sbh, sbh, C_V), F32),
        ],
        scratch_shapes=[
            pltpu.VMEM((TQH, C_IN), F32),
            pltpu.VMEM((TQH, C_V), BF16),
            pltpu.VMEM((C_HEADS, C_VAL_DIM, C_KEY_DIM), F32),
            pltpu.VMEM((TQH, C_F), F32),
        ],
        compiler_params=pltpu.CompilerParams(
            dimension_semantics=("arbitrary", "arbitrary"), vmem_limit_bytes=VMEM_LIMIT),
        name="hgrn_prompt",
    )(x, nm, w_in, clb, on, w_out, jnp.asarray(_pair_level_table()),
      state_s, cols4(f_s), cols4(q_s), rows3(i_s))
    return y, st, st_s, o_s.reshape(ns, C_V)


def _ab_sample_proj_kernel(w00_ref, b0_ref, x_ref, nm_ref, win_ref, qn_ref, kn_ref, lng_ref, lnb_ref,
                           qx_ref, knew_ref, vnew_ref, bm_ref, gv_ref):
    n = x_ref.shape[0]
    h = _rms(x_ref[...], nm_ref[...]).astype(BF16)
    z = _dot(h, win_ref[...])
    zeros = jnp.zeros((n, A_HEAD_DIM), F32)
    for hh in range(A_HEADS):
        qh = _rms(z[:, hh * A_HEAD_DIM:(hh + 1) * A_HEAD_DIM], qn_ref[...]) * ATTN_SCALE
        qx_ref[hh] = jnp.concatenate([qh, zeros] if hh // A_GROUP == 0 else [zeros, qh], axis=-1)
    kparts = []
    for g in range(A_KV_HEADS):
        kparts.append(_rms(z[:, A_Q + g * A_HEAD_DIM:A_Q + (g + 1) * A_HEAD_DIM], kn_ref[...]))
    knew_ref[...] = jnp.concatenate(kparts, axis=-1)
    vnew_ref[...] = z[:, A_Q + A_KV:A_Q + 2 * A_KV]

    u = _gelu(z[:, A_Q + 2 * A_KV:A_Q + 2 * A_KV + B_WIDTH])
    vln = _layernorm(_gelu(z[:, A_Q + 2 * A_KV + B_WIDTH:AB_IN]), lng_ref[...], lnb_ref[...])
    grp = lax.broadcasted_iota(jnp.int32, (1, B_WIDTH), 1) // B_GROUP_DIM
    srow = jnp.zeros((1, B_WIDTH), F32)
    brow = jnp.zeros((1, B_WIDTH), F32)
    for g in range(B_GROUPS):
        srow = jnp.where(grp == g, w00_ref[g], srow)
        brow = jnp.where(grp == g, b0_ref[g], brow)
    bm_ref[...] = u * (vln * srow + brow)
    gv_ref[...] = vln


def _ab_sample_proj(x, nm, w_in, qn, kn, lng, lnb, w00, b0):
    n = x.shape[0]
    return pl.pallas_call(
        _ab_sample_proj_kernel,
        in_specs=[_SMEM, _SMEM] + [pl.BlockSpec(memory_space=pltpu.VMEM)] * 7,
        out_shape=[
            jax.ShapeDtypeStruct((A_HEADS, n, A_KV), F32),
            jax.ShapeDtypeStruct((n, A_KV), F32),
            jax.ShapeDtypeStruct((n, A_KV), F32),
            jax.ShapeDtypeStruct((n, B_WIDTH), F32),
            jax.ShapeDtypeStruct((n, B_WIDTH), F32),
        ],
        compiler_params=pltpu.CompilerParams(vmem_limit_bytes=VMEM_LIMIT),
        name="ab_sample_proj",
    )(w00, b0, x, nm, w_in, qn, kn, lng, lnb)


def _ab_sample_attn_kernel(ck_ref, cv_ref, qx_ref, kn_ref, vn_ref, sb_ref, sink_ref,
                           nk_ref, nv_ref, om_ref):
    wb = ck_ref.shape[1]
    head = lax.broadcasted_iota(jnp.int32, (1, A_HEADS, A_KV), 1)
    lane = lax.broadcasted_iota(jnp.int32, (1, A_HEADS, A_KV), 2)
    own_group = (head // A_GROUP) == (lane // A_HEAD_DIM)
    sink = sink_ref[...][None]
    kc, vc = ck_ref[...], cv_ref[...]
    kn, vn = kn_ref[...], vn_ref[...]
    nk_ref[:, 0:wb - 1, :] = kc[:, 1:wb, :]
    nk_ref[:, wb - 1:wb, :] = kn
    nv_ref[:, 0:wb - 1, :] = vc[:, 1:wb, :]
    nv_ref[:, wb - 1:wb, :] = vn
    q = qx_ref[...]
    s = jnp.einsum('bhd,bkd->bhk', q.astype(BF16), kc.astype(BF16), preferred_element_type=F32)
    s = s + sb_ref[:, 0:wb][None]
    sn = jnp.sum(q * kn, axis=-1, keepdims=True) + sb_ref[:, wb:wb + 1][None]
    m = jnp.maximum(jnp.maximum(jnp.max(s, axis=-1, keepdims=True), sn), sink)
    e = jnp.exp(s - m)
    en = jnp.exp(sn - m)
    r = 1.0 / (jnp.sum(e, axis=-1, keepdims=True) + en + jnp.exp(sink - m))
    o = jnp.einsum('bhk,bkd->bhd', (e * r).astype(BF16), vc.astype(BF16),
                   preferred_element_type=F32) + (en * r) * vn
    om_ref[...] = jnp.where(own_group, o, 0.0)


def _ab_sample_attn(ck, cv, qx, kn, vn, sb, sink):
    n, wb, _ = ck.shape
    blk3 = lambda i: (i, 0, 0)
    return pl.pallas_call(
        _ab_sample_attn_kernel,
        grid=(n // SB,),
        in_specs=[
            pl.BlockSpec((SB, wb, A_KV), blk3),
            pl.BlockSpec((SB, wb, A_KV), blk3),
            pl.BlockSpec((SB, A_HEADS, A_KV), blk3),
            pl.BlockSpec((SB, 1, A_KV), blk3),
            pl.BlockSpec((SB, 1, A_KV), blk3),
            _full((A_HEADS, 2 * WINDOW)),
            _full((A_HEADS, 1)),
        ],
        out_specs=[
            pl.BlockSpec((SB, wb, A_KV), blk3),
            pl.BlockSpec((SB, wb, A_KV), blk3),
            pl.BlockSpec((SB, A_HEADS, A_KV), blk3),
        ],
        out_shape=[
            jax.ShapeDtypeStruct((n, wb, A_KV), F32),
            jax.ShapeDtypeStruct((n, wb, A_KV), F32),
            jax.ShapeDtypeStruct((n, A_HEADS, A_KV), F32),
        ],
        compiler_params=pltpu.CompilerParams(dimension_semantics=("arbitrary",)),
        name="ab_sample_attn",
    )(ck, cv, qx, kn, vn, sb, sink)


def _residual_proj_kernel(x_ref, m_ref, w_ref, y_ref):
    y_ref[...] = x_ref[...] + _dot(m_ref[...].astype(BF16), w_ref[...])


def _residual_proj(x, mix, w):
    return pl.pallas_call(
        _residual_proj_kernel,
        out_shape=jax.ShapeDtypeStruct(x.shape, F32),
        name="residual_proj",
    )(x, mix, w)


def _hgrn_sample_proj_kernel(x_ref, nm_ref, win_ref, clb_ref, q_ref, f_ref, i_ref, sg_ref):
    h = _rms(x_ref[...], nm_ref[...]).astype(BF16)
    z = _dot(h, win_ref[...])
    lb = _lower_bound(clb_ref[...])
    q_ref[...] = z[:, 0:C_F]
    f_ref[...] = lb + (1.0 - lb) * jax.nn.sigmoid(z[:, C_F:2 * C_F])
    i_ref[...] = z[:, 2 * C_F:2 * C_F + C_V]
    sg_ref[...] = jax.nn.sigmoid(z[:, 2 * C_F + C_V:C_IN])


def _hgrn_sample_proj(x, nm, w_in, clb):
    n = x.shape[0]
    return pl.pallas_call(
        _hgrn_sample_proj_kernel,
        out_shape=[jax.ShapeDtypeStruct((n, C_F), F32)] * 2 + [jax.ShapeDtypeStruct((n, C_V), F32)] * 2,
        compiler_params=pltpu.CompilerParams(vmem_limit_bytes=VMEM_LIMIT),
        name="hgrn_sample_proj",
    )(x, nm, w_in, clb)


def _hgrn_sample_out_kernel(o_ref, sg_ref, on_ref, w_ref, x_ref, y_ref):
    parts = []
    for hd in range(C_HEADS):
        parts.append(_rms(o_ref[:, hd * C_VAL_DIM:(hd + 1) * C_VAL_DIM], on_ref[...]))
    on = jnp.concatenate(parts, axis=-1) * sg_ref[...]
    y_ref[...] = x_ref[...] + _dot(on.astype(BF16), w_ref[...])


def _hgrn_sample_out(o, sg, on, w_out, x):
    return pl.pallas_call(
        _hgrn_sample_out_kernel,
        out_shape=jax.ShapeDtypeStruct(x.shape, F32),
        name="hgrn_sample_out",
    )(o, sg, on, w_out, x)


def kernel(x_prompt, x_sample, cache_k, cache_v, state_hgrn, norm_mix, norm_ffn, w_in_ab, w_out_ab,
           q_norm, k_norm, attn_sink, rel_bias, gmlp_ln_g, gmlp_ln_b, gmlp_w_s, gmlp_b_s,
           w_in_c, c_lower_bounds, c_out_norm, w_out_c, w_gate, w_up, w_down):
    assert norm_mix.shape[0] == DEPTH == 2 and w_in_ab.shape[0] == 1 and w_in_c.shape[0] == 1
    nb, seq, _ = x_prompt.shape
    ns = x_sample.shape[0]
    assert x_sample.shape[1] == 1 and cache_k.shape[2] == WINDOW

    row = lambda v: v.reshape(1, -1)
    bf = lambda w: w.astype(BF16)
    w_in_ab0, w_out_ab0 = bf(w_in_ab[0]), bf(w_out_ab[0])
    w_in_c0, w_out_c0 = bf(w_in_c[0]), bf(w_out_c[0])
    nm, nf = norm_mix, norm_ffn
    qn, kn = row(q_norm[0]), row(k_norm[0])
    lng, lnb = row(gmlp_ln_g[0]), row(gmlp_ln_b[0])
    sink = attn_sink[0]

    tab, tabp = _bias_table(rel_bias)

    xp, knew_p, vnew_p, gv_p = _ab_prompt(
        x_prompt, row(nm[0]), w_in_ab0, jnp.tile(qn, (1, A_HEADS)), jnp.tile(kn, (1, A_KV_HEADS)),
        sink, tabp, lng, lnb, gmlp_w_s[0], jnp.repeat(gmlp_b_s[0].T, B_GROUP_DIM, axis=1), w_out_ab0)
    xs = x_sample.reshape(ns, D_MODEL)
    qx, knew_s, vnew_s, bm_s, gv_s = _ab_sample_proj(
        xs, row(nm[0]), w_in_ab0, qn, kn, lng, lnb, gmlp_w_s[0, :, 0, 0], gmlp_b_s[0, :, 0])
    sb = jnp.pad(tab[:, WINDOW - 1, WINDOW - 1:], ((0, 0), (0, WINDOW - 1)))
    nk_s, nv_s, om = _ab_sample_attn(
        cache_k[0].reshape(ns, WINDOW, A_KV), cache_v[0].reshape(ns, WINDOW, A_KV),
        qx.transpose(1, 0, 2), knew_s[:, None, :], vnew_s[:, None, :], sb, sink.reshape(A_HEADS, 1))
    om = om.reshape(ns, A_KV_HEADS, A_GROUP, A_KV_HEADS, A_HEAD_DIM)
    a_s = jnp.stack([om[:, g, :, g, :] for g in range(A_KV_HEADS)], axis=1).reshape(ns, A_Q)
    xs = _residual_proj(xs, jnp.concatenate([a_s, bm_s], axis=-1), w_out_ab0)
    xp, xs = _ffn(xp.reshape(nb * seq, D_MODEL), xs, row(nf[0]), w_gate, w_up, w_down, 0)

    q_s, f_s, i_s, sg_s = _hgrn_sample_proj(xs, row(nm[1]), w_in_c0, c_lower_bounds)
    xp, st_p, st_s, o_s = _hgrn_prompt(xp.reshape(nb, seq, D_MODEL), row(nm[1]), w_in_c0, c_lower_bounds,
                                       row(c_out_norm[0]), w_out_c0, state_hgrn[0], f_s, q_s, i_s)
    xs = _hgrn_sample_out(o_s, sg_s, row(c_out_norm[0]), w_out_c0, xs)
    xp, xs = _ffn(xp.reshape(nb * seq, D_MODEL), xs, row(nf[1]), w_gate, w_up, w_down, 1)

    kv5 = lambda a: a.reshape(1, a.shape[0], WINDOW, A_KV_HEADS, A_HEAD_DIM)
    return (xp.reshape(nb, seq, D_MODEL), xs.reshape(ns, 1, D_MODEL),
            kv5(knew_p), kv5(vnew_p), kv5(nk_s), kv5(nv_s),
            gv_p[None], gv_s.reshape(1, ns, 1, B_WIDTH),
            st_p[None], st_s[None])
```

```python
import functools
import math

import jax
import jax.numpy as jnp
import numpy as np
from jax import lax
from jax.experimental import pallas as pl
from jax.experimental.pallas import tpu as pltpu

F32 = jnp.float32
BF16 = jnp.bfloat16

D_MODEL = 1024
DEPTH = 2
A_HEADS = 8
A_KV_HEADS = 2
A_GROUP = A_HEADS // A_KV_HEADS
A_HEAD_DIM = 64
WINDOW = 128
ATTN_SCALE = A_HEAD_DIM ** -0.5
NUM_BUCKETS = 32
MAX_DISTANCE = 128
A_Q = A_HEADS * A_HEAD_DIM
A_KV = A_KV_HEADS * A_HEAD_DIM
B_GROUPS = 8
B_GROUP_DIM = 64
B_WIDTH = B_GROUPS * B_GROUP_DIM
B_CHUNK = 128
AB_IN = A_Q + 2 * A_KV + 2 * B_WIDTH
AB_MIX = A_Q + B_WIDTH
C_HEADS = 8
C_KEY_DIM = 128
C_VAL_DIM = 128
C_F = C_HEADS * C_KEY_DIM
C_V = C_HEADS * C_VAL_DIM
C_IN = 2 * C_F + 2 * C_V
D_FF = 2816
EPS = 1e-6

NEG = -1e30

VMEM_LIMIT = 56 * 1024 * 1024
VREG_ROWS = 8

CHUNK = 128
TQ = 1024
TQH = 512
TM = 512
HEAD_SKEW = 2
SAFE_LOG2_RANGE = 64.0
PREFIX_GROUP = 4
FACTORED_UNROLL = 4
AB_UNROLL = 2

_NT = (((1,), (1,)), ((), ()))
_TN = (((0,), (0,)), ((), ()))


def _rms(x, g):
    return x * lax.rsqrt(jnp.mean(x * x, axis=-1, keepdims=True) + EPS) * g


def _gelu(x):
    return 0.5 * x * (1.0 + lax.erf(x * math.sqrt(0.5)))


def _layernorm(x, g, b):
    xc = x - jnp.mean(x, axis=-1, keepdims=True)
    return xc * lax.rsqrt(jnp.mean(xc * xc, axis=-1, keepdims=True) + EPS) * g + b


def _dot(a, b):
    return jnp.dot(a, b, preferred_element_type=F32)


def _full(shape):
    n = len(shape)
    return pl.BlockSpec(shape, lambda *_: (0,) * n)


def _resident(shape):
    n = len(shape)
    return pl.BlockSpec(shape, lambda *_: (0,) * n, pipeline_mode=pl.Buffered(1))


_SMEM = pl.BlockSpec(memory_space=pltpu.SMEM)


def _bias_table_kernel(rel_ref, tab_ref, tabp_ref):
    qi = lax.broadcasted_iota(jnp.int32, (WINDOW, 2 * WINDOW), 0)
    kj = lax.broadcasted_iota(jnp.int32, (WINDOW, 2 * WINDOW), 1)
    dist = qi + WINDOW - kj
    ok = (dist >= 0) & (dist < WINDOW)
    max_exact = NUM_BUCKETS // 2
    d = jnp.maximum(dist, 0)
    dl = jnp.maximum(d, 1).astype(F32)
    v = (jnp.log(dl / max_exact) / math.log(MAX_DISTANCE / max_exact) * (NUM_BUCKETS - max_exact))
    far = d >= max_exact
    hits = []
    for b in range(NUM_BUCKETS):
        if b < max_exact:
            hits.append(d == b)
        elif b < NUM_BUCKETS - 1:
            hits.append(far & (v >= b - max_exact) & (v < b - max_exact + 1))
        else:
            hits.append(far & (v >= b - max_exact))
    for h in range(A_HEADS):
        acc = jnp.zeros((WINDOW, 2 * WINDOW), F32)
        for b in range(NUM_BUCKETS):
            acc = jnp.where(hits[b], rel_ref[b, h], acc)
        t = jnp.where(ok, acc, NEG)
        tab_ref[h] = t
        cols = slice((h % 2) * 2 * WINDOW, (h % 2 + 1) * 2 * WINDOW)
        tabp_ref[0, h // 2, :, cols] = t
        tabp_ref[1, h // 2, :, cols] = jnp.where(kj < WINDOW, NEG, t)


def _bias_table(rel_bias):
    return pl.pallas_call(
        _bias_table_kernel,
        out_shape=[
            jax.ShapeDtypeStruct((A_HEADS, WINDOW, 2 * WINDOW), F32),
            jax.ShapeDtypeStruct((2, A_HEADS // 2, WINDOW, 4 * WINDOW), F32),
        ],
        in_specs=[_SMEM],
        name="bias_table",
    )(rel_bias)


PAIR = 2 * A_HEAD_DIM
N_PAIRS = A_HEADS // 2


def _sample_cache_attention(ck_ref, cv_ref, qx_ref, kn_ref, vn_ref, sb_ref, sink_ref,
                            nk_ref, nv_ref, om_ref):
    wb = ck_ref.shape[1]
    head = lax.broadcasted_iota(jnp.int32, (1, A_HEADS, A_KV), 1)
    lane = lax.broadcasted_iota(jnp.int32, (1, A_HEADS, A_KV), 2)
    own_group = (head // A_GROUP) == (lane // A_HEAD_DIM)
    sink = sink_ref[...][None]
    kc, vc = ck_ref[...], cv_ref[...]
    kn, vn = kn_ref[...], vn_ref[...]
    nk_ref[:, 0:wb - 1, :] = kc[:, 1:wb, :]
    nk_ref[:, wb - 1:wb, :] = kn
    nv_ref[:, 0:wb - 1, :] = vc[:, 1:wb, :]
    nv_ref[:, wb - 1:wb, :] = vn
    q = qx_ref[...]
    s = jnp.einsum('bhd,bkd->bhk', q.astype(BF16), kc.astype(BF16), preferred_element_type=F32)
    s = s + sb_ref[:, 0:wb][None]
    sn = jnp.sum(q * kn, axis=-1, keepdims=True) + sb_ref[:, wb:wb + 1][None]
    m = jnp.maximum(jnp.maximum(jnp.max(s, axis=-1, keepdims=True), sn), sink)
    e = jnp.exp(s - m)
    en = jnp.exp(sn - m)
    r = 1.0 / (jnp.sum(e, axis=-1, keepdims=True) + en + jnp.exp(sink - m))
    o = jnp.einsum('bhk,bkd->bhd', (e * r).astype(BF16), vc.astype(BF16),
                   preferred_element_type=F32) + (en * r) * vn
    om_ref[...] = jnp.where(own_group, o, 0.0)


def _ab_prompt_kernel(sink_ref, x_ref, nm_ref, win_ref, qg_ref, kg_ref, tabp_ref, lng_ref, lnb_ref,
                      ws_ref, bsp_ref, wout_ref,
                      ck_ref, cv_ref, qx_ref, kn_ref, vn_ref, sb_ref, sinkc_ref,
                      y_ref, knew_ref, vnew_ref, gv_ref, nk_ref, nv_ref, om_ref,
                      z_ref, mix_ref, q_ref, k_ref, kr_ref, v_ref, vr_ref, wpair_ref, kl_ref, vl_ref, gl_ref):
    j = pl.program_id(1)
    last_j = pl.num_programs(1) - 1
    n_chunks = TQ // CHUNK

    @pl.when(j == 0)
    def _():
        for ref in (k_ref, kr_ref, v_ref, vr_ref):
            ref[0:CHUNK, :] = jnp.zeros((CHUNK, A_KV), BF16)
        row = lax.broadcasted_iota(jnp.int32, (B_CHUNK, B_CHUNK), 0)
        col = lax.broadcasted_iota(jnp.int32, (B_CHUNK, B_CHUNK), 1)
        for g in range(B_GROUPS):
            wpair_ref[g // 2, :, (g % 2) * B_CHUNK:(g % 2 + 1) * B_CHUNK] = jnp.where(
                row >= col, ws_ref[g], 0.0).astype(BF16)

    h = _rms(x_ref[0], nm_ref[...]).astype(BF16)
    z_ref[...] = _dot(h, win_ref[...])

    _sample_cache_attention(ck_ref, cv_ref, qx_ref, kn_ref, vn_ref, sb_ref, sinkc_ref,
                            nk_ref, nv_ref, om_ref)

    lo_half = lax.broadcasted_iota(jnp.int32, (1, PAIR), 1) < A_HEAD_DIM
    r_i = lax.broadcasted_iota(jnp.int32, (PAIR, PAIR), 0) // A_HEAD_DIM
    c_i = lax.broadcasted_iota(jnp.int32, (PAIR, PAIR), 1) // A_HEAD_DIM
    half_mean = jnp.where(r_i == c_i, 1.0 / A_HEAD_DIM, 0.0).astype(BF16)

    def mean_sq_halves(x):
        x2 = x * x
        hi = x2.astype(BF16)
        lo = (x2 - hi.astype(F32)).astype(BF16)
        return _dot(hi, half_mean) + _dot(lo, half_mean)

    def block_diag(top, bot):
        zero = jnp.zeros_like(top)
        return jnp.concatenate([jnp.where(lo_half, top, zero), jnp.where(lo_half, zero, bot)], axis=0)

    kraw = z_ref[:, A_Q:A_Q + A_KV]
    v_all = z_ref[:, A_Q + A_KV:A_Q + 2 * A_KV]
    kn_all = kraw * lax.rsqrt(mean_sq_halves(kraw) + EPS) * kg_ref[...]
    k_ref[CHUNK:, :] = kn_all.astype(BF16)
    kr_ref[CHUNK:, :] = pltpu.roll(kn_all, A_HEAD_DIM, 1).astype(BF16)
    v_ref[CHUNK:, :] = v_all.astype(BF16)
    vr_ref[CHUNK:, :] = pltpu.roll(v_all, A_HEAD_DIM, 1).astype(BF16)
    kl_ref[...] = kn_all[TQ - CHUNK:, :]
    vl_ref[...] = v_all[TQ - CHUNK:, :]
    for i in range(N_PAIRS):
        ps = slice(i * PAIR, (i + 1) * PAIR)
        qraw = z_ref[:, ps]
        qn = qraw * lax.rsqrt(mean_sq_halves(qraw) + EPS) * (qg_ref[:, ps] * ATTN_SCALE)
        q_ref[:, ps] = qn.astype(BF16)

    def chunk(c, carry):
        r0 = pl.multiple_of(c * CHUNK, CHUNK)
        rows = pl.ds(r0, CHUNK)
        first = jnp.where(jnp.logical_and(j == 0, c == 0), 1, 0)

        both = pl.ds(r0, 2 * CHUNK)
        k2, k2r, v2, v2r = k_ref[both, :], kr_ref[both, :], v_ref[both, :], vr_ref[both, :]
        kbd = [block_diag(k2, k2r), block_diag(k2r, k2)]
        vbd = [block_diag(v2, v2r), block_diag(v2r, v2)]

        scores = []
        for i in range(N_PAIRS):
            s = lax.dot_general(q_ref[rows, i * PAIR:(i + 1) * PAIR], kbd[i // (A_GROUP // 2)], _NT,
                                preferred_element_type=F32)
            scores.append(s + tabp_ref[first, i])
        outs = []
        for i in range(N_PAIRS):
            es, rs = [], []
            for hh in range(2):
                sh = scores[i][:, hh * 2 * WINDOW:(hh + 1) * 2 * WINDOW]
                sk = sink_ref[2 * i + hh]
                m = jnp.maximum(jnp.max(sh, axis=-1, keepdims=True), sk)
                e = jnp.exp(sh - m)
                rs.append(1.0 / (jnp.sum(e, axis=-1, keepdims=True) + jnp.exp(sk - m)))
                es.append(e.astype(BF16))
            o = _dot(jnp.concatenate(es, axis=-1), vbd[i // (A_GROUP // 2)])
            outs.append(o * jnp.where(lo_half, rs[0], rs[1]))
        mix_ref[rows, 0:A_Q] = jnp.concatenate(outs, axis=-1).astype(BF16)

        zu = z_ref[rows, A_Q + 2 * A_KV:A_Q + 2 * A_KV + B_WIDTH]
        zv = z_ref[rows, A_Q + 2 * A_KV + B_WIDTH:AB_IN]
        u = _gelu(zu)
        vln = _layernorm(_gelu(zv), lng_ref[...], lnb_ref[...])
        vlb = vln.astype(BF16)
        sparts = []
        for i in range(B_GROUPS // 2):
            vpair = vlb[:, i * PAIR:(i + 1) * PAIR]
            sparts.append(_dot(wpair_ref[i], block_diag(vpair, vpair)))
        bm = u * (jnp.concatenate(sparts, axis=-1) + bsp_ref[...])
        mix_ref[rows, A_Q:AB_MIX] = bm.astype(BF16)

        gl_ref[...] = vln
        return carry

    lax.fori_loop(0, n_chunks, chunk, 0, unroll=AB_UNROLL)
    y_ref[0] = x_ref[0] + _dot(mix_ref[...], wout_ref[...])
    for ref in (k_ref, kr_ref, v_ref, vr_ref):
        ref[0:CHUNK, :] = ref[TQ:TQ + CHUNK, :]

    @pl.when(j == last_j)
    def _():
        knew_ref[0] = kl_ref[...]
        vnew_ref[0] = vl_ref[...]
        gv_ref[0] = gl_ref[...]


def _ab_prompt(x, nm, w_in, qg, kg, sink, tabp, lng, lnb, w_s, bsp, w_out, ck, cv, qx, kn_s, vn_s, sb):
    nb, seq, _ = x.shape
    nj = seq // TQ
    grid = (nb, nj)
    ns, wb, _ = ck.shape
    sba = ns // (nb * nj)
    assert sba * nb * nj == ns
    blk = lambda b, j: (b, j, 0)
    per_b = lambda b, j: (b, 0, 0)
    step = lambda b, j: (b * nj + j, 0, 0)
    return pl.pallas_call(
        _ab_prompt_kernel,
        grid=grid,
        in_specs=[
            _SMEM,
            pl.BlockSpec((1, TQ, D_MODEL), blk),
            _full((1, D_MODEL)),
            _resident((D_MODEL, AB_IN)),
            _full((1, A_Q)),
            _full((1, A_KV)),
            _resident((2, N_PAIRS, WINDOW, 4 * WINDOW)),
            _full((1, B_WIDTH)),
            _full((1, B_WIDTH)),
            _resident((B_GROUPS, B_CHUNK, B_CHUNK)),
            _resident((B_CHUNK, B_WIDTH)),
            _resident((AB_MIX, D_MODEL)),
            pl.BlockSpec((sba, wb, A_KV), step),
            pl.BlockSpec((sba, wb, A_KV), step),
            pl.BlockSpec((sba, A_HEADS, A_KV), step),
            pl.BlockSpec((sba, 1, A_KV), step),
            pl.BlockSpec((sba, 1, A_KV), step),
            _full((A_HEADS, 2 * WINDOW)),
            _full((A_HEADS, 1)),
        ],
        out_specs=[
            pl.BlockSpec((1, TQ, D_MODEL), blk),
            pl.BlockSpec((1, WINDOW, A_KV), per_b),
            pl.BlockSpec((1, WINDOW, A_KV), per_b),
            pl.BlockSpec((1, B_CHUNK, B_WIDTH), per_b),
            pl.BlockSpec((sba, wb, A_KV), step),
            pl.BlockSpec((sba, wb, A_KV), step),
            pl.BlockSpec((sba, A_HEADS, A_KV), step),
        ],
        out_shape=[
            jax.ShapeDtypeStruct((nb, seq, D_MODEL), F32),
            jax.ShapeDtypeStruct((nb, WINDOW, A_KV), F32),
            jax.ShapeDtypeStruct((nb, WINDOW, A_KV), F32),
            jax.ShapeDtypeStruct((nb, B_CHUNK, B_WIDTH), F32),
            jax.ShapeDtypeStruct((ns, wb, A_KV), F32),
            jax.ShapeDtypeStruct((ns, wb, A_KV), F32),
            jax.ShapeDtypeStruct((ns, A_HEADS, A_KV), F32),
        ],
        scratch_shapes=[
            pltpu.VMEM((TQ, AB_IN), F32),
            pltpu.VMEM((TQ, AB_MIX), BF16),
            pltpu.VMEM((TQ, A_Q), BF16),
            pltpu.VMEM((CHUNK + TQ, A_KV), BF16),
            pltpu.VMEM((CHUNK + TQ, A_KV), BF16),
            pltpu.VMEM((CHUNK + TQ, A_KV), BF16),
            pltpu.VMEM((CHUNK + TQ, A_KV), BF16),
            pltpu.VMEM((B_GROUPS // 2, B_CHUNK, 2 * B_CHUNK), BF16),
            pltpu.VMEM((WINDOW, A_KV), F32),
            pltpu.VMEM((WINDOW, A_KV), F32),
            pltpu.VMEM((B_CHUNK, B_WIDTH), F32),
        ],
        compiler_params=pltpu.CompilerParams(
            dimension_semantics=("arbitrary", "arbitrary"), vmem_limit_bytes=VMEM_LIMIT),
        name="ab_prompt",
    )(sink, x, nm, w_in, qg, kg, tabp, lng, lnb, w_s, bsp, w_out,
      ck, cv, qx, kn_s, vn_s, sb, sink.reshape(A_HEADS, 1))


FF_TILE = 256


def _ffn_kernel(xp_ref, xs_ref, g_ref, wg_ref, wu_ref, wd_ref, yp_ref, ys_ref,
                wg_s, wu_s, wd_s, h0_s, acc_s, *, n_cast, n_prompt):
    s = pl.program_id(0)

    def gated(h, wg, wu):
        gate = _dot(h, wg)
        return (gate * jax.nn.sigmoid(gate) * _dot(h, wu)).astype(BF16)

    @pl.when(s == 0)
    def _():
        x = xp_ref[...]
        h0_s[...] = _rms(x, g_ref[...]).astype(BF16)
        acc_s[...] = x

    for c in range(n_cast):
        @pl.when(s == c)
        def _(c=c):
            tile = slice(c * FF_TILE, (c + 1) * FF_TILE)
            wg_t, wu_t, wd_t = (r[...].astype(BF16) for r in (wg_ref, wu_ref, wd_ref))
            wg_s[:, tile] = wg_t
            wu_s[:, tile] = wu_t
            wd_s[tile, :] = wd_t
            acc_s[...] += _dot(gated(h0_s[...], wg_t, wu_t), wd_t)

    @pl.when(s == n_cast - 1)
    def _():
        yp_ref[...] = acc_s[...]

    def swiglu(x):
        h = _rms(x, g_ref[...]).astype(BF16)
        return x + _dot(gated(h, wg_s[...], wu_s[...]), wd_s[...])

    @pl.when(jnp.logical_and(s >= n_cast, s < n_cast + n_prompt - 1))
    def _():
        yp_ref[...] = swiglu(xp_ref[...])

    @pl.when(s == n_cast + n_prompt - 1)
    def _():
        ys_ref[...] = swiglu(xs_ref[...])


def _ffn(xp, xs, g, w_gate, w_up, w_down, layer):
    rows, ns = xp.shape[0], xs.shape[0]
    n_cast, n_prompt = D_FF // FF_TILE, rows // TM
    w_tile = lambda s: jnp.minimum(s, n_cast - 1)
    row_blk = lambda s: (jnp.clip(s - (n_cast - 1), 0, n_prompt - 1), 0)
    return pl.pallas_call(
        functools.partial(_ffn_kernel, n_cast=n_cast, n_prompt=n_prompt),
        grid=(n_cast + n_prompt,),
        in_specs=[
            pl.BlockSpec((TM, D_MODEL), row_blk),
            _full((ns, D_MODEL)),
            _full((1, D_MODEL)),
            pl.BlockSpec((None, D_MODEL, FF_TILE), lambda s: (layer, 0, w_tile(s))),
            pl.BlockSpec((None, D_MODEL, FF_TILE), lambda s: (layer, 0, w_tile(s))),
            pl.BlockSpec((None, FF_TILE, D_MODEL), lambda s: (layer, w_tile(s), 0)),
        ],
        out_specs=[pl.BlockSpec((TM, D_MODEL), row_blk), _full((ns, D_MODEL))],
        out_shape=[jax.ShapeDtypeStruct((rows, D_MODEL), F32), jax.ShapeDtypeStruct((ns, D_MODEL), F32)],
        scratch_shapes=[
            pltpu.VMEM((D_MODEL, D_FF), BF16),
            pltpu.VMEM((D_MODEL, D_FF), BF16),
            pltpu.VMEM((D_FF, D_MODEL), BF16),
            pltpu.VMEM((TM, D_MODEL), BF16),
            pltpu.VMEM((TM, D_MODEL), F32),
        ],
        compiler_params=pltpu.CompilerParams(
            dimension_semantics=("arbitrary",), vmem_limit_bytes=VMEM_LIMIT),
        name="ffn",
    )(xp, xs, g, w_gate, w_up, w_down)


def _lower_bound(clb):
    m = jnp.max(clb, axis=0, keepdims=True)
    e = jnp.exp(clb - m)
    sm = e / jnp.sum(e, axis=0, keepdims=True)
    return (sm[0:1] + sm[1:2]) - sm[0:1]


def _split3(x):
    hi = x.astype(BF16)
    r = x - hi.astype(F32)
    mid = r.astype(BF16)
    lo = (r - mid.astype(F32)).astype(BF16)
    return hi, mid, lo


def _neg_abs(x):
    return lax.bitcast_convert_type(
        lax.bitcast_convert_type(x, jnp.uint32) | jnp.uint32(0x80000000), F32)


def _pair_level_table():
    t = np.arange(CHUNK)[:, None]
    s = np.arange(CHUNK)[None, :]
    lev = np.floor(np.log2(np.maximum(t ^ s, 1))).astype(np.int32)
    lev = np.where(t == s, -1, lev)
    return np.where(s > t, -2, lev).astype(np.int32)


def _level_operand(p, q, kk, f, b2):
    m = 2 ** p
    if m < VREG_ROWS:
        shape3 = (CHUNK // VREG_ROWS, VREG_ROWS, q.shape[1])
        sub = lax.broadcasted_iota(jnp.int32, (1, VREG_ROWS, q.shape[1]), 1)
        upper = ((sub >> p) & 1) == 1
        q3, k3 = q.reshape(shape3), kk.reshape(shape3)
        if p == 0:
            y = jnp.where(upper, q3 * f.reshape(shape3), k3)
        else:
            b3 = b2.reshape(shape3)
            be = b3[:, m - 1:m, :]
            for k in range(1, VREG_ROWS // (2 * m)):
                be = jnp.where(sub >= 2 * m * k, b3[:, 2 * m * k + m - 1:2 * m * k + m, :], be)
            y = jnp.where(upper, q3, k3) * jnp.exp2(_neg_abs(b3 - be))
        return y.reshape(q.shape).astype(BF16)
    parts = []
    for k in range(CHUNK // (2 * m)):
        lo = slice(2 * m * k, 2 * m * k + m)
        up = slice(2 * m * k + m, 2 * m * (k + 1))
        be = b2[2 * m * k + m - 1:2 * m * k + m, :]
        parts.append(kk[lo] * jnp.exp2(be - b2[lo]))
        parts.append(q[up] * jnp.exp2(b2[up] - be))
    return jnp.concatenate(parts, axis=0).astype(BF16)


def _merge_level(p, att, pm, lev):
    m = 2 ** p
    if m < VREG_ROWS:
        return jnp.where(lev == p, pm, att)
    col = lax.broadcasted_iota(jnp.int32, (1, CHUNK), 1)
    parts = []
    for k in range(CHUNK // (2 * m)):
        lo = slice(2 * m * k, 2 * m * k + m)
        up = slice(2 * m * k + m, 2 * m * (k + 1))
        parts.append(att[lo])
        parts.append(jnp.where((col >= 2 * m * k) & (col < 2 * m * k + m), pm[up], att[up]))
    return jnp.concatenate(parts, axis=0)


def _hgrn_prompt_kernel(x_ref, nm_ref, win_ref, clb_ref, on_ref, wout_ref, lev_ref,
                        ss_ref, fs_ref, qs_ref, is_ref,
                        y_ref, st_ref, sso_ref, os_ref,
                        z_ref, o_ref, stt_ref, k_ref):
    j = pl.program_id(1)
    last_j = pl.num_programs(1) - 1
    n_chunks = TQH // CHUNK
    n_levels = int(math.log2(CHUNK))

    @pl.when(j == 0)
    def _():
        stt_ref[...] = jnp.zeros_like(stt_ref)

    h = _rms(x_ref[0], nm_ref[...]).astype(BF16)
    z_ref[...] = _dot(h, win_ref[...])

    out_rows = []
    for smp in range(ss_ref.shape[0]):
        parts = []
        for hd in range(C_HEADS):
            hs = slice(hd * C_VAL_DIM, (hd + 1) * C_VAL_DIM)
            fb = jnp.broadcast_to(fs_ref[hd, :, smp:smp + 1], (C_KEY_DIM, C_VAL_DIM))
            sn = fb * ss_ref[smp, hd] + (1.0 - fb) * is_ref[smp:smp + 1, hs]
            sso_ref[smp, hd] = sn
            parts.append(jnp.sum(qs_ref[hd, :, smp:smp + 1] * sn, axis=0, keepdims=True))
        out_rows.append(jnp.concatenate(parts, axis=-1))
    os_ref[...] = jnp.concatenate(out_rows, axis=0)

    lb = _lower_bound(clb_ref[...])

    row = lax.broadcasted_iota(jnp.int32, (CHUNK, CHUNK), 0)
    col = lax.broadcasted_iota(jnp.int32, (CHUNK, CHUNK), 1)
    ltri = (row >= col).astype(BF16)

    def chunk_rows(c):
        return pl.ds(pl.multiple_of(c * CHUNK, CHUNK), CHUNK)

    def prefix(g, worst):
        rows = [chunk_rows(g * PREFIX_GROUP + i) for i in range(PREFIX_GROUP)]
        gates = [z_ref[r, C_F:2 * C_F] for r in rows]
        for r, gate in zip(rows, gates):
            f_all = lb + (1.0 - lb) * jax.nn.sigmoid(gate)
            k_ref[r, :] = 1.0 - f_all
            hi, mid, lo = _split3(jnp.log2(f_all))
            b2 = (_dot(ltri, hi) + _dot(ltri, mid)) + _dot(ltri, lo)
            z_ref[r, C_F:2 * C_F] = b2
            b_mid = b2[CHUNK // 2 - 1:CHUNK // 2, :]
            b_last = b2[CHUNK - 1:CHUNK, :]
            worst = jnp.maximum(worst, jnp.maximum(-b_mid, b_mid - b_last))
        return worst

    worst = lax.fori_loop(0, n_chunks // PREFIX_GROUP, prefix, jnp.zeros((1, C_F), F32))
    bounded = jnp.max(worst) <= SAFE_LOG2_RANGE

    def finish_head(rows, hd, o):
        gt = z_ref[rows, 2 * C_F + C_V + hd * C_VAL_DIM:2 * C_F + C_V + (hd + 1) * C_VAL_DIM]
        o = _rms(o, on_ref[...]) * jax.nn.sigmoid(gt)
        o_ref[rows, hd * C_VAL_DIM:(hd + 1) * C_VAL_DIM] = o.astype(BF16)

    def head_inputs(rows, hd):
        q = z_ref[rows, hd * C_KEY_DIM:(hd + 1) * C_KEY_DIM]
        kk = k_ref[rows, hd * C_KEY_DIM:(hd + 1) * C_KEY_DIM]
        b2 = z_ref[rows, C_F + hd * C_KEY_DIM:C_F + (hd + 1) * C_KEY_DIM]
        ivb = z_ref[rows, 2 * C_F + hd * C_VAL_DIM:2 * C_F + (hd + 1) * C_VAL_DIM].astype(BF16)
        return q, kk, b2, ivb

    def factored_chunk(c, carry):
        rows = chunk_rows(c)
        for hd in range(C_HEADS):
            q, kk, b2, ivb = head_inputs(rows, hd)
            b_mid = b2[CHUNK // 2 - 1:CHUNK // 2, :]
            b_last = b2[CHUNK - 1:CHUNK, :]
            qs = (q * jnp.exp2(b2 - b_mid)).astype(BF16)
            kd = (kk * jnp.exp2(b_mid - b2)).astype(BF16)
            att = jnp.where(row >= col, lax.dot_general(qs, kd, _NT, preferred_element_type=F32), 0.0)
            stt = stt_ref[hd]
            o = lax.dot_general(qs, (stt * jnp.exp2(b_mid)).astype(BF16), _NT,
                                preferred_element_type=F32) + _dot(att.astype(BF16), ivb)
            stt_ref[hd] = stt * jnp.exp2(b_last) + jnp.exp2(b_last - b_mid) * lax.dot_general(
                ivb, kd, _TN, preferred_element_type=F32)
            finish_head(rows, hd, o)
        return carry

    def tree_chunk(c, carry):
        rows = chunk_rows(c)
        lev = lev_ref[...]

        def products(hd):
            q, kk, b2, ivb = head_inputs(rows, hd)
            diag = jnp.sum(q * kk, axis=-1, keepdims=True)
            pms = []
            for p in range(n_levels):
                y = _level_operand(p, q, kk, 1.0 - kk, b2)
                pms.append(lax.dot_general(y, y, _NT, preferred_element_type=F32))
            stt = stt_ref[hd]
            o_prev = lax.dot_general((q * jnp.exp2(b2)).astype(BF16), stt.astype(BF16), _NT,
                                     preferred_element_type=F32)
            b_last = b2[CHUNK - 1:CHUNK, :]
            kd = (kk * jnp.exp2(b_last - b2)).astype(BF16)
            stt_ref[hd] = stt * jnp.exp2(b_last) + lax.dot_general(
                ivb, kd, _TN, preferred_element_type=F32)
            return diag, pms, o_prev, ivb

        def finish(hd, diag, pms, o_prev, ivb):
            att = jnp.where(lev == -1, diag, 0.0)
            for p in range(n_levels):
                att = _merge_level(p, att, pms[p], lev)
            finish_head(rows, hd, o_prev + _dot(att.astype(BF16), ivb))

        pending = [products(hd) for hd in range(HEAD_SKEW)]
        for hd in range(C_HEADS):
            if hd + HEAD_SKEW < C_HEADS:
                pending.append(products(hd + HEAD_SKEW))
            finish(hd, *pending.pop(0))
        return carry

    @pl.when(bounded)
    def _():
        lax.fori_loop(0, n_chunks, factored_chunk, 0, unroll=FACTORED_UNROLL)

    @pl.when(jnp.logical_not(bounded))
    def _():
        lax.fori_loop(0, n_chunks, tree_chunk, 0)

    y_ref[0] = x_ref[0] + _dot(o_ref[...], wout_ref[...])

    @pl.when(j == last_j)
    def _():
        for hd in range(C_HEADS):
            st_ref[0, hd] = stt_ref[hd].T


def _hgrn_prompt(x, nm, w_in, clb, on, w_out, state_s, f_s, q_s, i_s):
    nb, seq, _ = x.shape
    nj = seq // TQH
    ns = state_s.shape[0]
    sbh = ns // (nb * nj)
    assert sbh * nb * nj == ns
    blk = lambda b, j: (b, j, 0)
    step = lambda b, j: (b * nj + j, 0, 0)
    step4 = lambda b, j: (b * nj + j, 0, 0, 0)
    rows3 = lambda a: a.reshape(ns // sbh, sbh, a.shape[1])
    cols4 = lambda a: a.reshape(ns // sbh, sbh, C_HEADS, C_KEY_DIM).transpose(0, 2, 3, 1)
    y, st, st_s, o_s = pl.pallas_call(
        _hgrn_prompt_kernel,
        grid=(nb, nj),
        in_specs=[
            pl.BlockSpec((1, TQH, D_MODEL), blk),
            _full((1, D_MODEL)),
            _resident((D_MODEL, C_IN)),
            _full((DEPTH, C_F)),
            _full((1, C_VAL_DIM)),
            _resident((C_V, D_MODEL)),
            _full((CHUNK, CHUNK)),
            pl.BlockSpec((sbh, C_HEADS, C_KEY_DIM, C_VAL_DIM), step4),
            pl.BlockSpec((None, C_HEADS, C_KEY_DIM, sbh), step4),
            pl.BlockSpec((None, C_HEADS, C_KEY_DIM, sbh), step4),
            pl.BlockSpec((None, sbh, C_V), step),
        ],
        out_specs=[
            pl.BlockSpec((1, TQH, D_MODEL), blk),
            pl.BlockSpec((1, C_HEADS, C_KEY_DIM, C_VAL_DIM), lambda b, j: (b, 0, 0, 0)),
            pl.BlockSpec((sbh, C_HEADS, C_KEY_DIM, C_VAL_DIM), step4),
            pl.BlockSpec((None, sbh, C_V), step),
        ],
        out_shape=[
            jax.ShapeDtypeStruct((nb, seq, D_MODEL), F32),
            jax.ShapeDtypeStruct((nb, C_HEADS, C_KEY_DIM, C_VAL_DIM), F32),
            jax.ShapeDtypeStruct(state_s.shape, F32),
            jax.ShapeDtypeStruct((ns // sbh, sbh, C_V), F32),
        ],
        scratch_shapes=[
            pltpu.VMEM((TQH, C_IN), F32),
            pltpu.VMEM((TQH, C_V), BF16),
            pltpu.VMEM((C_HEADS, C_VAL_DIM, C_KEY_DIM), F32),
            pltpu.VMEM((TQH, C_F), F32),
        ],
        compiler_params=pltpu.CompilerParams(
            dimension_semantics=("arbitrary", "arbitrary"), vmem_limit_bytes=VMEM_LIMIT),
        name="hgrn_prompt",
    )(x, nm, w_in, clb, on, w_out, jnp.asarray(_pair_level_table()),
      state_s, cols4(f_s), cols4(q_s), rows3(i_s))
    return y, st, st_s, o_s.reshape(ns, C_V)


def _ab_sample_proj_kernel(w00_ref, b0_ref, x_ref, nm_ref, win_ref, qn_ref, kn_ref, lng_ref, lnb_ref,
                           qx_ref, knew_ref, vnew_ref, bm_ref, gv_ref):
    n = x_ref.shape[0]
    h = _rms(x_ref[...], nm_ref[...]).astype(BF16)
    z = _dot(h, win_ref[...])
    zeros = jnp.zeros((n, A_HEAD_DIM), F32)
    for hh in range(A_HEADS):
        qh = _rms(z[:, hh * A_HEAD_DIM:(hh + 1) * A_HEAD_DIM], qn_ref[...]) * ATTN_SCALE
        qx_ref[hh] = jnp.concatenate([qh, zeros] if hh // A_GROUP == 0 else [zeros, qh], axis=-1)
    kparts = []
    for g in range(A_KV_HEADS):
        kparts.append(_rms(z[:, A_Q + g * A_HEAD_DIM:A_Q + (g + 1) * A_HEAD_DIM], kn_ref[...]))
    knew_ref[...] = jnp.concatenate(kparts, axis=-1)
    vnew_ref[...] = z[:, A_Q + A_KV:A_Q + 2 * A_KV]

    u = _gelu(z[:, A_Q + 2 * A_KV:A_Q + 2 * A_KV + B_WIDTH])
    vln = _layernorm(_gelu(z[:, A_Q + 2 * A_KV + B_WIDTH:AB_IN]), lng_ref[...], lnb_ref[...])
    grp = lax.broadcasted_iota(jnp.int32, (1, B_WIDTH), 1) // B_GROUP_DIM
    srow = jnp.zeros((1, B_WIDTH), F32)
    brow = jnp.zeros((1, B_WIDTH), F32)
    for g in range(B_GROUPS):
        srow = jnp.where(grp == g, w00_ref[g], srow)
        brow = jnp.where(grp == g, b0_ref[g], brow)
    bm_ref[...] = u * (vln * srow + brow)
    gv_ref[...] = vln


def _ab_sample_proj(x, nm, w_in, qn, kn, lng, lnb, w00, b0):
    n = x.shape[0]
    return pl.pallas_call(
        _ab_sample_proj_kernel,
        in_specs=[_SMEM, _SMEM] + [pl.BlockSpec(memory_space=pltpu.VMEM)] * 7,
        out_shape=[
            jax.ShapeDtypeStruct((A_HEADS, n, A_KV), F32),
            jax.ShapeDtypeStruct((n, A_KV), F32),
            jax.ShapeDtypeStruct((n, A_KV), F32),
            jax.ShapeDtypeStruct((n, B_WIDTH), F32),
            jax.ShapeDtypeStruct((n, B_WIDTH), F32),
        ],
        compiler_params=pltpu.CompilerParams(vmem_limit_bytes=VMEM_LIMIT),
        name="ab_sample_proj",
    )(w00, b0, x, nm, w_in, qn, kn, lng, lnb)


def _residual_proj_kernel(x_ref, m_ref, w_ref, y_ref):
    y_ref[...] = x_ref[...] + _dot(m_ref[...].astype(BF16), w_ref[...])


def _residual_proj(x, mix, w):
    return pl.pallas_call(
        _residual_proj_kernel,
        out_shape=jax.ShapeDtypeStruct(x.shape, F32),
        name="residual_proj",
    )(x, mix, w)


def _hgrn_sample_proj_kernel(x_ref, nm_ref, win_ref, clb_ref, q_ref, f_ref, i_ref, sg_ref):
    h = _rms(x_ref[...], nm_ref[...]).astype(BF16)
    z = _dot(h, win_ref[...])
    lb = _lower_bound(clb_ref[...])
    q_ref[...] = z[:, 0:C_F]
    f_ref[...] = lb + (1.0 - lb) * jax.nn.sigmoid(z[:, C_F:2 * C_F])
    i_ref[...] = z[:, 2 * C_F:2 * C_F + C_V]
    sg_ref[...] = jax.nn.sigmoid(z[:, 2 * C_F + C_V:C_IN])


def _hgrn_sample_proj(x, nm, w_in, clb):
    n = x.shape[0]
    return pl.pallas_call(
        _hgrn_sample_proj_kernel,
        out_shape=[jax.ShapeDtypeStruct((n, C_F), F32)] * 2 + [jax.ShapeDtypeStruct((n, C_V), F32)] * 2,
        compiler_params=pltpu.CompilerParams(vmem_limit_bytes=VMEM_LIMIT),
        name="hgrn_sample_proj",
    )(x, nm, w_in, clb)


def _hgrn_sample_out_kernel(o_ref, sg_ref, on_ref, w_ref, x_ref, y_ref):
    parts = []
    for hd in range(C_HEADS):
        parts.append(_rms(o_ref[:, hd * C_VAL_DIM:(hd + 1) * C_VAL_DIM], on_ref[...]))
    on = jnp.concatenate(parts, axis=-1) * sg_ref[...]
    y_ref[...] = x_ref[...] + _dot(on.astype(BF16), w_ref[...])


def _hgrn_sample_out(o, sg, on, w_out, x):
    return pl.pallas_call(
        _hgrn_sample_out_kernel,
        out_shape=jax.ShapeDtypeStruct(x.shape, F32),
        name="hgrn_sample_out",
    )(o, sg, on, w_out, x)


def kernel(x_prompt, x_sample, cache_k, cache_v, state_hgrn, norm_mix, norm_ffn, w_in_ab, w_out_ab,
           q_norm, k_norm, attn_sink, rel_bias, gmlp_ln_g, gmlp_ln_b, gmlp_w_s, gmlp_b_s,
           w_in_c, c_lower_bounds, c_out_norm, w_out_c, w_gate, w_up, w_down):
    assert norm_mix.shape[0] == DEPTH == 2 and w_in_ab.shape[0] == 1 and w_in_c.shape[0] == 1
    nb, seq, _ = x_prompt.shape
    ns = x_sample.shape[0]
    assert x_sample.shape[1] == 1 and cache_k.shape[2] == WINDOW

    row = lambda v: v.reshape(1, -1)
    bf = lambda w: w.astype(BF16)
    w_in_ab0, w_out_ab0 = bf(w_in_ab[0]), bf(w_out_ab[0])
    w_in_c0, w_out_c0 = bf(w_in_c[0]), bf(w_out_c[0])
    nm, nf = norm_mix, norm_ffn
    qn, kn = row(q_norm[0]), row(k_norm[0])
    lng, lnb = row(gmlp_ln_g[0]), row(gmlp_ln_b[0])
    sink = attn_sink[0]

    tab, tabp = _bias_table(rel_bias)

    xs = x_sample.reshape(ns, D_MODEL)
    qx, knew_s, vnew_s, bm_s, gv_s = _ab_sample_proj(
        xs, row(nm[0]), w_in_ab0, qn, kn, lng, lnb, gmlp_w_s[0, :, 0, 0], gmlp_b_s[0, :, 0])
    sb = jnp.pad(tab[:, WINDOW - 1, WINDOW - 1:], ((0, 0), (0, WINDOW - 1)))
    xp, knew_p, vnew_p, gv_p, nk_s, nv_s, om = _ab_prompt(
        x_prompt, row(nm[0]), w_in_ab0, jnp.tile(qn, (1, A_HEADS)), jnp.tile(kn, (1, A_KV_HEADS)),
        sink, tabp, lng, lnb, gmlp_w_s[0], jnp.repeat(gmlp_b_s[0].T, B_GROUP_DIM, axis=1), w_out_ab0,
        cache_k[0].reshape(ns, WINDOW, A_KV), cache_v[0].reshape(ns, WINDOW, A_KV),
        qx.transpose(1, 0, 2), knew_s[:, None, :], vnew_s[:, None, :], sb)
    om = om.reshape(ns, A_KV_HEADS, A_GROUP, A_KV_HEADS, A_HEAD_DIM)
    a_s = jnp.stack([om[:, g, :, g, :] for g in range(A_KV_HEADS)], axis=1).reshape(ns, A_Q)
    xs = _residual_proj(xs, jnp.concatenate([a_s, bm_s], axis=-1), w_out_ab0)
    xp, xs = _ffn(xp.reshape(nb * seq, D_MODEL), xs, row(nf[0]), w_gate, w_up, w_down, 0)

    q_s, f_s, i_s, sg_s = _hgrn_sample_proj(xs, row(nm[1]), w_in_c0, c_lower_bounds)
    xp, st_p, st_s, o_s = _hgrn_prompt(xp.reshape(nb, seq, D_MODEL), row(nm[1]), w_in_c0, c_lower_bounds,
                                       row(c_out_norm[0]), w_out_c0, state_hgrn[0], f_s, q_s, i_s)
    xs = _hgrn_sample_out(o_s, sg_s, row(c_out_norm[0]), w_out_c0, xs)
    xp, xs = _ffn(xp.reshape(nb * seq, D_MODEL), xs, row(nf[1]), w_gate, w_up, w_down, 1)

    kv5 = lambda a: a.reshape(1, a.shape[0], WINDOW, A_KV_HEADS, A_HEAD_DIM)
    return (xp.reshape(nb, seq, D_MODEL), xs.reshape(ns, 1, D_MODEL),
            kv5(knew_p), kv5(vnew_p), kv5(nk_s), kv5(nv_s),
            gv_p[None], gv_s.reshape(1, ns, 1, B_WIDTH),
            st_p[None], st_s[None])
```

```python
import functools
import math

import jax
import jax.numpy as jnp
import numpy as np
from jax import lax
from jax.experimental import pallas as pl
from jax.experimental.pallas import tpu as pltpu

F32 = jnp.float32
BF16 = jnp.bfloat16

D_MODEL = 1024
DEPTH = 2
A_HEADS = 8
A_KV_HEADS = 2
A_GROUP = A_HEADS // A_KV_HEADS
A_HEAD_DIM = 64
WINDOW = 128
ATTN_SCALE = A_HEAD_DIM ** -0.5
NUM_BUCKETS = 32
MAX_DISTANCE = 128
A_Q = A_HEADS * A_HEAD_DIM
A_KV = A_KV_HEADS * A_HEAD_DIM
B_GROUPS = 8
B_GROUP_DIM = 64
B_WIDTH = B_GROUPS * B_GROUP_DIM
B_CHUNK = 128
AB_IN = A_Q + 2 * A_KV + 2 * B_WIDTH
AB_MIX = A_Q + B_WIDTH
C_HEADS = 8
C_KEY_DIM = 128
C_VAL_DIM = 128
C_F = C_HEADS * C_KEY_DIM
C_V = C_HEADS * C_VAL_DIM
C_IN = 2 * C_F + 2 * C_V
D_FF = 2816
EPS = 1e-6

NEG = -1e30

VMEM_LIMIT = 56 * 1024 * 1024
VREG_ROWS = 8

CHUNK = 128
TQ = 1024
TQH = 512
TM = 512
HEAD_SKEW = 2
SAFE_LOG2_RANGE = 64.0
PREFIX_GROUP = 4
FACTORED_UNROLL = 4
AB_UNROLL = 2

_NT = (((1,), (1,)), ((), ()))
_TN = (((0,), (0,)), ((), ()))


def _rms(x, g):
    return x * lax.rsqrt(jnp.mean(x * x, axis=-1, keepdims=True) + EPS) * g


def _gelu(x):
    return 0.5 * x * (1.0 + lax.erf(x * math.sqrt(0.5)))


def _layernorm(x, g, b):
    xc = x - jnp.mean(x, axis=-1, keepdims=True)
    return xc * lax.rsqrt(jnp.mean(xc * xc, axis=-1, keepdims=True) + EPS) * g + b


def _dot(a, b):
    return jnp.dot(a, b, preferred_element_type=F32)


def _full(shape):
    n = len(shape)
    return pl.BlockSpec(shape, lambda *_: (0,) * n)


def _resident(shape):
    n = len(shape)
    return pl.BlockSpec(shape, lambda *_: (0,) * n, pipeline_mode=pl.Buffered(1))


_SMEM = pl.BlockSpec(memory_space=pltpu.SMEM)


def _bias_table_kernel(rel_ref, tab_ref, tabp_ref):
    qi = lax.broadcasted_iota(jnp.int32, (WINDOW, 2 * WINDOW), 0)
    kj = lax.broadcasted_iota(jnp.int32, (WINDOW, 2 * WINDOW), 1)
    dist = qi + WINDOW - kj
    ok = (dist >= 0) & (dist < WINDOW)
    max_exact = NUM_BUCKETS // 2
    d = jnp.maximum(dist, 0)
    dl = jnp.maximum(d, 1).astype(F32)
    v = (jnp.log(dl / max_exact) / math.log(MAX_DISTANCE / max_exact) * (NUM_BUCKETS - max_exact))
    far = d >= max_exact
    hits = []
    for b in range(NUM_BUCKETS):
        if b < max_exact:
            hits.append(d == b)
        elif b < NUM_BUCKETS - 1:
            hits.append(far & (v >= b - max_exact) & (v < b - max_exact + 1))
        else:
            hits.append(far & (v >= b - max_exact))
    for h in range(A_HEADS):
        acc = jnp.zeros((WINDOW, 2 * WINDOW), F32)
        for b in range(NUM_BUCKETS):
            acc = jnp.where(hits[b], rel_ref[b, h], acc)
        t = jnp.where(ok, acc, NEG)
        tab_ref[h] = t
        cols = slice((h % 2) * 2 * WINDOW, (h % 2 + 1) * 2 * WINDOW)
        tabp_ref[0, h // 2, :, cols] = t
        tabp_ref[1, h // 2, :, cols] = jnp.where(kj < WINDOW, NEG, t)


def _bias_table(rel_bias):
    return pl.pallas_call(
        _bias_table_kernel,
        out_shape=[
            jax.ShapeDtypeStruct((A_HEADS, WINDOW, 2 * WINDOW), F32),
            jax.ShapeDtypeStruct((2, A_HEADS // 2, WINDOW, 4 * WINDOW), F32),
        ],
        in_specs=[_SMEM],
        name="bias_table",
    )(rel_bias)


PAIR = 2 * A_HEAD_DIM
N_PAIRS = A_HEADS // 2


def _sample_cache_attention(ck_ref, cv_ref, qx_ref, kn_ref, vn_ref, knc_ref, vnc_ref, sb_ref, sink_ref,
                            nk_ref, nv_ref, om_ref):
    wb = ck_ref.shape[2]
    head = lax.broadcasted_iota(jnp.int32, (1, A_HEADS, A_KV), 1)
    lane = lax.broadcasted_iota(jnp.int32, (1, A_HEADS, A_KV), 2)
    own_group = (head // A_GROUP) == (lane // A_HEAD_DIM)
    newest = lax.broadcasted_iota(jnp.int32, (1, 1, wb), 2) == wb - 1
    sink = sink_ref[...][None]
    kc, vc = ck_ref[...], cv_ref[...]
    kn, vn = kn_ref[...], vn_ref[...]
    for i in range(kc.shape[0]):
        nk_ref[i] = jnp.where(newest[0], knc_ref[:, i:i + 1], pltpu.roll(kc[i], wb - 1, 1))
        nv_ref[i] = jnp.where(newest[0], vnc_ref[:, i:i + 1], pltpu.roll(vc[i], wb - 1, 1))
    q = qx_ref[...]
    s = jnp.einsum('bhd,bdk->bhk', q.astype(BF16), kc.astype(BF16), preferred_element_type=F32)
    s = s + sb_ref[:, 0:wb][None]
    sn = jnp.sum(q * kn, axis=-1, keepdims=True) + sb_ref[:, wb:wb + 1][None]
    m = jnp.maximum(jnp.maximum(jnp.max(s, axis=-1, keepdims=True), sn), sink)
    e = jnp.exp(s - m)
    en = jnp.exp(sn - m)
    r = 1.0 / (jnp.sum(e, axis=-1, keepdims=True) + en + jnp.exp(sink - m))
    o = jnp.einsum('bhk,bdk->bhd', (e * r).astype(BF16), vc.astype(BF16),
                   preferred_element_type=F32) + (en * r) * vn
    om_ref[...] = jnp.where(own_group, o, 0.0)


def _ab_prompt_kernel(sink_ref, x_ref, nm_ref, win_ref, qg_ref, kg_ref, tabp_ref, lng_ref, lnb_ref,
                      ws_ref, bsp_ref, wout_ref,
                      ck_ref, cv_ref, qx_ref, kn_ref, vn_ref, knc_ref, vnc_ref, sb_ref, sinkc_ref,
                      y_ref, knew_ref, vnew_ref, gv_ref, nk_ref, nv_ref, om_ref,
                      z_ref, mix_ref, q_ref, k_ref, kr_ref, v_ref, vr_ref, wpair_ref, kl_ref, vl_ref, gl_ref):
    j = pl.program_id(1)
    last_j = pl.num_programs(1) - 1
    n_chunks = TQ // CHUNK

    @pl.when(j == 0)
    def _():
        for ref in (k_ref, kr_ref, v_ref, vr_ref):
            ref[0:CHUNK, :] = jnp.zeros((CHUNK, A_KV), BF16)
        row = lax.broadcasted_iota(jnp.int32, (B_CHUNK, B_CHUNK), 0)
        col = lax.broadcasted_iota(jnp.int32, (B_CHUNK, B_CHUNK), 1)
        for g in range(B_GROUPS):
            wpair_ref[g // 2, :, (g % 2) * B_CHUNK:(g % 2 + 1) * B_CHUNK] = jnp.where(
                row >= col, ws_ref[g], 0.0).astype(BF16)

    h = _rms(x_ref[0], nm_ref[...]).astype(BF16)
    z_ref[...] = _dot(h, win_ref[...])

    _sample_cache_attention(ck_ref, cv_ref, qx_ref, kn_ref, vn_ref, knc_ref, vnc_ref, sb_ref, sinkc_ref,
                            nk_ref, nv_ref, om_ref)

    lo_half = lax.broadcasted_iota(jnp.int32, (1, PAIR), 1) < A_HEAD_DIM
    r_i = lax.broadcasted_iota(jnp.int32, (PAIR, PAIR), 0) // A_HEAD_DIM
    c_i = lax.broadcasted_iota(jnp.int32, (PAIR, PAIR), 1) // A_HEAD_DIM
    half_mean = jnp.where(r_i == c_i, 1.0 / A_HEAD_DIM, 0.0).astype(BF16)

    def mean_sq_halves(x):
        x2 = x * x
        hi = x2.astype(BF16)
        lo = (x2 - hi.astype(F32)).astype(BF16)
        return _dot(hi, half_mean) + _dot(lo, half_mean)

    def block_diag(top, bot):
        zero = jnp.zeros_like(top)
        return jnp.concatenate([jnp.where(lo_half, top, zero), jnp.where(lo_half, zero, bot)], axis=0)

    kraw = z_ref[:, A_Q:A_Q + A_KV]
    v_all = z_ref[:, A_Q + A_KV:A_Q + 2 * A_KV]
    kn_all = kraw * lax.rsqrt(mean_sq_halves(kraw) + EPS) * kg_ref[...]
    k_ref[CHUNK:, :] = kn_all.astype(BF16)
    kr_ref[CHUNK:, :] = pltpu.roll(kn_all, A_HEAD_DIM, 1).astype(BF16)
    v_ref[CHUNK:, :] = v_all.astype(BF16)
    vr_ref[CHUNK:, :] = pltpu.roll(v_all, A_HEAD_DIM, 1).astype(BF16)
    kl_ref[...] = kn_all[TQ - CHUNK:, :]
    vl_ref[...] = v_all[TQ - CHUNK:, :]
    for i in range(N_PAIRS):
        ps = slice(i * PAIR, (i + 1) * PAIR)
        qraw = z_ref[:, ps]
        qn = qraw * lax.rsqrt(mean_sq_halves(qraw) + EPS) * (qg_ref[:, ps] * ATTN_SCALE)
        q_ref[:, ps] = qn.astype(BF16)

    def chunk(c, carry):
        r0 = pl.multiple_of(c * CHUNK, CHUNK)
        rows = pl.ds(r0, CHUNK)
        first = jnp.where(jnp.logical_and(j == 0, c == 0), 1, 0)

        both = pl.ds(r0, 2 * CHUNK)
        k2, k2r, v2, v2r = k_ref[both, :], kr_ref[both, :], v_ref[both, :], vr_ref[both, :]
        kbd = [block_diag(k2, k2r), block_diag(k2r, k2)]
        vbd = [block_diag(v2, v2r), block_diag(v2r, v2)]

        scores = []
        for i in range(N_PAIRS):
            s = lax.dot_general(q_ref[rows, i * PAIR:(i + 1) * PAIR], kbd[i // (A_GROUP // 2)], _NT,
                                preferred_element_type=F32)
            scores.append(s + tabp_ref[first, i])
        outs = []
        for i in range(N_PAIRS):
            es, rs = [], []
            for hh in range(2):
                sh = scores[i][:, hh * 2 * WINDOW:(hh + 1) * 2 * WINDOW]
                sk = sink_ref[2 * i + hh]
                m = jnp.maximum(jnp.max(sh, axis=-1, keepdims=True), sk)
                e = jnp.exp(sh - m)
                rs.append(1.0 / (jnp.sum(e, axis=-1, keepdims=True) + jnp.exp(sk - m)))
                es.append(e.astype(BF16))
            o = _dot(jnp.concatenate(es, axis=-1), vbd[i // (A_GROUP // 2)])
            outs.append(o * jnp.where(lo_half, rs[0], rs[1]))
        mix_ref[rows, 0:A_Q] = jnp.concatenate(outs, axis=-1).astype(BF16)

        zu = z_ref[rows, A_Q + 2 * A_KV:A_Q + 2 * A_KV + B_WIDTH]
        zv = z_ref[rows, A_Q + 2 * A_KV + B_WIDTH:AB_IN]
        u = _gelu(zu)
        vln = _layernorm(_gelu(zv), lng_ref[...], lnb_ref[...])
        vlb = vln.astype(BF16)
        sparts = []
        for i in range(B_GROUPS // 2):
            vpair = vlb[:, i * PAIR:(i + 1) * PAIR]
            sparts.append(_dot(wpair_ref[i], block_diag(vpair, vpair)))
        bm = u * (jnp.concatenate(sparts, axis=-1) + bsp_ref[...])
        mix_ref[rows, A_Q:AB_MIX] = bm.astype(BF16)

        gl_ref[...] = vln
        return carry

    lax.fori_loop(0, n_chunks, chunk, 0, unroll=AB_UNROLL)
    y_ref[0] = x_ref[0] + _dot(mix_ref[...], wout_ref[...])
    for ref in (k_ref, kr_ref, v_ref, vr_ref):
        ref[0:CHUNK, :] = ref[TQ:TQ + CHUNK, :]

    @pl.when(j == last_j)
    def _():
        knew_ref[0] = kl_ref[...]
        vnew_ref[0] = vl_ref[...]
        gv_ref[0] = gl_ref[...]


def _ab_prompt(x, nm, w_in, qg, kg, sink, tabp, lng, lnb, w_s, bsp, w_out, ck, cv, qx, kn_s, vn_s, sb):
    nb, seq, _ = x.shape
    nj = seq // TQ
    grid = (nb, nj)
    ns, _, wb = ck.shape
    sba = ns // (nb * nj)
    assert sba * nb * nj == ns
    blk = lambda b, j: (b, j, 0)
    per_b = lambda b, j: (b, 0, 0)
    step = lambda b, j: (b * nj + j, 0, 0)
    cols = lambda a: a.reshape(ns // sba, sba, A_KV).transpose(0, 2, 1)
    return pl.pallas_call(
        _ab_prompt_kernel,
        grid=grid,
        in_specs=[
            _SMEM,
            pl.BlockSpec((1, TQ, D_MODEL), blk),
            _full((1, D_MODEL)),
            _resident((D_MODEL, AB_IN)),
            _full((1, A_Q)),
            _full((1, A_KV)),
            _resident((2, N_PAIRS, WINDOW, 4 * WINDOW)),
            _full((1, B_WIDTH)),
            _full((1, B_WIDTH)),
            _resident((B_GROUPS, B_CHUNK, B_CHUNK)),
            _resident((B_CHUNK, B_WIDTH)),
            _resident((AB_MIX, D_MODEL)),
            pl.BlockSpec((sba, A_KV, wb), step),
            pl.BlockSpec((sba, A_KV, wb), step),
            pl.BlockSpec((sba, A_HEADS, A_KV), step),
            pl.BlockSpec((sba, 1, A_KV), step),
            pl.BlockSpec((sba, 1, A_KV), step),
            pl.BlockSpec((None, A_KV, sba), step),
            pl.BlockSpec((None, A_KV, sba), step),
            _full((A_HEADS, 2 * WINDOW)),
            _full((A_HEADS, 1)),
        ],
        out_specs=[
            pl.BlockSpec((1, TQ, D_MODEL), blk),
            pl.BlockSpec((1, WINDOW, A_KV), per_b),
            pl.BlockSpec((1, WINDOW, A_KV), per_b),
            pl.BlockSpec((1, B_CHUNK, B_WIDTH), per_b),
            pl.BlockSpec((sba, A_KV, wb), step),
            pl.BlockSpec((sba, A_KV, wb), step),
            pl.BlockSpec((sba, A_HEADS, A_KV), step),
        ],
        out_shape=[
            jax.ShapeDtypeStruct((nb, seq, D_MODEL), F32),
            jax.ShapeDtypeStruct((nb, WINDOW, A_KV), F32),
            jax.ShapeDtypeStruct((nb, WINDOW, A_KV), F32),
            jax.ShapeDtypeStruct((nb, B_CHUNK, B_WIDTH), F32),
            jax.ShapeDtypeStruct((ns, A_KV, wb), F32),
            jax.ShapeDtypeStruct((ns, A_KV, wb), F32),
            jax.ShapeDtypeStruct((ns, A_HEADS, A_KV), F32),
        ],
        scratch_shapes=[
            pltpu.VMEM((TQ, AB_IN), F32),
            pltpu.VMEM((TQ, AB_MIX), BF16),
            pltpu.VMEM((TQ, A_Q), BF16),
            pltpu.VMEM((CHUNK + TQ, A_KV), BF16),
            pltpu.VMEM((CHUNK + TQ, A_KV), BF16),
            pltpu.VMEM((CHUNK + TQ, A_KV), BF16),
            pltpu.VMEM((CHUNK + TQ, A_KV), BF16),
            pltpu.VMEM((B_GROUPS // 2, B_CHUNK, 2 * B_CHUNK), BF16),
            pltpu.VMEM((WINDOW, A_KV), F32),
            pltpu.VMEM((WINDOW, A_KV), F32),
            pltpu.VMEM((B_CHUNK, B_WIDTH), F32),
        ],
        compiler_params=pltpu.CompilerParams(
            dimension_semantics=("arbitrary", "arbitrary"), vmem_limit_bytes=VMEM_LIMIT),
        name="ab_prompt",
    )(sink, x, nm, w_in, qg, kg, tabp, lng, lnb, w_s, bsp, w_out,
      ck, cv, qx, kn_s[:, None, :], vn_s[:, None, :], cols(kn_s), cols(vn_s), sb, sink.reshape(A_HEADS, 1))


FF_TILE = 256


def _ffn_kernel(xp_ref, xs_ref, g_ref, wg_ref, wu_ref, wd_ref, yp_ref, ys_ref,
                wg_s, wu_s, wd_s, h0_s, acc_s, *, n_cast, n_prompt):
    s = pl.program_id(0)

    def gated(h, wg, wu):
        gate = _dot(h, wg)
        return (gate * jax.nn.sigmoid(gate) * _dot(h, wu)).astype(BF16)

    @pl.when(s == 0)
    def _():
        x = xp_ref[...]
        h0_s[...] = _rms(x, g_ref[...]).astype(BF16)
        acc_s[...] = x

    for c in range(n_cast):
        @pl.when(s == c)
        def _(c=c):
            tile = slice(c * FF_TILE, (c + 1) * FF_TILE)
            wg_t, wu_t, wd_t = (r[...].astype(BF16) for r in (wg_ref, wu_ref, wd_ref))
            wg_s[:, tile] = wg_t
            wu_s[:, tile] = wu_t
            wd_s[tile, :] = wd_t
            acc_s[...] += _dot(gated(h0_s[...], wg_t, wu_t), wd_t)

    @pl.when(s == n_cast - 1)
    def _():
        yp_ref[...] = acc_s[...]

    def swiglu(x):
        h = _rms(x, g_ref[...]).astype(BF16)
        return x + _dot(gated(h, wg_s[...], wu_s[...]), wd_s[...])

    @pl.when(jnp.logical_and(s >= n_cast, s < n_cast + n_prompt - 1))
    def _():
        yp_ref[...] = swiglu(xp_ref[...])

    @pl.when(s == n_cast + n_prompt - 1)
    def _():
        ys_ref[...] = swiglu(xs_ref[...])


def _ffn(xp, xs, g, w_gate, w_up, w_down, layer):
    rows, ns = xp.shape[0], xs.shape[0]
    n_cast, n_prompt = D_FF // FF_TILE, rows // TM
    w_tile = lambda s: jnp.minimum(s, n_cast - 1)
    row_blk = lambda s: (jnp.clip(s - (n_cast - 1), 0, n_prompt - 1), 0)
    return pl.pallas_call(
        functools.partial(_ffn_kernel, n_cast=n_cast, n_prompt=n_prompt),
        grid=(n_cast + n_prompt,),
        in_specs=[
            pl.BlockSpec((TM, D_MODEL), row_blk),
            _full((ns, D_MODEL)),
            _full((1, D_MODEL)),
            pl.BlockSpec((None, D_MODEL, FF_TILE), lambda s: (layer, 0, w_tile(s))),
            pl.BlockSpec((None, D_MODEL, FF_TILE), lambda s: (layer, 0, w_tile(s))),
            pl.BlockSpec((None, FF_TILE, D_MODEL), lambda s: (layer, w_tile(s), 0)),
        ],
        out_specs=[pl.BlockSpec((TM, D_MODEL), row_blk), _full((ns, D_MODEL))],
        out_shape=[jax.ShapeDtypeStruct((rows, D_MODEL), F32), jax.ShapeDtypeStruct((ns, D_MODEL), F32)],
        scratch_shapes=[
            pltpu.VMEM((D_MODEL, D_FF), BF16),
            pltpu.VMEM((D_MODEL, D_FF), BF16),
            pltpu.VMEM((D_FF, D_MODEL), BF16),
            pltpu.VMEM((TM, D_MODEL), BF16),
            pltpu.VMEM((TM, D_MODEL), F32),
        ],
        compiler_params=pltpu.CompilerParams(
            dimension_semantics=("arbitrary",), vmem_limit_bytes=VMEM_LIMIT),
        name="ffn",
    )(xp, xs, g, w_gate, w_up, w_down)


def _lower_bound(clb):
    m = jnp.max(clb, axis=0, keepdims=True)
    e = jnp.exp(clb - m)
    sm = e / jnp.sum(e, axis=0, keepdims=True)
    return (sm[0:1] + sm[1:2]) - sm[0:1]


def _split3(x):
    hi = x.astype(BF16)
    r = x - hi.astype(F32)
    mid = r.astype(BF16)
    lo = (r - mid.astype(F32)).astype(BF16)
    return hi, mid, lo


def _neg_abs(x):
    return lax.bitcast_convert_type(
        lax.bitcast_convert_type(x, jnp.uint32) | jnp.uint32(0x80000000), F32)


def _pair_level_table():
    t = np.arange(CHUNK)[:, None]
    s = np.arange(CHUNK)[None, :]
    lev = np.floor(np.log2(np.maximum(t ^ s, 1))).astype(np.int32)
    lev = np.where(t == s, -1, lev)
    return np.where(s > t, -2, lev).astype(np.int32)


def _level_operand(p, q, kk, f, b2):
    m = 2 ** p
    if m < VREG_ROWS:
        shape3 = (CHUNK // VREG_ROWS, VREG_ROWS, q.shape[1])
        sub = lax.broadcasted_iota(jnp.int32, (1, VREG_ROWS, q.shape[1]), 1)
        upper = ((sub >> p) & 1) == 1
        q3, k3 = q.reshape(shape3), kk.reshape(shape3)
        if p == 0:
            y = jnp.where(upper, q3 * f.reshape(shape3), k3)
        else:
            b3 = b2.reshape(shape3)
            be = b3[:, m - 1:m, :]
            for k in range(1, VREG_ROWS // (2 * m)):
                be = jnp.where(sub >= 2 * m * k, b3[:, 2 * m * k + m - 1:2 * m * k + m, :], be)
            y = jnp.where(upper, q3, k3) * jnp.exp2(_neg_abs(b3 - be))
        return y.reshape(q.shape).astype(BF16)
    parts = []
    for k in range(CHUNK // (2 * m)):
        lo = slice(2 * m * k, 2 * m * k + m)
        up = slice(2 * m * k + m, 2 * m * (k + 1))
        be = b2[2 * m * k + m - 1:2 * m * k + m, :]
        parts.append(kk[lo] * jnp.exp2(be - b2[lo]))
        parts.append(q[up] * jnp.exp2(b2[up] - be))
    return jnp.concatenate(parts, axis=0).astype(BF16)


def _merge_level(p, att, pm, lev):
    m = 2 ** p
    if m < VREG_ROWS:
        return jnp.where(lev == p, pm, att)
    col = lax.broadcasted_iota(jnp.int32, (1, CHUNK), 1)
    parts = []
    for k in range(CHUNK // (2 * m)):
        lo = slice(2 * m * k, 2 * m * k + m)
        up = slice(2 * m * k + m, 2 * m * (k + 1))
        parts.append(att[lo])
        parts.append(jnp.where((col >= 2 * m * k) & (col < 2 * m * k + m), pm[up], att[up]))
    return jnp.concatenate(parts, axis=0)


def _hgrn_prompt_kernel(x_ref, nm_ref, win_ref, clb_ref, on_ref, wout_ref, lev_ref,
                        ss_ref, fs_ref, qs_ref, is_ref,
                        y_ref, st_ref, sso_ref, os_ref,
                        z_ref, o_ref, stt_ref, k_ref):
    j = pl.program_id(1)
    last_j = pl.num_programs(1) - 1
    n_chunks = TQH // CHUNK
    n_levels = int(math.log2(CHUNK))

    @pl.when(j == 0)
    def _():
        stt_ref[...] = jnp.zeros_like(stt_ref)

    h = _rms(x_ref[0], nm_ref[...]).astype(BF16)
    z_ref[...] = _dot(h, win_ref[...])

    out_rows = []
    for smp in range(ss_ref.shape[0]):
        parts = []
        for hd in range(C_HEADS):
            hs = slice(hd * C_VAL_DIM, (hd + 1) * C_VAL_DIM)
            fb = jnp.broadcast_to(fs_ref[hd, :, smp:smp + 1], (C_KEY_DIM, C_VAL_DIM))
            sn = fb * ss_ref[smp, hd] + (1.0 - fb) * is_ref[smp:smp + 1, hs]
            sso_ref[smp, hd] = sn
            parts.append(jnp.sum(qs_ref[hd, :, smp:smp + 1] * sn, axis=0, keepdims=True))
        out_rows.append(jnp.concatenate(parts, axis=-1))
    os_ref[...] = jnp.concatenate(out_rows, axis=0)

    lb = _lower_bound(clb_ref[...])

    row = lax.broadcasted_iota(jnp.int32, (CHUNK, CHUNK), 0)
    col = lax.broadcasted_iota(jnp.int32, (CHUNK, CHUNK), 1)
    ltri = (row >= col).astype(BF16)

    def chunk_rows(c):
        return pl.ds(pl.multiple_of(c * CHUNK, CHUNK), CHUNK)

    def prefix(g, worst):
        rows = [chunk_rows(g * PREFIX_GROUP + i) for i in range(PREFIX_GROUP)]
        gates = [z_ref[r, C_F:2 * C_F] for r in rows]
        for r, gate in zip(rows, gates):
            f_all = lb + (1.0 - lb) * jax.nn.sigmoid(gate)
            k_ref[r, :] = 1.0 - f_all
            hi, mid, lo = _split3(jnp.log2(f_all))
            b2 = (_dot(ltri, hi) + _dot(ltri, mid)) + _dot(ltri, lo)
            z_ref[r, C_F:2 * C_F] = b2
            b_mid = b2[CHUNK // 2 - 1:CHUNK // 2, :]
            b_last = b2[CHUNK - 1:CHUNK, :]
            worst = jnp.maximum(worst, jnp.maximum(-b_mid, b_mid - b_last))
        return worst

    worst = lax.fori_loop(0, n_chunks // PREFIX_GROUP, prefix, jnp.zeros((1, C_F), F32))
    bounded = jnp.max(worst) <= SAFE_LOG2_RANGE

    def finish_head(rows, hd, o):
        gt = z_ref[rows, 2 * C_F + C_V + hd * C_VAL_DIM:2 * C_F + C_V + (hd + 1) * C_VAL_DIM]
        o = _rms(o, on_ref[...]) * jax.nn.sigmoid(gt)
        o_ref[rows, hd * C_VAL_DIM:(hd + 1) * C_VAL_DIM] = o.astype(BF16)

    def head_inputs(rows, hd):
        q = z_ref[rows, hd * C_KEY_DIM:(hd + 1) * C_KEY_DIM]
        kk = k_ref[rows, hd * C_KEY_DIM:(hd + 1) * C_KEY_DIM]
        b2 = z_ref[rows, C_F + hd * C_KEY_DIM:C_F + (hd + 1) * C_KEY_DIM]
        ivb = z_ref[rows, 2 * C_F + hd * C_VAL_DIM:2 * C_F + (hd + 1) * C_VAL_DIM].astype(BF16)
        return q, kk, b2, ivb

    def factored_chunk(c, carry):
        rows = chunk_rows(c)
        for hd in range(C_HEADS):
            q, kk, b2, ivb = head_inputs(rows, hd)
            b_mid = b2[CHUNK // 2 - 1:CHUNK // 2, :]
            b_last = b2[CHUNK - 1:CHUNK, :]
            qs = (q * jnp.exp2(b2 - b_mid)).astype(BF16)
            kd = (kk * jnp.exp2(b_mid - b2)).astype(BF16)
            att = jnp.where(row >= col, lax.dot_general(qs, kd, _NT, preferred_element_type=F32), 0.0)
            stt = stt_ref[hd]
            o = lax.dot_general(qs, (stt * jnp.exp2(b_mid)).astype(BF16), _NT,
                                preferred_element_type=F32) + _dot(att.astype(BF16), ivb)
            stt_ref[hd] = stt * jnp.exp2(b_last) + jnp.exp2(b_last - b_mid) * lax.dot_general(
                ivb, kd, _TN, preferred_element_type=F32)
            finish_head(rows, hd, o)
        return carry

    def tree_chunk(c, carry):
        rows = chunk_rows(c)
        lev = lev_ref[...]

        def products(hd):
            q, kk, b2, ivb = head_inputs(rows, hd)
            diag = jnp.sum(q * kk, axis=-1, keepdims=True)
            pms = []
            for p in range(n_levels):
                y = _level_operand(p, q, kk, 1.0 - kk, b2)
                pms.append(lax.dot_general(y, y, _NT, preferred_element_type=F32))
            stt = stt_ref[hd]
            o_prev = lax.dot_general((q * jnp.exp2(b2)).astype(BF16), stt.astype(BF16), _NT,
                                     preferred_element_type=F32)
            b_last = b2[CHUNK - 1:CHUNK, :]
            kd = (kk * jnp.exp2(b_last - b2)).astype(BF16)
            stt_ref[hd] = stt * jnp.exp2(b_last) + lax.dot_general(
                ivb, kd, _TN, preferred_element_type=F32)
            return diag, pms, o_prev, ivb

        def finish(hd, diag, pms, o_prev, ivb):
            att = jnp.where(lev == -1, diag, 0.0)
            for p in range(n_levels):
                att = _merge_level(p, att, pms[p], lev)
            finish_head(rows, hd, o_prev + _dot(att.astype(BF16), ivb))

        pending = [products(hd) for hd in range(HEAD_SKEW)]
        for hd in range(C_HEADS):
            if hd + HEAD_SKEW < C_HEADS:
                pending.append(products(hd + HEAD_SKEW))
            finish(hd, *pending.pop(0))
        return carry

    @pl.when(bounded)
    def _():
        lax.fori_loop(0, n_chunks, factored_chunk, 0, unroll=FACTORED_UNROLL)

    @pl.when(jnp.logical_not(bounded))
    def _():
        lax.fori_loop(0, n_chunks, tree_chunk, 0)

    y_ref[0] = x_ref[0] + _dot(o_ref[...], wout_ref[...])

    @pl.when(j == last_j)
    def _():
        for hd in range(C_HEADS):
            st_ref[0, hd] = stt_ref[hd].T


def _hgrn_prompt(x, nm, w_in, clb, on, w_out, state_s, f_s, q_s, i_s):
    nb, seq, _ = x.shape
    nj = seq // TQH
    ns = state_s.shape[0]
    sbh = ns // (nb * nj)
    assert sbh * nb * nj == ns
    blk = lambda b, j: (b, j, 0)
    step = lambda b, j: (b * nj + j, 0, 0)
    step4 = lambda b, j: (b * nj + j, 0, 0, 0)
    rows3 = lambda a: a.reshape(ns // sbh, sbh, a.shape[1])
    cols4 = lambda a: a.reshape(ns // sbh, sbh, C_HEADS, C_KEY_DIM).transpose(0, 2, 3, 1)
    y, st, st_s, o_s = pl.pallas_call(
        _hgrn_prompt_kernel,
        grid=(nb, nj),
        in_specs=[
            pl.BlockSpec((1, TQH, D_MODEL), blk),
            _full((1, D_MODEL)),
            _resident((D_MODEL, C_IN)),
            _full((DEPTH, C_F)),
            _full((1, C_VAL_DIM)),
            _resident((C_V, D_MODEL)),
            _full((CHUNK, CHUNK)),
            pl.BlockSpec((sbh, C_HEADS, C_KEY_DIM, C_VAL_DIM), step4),
            pl.BlockSpec((None, C_HEADS, C_KEY_DIM, sbh), step4),
            pl.BlockSpec((None, C_HEADS, C_KEY_DIM, sbh), step4),
            pl.BlockSpec((None, sbh, C_V), step),
        ],
        out_specs=[
            pl.BlockSpec((1, TQH, D_MODEL), blk),
            pl.BlockSpec((1, C_HEADS, C_KEY_DIM, C_VAL_DIM), lambda b, j: (b, 0, 0, 0)),
            pl.BlockSpec((sbh, C_HEADS, C_KEY_DIM, C_VAL_DIM), step4),
            pl.BlockSpec((None, sbh, C_V), step),
        ],
        out_shape=[
            jax.ShapeDtypeStruct((nb, seq, D_MODEL), F32),
            jax.ShapeDtypeStruct((nb, C_HEADS, C_KEY_DIM, C_VAL_DIM), F32),
            jax.ShapeDtypeStruct(state_s.shape, F32),
            jax.ShapeDtypeStruct((ns // sbh, sbh, C_V), F32),
        ],
        scratch_shapes=[
            pltpu.VMEM((TQH, C_IN), F32),
            pltpu.VMEM((TQH, C_V), BF16),
            pltpu.VMEM((C_HEADS, C_VAL_DIM, C_KEY_DIM), F32),
            pltpu.VMEM((TQH, C_F), F32),
        ],
        compiler_params=pltpu.CompilerParams(
            dimension_semantics=("arbitrary", "arbitrary"), vmem_limit_bytes=VMEM_LIMIT),
        name="hgrn_prompt",
    )(x, nm, w_in, clb, on, w_out, jnp.asarray(_pair_level_table()),
      state_s, cols4(f_s), cols4(q_s), rows3(i_s))
    return y, st, st_s, o_s.reshape(ns, C_V)


def _ab_sample_proj_kernel(w00_ref, b0_ref, x_ref, nm_ref, win_ref, qn_ref, kn_ref, lng_ref, lnb_ref,
                           qx_ref, knew_ref, vnew_ref, bm_ref, gv_ref):
    n = x_ref.shape[0]
    h = _rms(x_ref[...], nm_ref[...]).astype(BF16)
    z = _dot(h, win_ref[...])
    zeros = jnp.zeros((n, A_HEAD_DIM), F32)
    for hh in range(A_HEADS):
        qh = _rms(z[:, hh * A_HEAD_DIM:(hh + 1) * A_HEAD_DIM], qn_ref[...]) * ATTN_SCALE
        qx_ref[hh] = jnp.concatenate([qh, zeros] if hh // A_GROUP == 0 else [zeros, qh], axis=-1)
    kparts = []
    for g in range(A_KV_HEADS):
        kparts.append(_rms(z[:, A_Q + g * A_HEAD_DIM:A_Q + (g + 1) * A_HEAD_DIM], kn_ref[...]))
    knew_ref[...] = jnp.concatenate(kparts, axis=-1)
    vnew_ref[...] = z[:, A_Q + A_KV:A_Q + 2 * A_KV]

    u = _gelu(z[:, A_Q + 2 * A_KV:A_Q + 2 * A_KV + B_WIDTH])
    vln = _layernorm(_gelu(z[:, A_Q + 2 * A_KV + B_WIDTH:AB_IN]), lng_ref[...], lnb_ref[...])
    grp = lax.broadcasted_iota(jnp.int32, (1, B_WIDTH), 1) // B_GROUP_DIM
    srow = jnp.zeros((1, B_WIDTH), F32)
    brow = jnp.zeros((1, B_WIDTH), F32)
    for g in range(B_GROUPS):
        srow = jnp.where(grp == g, w00_ref[g], srow)
        brow = jnp.where(grp == g, b0_ref[g], brow)
    bm_ref[...] = u * (vln * srow + brow)
    gv_ref[...] = vln


def _ab_sample_proj(x, nm, w_in, qn, kn, lng, lnb, w00, b0):
    n = x.shape[0]
    return pl.pallas_call(
        _ab_sample_proj_kernel,
        in_specs=[_SMEM, _SMEM] + [pl.BlockSpec(memory_space=pltpu.VMEM)] * 7,
        out_shape=[
            jax.ShapeDtypeStruct((A_HEADS, n, A_KV), F32),
            jax.ShapeDtypeStruct((n, A_KV), F32),
            jax.ShapeDtypeStruct((n, A_KV), F32),
            jax.ShapeDtypeStruct((n, B_WIDTH), F32),
            jax.ShapeDtypeStruct((n, B_WIDTH), F32),
        ],
        compiler_params=pltpu.CompilerParams(vmem_limit_bytes=VMEM_LIMIT),
        name="ab_sample_proj",
    )(w00, b0, x, nm, w_in, qn, kn, lng, lnb)


def _residual_proj_kernel(x_ref, m_ref, w_ref, y_ref):
    y_ref[...] = x_ref[...] + _dot(m_ref[...].astype(BF16), w_ref[...])


def _residual_proj(x, mix, w):
    return pl.pallas_call(
        _residual_proj_kernel,
        out_shape=jax.ShapeDtypeStruct(x.shape, F32),
        name="residual_proj",
    )(x, mix, w)


def _hgrn_sample_proj_kernel(x_ref, nm_ref, win_ref, clb_ref, q_ref, f_ref, i_ref, sg_ref):
    h = _rms(x_ref[...], nm_ref[...]).astype(BF16)
    z = _dot(h, win_ref[...])
    lb = _lower_bound(clb_ref[...])
    q_ref[...] = z[:, 0:C_F]
    f_ref[...] = lb + (1.0 - lb) * jax.nn.sigmoid(z[:, C_F:2 * C_F])
    i_ref[...] = z[:, 2 * C_F:2 * C_F + C_V]
    sg_ref[...] = jax.nn.sigmoid(z[:, 2 * C_F + C_V:C_IN])


def _hgrn_sample_proj(x, nm, w_in, clb):
    n = x.shape[0]
    return pl.pallas_call(
        _hgrn_sample_proj_kernel,
        out_shape=[jax.ShapeDtypeStruct((n, C_F), F32)] * 2 + [jax.ShapeDtypeStruct((n, C_V), F32)] * 2,
        compiler_params=pltpu.CompilerParams(vmem_limit_bytes=VMEM_LIMIT),
        name="hgrn_sample_proj",
    )(x, nm, w_in, clb)


def _hgrn_sample_out_kernel(o_ref, sg_ref, on_ref, w_ref, x_ref, y_ref):
    parts = []
    for hd in range(C_HEADS):
        parts.append(_rms(o_ref[:, hd * C_VAL_DIM:(hd + 1) * C_VAL_DIM], on_ref[...]))
    on = jnp.concatenate(parts, axis=-1) * sg_ref[...]
    y_ref[...] = x_ref[...] + _dot(on.astype(BF16), w_ref[...])


def _hgrn_sample_out(o, sg, on, w_out, x):
    return pl.pallas_call(
        _hgrn_sample_out_kernel,
        out_shape=jax.ShapeDtypeStruct(x.shape, F32),
        name="hgrn_sample_out",
    )(o, sg, on, w_out, x)


def kernel(x_prompt, x_sample, cache_k, cache_v, state_hgrn, norm_mix, norm_ffn, w_in_ab, w_out_ab,
           q_norm, k_norm, attn_sink, rel_bias, gmlp_ln_g, gmlp_ln_b, gmlp_w_s, gmlp_b_s,
           w_in_c, c_lower_bounds, c_out_norm, w_out_c, w_gate, w_up, w_down):
    assert norm_mix.shape[0] == DEPTH == 2 and w_in_ab.shape[0] == 1 and w_in_c.shape[0] == 1
    nb, seq, _ = x_prompt.shape
    ns = x_sample.shape[0]
    assert x_sample.shape[1] == 1 and cache_k.shape[2] == WINDOW

    row = lambda v: v.reshape(1, -1)
    bf = lambda w: w.astype(BF16)
    w_in_ab0, w_out_ab0 = bf(w_in_ab[0]), bf(w_out_ab[0])
    w_in_c0, w_out_c0 = bf(w_in_c[0]), bf(w_out_c[0])
    nm, nf = norm_mix, norm_ffn
    qn, kn = row(q_norm[0]), row(k_norm[0])
    lng, lnb = row(gmlp_ln_g[0]), row(gmlp_ln_b[0])
    sink = attn_sink[0]

    tab, tabp = _bias_table(rel_bias)

    xs = x_sample.reshape(ns, D_MODEL)
    qx, knew_s, vnew_s, bm_s, gv_s = _ab_sample_proj(
        xs, row(nm[0]), w_in_ab0, qn, kn, lng, lnb, gmlp_w_s[0, :, 0, 0], gmlp_b_s[0, :, 0])
    sb = jnp.pad(tab[:, WINDOW - 1, WINDOW - 1:], ((0, 0), (0, WINDOW - 1)))
    to_t = lambda c: c.transpose(0, 2, 3, 1).reshape(ns, A_KV, WINDOW)
    xp, knew_p, vnew_p, gv_p, nk_s, nv_s, om = _ab_prompt(
        x_prompt, row(nm[0]), w_in_ab0, jnp.tile(qn, (1, A_HEADS)), jnp.tile(kn, (1, A_KV_HEADS)),
        sink, tabp, lng, lnb, gmlp_w_s[0], jnp.repeat(gmlp_b_s[0].T, B_GROUP_DIM, axis=1), w_out_ab0,
        to_t(cache_k[0]), to_t(cache_v[0]), qx.transpose(1, 0, 2), knew_s, vnew_s, sb)
    om = om.reshape(ns, A_KV_HEADS, A_GROUP, A_KV_HEADS, A_HEAD_DIM)
    a_s = jnp.stack([om[:, g, :, g, :] for g in range(A_KV_HEADS)], axis=1).reshape(ns, A_Q)
    xs = _residual_proj(xs, jnp.concatenate([a_s, bm_s], axis=-1), w_out_ab0)
    xp, xs = _ffn(xp.reshape(nb * seq, D_MODEL), xs, row(nf[0]), w_gate, w_up, w_down, 0)

    q_s, f_s, i_s, sg_s = _hgrn_sample_proj(xs, row(nm[1]), w_in_c0, c_lower_bounds)
    xp, st_p, st_s, o_s = _hgrn_prompt(xp.reshape(nb, seq, D_MODEL), row(nm[1]), w_in_c0, c_lower_bounds,
                                       row(c_out_norm[0]), w_out_c0, state_hgrn[0], f_s, q_s, i_s)
    xs = _hgrn_sample_out(o_s, sg_s, row(c_out_norm[0]), w_out_c0, xs)
    xp, xs = _ffn(xp.reshape(nb * seq, D_MODEL), xs, row(nf[1]), w_gate, w_up, w_down, 1)

    kv5 = lambda a: a.reshape(1, a.shape[0], WINDOW, A_KV_HEADS, A_HEAD_DIM)
    from_t = lambda a: a.reshape(ns, A_KV_HEADS, A_HEAD_DIM, WINDOW).transpose(0, 3, 1, 2)[None]
    return (xp.reshape(nb, seq, D_MODEL), xs.reshape(ns, 1, D_MODEL),
            kv5(knew_p), kv5(vnew_p), from_t(nk_s), from_t(nv_s),
            gv_p[None], gv_s.reshape(1, ns, 1, B_WIDTH),
            st_p[None], st_s[None])
```

```python
import functools
import math

import jax
import jax.numpy as jnp
import numpy as np
from jax import lax
from jax.experimental import pallas as pl
from jax.experimental.pallas import tpu as pltpu

F32 = jnp.float32
BF16 = jnp.bfloat16

D_MODEL = 1024
DEPTH = 2
A_HEADS = 8
A_KV_HEADS = 2
A_GROUP = A_HEADS // A_KV_HEADS
A_HEAD_DIM = 64
WINDOW = 128
ATTN_SCALE = A_HEAD_DIM ** -0.5
NUM_BUCKETS = 32
MAX_DISTANCE = 128
A_Q = A_HEADS * A_HEAD_DIM
A_KV = A_KV_HEADS * A_HEAD_DIM
B_GROUPS = 8
B_GROUP_DIM = 64
B_WIDTH = B_GROUPS * B_GROUP_DIM
B_CHUNK = 128
AB_IN = A_Q + 2 * A_KV + 2 * B_WIDTH
AB_MIX = A_Q + B_WIDTH
C_HEADS = 8
C_KEY_DIM = 128
C_VAL_DIM = 128
C_F = C_HEADS * C_KEY_DIM
C_V = C_HEADS * C_VAL_DIM
C_IN = 2 * C_F + 2 * C_V
D_FF = 2816
EPS = 1e-6

NEG = -1e30

VMEM_LIMIT = 56 * 1024 * 1024
VREG_ROWS = 8

CHUNK = 128
TQ = 1024
TQH = 512
TM = 512
HEAD_SKEW = 2
SAFE_LOG2_RANGE = 64.0
PREFIX_GROUP = 4
FACTORED_UNROLL = 4
AB_UNROLL = 2

_NT = (((1,), (1,)), ((), ()))
_TN = (((0,), (0,)), ((), ()))


def _rms(x, g):
    return x * lax.rsqrt(jnp.mean(x * x, axis=-1, keepdims=True) + EPS) * g


def _gelu(x):
    return 0.5 * x * (1.0 + lax.erf(x * math.sqrt(0.5)))


def _layernorm(x, g, b):
    xc = x - jnp.mean(x, axis=-1, keepdims=True)
    return xc * lax.rsqrt(jnp.mean(xc * xc, axis=-1, keepdims=True) + EPS) * g + b


def _dot(a, b):
    return jnp.dot(a, b, preferred_element_type=F32)


def _full(shape):
    n = len(shape)
    return pl.BlockSpec(shape, lambda *_: (0,) * n)


def _resident(shape):
    n = len(shape)
    return pl.BlockSpec(shape, lambda *_: (0,) * n, pipeline_mode=pl.Buffered(1))


_SMEM = pl.BlockSpec(memory_space=pltpu.SMEM)


def _bias_table_kernel(rel_ref, tab_ref, tabp_ref):
    qi = lax.broadcasted_iota(jnp.int32, (WINDOW, 2 * WINDOW), 0)
    kj = lax.broadcasted_iota(jnp.int32, (WINDOW, 2 * WINDOW), 1)
    dist = qi + WINDOW - kj
    ok = (dist >= 0) & (dist < WINDOW)
    max_exact = NUM_BUCKETS // 2
    d = jnp.maximum(dist, 0)
    dl = jnp.maximum(d, 1).astype(F32)
    v = (jnp.log(dl / max_exact) / math.log(MAX_DISTANCE / max_exact) * (NUM_BUCKETS - max_exact))
    far = d >= max_exact
    hits = []
    for b in range(NUM_BUCKETS):
        if b < max_exact:
            hits.append(d == b)
        elif b < NUM_BUCKETS - 1:
            hits.append(far & (v >= b - max_exact) & (v < b - max_exact + 1))
        else:
            hits.append(far & (v >= b - max_exact))
    for h in range(A_HEADS):
        acc = jnp.zeros((WINDOW, 2 * WINDOW), F32)
        for b in range(NUM_BUCKETS):
            acc = jnp.where(hits[b], rel_ref[b, h], acc)
        t = jnp.where(ok, acc, NEG)
        tab_ref[h] = t
        cols = slice((h % 2) * 2 * WINDOW, (h % 2 + 1) * 2 * WINDOW)
        tabp_ref[0, h // 2, :, cols] = t
        tabp_ref[1, h // 2, :, cols] = jnp.where(kj < WINDOW, NEG, t)


def _bias_table(rel_bias):
    return pl.pallas_call(
        _bias_table_kernel,
        out_shape=[
            jax.ShapeDtypeStruct((A_HEADS, WINDOW, 2 * WINDOW), F32),
            jax.ShapeDtypeStruct((2, A_HEADS // 2, WINDOW, 4 * WINDOW), F32),
        ],
        in_specs=[_SMEM],
        name="bias_table",
    )(rel_bias)


PAIR = 2 * A_HEAD_DIM
N_PAIRS = A_HEADS // 2


def _sample_cache_attention(ck_ref, cv_ref, qx_ref, kn_ref, vn_ref, sb_ref, sink_ref,
                            nk_ref, nv_ref, om_ref):
    wb = ck_ref.shape[2]
    head = lax.broadcasted_iota(jnp.int32, (1, A_HEADS, A_KV), 1)
    lane = lax.broadcasted_iota(jnp.int32, (1, A_HEADS, A_KV), 2)
    own_group = (head // A_GROUP) == (lane // A_HEAD_DIM)
    newest = lax.broadcasted_iota(jnp.int32, (1, 1, wb), 2) == wb - 1
    sink = sink_ref[...][None]
    kc, vc = ck_ref[...], cv_ref[...]
    kn, vn = kn_ref[...], vn_ref[...]
    kn_cols, vn_cols = kn_ref[:, 0, :].T, vn_ref[:, 0, :].T
    for i in range(kc.shape[0]):
        nk_ref[i] = jnp.where(newest[0], kn_cols[:, i:i + 1], pltpu.roll(kc[i], wb - 1, 1))
        nv_ref[i] = jnp.where(newest[0], vn_cols[:, i:i + 1], pltpu.roll(vc[i], wb - 1, 1))
    q = qx_ref[...]
    s = jnp.einsum('bhd,bdk->bhk', q.astype(BF16), kc.astype(BF16), preferred_element_type=F32)
    s = s + sb_ref[:, 0:wb][None]
    sn = jnp.sum(q * kn, axis=-1, keepdims=True) + sb_ref[:, wb:wb + 1][None]
    m = jnp.maximum(jnp.maximum(jnp.max(s, axis=-1, keepdims=True), sn), sink)
    e = jnp.exp(s - m)
    en = jnp.exp(sn - m)
    r = 1.0 / (jnp.sum(e, axis=-1, keepdims=True) + en + jnp.exp(sink - m))
    o = jnp.einsum('bhk,bdk->bhd', (e * r).astype(BF16), vc.astype(BF16),
                   preferred_element_type=F32) + (en * r) * vn
    om_ref[...] = jnp.where(own_group, o, 0.0)


def _ab_prompt_kernel(sink_ref, x_ref, nm_ref, win_ref, qg_ref, kg_ref, tabp_ref, lng_ref, lnb_ref,
                      ws_ref, bsp_ref, wout_ref,
                      ck_ref, cv_ref, qx_ref, kn_ref, vn_ref, sb_ref, sinkc_ref,
                      y_ref, knew_ref, vnew_ref, gv_ref, nk_ref, nv_ref, om_ref,
                      z_ref, mix_ref, q_ref, k_ref, kr_ref, v_ref, vr_ref, wpair_ref, kl_ref, vl_ref, gl_ref):
    j = pl.program_id(1)
    last_j = pl.num_programs(1) - 1
    n_chunks = TQ // CHUNK

    @pl.when(j == 0)
    def _():
        for ref in (k_ref, kr_ref, v_ref, vr_ref):
            ref[0:CHUNK, :] = jnp.zeros((CHUNK, A_KV), BF16)
        row = lax.broadcasted_iota(jnp.int32, (B_CHUNK, B_CHUNK), 0)
        col = lax.broadcasted_iota(jnp.int32, (B_CHUNK, B_CHUNK), 1)
        for g in range(B_GROUPS):
            wpair_ref[g // 2, :, (g % 2) * B_CHUNK:(g % 2 + 1) * B_CHUNK] = jnp.where(
                row >= col, ws_ref[g], 0.0).astype(BF16)

    h = _rms(x_ref[0], nm_ref[...]).astype(BF16)
    z_ref[...] = _dot(h, win_ref[...])

    _sample_cache_attention(ck_ref, cv_ref, qx_ref, kn_ref, vn_ref, sb_ref, sinkc_ref,
                            nk_ref, nv_ref, om_ref)

    lo_half = lax.broadcasted_iota(jnp.int32, (1, PAIR), 1) < A_HEAD_DIM
    r_i = lax.broadcasted_iota(jnp.int32, (PAIR, PAIR), 0) // A_HEAD_DIM
    c_i = lax.broadcasted_iota(jnp.int32, (PAIR, PAIR), 1) // A_HEAD_DIM
    half_mean = jnp.where(r_i == c_i, 1.0 / A_HEAD_DIM, 0.0).astype(BF16)

    def mean_sq_halves(x):
        x2 = x * x
        hi = x2.astype(BF16)
        lo = (x2 - hi.astype(F32)).astype(BF16)
        return _dot(hi, half_mean) + _dot(lo, half_mean)

    def block_diag(top, bot):
        zero = jnp.zeros_like(top)
        return jnp.concatenate([jnp.where(lo_half, top, zero), jnp.where(lo_half, zero, bot)], axis=0)

    kraw = z_ref[:, A_Q:A_Q + A_KV]
    v_all = z_ref[:, A_Q + A_KV:A_Q + 2 * A_KV]
    kn_all = kraw * lax.rsqrt(mean_sq_halves(kraw) + EPS) * kg_ref[...]
    k_ref[CHUNK:, :] = kn_all.astype(BF16)
    kr_ref[CHUNK:, :] = pltpu.roll(kn_all, A_HEAD_DIM, 1).astype(BF16)
    v_ref[CHUNK:, :] = v_all.astype(BF16)
    vr_ref[CHUNK:, :] = pltpu.roll(v_all, A_HEAD_DIM, 1).astype(BF16)
    kl_ref[...] = kn_all[TQ - CHUNK:, :]
    vl_ref[...] = v_all[TQ - CHUNK:, :]
    for i in range(N_PAIRS):
        ps = slice(i * PAIR, (i + 1) * PAIR)
        qraw = z_ref[:, ps]
        qn = qraw * lax.rsqrt(mean_sq_halves(qraw) + EPS) * (qg_ref[:, ps] * ATTN_SCALE)
        q_ref[:, ps] = qn.astype(BF16)

    def chunk(c, carry):
        r0 = pl.multiple_of(c * CHUNK, CHUNK)
        rows = pl.ds(r0, CHUNK)
        first = jnp.where(jnp.logical_and(j == 0, c == 0), 1, 0)

        both = pl.ds(r0, 2 * CHUNK)
        k2, k2r, v2, v2r = k_ref[both, :], kr_ref[both, :], v_ref[both, :], vr_ref[both, :]
        kbd = [block_diag(k2, k2r), block_diag(k2r, k2)]
        vbd = [block_diag(v2, v2r), block_diag(v2r, v2)]

        scores = []
        for i in range(N_PAIRS):
            s = lax.dot_general(q_ref[rows, i * PAIR:(i + 1) * PAIR], kbd[i // (A_GROUP // 2)], _NT,
                                preferred_element_type=F32)
            scores.append(s + tabp_ref[first, i])
        outs = []
        for i in range(N_PAIRS):
            es, rs = [], []
            for hh in range(2):
                sh = scores[i][:, hh * 2 * WINDOW:(hh + 1) * 2 * WINDOW]
                sk = sink_ref[2 * i + hh]
                m = jnp.maximum(jnp.max(sh, axis=-1, keepdims=True), sk)
                e = jnp.exp(sh - m)
                rs.append(1.0 / (jnp.sum(e, axis=-1, keepdims=True) + jnp.exp(sk - m)))
                es.append(e.astype(BF16))
            o = _dot(jnp.concatenate(es, axis=-1), vbd[i // (A_GROUP // 2)])
            outs.append(o * jnp.where(lo_half, rs[0], rs[1]))
        mix_ref[rows, 0:A_Q] = jnp.concatenate(outs, axis=-1).astype(BF16)

        zu = z_ref[rows, A_Q + 2 * A_KV:A_Q + 2 * A_KV + B_WIDTH]
        zv = z_ref[rows, A_Q + 2 * A_KV + B_WIDTH:AB_IN]
        u = _gelu(zu)
        vln = _layernorm(_gelu(zv), lng_ref[...], lnb_ref[...])
        vlb = vln.astype(BF16)
        sparts = []
        for i in range(B_GROUPS // 2):
            vpair = vlb[:, i * PAIR:(i + 1) * PAIR]
            sparts.append(_dot(wpair_ref[i], block_diag(vpair, vpair)))
        bm = u * (jnp.concatenate(sparts, axis=-1) + bsp_ref[...])
        mix_ref[rows, A_Q:AB_MIX] = bm.astype(BF16)

        gl_ref[...] = vln
        return carry

    lax.fori_loop(0, n_chunks, chunk, 0, unroll=AB_UNROLL)
    y_ref[0] = x_ref[0] + _dot(mix_ref[...], wout_ref[...])
    for ref in (k_ref, kr_ref, v_ref, vr_ref):
        ref[0:CHUNK, :] = ref[TQ:TQ + CHUNK, :]

    @pl.when(j == last_j)
    def _():
        knew_ref[0] = kl_ref[...]
        vnew_ref[0] = vl_ref[...]
        gv_ref[0] = gl_ref[...]


def _ab_prompt(x, nm, w_in, qg, kg, sink, tabp, lng, lnb, w_s, bsp, w_out, ck, cv, qx, kn_s, vn_s, sb):
    nb, seq, _ = x.shape
    nj = seq // TQ
    grid = (nb, nj)
    ns, _, wb = ck.shape
    sba = ns // (nb * nj)
    assert sba * nb * nj == ns
    blk = lambda b, j: (b, j, 0)
    per_b = lambda b, j: (b, 0, 0)
    step = lambda b, j: (b * nj + j, 0, 0)
    return pl.pallas_call(
        _ab_prompt_kernel,
        grid=grid,
        in_specs=[
            _SMEM,
            pl.BlockSpec((1, TQ, D_MODEL), blk),
            _full((1, D_MODEL)),
            _resident((D_MODEL, AB_IN)),
            _full((1, A_Q)),
            _full((1, A_KV)),
            _resident((2, N_PAIRS, WINDOW, 4 * WINDOW)),
            _full((1, B_WIDTH)),
            _full((1, B_WIDTH)),
            _resident((B_GROUPS, B_CHUNK, B_CHUNK)),
            _resident((B_CHUNK, B_WIDTH)),
            _resident((AB_MIX, D_MODEL)),
            pl.BlockSpec((sba, A_KV, wb), step),
            pl.BlockSpec((sba, A_KV, wb), step),
            pl.BlockSpec((sba, A_HEADS, A_KV), step),
            pl.BlockSpec((sba, 1, A_KV), step),
            pl.BlockSpec((sba, 1, A_KV), step),
            _full((A_HEADS, 2 * WINDOW)),
            _full((A_HEADS, 1)),
        ],
        out_specs=[
            pl.BlockSpec((1, TQ, D_MODEL), blk),
            pl.BlockSpec((1, WINDOW, A_KV), per_b),
            pl.BlockSpec((1, WINDOW, A_KV), per_b),
            pl.BlockSpec((1, B_CHUNK, B_WIDTH), per_b),
            pl.BlockSpec((sba, A_KV, wb), step),
            pl.BlockSpec((sba, A_KV, wb), step),
            pl.BlockSpec((sba, A_HEADS, A_KV), step),
        ],
        out_shape=[
            jax.ShapeDtypeStruct((nb, seq, D_MODEL), F32),
            jax.ShapeDtypeStruct((nb, WINDOW, A_KV), F32),
            jax.ShapeDtypeStruct((nb, WINDOW, A_KV), F32),
            jax.ShapeDtypeStruct((nb, B_CHUNK, B_WIDTH), F32),
            jax.ShapeDtypeStruct((ns, A_KV, wb), F32),
            jax.ShapeDtypeStruct((ns, A_KV, wb), F32),
            jax.ShapeDtypeStruct((ns, A_HEADS, A_KV), F32),
        ],
        scratch_shapes=[
            pltpu.VMEM((TQ, AB_IN), F32),
            pltpu.VMEM((TQ, AB_MIX), BF16),
            pltpu.VMEM((TQ, A_Q), BF16),
            pltpu.VMEM((CHUNK + TQ, A_KV), BF16),
            pltpu.VMEM((CHUNK + TQ, A_KV), BF16),
            pltpu.VMEM((CHUNK + TQ, A_KV), BF16),
            pltpu.VMEM((CHUNK + TQ, A_KV), BF16),
            pltpu.VMEM((B_GROUPS // 2, B_CHUNK, 2 * B_CHUNK), BF16),
            pltpu.VMEM((WINDOW, A_KV), F32),
            pltpu.VMEM((WINDOW, A_KV), F32),
            pltpu.VMEM((B_CHUNK, B_WIDTH), F32),
        ],
        compiler_params=pltpu.CompilerParams(
            dimension_semantics=("arbitrary", "arbitrary"), vmem_limit_bytes=VMEM_LIMIT),
        name="ab_prompt",
    )(sink, x, nm, w_in, qg, kg, tabp, lng, lnb, w_s, bsp, w_out,
      ck, cv, qx, kn_s[:, None, :], vn_s[:, None, :], sb, sink.reshape(A_HEADS, 1))


FF_TILE = 256


def _ffn_kernel(xp_ref, xs_ref, g_ref, wg_ref, wu_ref, wd_ref, yp_ref, ys_ref,
                wg_s, wu_s, wd_s, h0_s, acc_s, *, n_cast, n_prompt):
    s = pl.program_id(0)

    def gated(h, wg, wu):
        gate = _dot(h, wg)
        return (gate * jax.nn.sigmoid(gate) * _dot(h, wu)).astype(BF16)

    @pl.when(s == 0)
    def _():
        x = xp_ref[...]
        h0_s[...] = _rms(x, g_ref[...]).astype(BF16)
        acc_s[...] = x

    for c in range(n_cast):
        @pl.when(s == c)
        def _(c=c):
            tile = slice(c * FF_TILE, (c + 1) * FF_TILE)
            wg_t, wu_t, wd_t = (r[...].astype(BF16) for r in (wg_ref, wu_ref, wd_ref))
            wg_s[:, tile] = wg_t
            wu_s[:, tile] = wu_t
            wd_s[tile, :] = wd_t
            acc_s[...] += _dot(gated(h0_s[...], wg_t, wu_t), wd_t)

    @pl.when(s == n_cast - 1)
    def _():
        yp_ref[...] = acc_s[...]

    def swiglu(x):
        h = _rms(x, g_ref[...]).astype(BF16)
        return x + _dot(gated(h, wg_s[...], wu_s[...]), wd_s[...])

    @pl.when(jnp.logical_and(s >= n_cast, s < n_cast + n_prompt - 1))
    def _():
        yp_ref[...] = swiglu(xp_ref[...])

    @pl.when(s == n_cast + n_prompt - 1)
    def _():
        ys_ref[...] = swiglu(xs_ref[...])


def _ffn(xp, xs, g, w_gate, w_up, w_down, layer):
    rows, ns = xp.shape[0], xs.shape[0]
    n_cast, n_prompt = D_FF // FF_TILE, rows // TM
    w_tile = lambda s: jnp.minimum(s, n_cast - 1)
    row_blk = lambda s: (jnp.clip(s - (n_cast - 1), 0, n_prompt - 1), 0)
    return pl.pallas_call(
        functools.partial(_ffn_kernel, n_cast=n_cast, n_prompt=n_prompt),
        grid=(n_cast + n_prompt,),
        in_specs=[
            pl.BlockSpec((TM, D_MODEL), row_blk),
            _full((ns, D_MODEL)),
            _full((1, D_MODEL)),
            pl.BlockSpec((None, D_MODEL, FF_TILE), lambda s: (layer, 0, w_tile(s))),
            pl.BlockSpec((None, D_MODEL, FF_TILE), lambda s: (layer, 0, w_tile(s))),
            pl.BlockSpec((None, FF_TILE, D_MODEL), lambda s: (layer, w_tile(s), 0)),
        ],
        out_specs=[pl.BlockSpec((TM, D_MODEL), row_blk), _full((ns, D_MODEL))],
        out_shape=[jax.ShapeDtypeStruct((rows, D_MODEL), F32), jax.ShapeDtypeStruct((ns, D_MODEL), F32)],
        scratch_shapes=[
            pltpu.VMEM((D_MODEL, D_FF), BF16),
            pltpu.VMEM((D_MODEL, D_FF), BF16),
            pltpu.VMEM((D_FF, D_MODEL), BF16),
            pltpu.VMEM((TM, D_MODEL), BF16),
            pltpu.VMEM((TM, D_MODEL), F32),
        ],
        compiler_params=pltpu.CompilerParams(
            dimension_semantics=("arbitrary",), vmem_limit_bytes=VMEM_LIMIT),
        name="ffn",
    )(xp, xs, g, w_gate, w_up, w_down)


def _lower_bound(clb):
    m = jnp.max(clb, axis=0, keepdims=True)
    e = jnp.exp(clb - m)
    sm = e / jnp.sum(e, axis=0, keepdims=True)
    return (sm[0:1] + sm[1:2]) - sm[0:1]


def _split3(x):
    hi = x.astype(BF16)
    r = x - hi.astype(F32)
    mid = r.astype(BF16)
    lo = (r - mid.astype(F32)).astype(BF16)
    return hi, mid, lo


def _neg_abs(x):
    return lax.bitcast_convert_type(
        lax.bitcast_convert_type(x, jnp.uint32) | jnp.uint32(0x80000000), F32)


def _pair_level_table():
    t = np.arange(CHUNK)[:, None]
    s = np.arange(CHUNK)[None, :]
    lev = np.floor(np.log2(np.maximum(t ^ s, 1))).astype(np.int32)
    lev = np.where(t == s, -1, lev)
    return np.where(s > t, -2, lev).astype(np.int32)


def _level_operand(p, q, kk, f, b2):
    m = 2 ** p
    if m < VREG_ROWS:
        shape3 = (CHUNK // VREG_ROWS, VREG_ROWS, q.shape[1])
        sub = lax.broadcasted_iota(jnp.int32, (1, VREG_ROWS, q.shape[1]), 1)
        upper = ((sub >> p) & 1) == 1
        q3, k3 = q.reshape(shape3), kk.reshape(shape3)
        if p == 0:
            y = jnp.where(upper, q3 * f.reshape(shape3), k3)
        else:
            b3 = b2.reshape(shape3)
            be = b3[:, m - 1:m, :]
            for k in range(1, VREG_ROWS // (2 * m)):
                be = jnp.where(sub >= 2 * m * k, b3[:, 2 * m * k + m - 1:2 * m * k + m, :], be)
            y = jnp.where(upper, q3, k3) * jnp.exp2(_neg_abs(b3 - be))
        return y.reshape(q.shape).astype(BF16)
    parts = []
    for k in range(CHUNK // (2 * m)):
        lo = slice(2 * m * k, 2 * m * k + m)
        up = slice(2 * m * k + m, 2 * m * (k + 1))
        be = b2[2 * m * k + m - 1:2 * m * k + m, :]
        parts.append(kk[lo] * jnp.exp2(be - b2[lo]))
        parts.append(q[up] * jnp.exp2(b2[up] - be))
    return jnp.concatenate(parts, axis=0).astype(BF16)


def _merge_level(p, att, pm, lev):
    m = 2 ** p
    if m < VREG_ROWS:
        return jnp.where(lev == p, pm, att)
    col = lax.broadcasted_iota(jnp.int32, (1, CHUNK), 1)
    parts = []
    for k in range(CHUNK // (2 * m)):
        lo = slice(2 * m * k, 2 * m * k + m)
        up = slice(2 * m * k + m, 2 * m * (k + 1))
        parts.append(att[lo])
        parts.append(jnp.where((col >= 2 * m * k) & (col < 2 * m * k + m), pm[up], att[up]))
    return jnp.concatenate(parts, axis=0)


def _hgrn_prompt_kernel(x_ref, nm_ref, win_ref, clb_ref, on_ref, wout_ref, lev_ref,
                        ss_ref, fs_ref, qs_ref, is_ref,
                        y_ref, st_ref, sso_ref, os_ref,
                        z_ref, o_ref, stt_ref, k_ref):
    j = pl.program_id(1)
    last_j = pl.num_programs(1) - 1
    n_chunks = TQH // CHUNK
    n_levels = int(math.log2(CHUNK))

    @pl.when(j == 0)
    def _():
        stt_ref[...] = jnp.zeros_like(stt_ref)

    h = _rms(x_ref[0], nm_ref[...]).astype(BF16)
    z_ref[...] = _dot(h, win_ref[...])

    head_cols = lambda ref: [ref[:, hd * C_KEY_DIM:(hd + 1) * C_KEY_DIM].T for hd in range(C_HEADS)]
    f_cols, q_cols = head_cols(fs_ref), head_cols(qs_ref)
    out_rows = []
    for smp in range(ss_ref.shape[0]):
        parts = []
        for hd in range(C_HEADS):
            hs = slice(hd * C_VAL_DIM, (hd + 1) * C_VAL_DIM)
            fb = jnp.broadcast_to(f_cols[hd][:, smp:smp + 1], (C_KEY_DIM, C_VAL_DIM))
            sn = fb * ss_ref[smp, hd] + (1.0 - fb) * is_ref[smp:smp + 1, hs]
            sso_ref[smp, hd] = sn
            parts.append(jnp.sum(q_cols[hd][:, smp:smp + 1] * sn, axis=0, keepdims=True))
        out_rows.append(jnp.concatenate(parts, axis=-1))
    os_ref[...] = jnp.concatenate(out_rows, axis=0)

    lb = _lower_bound(clb_ref[...])

    row = lax.broadcasted_iota(jnp.int32, (CHUNK, CHUNK), 0)
    col = lax.broadcasted_iota(jnp.int32, (CHUNK, CHUNK), 1)
    ltri = (row >= col).astype(BF16)

    def chunk_rows(c):
        return pl.ds(pl.multiple_of(c * CHUNK, CHUNK), CHUNK)

    def prefix(g, worst):
        rows = [chunk_rows(g * PREFIX_GROUP + i) for i in range(PREFIX_GROUP)]
        gates = [z_ref[r, C_F:2 * C_F] for r in rows]
        for r, gate in zip(rows, gates):
            f_all = lb + (1.0 - lb) * jax.nn.sigmoid(gate)
            k_ref[r, :] = 1.0 - f_all
            hi, mid, lo = _split3(jnp.log2(f_all))
            b2 = (_dot(ltri, hi) + _dot(ltri, mid)) + _dot(ltri, lo)
            z_ref[r, C_F:2 * C_F] = b2
            b_mid = b2[CHUNK // 2 - 1:CHUNK // 2, :]
            b_last = b2[CHUNK - 1:CHUNK, :]
            worst = jnp.maximum(worst, jnp.maximum(-b_mid, b_mid - b_last))
        return worst

    worst = lax.fori_loop(0, n_chunks // PREFIX_GROUP, prefix, jnp.zeros((1, C_F), F32))
    bounded = jnp.max(worst) <= SAFE_LOG2_RANGE

    def finish_head(rows, hd, o):
        gt = z_ref[rows, 2 * C_F + C_V + hd * C_VAL_DIM:2 * C_F + C_V + (hd + 1) * C_VAL_DIM]
        o = _rms(o, on_ref[...]) * jax.nn.sigmoid(gt)
        o_ref[rows, hd * C_VAL_DIM:(hd + 1) * C_VAL_DIM] = o.astype(BF16)

    def head_inputs(rows, hd):
        q = z_ref[rows, hd * C_KEY_DIM:(hd + 1) * C_KEY_DIM]
        kk = k_ref[rows, hd * C_KEY_DIM:(hd + 1) * C_KEY_DIM]
        b2 = z_ref[rows, C_F + hd * C_KEY_DIM:C_F + (hd + 1) * C_KEY_DIM]
        ivb = z_ref[rows, 2 * C_F + hd * C_VAL_DIM:2 * C_F + (hd + 1) * C_VAL_DIM].astype(BF16)
        return q, kk, b2, ivb

    def factored_chunk(c, carry):
        rows = chunk_rows(c)
        for hd in range(C_HEADS):
            q, kk, b2, ivb = head_inputs(rows, hd)
            b_mid = b2[CHUNK // 2 - 1:CHUNK // 2, :]
            b_last = b2[CHUNK - 1:CHUNK, :]
            qs = (q * jnp.exp2(b2 - b_mid)).astype(BF16)
            kd = (kk * jnp.exp2(b_mid - b2)).astype(BF16)
            att = jnp.where(row >= col, lax.dot_general(qs, kd, _NT, preferred_element_type=F32), 0.0)
            stt = stt_ref[hd]
            o = lax.dot_general(qs, (stt * jnp.exp2(b_mid)).astype(BF16), _NT,
                                preferred_element_type=F32) + _dot(att.astype(BF16), ivb)
            stt_ref[hd] = stt * jnp.exp2(b_last) + jnp.exp2(b_last - b_mid) * lax.dot_general(
                ivb, kd, _TN, preferred_element_type=F32)
            finish_head(rows, hd, o)
        return carry

    def tree_chunk(c, carry):
        rows = chunk_rows(c)
        lev = lev_ref[...]

        def products(hd):
            q, kk, b2, ivb = head_inputs(rows, hd)
            diag = jnp.sum(q * kk, axis=-1, keepdims=True)
            pms = []
            for p in range(n_levels):
                y = _level_operand(p, q, kk, 1.0 - kk, b2)
                pms.append(lax.dot_general(y, y, _NT, preferred_element_type=F32))
            stt = stt_ref[hd]
            o_prev = lax.dot_general((q * jnp.exp2(b2)).astype(BF16), stt.astype(BF16), _NT,
                                     preferred_element_type=F32)
            b_last = b2[CHUNK - 1:CHUNK, :]
            kd = (kk * jnp.exp2(b_last - b2)).astype(BF16)
            stt_ref[hd] = stt * jnp.exp2(b_last) + lax.dot_general(
                ivb, kd, _TN, preferred_element_type=F32)
            return diag, pms, o_prev, ivb

        def finish(hd, diag, pms, o_prev, ivb):
            att = jnp.where(lev == -1, diag, 0.0)
            for p in range(n_levels):
                att = _merge_level(p, att, pms[p], lev)
            finish_head(rows, hd, o_prev + _dot(att.astype(BF16), ivb))

        pending = [products(hd) for hd in range(HEAD_SKEW)]
        for hd in range(C_HEADS):
            if hd + HEAD_SKEW < C_HEADS:
                pending.append(products(hd + HEAD_SKEW))
            finish(hd, *pending.pop(0))
        return carry

    @pl.when(bounded)
    def _():
        lax.fori_loop(0, n_chunks, factored_chunk, 0, unroll=FACTORED_UNROLL)

    @pl.when(jnp.logical_not(bounded))
    def _():
        lax.fori_loop(0, n_chunks, tree_chunk, 0)

    y_ref[0] = x_ref[0] + _dot(o_ref[...], wout_ref[...])

    @pl.when(j == last_j)
    def _():
        for hd in range(C_HEADS):
            st_ref[0, hd] = stt_ref[hd].T


def _hgrn_prompt(x, nm, w_in, clb, on, w_out, state_s, f_s, q_s, i_s):
    nb, seq, _ = x.shape
    nj = seq // TQH
    ns = state_s.shape[0]
    sbh = ns // (nb * nj)
    assert sbh * nb * nj == ns
    blk = lambda b, j: (b, j, 0)
    step = lambda b, j: (b * nj + j, 0, 0)
    step4 = lambda b, j: (b * nj + j, 0, 0, 0)
    rows3 = lambda a: a.reshape(ns // sbh, sbh, a.shape[1])
    y, st, st_s, o_s = pl.pallas_call(
        _hgrn_prompt_kernel,
        grid=(nb, nj),
        in_specs=[
            pl.BlockSpec((1, TQH, D_MODEL), blk),
            _full((1, D_MODEL)),
            _resident((D_MODEL, C_IN)),
            _full((DEPTH, C_F)),
            _full((1, C_VAL_DIM)),
            _resident((C_V, D_MODEL)),
            _full((CHUNK, CHUNK)),
            pl.BlockSpec((sbh, C_HEADS, C_KEY_DIM, C_VAL_DIM), step4),
            pl.BlockSpec((None, sbh, C_F), step),
            pl.BlockSpec((None, sbh, C_F), step),
            pl.BlockSpec((None, sbh, C_V), step),
        ],
        out_specs=[
            pl.BlockSpec((1, TQH, D_MODEL), blk),
            pl.BlockSpec((1, C_HEADS, C_KEY_DIM, C_VAL_DIM), lambda b, j: (b, 0, 0, 0)),
            pl.BlockSpec((sbh, C_HEADS, C_KEY_DIM, C_VAL_DIM), step4),
            pl.BlockSpec((None, sbh, C_V), step),
        ],
        out_shape=[
            jax.ShapeDtypeStruct((nb, seq, D_MODEL), F32),
            jax.ShapeDtypeStruct((nb, C_HEADS, C_KEY_DIM, C_VAL_DIM), F32),
            jax.ShapeDtypeStruct(state_s.shape, F32),
            jax.ShapeDtypeStruct((ns // sbh, sbh, C_V), F32),
        ],
        scratch_shapes=[
            pltpu.VMEM((TQH, C_IN), F32),
            pltpu.VMEM((TQH, C_V), BF16),
            pltpu.VMEM((C_HEADS, C_VAL_DIM, C_KEY_DIM), F32),
            pltpu.VMEM((TQH, C_F), F32),
        ],
        compiler_params=pltpu.CompilerParams(
            dimension_semantics=("arbitrary", "arbitrary"), vmem_limit_bytes=VMEM_LIMIT),
        name="hgrn_prompt",
    )(x, nm, w_in, clb, on, w_out, jnp.asarray(_pair_level_table()),
      state_s, rows3(f_s), rows3(q_s), rows3(i_s))
    return y, st, st_s, o_s.reshape(ns, C_V)


def _ab_sample_proj_kernel(w00_ref, b0_ref, x_ref, nm_ref, win_ref, qn_ref, kn_ref, lng_ref, lnb_ref,
                           qx_ref, knew_ref, vnew_ref, bm_ref, gv_ref):
    n = x_ref.shape[0]
    h = _rms(x_ref[...], nm_ref[...]).astype(BF16)
    z = _dot(h, win_ref[...])
    zeros = jnp.zeros((n, A_HEAD_DIM), F32)
    for hh in range(A_HEADS):
        qh = _rms(z[:, hh * A_HEAD_DIM:(hh + 1) * A_HEAD_DIM], qn_ref[...]) * ATTN_SCALE
        qx_ref[hh] = jnp.concatenate([qh, zeros] if hh // A_GROUP == 0 else [zeros, qh], axis=-1)
    kparts = []
    for g in range(A_KV_HEADS):
        kparts.append(_rms(z[:, A_Q + g * A_HEAD_DIM:A_Q + (g + 1) * A_HEAD_DIM], kn_ref[...]))
    knew_ref[...] = jnp.concatenate(kparts, axis=-1)
    vnew_ref[...] = z[:, A_Q + A_KV:A_Q + 2 * A_KV]

    u = _gelu(z[:, A_Q + 2 * A_KV:A_Q + 2 * A_KV + B_WIDTH])
    vln = _layernorm(_gelu(z[:, A_Q + 2 * A_KV + B_WIDTH:AB_IN]), lng_ref[...], lnb_ref[...])
    grp = lax.broadcasted_iota(jnp.int32, (1, B_WIDTH), 1) // B_GROUP_DIM
    srow = jnp.zeros((1, B_WIDTH), F32)
    brow = jnp.zeros((1, B_WIDTH), F32)
    for g in range(B_GROUPS):
        srow = jnp.where(grp == g, w00_ref[g], srow)
        brow = jnp.where(grp == g, b0_ref[g], brow)
    bm_ref[...] = u * (vln * srow + brow)
    gv_ref[...] = vln


def _ab_sample_proj(x, nm, w_in, qn, kn, lng, lnb, w00, b0):
    n = x.shape[0]
    return pl.pallas_call(
        _ab_sample_proj_kernel,
        in_specs=[_SMEM, _SMEM] + [pl.BlockSpec(memory_space=pltpu.VMEM)] * 7,
        out_shape=[
            jax.ShapeDtypeStruct((A_HEADS, n, A_KV), F32),
            jax.ShapeDtypeStruct((n, A_KV), F32),
            jax.ShapeDtypeStruct((n, A_KV), F32),
            jax.ShapeDtypeStruct((n, B_WIDTH), F32),
            jax.ShapeDtypeStruct((n, B_WIDTH), F32),
        ],
        compiler_params=pltpu.CompilerParams(vmem_limit_bytes=VMEM_LIMIT),
        name="ab_sample_proj",
    )(w00, b0, x, nm, w_in, qn, kn, lng, lnb)


def _residual_proj_kernel(x_ref, m_ref, w_ref, y_ref):
    y_ref[...] = x_ref[...] + _dot(m_ref[...].astype(BF16), w_ref[...])


def _residual_proj(x, mix, w):
    return pl.pallas_call(
        _residual_proj_kernel,
        out_shape=jax.ShapeDtypeStruct(x.shape, F32),
        name="residual_proj",
    )(x, mix, w)


def _hgrn_sample_proj_kernel(x_ref, nm_ref, win_ref, clb_ref, q_ref, f_ref, i_ref, sg_ref):
    h = _rms(x_ref[...], nm_ref[...]).astype(BF16)
    z = _dot(h, win_ref[...])
    lb = _lower_bound(clb_ref[...])
    q_ref[...] = z[:, 0:C_F]
    f_ref[...] = lb + (1.0 - lb) * jax.nn.sigmoid(z[:, C_F:2 * C_F])
    i_ref[...] = z[:, 2 * C_F:2 * C_F + C_V]
    sg_ref[...] = jax.nn.sigmoid(z[:, 2 * C_F + C_V:C_IN])


def _hgrn_sample_proj(x, nm, w_in, clb):
    n = x.shape[0]
    return pl.pallas_call(
        _hgrn_sample_proj_kernel,
        out_shape=[jax.ShapeDtypeStruct((n, C_F), F32)] * 2 + [jax.ShapeDtypeStruct((n, C_V), F32)] * 2,
        compiler_params=pltpu.CompilerParams(vmem_limit_bytes=VMEM_LIMIT),
        name="hgrn_sample_proj",
    )(x, nm, w_in, clb)


def _hgrn_sample_out_kernel(o_ref, sg_ref, on_ref, w_ref, x_ref, y_ref):
    parts = []
    for hd in range(C_HEADS):
        parts.append(_rms(o_ref[:, hd * C_VAL_DIM:(hd + 1) * C_VAL_DIM], on_ref[...]))
    on = jnp.concatenate(parts, axis=-1) * sg_ref[...]
    y_ref[...] = x_ref[...] + _dot(on.astype(BF16), w_ref[...])


def _hgrn_sample_out(o, sg, on, w_out, x):
    return pl.pallas_call(
        _hgrn_sample_out_kernel,
        out_shape=jax.ShapeDtypeStruct(x.shape, F32),
        name="hgrn_sample_out",
    )(o, sg, on, w_out, x)


def kernel(x_prompt, x_sample, cache_k, cache_v, state_hgrn, norm_mix, norm_ffn, w_in_ab, w_out_ab,
           q_norm, k_norm, attn_sink, rel_bias, gmlp_ln_g, gmlp_ln_b, gmlp_w_s, gmlp_b_s,
           w_in_c, c_lower_bounds, c_out_norm, w_out_c, w_gate, w_up, w_down):
    assert norm_mix.shape[0] == DEPTH == 2 and w_in_ab.shape[0] == 1 and w_in_c.shape[0] == 1
    nb, seq, _ = x_prompt.shape
    ns = x_sample.shape[0]
    assert x_sample.shape[1] == 1 and cache_k.shape[2] == WINDOW

    row = lambda v: v.reshape(1, -1)
    bf = lambda w: w.astype(BF16)
    w_in_ab0, w_out_ab0 = bf(w_in_ab[0]), bf(w_out_ab[0])
    w_in_c0, w_out_c0 = bf(w_in_c[0]), bf(w_out_c[0])
    nm, nf = norm_mix, norm_ffn
    qn, kn = row(q_norm[0]), row(k_norm[0])
    lng, lnb = row(gmlp_ln_g[0]), row(gmlp_ln_b[0])
    sink = attn_sink[0]

    tab, tabp = _bias_table(rel_bias)

    xs = x_sample.reshape(ns, D_MODEL)
    qx, knew_s, vnew_s, bm_s, gv_s = _ab_sample_proj(
        xs, row(nm[0]), w_in_ab0, qn, kn, lng, lnb, gmlp_w_s[0, :, 0, 0], gmlp_b_s[0, :, 0])
    sb = jnp.pad(tab[:, WINDOW - 1, WINDOW - 1:], ((0, 0), (0, WINDOW - 1)))
    to_t = lambda c: c.transpose(0, 2, 3, 1).reshape(ns, A_KV, WINDOW)
    xp, knew_p, vnew_p, gv_p, nk_s, nv_s, om = _ab_prompt(
        x_prompt, row(nm[0]), w_in_ab0, jnp.tile(qn, (1, A_HEADS)), jnp.tile(kn, (1, A_KV_HEADS)),
        sink, tabp, lng, lnb, gmlp_w_s[0], jnp.repeat(gmlp_b_s[0].T, B_GROUP_DIM, axis=1), w_out_ab0,
        to_t(cache_k[0]), to_t(cache_v[0]), qx.transpose(1, 0, 2), knew_s, vnew_s, sb)
    om = om.reshape(ns, A_KV_HEADS, A_GROUP, A_KV_HEADS, A_HEAD_DIM)
    a_s = jnp.stack([om[:, g, :, g, :] for g in range(A_KV_HEADS)], axis=1).reshape(ns, A_Q)
    xs = _residual_proj(xs, jnp.concatenate([a_s, bm_s], axis=-1), w_out_ab0)
    xp, xs = _ffn(xp.reshape(nb * seq, D_MODEL), xs, row(nf[0]), w_gate, w_up, w_down, 0)

    q_s, f_s, i_s, sg_s = _hgrn_sample_proj(xs, row(nm[1]), w_in_c0, c_lower_bounds)
    xp, st_p, st_s, o_s = _hgrn_prompt(xp.reshape(nb, seq, D_MODEL), row(nm[1]), w_in_c0, c_lower_bounds,
                                       row(c_out_norm[0]), w_out_c0, state_hgrn[0], f_s, q_s, i_s)
    xs = _hgrn_sample_out(o_s, sg_s, row(c_out_norm[0]), w_out_c0, xs)
    xp, xs = _ffn(xp.reshape(nb * seq, D_MODEL), xs, row(nf[1]), w_gate, w_up, w_down, 1)

    kv5 = lambda a: a.reshape(1, a.shape[0], WINDOW, A_KV_HEADS, A_HEAD_DIM)
    from_t = lambda a: a.reshape(ns, A_KV_HEADS, A_HEAD_DIM, WINDOW).transpose(0, 3, 1, 2)[None]
    return (xp.reshape(nb, seq, D_MODEL), xs.reshape(ns, 1, D_MODEL),
            kv5(knew_p), kv5(vnew_p), from_t(nk_s), from_t(nv_s),
            gv_p[None], gv_s.reshape(1, ns, 1, B_WIDTH),
            st_p[None], st_s[None])
```

```python
import functools
import math

import jax
import jax.numpy as jnp
import numpy as np
from jax import lax
from jax.experimental import pallas as pl
from jax.experimental.pallas import tpu as pltpu

F32 = jnp.float32
BF16 = jnp.bfloat16

D_MODEL = 1024
DEPTH = 2
A_HEADS = 8
A_KV_HEADS = 2
A_GROUP = A_HEADS // A_KV_HEADS
A_HEAD_DIM = 64
WINDOW = 128
ATTN_SCALE = A_HEAD_DIM ** -0.5
NUM_BUCKETS = 32
MAX_DISTANCE = 128
A_Q = A_HEADS * A_HEAD_DIM
A_KV = A_KV_HEADS * A_HEAD_DIM
B_GROUPS = 8
B_GROUP_DIM = 64
B_WIDTH = B_GROUPS * B_GROUP_DIM
B_CHUNK = 128
AB_IN = A_Q + 2 * A_KV + 2 * B_WIDTH
AB_MIX = A_Q + B_WIDTH
C_HEADS = 8
C_KEY_DIM = 128
C_VAL_DIM = 128
C_F = C_HEADS * C_KEY_DIM
C_V = C_HEADS * C_VAL_DIM
C_IN = 2 * C_F + 2 * C_V
D_FF = 2816
EPS = 1e-6

NEG = -1e30

VMEM_LIMIT = 56 * 1024 * 1024
VREG_ROWS = 8

CHUNK = 128
TQ = 1024
TQH = 512
TM = 1024
HEAD_SKEW = 2
SAFE_LOG2_RANGE = 64.0
PREFIX_GROUP = 4
FACTORED_UNROLL = 4
AB_UNROLL = 2

_NT = (((1,), (1,)), ((), ()))
_TN = (((0,), (0,)), ((), ()))


def _rms(x, g):
    return x * lax.rsqrt(jnp.mean(x * x, axis=-1, keepdims=True) + EPS) * g


def _gelu(x):
    return 0.5 * x * (1.0 + lax.erf(x * math.sqrt(0.5)))


def _layernorm(x, g, b):
    xc = x - jnp.mean(x, axis=-1, keepdims=True)
    return xc * lax.rsqrt(jnp.mean(xc * xc, axis=-1, keepdims=True) + EPS) * g + b


def _dot(a, b):
    return jnp.dot(a, b, preferred_element_type=F32)


def _full(shape):
    n = len(shape)
    return pl.BlockSpec(shape, lambda *_: (0,) * n)


def _resident(shape):
    n = len(shape)
    return pl.BlockSpec(shape, lambda *_: (0,) * n, pipeline_mode=pl.Buffered(1))


_SMEM = pl.BlockSpec(memory_space=pltpu.SMEM)


def _bias_table_kernel(rel_ref, tab_ref, tabp_ref):
    qi = lax.broadcasted_iota(jnp.int32, (WINDOW, 2 * WINDOW), 0)
    kj = lax.broadcasted_iota(jnp.int32, (WINDOW, 2 * WINDOW), 1)
    dist = qi + WINDOW - kj
    ok = (dist >= 0) & (dist < WINDOW)
    max_exact = NUM_BUCKETS // 2
    d = jnp.maximum(dist, 0)
    dl = jnp.maximum(d, 1).astype(F32)
    v = (jnp.log(dl / max_exact) / math.log(MAX_DISTANCE / max_exact) * (NUM_BUCKETS - max_exact))
    far = d >= max_exact
    hits = []
    for b in range(NUM_BUCKETS):
        if b < max_exact:
            hits.append(d == b)
        elif b < NUM_BUCKETS - 1:
            hits.append(far & (v >= b - max_exact) & (v < b - max_exact + 1))
        else:
            hits.append(far & (v >= b - max_exact))
    for h in range(A_HEADS):
        acc = jnp.zeros((WINDOW, 2 * WINDOW), F32)
        for b in range(NUM_BUCKETS):
            acc = jnp.where(hits[b], rel_ref[b, h], acc)
        t = jnp.where(ok, acc, NEG)
        tab_ref[h] = t
        cols = slice((h % 2) * 2 * WINDOW, (h % 2 + 1) * 2 * WINDOW)
        tabp_ref[0, h // 2, :, cols] = t
        tabp_ref[1, h // 2, :, cols] = jnp.where(kj < WINDOW, NEG, t)


def _bias_table(rel_bias):
    return pl.pallas_call(
        _bias_table_kernel,
        out_shape=[
            jax.ShapeDtypeStruct((A_HEADS, WINDOW, 2 * WINDOW), F32),
            jax.ShapeDtypeStruct((2, A_HEADS // 2, WINDOW, 4 * WINDOW), F32),
        ],
        in_specs=[_SMEM],
        name="bias_table",
    )(rel_bias)


PAIR = 2 * A_HEAD_DIM
N_PAIRS = A_HEADS // 2


def _sample_cache_attention(ck_ref, cv_ref, qx_ref, kn_ref, vn_ref, sb_ref, sink_ref,
                            nk_ref, nv_ref, om_ref):
    wb = ck_ref.shape[2]
    head = lax.broadcasted_iota(jnp.int32, (1, A_HEADS, A_KV), 1)
    lane = lax.broadcasted_iota(jnp.int32, (1, A_HEADS, A_KV), 2)
    own_group = (head // A_GROUP) == (lane // A_HEAD_DIM)
    newest = lax.broadcasted_iota(jnp.int32, (1, 1, wb), 2) == wb - 1
    sink = sink_ref[...][None]
    kc, vc = ck_ref[...], cv_ref[...]
    kn, vn = kn_ref[...], vn_ref[...]
    kn_cols, vn_cols = kn_ref[:, 0, :].T, vn_ref[:, 0, :].T
    for i in range(kc.shape[0]):
        nk_ref[i] = jnp.where(newest[0], kn_cols[:, i:i + 1], pltpu.roll(kc[i], wb - 1, 1))
        nv_ref[i] = jnp.where(newest[0], vn_cols[:, i:i + 1], pltpu.roll(vc[i], wb - 1, 1))
    q = qx_ref[...]
    s = jnp.einsum('bhd,bdk->bhk', q.astype(BF16), kc.astype(BF16), preferred_element_type=F32)
    s = s + sb_ref[:, 0:wb][None]
    sn = jnp.sum(q * kn, axis=-1, keepdims=True) + sb_ref[:, wb:wb + 1][None]
    m = jnp.maximum(jnp.maximum(jnp.max(s, axis=-1, keepdims=True), sn), sink)
    e = jnp.exp(s - m)
    en = jnp.exp(sn - m)
    r = 1.0 / (jnp.sum(e, axis=-1, keepdims=True) + en + jnp.exp(sink - m))
    o = jnp.einsum('bhk,bdk->bhd', (e * r).astype(BF16), vc.astype(BF16),
                   preferred_element_type=F32) + (en * r) * vn
    om_ref[...] = jnp.where(own_group, o, 0.0)


def _ab_prompt_kernel(sink_ref, x_ref, nm_ref, win_ref, qg_ref, kg_ref, tabp_ref, lng_ref, lnb_ref,
                      ws_ref, bsp_ref, wout_ref,
                      ck_ref, cv_ref, qx_ref, kn_ref, vn_ref, sb_ref, sinkc_ref,
                      y_ref, knew_ref, vnew_ref, gv_ref, nk_ref, nv_ref, om_ref,
                      z_ref, mix_ref, q_ref, k_ref, kr_ref, v_ref, vr_ref, wpair_ref, kl_ref, vl_ref, gl_ref):
    j = pl.program_id(1)
    last_j = pl.num_programs(1) - 1
    n_chunks = TQ // CHUNK

    @pl.when(j == 0)
    def _():
        for ref in (k_ref, kr_ref, v_ref, vr_ref):
            ref[0:CHUNK, :] = jnp.zeros((CHUNK, A_KV), BF16)
        row = lax.broadcasted_iota(jnp.int32, (B_CHUNK, B_CHUNK), 0)
        col = lax.broadcasted_iota(jnp.int32, (B_CHUNK, B_CHUNK), 1)
        for g in range(B_GROUPS):
            wpair_ref[g // 2, :, (g % 2) * B_CHUNK:(g % 2 + 1) * B_CHUNK] = jnp.where(
                row >= col, ws_ref[g], 0.0).astype(BF16)

    h = _rms(x_ref[0], nm_ref[...]).astype(BF16)
    z_ref[...] = _dot(h, win_ref[...])

    _sample_cache_attention(ck_ref, cv_ref, qx_ref, kn_ref, vn_ref, sb_ref, sinkc_ref,
                            nk_ref, nv_ref, om_ref)

    lo_half = lax.broadcasted_iota(jnp.int32, (1, PAIR), 1) < A_HEAD_DIM
    r_i = lax.broadcasted_iota(jnp.int32, (PAIR, PAIR), 0) // A_HEAD_DIM
    c_i = lax.broadcasted_iota(jnp.int32, (PAIR, PAIR), 1) // A_HEAD_DIM
    half_mean = jnp.where(r_i == c_i, 1.0 / A_HEAD_DIM, 0.0).astype(BF16)

    def mean_sq_halves(x):
        x2 = x * x
        hi = x2.astype(BF16)
        lo = (x2 - hi.astype(F32)).astype(BF16)
        return _dot(hi, half_mean) + _dot(lo, half_mean)

    def block_diag(top, bot):
        zero = jnp.zeros_like(top)
        return jnp.concatenate([jnp.where(lo_half, top, zero), jnp.where(lo_half, zero, bot)], axis=0)

    kraw = z_ref[:, A_Q:A_Q + A_KV]
    v_all = z_ref[:, A_Q + A_KV:A_Q + 2 * A_KV]
    kn_all = kraw * lax.rsqrt(mean_sq_halves(kraw) + EPS) * kg_ref[...]
    k_ref[CHUNK:, :] = kn_all.astype(BF16)
    kr_ref[CHUNK:, :] = pltpu.roll(kn_all, A_HEAD_DIM, 1).astype(BF16)
    v_ref[CHUNK:, :] = v_all.astype(BF16)
    vr_ref[CHUNK:, :] = pltpu.roll(v_all, A_HEAD_DIM, 1).astype(BF16)
    kl_ref[...] = kn_all[TQ - CHUNK:, :]
    vl_ref[...] = v_all[TQ - CHUNK:, :]
    for i in range(N_PAIRS):
        ps = slice(i * PAIR, (i + 1) * PAIR)
        qraw = z_ref[:, ps]
        qn = qraw * lax.rsqrt(mean_sq_halves(qraw) + EPS) * (qg_ref[:, ps] * ATTN_SCALE)
        q_ref[:, ps] = qn.astype(BF16)

    def chunk(c, carry):
        r0 = pl.multiple_of(c * CHUNK, CHUNK)
        rows = pl.ds(r0, CHUNK)
        first = jnp.where(jnp.logical_and(j == 0, c == 0), 1, 0)

        both = pl.ds(r0, 2 * CHUNK)
        k2, k2r, v2, v2r = k_ref[both, :], kr_ref[both, :], v_ref[both, :], vr_ref[both, :]
        kbd = [block_diag(k2, k2r), block_diag(k2r, k2)]
        vbd = [block_diag(v2, v2r), block_diag(v2r, v2)]

        scores = []
        for i in range(N_PAIRS):
            s = lax.dot_general(q_ref[rows, i * PAIR:(i + 1) * PAIR], kbd[i // (A_GROUP // 2)], _NT,
                                preferred_element_type=F32)
            scores.append(s + tabp_ref[first, i])
        outs = []
        for i in range(N_PAIRS):
            es, rs = [], []
            for hh in range(2):
                sh = scores[i][:, hh * 2 * WINDOW:(hh + 1) * 2 * WINDOW]
                sk = sink_ref[2 * i + hh]
                m = jnp.maximum(jnp.max(sh, axis=-1, keepdims=True), sk)
                e = jnp.exp(sh - m)
                rs.append(1.0 / (jnp.sum(e, axis=-1, keepdims=True) + jnp.exp(sk - m)))
                es.append(e.astype(BF16))
            o = _dot(jnp.concatenate(es, axis=-1), vbd[i // (A_GROUP // 2)])
            outs.append(o * jnp.where(lo_half, rs[0], rs[1]))
        mix_ref[rows, 0:A_Q] = jnp.concatenate(outs, axis=-1).astype(BF16)

        zu = z_ref[rows, A_Q + 2 * A_KV:A_Q + 2 * A_KV + B_WIDTH]
        zv = z_ref[rows, A_Q + 2 * A_KV + B_WIDTH:AB_IN]
        u = _gelu(zu)
        vln = _layernorm(_gelu(zv), lng_ref[...], lnb_ref[...])
        vlb = vln.astype(BF16)
        sparts = []
        for i in range(B_GROUPS // 2):
            vpair = vlb[:, i * PAIR:(i + 1) * PAIR]
            sparts.append(_dot(wpair_ref[i], block_diag(vpair, vpair)))
        bm = u * (jnp.concatenate(sparts, axis=-1) + bsp_ref[...])
        mix_ref[rows, A_Q:AB_MIX] = bm.astype(BF16)

        gl_ref[...] = vln
        return carry

    lax.fori_loop(0, n_chunks, chunk, 0, unroll=AB_UNROLL)
    y_ref[0] = x_ref[0] + _dot(mix_ref[...], wout_ref[...])
    for ref in (k_ref, kr_ref, v_ref, vr_ref):
        ref[0:CHUNK, :] = ref[TQ:TQ + CHUNK, :]

    @pl.when(j == last_j)
    def _():
        knew_ref[0] = kl_ref[...]
        vnew_ref[0] = vl_ref[...]
        gv_ref[0] = gl_ref[...]


def _ab_prompt(x, nm, w_in, qg, kg, sink, tabp, lng, lnb, w_s, bsp, w_out, ck, cv, qx, kn_s, vn_s, sb):
    nb, seq, _ = x.shape
    nj = seq // TQ
    grid = (nb, nj)
    ns, _, wb = ck.shape
    sba = ns // (nb * nj)
    assert sba * nb * nj == ns
    blk = lambda b, j: (b, j, 0)
    per_b = lambda b, j: (b, 0, 0)
    step = lambda b, j: (b * nj + j, 0, 0)
    return pl.pallas_call(
        _ab_prompt_kernel,
        grid=grid,
        in_specs=[
            _SMEM,
            pl.BlockSpec((1, TQ, D_MODEL), blk),
            _full((1, D_MODEL)),
            _resident((D_MODEL, AB_IN)),
            _full((1, A_Q)),
            _full((1, A_KV)),
            _resident((2, N_PAIRS, WINDOW, 4 * WINDOW)),
            _full((1, B_WIDTH)),
            _full((1, B_WIDTH)),
            _resident((B_GROUPS, B_CHUNK, B_CHUNK)),
            _resident((B_CHUNK, B_WIDTH)),
            _resident((AB_MIX, D_MODEL)),
            pl.BlockSpec((sba, A_KV, wb), step),
            pl.BlockSpec((sba, A_KV, wb), step),
            pl.BlockSpec((sba, A_HEADS, A_KV), step),
            pl.BlockSpec((sba, 1, A_KV), step),
            pl.BlockSpec((sba, 1, A_KV), step),
            _full((A_HEADS, 2 * WINDOW)),
            _full((A_HEADS, 1)),
        ],
        out_specs=[
            pl.BlockSpec((1, TQ, D_MODEL), blk),
            pl.BlockSpec((1, WINDOW, A_KV), per_b),
            pl.BlockSpec((1, WINDOW, A_KV), per_b),
            pl.BlockSpec((1, B_CHUNK, B_WIDTH), per_b),
            pl.BlockSpec((sba, A_KV, wb), step),
            pl.BlockSpec((sba, A_KV, wb), step),
            pl.BlockSpec((sba, A_HEADS, A_KV), step),
        ],
        out_shape=[
            jax.ShapeDtypeStruct((nb, seq, D_MODEL), F32),
            jax.ShapeDtypeStruct((nb, WINDOW, A_KV), F32),
            jax.ShapeDtypeStruct((nb, WINDOW, A_KV), F32),
            jax.ShapeDtypeStruct((nb, B_CHUNK, B_WIDTH), F32),
            jax.ShapeDtypeStruct((ns, A_KV, wb), F32),
            jax.ShapeDtypeStruct((ns, A_KV, wb), F32),
            jax.ShapeDtypeStruct((ns, A_HEADS, A_KV), F32),
        ],
        scratch_shapes=[
            pltpu.VMEM((TQ, AB_IN), F32),
            pltpu.VMEM((TQ, AB_MIX), BF16),
            pltpu.VMEM((TQ, A_Q), BF16),
            pltpu.VMEM((CHUNK + TQ, A_KV), BF16),
            pltpu.VMEM((CHUNK + TQ, A_KV), BF16),
            pltpu.VMEM((CHUNK + TQ, A_KV), BF16),
            pltpu.VMEM((CHUNK + TQ, A_KV), BF16),
            pltpu.VMEM((B_GROUPS // 2, B_CHUNK, 2 * B_CHUNK), BF16),
            pltpu.VMEM((WINDOW, A_KV), F32),
            pltpu.VMEM((WINDOW, A_KV), F32),
            pltpu.VMEM((B_CHUNK, B_WIDTH), F32),
        ],
        compiler_params=pltpu.CompilerParams(
            dimension_semantics=("arbitrary", "arbitrary"), vmem_limit_bytes=VMEM_LIMIT),
        name="ab_prompt",
    )(sink, x, nm, w_in, qg, kg, tabp, lng, lnb, w_s, bsp, w_out,
      ck, cv, qx, kn_s[:, None, :], vn_s[:, None, :], sb, sink.reshape(A_HEADS, 1))


FF_TILE = 256
FF_SLABS = ((0, 768), (768, 1536), (1536, 2304), (2304, 2816))


def _ffn_kernel(xp_ref, xs_ref, g_ref, wg_ref, wu_ref, wd_ref, yp_ref, ys_ref,
                wg_s, wu_s, wd_s, h0_s, acc_s, *, n_cast, n_prompt):
    s = pl.program_id(0)

    def gated(h, wg, wu):
        gate = _dot(h, wg)
        return (gate * jax.nn.sigmoid(gate) * _dot(h, wu)).astype(BF16)

    @pl.when(s == 0)
    def _():
        x = xp_ref[...]
        h0_s[...] = _rms(x, g_ref[...]).astype(BF16)
        acc_s[...] = x

    for c in range(n_cast):
        @pl.when(s == c)
        def _(c=c):
            tile = slice(c * FF_TILE, (c + 1) * FF_TILE)
            wg_t, wu_t, wd_t = (r[...].astype(BF16) for r in (wg_ref, wu_ref, wd_ref))
            wg_s[:, tile] = wg_t
            wu_s[:, tile] = wu_t
            wd_s[tile, :] = wd_t
            acc_s[...] += _dot(gated(h0_s[...], wg_t, wu_t), wd_t)

    @pl.when(s == n_cast - 1)
    def _():
        yp_ref[...] = acc_s[...]

    def swiglu(x):
        h = _rms(x, g_ref[...]).astype(BF16)
        acc = x
        for lo, hi in FF_SLABS:
            acc = acc + _dot(gated(h, wg_s[:, lo:hi], wu_s[:, lo:hi]), wd_s[lo:hi, :])
        return acc

    @pl.when(jnp.logical_and(s >= n_cast, s < n_cast + n_prompt - 1))
    def _():
        yp_ref[...] = swiglu(xp_ref[...])

    @pl.when(s == n_cast + n_prompt - 1)
    def _():
        ys_ref[...] = swiglu(xs_ref[...])


def _ffn(xp, xs, g, w_gate, w_up, w_down, layer):
    rows, ns = xp.shape[0], xs.shape[0]
    n_cast, n_prompt = D_FF // FF_TILE, rows // TM
    w_tile = lambda s: jnp.minimum(s, n_cast - 1)
    row_blk = lambda s: (jnp.clip(s - (n_cast - 1), 0, n_prompt - 1), 0)
    return pl.pallas_call(
        functools.partial(_ffn_kernel, n_cast=n_cast, n_prompt=n_prompt),
        grid=(n_cast + n_prompt,),
        in_specs=[
            pl.BlockSpec((TM, D_MODEL), row_blk),
            _full((ns, D_MODEL)),
            _full((1, D_MODEL)),
            pl.BlockSpec((None, D_MODEL, FF_TILE), lambda s: (layer, 0, w_tile(s))),
            pl.BlockSpec((None, D_MODEL, FF_TILE), lambda s: (layer, 0, w_tile(s))),
            pl.BlockSpec((None, FF_TILE, D_MODEL), lambda s: (layer, w_tile(s), 0)),
        ],
        out_specs=[pl.BlockSpec((TM, D_MODEL), row_blk), _full((ns, D_MODEL))],
        out_shape=[jax.ShapeDtypeStruct((rows, D_MODEL), F32), jax.ShapeDtypeStruct((ns, D_MODEL), F32)],
        scratch_shapes=[
            pltpu.VMEM((D_MODEL, D_FF), BF16),
            pltpu.VMEM((D_MODEL, D_FF), BF16),
            pltpu.VMEM((D_FF, D_MODEL), BF16),
            pltpu.VMEM((TM, D_MODEL), BF16),
            pltpu.VMEM((TM, D_MODEL), F32),
        ],
        compiler_params=pltpu.CompilerParams(
            dimension_semantics=("arbitrary",), vmem_limit_bytes=VMEM_LIMIT),
        name="ffn",
    )(xp, xs, g, w_gate, w_up, w_down)


def _lower_bound(clb):
    m = jnp.max(clb, axis=0, keepdims=True)
    e = jnp.exp(clb - m)
    sm = e / jnp.sum(e, axis=0, keepdims=True)
    return (sm[0:1] + sm[1:2]) - sm[0:1]


def _split3(x):
    hi = x.astype(BF16)
    r = x - hi.astype(F32)
    mid = r.astype(BF16)
    lo = (r - mid.astype(F32)).astype(BF16)
    return hi, mid, lo


def _neg_abs(x):
    return lax.bitcast_convert_type(
        lax.bitcast_convert_type(x, jnp.uint32) | jnp.uint32(0x80000000), F32)


def _pair_level_table():
    t = np.arange(CHUNK)[:, None]
    s = np.arange(CHUNK)[None, :]
    lev = np.floor(np.log2(np.maximum(t ^ s, 1))).astype(np.int32)
    lev = np.where(t == s, -1, lev)
    return np.where(s > t, -2, lev).astype(np.int32)


def _level_operand(p, q, kk, f, b2):
    m = 2 ** p
    if m < VREG_ROWS:
        shape3 = (CHUNK // VREG_ROWS, VREG_ROWS, q.shape[1])
        sub = lax.broadcasted_iota(jnp.int32, (1, VREG_ROWS, q.shape[1]), 1)
        upper = ((sub >> p) & 1) == 1
        q3, k3 = q.reshape(shape3), kk.reshape(shape3)
        if p == 0:
            y = jnp.where(upper, q3 * f.reshape(shape3), k3)
        else:
            b3 = b2.reshape(shape3)
            be = b3[:, m - 1:m, :]
            for k in range(1, VREG_ROWS // (2 * m)):
                be = jnp.where(sub >= 2 * m * k, b3[:, 2 * m * k + m - 1:2 * m * k + m, :], be)
            y = jnp.where(upper, q3, k3) * jnp.exp2(_neg_abs(b3 - be))
        return y.reshape(q.shape).astype(BF16)
    parts = []
    for k in range(CHUNK // (2 * m)):
        lo = slice(2 * m * k, 2 * m * k + m)
        up = slice(2 * m * k + m, 2 * m * (k + 1))
        be = b2[2 * m * k + m - 1:2 * m * k + m, :]
        parts.append(kk[lo] * jnp.exp2(be - b2[lo]))
        parts.append(q[up] * jnp.exp2(b2[up] - be))
    return jnp.concatenate(parts, axis=0).astype(BF16)


def _merge_level(p, att, pm, lev):
    m = 2 ** p
    if m < VREG_ROWS:
        return jnp.where(lev == p, pm, att)
    col = lax.broadcasted_iota(jnp.int32, (1, CHUNK), 1)
    parts = []
    for k in range(CHUNK // (2 * m)):
        lo = slice(2 * m * k, 2 * m * k + m)
        up = slice(2 * m * k + m, 2 * m * (k + 1))
        parts.append(att[lo])
        parts.append(jnp.where((col >= 2 * m * k) & (col < 2 * m * k + m), pm[up], att[up]))
    return jnp.concatenate(parts, axis=0)


def _hgrn_prompt_kernel(x_ref, nm_ref, win_ref, clb_ref, on_ref, wout_ref, lev_ref,
                        ss_ref, fs_ref, qs_ref, is_ref,
                        y_ref, st_ref, sso_ref, os_ref,
                        z_ref, o_ref, stt_ref, k_ref):
    j = pl.program_id(1)
    last_j = pl.num_programs(1) - 1
    n_chunks = TQH // CHUNK
    n_levels = int(math.log2(CHUNK))

    @pl.when(j == 0)
    def _():
        stt_ref[...] = jnp.zeros_like(stt_ref)

    h = _rms(x_ref[0], nm_ref[...]).astype(BF16)
    z_ref[...] = _dot(h, win_ref[...])

    head_cols = lambda ref: [ref[:, hd * C_KEY_DIM:(hd + 1) * C_KEY_DIM].T for hd in range(C_HEADS)]
    f_cols, q_cols = head_cols(fs_ref), head_cols(qs_ref)
    out_rows = []
    for smp in range(ss_ref.shape[0]):
        parts = []
        for hd in range(C_HEADS):
            hs = slice(hd * C_VAL_DIM, (hd + 1) * C_VAL_DIM)
            fb = jnp.broadcast_to(f_cols[hd][:, smp:smp + 1], (C_KEY_DIM, C_VAL_DIM))
            sn = fb * ss_ref[smp, hd] + (1.0 - fb) * is_ref[smp:smp + 1, hs]
            sso_ref[smp, hd] = sn
            parts.append(jnp.sum(q_cols[hd][:, smp:smp + 1] * sn, axis=0, keepdims=True))
        out_rows.append(jnp.concatenate(parts, axis=-1))
    os_ref[...] = jnp.concatenate(out_rows, axis=0)

    lb = _lower_bound(clb_ref[...])

    row = lax.broadcasted_iota(jnp.int32, (CHUNK, CHUNK), 0)
    col = lax.broadcasted_iota(jnp.int32, (CHUNK, CHUNK), 1)
    ltri = (row >= col).astype(BF16)

    def chunk_rows(c):
        return pl.ds(pl.multiple_of(c * CHUNK, CHUNK), CHUNK)

    def prefix(g, worst):
        rows = [chunk_rows(g * PREFIX_GROUP + i) for i in range(PREFIX_GROUP)]
        gates = [z_ref[r, C_F:2 * C_F] for r in rows]
        for r, gate in zip(rows, gates):
            f_all = lb + (1.0 - lb) * jax.nn.sigmoid(gate)
            k_ref[r, :] = 1.0 - f_all
            hi, mid, lo = _split3(jnp.log2(f_all))
            b2 = (_dot(ltri, hi) + _dot(ltri, mid)) + _dot(ltri, lo)
            z_ref[r, C_F:2 * C_F] = b2
            b_mid = b2[CHUNK // 2 - 1:CHUNK // 2, :]
            b_last = b2[CHUNK - 1:CHUNK, :]
            worst = jnp.maximum(worst, jnp.maximum(-b_mid, b_mid - b_last))
        return worst

    worst = lax.fori_loop(0, n_chunks // PREFIX_GROUP, prefix, jnp.zeros((1, C_F), F32))
    bounded = jnp.max(worst) <= SAFE_LOG2_RANGE

    def finish_head(rows, hd, o):
        gt = z_ref[rows, 2 * C_F + C_V + hd * C_VAL_DIM:2 * C_F + C_V + (hd + 1) * C_VAL_DIM]
        o = _rms(o, on_ref[...]) * jax.nn.sigmoid(gt)
        o_ref[rows, hd * C_VAL_DIM:(hd + 1) * C_VAL_DIM] = o.astype(BF16)

    def head_inputs(rows, hd):
        q = z_ref[rows, hd * C_KEY_DIM:(hd + 1) * C_KEY_DIM]
        kk = k_ref[rows, hd * C_KEY_DIM:(hd + 1) * C_KEY_DIM]
        b2 = z_ref[rows, C_F + hd * C_KEY_DIM:C_F + (hd + 1) * C_KEY_DIM]
        ivb = z_ref[rows, 2 * C_F + hd * C_VAL_DIM:2 * C_F + (hd + 1) * C_VAL_DIM].astype(BF16)
        return q, kk, b2, ivb

    def factored_chunk(c, carry):
        rows = chunk_rows(c)
        for hd in range(C_HEADS):
            q, kk, b2, ivb = head_inputs(rows, hd)
            b_mid = b2[CHUNK // 2 - 1:CHUNK // 2, :]
            b_last = b2[CHUNK - 1:CHUNK, :]
            qs = (q * jnp.exp2(b2 - b_mid)).astype(BF16)
            kd = (kk * jnp.exp2(b_mid - b2)).astype(BF16)
            att = jnp.where(row >= col, lax.dot_general(qs, kd, _NT, preferred_element_type=F32), 0.0)
            stt = stt_ref[hd]
            o = lax.dot_general(qs, (stt * jnp.exp2(b_mid)).astype(BF16), _NT,
                                preferred_element_type=F32) + _dot(att.astype(BF16), ivb)
            stt_ref[hd] = stt * jnp.exp2(b_last) + jnp.exp2(b_last - b_mid) * lax.dot_general(
                ivb, kd, _TN, preferred_element_type=F32)
            finish_head(rows, hd, o)
        return carry

    def tree_chunk(c, carry):
        rows = chunk_rows(c)
        lev = lev_ref[...]

        def products(hd):
            q, kk, b2, ivb = head_inputs(rows, hd)
            diag = jnp.sum(q * kk, axis=-1, keepdims=True)
            pms = []
            for p in range(n_levels):
                y = _level_operand(p, q, kk, 1.0 - kk, b2)
                pms.append(lax.dot_general(y, y, _NT, preferred_element_type=F32))
            stt = stt_ref[hd]
            o_prev = lax.dot_general((q * jnp.exp2(b2)).astype(BF16), stt.astype(BF16), _NT,
                                     preferred_element_type=F32)
            b_last = b2[CHUNK - 1:CHUNK, :]
            kd = (kk * jnp.exp2(b_last - b2)).astype(BF16)
            stt_ref[hd] = stt * jnp.exp2(b_last) + lax.dot_general(
                ivb, kd, _TN, preferred_element_type=F32)
            return diag, pms, o_prev, ivb

        def finish(hd, diag, pms, o_prev, ivb):
            att = jnp.where(lev == -1, diag, 0.0)
            for p in range(n_levels):
                att = _merge_level(p, att, pms[p], lev)
            finish_head(rows, hd, o_prev + _dot(att.astype(BF16), ivb))

        pending = [products(hd) for hd in range(HEAD_SKEW)]
        for hd in range(C_HEADS):
            if hd + HEAD_SKEW < C_HEADS:
                pending.append(products(hd + HEAD_SKEW))
            finish(hd, *pending.pop(0))
        return carry

    @pl.when(bounded)
    def _():
        lax.fori_loop(0, n_chunks, factored_chunk, 0, unroll=FACTORED_UNROLL)

    @pl.when(jnp.logical_not(bounded))
    def _():
        lax.fori_loop(0, n_chunks, tree_chunk, 0)

    y_ref[0] = x_ref[0] + _dot(o_ref[...], wout_ref[...])

    @pl.when(j == last_j)
    def _():
        for hd in range(C_HEADS):
            st_ref[0, hd] = stt_ref[hd].T


def _hgrn_prompt(x, nm, w_in, clb, on, w_out, state_s, f_s, q_s, i_s):
    nb, seq, _ = x.shape
    nj = seq // TQH
    ns = state_s.shape[0]
    sbh = ns // (nb * nj)
    assert sbh * nb * nj == ns
    blk = lambda b, j: (b, j, 0)
    step = lambda b, j: (b * nj + j, 0, 0)
    step4 = lambda b, j: (b * nj + j, 0, 0, 0)
    rows3 = lambda a: a.reshape(ns // sbh, sbh, a.shape[1])
    y, st, st_s, o_s = pl.pallas_call(
        _hgrn_prompt_kernel,
        grid=(nb, nj),
        in_specs=[
            pl.BlockSpec((1, TQH, D_MODEL), blk),
            _full((1, D_MODEL)),
            _resident((D_MODEL, C_IN)),
            _full((DEPTH, C_F)),
            _full((1, C_VAL_DIM)),
            _resident((C_V, D_MODEL)),
            _full((CHUNK, CHUNK)),
            pl.BlockSpec((sbh, C_HEADS, C_KEY_DIM, C_VAL_DIM), step4),
            pl.BlockSpec((None, sbh, C_F), step),
            pl.BlockSpec((None, sbh, C_F), step),
            pl.BlockSpec((None, sbh, C_V), step),
        ],
        out_specs=[
            pl.BlockSpec((1, TQH, D_MODEL), blk),
            pl.BlockSpec((1, C_HEADS, C_KEY_DIM, C_VAL_DIM), lambda b, j: (b, 0, 0, 0)),
            pl.BlockSpec((sbh, C_HEADS, C_KEY_DIM, C_VAL_DIM), step4),
            pl.BlockSpec((None, sbh, C_V), step),
        ],
        out_shape=[
            jax.ShapeDtypeStruct((nb, seq, D_MODEL), F32),
            jax.ShapeDtypeStruct((nb, C_HEADS, C_KEY_DIM, C_VAL_DIM), F32),
            jax.ShapeDtypeStruct(state_s.shape, F32),
            jax.ShapeDtypeStruct((ns // sbh, sbh, C_V), F32),
        ],
        scratch_shapes=[
            pltpu.VMEM((TQH, C_IN), F32),
            pltpu.VMEM((TQH, C_V), BF16),
            pltpu.VMEM((C_HEADS, C_VAL_DIM, C_KEY_DIM), F32),
            pltpu.VMEM((TQH, C_F), F32),
        ],
        compiler_params=pltpu.CompilerParams(
            dimension_semantics=("arbitrary", "arbitrary"), vmem_limit_bytes=VMEM_LIMIT),
        name="hgrn_prompt",
    )(x, nm, w_in, clb, on, w_out, jnp.asarray(_pair_level_table()),
      state_s, rows3(f_s), rows3(q_s), rows3(i_s))
    return y, st, st_s, o_s.reshape(ns, C_V)


def _ab_sample_proj_kernel(w00_ref, b0_ref, x_ref, nm_ref, win_ref, qn_ref, kn_ref, lng_ref, lnb_ref,
                           qx_ref, knew_ref, vnew_ref, bm_ref, gv_ref):
    n = x_ref.shape[0]
    h = _rms(x_ref[...], nm_ref[...]).astype(BF16)
    z = _dot(h, win_ref[...])
    zeros = jnp.zeros((n, A_HEAD_DIM), F32)
    for hh in range(A_HEADS):
        qh = _rms(z[:, hh * A_HEAD_DIM:(hh + 1) * A_HEAD_DIM], qn_ref[...]) * ATTN_SCALE
        qx_ref[hh] = jnp.concatenate([qh, zeros] if hh // A_GROUP == 0 else [zeros, qh], axis=-1)
    kparts = []
    for g in range(A_KV_HEADS):
        kparts.append(_rms(z[:, A_Q + g * A_HEAD_DIM:A_Q + (g + 1) * A_HEAD_DIM], kn_ref[...]))
    knew_ref[...] = jnp.concatenate(kparts, axis=-1)
    vnew_ref[...] = z[:, A_Q + A_KV:A_Q + 2 * A_KV]

    u = _gelu(z[:, A_Q + 2 * A_KV:A_Q + 2 * A_KV + B_WIDTH])
    vln = _layernorm(_gelu(z[:, A_Q + 2 * A_KV + B_WIDTH:AB_IN]), lng_ref[...], lnb_ref[...])
    grp = lax.broadcasted_iota(jnp.int32, (1, B_WIDTH), 1) // B_GROUP_DIM
    srow = jnp.zeros((1, B_WIDTH), F32)
    brow = jnp.zeros((1, B_WIDTH), F32)
    for g in range(B_GROUPS):
        srow = jnp.where(grp == g, w00_ref[g], srow)
        brow = jnp.where(grp == g, b0_ref[g], brow)
    bm_ref[...] = u * (vln * srow + brow)
    gv_ref[...] = vln


def _ab_sample_proj(x, nm, w_in, qn, kn, lng, lnb, w00, b0):
    n = x.shape[0]
    return pl.pallas_call(
        _ab_sample_proj_kernel,
        in_specs=[_SMEM, _SMEM] + [pl.BlockSpec(memory_space=pltpu.VMEM)] * 7,
        out_shape=[
            jax.ShapeDtypeStruct((A_HEADS, n, A_KV), F32),
            jax.ShapeDtypeStruct((n, A_KV), F32),
            jax.ShapeDtypeStruct((n, A_KV), F32),
            jax.ShapeDtypeStruct((n, B_WIDTH), F32),
            jax.ShapeDtypeStruct((n, B_WIDTH), F32),
        ],
        compiler_params=pltpu.CompilerParams(vmem_limit_bytes=VMEM_LIMIT),
        name="ab_sample_proj",
    )(w00, b0, x, nm, w_in, qn, kn, lng, lnb)


def _residual_proj_kernel(x_ref, m_ref, w_ref, y_ref):
    y_ref[...] = x_ref[...] + _dot(m_ref[...].astype(BF16), w_ref[...])


def _residual_proj(x, mix, w):
    return pl.pallas_call(
        _residual_proj_kernel,
        out_shape=jax.ShapeDtypeStruct(x.shape, F32),
        name="residual_proj",
    )(x, mix, w)


def _hgrn_sample_proj_kernel(x_ref, nm_ref, win_ref, clb_ref, q_ref, f_ref, i_ref, sg_ref):
    h = _rms(x_ref[...], nm_ref[...]).astype(BF16)
    z = _dot(h, win_ref[...])
    lb = _lower_bound(clb_ref[...])
    q_ref[...] = z[:, 0:C_F]
    f_ref[...] = lb + (1.0 - lb) * jax.nn.sigmoid(z[:, C_F:2 * C_F])
    i_ref[...] = z[:, 2 * C_F:2 * C_F + C_V]
    sg_ref[...] = jax.nn.sigmoid(z[:, 2 * C_F + C_V:C_IN])


def _hgrn_sample_proj(x, nm, w_in, clb):
    n = x.shape[0]
    return pl.pallas_call(
        _hgrn_sample_proj_kernel,
        out_shape=[jax.ShapeDtypeStruct((n, C_F), F32)] * 2 + [jax.ShapeDtypeStruct((n, C_V), F32)] * 2,
        compiler_params=pltpu.CompilerParams(vmem_limit_bytes=VMEM_LIMIT),
        name="hgrn_sample_proj",
    )(x, nm, w_in, clb)


def _hgrn_sample_out_kernel(o_ref, sg_ref, on_ref, w_ref, x_ref, y_ref):
    parts = []
    for hd in range(C_HEADS):
        parts.append(_rms(o_ref[:, hd * C_VAL_DIM:(hd + 1) * C_VAL_DIM], on_ref[...]))
    on = jnp.concatenate(parts, axis=-1) * sg_ref[...]
    y_ref[...] = x_ref[...] + _dot(on.astype(BF16), w_ref[...])


def _hgrn_sample_out(o, sg, on, w_out, x):
    return pl.pallas_call(
        _hgrn_sample_out_kernel,
        out_shape=jax.ShapeDtypeStruct(x.shape, F32),
        name="hgrn_sample_out",
    )(o, sg, on, w_out, x)


def kernel(x_prompt, x_sample, cache_k, cache_v, state_hgrn, norm_mix, norm_ffn, w_in_ab, w_out_ab,
           q_norm, k_norm, attn_sink, rel_bias, gmlp_ln_g, gmlp_ln_b, gmlp_w_s, gmlp_b_s,
           w_in_c, c_lower_bounds, c_out_norm, w_out_c, w_gate, w_up, w_down):
    assert norm_mix.shape[0] == DEPTH == 2 and w_in_ab.shape[0] == 1 and w_in_c.shape[0] == 1
    nb, seq, _ = x_prompt.shape
    ns = x_sample.shape[0]
    assert x_sample.shape[1] == 1 and cache_k.shape[2] == WINDOW

    row = lambda v: v.reshape(1, -1)
    bf = lambda w: w.astype(BF16)
    w_in_ab0, w_out_ab0 = bf(w_in_ab[0]), bf(w_out_ab[0])
    w_in_c0, w_out_c0 = bf(w_in_c[0]), bf(w_out_c[0])
    nm, nf = norm_mix, norm_ffn
    qn, kn = row(q_norm[0]), row(k_norm[0])
    lng, lnb = row(gmlp_ln_g[0]), row(gmlp_ln_b[0])
    sink = attn_sink[0]

    tab, tabp = _bias_table(rel_bias)

    xs = x_sample.reshape(ns, D_MODEL)
    qx, knew_s, vnew_s, bm_s, gv_s = _ab_sample_proj(
        xs, row(nm[0]), w_in_ab0, qn, kn, lng, lnb, gmlp_w_s[0, :, 0, 0], gmlp_b_s[0, :, 0])
    sb = jnp.pad(tab[:, WINDOW - 1, WINDOW - 1:], ((0, 0), (0, WINDOW - 1)))
    to_t = lambda c: c.transpose(0, 2, 3, 1).reshape(ns, A_KV, WINDOW)
    xp, knew_p, vnew_p, gv_p, nk_s, nv_s, om = _ab_prompt(
        x_prompt, row(nm[0]), w_in_ab0, jnp.tile(qn, (1, A_HEADS)), jnp.tile(kn, (1, A_KV_HEADS)),
        sink, tabp, lng, lnb, gmlp_w_s[0], jnp.repeat(gmlp_b_s[0].T, B_GROUP_DIM, axis=1), w_out_ab0,
        to_t(cache_k[0]), to_t(cache_v[0]), qx.transpose(1, 0, 2), knew_s, vnew_s, sb)
    om = om.reshape(ns, A_KV_HEADS, A_GROUP, A_KV_HEADS, A_HEAD_DIM)
    a_s = jnp.stack([om[:, g, :, g, :] for g in range(A_KV_HEADS)], axis=1).reshape(ns, A_Q)
    xs = _residual_proj(xs, jnp.concatenate([a_s, bm_s], axis=-1), w_out_ab0)
    xp, xs = _ffn(xp.reshape(nb * seq, D_MODEL), xs, row(nf[0]), w_gate, w_up, w_down, 0)

    q_s, f_s, i_s, sg_s = _hgrn_sample_proj(xs, row(nm[1]), w_in_c0, c_lower_bounds)
    xp, st_p, st_s, o_s = _hgrn_prompt(xp.reshape(nb, seq, D_MODEL), row(nm[1]), w_in_c0, c_lower_bounds,
                                       row(c_out_norm[0]), w_out_c0, state_hgrn[0], f_s, q_s, i_s)
    xs = _hgrn_sample_out(o_s, sg_s, row(c_out_norm[0]), w_out_c0, xs)
    xp, xs = _ffn(xp.reshape(nb * seq, D_MODEL), xs, row(nf[1]), w_gate, w_up, w_down, 1)

    kv5 = lambda a: a.reshape(1, a.shape[0], WINDOW, A_KV_HEADS, A_HEAD_DIM)
    from_t = lambda a: a.reshape(ns, A_KV_HEADS, A_HEAD_DIM, WINDOW).transpose(0, 3, 1, 2)[None]
    return (xp.reshape(nb, seq, D_MODEL), xs.reshape(ns, 1, D_MODEL),
            kv5(knew_p), kv5(vnew_p), from_t(nk_s), from_t(nv_s),
            gv_p[None], gv_s.reshape(1, ns, 1, B_WIDTH),
            st_p[None], st_s[None])
```

```python
import functools
import math

import jax
import jax.numpy as jnp
import numpy as np
from jax import lax
from jax.experimental import pallas as pl
from jax.experimental.pallas import tpu as pltpu

F32 = jnp.float32
BF16 = jnp.bfloat16

D_MODEL = 1024
DEPTH = 2
A_HEADS = 8
A_KV_HEADS = 2
A_GROUP = A_HEADS // A_KV_HEADS
A_HEAD_DIM = 64
WINDOW = 128
ATTN_SCALE = A_HEAD_DIM ** -0.5
NUM_BUCKETS = 32
MAX_DISTANCE = 128
A_Q = A_HEADS * A_HEAD_DIM
A_KV = A_KV_HEADS * A_HEAD_DIM
B_GROUPS = 8
B_GROUP_DIM = 64
B_WIDTH = B_GROUPS * B_GROUP_DIM
B_CHUNK = 128
AB_IN = A_Q + 2 * A_KV + 2 * B_WIDTH
AB_MIX = A_Q + B_WIDTH
C_HEADS = 8
C_KEY_DIM = 128
C_VAL_DIM = 128
C_F = C_HEADS * C_KEY_DIM
C_V = C_HEADS * C_VAL_DIM
C_IN = 2 * C_F + 2 * C_V
D_FF = 2816
EPS = 1e-6

NEG = -1e30

VMEM_LIMIT = 56 * 1024 * 1024
VREG_ROWS = 8

CHUNK = 128
TQ = 1024
TQH = 512
TM = 512
HEAD_SKEW = 2
SAFE_LOG2_RANGE = 64.0
PREFIX_GROUP = 4
FACTORED_UNROLL = 4
AB_UNROLL = 2

_NT = (((1,), (1,)), ((), ()))
_TN = (((0,), (0,)), ((), ()))


def _rms(x, g):
    return x * lax.rsqrt(jnp.mean(x * x, axis=-1, keepdims=True) + EPS) * g


def _gelu(x):
    return 0.5 * x * (1.0 + lax.erf(x * math.sqrt(0.5)))


def _layernorm(x, g, b):
    xc = x - jnp.mean(x, axis=-1, keepdims=True)
    return xc * lax.rsqrt(jnp.mean(xc * xc, axis=-1, keepdims=True) + EPS) * g + b


def _dot(a, b):
    return jnp.dot(a, b, preferred_element_type=F32)


def _full(shape):
    n = len(shape)
    return pl.BlockSpec(shape, lambda *_: (0,) * n)


def _resident(shape):
    n = len(shape)
    return pl.BlockSpec(shape, lambda *_: (0,) * n, pipeline_mode=pl.Buffered(1))


_SMEM = pl.BlockSpec(memory_space=pltpu.SMEM)


def _bias_table_kernel(rel_ref, tab_ref, tabp_ref):
    qi = lax.broadcasted_iota(jnp.int32, (WINDOW, 2 * WINDOW), 0)
    kj = lax.broadcasted_iota(jnp.int32, (WINDOW, 2 * WINDOW), 1)
    dist = qi + WINDOW - kj
    ok = (dist >= 0) & (dist < WINDOW)
    max_exact = NUM_BUCKETS // 2
    d = jnp.maximum(dist, 0)
    dl = jnp.maximum(d, 1).astype(F32)
    v = (jnp.log(dl / max_exact) / math.log(MAX_DISTANCE / max_exact) * (NUM_BUCKETS - max_exact))
    far = d >= max_exact
    hits = []
    for b in range(NUM_BUCKETS):
        if b < max_exact:
            hits.append(d == b)
        elif b < NUM_BUCKETS - 1:
            hits.append(far & (v >= b - max_exact) & (v < b - max_exact + 1))
        else:
            hits.append(far & (v >= b - max_exact))
    for h in range(A_HEADS):
        acc = jnp.zeros((WINDOW, 2 * WINDOW), F32)
        for b in range(NUM_BUCKETS):
            acc = jnp.where(hits[b], rel_ref[b, h], acc)
        t = jnp.where(ok, acc, NEG)
        tab_ref[h] = t
        cols = slice((h % 2) * 2 * WINDOW, (h % 2 + 1) * 2 * WINDOW)
        tabp_ref[0, h // 2, :, cols] = t
        tabp_ref[1, h // 2, :, cols] = jnp.where(kj < WINDOW, NEG, t)


def _bias_table(rel_bias):
    return pl.pallas_call(
        _bias_table_kernel,
        out_shape=[
            jax.ShapeDtypeStruct((A_HEADS, WINDOW, 2 * WINDOW), F32),
            jax.ShapeDtypeStruct((2, A_HEADS // 2, WINDOW, 4 * WINDOW), F32),
        ],
        in_specs=[_SMEM],
        name="bias_table",
    )(rel_bias)


PAIR = 2 * A_HEAD_DIM
N_PAIRS = A_HEADS // 2


def _sample_cache_attention(ck_ref, cv_ref, qx_ref, kn_ref, vn_ref, sb_ref, sink_ref,
                            nk_ref, nv_ref, om_ref):
    wb = ck_ref.shape[2]
    head = lax.broadcasted_iota(jnp.int32, (1, A_HEADS, A_KV), 1)
    lane = lax.broadcasted_iota(jnp.int32, (1, A_HEADS, A_KV), 2)
    own_group = (head // A_GROUP) == (lane // A_HEAD_DIM)
    newest = lax.broadcasted_iota(jnp.int32, (1, 1, wb), 2) == wb - 1
    sink = sink_ref[...][None]
    kc, vc = ck_ref[...], cv_ref[...]
    kn, vn = kn_ref[...], vn_ref[...]
    kn_cols, vn_cols = kn_ref[:, 0, :].T, vn_ref[:, 0, :].T
    for i in range(kc.shape[0]):
        nk_ref[i] = jnp.where(newest[0], kn_cols[:, i:i + 1], pltpu.roll(kc[i], wb - 1, 1))
        nv_ref[i] = jnp.where(newest[0], vn_cols[:, i:i + 1], pltpu.roll(vc[i], wb - 1, 1))
    q = qx_ref[...]
    s = jnp.einsum('bhd,bdk->bhk', q.astype(BF16), kc.astype(BF16), preferred_element_type=F32)
    s = s + sb_ref[:, 0:wb][None]
    sn = jnp.sum(q * kn, axis=-1, keepdims=True) + sb_ref[:, wb:wb + 1][None]
    m = jnp.maximum(jnp.maximum(jnp.max(s, axis=-1, keepdims=True), sn), sink)
    e = jnp.exp(s - m)
    en = jnp.exp(sn - m)
    r = 1.0 / (jnp.sum(e, axis=-1, keepdims=True) + en + jnp.exp(sink - m))
    o = jnp.einsum('bhk,bdk->bhd', (e * r).astype(BF16), vc.astype(BF16),
                   preferred_element_type=F32) + (en * r) * vn
    om_ref[...] = jnp.where(own_group, o, 0.0)


def _ab_prompt_kernel(sink_ref, x_ref, nm_ref, win_ref, qg_ref, kg_ref, tabp_ref, lng_ref, lnb_ref,
                      ws_ref, bsp_ref, wout_ref,
                      ck_ref, cv_ref, qx_ref, kn_ref, vn_ref, sb_ref, sinkc_ref,
                      y_ref, knew_ref, vnew_ref, gv_ref, nk_ref, nv_ref, om_ref,
                      z_ref, mix_ref, q_ref, k_ref, kr_ref, v_ref, vr_ref, wpair_ref, kl_ref, vl_ref, gl_ref):
    j = pl.program_id(1)
    last_j = pl.num_programs(1) - 1
    n_chunks = TQ // CHUNK

    @pl.when(j == 0)
    def _():
        for ref in (k_ref, kr_ref, v_ref, vr_ref):
            ref[0:CHUNK, :] = jnp.zeros((CHUNK, A_KV), BF16)
        row = lax.broadcasted_iota(jnp.int32, (B_CHUNK, B_CHUNK), 0)
        col = lax.broadcasted_iota(jnp.int32, (B_CHUNK, B_CHUNK), 1)
        for g in range(B_GROUPS):
            wpair_ref[g // 2, :, (g % 2) * B_CHUNK:(g % 2 + 1) * B_CHUNK] = jnp.where(
                row >= col, ws_ref[g], 0.0).astype(BF16)

    h = _rms(x_ref[0], nm_ref[...]).astype(BF16)
    z_ref[...] = _dot(h, win_ref[...])

    _sample_cache_attention(ck_ref, cv_ref, qx_ref, kn_ref, vn_ref, sb_ref, sinkc_ref,
                            nk_ref, nv_ref, om_ref)

    lo_half = lax.broadcasted_iota(jnp.int32, (1, PAIR), 1) < A_HEAD_DIM

    def mean_sq_halves(x):
        x2 = x * x
        lo_sum = jnp.sum(jnp.where(lo_half, x2, 0.0), axis=-1, keepdims=True)
        hi_sum = jnp.sum(jnp.where(lo_half, 0.0, x2), axis=-1, keepdims=True)
        return jnp.where(lo_half, lo_sum, hi_sum) * (1.0 / A_HEAD_DIM)

    def block_diag(top, bot):
        zero = jnp.zeros_like(top)
        return jnp.concatenate([jnp.where(lo_half, top, zero), jnp.where(lo_half, zero, bot)], axis=0)

    kraw = z_ref[:, A_Q:A_Q + A_KV]
    v_all = z_ref[:, A_Q + A_KV:A_Q + 2 * A_KV]
    kn_all = kraw * lax.rsqrt(mean_sq_halves(kraw) + EPS) * kg_ref[...]
    k_ref[CHUNK:, :] = kn_all.astype(BF16)
    kr_ref[CHUNK:, :] = pltpu.roll(kn_all, A_HEAD_DIM, 1).astype(BF16)
    v_ref[CHUNK:, :] = v_all.astype(BF16)
    vr_ref[CHUNK:, :] = pltpu.roll(v_all, A_HEAD_DIM, 1).astype(BF16)
    kl_ref[...] = kn_all[TQ - CHUNK:, :]
    vl_ref[...] = v_all[TQ - CHUNK:, :]
    for i in range(N_PAIRS):
        ps = slice(i * PAIR, (i + 1) * PAIR)
        qraw = z_ref[:, ps]
        qn = qraw * lax.rsqrt(mean_sq_halves(qraw) + EPS) * (qg_ref[:, ps] * ATTN_SCALE)
        q_ref[:, ps] = qn.astype(BF16)

    def chunk(c, carry):
        r0 = pl.multiple_of(c * CHUNK, CHUNK)
        rows = pl.ds(r0, CHUNK)
        first = jnp.where(jnp.logical_and(j == 0, c == 0), 1, 0)

        both = pl.ds(r0, 2 * CHUNK)
        k2, k2r, v2, v2r = k_ref[both, :], kr_ref[both, :], v_ref[both, :], vr_ref[both, :]
        kbd = [block_diag(k2, k2r), block_diag(k2r, k2)]
        vbd = [block_diag(v2, v2r), block_diag(v2r, v2)]

        scores = []
        for i in range(N_PAIRS):
            s = lax.dot_general(q_ref[rows, i * PAIR:(i + 1) * PAIR], kbd[i // (A_GROUP // 2)], _NT,
                                preferred_element_type=F32)
            scores.append(s + tabp_ref[first, i])
        outs = []
        for i in range(N_PAIRS):
            es, rs = [], []
            for hh in range(2):
                sh = scores[i][:, hh * 2 * WINDOW:(hh + 1) * 2 * WINDOW]
                sk = sink_ref[2 * i + hh]
                m = jnp.maximum(jnp.max(sh, axis=-1, keepdims=True), sk)
                e = jnp.exp(sh - m)
                rs.append(1.0 / (jnp.sum(e, axis=-1, keepdims=True) + jnp.exp(sk - m)))
                es.append(e.astype(BF16))
            o = _dot(jnp.concatenate(es, axis=-1), vbd[i // (A_GROUP // 2)])
            outs.append(o * jnp.where(lo_half, rs[0], rs[1]))
        mix_ref[rows, 0:A_Q] = jnp.concatenate(outs, axis=-1).astype(BF16)

        zu = z_ref[rows, A_Q + 2 * A_KV:A_Q + 2 * A_KV + B_WIDTH]
        zv = z_ref[rows, A_Q + 2 * A_KV + B_WIDTH:AB_IN]
        u = _gelu(zu)
        vln = _layernorm(_gelu(zv), lng_ref[...], lnb_ref[...])
        vlb = vln.astype(BF16)
        sparts = []
        for i in range(B_GROUPS // 2):
            vpair = vlb[:, i * PAIR:(i + 1) * PAIR]
            sparts.append(_dot(wpair_ref[i], block_diag(vpair, vpair)))
        bm = u * (jnp.concatenate(sparts, axis=-1) + bsp_ref[...])
        mix_ref[rows, A_Q:AB_MIX] = bm.astype(BF16)

        gl_ref[...] = vln
        return carry

    lax.fori_loop(0, n_chunks, chunk, 0, unroll=AB_UNROLL)
    y_ref[0] = x_ref[0] + _dot(mix_ref[...], wout_ref[...])
    for ref in (k_ref, kr_ref, v_ref, vr_ref):
        ref[0:CHUNK, :] = ref[TQ:TQ + CHUNK, :]

    @pl.when(j == last_j)
    def _():
        knew_ref[0] = kl_ref[...]
        vnew_ref[0] = vl_ref[...]
        gv_ref[0] = gl_ref[...]


def _ab_prompt(x, nm, w_in, qg, kg, sink, tabp, lng, lnb, w_s, bsp, w_out, ck, cv, qx, kn_s, vn_s, sb):
    nb, seq, _ = x.shape
    nj = seq // TQ
    grid = (nb, nj)
    ns, _, wb = ck.shape
    sba = ns // (nb * nj)
    assert sba * nb * nj == ns
    blk = lambda b, j: (b, j, 0)
    per_b = lambda b, j: (b, 0, 0)
    step = lambda b, j: (b * nj + j, 0, 0)
    return pl.pallas_call(
        _ab_prompt_kernel,
        grid=grid,
        in_specs=[
            _SMEM,
            pl.BlockSpec((1, TQ, D_MODEL), blk),
            _full((1, D_MODEL)),
            _resident((D_MODEL, AB_IN)),
            _full((1, A_Q)),
            _full((1, A_KV)),
            _resident((2, N_PAIRS, WINDOW, 4 * WINDOW)),
            _full((1, B_WIDTH)),
            _full((1, B_WIDTH)),
            _resident((B_GROUPS, B_CHUNK, B_CHUNK)),
            _resident((B_CHUNK, B_WIDTH)),
            _resident((AB_MIX, D_MODEL)),
            pl.BlockSpec((sba, A_KV, wb), step),
            pl.BlockSpec((sba, A_KV, wb), step),
            pl.BlockSpec((sba, A_HEADS, A_KV), step),
            pl.BlockSpec((sba, 1, A_KV), step),
            pl.BlockSpec((sba, 1, A_KV), step),
            _full((A_HEADS, 2 * WINDOW)),
            _full((A_HEADS, 1)),
        ],
        out_specs=[
            pl.BlockSpec((1, TQ, D_MODEL), blk),
            pl.BlockSpec((1, WINDOW, A_KV), per_b),
            pl.BlockSpec((1, WINDOW, A_KV), per_b),
            pl.BlockSpec((1, B_CHUNK, B_WIDTH), per_b),
            pl.BlockSpec((sba, A_KV, wb), step),
            pl.BlockSpec((sba, A_KV, wb), step),
            pl.BlockSpec((sba, A_HEADS, A_KV), step),
        ],
        out_shape=[
            jax.ShapeDtypeStruct((nb, seq, D_MODEL), F32),
            jax.ShapeDtypeStruct((nb, WINDOW, A_KV), F32),
            jax.ShapeDtypeStruct((nb, WINDOW, A_KV), F32),
            jax.ShapeDtypeStruct((nb, B_CHUNK, B_WIDTH), F32),
            jax.ShapeDtypeStruct((ns, A_KV, wb), F32),
            jax.ShapeDtypeStruct((ns, A_KV, wb), F32),
            jax.ShapeDtypeStruct((ns, A_HEADS, A_KV), F32),
        ],
        scratch_shapes=[
            pltpu.VMEM((TQ, AB_IN), F32),
            pltpu.VMEM((TQ, AB_MIX), BF16),
            pltpu.VMEM((TQ, A_Q), BF16),
            pltpu.VMEM((CHUNK + TQ, A_KV), BF16),
            pltpu.VMEM((CHUNK + TQ, A_KV), BF16),
            pltpu.VMEM((CHUNK + TQ, A_KV), BF16),
            pltpu.VMEM((CHUNK + TQ, A_KV), BF16),
            pltpu.VMEM((B_GROUPS // 2, B_CHUNK, 2 * B_CHUNK), BF16),
            pltpu.VMEM((WINDOW, A_KV), F32),
            pltpu.VMEM((WINDOW, A_KV), F32),
            pltpu.VMEM((B_CHUNK, B_WIDTH), F32),
        ],
        compiler_params=pltpu.CompilerParams(
            dimension_semantics=("arbitrary", "arbitrary"), vmem_limit_bytes=VMEM_LIMIT),
        name="ab_prompt",
    )(sink, x, nm, w_in, qg, kg, tabp, lng, lnb, w_s, bsp, w_out,
      ck, cv, qx, kn_s[:, None, :], vn_s[:, None, :], sb, sink.reshape(A_HEADS, 1))


FF_TILE = 256


def _ffn_kernel(xp_ref, xs_ref, g_ref, wg_ref, wu_ref, wd_ref, yp_ref, ys_ref,
                wg_s, wu_s, wd_s, h0_s, acc_s, *, n_cast, n_prompt):
    s = pl.program_id(0)

    def gated(h, wg, wu):
        gate = _dot(h, wg)
        return (gate * jax.nn.sigmoid(gate) * _dot(h, wu)).astype(BF16)

    @pl.when(s == 0)
    def _():
        x = xp_ref[...]
        h0_s[...] = _rms(x, g_ref[...]).astype(BF16)
        acc_s[...] = x

    for c in range(n_cast):
        @pl.when(s == c)
        def _(c=c):
            tile = slice(c * FF_TILE, (c + 1) * FF_TILE)
            wg_t, wu_t, wd_t = (r[...].astype(BF16) for r in (wg_ref, wu_ref, wd_ref))
            wg_s[:, tile] = wg_t
            wu_s[:, tile] = wu_t
            wd_s[tile, :] = wd_t
            acc_s[...] += _dot(gated(h0_s[...], wg_t, wu_t), wd_t)

    @pl.when(s == n_cast - 1)
    def _():
        yp_ref[...] = acc_s[...]

    def swiglu(x):
        h = _rms(x, g_ref[...]).astype(BF16)
        return x + _dot(gated(h, wg_s[...], wu_s[...]), wd_s[...])

    @pl.when(jnp.logical_and(s >= n_cast, s < n_cast + n_prompt - 1))
    def _():
        yp_ref[...] = swiglu(xp_ref[...])

    @pl.when(s == n_cast + n_prompt - 1)
    def _():
        ys_ref[...] = swiglu(xs_ref[...])


def _ffn(xp, xs, g, w_gate, w_up, w_down, layer):
    rows, ns = xp.shape[0], xs.shape[0]
    n_cast, n_prompt = D_FF // FF_TILE, rows // TM
    w_tile = lambda s: jnp.minimum(s, n_cast - 1)
    row_blk = lambda s: (jnp.clip(s - (n_cast - 1), 0, n_prompt - 1), 0)
    return pl.pallas_call(
        functools.partial(_ffn_kernel, n_cast=n_cast, n_prompt=n_prompt),
        grid=(n_cast + n_prompt,),
        in_specs=[
            pl.BlockSpec((TM, D_MODEL), row_blk),
            _full((ns, D_MODEL)),
            _full((1, D_MODEL)),
            pl.BlockSpec((None, D_MODEL, FF_TILE), lambda s: (layer, 0, w_tile(s))),
            pl.BlockSpec((None, D_MODEL, FF_TILE), lambda s: (layer, 0, w_tile(s))),
            pl.BlockSpec((None, FF_TILE, D_MODEL), lambda s: (layer, w_tile(s), 0)),
        ],
        out_specs=[pl.BlockSpec((TM, D_MODEL), row_blk), _full((ns, D_MODEL))],
        out_shape=[jax.ShapeDtypeStruct((rows, D_MODEL), F32), jax.ShapeDtypeStruct((ns, D_MODEL), F32)],
        scratch_shapes=[
            pltpu.VMEM((D_MODEL, D_FF), BF16),
            pltpu.VMEM((D_MODEL, D_FF), BF16),
            pltpu.VMEM((D_FF, D_MODEL), BF16),
            pltpu.VMEM((TM, D_MODEL), BF16),
            pltpu.VMEM((TM, D_MODEL), F32),
        ],
        compiler_params=pltpu.CompilerParams(
            dimension_semantics=("arbitrary",), vmem_limit_bytes=VMEM_LIMIT),
        name="ffn",
    )(xp, xs, g, w_gate, w_up, w_down)


def _lower_bound(clb):
    m = jnp.max(clb, axis=0, keepdims=True)
    e = jnp.exp(clb - m)
    sm = e / jnp.sum(e, axis=0, keepdims=True)
    return (sm[0:1] + sm[1:2]) - sm[0:1]


def _split3(x):
    hi = x.astype(BF16)
    r = x - hi.astype(F32)
    mid = r.astype(BF16)
    lo = (r - mid.astype(F32)).astype(BF16)
    return hi, mid, lo


def _neg_abs(x):
    return lax.bitcast_convert_type(
        lax.bitcast_convert_type(x, jnp.uint32) | jnp.uint32(0x80000000), F32)


def _pair_level_table():
    t = np.arange(CHUNK)[:, None]
    s = np.arange(CHUNK)[None, :]
    lev = np.floor(np.log2(np.maximum(t ^ s, 1))).astype(np.int32)
    lev = np.where(t == s, -1, lev)
    return np.where(s > t, -2, lev).astype(np.int32)


def _level_operand(p, q, kk, f, b2):
    m = 2 ** p
    if m < VREG_ROWS:
        shape3 = (CHUNK // VREG_ROWS, VREG_ROWS, q.shape[1])
        sub = lax.broadcasted_iota(jnp.int32, (1, VREG_ROWS, q.shape[1]), 1)
        upper = ((sub >> p) & 1) == 1
        q3, k3 = q.reshape(shape3), kk.reshape(shape3)
        if p == 0:
            y = jnp.where(upper, q3 * f.reshape(shape3), k3)
        else:
            b3 = b2.reshape(shape3)
            be = b3[:, m - 1:m, :]
            for k in range(1, VREG_ROWS // (2 * m)):
                be = jnp.where(sub >= 2 * m * k, b3[:, 2 * m * k + m - 1:2 * m * k + m, :], be)
            y = jnp.where(upper, q3, k3) * jnp.exp2(_neg_abs(b3 - be))
        return y.reshape(q.shape).astype(BF16)
    parts = []
    for k in range(CHUNK // (2 * m)):
        lo = slice(2 * m * k, 2 * m * k + m)
        up = slice(2 * m * k + m, 2 * m * (k + 1))
        be = b2[2 * m * k + m - 1:2 * m * k + m, :]
        parts.append(kk[lo] * jnp.exp2(be - b2[lo]))
        parts.append(q[up] * jnp.exp2(b2[up] - be))
    return jnp.concatenate(parts, axis=0).astype(BF16)


def _merge_level(p, att, pm, lev):
    m = 2 ** p
    if m < VREG_ROWS:
        return jnp.where(lev == p, pm, att)
    col = lax.broadcasted_iota(jnp.int32, (1, CHUNK), 1)
    parts = []
    for k in range(CHUNK // (2 * m)):
        lo = slice(2 * m * k, 2 * m * k + m)
        up = slice(2 * m * k + m, 2 * m * (k + 1))
        parts.append(att[lo])
        parts.append(jnp.where((col >= 2 * m * k) & (col < 2 * m * k + m), pm[up], att[up]))
    return jnp.concatenate(parts, axis=0)


def _hgrn_prompt_kernel(x_ref, nm_ref, win_ref, clb_ref, on_ref, wout_ref, lev_ref,
                        ss_ref, fs_ref, qs_ref, is_ref,
                        y_ref, st_ref, sso_ref, os_ref,
                        z_ref, o_ref, stt_ref, k_ref):
    j = pl.program_id(1)
    last_j = pl.num_programs(1) - 1
    n_chunks = TQH // CHUNK
    n_levels = int(math.log2(CHUNK))

    @pl.when(j == 0)
    def _():
        stt_ref[...] = jnp.zeros_like(stt_ref)

    h = _rms(x_ref[0], nm_ref[...]).astype(BF16)
    z_ref[...] = _dot(h, win_ref[...])

    head_cols = lambda ref: [ref[:, hd * C_KEY_DIM:(hd + 1) * C_KEY_DIM].T for hd in range(C_HEADS)]
    f_cols, q_cols = head_cols(fs_ref), head_cols(qs_ref)
    out_rows = []
    for smp in range(ss_ref.shape[0]):
        parts = []
        for hd in range(C_HEADS):
            hs = slice(hd * C_VAL_DIM, (hd + 1) * C_VAL_DIM)
            fb = jnp.broadcast_to(f_cols[hd][:, smp:smp + 1], (C_KEY_DIM, C_VAL_DIM))
            sn = fb * ss_ref[smp, hd] + (1.0 - fb) * is_ref[smp:smp + 1, hs]
            sso_ref[smp, hd] = sn
            parts.append(jnp.sum(q_cols[hd][:, smp:smp + 1] * sn, axis=0, keepdims=True))
        out_rows.append(jnp.concatenate(parts, axis=-1))
    os_ref[...] = jnp.concatenate(out_rows, axis=0)

    lb = _lower_bound(clb_ref[...])

    row = lax.broadcasted_iota(jnp.int32, (CHUNK, CHUNK), 0)
    col = lax.broadcasted_iota(jnp.int32, (CHUNK, CHUNK), 1)
    ltri = (row >= col).astype(BF16)

    def chunk_rows(c):
        return pl.ds(pl.multiple_of(c * CHUNK, CHUNK), CHUNK)

    def prefix(g, worst):
        rows = [chunk_rows(g * PREFIX_GROUP + i) for i in range(PREFIX_GROUP)]
        gates = [z_ref[r, C_F:2 * C_F] for r in rows]
        for r, gate in zip(rows, gates):
            f_all = lb + (1.0 - lb) * jax.nn.sigmoid(gate)
            k_ref[r, :] = 1.0 - f_all
            hi, mid, lo = _split3(jnp.log2(f_all))
            b2 = (_dot(ltri, hi) + _dot(ltri, mid)) + _dot(ltri, lo)
            z_ref[r, C_F:2 * C_F] = b2
            b_mid = b2[CHUNK // 2 - 1:CHUNK // 2, :]
            b_last = b2[CHUNK - 1:CHUNK, :]
            worst = jnp.maximum(worst, jnp.maximum(-b_mid, b_mid - b_last))
        return worst

    worst = lax.fori_loop(0, n_chunks // PREFIX_GROUP, prefix, jnp.zeros((1, C_F), F32))
    bounded = jnp.max(worst) <= SAFE_LOG2_RANGE

    def finish_head(rows, hd, o):
        gt = z_ref[rows, 2 * C_F + C_V + hd * C_VAL_DIM:2 * C_F + C_V + (hd + 1) * C_VAL_DIM]
        o = _rms(o, on_ref[...]) * jax.nn.sigmoid(gt)
        o_ref[rows, hd * C_VAL_DIM:(hd + 1) * C_VAL_DIM] = o.astype(BF16)

    def head_inputs(rows, hd):
        q = z_ref[rows, hd * C_KEY_DIM:(hd + 1) * C_KEY_DIM]
        kk = k_ref[rows, hd * C_KEY_DIM:(hd + 1) * C_KEY_DIM]
        b2 = z_ref[rows, C_F + hd * C_KEY_DIM:C_F + (hd + 1) * C_KEY_DIM]
        ivb = z_ref[rows, 2 * C_F + hd * C_VAL_DIM:2 * C_F + (hd + 1) * C_VAL_DIM].astype(BF16)
        return q, kk, b2, ivb

    def factored_chunk(c, carry):
        rows = chunk_rows(c)
        for hd in range(C_HEADS):
            q, kk, b2, ivb = head_inputs(rows, hd)
            b_mid = b2[CHUNK // 2 - 1:CHUNK // 2, :]
            b_last = b2[CHUNK - 1:CHUNK, :]
            qs = (q * jnp.exp2(b2 - b_mid)).astype(BF16)
            kd = (kk * jnp.exp2(b_mid - b2)).astype(BF16)
            att = jnp.where(row >= col, lax.dot_general(qs, kd, _NT, preferred_element_type=F32), 0.0)
            stt = stt_ref[hd]
            o = lax.dot_general(qs, (stt * jnp.exp2(b_mid)).astype(BF16), _NT,
                                preferred_element_type=F32) + _dot(att.astype(BF16), ivb)
            stt_ref[hd] = stt * jnp.exp2(b_last) + jnp.exp2(b_last - b_mid) * lax.dot_general(
                ivb, kd, _TN, preferred_element_type=F32)
            finish_head(rows, hd, o)
        return carry

    def tree_chunk(c, carry):
        rows = chunk_rows(c)
        lev = lev_ref[...]

        def products(hd):
            q, kk, b2, ivb = head_inputs(rows, hd)
            diag = jnp.sum(q * kk, axis=-1, keepdims=True)
            pms = []
            for p in range(n_levels):
                y = _level_operand(p, q, kk, 1.0 - kk, b2)
                pms.append(lax.dot_general(y, y, _NT, preferred_element_type=F32))
            stt = stt_ref[hd]
            o_prev = lax.dot_general((q * jnp.exp2(b2)).astype(BF16), stt.astype(BF16), _NT,
                                     preferred_element_type=F32)
            b_last = b2[CHUNK - 1:CHUNK, :]
            kd = (kk * jnp.exp2(b_last - b2)).astype(BF16)
            stt_ref[hd] = stt * jnp.exp2(b_last) + lax.dot_general(
                ivb, kd, _TN, preferred_element_type=F32)
            return diag, pms, o_prev, ivb

        def finish(hd, diag, pms, o_prev, ivb):
            att = jnp.where(lev == -1, diag, 0.0)
            for p in range(n_levels):
                att = _merge_level(p, att, pms[p], lev)
            finish_head(rows, hd, o_prev + _dot(att.astype(BF16), ivb))

        pending = [products(hd) for hd in range(HEAD_SKEW)]
        for hd in range(C_HEADS):
            if hd + HEAD_SKEW < C_HEADS:
                pending.append(products(hd + HEAD_SKEW))
            finish(hd, *pending.pop(0))
        return carry

    @pl.when(bounded)
    def _():
        lax.fori_loop(0, n_chunks, factored_chunk, 0, unroll=FACTORED_UNROLL)

    @pl.when(jnp.logical_not(bounded))
    def _():
        lax.fori_loop(0, n_chunks, tree_chunk, 0)

    y_ref[0] = x_ref[0] + _dot(o_ref[...], wout_ref[...])

    @pl.when(j == last_j)
    def _():
        for hd in range(C_HEADS):
            st_ref[0, hd] = stt_ref[hd].T


def _hgrn_prompt(x, nm, w_in, clb, on, w_out, state_s, f_s, q_s, i_s):
    nb, seq, _ = x.shape
    nj = seq // TQH
    ns = state_s.shape[0]
    sbh = ns // (nb * nj)
    assert sbh * nb * nj == ns
    blk = lambda b, j: (b, j, 0)
    step = lambda b, j: (b * nj + j, 0, 0)
    step4 = lambda b, j: (b * nj + j, 0, 0, 0)
    rows3 = lambda a: a.reshape(ns // sbh, sbh, a.shape[1])
    y, st, st_s, o_s = pl.pallas_call(
        _hgrn_prompt_kernel,
        grid=(nb, nj),
        in_specs=[
            pl.BlockSpec((1, TQH, D_MODEL), blk),
            _full((1, D_MODEL)),
            _resident((D_MODEL, C_IN)),
            _full((DEPTH, C_F)),
            _full((1, C_VAL_DIM)),
            _resident((C_V, D_MODEL)),
            _full((CHUNK, CHUNK)),
            pl.BlockSpec((sbh, C_HEADS, C_KEY_DIM, C_VAL_DIM), step4),
            pl.BlockSpec((None, sbh, C_F), step),
            pl.BlockSpec((None, sbh, C_F), step),
            pl.BlockSpec((None, sbh, C_V), step),
        ],
        out_specs=[
            pl.BlockSpec((1, TQH, D_MODEL), blk),
            pl.BlockSpec((1, C_HEADS, C_KEY_DIM, C_VAL_DIM), lambda b, j: (b, 0, 0, 0)),
            pl.BlockSpec((sbh, C_HEADS, C_KEY_DIM, C_VAL_DIM), step4),
            pl.BlockSpec((None, sbh, C_V), step),
        ],
        out_shape=[
            jax.ShapeDtypeStruct((nb, seq, D_MODEL), F32),
            jax.ShapeDtypeStruct((nb, C_HEADS, C_KEY_DIM, C_VAL_DIM), F32),
            jax.ShapeDtypeStruct(state_s.shape, F32),
            jax.ShapeDtypeStruct((ns // sbh, sbh, C_V), F32),
        ],
        scratch_shapes=[
            pltpu.VMEM((TQH, C_IN), F32),
            pltpu.VMEM((TQH, C_V), BF16),
            pltpu.VMEM((C_HEADS, C_VAL_DIM, C_KEY_DIM), F32),
            pltpu.VMEM((TQH, C_F), F32),
        ],
        compiler_params=pltpu.CompilerParams(
            dimension_semantics=("arbitrary", "arbitrary"), vmem_limit_bytes=VMEM_LIMIT),
        name="hgrn_prompt",
    )(x, nm, w_in, clb, on, w_out, jnp.asarray(_pair_level_table()),
      state_s, rows3(f_s), rows3(q_s), rows3(i_s))
    return y, st, st_s, o_s.reshape(ns, C_V)


def _ab_sample_proj_kernel(w00_ref, b0_ref, x_ref, nm_ref, win_ref, qn_ref, kn_ref, lng_ref, lnb_ref,
                           qx_ref, knew_ref, vnew_ref, bm_ref, gv_ref):
    n = x_ref.shape[0]
    h = _rms(x_ref[...], nm_ref[...]).astype(BF16)
    z = _dot(h, win_ref[...])
    zeros = jnp.zeros((n, A_HEAD_DIM), F32)
    for hh in range(A_HEADS):
        qh = _rms(z[:, hh * A_HEAD_DIM:(hh + 1) * A_HEAD_DIM], qn_ref[...]) * ATTN_SCALE
        qx_ref[hh] = jnp.concatenate([qh, zeros] if hh // A_GROUP == 0 else [zeros, qh], axis=-1)
    kparts = []
    for g in range(A_KV_HEADS):
        kparts.append(_rms(z[:, A_Q + g * A_HEAD_DIM:A_Q + (g + 1) * A_HEAD_DIM], kn_ref[...]))
    knew_ref[...] = jnp.concatenate(kparts, axis=-1)
    vnew_ref[...] = z[:, A_Q + A_KV:A_Q + 2 * A_KV]

    u = _gelu(z[:, A_Q + 2 * A_KV:A_Q + 2 * A_KV + B_WIDTH])
    vln = _layernorm(_gelu(z[:, A_Q + 2 * A_KV + B_WIDTH:AB_IN]), lng_ref[...], lnb_ref[...])
    grp = lax.broadcasted_iota(jnp.int32, (1, B_WIDTH), 1) // B_GROUP_DIM
    srow = jnp.zeros((1, B_WIDTH), F32)
    brow = jnp.zeros((1, B_WIDTH), F32)
    for g in range(B_GROUPS):
        srow = jnp.where(grp == g, w00_ref[g], srow)
        brow = jnp.where(grp == g, b0_ref[g], brow)
    bm_ref[...] = u * (vln * srow + brow)
    gv_ref[...] = vln


def _ab_sample_proj(x, nm, w_in, qn, kn, lng, lnb, w00, b0):
    n = x.shape[0]
    return pl.pallas_call(
        _ab_sample_proj_kernel,
        in_specs=[_SMEM, _SMEM] + [pl.BlockSpec(memory_space=pltpu.VMEM)] * 7,
        out_shape=[
            jax.ShapeDtypeStruct((A_HEADS, n, A_KV), F32),
            jax.ShapeDtypeStruct((n, A_KV), F32),
            jax.ShapeDtypeStruct((n, A_KV), F32),
            jax.ShapeDtypeStruct((n, B_WIDTH), F32),
            jax.ShapeDtypeStruct((n, B_WIDTH), F32),
        ],
        compiler_params=pltpu.CompilerParams(vmem_limit_bytes=VMEM_LIMIT),
        name="ab_sample_proj",
    )(w00, b0, x, nm, w_in, qn, kn, lng, lnb)


def _residual_proj_kernel(x_ref, m_ref, w_ref, y_ref):
    y_ref[...] = x_ref[...] + _dot(m_ref[...].astype(BF16), w_ref[...])


def _residual_proj(x, mix, w):
    return pl.pallas_call(
        _residual_proj_kernel,
        out_shape=jax.ShapeDtypeStruct(x.shape, F32),
        name="residual_proj",
    )(x, mix, w)


def _hgrn_sample_proj_kernel(x_ref, nm_ref, win_ref, clb_ref, q_ref, f_ref, i_ref, sg_ref):
    h = _rms(x_ref[...], nm_ref[...]).astype(BF16)
    z = _dot(h, win_ref[...])
    lb = _lower_bound(clb_ref[...])
    q_ref[...] = z[:, 0:C_F]
    f_ref[...] = lb + (1.0 - lb) * jax.nn.sigmoid(z[:, C_F:2 * C_F])
    i_ref[...] = z[:, 2 * C_F:2 * C_F + C_V]
    sg_ref[...] = jax.nn.sigmoid(z[:, 2 * C_F + C_V:C_IN])


def _hgrn_sample_proj(x, nm, w_in, clb):
    n = x.shape[0]
    return pl.pallas_call(
        _hgrn_sample_proj_kernel,
        out_shape=[jax.ShapeDtypeStruct((n, C_F), F32)] * 2 + [jax.ShapeDtypeStruct((n, C_V), F32)] * 2,
        compiler_params=pltpu.CompilerParams(vmem_limit_bytes=VMEM_LIMIT),
        name="hgrn_sample_proj",
    )(x, nm, w_in, clb)


def _hgrn_sample_out_kernel(o_ref, sg_ref, on_ref, w_ref, x_ref, y_ref):
    parts = []
    for hd in range(C_HEADS):
        parts.append(_rms(o_ref[:, hd * C_VAL_DIM:(hd + 1) * C_VAL_DIM], on_ref[...]))
    on = jnp.concatenate(parts, axis=-1) * sg_ref[...]
    y_ref[...] = x_ref[...] + _dot(on.astype(BF16), w_ref[...])


def _hgrn_sample_out(o, sg, on, w_out, x):
    return pl.pallas_call(
        _hgrn_sample_out_kernel,
        out_shape=jax.ShapeDtypeStruct(x.shape, F32),
        name="hgrn_sample_out",
    )(o, sg, on, w_out, x)


def kernel(x_prompt, x_sample, cache_k, cache_v, state_hgrn, norm_mix, norm_ffn, w_in_ab, w_out_ab,
           q_norm, k_norm, attn_sink, rel_bias, gmlp_ln_g, gmlp_ln_b, gmlp_w_s, gmlp_b_s,
           w_in_c, c_lower_bounds, c_out_norm, w_out_c, w_gate, w_up, w_down):
    assert norm_mix.shape[0] == DEPTH == 2 and w_in_ab.shape[0] == 1 and w_in_c.shape[0] == 1
    nb, seq, _ = x_prompt.shape
    ns = x_sample.shape[0]
    assert x_sample.shape[1] == 1 and cache_k.shape[2] == WINDOW

    row = lambda v: v.reshape(1, -1)
    bf = lambda w: w.astype(BF16)
    w_in_ab0, w_out_ab0 = bf(w_in_ab[0]), bf(w_out_ab[0])
    w_in_c0, w_out_c0 = bf(w_in_c[0]), bf(w_out_c[0])
    nm, nf = norm_mix, norm_ffn
    qn, kn = row(q_norm[0]), row(k_norm[0])
    lng, lnb = row(gmlp_ln_g[0]), row(gmlp_ln_b[0])
    sink = attn_sink[0]

    tab, tabp = _bias_table(rel_bias)

    xs = x_sample.reshape(ns, D_MODEL)
    qx, knew_s, vnew_s, bm_s, gv_s = _ab_sample_proj(
        xs, row(nm[0]), w_in_ab0, qn, kn, lng, lnb, gmlp_w_s[0, :, 0, 0], gmlp_b_s[0, :, 0])
    sb = jnp.pad(tab[:, WINDOW - 1, WINDOW - 1:], ((0, 0), (0, WINDOW - 1)))
    to_t = lambda c: c.transpose(0, 2, 3, 1).reshape(ns, A_KV, WINDOW)
    xp, knew_p, vnew_p, gv_p, nk_s, nv_s, om = _ab_prompt(
        x_prompt, row(nm[0]), w_in_ab0, jnp.tile(qn, (1, A_HEADS)), jnp.tile(kn, (1, A_KV_HEADS)),
        sink, tabp, lng, lnb, gmlp_w_s[0], jnp.repeat(gmlp_b_s[0].T, B_GROUP_DIM, axis=1), w_out_ab0,
        to_t(cache_k[0]), to_t(cache_v[0]), qx.transpose(1, 0, 2), knew_s, vnew_s, sb)
    om = om.reshape(ns, A_KV_HEADS, A_GROUP, A_KV_HEADS, A_HEAD_DIM)
    a_s = jnp.stack([om[:, g, :, g, :] for g in range(A_KV_HEADS)], axis=1).reshape(ns, A_Q)
    xs = _residual_proj(xs, jnp.concatenate([a_s, bm_s], axis=-1), w_out_ab0)
    xp, xs = _ffn(xp.reshape(nb * seq, D_MODEL), xs, row(nf[0]), w_gate, w_up, w_down, 0)

    q_s, f_s, i_s, sg_s = _hgrn_sample_proj(xs, row(nm[1]), w_in_c0, c_lower_bounds)
    xp, st_p, st_s, o_s = _hgrn_prompt(xp.reshape(nb, seq, D_MODEL), row(nm[1]), w_in_c0, c_lower_bounds,
                                       row(c_out_norm[0]), w_out_c0, state_hgrn[0], f_s, q_s, i_s)
    xs = _hgrn_sample_out(o_s, sg_s, row(c_out_norm[0]), w_out_c0, xs)
    xp, xs = _ffn(xp.reshape(nb * seq, D_MODEL), xs, row(nf[1]), w_gate, w_up, w_down, 1)

    kv5 = lambda a: a.reshape(1, a.shape[0], WINDOW, A_KV_HEADS, A_HEAD_DIM)
    from_t = lambda a: a.reshape(ns, A_KV_HEADS, A_HEAD_DIM, WINDOW).transpose(0, 3, 1, 2)[None]
    return (xp.reshape(nb, seq, D_MODEL), xs.reshape(ns, 1, D_MODEL),
            kv5(knew_p), kv5(vnew_p), from_t(nk_s), from_t(nv_s),
            gv_p[None], gv_s.reshape(1, ns, 1, B_WIDTH),
            st_p[None], st_s[None])
```

```python
import functools
import math

import jax
import jax.numpy as jnp
import numpy as np
from jax import lax
from jax.experimental import pallas as pl
from jax.experimental.pallas import tpu as pltpu

F32 = jnp.float32
BF16 = jnp.bfloat16

D_MODEL = 1024
DEPTH = 2
A_HEADS = 8
A_KV_HEADS = 2
A_GROUP = A_HEADS // A_KV_HEADS
A_HEAD_DIM = 64
WINDOW = 128
ATTN_SCALE = A_HEAD_DIM ** -0.5
NUM_BUCKETS = 32
MAX_DISTANCE = 128
A_Q = A_HEADS * A_HEAD_DIM
A_KV = A_KV_HEADS * A_HEAD_DIM
B_GROUPS = 8
B_GROUP_DIM = 64
B_WIDTH = B_GROUPS * B_GROUP_DIM
B_CHUNK = 128
AB_IN = A_Q + 2 * A_KV + 2 * B_WIDTH
AB_MIX = A_Q + B_WIDTH
C_HEADS = 8
C_KEY_DIM = 128
C_VAL_DIM = 128
C_F = C_HEADS * C_KEY_DIM
C_V = C_HEADS * C_VAL_DIM
C_IN = 2 * C_F + 2 * C_V
D_FF = 2816
EPS = 1e-6

NEG = -1e30

VMEM_LIMIT = 56 * 1024 * 1024
VREG_ROWS = 8

CHUNK = 128
TQ = 1024
TQH = 512
TM = 512
HEAD_SKEW = 2
SAFE_LOG2_RANGE = 64.0
PREFIX_GROUP = 4
FACTORED_UNROLL = 4
AB_UNROLL = 2

_NT = (((1,), (1,)), ((), ()))
_TN = (((0,), (0,)), ((), ()))


def _rms(x, g):
    return x * lax.rsqrt(jnp.mean(x * x, axis=-1, keepdims=True) + EPS) * g


def _gelu(x):
    return 0.5 * x * (1.0 + lax.erf(x * math.sqrt(0.5)))


def _layernorm(x, g, b):
    xc = x - jnp.mean(x, axis=-1, keepdims=True)
    return xc * lax.rsqrt(jnp.mean(xc * xc, axis=-1, keepdims=True) + EPS) * g + b


def _dot(a, b):
    return jnp.dot(a, b, preferred_element_type=F32)


def _full(shape):
    n = len(shape)
    return pl.BlockSpec(shape, lambda *_: (0,) * n)


def _resident(shape):
    n = len(shape)
    return pl.BlockSpec(shape, lambda *_: (0,) * n, pipeline_mode=pl.Buffered(1))


_SMEM = pl.BlockSpec(memory_space=pltpu.SMEM)


def _bias_table_kernel(rel_ref, tab_ref, tabp_ref):
    qi = lax.broadcasted_iota(jnp.int32, (WINDOW, 2 * WINDOW), 0)
    kj = lax.broadcasted_iota(jnp.int32, (WINDOW, 2 * WINDOW), 1)
    dist = qi + WINDOW - kj
    ok = (dist >= 0) & (dist < WINDOW)
    max_exact = NUM_BUCKETS // 2
    d = jnp.maximum(dist, 0)
    dl = jnp.maximum(d, 1).astype(F32)
    v = (jnp.log(dl / max_exact) / math.log(MAX_DISTANCE / max_exact) * (NUM_BUCKETS - max_exact))
    far = d >= max_exact
    hits = []
    for b in range(NUM_BUCKETS):
        if b < max_exact:
            hits.append(d == b)
        elif b < NUM_BUCKETS - 1:
            hits.append(far & (v >= b - max_exact) & (v < b - max_exact + 1))
        else:
            hits.append(far & (v >= b - max_exact))
    for h in range(A_HEADS):
        acc = jnp.zeros((WINDOW, 2 * WINDOW), F32)
        for b in range(NUM_BUCKETS):
            acc = jnp.where(hits[b], rel_ref[b, h], acc)
        t = jnp.where(ok, acc, NEG)
        tab_ref[h] = t
        cols = slice((h % 2) * 2 * WINDOW, (h % 2 + 1) * 2 * WINDOW)
        tabp_ref[0, h // 2, :, cols] = t
        tabp_ref[1, h // 2, :, cols] = jnp.where(kj < WINDOW, NEG, t)


def _bias_table(rel_bias):
    return pl.pallas_call(
        _bias_table_kernel,
        out_shape=[
            jax.ShapeDtypeStruct((A_HEADS, WINDOW, 2 * WINDOW), F32),
            jax.ShapeDtypeStruct((2, A_HEADS // 2, WINDOW, 4 * WINDOW), F32),
        ],
        in_specs=[_SMEM],
        name="bias_table",
    )(rel_bias)


PAIR = 2 * A_HEAD_DIM
N_PAIRS = A_HEADS // 2


def _sample_cache_attention(ck_ref, cv_ref, qx_ref, kn_ref, vn_ref, sb_ref, sink_ref,
                            nk_ref, nv_ref, om_ref):
    wb = ck_ref.shape[2]
    head = lax.broadcasted_iota(jnp.int32, (1, A_HEADS, A_KV), 1)
    lane = lax.broadcasted_iota(jnp.int32, (1, A_HEADS, A_KV), 2)
    own_group = (head // A_GROUP) == (lane // A_HEAD_DIM)
    newest = lax.broadcasted_iota(jnp.int32, (1, 1, wb), 2) == wb - 1
    sink = sink_ref[...][None]
    kc, vc = ck_ref[...], cv_ref[...]
    kn, vn = kn_ref[...], vn_ref[...]
    kn_cols, vn_cols = kn_ref[:, 0, :].T, vn_ref[:, 0, :].T
    for i in range(kc.shape[0]):
        nk_ref[i] = jnp.where(newest[0], kn_cols[:, i:i + 1], pltpu.roll(kc[i], wb - 1, 1))
        nv_ref[i] = jnp.where(newest[0], vn_cols[:, i:i + 1], pltpu.roll(vc[i], wb - 1, 1))
    q = qx_ref[...]
    s = jnp.einsum('bhd,bdk->bhk', q.astype(BF16), kc.astype(BF16), preferred_element_type=F32)
    s = s + sb_ref[:, 0:wb][None]
    sn = jnp.sum(q * kn, axis=-1, keepdims=True) + sb_ref[:, wb:wb + 1][None]
    m = jnp.maximum(jnp.maximum(jnp.max(s, axis=-1, keepdims=True), sn), sink)
    e = jnp.exp(s - m)
    en = jnp.exp(sn - m)
    r = 1.0 / (jnp.sum(e, axis=-1, keepdims=True) + en + jnp.exp(sink - m))
    o = jnp.einsum('bhk,bdk->bhd', (e * r).astype(BF16), vc.astype(BF16),
                   preferred_element_type=F32) + (en * r) * vn
    om_ref[...] = jnp.where(own_group, o, 0.0)


def _ab_prompt_kernel(sink_ref, x_ref, nm_ref, win_ref, qg_ref, kg_ref, tabp_ref, lng_ref, lnb_ref,
                      ws_ref, bsp_ref, wout_ref,
                      ck_ref, cv_ref, qx_ref, kn_ref, vn_ref, sb_ref, sinkc_ref,
                      y_ref, knew_ref, vnew_ref, gv_ref, nk_ref, nv_ref, om_ref,
                      z_ref, mix_ref, q_ref, k_ref, kr_ref, v_ref, vr_ref, wpair_ref, kl_ref, vl_ref, gl_ref):
    j = pl.program_id(1)
    last_j = pl.num_programs(1) - 1
    n_chunks = TQ // CHUNK

    @pl.when(j == 0)
    def _():
        for ref in (k_ref, kr_ref, v_ref, vr_ref):
            ref[0:CHUNK, :] = jnp.zeros((CHUNK, A_KV), BF16)
        row = lax.broadcasted_iota(jnp.int32, (B_CHUNK, B_CHUNK), 0)
        col = lax.broadcasted_iota(jnp.int32, (B_CHUNK, B_CHUNK), 1)
        for g in range(B_GROUPS):
            wpair_ref[g // 2, :, (g % 2) * B_CHUNK:(g % 2 + 1) * B_CHUNK] = jnp.where(
                row >= col, ws_ref[g], 0.0).astype(BF16)

    h = _rms(x_ref[0], nm_ref[...]).astype(BF16)
    z_ref[...] = _dot(h, win_ref[...])

    _sample_cache_attention(ck_ref, cv_ref, qx_ref, kn_ref, vn_ref, sb_ref, sinkc_ref,
                            nk_ref, nv_ref, om_ref)

    lo_half = lax.broadcasted_iota(jnp.int32, (1, PAIR), 1) < A_HEAD_DIM

    def mean_sq_halves(x):
        x2 = x * x
        lo_sum = jnp.sum(jnp.where(lo_half, x2, 0.0), axis=-1, keepdims=True)
        hi_sum = jnp.sum(jnp.where(lo_half, 0.0, x2), axis=-1, keepdims=True)
        return jnp.where(lo_half, lo_sum, hi_sum) * (1.0 / A_HEAD_DIM)

    def block_diag(top, bot):
        zero = jnp.zeros_like(top)
        return jnp.concatenate([jnp.where(lo_half, top, zero), jnp.where(lo_half, zero, bot)], axis=0)

    kraw = z_ref[:, A_Q:A_Q + A_KV]
    v_all = z_ref[:, A_Q + A_KV:A_Q + 2 * A_KV]
    kn_all = kraw * lax.rsqrt(mean_sq_halves(kraw) + EPS) * kg_ref[...]
    k_ref[CHUNK:, :] = kn_all.astype(BF16)
    kr_ref[CHUNK:, :] = pltpu.roll(kn_all, A_HEAD_DIM, 1).astype(BF16)
    v_ref[CHUNK:, :] = v_all.astype(BF16)
    vr_ref[CHUNK:, :] = pltpu.roll(v_all, A_HEAD_DIM, 1).astype(BF16)
    kl_ref[...] = kn_all[TQ - CHUNK:, :]
    vl_ref[...] = v_all[TQ - CHUNK:, :]
    for i in range(N_PAIRS):
        ps = slice(i * PAIR, (i + 1) * PAIR)
        qraw = z_ref[:, ps]
        qn = qraw * lax.rsqrt(mean_sq_halves(qraw) + EPS) * (qg_ref[:, ps] * ATTN_SCALE)
        q_ref[:, ps] = qn.astype(BF16)

    def chunk(c, carry):
        r0 = pl.multiple_of(c * CHUNK, CHUNK)
        rows = pl.ds(r0, CHUNK)
        first = jnp.where(jnp.logical_and(j == 0, c == 0), 1, 0)

        both = pl.ds(r0, 2 * CHUNK)
        k2, k2r, v2, v2r = k_ref[both, :], kr_ref[both, :], v_ref[both, :], vr_ref[both, :]
        kbd = [block_diag(k2, k2r), block_diag(k2r, k2)]
        vbd = [block_diag(v2, v2r), block_diag(v2r, v2)]

        scores = []
        for i in range(N_PAIRS):
            s = lax.dot_general(q_ref[rows, i * PAIR:(i + 1) * PAIR], kbd[i // (A_GROUP // 2)], _NT,
                                preferred_element_type=F32)
            scores.append(s + tabp_ref[first, i])
        outs = []
        for i in range(N_PAIRS):
            es, rs = [], []
            for hh in range(2):
                sh = scores[i][:, hh * 2 * WINDOW:(hh + 1) * 2 * WINDOW]
                sk = sink_ref[2 * i + hh]
                m = jnp.maximum(jnp.max(sh, axis=-1, keepdims=True), sk)
                e = jnp.exp(sh - m)
                rs.append(1.0 / (jnp.sum(e, axis=-1, keepdims=True) + jnp.exp(sk - m)))
                es.append(e.astype(BF16))
            o = _dot(jnp.concatenate(es, axis=-1), vbd[i // (A_GROUP // 2)])
            outs.append(o * jnp.where(lo_half, rs[0], rs[1]))
        mix_ref[rows, 0:A_Q] = jnp.concatenate(outs, axis=-1).astype(BF16)

        zu = z_ref[rows, A_Q + 2 * A_KV:A_Q + 2 * A_KV + B_WIDTH]
        zv = z_ref[rows, A_Q + 2 * A_KV + B_WIDTH:AB_IN]
        u = _gelu(zu)
        vln = _layernorm(_gelu(zv), lng_ref[...], lnb_ref[...])
        vlb = vln.astype(BF16)
        sparts = []
        for i in range(B_GROUPS // 2):
            vpair = vlb[:, i * PAIR:(i + 1) * PAIR]
            sparts.append(_dot(wpair_ref[i], block_diag(vpair, vpair)))
        bm = u * (jnp.concatenate(sparts, axis=-1) + bsp_ref[...])
        mix_ref[rows, A_Q:AB_MIX] = bm.astype(BF16)

        gl_ref[...] = vln
        return carry

    lax.fori_loop(0, n_chunks, chunk, 0, unroll=AB_UNROLL)
    y_ref[0] = x_ref[0] + _dot(mix_ref[...], wout_ref[...])
    for ref in (k_ref, kr_ref, v_ref, vr_ref):
        ref[0:CHUNK, :] = ref[TQ:TQ + CHUNK, :]

    @pl.when(j == last_j)
    def _():
        knew_ref[0] = kl_ref[...]
        vnew_ref[0] = vl_ref[...]
        gv_ref[0] = gl_ref[...]


def _ab_prompt(x, nm, w_in, qg, kg, sink, tabp, lng, lnb, w_s, bsp, w_out, ck, cv, qx, kn_s, vn_s, sb):
    nb, seq, _ = x.shape
    nj = seq // TQ
    grid = (nb, nj)
    ns, _, wb = ck.shape
    sba = ns // (nb * nj)
    assert sba * nb * nj == ns
    blk = lambda b, j: (b, j, 0)
    per_b = lambda b, j: (b, 0, 0)
    step = lambda b, j: (b * nj + j, 0, 0)
    return pl.pallas_call(
        _ab_prompt_kernel,
        grid=grid,
        in_specs=[
            _SMEM,
            pl.BlockSpec((1, TQ, D_MODEL), blk),
            _full((1, D_MODEL)),
            _resident((D_MODEL, AB_IN)),
            _full((1, A_Q)),
            _full((1, A_KV)),
            _resident((2, N_PAIRS, WINDOW, 4 * WINDOW)),
            _full((1, B_WIDTH)),
            _full((1, B_WIDTH)),
            _resident((B_GROUPS, B_CHUNK, B_CHUNK)),
            _resident((B_CHUNK, B_WIDTH)),
            _resident((AB_MIX, D_MODEL)),
            pl.BlockSpec((sba, A_KV, wb), step),
            pl.BlockSpec((sba, A_KV, wb), step),
            pl.BlockSpec((sba, A_HEADS, A_KV), step),
            pl.BlockSpec((sba, 1, A_KV), step),
            pl.BlockSpec((sba, 1, A_KV), step),
            _full((A_HEADS, 2 * WINDOW)),
            _full((A_HEADS, 1)),
        ],
        out_specs=[
            pl.BlockSpec((1, TQ, D_MODEL), blk),
            pl.BlockSpec((1, WINDOW, A_KV), per_b),
            pl.BlockSpec((1, WINDOW, A_KV), per_b),
            pl.BlockSpec((1, B_CHUNK, B_WIDTH), per_b),
            pl.BlockSpec((sba, A_KV, wb), step),
            pl.BlockSpec((sba, A_KV, wb), step),
            pl.BlockSpec((sba, A_HEADS, A_KV), step),
        ],
        out_shape=[
            jax.ShapeDtypeStruct((nb, seq, D_MODEL), F32),
            jax.ShapeDtypeStruct((nb, WINDOW, A_KV), F32),
            jax.ShapeDtypeStruct((nb, WINDOW, A_KV), F32),
            jax.ShapeDtypeStruct((nb, B_CHUNK, B_WIDTH), F32),
            jax.ShapeDtypeStruct((ns, A_KV, wb), F32),
            jax.ShapeDtypeStruct((ns, A_KV, wb), F32),
            jax.ShapeDtypeStruct((ns, A_HEADS, A_KV), F32),
        ],
        scratch_shapes=[
            pltpu.VMEM((TQ, AB_IN), F32),
            pltpu.VMEM((TQ, AB_MIX), BF16),
            pltpu.VMEM((TQ, A_Q), BF16),
            pltpu.VMEM((CHUNK + TQ, A_KV), BF16),
            pltpu.VMEM((CHUNK + TQ, A_KV), BF16),
            pltpu.VMEM((CHUNK + TQ, A_KV), BF16),
            pltpu.VMEM((CHUNK + TQ, A_KV), BF16),
            pltpu.VMEM((B_GROUPS // 2, B_CHUNK, 2 * B_CHUNK), BF16),
            pltpu.VMEM((WINDOW, A_KV), F32),
            pltpu.VMEM((WINDOW, A_KV), F32),
            pltpu.VMEM((B_CHUNK, B_WIDTH), F32),
        ],
        compiler_params=pltpu.CompilerParams(
            dimension_semantics=("arbitrary", "arbitrary"), vmem_limit_bytes=VMEM_LIMIT),
        name="ab_prompt",
    )(sink, x, nm, w_in, qg, kg, tabp, lng, lnb, w_s, bsp, w_out,
      ck, cv, qx, kn_s[:, None, :], vn_s[:, None, :], sb, sink.reshape(A_HEADS, 1))


FF_TILE = 256


def _ffn_kernel(xp_ref, xs_ref, ms_ref, wo_ref, g_ref, wg_ref, wu_ref, wd_ref, yp_ref, ys_ref,
                wg_s, wu_s, wd_s, h0_s, acc_s, *, n_cast, n_prompt):
    s = pl.program_id(0)

    def gated(h, wg, wu):
        gate = _dot(h, wg)
        return (gate * jax.nn.sigmoid(gate) * _dot(h, wu)).astype(BF16)

    @pl.when(s == 0)
    def _():
        x = xp_ref[...]
        h0_s[...] = _rms(x, g_ref[...]).astype(BF16)
        acc_s[...] = x

    for c in range(n_cast):
        @pl.when(s == c)
        def _(c=c):
            tile = slice(c * FF_TILE, (c + 1) * FF_TILE)
            wg_t, wu_t, wd_t = (r[...].astype(BF16) for r in (wg_ref, wu_ref, wd_ref))
            wg_s[:, tile] = wg_t
            wu_s[:, tile] = wu_t
            wd_s[tile, :] = wd_t
            acc_s[...] += _dot(gated(h0_s[...], wg_t, wu_t), wd_t)

    @pl.when(s == n_cast - 1)
    def _():
        yp_ref[...] = acc_s[...]

    def swiglu(x):
        h = _rms(x, g_ref[...]).astype(BF16)
        return x + _dot(gated(h, wg_s[...], wu_s[...]), wd_s[...])

    @pl.when(jnp.logical_and(s >= n_cast, s < n_cast + n_prompt - 1))
    def _():
        yp_ref[...] = swiglu(xp_ref[...])

    @pl.when(s == n_cast + n_prompt - 1)
    def _():
        ys_ref[...] = swiglu(xs_ref[...] + _dot(ms_ref[...].astype(BF16), wo_ref[...]))


def _ffn(xp, xs, mix_s, w_o, g, w_gate, w_up, w_down, layer):
    rows, ns = xp.shape[0], xs.shape[0]
    n_cast, n_prompt = D_FF // FF_TILE, rows // TM
    w_tile = lambda s: jnp.minimum(s, n_cast - 1)
    row_blk = lambda s: (jnp.clip(s - (n_cast - 1), 0, n_prompt - 1), 0)
    return pl.pallas_call(
        functools.partial(_ffn_kernel, n_cast=n_cast, n_prompt=n_prompt),
        grid=(n_cast + n_prompt,),
        in_specs=[
            pl.BlockSpec((TM, D_MODEL), row_blk),
            _full((ns, D_MODEL)),
            _full((ns, D_MODEL)),
            _resident((D_MODEL, D_MODEL)),
            _full((1, D_MODEL)),
            pl.BlockSpec((None, D_MODEL, FF_TILE), lambda s: (layer, 0, w_tile(s))),
            pl.BlockSpec((None, D_MODEL, FF_TILE), lambda s: (layer, 0, w_tile(s))),
            pl.BlockSpec((None, FF_TILE, D_MODEL), lambda s: (layer, w_tile(s), 0)),
        ],
        out_specs=[pl.BlockSpec((TM, D_MODEL), row_blk), _full((ns, D_MODEL))],
        out_shape=[jax.ShapeDtypeStruct((rows, D_MODEL), F32), jax.ShapeDtypeStruct((ns, D_MODEL), F32)],
        scratch_shapes=[
            pltpu.VMEM((D_MODEL, D_FF), BF16),
            pltpu.VMEM((D_MODEL, D_FF), BF16),
            pltpu.VMEM((D_FF, D_MODEL), BF16),
            pltpu.VMEM((TM, D_MODEL), BF16),
            pltpu.VMEM((TM, D_MODEL), F32),
        ],
        compiler_params=pltpu.CompilerParams(
            dimension_semantics=("arbitrary",), vmem_limit_bytes=VMEM_LIMIT),
        name="ffn",
    )(xp, xs, mix_s, w_o, g, w_gate, w_up, w_down)


def _lower_bound(clb):
    m = jnp.max(clb, axis=0, keepdims=True)
    e = jnp.exp(clb - m)
    sm = e / jnp.sum(e, axis=0, keepdims=True)
    return (sm[0:1] + sm[1:2]) - sm[0:1]


def _split3(x):
    hi = x.astype(BF16)
    r = x - hi.astype(F32)
    mid = r.astype(BF16)
    lo = (r - mid.astype(F32)).astype(BF16)
    return hi, mid, lo


def _neg_abs(x):
    return lax.bitcast_convert_type(
        lax.bitcast_convert_type(x, jnp.uint32) | jnp.uint32(0x80000000), F32)


def _pair_level_table():
    t = np.arange(CHUNK)[:, None]
    s = np.arange(CHUNK)[None, :]
    lev = np.floor(np.log2(np.maximum(t ^ s, 1))).astype(np.int32)
    lev = np.where(t == s, -1, lev)
    return np.where(s > t, -2, lev).astype(np.int32)


def _level_operand(p, q, kk, f, b2):
    m = 2 ** p
    if m < VREG_ROWS:
        shape3 = (CHUNK // VREG_ROWS, VREG_ROWS, q.shape[1])
        sub = lax.broadcasted_iota(jnp.int32, (1, VREG_ROWS, q.shape[1]), 1)
        upper = ((sub >> p) & 1) == 1
        q3, k3 = q.reshape(shape3), kk.reshape(shape3)
        if p == 0:
            y = jnp.where(upper, q3 * f.reshape(shape3), k3)
        else:
            b3 = b2.reshape(shape3)
            be = b3[:, m - 1:m, :]
            for k in range(1, VREG_ROWS // (2 * m)):
                be = jnp.where(sub >= 2 * m * k, b3[:, 2 * m * k + m - 1:2 * m * k + m, :], be)
            y = jnp.where(upper, q3, k3) * jnp.exp2(_neg_abs(b3 - be))
        return y.reshape(q.shape).astype(BF16)
    parts = []
    for k in range(CHUNK // (2 * m)):
        lo = slice(2 * m * k, 2 * m * k + m)
        up = slice(2 * m * k + m, 2 * m * (k + 1))
        be = b2[2 * m * k + m - 1:2 * m * k + m, :]
        parts.append(kk[lo] * jnp.exp2(be - b2[lo]))
        parts.append(q[up] * jnp.exp2(b2[up] - be))
    return jnp.concatenate(parts, axis=0).astype(BF16)


def _merge_level(p, att, pm, lev):
    m = 2 ** p
    if m < VREG_ROWS:
        return jnp.where(lev == p, pm, att)
    col = lax.broadcasted_iota(jnp.int32, (1, CHUNK), 1)
    parts = []
    for k in range(CHUNK // (2 * m)):
        lo = slice(2 * m * k, 2 * m * k + m)
        up = slice(2 * m * k + m, 2 * m * (k + 1))
        parts.append(att[lo])
        parts.append(jnp.where((col >= 2 * m * k) & (col < 2 * m * k + m), pm[up], att[up]))
    return jnp.concatenate(parts, axis=0)


def _hgrn_prompt_kernel(x_ref, nm_ref, win_ref, clb_ref, on_ref, wout_ref, lev_ref,
                        ss_ref, fs_ref, qs_ref, is_ref, gs_ref,
                        y_ref, st_ref, sso_ref, os_ref,
                        z_ref, o_ref, stt_ref, k_ref):
    j = pl.program_id(1)
    last_j = pl.num_programs(1) - 1
    n_chunks = TQH // CHUNK
    n_levels = int(math.log2(CHUNK))

    @pl.when(j == 0)
    def _():
        stt_ref[...] = jnp.zeros_like(stt_ref)

    h = _rms(x_ref[0], nm_ref[...]).astype(BF16)
    z_ref[...] = _dot(h, win_ref[...])

    head_cols = lambda ref: [ref[:, hd * C_KEY_DIM:(hd + 1) * C_KEY_DIM].T for hd in range(C_HEADS)]
    f_cols, q_cols = head_cols(fs_ref), head_cols(qs_ref)
    out_rows = []
    for smp in range(ss_ref.shape[0]):
        parts = []
        for hd in range(C_HEADS):
            hs = slice(hd * C_VAL_DIM, (hd + 1) * C_VAL_DIM)
            fb = jnp.broadcast_to(f_cols[hd][:, smp:smp + 1], (C_KEY_DIM, C_VAL_DIM))
            sn = fb * ss_ref[smp, hd] + (1.0 - fb) * is_ref[smp:smp + 1, hs]
            sso_ref[smp, hd] = sn
            o = jnp.sum(q_cols[hd][:, smp:smp + 1] * sn, axis=0, keepdims=True)
            parts.append(_rms(o, on_ref[...]))
        out_rows.append(jnp.concatenate(parts, axis=-1))
    os_ref[...] = jnp.concatenate(out_rows, axis=0) * gs_ref[...]

    lb = _lower_bound(clb_ref[...])

    row = lax.broadcasted_iota(jnp.int32, (CHUNK, CHUNK), 0)
    col = lax.broadcasted_iota(jnp.int32, (CHUNK, CHUNK), 1)
    ltri = (row >= col).astype(BF16)

    def chunk_rows(c):
        return pl.ds(pl.multiple_of(c * CHUNK, CHUNK), CHUNK)

    def prefix(g, worst):
        rows = [chunk_rows(g * PREFIX_GROUP + i) for i in range(PREFIX_GROUP)]
        gates = [z_ref[r, C_F:2 * C_F] for r in rows]
        for r, gate in zip(rows, gates):
            f_all = lb + (1.0 - lb) * jax.nn.sigmoid(gate)
            k_ref[r, :] = 1.0 - f_all
            hi, mid, lo = _split3(jnp.log2(f_all))
            b2 = (_dot(ltri, hi) + _dot(ltri, mid)) + _dot(ltri, lo)
            z_ref[r, C_F:2 * C_F] = b2
            b_mid = b2[CHUNK // 2 - 1:CHUNK // 2, :]
            b_last = b2[CHUNK - 1:CHUNK, :]
            worst = jnp.maximum(worst, jnp.maximum(-b_mid, b_mid - b_last))
        return worst

    worst = lax.fori_loop(0, n_chunks // PREFIX_GROUP, prefix, jnp.zeros((1, C_F), F32))
    bounded = jnp.max(worst) <= SAFE_LOG2_RANGE

    def finish_head(rows, hd, o):
        gt = z_ref[rows, 2 * C_F + C_V + hd * C_VAL_DIM:2 * C_F + C_V + (hd + 1) * C_VAL_DIM]
        o = _rms(o, on_ref[...]) * jax.nn.sigmoid(gt)
        o_ref[rows, hd * C_VAL_DIM:(hd + 1) * C_VAL_DIM] = o.astype(BF16)

    def head_inputs(rows, hd):
        q = z_ref[rows, hd * C_KEY_DIM:(hd + 1) * C_KEY_DIM]
        kk = k_ref[rows, hd * C_KEY_DIM:(hd + 1) * C_KEY_DIM]
        b2 = z_ref[rows, C_F + hd * C_KEY_DIM:C_F + (hd + 1) * C_KEY_DIM]
        ivb = z_ref[rows, 2 * C_F + hd * C_VAL_DIM:2 * C_F + (hd + 1) * C_VAL_DIM].astype(BF16)
        return q, kk, b2, ivb

    def factored_chunk(c, carry):
        rows = chunk_rows(c)
        for hd in range(C_HEADS):
            q, kk, b2, ivb = head_inputs(rows, hd)
            b_mid = b2[CHUNK // 2 - 1:CHUNK // 2, :]
            b_last = b2[CHUNK - 1:CHUNK, :]
            qs = (q * jnp.exp2(b2 - b_mid)).astype(BF16)
            kd = (kk * jnp.exp2(b_mid - b2)).astype(BF16)
            att = jnp.where(row >= col, lax.dot_general(qs, kd, _NT, preferred_element_type=F32), 0.0)
            stt = stt_ref[hd]
            o = lax.dot_general(qs, (stt * jnp.exp2(b_mid)).astype(BF16), _NT,
                                preferred_element_type=F32) + _dot(att.astype(BF16), ivb)
            stt_ref[hd] = stt * jnp.exp2(b_last) + jnp.exp2(b_last - b_mid) * lax.dot_general(
                ivb, kd, _TN, preferred_element_type=F32)
            finish_head(rows, hd, o)
        return carry

    def tree_chunk(c, carry):
        rows = chunk_rows(c)
        lev = lev_ref[...]

        def products(hd):
            q, kk, b2, ivb = head_inputs(rows, hd)
            diag = jnp.sum(q * kk, axis=-1, keepdims=True)
            pms = []
            for p in range(n_levels):
                y = _level_operand(p, q, kk, 1.0 - kk, b2)
                pms.append(lax.dot_general(y, y, _NT, preferred_element_type=F32))
            stt = stt_ref[hd]
            o_prev = lax.dot_general((q * jnp.exp2(b2)).astype(BF16), stt.astype(BF16), _NT,
                                     preferred_element_type=F32)
            b_last = b2[CHUNK - 1:CHUNK, :]
            kd = (kk * jnp.exp2(b_last - b2)).astype(BF16)
            stt_ref[hd] = stt * jnp.exp2(b_last) + lax.dot_general(
                ivb, kd, _TN, preferred_element_type=F32)
            return diag, pms, o_prev, ivb

        def finish(hd, diag, pms, o_prev, ivb):
            att = jnp.where(lev == -1, diag, 0.0)
            for p in range(n_levels):
                att = _merge_level(p, att, pms[p], lev)
            finish_head(rows, hd, o_prev + _dot(att.astype(BF16), ivb))

        pending = [products(hd) for hd in range(HEAD_SKEW)]
        for hd in range(C_HEADS):
            if hd + HEAD_SKEW < C_HEADS:
                pending.append(products(hd + HEAD_SKEW))
            finish(hd, *pending.pop(0))
        return carry

    @pl.when(bounded)
    def _():
        lax.fori_loop(0, n_chunks, factored_chunk, 0, unroll=FACTORED_UNROLL)

    @pl.when(jnp.logical_not(bounded))
    def _():
        lax.fori_loop(0, n_chunks, tree_chunk, 0)

    y_ref[0] = x_ref[0] + _dot(o_ref[...], wout_ref[...])

    @pl.when(j == last_j)
    def _():
        for hd in range(C_HEADS):
            st_ref[0, hd] = stt_ref[hd].T


def _hgrn_prompt(x, nm, w_in, clb, on, w_out, state_s, f_s, q_s, i_s, g_s):
    nb, seq, _ = x.shape
    nj = seq // TQH
    ns = state_s.shape[0]
    sbh = ns // (nb * nj)
    assert sbh * nb * nj == ns
    blk = lambda b, j: (b, j, 0)
    step = lambda b, j: (b * nj + j, 0, 0)
    step4 = lambda b, j: (b * nj + j, 0, 0, 0)
    rows3 = lambda a: a.reshape(ns // sbh, sbh, a.shape[1])
    y, st, st_s, o_s = pl.pallas_call(
        _hgrn_prompt_kernel,
        grid=(nb, nj),
        in_specs=[
            pl.BlockSpec((1, TQH, D_MODEL), blk),
            _full((1, D_MODEL)),
            _resident((D_MODEL, C_IN)),
            _full((DEPTH, C_F)),
            _full((1, C_VAL_DIM)),
            _resident((C_V, D_MODEL)),
            _full((CHUNK, CHUNK)),
            pl.BlockSpec((sbh, C_HEADS, C_KEY_DIM, C_VAL_DIM), step4),
            pl.BlockSpec((None, sbh, C_F), step),
            pl.BlockSpec((None, sbh, C_F), step),
            pl.BlockSpec((None, sbh, C_V), step),
            pl.BlockSpec((None, sbh, C_V), step),
        ],
        out_specs=[
            pl.BlockSpec((1, TQH, D_MODEL), blk),
            pl.BlockSpec((1, C_HEADS, C_KEY_DIM, C_VAL_DIM), lambda b, j: (b, 0, 0, 0)),
            pl.BlockSpec((sbh, C_HEADS, C_KEY_DIM, C_VAL_DIM), step4),
            pl.BlockSpec((None, sbh, C_V), step),
        ],
        out_shape=[
            jax.ShapeDtypeStruct((nb, seq, D_MODEL), F32),
            jax.ShapeDtypeStruct((nb, C_HEADS, C_KEY_DIM, C_VAL_DIM), F32),
            jax.ShapeDtypeStruct(state_s.shape, F32),
            jax.ShapeDtypeStruct((ns // sbh, sbh, C_V), F32),
        ],
        scratch_shapes=[
            pltpu.VMEM((TQH, C_IN), F32),
            pltpu.VMEM((TQH, C_V), BF16),
            pltpu.VMEM((C_HEADS, C_VAL_DIM, C_KEY_DIM), F32),
            pltpu.VMEM((TQH, C_F), F32),
        ],
        compiler_params=pltpu.CompilerParams(
            dimension_semantics=("arbitrary", "arbitrary"), vmem_limit_bytes=VMEM_LIMIT),
        name="hgrn_prompt",
    )(x, nm, w_in, clb, on, w_out, jnp.asarray(_pair_level_table()),
      state_s, rows3(f_s), rows3(q_s), rows3(i_s), rows3(g_s))
    return y, st, st_s, o_s.reshape(ns, C_V)


def _ab_sample_proj_kernel(w00_ref, b0_ref, x_ref, nm_ref, win_ref, qn_ref, kn_ref, lng_ref, lnb_ref,
                           qx_ref, knew_ref, vnew_ref, bm_ref, gv_ref, wbf_ref):
    n = x_ref.shape[0]
    wbf_ref[...] = win_ref[...].astype(BF16)
    h = _rms(x_ref[...], nm_ref[...]).astype(BF16)
    z = _dot(h, wbf_ref[...])
    zeros = jnp.zeros((n, A_HEAD_DIM), F32)
    for hh in range(A_HEADS):
        qh = _rms(z[:, hh * A_HEAD_DIM:(hh + 1) * A_HEAD_DIM], qn_ref[...]) * ATTN_SCALE
        qx_ref[hh] = jnp.concatenate([qh, zeros] if hh // A_GROUP == 0 else [zeros, qh], axis=-1)
    kparts = []
    for g in range(A_KV_HEADS):
        kparts.append(_rms(z[:, A_Q + g * A_HEAD_DIM:A_Q + (g + 1) * A_HEAD_DIM], kn_ref[...]))
    knew_ref[...] = jnp.concatenate(kparts, axis=-1)
    vnew_ref[...] = z[:, A_Q + A_KV:A_Q + 2 * A_KV]

    u = _gelu(z[:, A_Q + 2 * A_KV:A_Q + 2 * A_KV + B_WIDTH])
    vln = _layernorm(_gelu(z[:, A_Q + 2 * A_KV + B_WIDTH:AB_IN]), lng_ref[...], lnb_ref[...])
    grp = lax.broadcasted_iota(jnp.int32, (1, B_WIDTH), 1) // B_GROUP_DIM
    srow = jnp.zeros((1, B_WIDTH), F32)
    brow = jnp.zeros((1, B_WIDTH), F32)
    for g in range(B_GROUPS):
        srow = jnp.where(grp == g, w00_ref[g], srow)
        brow = jnp.where(grp == g, b0_ref[g], brow)
    bm_ref[...] = u * (vln * srow + brow)
    gv_ref[...] = vln


def _ab_sample_proj(x, nm, w_in, qn, kn, lng, lnb, w00, b0):
    n = x.shape[0]
    return pl.pallas_call(
        _ab_sample_proj_kernel,
        in_specs=[_SMEM, _SMEM] + [pl.BlockSpec(memory_space=pltpu.VMEM)] * 7,
        out_shape=[
            jax.ShapeDtypeStruct((A_HEADS, n, A_KV), F32),
            jax.ShapeDtypeStruct((n, A_KV), F32),
            jax.ShapeDtypeStruct((n, A_KV), F32),
            jax.ShapeDtypeStruct((n, B_WIDTH), F32),
            jax.ShapeDtypeStruct((n, B_WIDTH), F32),
            jax.ShapeDtypeStruct(w_in.shape, BF16),
        ],
        compiler_params=pltpu.CompilerParams(vmem_limit_bytes=VMEM_LIMIT),
        name="ab_sample_proj",
    )(w00, b0, x, nm, w_in, qn, kn, lng, lnb)


def _hgrn_sample_proj_kernel(x_ref, nm_ref, win_ref, clb_ref, q_ref, f_ref, i_ref, sg_ref, wbf_ref):
    wbf_ref[...] = win_ref[...].astype(BF16)
    h = _rms(x_ref[...], nm_ref[...]).astype(BF16)
    z = _dot(h, wbf_ref[...])
    lb = _lower_bound(clb_ref[...])
    q_ref[...] = z[:, 0:C_F]
    f_ref[...] = lb + (1.0 - lb) * jax.nn.sigmoid(z[:, C_F:2 * C_F])
    i_ref[...] = z[:, 2 * C_F:2 * C_F + C_V]
    sg_ref[...] = jax.nn.sigmoid(z[:, 2 * C_F + C_V:C_IN])


def _hgrn_sample_proj(x, nm, w_in, clb):
    n = x.shape[0]
    return pl.pallas_call(
        _hgrn_sample_proj_kernel,
        out_shape=[jax.ShapeDtypeStruct((n, C_F), F32)] * 2 + [jax.ShapeDtypeStruct((n, C_V), F32)] * 2
        + [jax.ShapeDtypeStruct(w_in.shape, BF16)],
        compiler_params=pltpu.CompilerParams(vmem_limit_bytes=VMEM_LIMIT),
        name="hgrn_sample_proj",
    )(x, nm, w_in, clb)


def kernel(x_prompt, x_sample, cache_k, cache_v, state_hgrn, norm_mix, norm_ffn, w_in_ab, w_out_ab,
           q_norm, k_norm, attn_sink, rel_bias, gmlp_ln_g, gmlp_ln_b, gmlp_w_s, gmlp_b_s,
           w_in_c, c_lower_bounds, c_out_norm, w_out_c, w_gate, w_up, w_down):
    assert norm_mix.shape[0] == DEPTH == 2 and w_in_ab.shape[0] == 1 and w_in_c.shape[0] == 1
    nb, seq, _ = x_prompt.shape
    ns = x_sample.shape[0]
    assert x_sample.shape[1] == 1 and cache_k.shape[2] == WINDOW

    row = lambda v: v.reshape(1, -1)
    bf = lambda w: w.astype(BF16)
    w_out_ab0, w_out_c0 = bf(w_out_ab[0]), bf(w_out_c[0])
    nm, nf = norm_mix, norm_ffn
    qn, kn = row(q_norm[0]), row(k_norm[0])
    lng, lnb = row(gmlp_ln_g[0]), row(gmlp_ln_b[0])
    sink = attn_sink[0]

    tab, tabp = _bias_table(rel_bias)

    xs = x_sample.reshape(ns, D_MODEL)
    qx, knew_s, vnew_s, bm_s, gv_s, w_in_ab0 = _ab_sample_proj(
        xs, row(nm[0]), w_in_ab[0], qn, kn, lng, lnb, gmlp_w_s[0, :, 0, 0], gmlp_b_s[0, :, 0])
    sb = jnp.pad(tab[:, WINDOW - 1, WINDOW - 1:], ((0, 0), (0, WINDOW - 1)))
    to_t = lambda c: c.transpose(0, 2, 3, 1).reshape(ns, A_KV, WINDOW)
    xp, knew_p, vnew_p, gv_p, nk_s, nv_s, om = _ab_prompt(
        x_prompt, row(nm[0]), w_in_ab0, jnp.tile(qn, (1, A_HEADS)), jnp.tile(kn, (1, A_KV_HEADS)),
        sink, tabp, lng, lnb, gmlp_w_s[0], jnp.repeat(gmlp_b_s[0].T, B_GROUP_DIM, axis=1), w_out_ab0,
        to_t(cache_k[0]), to_t(cache_v[0]), qx.transpose(1, 0, 2), knew_s, vnew_s, sb)
    om = om.reshape(ns, A_KV_HEADS, A_GROUP, A_KV_HEADS, A_HEAD_DIM)
    a_s = jnp.stack([om[:, g, :, g, :] for g in range(A_KV_HEADS)], axis=1).reshape(ns, A_Q)
    xp, xs = _ffn(xp.reshape(nb * seq, D_MODEL), xs, jnp.concatenate([a_s, bm_s], axis=-1), w_out_ab0,
                  row(nf[0]), w_gate, w_up, w_down, 0)

    q_s, f_s, i_s, sg_s, w_in_c0 = _hgrn_sample_proj(xs, row(nm[1]), w_in_c[0], c_lower_bounds)
    xp, st_p, st_s, o_s = _hgrn_prompt(xp.reshape(nb, seq, D_MODEL), row(nm[1]), w_in_c0, c_lower_bounds,
                                       row(c_out_norm[0]), w_out_c0, state_hgrn[0], f_s, q_s, i_s, sg_s)
    xp, xs = _ffn(xp.reshape(nb * seq, D_MODEL), xs, o_s, w_out_c0, row(nf[1]), w_gate, w_up, w_down, 1)

    kv5 = lambda a: a.reshape(1, a.shape[0], WINDOW, A_KV_HEADS, A_HEAD_DIM)
    from_t = lambda a: a.reshape(ns, A_KV_HEADS, A_HEAD_DIM, WINDOW).transpose(0, 3, 1, 2)[None]
    return (xp.reshape(nb, seq, D_MODEL), xs.reshape(ns, 1, D_MODEL),
            kv5(knew_p), kv5(vnew_p), from_t(nk_s), from_t(nv_s),
            gv_p[None], gv_s.reshape(1, ns, 1, B_WIDTH),
            st_p[None], st_s[None])
```

```python
import functools
import math

import jax
import jax.numpy as jnp
import numpy as np
from jax import lax
from jax.experimental import pallas as pl
from jax.experimental.pallas import tpu as pltpu

F32 = jnp.float32
BF16 = jnp.bfloat16

D_MODEL = 1024
DEPTH = 2
A_HEADS = 8
A_KV_HEADS = 2
A_GROUP = A_HEADS // A_KV_HEADS
A_HEAD_DIM = 64
WINDOW = 128
ATTN_SCALE = A_HEAD_DIM ** -0.5
NUM_BUCKETS = 32
MAX_DISTANCE = 128
A_Q = A_HEADS * A_HEAD_DIM
A_KV = A_KV_HEADS * A_HEAD_DIM
B_GROUPS = 8
B_GROUP_DIM = 64
B_WIDTH = B_GROUPS * B_GROUP_DIM
B_CHUNK = 128
AB_IN = A_Q + 2 * A_KV + 2 * B_WIDTH
AB_MIX = A_Q + B_WIDTH
C_HEADS = 8
C_KEY_DIM = 128
C_VAL_DIM = 128
C_F = C_HEADS * C_KEY_DIM
C_V = C_HEADS * C_VAL_DIM
C_IN = 2 * C_F + 2 * C_V
D_FF = 2816
EPS = 1e-6

NEG = -1e30

VMEM_LIMIT = 56 * 1024 * 1024
VREG_ROWS = 8

CHUNK = 128
TQ = 1024
TQH = 512
TM = 512
HEAD_SKEW = 2
SAFE_LOG2_RANGE = 64.0
PREFIX_GROUP = 4
FACTORED_UNROLL = 4
AB_UNROLL = 2

_NT = (((1,), (1,)), ((), ()))
_TN = (((0,), (0,)), ((), ()))


def _rms(x, g):
    return x * lax.rsqrt(jnp.mean(x * x, axis=-1, keepdims=True) + EPS) * g


def _gelu(x):
    return 0.5 * x * (1.0 + lax.erf(x * math.sqrt(0.5)))


def _layernorm(x, g, b):
    xc = x - jnp.mean(x, axis=-1, keepdims=True)
    return xc * lax.rsqrt(jnp.mean(xc * xc, axis=-1, keepdims=True) + EPS) * g + b


def _dot(a, b):
    return jnp.dot(a, b, preferred_element_type=F32)


def _full(shape):
    n = len(shape)
    return pl.BlockSpec(shape, lambda *_: (0,) * n)


def _resident(shape):
    n = len(shape)
    return pl.BlockSpec(shape, lambda *_: (0,) * n, pipeline_mode=pl.Buffered(1))


_SMEM = pl.BlockSpec(memory_space=pltpu.SMEM)


def _bias_table_kernel(rel_ref, tab_ref, tabp_ref):
    qi = lax.broadcasted_iota(jnp.int32, (WINDOW, 2 * WINDOW), 0)
    kj = lax.broadcasted_iota(jnp.int32, (WINDOW, 2 * WINDOW), 1)
    dist = qi + WINDOW - kj
    ok = (dist >= 0) & (dist < WINDOW)
    max_exact = NUM_BUCKETS // 2
    d = jnp.maximum(dist, 0)
    dl = jnp.maximum(d, 1).astype(F32)
    v = (jnp.log(dl / max_exact) / math.log(MAX_DISTANCE / max_exact) * (NUM_BUCKETS - max_exact))
    far = d >= max_exact
    hits = []
    for b in range(NUM_BUCKETS):
        if b < max_exact:
            hits.append(d == b)
        elif b < NUM_BUCKETS - 1:
            hits.append(far & (v >= b - max_exact) & (v < b - max_exact + 1))
        else:
            hits.append(far & (v >= b - max_exact))
    for h in range(A_HEADS):
        acc = jnp.zeros((WINDOW, 2 * WINDOW), F32)
        for b in range(NUM_BUCKETS):
            acc = jnp.where(hits[b], rel_ref[b, h], acc)
        t = jnp.where(ok, acc, NEG)
        tab_ref[h] = t
        cols = slice((h % 2) * 2 * WINDOW, (h % 2 + 1) * 2 * WINDOW)
        tabp_ref[0, h // 2, :, cols] = t
        tabp_ref[1, h // 2, :, cols] = jnp.where(kj < WINDOW, NEG, t)


def _bias_table(rel_bias):
    return pl.pallas_call(
        _bias_table_kernel,
        out_shape=[
            jax.ShapeDtypeStruct((A_HEADS, WINDOW, 2 * WINDOW), F32),
            jax.ShapeDtypeStruct((2, A_HEADS // 2, WINDOW, 4 * WINDOW), F32),
        ],
        in_specs=[_SMEM],
        name="bias_table",
    )(rel_bias)


PAIR = 2 * A_HEAD_DIM
N_PAIRS = A_HEADS // 2


def _sample_cache_attention(ck_ref, cv_ref, qx_ref, kn_ref, vn_ref, sb_ref, sink_ref,
                            nk_ref, nv_ref, om_ref):
    wb = ck_ref.shape[2]
    head = lax.broadcasted_iota(jnp.int32, (1, A_HEADS, A_KV), 1)
    lane = lax.broadcasted_iota(jnp.int32, (1, A_HEADS, A_KV), 2)
    own_group = (head // A_GROUP) == (lane // A_HEAD_DIM)
    newest = lax.broadcasted_iota(jnp.int32, (1, 1, wb), 2) == wb - 1
    sink = sink_ref[...][None]
    kc, vc = ck_ref[...], cv_ref[...]
    kn, vn = kn_ref[...], vn_ref[...]
    kn_cols, vn_cols = kn_ref[:, 0, :].T, vn_ref[:, 0, :].T
    for i in range(kc.shape[0]):
        nk_ref[i] = jnp.where(newest[0], kn_cols[:, i:i + 1], pltpu.roll(kc[i], wb - 1, 1))
        nv_ref[i] = jnp.where(newest[0], vn_cols[:, i:i + 1], pltpu.roll(vc[i], wb - 1, 1))
    q = qx_ref[...]
    s = jnp.einsum('bhd,bdk->bhk', q.astype(BF16), kc.astype(BF16), preferred_element_type=F32)
    s = s + sb_ref[:, 0:wb][None]
    sn = jnp.sum(q * kn, axis=-1, keepdims=True) + sb_ref[:, wb:wb + 1][None]
    m = jnp.maximum(jnp.maximum(jnp.max(s, axis=-1, keepdims=True), sn), sink)
    e = jnp.exp(s - m)
    en = jnp.exp(sn - m)
    r = 1.0 / (jnp.sum(e, axis=-1, keepdims=True) + en + jnp.exp(sink - m))
    o = jnp.einsum('bhk,bdk->bhd', (e * r).astype(BF16), vc.astype(BF16),
                   preferred_element_type=F32) + (en * r) * vn
    om_ref[...] = jnp.where(own_group, o, 0.0)


def _ab_prompt_kernel(sink_ref, x_ref, nm_ref, win_ref, qg_ref, kg_ref, tabp_ref, lng_ref, lnb_ref,
                      ws_ref, bsp_ref, wout_ref,
                      ck_ref, cv_ref, qx_ref, kn_ref, vn_ref, sb_ref, sinkc_ref,
                      y_ref, knew_ref, vnew_ref, gv_ref, nk_ref, nv_ref, om_ref,
                      z_ref, mix_ref, q_ref, k_ref, kr_ref, v_ref, vr_ref, wpair_ref, kl_ref, vl_ref, gl_ref):
    j = pl.program_id(1)
    last_j = pl.num_programs(1) - 1
    n_chunks = TQ // CHUNK

    @pl.when(j == 0)
    def _():
        for ref in (k_ref, kr_ref, v_ref, vr_ref):
            ref[0:CHUNK, :] = jnp.zeros((CHUNK, A_KV), BF16)
        row = lax.broadcasted_iota(jnp.int32, (B_CHUNK, B_CHUNK), 0)
        col = lax.broadcasted_iota(jnp.int32, (B_CHUNK, B_CHUNK), 1)
        for g in range(B_GROUPS):
            wpair_ref[g // 2, :, (g % 2) * B_CHUNK:(g % 2 + 1) * B_CHUNK] = jnp.where(
                row >= col, ws_ref[g], 0.0).astype(BF16)

    h = _rms(x_ref[0], nm_ref[...]).astype(BF16)
    z_ref[...] = _dot(h, win_ref[...])

    _sample_cache_attention(ck_ref, cv_ref, qx_ref, kn_ref, vn_ref, sb_ref, sinkc_ref,
                            nk_ref, nv_ref, om_ref)

    lo_half = lax.broadcasted_iota(jnp.int32, (1, PAIR), 1) < A_HEAD_DIM

    def mean_sq_halves(x):
        x2 = x * x
        lo_sum = jnp.sum(jnp.where(lo_half, x2, 0.0), axis=-1, keepdims=True)
        hi_sum = jnp.sum(jnp.where(lo_half, 0.0, x2), axis=-1, keepdims=True)
        return jnp.where(lo_half, lo_sum, hi_sum) * (1.0 / A_HEAD_DIM)

    def block_diag(top, bot):
        zero = jnp.zeros_like(top)
        return jnp.concatenate([jnp.where(lo_half, top, zero), jnp.where(lo_half, zero, bot)], axis=0)

    kraw = z_ref[:, A_Q:A_Q + A_KV]
    v_all = z_ref[:, A_Q + A_KV:A_Q + 2 * A_KV]
    kn_all = kraw * lax.rsqrt(mean_sq_halves(kraw) + EPS) * kg_ref[...]
    k_ref[CHUNK:, :] = kn_all.astype(BF16)
    kr_ref[CHUNK:, :] = pltpu.roll(kn_all, A_HEAD_DIM, 1).astype(BF16)
    v_ref[CHUNK:, :] = v_all.astype(BF16)
    vr_ref[CHUNK:, :] = pltpu.roll(v_all, A_HEAD_DIM, 1).astype(BF16)
    kl_ref[...] = kn_all[TQ - CHUNK:, :]
    vl_ref[...] = v_all[TQ - CHUNK:, :]
    for i in range(N_PAIRS):
        ps = slice(i * PAIR, (i + 1) * PAIR)
        qraw = z_ref[:, ps]
        qn = qraw * lax.rsqrt(mean_sq_halves(qraw) + EPS) * (qg_ref[:, ps] * ATTN_SCALE)
        q_ref[:, ps] = qn.astype(BF16)

    def chunk(c, carry):
        r0 = pl.multiple_of(c * CHUNK, CHUNK)
        rows = pl.ds(r0, CHUNK)
        first = jnp.where(jnp.logical_and(j == 0, c == 0), 1, 0)

        both = pl.ds(r0, 2 * CHUNK)
        k2, k2r, v2, v2r = k_ref[both, :], kr_ref[both, :], v_ref[both, :], vr_ref[both, :]
        kbd = [block_diag(k2, k2r), block_diag(k2r, k2)]
        vbd = [block_diag(v2, v2r), block_diag(v2r, v2)]

        scores = []
        for i in range(N_PAIRS):
            s = lax.dot_general(q_ref[rows, i * PAIR:(i + 1) * PAIR], kbd[i // (A_GROUP // 2)], _NT,
                                preferred_element_type=F32)
            scores.append(s + tabp_ref[first, i])
        outs = []
        for i in range(N_PAIRS):
            es, rs = [], []
            for hh in range(2):
                sh = scores[i][:, hh * 2 * WINDOW:(hh + 1) * 2 * WINDOW]
                sk = sink_ref[2 * i + hh]
                m = jnp.maximum(jnp.max(sh, axis=-1, keepdims=True), sk)
                e = jnp.exp(sh - m)
                rs.append(1.0 / (jnp.sum(e, axis=-1, keepdims=True) + jnp.exp(sk - m)))
                es.append(e.astype(BF16))
            o = _dot(jnp.concatenate(es, axis=-1), vbd[i // (A_GROUP // 2)])
            outs.append(o * jnp.where(lo_half, rs[0], rs[1]))
        mix_ref[rows, 0:A_Q] = jnp.concatenate(outs, axis=-1).astype(BF16)

        zu = z_ref[rows, A_Q + 2 * A_KV:A_Q + 2 * A_KV + B_WIDTH]
        zv = z_ref[rows, A_Q + 2 * A_KV + B_WIDTH:AB_IN]
        u = _gelu(zu)
        vln = _layernorm(_gelu(zv), lng_ref[...], lnb_ref[...])
        vlb = vln.astype(BF16)
        sparts = []
        for i in range(B_GROUPS // 2):
            vpair = vlb[:, i * PAIR:(i + 1) * PAIR]
            sparts.append(_dot(wpair_ref[i], block_diag(vpair, vpair)))
        bm = u * (jnp.concatenate(sparts, axis=-1) + bsp_ref[...])
        mix_ref[rows, A_Q:AB_MIX] = bm.astype(BF16)

        gl_ref[...] = vln
        return carry

    lax.fori_loop(0, n_chunks, chunk, 0, unroll=AB_UNROLL)
    y_ref[0] = x_ref[0] + _dot(mix_ref[...], wout_ref[...])
    for ref in (k_ref, kr_ref, v_ref, vr_ref):
        ref[0:CHUNK, :] = ref[TQ:TQ + CHUNK, :]

    @pl.when(j == last_j)
    def _():
        knew_ref[0] = kl_ref[...]
        vnew_ref[0] = vl_ref[...]
        gv_ref[0] = gl_ref[...]


def _ab_prompt(x, nm, w_in, qg, kg, sink, tabp, lng, lnb, w_s, bsp, w_out, ck, cv, qx, kn_s, vn_s, sb):
    nb, seq, _ = x.shape
    nj = seq // TQ
    grid = (nb, nj)
    ns, _, wb = ck.shape
    sba = ns // (nb * nj)
    assert sba * nb * nj == ns
    blk = lambda b, j: (b, j, 0)
    per_b = lambda b, j: (b, 0, 0)
    step = lambda b, j: (b * nj + j, 0, 0)
    return pl.pallas_call(
        _ab_prompt_kernel,
        grid=grid,
        in_specs=[
            _SMEM,
            pl.BlockSpec((1, TQ, D_MODEL), blk),
            _full((1, D_MODEL)),
            _resident((D_MODEL, AB_IN)),
            _full((1, A_Q)),
            _full((1, A_KV)),
            _resident((2, N_PAIRS, WINDOW, 4 * WINDOW)),
            _full((1, B_WIDTH)),
            _full((1, B_WIDTH)),
            _resident((B_GROUPS, B_CHUNK, B_CHUNK)),
            _resident((B_CHUNK, B_WIDTH)),
            _resident((AB_MIX, D_MODEL)),
            pl.BlockSpec((sba, A_KV, wb), step),
            pl.BlockSpec((sba, A_KV, wb), step),
            pl.BlockSpec((sba, A_HEADS, A_KV), step),
            pl.BlockSpec((sba, 1, A_KV), step),
            pl.BlockSpec((sba, 1, A_KV), step),
            _full((A_HEADS, 2 * WINDOW)),
            _full((A_HEADS, 1)),
        ],
        out_specs=[
            pl.BlockSpec((1, TQ, D_MODEL), blk),
            pl.BlockSpec((1, WINDOW, A_KV), per_b),
            pl.BlockSpec((1, WINDOW, A_KV), per_b),
            pl.BlockSpec((1, B_CHUNK, B_WIDTH), per_b),
            pl.BlockSpec((sba, A_KV, wb), step),
            pl.BlockSpec((sba, A_KV, wb), step),
            pl.BlockSpec((sba, A_HEADS, A_KV), step),
        ],
        out_shape=[
            jax.ShapeDtypeStruct((nb, seq, D_MODEL), F32),
            jax.ShapeDtypeStruct((nb, WINDOW, A_KV), F32),
            jax.ShapeDtypeStruct((nb, WINDOW, A_KV), F32),
            jax.ShapeDtypeStruct((nb, B_CHUNK, B_WIDTH), F32),
            jax.ShapeDtypeStruct((ns, A_KV, wb), F32),
            jax.ShapeDtypeStruct((ns, A_KV, wb), F32),
            jax.ShapeDtypeStruct((ns, A_HEADS, A_KV), F32),
        ],
        scratch_shapes=[
            pltpu.VMEM((TQ, AB_IN), F32),
            pltpu.VMEM((TQ, AB_MIX), BF16),
            pltpu.VMEM((TQ, A_Q), BF16),
            pltpu.VMEM((CHUNK + TQ, A_KV), BF16),
            pltpu.VMEM((CHUNK + TQ, A_KV), BF16),
            pltpu.VMEM((CHUNK + TQ, A_KV), BF16),
            pltpu.VMEM((CHUNK + TQ, A_KV), BF16),
            pltpu.VMEM((B_GROUPS // 2, B_CHUNK, 2 * B_CHUNK), BF16),
            pltpu.VMEM((WINDOW, A_KV), F32),
            pltpu.VMEM((WINDOW, A_KV), F32),
            pltpu.VMEM((B_CHUNK, B_WIDTH), F32),
        ],
        compiler_params=pltpu.CompilerParams(
            dimension_semantics=("arbitrary", "arbitrary"), vmem_limit_bytes=VMEM_LIMIT),
        name="ab_prompt",
    )(sink, x, nm, w_in, qg, kg, tabp, lng, lnb, w_s, bsp, w_out,
      ck, cv, qx, kn_s[:, None, :], vn_s[:, None, :], sb, sink.reshape(A_HEADS, 1))


FF_TILE = 256


def _ffn_kernel(xp_ref, xs_ref, ms_ref, wo_ref, g_ref, wg_ref, wu_ref, wd_ref, yp_ref, ys_ref,
                wg_s, wu_s, wd_s, h0_s, acc_s, *, n_cast, n_prompt):
    s = pl.program_id(0)

    def gated(h, wg, wu):
        gate = _dot(h, wg)
        return (gate * jax.nn.sigmoid(gate) * _dot(h, wu)).astype(BF16)

    @pl.when(s == 0)
    def _():
        x = xp_ref[...]
        h0_s[...] = _rms(x, g_ref[...]).astype(BF16)
        acc_s[...] = x

    for c in range(n_cast):
        @pl.when(s == c)
        def _(c=c):
            tile = slice(c * FF_TILE, (c + 1) * FF_TILE)
            wg_t, wu_t, wd_t = (r[...].astype(BF16) for r in (wg_ref, wu_ref, wd_ref))
            wg_s[:, tile] = wg_t
            wu_s[:, tile] = wu_t
            wd_s[tile, :] = wd_t
            acc_s[...] += _dot(gated(h0_s[...], wg_t, wu_t), wd_t)

    @pl.when(s == n_cast - 1)
    def _():
        yp_ref[...] = acc_s[...]

    def swiglu(x):
        h = _rms(x, g_ref[...]).astype(BF16)
        return x + _dot(gated(h, wg_s[...], wu_s[...]), wd_s[...])

    @pl.when(jnp.logical_and(s >= n_cast, s < n_cast + n_prompt - 1))
    def _():
        yp_ref[...] = swiglu(xp_ref[...])

    @pl.when(s == n_cast + n_prompt - 1)
    def _():
        ys_ref[...] = swiglu(xs_ref[...] + _dot(ms_ref[...].astype(BF16), wo_ref[...]))


def _ffn(xp, xs, mix_s, w_o, g, w_gate, w_up, w_down, layer):
    rows, ns = xp.shape[0], xs.shape[0]
    n_cast, n_prompt = D_FF // FF_TILE, rows // TM
    w_tile = lambda s: jnp.minimum(s, n_cast - 1)
    row_blk = lambda s: (jnp.clip(s - (n_cast - 1), 0, n_prompt - 1), 0)
    return pl.pallas_call(
        functools.partial(_ffn_kernel, n_cast=n_cast, n_prompt=n_prompt),
        grid=(n_cast + n_prompt,),
        in_specs=[
            pl.BlockSpec((TM, D_MODEL), row_blk),
            _full((ns, D_MODEL)),
            _full((ns, D_MODEL)),
            _resident((D_MODEL, D_MODEL)),
            _full((1, D_MODEL)),
            pl.BlockSpec((None, D_MODEL, FF_TILE), lambda s: (layer, 0, w_tile(s))),
            pl.BlockSpec((None, D_MODEL, FF_TILE), lambda s: (layer, 0, w_tile(s))),
            pl.BlockSpec((None, FF_TILE, D_MODEL), lambda s: (layer, w_tile(s), 0)),
        ],
        out_specs=[pl.BlockSpec((TM, D_MODEL), row_blk), _full((ns, D_MODEL))],
        out_shape=[jax.ShapeDtypeStruct((rows, D_MODEL), F32), jax.ShapeDtypeStruct((ns, D_MODEL), F32)],
        scratch_shapes=[
            pltpu.VMEM((D_MODEL, D_FF), BF16),
            pltpu.VMEM((D_MODEL, D_FF), BF16),
            pltpu.VMEM((D_FF, D_MODEL), BF16),
            pltpu.VMEM((TM, D_MODEL), BF16),
            pltpu.VMEM((TM, D_MODEL), F32),
        ],
        compiler_params=pltpu.CompilerParams(
            dimension_semantics=("arbitrary",), vmem_limit_bytes=VMEM_LIMIT),
        name="ffn",
    )(xp, xs, mix_s, w_o, g, w_gate, w_up, w_down)


def _lower_bound(clb):
    m = jnp.max(clb, axis=0, keepdims=True)
    e = jnp.exp(clb - m)
    sm = e / jnp.sum(e, axis=0, keepdims=True)
    return (sm[0:1] + sm[1:2]) - sm[0:1]


def _split3(x):
    hi = x.astype(BF16)
    r = x - hi.astype(F32)
    mid = r.astype(BF16)
    lo = (r - mid.astype(F32)).astype(BF16)
    return hi, mid, lo


def _neg_abs(x):
    return lax.bitcast_convert_type(
        lax.bitcast_convert_type(x, jnp.uint32) | jnp.uint32(0x80000000), F32)


def _pair_level_table():
    t = np.arange(CHUNK)[:, None]
    s = np.arange(CHUNK)[None, :]
    lev = np.floor(np.log2(np.maximum(t ^ s, 1))).astype(np.int32)
    lev = np.where(t == s, -1, lev)
    return np.where(s > t, -2, lev).astype(np.int32)


def _level_operand(p, q, kk, f, b2):
    m = 2 ** p
    if m < VREG_ROWS:
        shape3 = (CHUNK // VREG_ROWS, VREG_ROWS, q.shape[1])
        sub = lax.broadcasted_iota(jnp.int32, (1, VREG_ROWS, q.shape[1]), 1)
        upper = ((sub >> p) & 1) == 1
        q3, k3 = q.reshape(shape3), kk.reshape(shape3)
        if p == 0:
            y = jnp.where(upper, q3 * f.reshape(shape3), k3)
        else:
            b3 = b2.reshape(shape3)
            be = b3[:, m - 1:m, :]
            for k in range(1, VREG_ROWS // (2 * m)):
                be = jnp.where(sub >= 2 * m * k, b3[:, 2 * m * k + m - 1:2 * m * k + m, :], be)
            y = jnp.where(upper, q3, k3) * jnp.exp2(_neg_abs(b3 - be))
        return y.reshape(q.shape).astype(BF16)
    parts = []
    for k in range(CHUNK // (2 * m)):
        lo = slice(2 * m * k, 2 * m * k + m)
        up = slice(2 * m * k + m, 2 * m * (k + 1))
        be = b2[2 * m * k + m - 1:2 * m * k + m, :]
        parts.append(kk[lo] * jnp.exp2(be - b2[lo]))
        parts.append(q[up] * jnp.exp2(b2[up] - be))
    return jnp.concatenate(parts, axis=0).astype(BF16)


def _merge_level(p, att, pm, lev):
    m = 2 ** p
    if m < VREG_ROWS:
        return jnp.where(lev == p, pm, att)
    col = lax.broadcasted_iota(jnp.int32, (1, CHUNK), 1)
    parts = []
    for k in range(CHUNK // (2 * m)):
        lo = slice(2 * m * k, 2 * m * k + m)
        up = slice(2 * m * k + m, 2 * m * (k + 1))
        parts.append(att[lo])
        parts.append(jnp.where((col >= 2 * m * k) & (col < 2 * m * k + m), pm[up], att[up]))
    return jnp.concatenate(parts, axis=0)


def _hgrn_prompt_kernel(x_ref, nm_ref, win_ref, clb_ref, on_ref, wout_ref, lev_ref,
                        ss_ref, fs_ref, qs_ref, is_ref, gs_ref,
                        y_ref, st_ref, sso_ref, os_ref,
                        z_ref, o_ref, stt_ref, k_ref):
    j = pl.program_id(1)
    last_j = pl.num_programs(1) - 1
    n_chunks = TQH // CHUNK
    n_levels = int(math.log2(CHUNK))

    @pl.when(j == 0)
    def _():
        stt_ref[...] = jnp.zeros_like(stt_ref)

    h = _rms(x_ref[0], nm_ref[...]).astype(BF16)
    z_ref[...] = _dot(h, win_ref[...])

    head_cols = lambda ref: [ref[:, hd * C_KEY_DIM:(hd + 1) * C_KEY_DIM].T for hd in range(C_HEADS)]
    f_cols, q_cols = head_cols(fs_ref), head_cols(qs_ref)
    out_rows = []
    for smp in range(ss_ref.shape[0]):
        parts = []
        for hd in range(C_HEADS):
            hs = slice(hd * C_VAL_DIM, (hd + 1) * C_VAL_DIM)
            fb = jnp.broadcast_to(f_cols[hd][:, smp:smp + 1], (C_KEY_DIM, C_VAL_DIM))
            sn = fb * ss_ref[smp, hd] + (1.0 - fb) * is_ref[smp:smp + 1, hs]
            sso_ref[smp, hd] = sn
            o = jnp.sum(q_cols[hd][:, smp:smp + 1] * sn, axis=0, keepdims=True)
            parts.append(_rms(o, on_ref[...]))
        out_rows.append(jnp.concatenate(parts, axis=-1))
    os_ref[...] = jnp.concatenate(out_rows, axis=0) * gs_ref[...]

    lb = _lower_bound(clb_ref[...])

    row = lax.broadcasted_iota(jnp.int32, (CHUNK, CHUNK), 0)
    col = lax.broadcasted_iota(jnp.int32, (CHUNK, CHUNK), 1)
    ltri = (row >= col).astype(BF16)

    def chunk_rows(c):
        return pl.ds(pl.multiple_of(c * CHUNK, CHUNK), CHUNK)

    def prefix(g, worst):
        rows = [chunk_rows(g * PREFIX_GROUP + i) for i in range(PREFIX_GROUP)]
        gates = [z_ref[r, C_F:2 * C_F] for r in rows]
        for r, gate in zip(rows, gates):
            f_all = lb + (1.0 - lb) * jax.nn.sigmoid(gate)
            k_ref[r, :] = 1.0 - f_all
            hi, mid, lo = _split3(jnp.log2(f_all))
            b2 = (_dot(ltri, hi) + _dot(ltri, mid)) + _dot(ltri, lo)
            z_ref[r, C_F:2 * C_F] = b2
            b_mid = b2[CHUNK // 2 - 1:CHUNK // 2, :]
            b_last = b2[CHUNK - 1:CHUNK, :]
            worst = jnp.maximum(worst, jnp.maximum(-b_mid, b_mid - b_last))
        return worst

    worst = lax.fori_loop(0, n_chunks // PREFIX_GROUP, prefix, jnp.zeros((1, C_F), F32))
    bounded = jnp.max(worst) <= SAFE_LOG2_RANGE

    def finish_head(rows, hd, o):
        gt = z_ref[rows, 2 * C_F + C_V + hd * C_VAL_DIM:2 * C_F + C_V + (hd + 1) * C_VAL_DIM]
        o = _rms(o, on_ref[...]) * jax.nn.sigmoid(gt)
        o_ref[rows, hd * C_VAL_DIM:(hd + 1) * C_VAL_DIM] = o.astype(BF16)

    def head_inputs(rows, hd):
        q = z_ref[rows, hd * C_KEY_DIM:(hd + 1) * C_KEY_DIM]
        kk = k_ref[rows, hd * C_KEY_DIM:(hd + 1) * C_KEY_DIM]
        b2 = z_ref[rows, C_F + hd * C_KEY_DIM:C_F + (hd + 1) * C_KEY_DIM]
        ivb = z_ref[rows, 2 * C_F + hd * C_VAL_DIM:2 * C_F + (hd + 1) * C_VAL_DIM].astype(BF16)
        return q, kk, b2, ivb

    def factored_chunk(c, carry):
        rows = chunk_rows(c)
        for hd in range(C_HEADS):
            q, kk, b2, ivb = head_inputs(rows, hd)
            b_mid = b2[CHUNK // 2 - 1:CHUNK // 2, :]
            b_last = b2[CHUNK - 1:CHUNK, :]
            qs = (q * jnp.exp2(b2 - b_mid)).astype(BF16)
            kd = (kk * jnp.exp2(b_mid - b2)).astype(BF16)
            att = jnp.where(row >= col, lax.dot_general(qs, kd, _NT, preferred_element_type=F32), 0.0)
            stt = stt_ref[hd]
            o = lax.dot_general(qs, (stt * jnp.exp2(b_mid)).astype(BF16), _NT,
                                preferred_element_type=F32) + _dot(att.astype(BF16), ivb)
            stt_ref[hd] = stt * jnp.exp2(b_last) + jnp.exp2(b_last - b_mid) * lax.dot_general(
                ivb, kd, _TN, preferred_element_type=F32)
            finish_head(rows, hd, o)
        return carry

    def tree_chunk(c, carry):
        rows = chunk_rows(c)
        lev = lev_ref[...]

        def products(hd):
            q, kk, b2, ivb = head_inputs(rows, hd)
            diag = jnp.sum(q * kk, axis=-1, keepdims=True)
            pms = []
            for p in range(n_levels):
                y = _level_operand(p, q, kk, 1.0 - kk, b2)
                pms.append(lax.dot_general(y, y, _NT, preferred_element_type=F32))
            stt = stt_ref[hd]
            o_prev = lax.dot_general((q * jnp.exp2(b2)).astype(BF16), stt.astype(BF16), _NT,
                                     preferred_element_type=F32)
            b_last = b2[CHUNK - 1:CHUNK, :]
            kd = (kk * jnp.exp2(b_last - b2)).astype(BF16)
            stt_ref[hd] = stt * jnp.exp2(b_last) + lax.dot_general(
                ivb, kd, _TN, preferred_element_type=F32)
            return diag, pms, o_prev, ivb

        def finish(hd, diag, pms, o_prev, ivb):
            att = jnp.where(lev == -1, diag, 0.0)
            for p in range(n_levels):
                att = _merge_level(p, att, pms[p], lev)
            finish_head(rows, hd, o_prev + _dot(att.astype(BF16), ivb))

        pending = [products(hd) for hd in range(HEAD_SKEW)]
        for hd in range(C_HEADS):
            if hd + HEAD_SKEW < C_HEADS:
                pending.append(products(hd + HEAD_SKEW))
            finish(hd, *pending.pop(0))
        return carry

    @pl.when(bounded)
    def _():
        lax.fori_loop(0, n_chunks, factored_chunk, 0, unroll=FACTORED_UNROLL)

    @pl.when(jnp.logical_not(bounded))
    def _():
        lax.fori_loop(0, n_chunks, tree_chunk, 0)

    y_ref[0] = x_ref[0] + _dot(o_ref[...], wout_ref[...])

    @pl.when(j == last_j)
    def _():
        for hd in range(C_HEADS):
            st_ref[0, hd] = stt_ref[hd].T


def _hgrn_prompt(x, nm, w_in, clb, on, w_out, state_s, f_s, q_s, i_s, g_s):
    nb, seq, _ = x.shape
    nj = seq // TQH
    ns = state_s.shape[0]
    sbh = f_s.shape[1]
    assert f_s.shape[0] == nb * nj and sbh * nb * nj == ns
    blk = lambda b, j: (b, j, 0)
    step = lambda b, j: (b * nj + j, 0, 0)
    step4 = lambda b, j: (b * nj + j, 0, 0, 0)
    y, st, st_s, o_s = pl.pallas_call(
        _hgrn_prompt_kernel,
        grid=(nb, nj),
        in_specs=[
            pl.BlockSpec((1, TQH, D_MODEL), blk),
            _full((1, D_MODEL)),
            _resident((D_MODEL, C_IN)),
            _full((DEPTH, C_F)),
            _full((1, C_VAL_DIM)),
            _resident((C_V, D_MODEL)),
            _full((CHUNK, CHUNK)),
            pl.BlockSpec((sbh, C_HEADS, C_KEY_DIM, C_VAL_DIM), step4),
            pl.BlockSpec((None, sbh, C_F), step),
            pl.BlockSpec((None, sbh, C_F), step),
            pl.BlockSpec((None, sbh, C_V), step),
            pl.BlockSpec((None, sbh, C_V), step),
        ],
        out_specs=[
            pl.BlockSpec((1, TQH, D_MODEL), blk),
            pl.BlockSpec((1, C_HEADS, C_KEY_DIM, C_VAL_DIM), lambda b, j: (b, 0, 0, 0)),
            pl.BlockSpec((sbh, C_HEADS, C_KEY_DIM, C_VAL_DIM), step4),
            pl.BlockSpec((None, sbh, C_V), step),
        ],
        out_shape=[
            jax.ShapeDtypeStruct((nb, seq, D_MODEL), F32),
            jax.ShapeDtypeStruct((nb, C_HEADS, C_KEY_DIM, C_VAL_DIM), F32),
            jax.ShapeDtypeStruct(state_s.shape, F32),
            jax.ShapeDtypeStruct((ns // sbh, sbh, C_V), F32),
        ],
        scratch_shapes=[
            pltpu.VMEM((TQH, C_IN), F32),
            pltpu.VMEM((TQH, C_V), BF16),
            pltpu.VMEM((C_HEADS, C_VAL_DIM, C_KEY_DIM), F32),
            pltpu.VMEM((TQH, C_F), F32),
        ],
        compiler_params=pltpu.CompilerParams(
            dimension_semantics=("arbitrary", "arbitrary"), vmem_limit_bytes=VMEM_LIMIT),
        name="hgrn_prompt",
    )(x, nm, w_in, clb, on, w_out, jnp.asarray(_pair_level_table()),
      state_s, f_s, q_s, i_s, g_s)
    return y, st, st_s, o_s.reshape(ns, C_V)


def _ab_sample_proj_kernel(w00_ref, b0_ref, x_ref, nm_ref, win_ref, qn_ref, kn_ref, lng_ref, lnb_ref,
                           qx_ref, knew_ref, vnew_ref, bm_ref, gv_ref, wbf_ref):
    n = x_ref.shape[0]
    wbf_ref[...] = win_ref[...].astype(BF16)
    h = _rms(x_ref[...], nm_ref[...]).astype(BF16)
    z = _dot(h, wbf_ref[...])
    zeros = jnp.zeros((n, A_HEAD_DIM), F32)
    for hh in range(A_HEADS):
        qh = _rms(z[:, hh * A_HEAD_DIM:(hh + 1) * A_HEAD_DIM], qn_ref[...]) * ATTN_SCALE
        qx_ref[hh] = jnp.concatenate([qh, zeros] if hh // A_GROUP == 0 else [zeros, qh], axis=-1)
    kparts = []
    for g in range(A_KV_HEADS):
        kparts.append(_rms(z[:, A_Q + g * A_HEAD_DIM:A_Q + (g + 1) * A_HEAD_DIM], kn_ref[...]))
    knew_ref[...] = jnp.concatenate(kparts, axis=-1)
    vnew_ref[...] = z[:, A_Q + A_KV:A_Q + 2 * A_KV]

    u = _gelu(z[:, A_Q + 2 * A_KV:A_Q + 2 * A_KV + B_WIDTH])
    vln = _layernorm(_gelu(z[:, A_Q + 2 * A_KV + B_WIDTH:AB_IN]), lng_ref[...], lnb_ref[...])
    grp = lax.broadcasted_iota(jnp.int32, (1, B_WIDTH), 1) // B_GROUP_DIM
    srow = jnp.zeros((1, B_WIDTH), F32)
    brow = jnp.zeros((1, B_WIDTH), F32)
    for g in range(B_GROUPS):
        srow = jnp.where(grp == g, w00_ref[g], srow)
        brow = jnp.where(grp == g, b0_ref[g], brow)
    bm_ref[...] = u * (vln * srow + brow)
    gv_ref[...] = vln


def _ab_sample_proj(x, nm, w_in, qn, kn, lng, lnb, w00, b0):
    n = x.shape[0]
    return pl.pallas_call(
        _ab_sample_proj_kernel,
        in_specs=[_SMEM, _SMEM] + [pl.BlockSpec(memory_space=pltpu.VMEM)] * 7,
        out_shape=[
            jax.ShapeDtypeStruct((A_HEADS, n, A_KV), F32),
            jax.ShapeDtypeStruct((n, A_KV), F32),
            jax.ShapeDtypeStruct((n, A_KV), F32),
            jax.ShapeDtypeStruct((n, B_WIDTH), F32),
            jax.ShapeDtypeStruct((n, B_WIDTH), F32),
            jax.ShapeDtypeStruct(w_in.shape, BF16),
        ],
        compiler_params=pltpu.CompilerParams(vmem_limit_bytes=VMEM_LIMIT),
        name="ab_sample_proj",
    )(w00, b0, x, nm, w_in, qn, kn, lng, lnb)


def _hgrn_sample_proj_kernel(x_ref, nm_ref, win_ref, clb_ref, q_ref, f_ref, i_ref, sg_ref, wbf_ref):
    wbf_ref[...] = win_ref[...].astype(BF16)
    h = _rms(x_ref[...], nm_ref[...]).astype(BF16)
    z = _dot(h, wbf_ref[...])
    lb = _lower_bound(clb_ref[...])

    def put(ref, val):
        per = ref.shape[1]
        for i in range(ref.shape[0]):
            ref[i] = val[i * per:(i + 1) * per, :]

    put(q_ref, z[:, 0:C_F])
    put(f_ref, lb + (1.0 - lb) * jax.nn.sigmoid(z[:, C_F:2 * C_F]))
    put(i_ref, z[:, 2 * C_F:2 * C_F + C_V])
    put(sg_ref, jax.nn.sigmoid(z[:, 2 * C_F + C_V:C_IN]))


def _hgrn_sample_proj(x, nm, w_in, clb, per_step):
    n = x.shape[0]
    return pl.pallas_call(
        _hgrn_sample_proj_kernel,
        out_shape=[jax.ShapeDtypeStruct((n // per_step, per_step, C_F), F32)] * 2
        + [jax.ShapeDtypeStruct((n // per_step, per_step, C_V), F32)] * 2
        + [jax.ShapeDtypeStruct(w_in.shape, BF16)],
        compiler_params=pltpu.CompilerParams(vmem_limit_bytes=VMEM_LIMIT),
        name="hgrn_sample_proj",
    )(x, nm, w_in, clb)


def kernel(x_prompt, x_sample, cache_k, cache_v, state_hgrn, norm_mix, norm_ffn, w_in_ab, w_out_ab,
           q_norm, k_norm, attn_sink, rel_bias, gmlp_ln_g, gmlp_ln_b, gmlp_w_s, gmlp_b_s,
           w_in_c, c_lower_bounds, c_out_norm, w_out_c, w_gate, w_up, w_down):
    assert norm_mix.shape[0] == DEPTH == 2 and w_in_ab.shape[0] == 1 and w_in_c.shape[0] == 1
    nb, seq, _ = x_prompt.shape
    ns = x_sample.shape[0]
    assert x_sample.shape[1] == 1 and cache_k.shape[2] == WINDOW

    row = lambda v: v.reshape(1, -1)
    bf = lambda w: w.astype(BF16)
    w_out_ab0, w_out_c0 = bf(w_out_ab[0]), bf(w_out_c[0])
    nm, nf = norm_mix, norm_ffn
    qn, kn = row(q_norm[0]), row(k_norm[0])
    lng, lnb = row(gmlp_ln_g[0]), row(gmlp_ln_b[0])
    sink = attn_sink[0]

    tab, tabp = _bias_table(rel_bias)

    xs = x_sample.reshape(ns, D_MODEL)
    qx, knew_s, vnew_s, bm_s, gv_s, w_in_ab0 = _ab_sample_proj(
        xs, row(nm[0]), w_in_ab[0], qn, kn, lng, lnb, gmlp_w_s[0, :, 0, 0], gmlp_b_s[0, :, 0])
    sb = jnp.pad(tab[:, WINDOW - 1, WINDOW - 1:], ((0, 0), (0, WINDOW - 1)))
    to_t = lambda c: c.transpose(0, 2, 3, 1).reshape(ns, A_KV, WINDOW)
    xp, knew_p, vnew_p, gv_p, nk_s, nv_s, om = _ab_prompt(
        x_prompt, row(nm[0]), w_in_ab0, jnp.tile(qn, (1, A_HEADS)), jnp.tile(kn, (1, A_KV_HEADS)),
        sink, tabp, lng, lnb, gmlp_w_s[0], jnp.repeat(gmlp_b_s[0].T, B_GROUP_DIM, axis=1), w_out_ab0,
        to_t(cache_k[0]), to_t(cache_v[0]), qx.transpose(1, 0, 2), knew_s, vnew_s, sb)
    om = om.reshape(ns, A_KV_HEADS, A_GROUP, A_KV_HEADS, A_HEAD_DIM)
    a_s = jnp.stack([om[:, g, :, g, :] for g in range(A_KV_HEADS)], axis=1).reshape(ns, A_Q)
    xp, xs = _ffn(xp.reshape(nb * seq, D_MODEL), xs, jnp.concatenate([a_s, bm_s], axis=-1), w_out_ab0,
                  row(nf[0]), w_gate, w_up, w_down, 0)

    q_s, f_s, i_s, sg_s, w_in_c0 = _hgrn_sample_proj(xs, row(nm[1]), w_in_c[0], c_lower_bounds,
                                                     ns // (nb * (seq // TQH)))
    xp, st_p, st_s, o_s = _hgrn_prompt(xp.reshape(nb, seq, D_MODEL), row(nm[1]), w_in_c0, c_lower_bounds,
                                       row(c_out_norm[0]), w_out_c0, state_hgrn[0], f_s, q_s, i_s, sg_s)
    xp, xs = _ffn(xp.reshape(nb * seq, D_MODEL), xs, o_s, w_out_c0, row(nf[1]), w_gate, w_up, w_down, 1)

    kv5 = lambda a: a.reshape(1, a.shape[0], WINDOW, A_KV_HEADS, A_HEAD_DIM)
    from_t = lambda a: a.reshape(ns, A_KV_HEADS, A_HEAD_DIM, WINDOW).transpose(0, 3, 1, 2)[None]
    return (xp.reshape(nb, seq, D_MODEL), xs.reshape(ns, 1, D_MODEL),
            kv5(knew_p), kv5(vnew_p), from_t(nk_s), from_t(nv_s),
            gv_p[None], gv_s.reshape(1, ns, 1, B_WIDTH),
            st_p[None], st_s[None])
```

```python
import functools
import math

import jax
import jax.numpy as jnp
import numpy as np
from jax import lax
from jax.experimental import pallas as pl
from jax.experimental.pallas import tpu as pltpu

F32 = jnp.float32
BF16 = jnp.bfloat16

D_MODEL = 1024
DEPTH = 2
A_HEADS = 8
A_KV_HEADS = 2
A_GROUP = A_HEADS // A_KV_HEADS
A_HEAD_DIM = 64
WINDOW = 128
ATTN_SCALE = A_HEAD_DIM ** -0.5
NUM_BUCKETS = 32
MAX_DISTANCE = 128
A_Q = A_HEADS * A_HEAD_DIM
A_KV = A_KV_HEADS * A_HEAD_DIM
B_GROUPS = 8
B_GROUP_DIM = 64
B_WIDTH = B_GROUPS * B_GROUP_DIM
B_CHUNK = 128
AB_IN = A_Q + 2 * A_KV + 2 * B_WIDTH
AB_MIX = A_Q + B_WIDTH
C_HEADS = 8
C_KEY_DIM = 128
C_VAL_DIM = 128
C_F = C_HEADS * C_KEY_DIM
C_V = C_HEADS * C_VAL_DIM
C_IN = 2 * C_F + 2 * C_V
D_FF = 2816
EPS = 1e-6

NEG = -1e30

VMEM_LIMIT = 56 * 1024 * 1024
VREG_ROWS = 8

CHUNK = 128
TQ = 1024
TQH = 512
TM = 512
HEAD_SKEW = 2
SAFE_LOG2_RANGE = 64.0
PREFIX_GROUP = 4
FACTORED_UNROLL = 4
AB_UNROLL = 2

_NT = (((1,), (1,)), ((), ()))
_TN = (((0,), (0,)), ((), ()))


def _rms(x, g):
    return x * lax.rsqrt(jnp.mean(x * x, axis=-1, keepdims=True) + EPS) * g


def _gelu(x):
    return 0.5 * x * (1.0 + lax.erf(x * math.sqrt(0.5)))


def _layernorm(x, g, b):
    xc = x - jnp.mean(x, axis=-1, keepdims=True)
    return xc * lax.rsqrt(jnp.mean(xc * xc, axis=-1, keepdims=True) + EPS) * g + b


def _dot(a, b):
    return jnp.dot(a, b, preferred_element_type=F32)


def _full(shape):
    n = len(shape)
    return pl.BlockSpec(shape, lambda *_: (0,) * n)


def _resident(shape):
    n = len(shape)
    return pl.BlockSpec(shape, lambda *_: (0,) * n, pipeline_mode=pl.Buffered(1))


_SMEM = pl.BlockSpec(memory_space=pltpu.SMEM)


def _bias_table_kernel(rel_ref, tab_ref, tabp_ref):
    qi = lax.broadcasted_iota(jnp.int32, (WINDOW, 2 * WINDOW), 0)
    kj = lax.broadcasted_iota(jnp.int32, (WINDOW, 2 * WINDOW), 1)
    dist = qi + WINDOW - kj
    ok = (dist >= 0) & (dist < WINDOW)
    max_exact = NUM_BUCKETS // 2
    d = jnp.maximum(dist, 0)
    dl = jnp.maximum(d, 1).astype(F32)
    v = (jnp.log(dl / max_exact) / math.log(MAX_DISTANCE / max_exact) * (NUM_BUCKETS - max_exact))
    far = d >= max_exact
    hits = []
    for b in range(NUM_BUCKETS):
        if b < max_exact:
            hits.append(d == b)
        elif b < NUM_BUCKETS - 1:
            hits.append(far & (v >= b - max_exact) & (v < b - max_exact + 1))
        else:
            hits.append(far & (v >= b - max_exact))
    for h in range(A_HEADS):
        acc = jnp.zeros((WINDOW, 2 * WINDOW), F32)
        for b in range(NUM_BUCKETS):
            acc = jnp.where(hits[b], rel_ref[b, h], acc)
        t = jnp.where(ok, acc, NEG)
        tab_ref[h] = t
        cols = slice((h % 2) * 2 * WINDOW, (h % 2 + 1) * 2 * WINDOW)
        tabp_ref[0, h // 2, :, cols] = t
        tabp_ref[1, h // 2, :, cols] = jnp.where(kj < WINDOW, NEG, t)


def _bias_table(rel_bias):
    return pl.pallas_call(
        _bias_table_kernel,
        out_shape=[
            jax.ShapeDtypeStruct((A_HEADS, WINDOW, 2 * WINDOW), F32),
            jax.ShapeDtypeStruct((2, A_HEADS // 2, WINDOW, 4 * WINDOW), F32),
        ],
        in_specs=[_SMEM],
        name="bias_table",
    )(rel_bias)


PAIR = 2 * A_HEAD_DIM
N_PAIRS = A_HEADS // 2


def _sample_cache_attention(ck_ref, cv_ref, qx_ref, kn_ref, vn_ref, sb_ref, sink_ref,
                            nk_ref, nv_ref, om_ref):
    wb = ck_ref.shape[2]
    head = lax.broadcasted_iota(jnp.int32, (1, A_HEADS, A_KV), 1)
    lane = lax.broadcasted_iota(jnp.int32, (1, A_HEADS, A_KV), 2)
    own_group = (head // A_GROUP) == (lane // A_HEAD_DIM)
    newest = lax.broadcasted_iota(jnp.int32, (1, 1, wb), 2) == wb - 1
    sink = sink_ref[...][None]
    kc, vc = ck_ref[...], cv_ref[...]
    kn, vn = kn_ref[...], vn_ref[...]
    kn_cols, vn_cols = kn_ref[:, 0, :].T, vn_ref[:, 0, :].T
    for i in range(kc.shape[0]):
        nk_ref[i] = jnp.where(newest[0], kn_cols[:, i:i + 1], pltpu.roll(kc[i], wb - 1, 1))
        nv_ref[i] = jnp.where(newest[0], vn_cols[:, i:i + 1], pltpu.roll(vc[i], wb - 1, 1))
    q = qx_ref[...]
    s = jnp.einsum('bhd,bdk->bhk', q.astype(BF16), kc.astype(BF16), preferred_element_type=F32)
    s = s + sb_ref[:, 0:wb][None]
    sn = jnp.sum(q * kn, axis=-1, keepdims=True) + sb_ref[:, wb:wb + 1][None]
    m = jnp.maximum(jnp.maximum(jnp.max(s, axis=-1, keepdims=True), sn), sink)
    e = jnp.exp(s - m)
    en = jnp.exp(sn - m)
    r = 1.0 / (jnp.sum(e, axis=-1, keepdims=True) + en + jnp.exp(sink - m))
    o = jnp.einsum('bhk,bdk->bhd', (e * r).astype(BF16), vc.astype(BF16),
                   preferred_element_type=F32) + (en * r) * vn
    om_ref[...] = jnp.where(own_group, o, 0.0)


def _ab_prompt_kernel(sink_ref, x_ref, nm_ref, win_ref, qg_ref, kg_ref, tabp_ref, lng_ref, lnb_ref,
                      ws_ref, bsp_ref, wout_ref,
                      ck_ref, cv_ref, qx_ref, kn_ref, vn_ref, sb_ref, sinkc_ref,
                      y_ref, knew_ref, vnew_ref, gv_ref, nk_ref, nv_ref, om_ref,
                      z_ref, mix_ref, q_ref, k_ref, kr_ref, v_ref, vr_ref, wpair_ref, kl_ref, vl_ref, gl_ref):
    j = pl.program_id(1)
    last_j = pl.num_programs(1) - 1
    n_chunks = TQ // CHUNK

    @pl.when(j == 0)
    def _():
        for ref in (k_ref, kr_ref, v_ref, vr_ref):
            ref[0:CHUNK, :] = jnp.zeros((CHUNK, A_KV), BF16)
        row = lax.broadcasted_iota(jnp.int32, (B_CHUNK, B_CHUNK), 0)
        col = lax.broadcasted_iota(jnp.int32, (B_CHUNK, B_CHUNK), 1)
        for g in range(B_GROUPS):
            wpair_ref[g // 2, :, (g % 2) * B_CHUNK:(g % 2 + 1) * B_CHUNK] = jnp.where(
                row >= col, ws_ref[g], 0.0).astype(BF16)

    h = _rms(x_ref[0], nm_ref[...]).astype(BF16)
    z_ref[...] = _dot(h, win_ref[...])

    _sample_cache_attention(ck_ref, cv_ref, qx_ref, kn_ref, vn_ref, sb_ref, sinkc_ref,
                            nk_ref, nv_ref, om_ref)

    lo_half = lax.broadcasted_iota(jnp.int32, (1, PAIR), 1) < A_HEAD_DIM

    def mean_sq_halves(x):
        x2 = x * x
        lo_sum = jnp.sum(jnp.where(lo_half, x2, 0.0), axis=-1, keepdims=True)
        hi_sum = jnp.sum(jnp.where(lo_half, 0.0, x2), axis=-1, keepdims=True)
        return jnp.where(lo_half, lo_sum, hi_sum) * (1.0 / A_HEAD_DIM)

    def block_diag(top, bot):
        zero = jnp.zeros_like(top)
        return jnp.concatenate([jnp.where(lo_half, top, zero), jnp.where(lo_half, zero, bot)], axis=0)

    kraw = z_ref[:, A_Q:A_Q + A_KV]
    v_all = z_ref[:, A_Q + A_KV:A_Q + 2 * A_KV]
    kn_all = kraw * lax.rsqrt(mean_sq_halves(kraw) + EPS) * kg_ref[...]
    k_ref[CHUNK:, :] = kn_all.astype(BF16)
    kr_ref[CHUNK:, :] = pltpu.roll(kn_all, A_HEAD_DIM, 1).astype(BF16)
    v_ref[CHUNK:, :] = v_all.astype(BF16)
    vr_ref[CHUNK:, :] = pltpu.roll(v_all, A_HEAD_DIM, 1).astype(BF16)
    kl_ref[...] = kn_all[TQ - CHUNK:, :]
    vl_ref[...] = v_all[TQ - CHUNK:, :]
    for i in range(N_PAIRS):
        ps = slice(i * PAIR, (i + 1) * PAIR)
        qraw = z_ref[:, ps]
        qn = qraw * lax.rsqrt(mean_sq_halves(qraw) + EPS) * (qg_ref[:, ps] * ATTN_SCALE)
        q_ref[:, ps] = qn.astype(BF16)

    def chunk(c, carry):
        r0 = pl.multiple_of(c * CHUNK, CHUNK)
        rows = pl.ds(r0, CHUNK)
        first = jnp.where(jnp.logical_and(j == 0, c == 0), 1, 0)

        both = pl.ds(r0, 2 * CHUNK)
        k2, k2r, v2, v2r = k_ref[both, :], kr_ref[both, :], v_ref[both, :], vr_ref[both, :]
        kbd = [block_diag(k2, k2r), block_diag(k2r, k2)]
        vbd = [block_diag(v2, v2r), block_diag(v2r, v2)]

        scores = []
        for i in range(N_PAIRS):
            s = lax.dot_general(q_ref[rows, i * PAIR:(i + 1) * PAIR], kbd[i // (A_GROUP // 2)], _NT,
                                preferred_element_type=F32)
            scores.append(s + tabp_ref[first, i])
        outs = []
        for i in range(N_PAIRS):
            es, rs = [], []
            for hh in range(2):
                sh = scores[i][:, hh * 2 * WINDOW:(hh + 1) * 2 * WINDOW]
                sk = sink_ref[2 * i + hh]
                m = jnp.maximum(jnp.max(sh, axis=-1, keepdims=True), sk)
                e = jnp.exp(sh - m)
                rs.append(1.0 / (jnp.sum(e, axis=-1, keepdims=True) + jnp.exp(sk - m)))
                es.append(e.astype(BF16))
            o = _dot(jnp.concatenate(es, axis=-1), vbd[i // (A_GROUP // 2)])
            outs.append(o * jnp.where(lo_half, rs[0], rs[1]))
        mix_ref[rows, 0:A_Q] = jnp.concatenate(outs, axis=-1).astype(BF16)

        zu = z_ref[rows, A_Q + 2 * A_KV:A_Q + 2 * A_KV + B_WIDTH]
        zv = z_ref[rows, A_Q + 2 * A_KV + B_WIDTH:AB_IN]
        u = _gelu(zu)
        vln = _layernorm(_gelu(zv), lng_ref[...], lnb_ref[...])
        vlb = vln.astype(BF16)
        sparts = []
        for i in range(B_GROUPS // 2):
            vpair = vlb[:, i * PAIR:(i + 1) * PAIR]
            sparts.append(_dot(wpair_ref[i], block_diag(vpair, vpair)))
        bm = u * (jnp.concatenate(sparts, axis=-1) + bsp_ref[...])
        mix_ref[rows, A_Q:AB_MIX] = bm.astype(BF16)

        gl_ref[...] = vln
        return carry

    lax.fori_loop(0, n_chunks, chunk, 0, unroll=AB_UNROLL)
    y_ref[0] = x_ref[0] + _dot(mix_ref[...], wout_ref[...])
    for ref in (k_ref, kr_ref, v_ref, vr_ref):
        ref[0:CHUNK, :] = ref[TQ:TQ + CHUNK, :]

    @pl.when(j == last_j)
    def _():
        knew_ref[0] = kl_ref[...]
        vnew_ref[0] = vl_ref[...]
        gv_ref[0] = gl_ref[...]


def _ab_prompt(x, nm, w_in, qg, kg, sink, tabp, lng, lnb, w_s, bsp, w_out, ck, cv, qx, kn_s, vn_s, sb):
    nb, seq, _ = x.shape
    nj = seq // TQ
    grid = (nb, nj)
    ns, _, wb = ck.shape
    sba = ns // (nb * nj)
    assert sba * nb * nj == ns
    blk = lambda b, j: (b, j, 0)
    per_b = lambda b, j: (b, 0, 0)
    step = lambda b, j: (b * nj + j, 0, 0)
    return pl.pallas_call(
        _ab_prompt_kernel,
        grid=grid,
        in_specs=[
            _SMEM,
            pl.BlockSpec((1, TQ, D_MODEL), blk),
            _full((1, D_MODEL)),
            _resident((D_MODEL, AB_IN)),
            _full((1, A_Q)),
            _full((1, A_KV)),
            _resident((2, N_PAIRS, WINDOW, 4 * WINDOW)),
            _full((1, B_WIDTH)),
            _full((1, B_WIDTH)),
            _resident((B_GROUPS, B_CHUNK, B_CHUNK)),
            _resident((B_CHUNK, B_WIDTH)),
            _resident((AB_MIX, D_MODEL)),
            pl.BlockSpec((sba, A_KV, wb), step),
            pl.BlockSpec((sba, A_KV, wb), step),
            pl.BlockSpec((sba, A_HEADS, A_KV), step),
            pl.BlockSpec((sba, 1, A_KV), step),
            pl.BlockSpec((sba, 1, A_KV), step),
            _full((A_HEADS, 2 * WINDOW)),
            _full((A_HEADS, 1)),
        ],
        out_specs=[
            pl.BlockSpec((1, TQ, D_MODEL), blk),
            pl.BlockSpec((1, WINDOW, A_KV), per_b),
            pl.BlockSpec((1, WINDOW, A_KV), per_b),
            pl.BlockSpec((1, B_CHUNK, B_WIDTH), per_b),
            pl.BlockSpec((sba, A_KV, wb), step),
            pl.BlockSpec((sba, A_KV, wb), step),
            pl.BlockSpec((sba, A_HEADS, A_KV), step),
        ],
        out_shape=[
            jax.ShapeDtypeStruct((nb, seq, D_MODEL), F32),
            jax.ShapeDtypeStruct((nb, WINDOW, A_KV), F32),
            jax.ShapeDtypeStruct((nb, WINDOW, A_KV), F32),
            jax.ShapeDtypeStruct((nb, B_CHUNK, B_WIDTH), F32),
            jax.ShapeDtypeStruct((ns, A_KV, wb), F32),
            jax.ShapeDtypeStruct((ns, A_KV, wb), F32),
            jax.ShapeDtypeStruct((ns, A_HEADS, A_KV), F32),
        ],
        scratch_shapes=[
            pltpu.VMEM((TQ, AB_IN), F32),
            pltpu.VMEM((TQ, AB_MIX), BF16),
            pltpu.VMEM((TQ, A_Q), BF16),
            pltpu.VMEM((CHUNK + TQ, A_KV), BF16),
            pltpu.VMEM((CHUNK + TQ, A_KV), BF16),
            pltpu.VMEM((CHUNK + TQ, A_KV), BF16),
            pltpu.VMEM((CHUNK + TQ, A_KV), BF16),
            pltpu.VMEM((B_GROUPS // 2, B_CHUNK, 2 * B_CHUNK), BF16),
            pltpu.VMEM((WINDOW, A_KV), F32),
            pltpu.VMEM((WINDOW, A_KV), F32),
            pltpu.VMEM((B_CHUNK, B_WIDTH), F32),
        ],
        compiler_params=pltpu.CompilerParams(
            dimension_semantics=("arbitrary", "arbitrary"), vmem_limit_bytes=VMEM_LIMIT),
        name="ab_prompt",
    )(sink, x, nm, w_in, qg, kg, tabp, lng, lnb, w_s, bsp, w_out,
      ck, cv, qx, kn_s[:, None, :], vn_s[:, None, :], sb, sink.reshape(A_HEADS, 1))


FF_TILE = 256
FF_SUB = 2


def _ffn_kernel(xp_ref, xs_ref, ms_ref, wo_ref, g_ref, wg_ref, wu_ref, wd_ref, yp_ref, ys_ref,
                wg_s, wu_s, wd_s, h0_s, acc_s, *, n_cast, n_prompt):
    s = pl.program_id(0)

    def gated(h, wg, wu):
        gate = _dot(h, wg)
        return (gate * jax.nn.sigmoid(gate) * _dot(h, wu)).astype(BF16)

    first = slice(0, TM)

    @pl.when(s == 0)
    def _():
        x = xp_ref[first, :]
        h0_s[...] = _rms(x, g_ref[...]).astype(BF16)
        acc_s[...] = x

    for c in range(n_cast):
        @pl.when(s == c)
        def _(c=c):
            tile = slice(c * FF_TILE, (c + 1) * FF_TILE)
            wg_t, wu_t, wd_t = (r[...].astype(BF16) for r in (wg_ref, wu_ref, wd_ref))
            wg_s[:, tile] = wg_t
            wu_s[:, tile] = wu_t
            wd_s[tile, :] = wd_t
            acc_s[...] += _dot(gated(h0_s[...], wg_t, wu_t), wd_t)

    def swiglu(x):
        h = _rms(x, g_ref[...]).astype(BF16)
        return x + _dot(gated(h, wg_s[...], wu_s[...]), wd_s[...])

    @pl.when(s == n_cast - 1)
    def _():
        yp_ref[first, :] = acc_s[...]
        for r in range(1, FF_SUB):
            rows = slice(r * TM, (r + 1) * TM)
            yp_ref[rows, :] = swiglu(xp_ref[rows, :])

    @pl.when(jnp.logical_and(s >= n_cast, s < n_cast + n_prompt - 1))
    def _():
        for r in range(FF_SUB):
            rows = slice(r * TM, (r + 1) * TM)
            yp_ref[rows, :] = swiglu(xp_ref[rows, :])

    @pl.when(s == n_cast + n_prompt - 1)
    def _():
        ys_ref[...] = swiglu(xs_ref[...] + _dot(ms_ref[...].astype(BF16), wo_ref[...]))


def _ffn(xp, xs, mix_s, w_o, g, w_gate, w_up, w_down, layer):
    rows, ns = xp.shape[0], xs.shape[0]
    n_cast, n_prompt = D_FF // FF_TILE, rows // (TM * FF_SUB)
    w_tile = lambda s: jnp.minimum(s, n_cast - 1)
    row_blk = lambda s: (jnp.clip(s - (n_cast - 1), 0, n_prompt - 1), 0)
    return pl.pallas_call(
        functools.partial(_ffn_kernel, n_cast=n_cast, n_prompt=n_prompt),
        grid=(n_cast + n_prompt,),
        in_specs=[
            pl.BlockSpec((TM * FF_SUB, D_MODEL), row_blk),
            _full((ns, D_MODEL)),
            _full((ns, D_MODEL)),
            _resident((D_MODEL, D_MODEL)),
            _full((1, D_MODEL)),
            pl.BlockSpec((None, D_MODEL, FF_TILE), lambda s: (layer, 0, w_tile(s))),
            pl.BlockSpec((None, D_MODEL, FF_TILE), lambda s: (layer, 0, w_tile(s))),
            pl.BlockSpec((None, FF_TILE, D_MODEL), lambda s: (layer, w_tile(s), 0)),
        ],
        out_specs=[pl.BlockSpec((TM * FF_SUB, D_MODEL), row_blk), _full((ns, D_MODEL))],
        out_shape=[jax.ShapeDtypeStruct((rows, D_MODEL), F32), jax.ShapeDtypeStruct((ns, D_MODEL), F32)],
        scratch_shapes=[
            pltpu.VMEM((D_MODEL, D_FF), BF16),
            pltpu.VMEM((D_MODEL, D_FF), BF16),
            pltpu.VMEM((D_FF, D_MODEL), BF16),
            pltpu.VMEM((TM, D_MODEL), BF16),
            pltpu.VMEM((TM, D_MODEL), F32),
        ],
        compiler_params=pltpu.CompilerParams(
            dimension_semantics=("arbitrary",), vmem_limit_bytes=VMEM_LIMIT),
        name="ffn",
    )(xp, xs, mix_s, w_o, g, w_gate, w_up, w_down)


def _lower_bound(clb):
    m = jnp.max(clb, axis=0, keepdims=True)
    e = jnp.exp(clb - m)
    sm = e / jnp.sum(e, axis=0, keepdims=True)
    return (sm[0:1] + sm[1:2]) - sm[0:1]


def _split3(x):
    hi = x.astype(BF16)
    r = x - hi.astype(F32)
    mid = r.astype(BF16)
    lo = (r - mid.astype(F32)).astype(BF16)
    return hi, mid, lo


def _neg_abs(x):
    return lax.bitcast_convert_type(
        lax.bitcast_convert_type(x, jnp.uint32) | jnp.uint32(0x80000000), F32)


def _pair_level_table():
    t = np.arange(CHUNK)[:, None]
    s = np.arange(CHUNK)[None, :]
    lev = np.floor(np.log2(np.maximum(t ^ s, 1))).astype(np.int32)
    lev = np.where(t == s, -1, lev)
    return np.where(s > t, -2, lev).astype(np.int32)


def _level_operand(p, q, kk, f, b2):
    m = 2 ** p
    if m < VREG_ROWS:
        shape3 = (CHUNK // VREG_ROWS, VREG_ROWS, q.shape[1])
        sub = lax.broadcasted_iota(jnp.int32, (1, VREG_ROWS, q.shape[1]), 1)
        upper = ((sub >> p) & 1) == 1
        q3, k3 = q.reshape(shape3), kk.reshape(shape3)
        if p == 0:
            y = jnp.where(upper, q3 * f.reshape(shape3), k3)
        else:
            b3 = b2.reshape(shape3)
            be = b3[:, m - 1:m, :]
            for k in range(1, VREG_ROWS // (2 * m)):
                be = jnp.where(sub >= 2 * m * k, b3[:, 2 * m * k + m - 1:2 * m * k + m, :], be)
            y = jnp.where(upper, q3, k3) * jnp.exp2(_neg_abs(b3 - be))
        return y.reshape(q.shape).astype(BF16)
    parts = []
    for k in range(CHUNK // (2 * m)):
        lo = slice(2 * m * k, 2 * m * k + m)
        up = slice(2 * m * k + m, 2 * m * (k + 1))
        be = b2[2 * m * k + m - 1:2 * m * k + m, :]
        parts.append(kk[lo] * jnp.exp2(be - b2[lo]))
        parts.append(q[up] * jnp.exp2(b2[up] - be))
    return jnp.concatenate(parts, axis=0).astype(BF16)


def _merge_level(p, att, pm, lev):
    m = 2 ** p
    if m < VREG_ROWS:
        return jnp.where(lev == p, pm, att)
    col = lax.broadcasted_iota(jnp.int32, (1, CHUNK), 1)
    parts = []
    for k in range(CHUNK // (2 * m)):
        lo = slice(2 * m * k, 2 * m * k + m)
        up = slice(2 * m * k + m, 2 * m * (k + 1))
        parts.append(att[lo])
        parts.append(jnp.where((col >= 2 * m * k) & (col < 2 * m * k + m), pm[up], att[up]))
    return jnp.concatenate(parts, axis=0)


def _hgrn_prompt_kernel(x_ref, nm_ref, win_ref, clb_ref, on_ref, wout_ref, lev_ref,
                        ss_ref, fs_ref, qs_ref, is_ref, gs_ref,
                        y_ref, st_ref, sso_ref, os_ref,
                        z_ref, o_ref, stt_ref, k_ref):
    j = pl.program_id(1)
    last_j = pl.num_programs(1) - 1
    n_chunks = TQH // CHUNK
    n_levels = int(math.log2(CHUNK))

    @pl.when(j == 0)
    def _():
        stt_ref[...] = jnp.zeros_like(stt_ref)

    h = _rms(x_ref[0], nm_ref[...]).astype(BF16)
    z_ref[...] = _dot(h, win_ref[...])

    head_cols = lambda ref: [ref[:, hd * C_KEY_DIM:(hd + 1) * C_KEY_DIM].T for hd in range(C_HEADS)]
    f_cols, q_cols = head_cols(fs_ref), head_cols(qs_ref)
    out_rows = []
    for smp in range(ss_ref.shape[0]):
        parts = []
        for hd in range(C_HEADS):
            hs = slice(hd * C_VAL_DIM, (hd + 1) * C_VAL_DIM)
            fb = jnp.broadcast_to(f_cols[hd][:, smp:smp + 1], (C_KEY_DIM, C_VAL_DIM))
            sn = fb * ss_ref[smp, hd] + (1.0 - fb) * is_ref[smp:smp + 1, hs]
            sso_ref[smp, hd] = sn
            o = jnp.sum(q_cols[hd][:, smp:smp + 1] * sn, axis=0, keepdims=True)
            parts.append(_rms(o, on_ref[...]))
        out_rows.append(jnp.concatenate(parts, axis=-1))
    os_ref[...] = jnp.concatenate(out_rows, axis=0) * gs_ref[...]

    lb = _lower_bound(clb_ref[...])

    row = lax.broadcasted_iota(jnp.int32, (CHUNK, CHUNK), 0)
    col = lax.broadcasted_iota(jnp.int32, (CHUNK, CHUNK), 1)
    ltri = (row >= col).astype(BF16)

    def chunk_rows(c):
        return pl.ds(pl.multiple_of(c * CHUNK, CHUNK), CHUNK)

    def prefix(g, worst):
        rows = [chunk_rows(g * PREFIX_GROUP + i) for i in range(PREFIX_GROUP)]
        gates = [z_ref[r, C_F:2 * C_F] for r in rows]
        for r, gate in zip(rows, gates):
            f_all = lb + (1.0 - lb) * jax.nn.sigmoid(gate)
            k_ref[r, :] = 1.0 - f_all
            hi, mid, lo = _split3(jnp.log2(f_all))
            b2 = (_dot(ltri, hi) + _dot(ltri, mid)) + _dot(ltri, lo)
            z_ref[r, C_F:2 * C_F] = b2
            b_mid = b2[CHUNK // 2 - 1:CHUNK // 2, :]
            b_last = b2[CHUNK - 1:CHUNK, :]
            worst = jnp.maximum(worst, jnp.maximum(-b_mid, b_mid - b_last))
        return worst

    worst = lax.fori_loop(0, n_chunks // PREFIX_GROUP, prefix, jnp.zeros((1, C_F), F32))
    bounded = jnp.max(worst) <= SAFE_LOG2_RANGE

    def finish_head(rows, hd, o):
        gt = z_ref[rows, 2 * C_F + C_V + hd * C_VAL_DIM:2 * C_F + C_V + (hd + 1) * C_VAL_DIM]
        o = _rms(o, on_ref[...]) * jax.nn.sigmoid(gt)
        o_ref[rows, hd * C_VAL_DIM:(hd + 1) * C_VAL_DIM] = o.astype(BF16)

    def head_inputs(rows, hd):
        q = z_ref[rows, hd * C_KEY_DIM:(hd + 1) * C_KEY_DIM]
        kk = k_ref[rows, hd * C_KEY_DIM:(hd + 1) * C_KEY_DIM]
        b2 = z_ref[rows, C_F + hd * C_KEY_DIM:C_F + (hd + 1) * C_KEY_DIM]
        ivb = z_ref[rows, 2 * C_F + hd * C_VAL_DIM:2 * C_F + (hd + 1) * C_VAL_DIM].astype(BF16)
        return q, kk, b2, ivb

    def factored_chunk(c, carry):
        rows = chunk_rows(c)
        for hd in range(C_HEADS):
            q, kk, b2, ivb = head_inputs(rows, hd)
            b_mid = b2[CHUNK // 2 - 1:CHUNK // 2, :]
            b_last = b2[CHUNK - 1:CHUNK, :]
            qs = (q * jnp.exp2(b2 - b_mid)).astype(BF16)
            kd = (kk * jnp.exp2(b_mid - b2)).astype(BF16)
            att = jnp.where(row >= col, lax.dot_general(qs, kd, _NT, preferred_element_type=F32), 0.0)
            stt = stt_ref[hd]
            o = lax.dot_general(qs, (stt * jnp.exp2(b_mid)).astype(BF16), _NT,
                                preferred_element_type=F32) + _dot(att.astype(BF16), ivb)
            stt_ref[hd] = stt * jnp.exp2(b_last) + jnp.exp2(b_last - b_mid) * lax.dot_general(
                ivb, kd, _TN, preferred_element_type=F32)
            finish_head(rows, hd, o)
        return carry

    def tree_chunk(c, carry):
        rows = chunk_rows(c)
        lev = lev_ref[...]

        def products(hd):
            q, kk, b2, ivb = head_inputs(rows, hd)
            diag = jnp.sum(q * kk, axis=-1, keepdims=True)
            pms = []
            for p in range(n_levels):
                y = _level_operand(p, q, kk, 1.0 - kk, b2)
                pms.append(lax.dot_general(y, y, _NT, preferred_element_type=F32))
            stt = stt_ref[hd]
            o_prev = lax.dot_general((q * jnp.exp2(b2)).astype(BF16), stt.astype(BF16), _NT,
                                     preferred_element_type=F32)
            b_last = b2[CHUNK - 1:CHUNK, :]
            kd = (kk * jnp.exp2(b_last - b2)).astype(BF16)
            stt_ref[hd] = stt * jnp.exp2(b_last) + lax.dot_general(
                ivb, kd, _TN, preferred_element_type=F32)
            return diag, pms, o_prev, ivb

        def finish(hd, diag, pms, o_prev, ivb):
            att = jnp.where(lev == -1, diag, 0.0)
            for p in range(n_levels):
                att = _merge_level(p, att, pms[p], lev)
            finish_head(rows, hd, o_prev + _dot(att.astype(BF16), ivb))

        pending = [products(hd) for hd in range(HEAD_SKEW)]
        for hd in range(C_HEADS):
            if hd + HEAD_SKEW < C_HEADS:
                pending.append(products(hd + HEAD_SKEW))
            finish(hd, *pending.pop(0))
        return carry

    @pl.when(bounded)
    def _():
        lax.fori_loop(0, n_chunks, factored_chunk, 0, unroll=FACTORED_UNROLL)

    @pl.when(jnp.logical_not(bounded))
    def _():
        lax.fori_loop(0, n_chunks, tree_chunk, 0)

    y_ref[0] = x_ref[0] + _dot(o_ref[...], wout_ref[...])

    @pl.when(j == last_j)
    def _():
        for hd in range(C_HEADS):
            st_ref[0, hd] = stt_ref[hd].T


def _hgrn_prompt(x, nm, w_in, clb, on, w_out, state_s, f_s, q_s, i_s, g_s):
    nb, seq, _ = x.shape
    nj = seq // TQH
    ns = state_s.shape[0]
    sbh = f_s.shape[1]
    assert f_s.shape[0] == nb * nj and sbh * nb * nj == ns
    blk = lambda b, j: (b, j, 0)
    step = lambda b, j: (b * nj + j, 0, 0)
    step4 = lambda b, j: (b * nj + j, 0, 0, 0)
    y, st, st_s, o_s = pl.pallas_call(
        _hgrn_prompt_kernel,
        grid=(nb, nj),
        in_specs=[
            pl.BlockSpec((1, TQH, D_MODEL), blk),
            _full((1, D_MODEL)),
            _resident((D_MODEL, C_IN)),
            _full((DEPTH, C_F)),
            _full((1, C_VAL_DIM)),
            _resident((C_V, D_MODEL)),
            _full((CHUNK, CHUNK)),
            pl.BlockSpec((sbh, C_HEADS, C_KEY_DIM, C_VAL_DIM), step4),
            pl.BlockSpec((None, sbh, C_F), step),
            pl.BlockSpec((None, sbh, C_F), step),
            pl.BlockSpec((None, sbh, C_V), step),
            pl.BlockSpec((None, sbh, C_V), step),
        ],
        out_specs=[
            pl.BlockSpec((1, TQH, D_MODEL), blk),
            pl.BlockSpec((1, C_HEADS, C_KEY_DIM, C_VAL_DIM), lambda b, j: (b, 0, 0, 0)),
            pl.BlockSpec((sbh, C_HEADS, C_KEY_DIM, C_VAL_DIM), step4),
            pl.BlockSpec((None, sbh, C_V), step),
        ],
        out_shape=[
            jax.ShapeDtypeStruct((nb, seq, D_MODEL), F32),
            jax.ShapeDtypeStruct((nb, C_HEADS, C_KEY_DIM, C_VAL_DIM), F32),
            jax.ShapeDtypeStruct(state_s.shape, F32),
            jax.ShapeDtypeStruct((ns // sbh, sbh, C_V), F32),
        ],
        scratch_shapes=[
            pltpu.VMEM((TQH, C_IN), F32),
            pltpu.VMEM((TQH, C_V), BF16),
            pltpu.VMEM((C_HEADS, C_VAL_DIM, C_KEY_DIM), F32),
            pltpu.VMEM((TQH, C_F), F32),
        ],
        compiler_params=pltpu.CompilerParams(
            dimension_semantics=("arbitrary", "arbitrary"), vmem_limit_bytes=VMEM_LIMIT),
        name="hgrn_prompt",
    )(x, nm, w_in, clb, on, w_out, jnp.asarray(_pair_level_table()),
      state_s, f_s, q_s, i_s, g_s)
    return y, st, st_s, o_s.reshape(ns, C_V)


def _ab_sample_proj_kernel(w00_ref, b0_ref, x_ref, nm_ref, win_ref, qn_ref, kn_ref, lng_ref, lnb_ref,
                           qx_ref, knew_ref, vnew_ref, bm_ref, gv_ref, wbf_ref):
    n = x_ref.shape[0]
    wbf_ref[...] = win_ref[...].astype(BF16)
    h = _rms(x_ref[...], nm_ref[...]).astype(BF16)
    z = _dot(h, wbf_ref[...])
    zeros = jnp.zeros((n, A_HEAD_DIM), F32)
    for hh in range(A_HEADS):
        qh = _rms(z[:, hh * A_HEAD_DIM:(hh + 1) * A_HEAD_DIM], qn_ref[...]) * ATTN_SCALE
        qx_ref[hh] = jnp.concatenate([qh, zeros] if hh // A_GROUP == 0 else [zeros, qh], axis=-1)
    kparts = []
    for g in range(A_KV_HEADS):
        kparts.append(_rms(z[:, A_Q + g * A_HEAD_DIM:A_Q + (g + 1) * A_HEAD_DIM], kn_ref[...]))
    knew_ref[...] = jnp.concatenate(kparts, axis=-1)
    vnew_ref[...] = z[:, A_Q + A_KV:A_Q + 2 * A_KV]

    u = _gelu(z[:, A_Q + 2 * A_KV:A_Q + 2 * A_KV + B_WIDTH])
    vln = _layernorm(_gelu(z[:, A_Q + 2 * A_KV + B_WIDTH:AB_IN]), lng_ref[...], lnb_ref[...])
    grp = lax.broadcasted_iota(jnp.int32, (1, B_WIDTH), 1) // B_GROUP_DIM
    srow = jnp.zeros((1, B_WIDTH), F32)
    brow = jnp.zeros((1, B_WIDTH), F32)
    for g in range(B_GROUPS):
        srow = jnp.where(grp == g, w00_ref[g], srow)
        brow = jnp.where(grp == g, b0_ref[g], brow)
    bm_ref[...] = u * (vln * srow + brow)
    gv_ref[...] = vln


def _ab_sample_proj(x, nm, w_in, qn, kn, lng, lnb, w00, b0):
    n = x.shape[0]
    return pl.pallas_call(
        _ab_sample_proj_kernel,
        in_specs=[_SMEM, _SMEM] + [pl.BlockSpec(memory_space=pltpu.VMEM)] * 7,
        out_shape=[
            jax.ShapeDtypeStruct((A_HEADS, n, A_KV), F32),
            jax.ShapeDtypeStruct((n, A_KV), F32),
            jax.ShapeDtypeStruct((n, A_KV), F32),
            jax.ShapeDtypeStruct((n, B_WIDTH), F32),
            jax.ShapeDtypeStruct((n, B_WIDTH), F32),
            jax.ShapeDtypeStruct(w_in.shape, BF16),
        ],
        compiler_params=pltpu.CompilerParams(vmem_limit_bytes=VMEM_LIMIT),
        name="ab_sample_proj",
    )(w00, b0, x, nm, w_in, qn, kn, lng, lnb)


def _hgrn_sample_proj_kernel(x_ref, nm_ref, win_ref, clb_ref, q_ref, f_ref, i_ref, sg_ref, wbf_ref):
    wbf_ref[...] = win_ref[...].astype(BF16)
    h = _rms(x_ref[...], nm_ref[...]).astype(BF16)
    z = _dot(h, wbf_ref[...])
    lb = _lower_bound(clb_ref[...])

    def put(ref, val):
        per = ref.shape[1]
        for i in range(ref.shape[0]):
            ref[i] = val[i * per:(i + 1) * per, :]

    put(q_ref, z[:, 0:C_F])
    put(f_ref, lb + (1.0 - lb) * jax.nn.sigmoid(z[:, C_F:2 * C_F]))
    put(i_ref, z[:, 2 * C_F:2 * C_F + C_V])
    put(sg_ref, jax.nn.sigmoid(z[:, 2 * C_F + C_V:C_IN]))


def _hgrn_sample_proj(x, nm, w_in, clb, per_step):
    n = x.shape[0]
    return pl.pallas_call(
        _hgrn_sample_proj_kernel,
        out_shape=[jax.ShapeDtypeStruct((n // per_step, per_step, C_F), F32)] * 2
        + [jax.ShapeDtypeStruct((n // per_step, per_step, C_V), F32)] * 2
        + [jax.ShapeDtypeStruct(w_in.shape, BF16)],
        compiler_params=pltpu.CompilerParams(vmem_limit_bytes=VMEM_LIMIT),
        name="hgrn_sample_proj",
    )(x, nm, w_in, clb)


def kernel(x_prompt, x_sample, cache_k, cache_v, state_hgrn, norm_mix, norm_ffn, w_in_ab, w_out_ab,
           q_norm, k_norm, attn_sink, rel_bias, gmlp_ln_g, gmlp_ln_b, gmlp_w_s, gmlp_b_s,
           w_in_c, c_lower_bounds, c_out_norm, w_out_c, w_gate, w_up, w_down):
    assert norm_mix.shape[0] == DEPTH == 2 and w_in_ab.shape[0] == 1 and w_in_c.shape[0] == 1
    nb, seq, _ = x_prompt.shape
    ns = x_sample.shape[0]
    assert x_sample.shape[1] == 1 and cache_k.shape[2] == WINDOW

    row = lambda v: v.reshape(1, -1)
    bf = lambda w: w.astype(BF16)
    w_out_ab0, w_out_c0 = bf(w_out_ab[0]), bf(w_out_c[0])
    nm, nf = norm_mix, norm_ffn
    qn, kn = row(q_norm[0]), row(k_norm[0])
    lng, lnb = row(gmlp_ln_g[0]), row(gmlp_ln_b[0])
    sink = attn_sink[0]

    tab, tabp = _bias_table(rel_bias)

    xs = x_sample.reshape(ns, D_MODEL)
    qx, knew_s, vnew_s, bm_s, gv_s, w_in_ab0 = _ab_sample_proj(
        xs, row(nm[0]), w_in_ab[0], qn, kn, lng, lnb, gmlp_w_s[0, :, 0, 0], gmlp_b_s[0, :, 0])
    sb = jnp.pad(tab[:, WINDOW - 1, WINDOW - 1:], ((0, 0), (0, WINDOW - 1)))
    to_t = lambda c: c.transpose(0, 2, 3, 1).reshape(ns, A_KV, WINDOW)
    xp, knew_p, vnew_p, gv_p, nk_s, nv_s, om = _ab_prompt(
        x_prompt, row(nm[0]), w_in_ab0, jnp.tile(qn, (1, A_HEADS)), jnp.tile(kn, (1, A_KV_HEADS)),
        sink, tabp, lng, lnb, gmlp_w_s[0], jnp.repeat(gmlp_b_s[0].T, B_GROUP_DIM, axis=1), w_out_ab0,
        to_t(cache_k[0]), to_t(cache_v[0]), qx.transpose(1, 0, 2), knew_s, vnew_s, sb)
    om = om.reshape(ns, A_KV_HEADS, A_GROUP, A_KV_HEADS, A_HEAD_DIM)
    a_s = jnp.stack([om[:, g, :, g, :] for g in range(A_KV_HEADS)], axis=1).reshape(ns, A_Q)
    xp, xs = _ffn(xp.reshape(nb * seq, D_MODEL), xs, jnp.concatenate([a_s, bm_s], axis=-1), w_out_ab0,
                  row(nf[0]), w_gate, w_up, w_down, 0)

    q_s, f_s, i_s, sg_s, w_in_c0 = _hgrn_sample_proj(xs, row(nm[1]), w_in_c[0], c_lower_bounds,
                                                     ns // (nb * (seq // TQH)))
    xp, st_p, st_s, o_s = _hgrn_prompt(xp.reshape(nb, seq, D_MODEL), row(nm[1]), w_in_c0, c_lower_bounds,
                                       row(c_out_norm[0]), w_out_c0, state_hgrn[0], f_s, q_s, i_s, sg_s)
    xp, xs = _ffn(xp.reshape(nb * seq, D_MODEL), xs, o_s, w_out_c0, row(nf[1]), w_gate, w_up, w_down, 1)

    kv5 = lambda a: a.reshape(1, a.shape[0], WINDOW, A_KV_HEADS, A_HEAD_DIM)
    from_t = lambda a: a.reshape(ns, A_KV_HEADS, A_HEAD_DIM, WINDOW).transpose(0, 3, 1, 2)[None]
    return (xp.reshape(nb, seq, D_MODEL), xs.reshape(ns, 1, D_MODEL),
            kv5(knew_p), kv5(vnew_p), from_t(nk_s), from_t(nv_s),
            gv_p[None], gv_s.reshape(1, ns, 1, B_WIDTH),
            st_p[None], st_s[None])
```

```python
import functools
import math

import jax
import jax.numpy as jnp
import numpy as np
from jax import lax
from jax.experimental import pallas as pl
from jax.experimental.pallas import tpu as pltpu

F32 = jnp.float32
BF16 = jnp.bfloat16

D_MODEL = 1024
DEPTH = 2
A_HEADS = 8
A_KV_HEADS = 2
A_GROUP = A_HEADS // A_KV_HEADS
A_HEAD_DIM = 64
WINDOW = 128
ATTN_SCALE = A_HEAD_DIM ** -0.5
NUM_BUCKETS = 32
MAX_DISTANCE = 128
A_Q = A_HEADS * A_HEAD_DIM
A_KV = A_KV_HEADS * A_HEAD_DIM
B_GROUPS = 8
B_GROUP_DIM = 64
B_WIDTH = B_GROUPS * B_GROUP_DIM
B_CHUNK = 128
AB_IN = A_Q + 2 * A_KV + 2 * B_WIDTH
AB_MIX = A_Q + B_WIDTH
C_HEADS = 8
C_KEY_DIM = 128
C_VAL_DIM = 128
C_F = C_HEADS * C_KEY_DIM
C_V = C_HEADS * C_VAL_DIM
C_IN = 2 * C_F + 2 * C_V
D_FF = 2816
EPS = 1e-6

NEG = -1e30

VMEM_LIMIT = 56 * 1024 * 1024
VREG_ROWS = 8

CHUNK = 128
TQ = 512
TQH = 512
TM = 512
HEAD_SKEW = 2
SAFE_LOG2_RANGE = 64.0
PREFIX_GROUP = 4
FACTORED_UNROLL = 4
AB_UNROLL = 4

_NT = (((1,), (1,)), ((), ()))
_TN = (((0,), (0,)), ((), ()))


def _rms(x, g):
    return x * lax.rsqrt(jnp.mean(x * x, axis=-1, keepdims=True) + EPS) * g


def _gelu(x):
    return 0.5 * x * (1.0 + lax.erf(x * math.sqrt(0.5)))


def _layernorm(x, g, b):
    xc = x - jnp.mean(x, axis=-1, keepdims=True)
    return xc * lax.rsqrt(jnp.mean(xc * xc, axis=-1, keepdims=True) + EPS) * g + b


def _dot(a, b):
    return jnp.dot(a, b, preferred_element_type=F32)


def _full(shape):
    n = len(shape)
    return pl.BlockSpec(shape, lambda *_: (0,) * n)


def _resident(shape):
    n = len(shape)
    return pl.BlockSpec(shape, lambda *_: (0,) * n, pipeline_mode=pl.Buffered(1))


_SMEM = pl.BlockSpec(memory_space=pltpu.SMEM)


def _bias_table_kernel(rel_ref, tab_ref, tabp_ref):
    qi = lax.broadcasted_iota(jnp.int32, (WINDOW, 2 * WINDOW), 0)
    kj = lax.broadcasted_iota(jnp.int32, (WINDOW, 2 * WINDOW), 1)
    dist = qi + WINDOW - kj
    ok = (dist >= 0) & (dist < WINDOW)
    max_exact = NUM_BUCKETS // 2
    d = jnp.maximum(dist, 0)
    dl = jnp.maximum(d, 1).astype(F32)
    v = (jnp.log(dl / max_exact) / math.log(MAX_DISTANCE / max_exact) * (NUM_BUCKETS - max_exact))
    far = d >= max_exact
    hits = []
    for b in range(NUM_BUCKETS):
        if b < max_exact:
            hits.append(d == b)
        elif b < NUM_BUCKETS - 1:
            hits.append(far & (v >= b - max_exact) & (v < b - max_exact + 1))
        else:
            hits.append(far & (v >= b - max_exact))
    for h in range(A_HEADS):
        acc = jnp.zeros((WINDOW, 2 * WINDOW), F32)
        for b in range(NUM_BUCKETS):
            acc = jnp.where(hits[b], rel_ref[b, h], acc)
        t = jnp.where(ok, acc, NEG)
        tab_ref[h] = t
        cols = slice((h % 2) * 2 * WINDOW, (h % 2 + 1) * 2 * WINDOW)
        tabp_ref[0, h // 2, :, cols] = t
        tabp_ref[1, h // 2, :, cols] = jnp.where(kj < WINDOW, NEG, t)


def _bias_table(rel_bias):
    return pl.pallas_call(
        _bias_table_kernel,
        out_shape=[
            jax.ShapeDtypeStruct((A_HEADS, WINDOW, 2 * WINDOW), F32),
            jax.ShapeDtypeStruct((2, A_HEADS // 2, WINDOW, 4 * WINDOW), F32),
        ],
        in_specs=[_SMEM],
        name="bias_table",
    )(rel_bias)


PAIR = 2 * A_HEAD_DIM
N_PAIRS = A_HEADS // 2


def _sample_cache_attention(ck_ref, cv_ref, qx_ref, kn_ref, vn_ref, sb_ref, sink_ref,
                            nk_ref, nv_ref, om_ref):
    wb = ck_ref.shape[2]
    head = lax.broadcasted_iota(jnp.int32, (1, A_HEADS, A_KV), 1)
    lane = lax.broadcasted_iota(jnp.int32, (1, A_HEADS, A_KV), 2)
    own_group = (head // A_GROUP) == (lane // A_HEAD_DIM)
    newest = lax.broadcasted_iota(jnp.int32, (1, 1, wb), 2) == wb - 1
    sink = sink_ref[...][None]
    kc, vc = ck_ref[...], cv_ref[...]
    kn, vn = kn_ref[...], vn_ref[...]
    kn_cols, vn_cols = kn_ref[:, 0, :].T, vn_ref[:, 0, :].T
    for i in range(kc.shape[0]):
        nk_ref[i] = jnp.where(newest[0], kn_cols[:, i:i + 1], pltpu.roll(kc[i], wb - 1, 1))
        nv_ref[i] = jnp.where(newest[0], vn_cols[:, i:i + 1], pltpu.roll(vc[i], wb - 1, 1))
    q = qx_ref[...]
    s = jnp.einsum('bhd,bdk->bhk', q.astype(BF16), kc.astype(BF16), preferred_element_type=F32)
    s = s + sb_ref[:, 0:wb][None]
    sn = jnp.sum(q * kn, axis=-1, keepdims=True) + sb_ref[:, wb:wb + 1][None]
    m = jnp.maximum(jnp.maximum(jnp.max(s, axis=-1, keepdims=True), sn), sink)
    e = jnp.exp(s - m)
    en = jnp.exp(sn - m)
    r = 1.0 / (jnp.sum(e, axis=-1, keepdims=True) + en + jnp.exp(sink - m))
    o = jnp.einsum('bhk,bdk->bhd', (e * r).astype(BF16), vc.astype(BF16),
                   preferred_element_type=F32) + (en * r) * vn
    om_ref[...] = jnp.where(own_group, o, 0.0)


def _ab_prompt_kernel(sink_ref, x_ref, nm_ref, win_ref, qg_ref, kg_ref, tabp_ref, lng_ref, lnb_ref,
                      ws_ref, bsp_ref, wout_ref,
                      ck_ref, cv_ref, qx_ref, kn_ref, vn_ref, sb_ref, sinkc_ref,
                      y_ref, knew_ref, vnew_ref, gv_ref, nk_ref, nv_ref, om_ref,
                      z_ref, mix_ref, q_ref, k_ref, kr_ref, v_ref, vr_ref, wpair_ref, kl_ref, vl_ref, gl_ref):
    j = pl.program_id(1)
    last_j = pl.num_programs(1) - 1
    n_chunks = TQ // CHUNK

    @pl.when(j == 0)
    def _():
        for ref in (k_ref, kr_ref, v_ref, vr_ref):
            ref[0:CHUNK, :] = jnp.zeros((CHUNK, A_KV), BF16)
        row = lax.broadcasted_iota(jnp.int32, (B_CHUNK, B_CHUNK), 0)
        col = lax.broadcasted_iota(jnp.int32, (B_CHUNK, B_CHUNK), 1)
        for g in range(B_GROUPS):
            wpair_ref[g // 2, :, (g % 2) * B_CHUNK:(g % 2 + 1) * B_CHUNK] = jnp.where(
                row >= col, ws_ref[g], 0.0).astype(BF16)

    h = _rms(x_ref[0], nm_ref[...]).astype(BF16)
    z_ref[...] = _dot(h, win_ref[...])

    _sample_cache_attention(ck_ref, cv_ref, qx_ref, kn_ref, vn_ref, sb_ref, sinkc_ref,
                            nk_ref, nv_ref, om_ref)

    lo_half = lax.broadcasted_iota(jnp.int32, (1, PAIR), 1) < A_HEAD_DIM

    def mean_sq_halves(x):
        x2 = x * x
        lo_sum = jnp.sum(jnp.where(lo_half, x2, 0.0), axis=-1, keepdims=True)
        hi_sum = jnp.sum(jnp.where(lo_half, 0.0, x2), axis=-1, keepdims=True)
        return jnp.where(lo_half, lo_sum, hi_sum) * (1.0 / A_HEAD_DIM)

    def block_diag(top, bot):
        zero = jnp.zeros_like(top)
        return jnp.concatenate([jnp.where(lo_half, top, zero), jnp.where(lo_half, zero, bot)], axis=0)

    kraw = z_ref[:, A_Q:A_Q + A_KV]
    v_all = z_ref[:, A_Q + A_KV:A_Q + 2 * A_KV]
    kn_all = kraw * lax.rsqrt(mean_sq_halves(kraw) + EPS) * kg_ref[...]
    k_ref[CHUNK:, :] = kn_all.astype(BF16)
    kr_ref[CHUNK:, :] = pltpu.roll(kn_all, A_HEAD_DIM, 1).astype(BF16)
    v_ref[CHUNK:, :] = v_all.astype(BF16)
    vr_ref[CHUNK:, :] = pltpu.roll(v_all, A_HEAD_DIM, 1).astype(BF16)
    kl_ref[...] = kn_all[TQ - CHUNK:, :]
    vl_ref[...] = v_all[TQ - CHUNK:, :]
    for i in range(N_PAIRS):
        ps = slice(i * PAIR, (i + 1) * PAIR)
        qraw = z_ref[:, ps]
        qn = qraw * lax.rsqrt(mean_sq_halves(qraw) + EPS) * (qg_ref[:, ps] * ATTN_SCALE)
        q_ref[:, ps] = qn.astype(BF16)

    def chunk(c, carry):
        r0 = pl.multiple_of(c * CHUNK, CHUNK)
        rows = pl.ds(r0, CHUNK)
        first = jnp.where(jnp.logical_and(j == 0, c == 0), 1, 0)

        both = pl.ds(r0, 2 * CHUNK)
        k2, k2r, v2, v2r = k_ref[both, :], kr_ref[both, :], v_ref[both, :], vr_ref[both, :]
        kbd = [block_diag(k2, k2r), block_diag(k2r, k2)]
        vbd = [block_diag(v2, v2r), block_diag(v2r, v2)]

        scores = []
        for i in range(N_PAIRS):
            s = lax.dot_general(q_ref[rows, i * PAIR:(i + 1) * PAIR], kbd[i // (A_GROUP // 2)], _NT,
                                preferred_element_type=F32)
            scores.append(s + tabp_ref[first, i])
        outs = []
        for i in range(N_PAIRS):
            es, rs = [], []
            for hh in range(2):
                sh = scores[i][:, hh * 2 * WINDOW:(hh + 1) * 2 * WINDOW]
                sk = sink_ref[2 * i + hh]
                m = jnp.maximum(jnp.max(sh, axis=-1, keepdims=True), sk)
                e = jnp.exp(sh - m)
                rs.append(1.0 / (jnp.sum(e, axis=-1, keepdims=True) + jnp.exp(sk - m)))
                es.append(e.astype(BF16))
            o = _dot(jnp.concatenate(es, axis=-1), vbd[i // (A_GROUP // 2)])
            outs.append(o * jnp.where(lo_half, rs[0], rs[1]))
        mix_ref[rows, 0:A_Q] = jnp.concatenate(outs, axis=-1).astype(BF16)

        zu = z_ref[rows, A_Q + 2 * A_KV:A_Q + 2 * A_KV + B_WIDTH]
        zv = z_ref[rows, A_Q + 2 * A_KV + B_WIDTH:AB_IN]
        u = _gelu(zu)
        vln = _layernorm(_gelu(zv), lng_ref[...], lnb_ref[...])
        vlb = vln.astype(BF16)
        sparts = []
        for i in range(B_GROUPS // 2):
            vpair = vlb[:, i * PAIR:(i + 1) * PAIR]
            sparts.append(_dot(wpair_ref[i], block_diag(vpair, vpair)))
        bm = u * (jnp.concatenate(sparts, axis=-1) + bsp_ref[...])
        mix_ref[rows, A_Q:AB_MIX] = bm.astype(BF16)

        gl_ref[...] = vln
        return carry

    lax.fori_loop(0, n_chunks, chunk, 0, unroll=AB_UNROLL)
    y_ref[0] = x_ref[0] + _dot(mix_ref[...], wout_ref[...])
    for ref in (k_ref, kr_ref, v_ref, vr_ref):
        ref[0:CHUNK, :] = ref[TQ:TQ + CHUNK, :]

    @pl.when(j == last_j)
    def _():
        knew_ref[0] = kl_ref[...]
        vnew_ref[0] = vl_ref[...]
        gv_ref[0] = gl_ref[...]


def _ab_prompt(x, nm, w_in, qg, kg, sink, tabp, lng, lnb, w_s, bsp, w_out, ck, cv, qx, kn_s, vn_s, sb):
    nb, seq, _ = x.shape
    nj = seq // TQ
    grid = (nb, nj)
    ns, _, wb = ck.shape
    sba = ns // (nb * nj)
    assert sba * nb * nj == ns
    blk = lambda b, j: (b, j, 0)
    per_b = lambda b, j: (b, 0, 0)
    step = lambda b, j: (b * nj + j, 0, 0)
    return pl.pallas_call(
        _ab_prompt_kernel,
        grid=grid,
        in_specs=[
            _SMEM,
            pl.BlockSpec((1, TQ, D_MODEL), blk),
            _full((1, D_MODEL)),
            _resident((D_MODEL, AB_IN)),
            _full((1, A_Q)),
            _full((1, A_KV)),
            _resident((2, N_PAIRS, WINDOW, 4 * WINDOW)),
            _full((1, B_WIDTH)),
            _full((1, B_WIDTH)),
            _resident((B_GROUPS, B_CHUNK, B_CHUNK)),
            _resident((B_CHUNK, B_WIDTH)),
            _resident((AB_MIX, D_MODEL)),
            pl.BlockSpec((sba, A_KV, wb), step),
            pl.BlockSpec((sba, A_KV, wb), step),
            pl.BlockSpec((sba, A_HEADS, A_KV), step),
            pl.BlockSpec((sba, 1, A_KV), step),
            pl.BlockSpec((sba, 1, A_KV), step),
            _full((A_HEADS, 2 * WINDOW)),
            _full((A_HEADS, 1)),
        ],
        out_specs=[
            pl.BlockSpec((1, TQ, D_MODEL), blk),
            pl.BlockSpec((1, WINDOW, A_KV), per_b),
            pl.BlockSpec((1, WINDOW, A_KV), per_b),
            pl.BlockSpec((1, B_CHUNK, B_WIDTH), per_b),
            pl.BlockSpec((sba, A_KV, wb), step),
            pl.BlockSpec((sba, A_KV, wb), step),
            pl.BlockSpec((sba, A_HEADS, A_KV), step),
        ],
        out_shape=[
            jax.ShapeDtypeStruct((nb, seq, D_MODEL), F32),
            jax.ShapeDtypeStruct((nb, WINDOW, A_KV), F32),
            jax.ShapeDtypeStruct((nb, WINDOW, A_KV), F32),
            jax.ShapeDtypeStruct((nb, B_CHUNK, B_WIDTH), F32),
            jax.ShapeDtypeStruct((ns, A_KV, wb), F32),
            jax.ShapeDtypeStruct((ns, A_KV, wb), F32),
            jax.ShapeDtypeStruct((ns, A_HEADS, A_KV), F32),
        ],
        scratch_shapes=[
            pltpu.VMEM((TQ, AB_IN), F32),
            pltpu.VMEM((TQ, AB_MIX), BF16),
            pltpu.VMEM((TQ, A_Q), BF16),
            pltpu.VMEM((CHUNK + TQ, A_KV), BF16),
            pltpu.VMEM((CHUNK + TQ, A_KV), BF16),
            pltpu.VMEM((CHUNK + TQ, A_KV), BF16),
            pltpu.VMEM((CHUNK + TQ, A_KV), BF16),
            pltpu.VMEM((B_GROUPS // 2, B_CHUNK, 2 * B_CHUNK), BF16),
            pltpu.VMEM((WINDOW, A_KV), F32),
            pltpu.VMEM((WINDOW, A_KV), F32),
            pltpu.VMEM((B_CHUNK, B_WIDTH), F32),
        ],
        compiler_params=pltpu.CompilerParams(
            dimension_semantics=("arbitrary", "arbitrary"), vmem_limit_bytes=VMEM_LIMIT),
        name="ab_prompt",
    )(sink, x, nm, w_in, qg, kg, tabp, lng, lnb, w_s, bsp, w_out,
      ck, cv, qx, kn_s[:, None, :], vn_s[:, None, :], sb, sink.reshape(A_HEADS, 1))


FF_TILE = 256


def _ffn_kernel(xp_ref, xs_ref, ms_ref, wo_ref, g_ref, wg_ref, wu_ref, wd_ref, yp_ref, ys_ref,
                wg_s, wu_s, wd_s, h0_s, acc_s, *, n_cast, n_prompt):
    s = pl.program_id(0)

    def gated(h, wg, wu):
        gate = _dot(h, wg)
        return (gate * jax.nn.sigmoid(gate) * _dot(h, wu)).astype(BF16)

    @pl.when(s == 0)
    def _():
        x = xp_ref[...]
        h0_s[...] = _rms(x, g_ref[...]).astype(BF16)
        acc_s[...] = x

    for c in range(n_cast):
        @pl.when(s == c)
        def _(c=c):
            tile = slice(c * FF_TILE, (c + 1) * FF_TILE)
            wg_t, wu_t, wd_t = (r[...].astype(BF16) for r in (wg_ref, wu_ref, wd_ref))
            wg_s[:, tile] = wg_t
            wu_s[:, tile] = wu_t
            wd_s[tile, :] = wd_t
            acc_s[...] += _dot(gated(h0_s[...], wg_t, wu_t), wd_t)

    @pl.when(s == n_cast - 1)
    def _():
        yp_ref[...] = acc_s[...]

    def swiglu(x):
        h = _rms(x, g_ref[...]).astype(BF16)
        return x + _dot(gated(h, wg_s[...], wu_s[...]), wd_s[...])

    @pl.when(jnp.logical_and(s >= n_cast, s < n_cast + n_prompt - 1))
    def _():
        yp_ref[...] = swiglu(xp_ref[...])

    @pl.when(s == n_cast + n_prompt - 1)
    def _():
        ys_ref[...] = swiglu(xs_ref[...] + _dot(ms_ref[...].astype(BF16), wo_ref[...]))


def _ffn(xp, xs, mix_s, w_o, g, w_gate, w_up, w_down, layer):
    rows, ns = xp.shape[0], xs.shape[0]
    n_cast, n_prompt = D_FF // FF_TILE, rows // TM
    w_tile = lambda s: jnp.minimum(s, n_cast - 1)
    row_blk = lambda s: (jnp.clip(s - (n_cast - 1), 0, n_prompt - 1), 0)
    return pl.pallas_call(
        functools.partial(_ffn_kernel, n_cast=n_cast, n_prompt=n_prompt),
        grid=(n_cast + n_prompt,),
        in_specs=[
            pl.BlockSpec((TM, D_MODEL), row_blk),
            _full((ns, D_MODEL)),
            _full((ns, D_MODEL)),
            _resident((D_MODEL, D_MODEL)),
            _full((1, D_MODEL)),
            pl.BlockSpec((None, D_MODEL, FF_TILE), lambda s: (layer, 0, w_tile(s))),
            pl.BlockSpec((None, D_MODEL, FF_TILE), lambda s: (layer, 0, w_tile(s))),
            pl.BlockSpec((None, FF_TILE, D_MODEL), lambda s: (layer, w_tile(s), 0)),
        ],
        out_specs=[pl.BlockSpec((TM, D_MODEL), row_blk), _full((ns, D_MODEL))],
        out_shape=[jax.ShapeDtypeStruct((rows, D_MODEL), F32), jax.ShapeDtypeStruct((ns, D_MODEL), F32)],
        scratch_shapes=[
            pltpu.VMEM((D_MODEL, D_FF), BF16),
            pltpu.VMEM((D_MODEL, D_FF), BF16),
            pltpu.VMEM((D_FF, D_MODEL), BF16),
            pltpu.VMEM((TM, D_MODEL), BF16),
            pltpu.VMEM((TM, D_MODEL), F32),
        ],
        compiler_params=pltpu.CompilerParams(
            dimension_semantics=("arbitrary",), vmem_limit_bytes=VMEM_LIMIT),
        name="ffn",
    )(xp, xs, mix_s, w_o, g, w_gate, w_up, w_down)


def _lower_bound(clb):
    m = jnp.max(clb, axis=0, keepdims=True)
    e = jnp.exp(clb - m)
    sm = e / jnp.sum(e, axis=0, keepdims=True)
    return (sm[0:1] + sm[1:2]) - sm[0:1]


def _split3(x):
    hi = x.astype(BF16)
    r = x - hi.astype(F32)
    mid = r.astype(BF16)
    lo = (r - mid.astype(F32)).astype(BF16)
    return hi, mid, lo


def _neg_abs(x):
    return lax.bitcast_convert_type(
        lax.bitcast_convert_type(x, jnp.uint32) | jnp.uint32(0x80000000), F32)


def _pair_level_table():
    t = np.arange(CHUNK)[:, None]
    s = np.arange(CHUNK)[None, :]
    lev = np.floor(np.log2(np.maximum(t ^ s, 1))).astype(np.int32)
    lev = np.where(t == s, -1, lev)
    return np.where(s > t, -2, lev).astype(np.int32)


def _level_operand(p, q, kk, f, b2):
    m = 2 ** p
    if m < VREG_ROWS:
        shape3 = (CHUNK // VREG_ROWS, VREG_ROWS, q.shape[1])
        sub = lax.broadcasted_iota(jnp.int32, (1, VREG_ROWS, q.shape[1]), 1)
        upper = ((sub >> p) & 1) == 1
        q3, k3 = q.reshape(shape3), kk.reshape(shape3)
        if p == 0:
            y = jnp.where(upper, q3 * f.reshape(shape3), k3)
        else:
            b3 = b2.reshape(shape3)
            be = b3[:, m - 1:m, :]
            for k in range(1, VREG_ROWS // (2 * m)):
                be = jnp.where(sub >= 2 * m * k, b3[:, 2 * m * k + m - 1:2 * m * k + m, :], be)
            y = jnp.where(upper, q3, k3) * jnp.exp2(_neg_abs(b3 - be))
        return y.reshape(q.shape).astype(BF16)
    parts = []
    for k in range(CHUNK // (2 * m)):
        lo = slice(2 * m * k, 2 * m * k + m)
        up = slice(2 * m * k + m, 2 * m * (k + 1))
        be = b2[2 * m * k + m - 1:2 * m * k + m, :]
        parts.append(kk[lo] * jnp.exp2(be - b2[lo]))
        parts.append(q[up] * jnp.exp2(b2[up] - be))
    return jnp.concatenate(parts, axis=0).astype(BF16)


def _merge_level(p, att, pm, lev):
    m = 2 ** p
    if m < VREG_ROWS:
        return jnp.where(lev == p, pm, att)
    col = lax.broadcasted_iota(jnp.int32, (1, CHUNK), 1)
    parts = []
    for k in range(CHUNK // (2 * m)):
        lo = slice(2 * m * k, 2 * m * k + m)
        up = slice(2 * m * k + m, 2 * m * (k + 1))
        parts.append(att[lo])
        parts.append(jnp.where((col >= 2 * m * k) & (col < 2 * m * k + m), pm[up], att[up]))
    return jnp.concatenate(parts, axis=0)


def _hgrn_prompt_kernel(x_ref, nm_ref, win_ref, clb_ref, on_ref, wout_ref, lev_ref,
                        ss_ref, fs_ref, qs_ref, is_ref, gs_ref,
                        y_ref, st_ref, sso_ref, os_ref,
                        z_ref, o_ref, stt_ref, k_ref):
    j = pl.program_id(1)
    last_j = pl.num_programs(1) - 1
    n_chunks = TQH // CHUNK
    n_levels = int(math.log2(CHUNK))

    @pl.when(j == 0)
    def _():
        stt_ref[...] = jnp.zeros_like(stt_ref)

    h = _rms(x_ref[0], nm_ref[...]).astype(BF16)
    z_ref[...] = _dot(h, win_ref[...])

    head_cols = lambda ref: [ref[:, hd * C_KEY_DIM:(hd + 1) * C_KEY_DIM].T for hd in range(C_HEADS)]
    f_cols, q_cols = head_cols(fs_ref), head_cols(qs_ref)
    out_rows = []
    for smp in range(ss_ref.shape[0]):
        parts = []
        for hd in range(C_HEADS):
            hs = slice(hd * C_VAL_DIM, (hd + 1) * C_VAL_DIM)
            fb = jnp.broadcast_to(f_cols[hd][:, smp:smp + 1], (C_KEY_DIM, C_VAL_DIM))
            sn = fb * ss_ref[smp, hd] + (1.0 - fb) * is_ref[smp:smp + 1, hs]
            sso_ref[smp, hd] = sn
            o = jnp.sum(q_cols[hd][:, smp:smp + 1] * sn, axis=0, keepdims=True)
            parts.append(_rms(o, on_ref[...]))
        out_rows.append(jnp.concatenate(parts, axis=-1))
    os_ref[...] = jnp.concatenate(out_rows, axis=0) * gs_ref[...]

    lb = _lower_bound(clb_ref[...])

    row = lax.broadcasted_iota(jnp.int32, (CHUNK, CHUNK), 0)
    col = lax.broadcasted_iota(jnp.int32, (CHUNK, CHUNK), 1)
    ltri = (row >= col).astype(BF16)

    def chunk_rows(c):
        return pl.ds(pl.multiple_of(c * CHUNK, CHUNK), CHUNK)

    def prefix(g, worst):
        rows = [chunk_rows(g * PREFIX_GROUP + i) for i in range(PREFIX_GROUP)]
        gates = [z_ref[r, C_F:2 * C_F] for r in rows]
        for r, gate in zip(rows, gates):
            f_all = lb + (1.0 - lb) * jax.nn.sigmoid(gate)
            k_ref[r, :] = 1.0 - f_all
            hi, mid, lo = _split3(jnp.log2(f_all))
            b2 = (_dot(ltri, hi) + _dot(ltri, mid)) + _dot(ltri, lo)
            z_ref[r, C_F:2 * C_F] = b2
            b_mid = b2[CHUNK // 2 - 1:CHUNK // 2, :]
            b_last = b2[CHUNK - 1:CHUNK, :]
            worst = jnp.maximum(worst, jnp.maximum(-b_mid, b_mid - b_last))
        return worst

    worst = lax.fori_loop(0, n_chunks // PREFIX_GROUP, prefix, jnp.zeros((1, C_F), F32))
    bounded = jnp.max(worst) <= SAFE_LOG2_RANGE

    def finish_head(rows, hd, o):
        gt = z_ref[rows, 2 * C_F + C_V + hd * C_VAL_DIM:2 * C_F + C_V + (hd + 1) * C_VAL_DIM]
        o = _rms(o, on_ref[...]) * jax.nn.sigmoid(gt)
        o_ref[rows, hd * C_VAL_DIM:(hd + 1) * C_VAL_DIM] = o.astype(BF16)

    def head_inputs(rows, hd):
        q = z_ref[rows, hd * C_KEY_DIM:(hd + 1) * C_KEY_DIM]
        kk = k_ref[rows, hd * C_KEY_DIM:(hd + 1) * C_KEY_DIM]
        b2 = z_ref[rows, C_F + hd * C_KEY_DIM:C_F + (hd + 1) * C_KEY_DIM]
        ivb = z_ref[rows, 2 * C_F + hd * C_VAL_DIM:2 * C_F + (hd + 1) * C_VAL_DIM].astype(BF16)
        return q, kk, b2, ivb

    def factored_chunk(c, carry):
        rows = chunk_rows(c)
        for hd in range(C_HEADS):
            q, kk, b2, ivb = head_inputs(rows, hd)
            b_mid = b2[CHUNK // 2 - 1:CHUNK // 2, :]
            b_last = b2[CHUNK - 1:CHUNK, :]
            qs = (q * jnp.exp2(b2 - b_mid)).astype(BF16)
            kd = (kk * jnp.exp2(b_mid - b2)).astype(BF16)
            att = jnp.where(row >= col, lax.dot_general(qs, kd, _NT, preferred_element_type=F32), 0.0)
            stt = stt_ref[hd]
            o = lax.dot_general(qs, (stt * jnp.exp2(b_mid)).astype(BF16), _NT,
                                preferred_element_type=F32) + _dot(att.astype(BF16), ivb)
            stt_ref[hd] = stt * jnp.exp2(b_last) + jnp.exp2(b_last - b_mid) * lax.dot_general(
                ivb, kd, _TN, preferred_element_type=F32)
            finish_head(rows, hd, o)
        return carry

    def tree_chunk(c, carry):
        rows = chunk_rows(c)
        lev = lev_ref[...]

        def products(hd):
            q, kk, b2, ivb = head_inputs(rows, hd)
            diag = jnp.sum(q * kk, axis=-1, keepdims=True)
            pms = []
            for p in range(n_levels):
                y = _level_operand(p, q, kk, 1.0 - kk, b2)
                pms.append(lax.dot_general(y, y, _NT, preferred_element_type=F32))
            stt = stt_ref[hd]
            o_prev = lax.dot_general((q * jnp.exp2(b2)).astype(BF16), stt.astype(BF16), _NT,
                                     preferred_element_type=F32)
            b_last = b2[CHUNK - 1:CHUNK, :]
            kd = (kk * jnp.exp2(b_last - b2)).astype(BF16)
            stt_ref[hd] = stt * jnp.exp2(b_last) + lax.dot_general(
                ivb, kd, _TN, preferred_element_type=F32)
            return diag, pms, o_prev, ivb

        def finish(hd, diag, pms, o_prev, ivb):
            att = jnp.where(lev == -1, diag, 0.0)
            for p in range(n_levels):
                att = _merge_level(p, att, pms[p], lev)
            finish_head(rows, hd, o_prev + _dot(att.astype(BF16), ivb))

        pending = [products(hd) for hd in range(HEAD_SKEW)]
        for hd in range(C_HEADS):
            if hd + HEAD_SKEW < C_HEADS:
                pending.append(products(hd + HEAD_SKEW))
            finish(hd, *pending.pop(0))
        return carry

    @pl.when(bounded)
    def _():
        lax.fori_loop(0, n_chunks, factored_chunk, 0, unroll=FACTORED_UNROLL)

    @pl.when(jnp.logical_not(bounded))
    def _():
        lax.fori_loop(0, n_chunks, tree_chunk, 0)

    y_ref[0] = x_ref[0] + _dot(o_ref[...], wout_ref[...])

    @pl.when(j == last_j)
    def _():
        for hd in range(C_HEADS):
            st_ref[0, hd] = stt_ref[hd].T


def _hgrn_prompt(x, nm, w_in, clb, on, w_out, state_s, f_s, q_s, i_s, g_s):
    nb, seq, _ = x.shape
    nj = seq // TQH
    ns = state_s.shape[0]
    sbh = f_s.shape[1]
    assert f_s.shape[0] == nb * nj and sbh * nb * nj == ns
    blk = lambda b, j: (b, j, 0)
    step = lambda b, j: (b * nj + j, 0, 0)
    step4 = lambda b, j: (b * nj + j, 0, 0, 0)
    y, st, st_s, o_s = pl.pallas_call(
        _hgrn_prompt_kernel,
        grid=(nb, nj),
        in_specs=[
            pl.BlockSpec((1, TQH, D_MODEL), blk),
            _full((1, D_MODEL)),
            _resident((D_MODEL, C_IN)),
            _full((DEPTH, C_F)),
            _full((1, C_VAL_DIM)),
            _resident((C_V, D_MODEL)),
            _full((CHUNK, CHUNK)),
            pl.BlockSpec((sbh, C_HEADS, C_KEY_DIM, C_VAL_DIM), step4),
            pl.BlockSpec((None, sbh, C_F), step),
            pl.BlockSpec((None, sbh, C_F), step),
            pl.BlockSpec((None, sbh, C_V), step),
            pl.BlockSpec((None, sbh, C_V), step),
        ],
        out_specs=[
            pl.BlockSpec((1, TQH, D_MODEL), blk),
            pl.BlockSpec((1, C_HEADS, C_KEY_DIM, C_VAL_DIM), lambda b, j: (b, 0, 0, 0)),
            pl.BlockSpec((sbh, C_HEADS, C_KEY_DIM, C_VAL_DIM), step4),
            pl.BlockSpec((None, sbh, C_V), step),
        ],
        out_shape=[
            jax.ShapeDtypeStruct((nb, seq, D_MODEL), F32),
            jax.ShapeDtypeStruct((nb, C_HEADS, C_KEY_DIM, C_VAL_DIM), F32),
            jax.ShapeDtypeStruct(state_s.shape, F32),
            jax.ShapeDtypeStruct((ns // sbh, sbh, C_V), F32),
        ],
        scratch_shapes=[
            pltpu.VMEM((TQH, C_IN), F32),
            pltpu.VMEM((TQH, C_V), BF16),
            pltpu.VMEM((C_HEADS, C_VAL_DIM, C_KEY_DIM), F32),
            pltpu.VMEM((TQH, C_F), F32),
        ],
        compiler_params=pltpu.CompilerParams(
            dimension_semantics=("arbitrary", "arbitrary"), vmem_limit_bytes=VMEM_LIMIT),
        name="hgrn_prompt",
    )(x, nm, w_in, clb, on, w_out, jnp.asarray(_pair_level_table()),
      state_s, f_s, q_s, i_s, g_s)
    return y, st, st_s, o_s.reshape(ns, C_V)


def _ab_sample_proj_kernel(w00_ref, b0_ref, x_ref, nm_ref, win_ref, qn_ref, kn_ref, lng_ref, lnb_ref,
                           qx_ref, knew_ref, vnew_ref, bm_ref, gv_ref, wbf_ref):
    n = x_ref.shape[0]
    wbf_ref[...] = win_ref[...].astype(BF16)
    h = _rms(x_ref[...], nm_ref[...]).astype(BF16)
    z = _dot(h, wbf_ref[...])
    zeros = jnp.zeros((n, A_HEAD_DIM), F32)
    for hh in range(A_HEADS):
        qh = _rms(z[:, hh * A_HEAD_DIM:(hh + 1) * A_HEAD_DIM], qn_ref[...]) * ATTN_SCALE
        qx_ref[hh] = jnp.concatenate([qh, zeros] if hh // A_GROUP == 0 else [zeros, qh], axis=-1)
    kparts = []
    for g in range(A_KV_HEADS):
        kparts.append(_rms(z[:, A_Q + g * A_HEAD_DIM:A_Q + (g + 1) * A_HEAD_DIM], kn_ref[...]))
    knew_ref[...] = jnp.concatenate(kparts, axis=-1)
    vnew_ref[...] = z[:, A_Q + A_KV:A_Q + 2 * A_KV]

    u = _gelu(z[:, A_Q + 2 * A_KV:A_Q + 2 * A_KV + B_WIDTH])
    vln = _layernorm(_gelu(z[:, A_Q + 2 * A_KV + B_WIDTH:AB_IN]), lng_ref[...], lnb_ref[...])
    grp = lax.broadcasted_iota(jnp.int32, (1, B_WIDTH), 1) // B_GROUP_DIM
    srow = jnp.zeros((1, B_WIDTH), F32)
    brow = jnp.zeros((1, B_WIDTH), F32)
    for g in range(B_GROUPS):
        srow = jnp.where(grp == g, w00_ref[g], srow)
        brow = jnp.where(grp == g, b0_ref[g], brow)
    bm_ref[...] = u * (vln * srow + brow)
    gv_ref[...] = vln


def _ab_sample_proj(x, nm, w_in, qn, kn, lng, lnb, w00, b0):
    n = x.shape[0]
    return pl.pallas_call(
        _ab_sample_proj_kernel,
        in_specs=[_SMEM, _SMEM] + [pl.BlockSpec(memory_space=pltpu.VMEM)] * 7,
        out_shape=[
            jax.ShapeDtypeStruct((A_HEADS, n, A_KV), F32),
            jax.ShapeDtypeStruct((n, A_KV), F32),
            jax.ShapeDtypeStruct((n, A_KV), F32),
            jax.ShapeDtypeStruct((n, B_WIDTH), F32),
            jax.ShapeDtypeStruct((n, B_WIDTH), F32),
            jax.ShapeDtypeStruct(w_in.shape, BF16),
        ],
        compiler_params=pltpu.CompilerParams(vmem_limit_bytes=VMEM_LIMIT),
        name="ab_sample_proj",
    )(w00, b0, x, nm, w_in, qn, kn, lng, lnb)


def _hgrn_sample_proj_kernel(x_ref, nm_ref, win_ref, clb_ref, q_ref, f_ref, i_ref, sg_ref, wbf_ref):
    wbf_ref[...] = win_ref[...].astype(BF16)
    h = _rms(x_ref[...], nm_ref[...]).astype(BF16)
    z = _dot(h, wbf_ref[...])
    lb = _lower_bound(clb_ref[...])

    def put(ref, val):
        per = ref.shape[1]
        for i in range(ref.shape[0]):
            ref[i] = val[i * per:(i + 1) * per, :]

    put(q_ref, z[:, 0:C_F])
    put(f_ref, lb + (1.0 - lb) * jax.nn.sigmoid(z[:, C_F:2 * C_F]))
    put(i_ref, z[:, 2 * C_F:2 * C_F + C_V])
    put(sg_ref, jax.nn.sigmoid(z[:, 2 * C_F + C_V:C_IN]))


def _hgrn_sample_proj(x, nm, w_in, clb, per_step):
    n = x.shape[0]
    return pl.pallas_call(
        _hgrn_sample_proj_kernel,
        out_shape=[jax.ShapeDtypeStruct((n // per_step, per_step, C_F), F32)] * 2
        + [jax.ShapeDtypeStruct((n // per_step, per_step, C_V), F32)] * 2
        + [jax.ShapeDtypeStruct(w_in.shape, BF16)],
        compiler_params=pltpu.CompilerParams(vmem_limit_bytes=VMEM_LIMIT),
        name="hgrn_sample_proj",
    )(x, nm, w_in, clb)


def kernel(x_prompt, x_sample, cache_k, cache_v, state_hgrn, norm_mix, norm_ffn, w_in_ab, w_out_ab,
           q_norm, k_norm, attn_sink, rel_bias, gmlp_ln_g, gmlp_ln_b, gmlp_w_s, gmlp_b_s,
           w_in_c, c_lower_bounds, c_out_norm, w_out_c, w_gate, w_up, w_down):
    assert norm_mix.shape[0] == DEPTH == 2 and w_in_ab.shape[0] == 1 and w_in_c.shape[0] == 1
    nb, seq, _ = x_prompt.shape
    ns = x_sample.shape[0]
    assert x_sample.shape[1] == 1 and cache_k.shape[2] == WINDOW

    row = lambda v: v.reshape(1, -1)
    bf = lambda w: w.astype(BF16)
    w_out_ab0, w_out_c0 = bf(w_out_ab[0]), bf(w_out_c[0])
    nm, nf = norm_mix, norm_ffn
    qn, kn = row(q_norm[0]), row(k_norm[0])
    lng, lnb = row(gmlp_ln_g[0]), row(gmlp_ln_b[0])
    sink = attn_sink[0]

    tab, tabp = _bias_table(rel_bias)

    xs = x_sample.reshape(ns, D_MODEL)
    qx, knew_s, vnew_s, bm_s, gv_s, w_in_ab0 = _ab_sample_proj(
        xs, row(nm[0]), w_in_ab[0], qn, kn, lng, lnb, gmlp_w_s[0, :, 0, 0], gmlp_b_s[0, :, 0])
    sb = jnp.pad(tab[:, WINDOW - 1, WINDOW - 1:], ((0, 0), (0, WINDOW - 1)))
    to_t = lambda c: c.transpose(0, 2, 3, 1).reshape(ns, A_KV, WINDOW)
    xp, knew_p, vnew_p, gv_p, nk_s, nv_s, om = _ab_prompt(
        x_prompt, row(nm[0]), w_in_ab0, jnp.tile(qn, (1, A_HEADS)), jnp.tile(kn, (1, A_KV_HEADS)),
        sink, tabp, lng, lnb, gmlp_w_s[0], jnp.repeat(gmlp_b_s[0].T, B_GROUP_DIM, axis=1), w_out_ab0,
        to_t(cache_k[0]), to_t(cache_v[0]), qx.transpose(1, 0, 2), knew_s, vnew_s, sb)
    om = om.reshape(ns, A_KV_HEADS, A_GROUP, A_KV_HEADS, A_HEAD_DIM)
    a_s = jnp.stack([om[:, g, :, g, :] for g in range(A_KV_HEADS)], axis=1).reshape(ns, A_Q)
    xp, xs = _ffn(xp.reshape(nb * seq, D_MODEL), xs, jnp.concatenate([a_s, bm_s], axis=-1), w_out_ab0,
                  row(nf[0]), w_gate, w_up, w_down, 0)

    q_s, f_s, i_s, sg_s, w_in_c0 = _hgrn_sample_proj(xs, row(nm[1]), w_in_c[0], c_lower_bounds,
                                                     ns // (nb * (seq // TQH)))
    xp, st_p, st_s, o_s = _hgrn_prompt(xp.reshape(nb, seq, D_MODEL), row(nm[1]), w_in_c0, c_lower_bounds,
                                       row(c_out_norm[0]), w_out_c0, state_hgrn[0], f_s, q_s, i_s, sg_s)
    xp, xs = _ffn(xp.reshape(nb * seq, D_MODEL), xs, o_s, w_out_c0, row(nf[1]), w_gate, w_up, w_down, 1)

    kv5 = lambda a: a.reshape(1, a.shape[0], WINDOW, A_KV_HEADS, A_HEAD_DIM)
    from_t = lambda a: a.reshape(ns, A_KV_HEADS, A_HEAD_DIM, WINDOW).transpose(0, 3, 1, 2)[None]
    return (xp.reshape(nb, seq, D_MODEL), xs.reshape(ns, 1, D_MODEL),
            kv5(knew_p), kv5(vnew_p), from_t(nk_s), from_t(nv_s),
            gv_p[None], gv_s.reshape(1, ns, 1, B_WIDTH),
            st_p[None], st_s[None])
```

```python
import functools
import math

import jax
import jax.numpy as jnp
import numpy as np
from jax import lax
from jax.experimental import pallas as pl
from jax.experimental.pallas import tpu as pltpu

F32 = jnp.float32
BF16 = jnp.bfloat16

D_MODEL = 1024
DEPTH = 2
A_HEADS = 8
A_KV_HEADS = 2
A_GROUP = A_HEADS // A_KV_HEADS
A_HEAD_DIM = 64
WINDOW = 128
ATTN_SCALE = A_HEAD_DIM ** -0.5
NUM_BUCKETS = 32
MAX_DISTANCE = 128
A_Q = A_HEADS * A_HEAD_DIM
A_KV = A_KV_HEADS * A_HEAD_DIM
B_GROUPS = 8
B_GROUP_DIM = 64
B_WIDTH = B_GROUPS * B_GROUP_DIM
B_CHUNK = 128
AB_IN = A_Q + 2 * A_KV + 2 * B_WIDTH
AB_MIX = A_Q + B_WIDTH
C_HEADS = 8
C_KEY_DIM = 128
C_VAL_DIM = 128
C_F = C_HEADS * C_KEY_DIM
C_V = C_HEADS * C_VAL_DIM
C_IN = 2 * C_F + 2 * C_V
D_FF = 2816
EPS = 1e-6

NEG = -1e30

VMEM_LIMIT = 56 * 1024 * 1024
VREG_ROWS = 8

CHUNK = 128
TQ = 512
TQH = 512
TM = 512
HEAD_SKEW = 2
SAFE_LOG2_RANGE = 64.0
PREFIX_GROUP = 4
FACTORED_UNROLL = 4
AB_UNROLL = 4

_NT = (((1,), (1,)), ((), ()))
_TN = (((0,), (0,)), ((), ()))


def _rms(x, g):
    return x * lax.rsqrt(jnp.mean(x * x, axis=-1, keepdims=True) + EPS) * g


def _gelu(x):
    return 0.5 * x * (1.0 + lax.erf(x * math.sqrt(0.5)))


def _layernorm(x, g, b):
    xc = x - jnp.mean(x, axis=-1, keepdims=True)
    return xc * lax.rsqrt(jnp.mean(xc * xc, axis=-1, keepdims=True) + EPS) * g + b


def _dot(a, b):
    return jnp.dot(a, b, preferred_element_type=F32)


def _full(shape):
    n = len(shape)
    return pl.BlockSpec(shape, lambda *_: (0,) * n)


def _resident(shape):
    n = len(shape)
    return pl.BlockSpec(shape, lambda *_: (0,) * n, pipeline_mode=pl.Buffered(1))


_SMEM = pl.BlockSpec(memory_space=pltpu.SMEM)


def _bias_table_kernel(rel_ref, sb_ref, tabp_ref):
    qi = lax.broadcasted_iota(jnp.int32, (WINDOW, 2 * WINDOW), 0)
    kj = lax.broadcasted_iota(jnp.int32, (WINDOW, 2 * WINDOW), 1)
    dist = qi + WINDOW - kj
    ok = (dist >= 0) & (dist < WINDOW)
    max_exact = NUM_BUCKETS // 2
    d = jnp.maximum(dist, 0)
    dl = jnp.maximum(d, 1).astype(F32)
    v = (jnp.log(dl / max_exact) / math.log(MAX_DISTANCE / max_exact) * (NUM_BUCKETS - max_exact))
    far = d >= max_exact
    hits = []
    for b in range(NUM_BUCKETS):
        if b < max_exact:
            hits.append(d == b)
        elif b < NUM_BUCKETS - 1:
            hits.append(far & (v >= b - max_exact) & (v < b - max_exact + 1))
        else:
            hits.append(far & (v >= b - max_exact))
    for h in range(A_HEADS):
        acc = jnp.zeros((WINDOW, 2 * WINDOW), F32)
        for b in range(NUM_BUCKETS):
            acc = jnp.where(hits[b], rel_ref[b, h], acc)
        t = jnp.where(ok, acc, NEG)
        last = pltpu.roll(t[WINDOW - 1:WINDOW, :], WINDOW + 1, 1)
        sb_ref[h:h + 1, :] = jnp.where(kj[0:1, :] <= WINDOW, last, 0.0)
        cols = slice((h % 2) * 2 * WINDOW, (h % 2 + 1) * 2 * WINDOW)
        tabp_ref[0, h // 2, :, cols] = t
        tabp_ref[1, h // 2, :, cols] = jnp.where(kj < WINDOW, NEG, t)


def _bias_table(rel_bias):
    return pl.pallas_call(
        _bias_table_kernel,
        out_shape=[
            jax.ShapeDtypeStruct((A_HEADS, 2 * WINDOW), F32),
            jax.ShapeDtypeStruct((2, A_HEADS // 2, WINDOW, 4 * WINDOW), F32),
        ],
        in_specs=[_SMEM],
        name="bias_table",
    )(rel_bias)


PAIR = 2 * A_HEAD_DIM
N_PAIRS = A_HEADS // 2


def _sample_cache_attention(ck_ref, cv_ref, qx_ref, kn_ref, vn_ref, sb_ref, sink_ref,
                            nk_ref, nv_ref, om_ref):
    wb = ck_ref.shape[2]
    first_group = lax.broadcasted_iota(jnp.int32, (1, A_HEADS, A_HEAD_DIM), 1) < A_GROUP
    newest = lax.broadcasted_iota(jnp.int32, (1, 1, wb), 2) == wb - 1
    sink = sink_ref[...][None]
    kc, vc = ck_ref[...], cv_ref[...]
    kn, vn = kn_ref[...], vn_ref[...]
    kn_cols, vn_cols = kn_ref[:, 0, :].T, vn_ref[:, 0, :].T
    for i in range(kc.shape[0]):
        nk_ref[i] = jnp.where(newest[0], kn_cols[:, i:i + 1], pltpu.roll(kc[i], wb - 1, 1))
        nv_ref[i] = jnp.where(newest[0], vn_cols[:, i:i + 1], pltpu.roll(vc[i], wb - 1, 1))
    q = qx_ref[...]
    s = jnp.einsum('bhd,bdk->bhk', q.astype(BF16), kc.astype(BF16), preferred_element_type=F32)
    s = s + sb_ref[:, 0:wb][None]
    sn = jnp.sum(q * kn, axis=-1, keepdims=True) + sb_ref[:, wb:wb + 1][None]
    m = jnp.maximum(jnp.maximum(jnp.max(s, axis=-1, keepdims=True), sn), sink)
    e = jnp.exp(s - m)
    en = jnp.exp(sn - m)
    r = 1.0 / (jnp.sum(e, axis=-1, keepdims=True) + en + jnp.exp(sink - m))
    o = jnp.einsum('bhk,bdk->bhd', (e * r).astype(BF16), vc.astype(BF16),
                   preferred_element_type=F32) + (en * r) * vn
    om_ref[...] = jnp.where(first_group, o[:, :, 0:A_HEAD_DIM], o[:, :, A_HEAD_DIM:A_KV])


def _ab_prompt_kernel(sink_ref, x_ref, nm_ref, win_ref, qg_ref, kg_ref, tabp_ref, lng_ref, lnb_ref,
                      ws_ref, bsp_ref, wout_ref,
                      ck_ref, cv_ref, qx_ref, kn_ref, vn_ref, sb_ref, sinkc_ref,
                      y_ref, knew_ref, vnew_ref, gv_ref, nk_ref, nv_ref, om_ref,
                      z_ref, mix_ref, q_ref, k_ref, kr_ref, v_ref, vr_ref, wpair_ref, kl_ref, vl_ref, gl_ref):
    j = pl.program_id(1)
    last_j = pl.num_programs(1) - 1
    n_chunks = TQ // CHUNK

    @pl.when(j == 0)
    def _():
        for ref in (k_ref, kr_ref, v_ref, vr_ref):
            ref[0:CHUNK, :] = jnp.zeros((CHUNK, A_KV), BF16)
        row = lax.broadcasted_iota(jnp.int32, (B_CHUNK, B_CHUNK), 0)
        col = lax.broadcasted_iota(jnp.int32, (B_CHUNK, B_CHUNK), 1)
        for g in range(B_GROUPS):
            wpair_ref[g // 2, :, (g % 2) * B_CHUNK:(g % 2 + 1) * B_CHUNK] = jnp.where(
                row >= col, ws_ref[g], 0.0).astype(BF16)

    h = _rms(x_ref[0], nm_ref[...]).astype(BF16)
    z_ref[...] = _dot(h, win_ref[...])

    _sample_cache_attention(ck_ref, cv_ref, qx_ref, kn_ref, vn_ref, sb_ref, sinkc_ref,
                            nk_ref, nv_ref, om_ref)

    lo_half = lax.broadcasted_iota(jnp.int32, (1, PAIR), 1) < A_HEAD_DIM

    def mean_sq_halves(x):
        x2 = x * x
        lo_sum = jnp.sum(jnp.where(lo_half, x2, 0.0), axis=-1, keepdims=True)
        hi_sum = jnp.sum(jnp.where(lo_half, 0.0, x2), axis=-1, keepdims=True)
        return jnp.where(lo_half, lo_sum, hi_sum) * (1.0 / A_HEAD_DIM)

    def block_diag(top, bot):
        zero = jnp.zeros_like(top)
        return jnp.concatenate([jnp.where(lo_half, top, zero), jnp.where(lo_half, zero, bot)], axis=0)

    kraw = z_ref[:, A_Q:A_Q + A_KV]
    v_all = z_ref[:, A_Q + A_KV:A_Q + 2 * A_KV]
    kn_all = kraw * lax.rsqrt(mean_sq_halves(kraw) + EPS) * kg_ref[...]
    k_ref[CHUNK:, :] = kn_all.astype(BF16)
    kr_ref[CHUNK:, :] = pltpu.roll(kn_all, A_HEAD_DIM, 1).astype(BF16)
    v_ref[CHUNK:, :] = v_all.astype(BF16)
    vr_ref[CHUNK:, :] = pltpu.roll(v_all, A_HEAD_DIM, 1).astype(BF16)
    kl_ref[...] = kn_all[TQ - CHUNK:, :]
    vl_ref[...] = v_all[TQ - CHUNK:, :]
    for i in range(N_PAIRS):
        ps = slice(i * PAIR, (i + 1) * PAIR)
        qraw = z_ref[:, ps]
        qn = qraw * lax.rsqrt(mean_sq_halves(qraw) + EPS) * (qg_ref[:, ps] * ATTN_SCALE)
        q_ref[:, ps] = qn.astype(BF16)

    def chunk(c, carry):
        r0 = pl.multiple_of(c * CHUNK, CHUNK)
        rows = pl.ds(r0, CHUNK)
        first = jnp.where(jnp.logical_and(j == 0, c == 0), 1, 0)

        both = pl.ds(r0, 2 * CHUNK)
        k2, k2r, v2, v2r = k_ref[both, :], kr_ref[both, :], v_ref[both, :], vr_ref[both, :]
        kbd = [block_diag(k2, k2r), block_diag(k2r, k2)]
        vbd = [block_diag(v2, v2r), block_diag(v2r, v2)]

        scores = []
        for i in range(N_PAIRS):
            s = lax.dot_general(q_ref[rows, i * PAIR:(i + 1) * PAIR], kbd[i // (A_GROUP // 2)], _NT,
                                preferred_element_type=F32)
            scores.append(s + tabp_ref[first, i])
        outs = []
        for i in range(N_PAIRS):
            es, rs = [], []
            for hh in range(2):
                sh = scores[i][:, hh * 2 * WINDOW:(hh + 1) * 2 * WINDOW]
                sk = sink_ref[2 * i + hh]
                m = jnp.maximum(jnp.max(sh, axis=-1, keepdims=True), sk)
                e = jnp.exp(sh - m)
                rs.append(1.0 / (jnp.sum(e, axis=-1, keepdims=True) + jnp.exp(sk - m)))
                es.append(e.astype(BF16))
            o = _dot(jnp.concatenate(es, axis=-1), vbd[i // (A_GROUP // 2)])
            outs.append(o * jnp.where(lo_half, rs[0], rs[1]))
        mix_ref[rows, 0:A_Q] = jnp.concatenate(outs, axis=-1).astype(BF16)

        zu = z_ref[rows, A_Q + 2 * A_KV:A_Q + 2 * A_KV + B_WIDTH]
        zv = z_ref[rows, A_Q + 2 * A_KV + B_WIDTH:AB_IN]
        u = _gelu(zu)
        vln = _layernorm(_gelu(zv), lng_ref[...], lnb_ref[...])
        vlb = vln.astype(BF16)
        sparts = []
        for i in range(B_GROUPS // 2):
            vpair = vlb[:, i * PAIR:(i + 1) * PAIR]
            sparts.append(_dot(wpair_ref[i], block_diag(vpair, vpair)))
        bm = u * (jnp.concatenate(sparts, axis=-1) + bsp_ref[...])
        mix_ref[rows, A_Q:AB_MIX] = bm.astype(BF16)

        gl_ref[...] = vln
        return carry

    lax.fori_loop(0, n_chunks, chunk, 0, unroll=AB_UNROLL)
    y_ref[0] = x_ref[0] + _dot(mix_ref[...], wout_ref[...])
    for ref in (k_ref, kr_ref, v_ref, vr_ref):
        ref[0:CHUNK, :] = ref[TQ:TQ + CHUNK, :]

    @pl.when(j == last_j)
    def _():
        knew_ref[0] = kl_ref[...]
        vnew_ref[0] = vl_ref[...]
        gv_ref[0] = gl_ref[...]


def _ab_prompt(x, nm, w_in, qg, kg, sink, tabp, lng, lnb, w_s, bsp, w_out, ck, cv, qx, kn_s, vn_s, sb):
    nb, seq, _ = x.shape
    nj = seq // TQ
    grid = (nb, nj)
    ns, _, wb = ck.shape
    sba = ns // (nb * nj)
    assert sba * nb * nj == ns
    blk = lambda b, j: (b, j, 0)
    per_b = lambda b, j: (b, 0, 0)
    step = lambda b, j: (b * nj + j, 0, 0)
    return pl.pallas_call(
        _ab_prompt_kernel,
        grid=grid,
        in_specs=[
            _SMEM,
            pl.BlockSpec((1, TQ, D_MODEL), blk),
            _full((1, D_MODEL)),
            _resident((D_MODEL, AB_IN)),
            _full((1, A_Q)),
            _full((1, A_KV)),
            _resident((2, N_PAIRS, WINDOW, 4 * WINDOW)),
            _full((1, B_WIDTH)),
            _full((1, B_WIDTH)),
            _resident((B_GROUPS, B_CHUNK, B_CHUNK)),
            _resident((B_CHUNK, B_WIDTH)),
            _resident((AB_MIX, D_MODEL)),
            pl.BlockSpec((sba, A_KV, wb), step),
            pl.BlockSpec((sba, A_KV, wb), step),
            pl.BlockSpec((sba, A_HEADS, A_KV), step),
            pl.BlockSpec((sba, 1, A_KV), step),
            pl.BlockSpec((sba, 1, A_KV), step),
            _full((A_HEADS, 2 * WINDOW)),
            _full((A_HEADS, 1)),
        ],
        out_specs=[
            pl.BlockSpec((1, TQ, D_MODEL), blk),
            pl.BlockSpec((1, WINDOW, A_KV), per_b),
            pl.BlockSpec((1, WINDOW, A_KV), per_b),
            pl.BlockSpec((1, B_CHUNK, B_WIDTH), per_b),
            pl.BlockSpec((sba, A_KV, wb), step),
            pl.BlockSpec((sba, A_KV, wb), step),
            pl.BlockSpec((sba, A_HEADS, A_HEAD_DIM), step),
        ],
        out_shape=[
            jax.ShapeDtypeStruct((nb, seq, D_MODEL), F32),
            jax.ShapeDtypeStruct((nb, WINDOW, A_KV), F32),
            jax.ShapeDtypeStruct((nb, WINDOW, A_KV), F32),
            jax.ShapeDtypeStruct((nb, B_CHUNK, B_WIDTH), F32),
            jax.ShapeDtypeStruct((ns, A_KV, wb), F32),
            jax.ShapeDtypeStruct((ns, A_KV, wb), F32),
            jax.ShapeDtypeStruct((ns, A_HEADS, A_HEAD_DIM), F32),
        ],
        scratch_shapes=[
            pltpu.VMEM((TQ, AB_IN), F32),
            pltpu.VMEM((TQ, AB_MIX), BF16),
            pltpu.VMEM((TQ, A_Q), BF16),
            pltpu.VMEM((CHUNK + TQ, A_KV), BF16),
            pltpu.VMEM((CHUNK + TQ, A_KV), BF16),
            pltpu.VMEM((CHUNK + TQ, A_KV), BF16),
            pltpu.VMEM((CHUNK + TQ, A_KV), BF16),
            pltpu.VMEM((B_GROUPS // 2, B_CHUNK, 2 * B_CHUNK), BF16),
            pltpu.VMEM((WINDOW, A_KV), F32),
            pltpu.VMEM((WINDOW, A_KV), F32),
            pltpu.VMEM((B_CHUNK, B_WIDTH), F32),
        ],
        compiler_params=pltpu.CompilerParams(
            dimension_semantics=("arbitrary", "arbitrary"), vmem_limit_bytes=VMEM_LIMIT),
        name="ab_prompt",
    )(sink, x, nm, w_in, qg, kg, tabp, lng, lnb, w_s, bsp, w_out,
      ck, cv, qx, kn_s[:, None, :], vn_s[:, None, :], sb, sink.reshape(A_HEADS, 1))


FF_TILE = 256


def _ffn_kernel(xp_ref, xs_ref, ms_ref, wo_ref, g_ref, wg_ref, wu_ref, wd_ref, yp_ref, ys_ref,
                wg_s, wu_s, wd_s, h0_s, acc_s, *, n_cast, n_prompt):
    s = pl.program_id(0)

    def gated(h, wg, wu):
        gate = _dot(h, wg)
        return (gate * jax.nn.sigmoid(gate) * _dot(h, wu)).astype(BF16)

    @pl.when(s == 0)
    def _():
        x = xp_ref[...]
        h0_s[...] = _rms(x, g_ref[...]).astype(BF16)
        acc_s[...] = x

    for c in range(n_cast):
        @pl.when(s == c)
        def _(c=c):
            tile = slice(c * FF_TILE, (c + 1) * FF_TILE)
            wg_t, wu_t, wd_t = (r[...].astype(BF16) for r in (wg_ref, wu_ref, wd_ref))
            wg_s[:, tile] = wg_t
            wu_s[:, tile] = wu_t
            wd_s[tile, :] = wd_t
            acc_s[...] += _dot(gated(h0_s[...], wg_t, wu_t), wd_t)

    @pl.when(s == n_cast - 1)
    def _():
        yp_ref[...] = acc_s[...]

    def swiglu(x):
        h = _rms(x, g_ref[...]).astype(BF16)
        return x + _dot(gated(h, wg_s[...], wu_s[...]), wd_s[...])

    @pl.when(jnp.logical_and(s >= n_cast, s < n_cast + n_prompt - 1))
    def _():
        yp_ref[...] = swiglu(xp_ref[...])

    @pl.when(s == n_cast + n_prompt - 1)
    def _():
        ys_ref[...] = swiglu(xs_ref[...] + _dot(ms_ref[...].astype(BF16), wo_ref[...]))


def _ffn(xp, xs, mix_s, w_o, g, w_gate, w_up, w_down, layer):
    rows, ns = xp.shape[0], xs.shape[0]
    n_cast, n_prompt = D_FF // FF_TILE, rows // TM
    w_tile = lambda s: jnp.minimum(s, n_cast - 1)
    row_blk = lambda s: (jnp.clip(s - (n_cast - 1), 0, n_prompt - 1), 0)
    return pl.pallas_call(
        functools.partial(_ffn_kernel, n_cast=n_cast, n_prompt=n_prompt),
        grid=(n_cast + n_prompt,),
        in_specs=[
            pl.BlockSpec((TM, D_MODEL), row_blk),
            _full((ns, D_MODEL)),
            _full((ns, D_MODEL)),
            _resident((D_MODEL, D_MODEL)),
            _full((1, D_MODEL)),
            pl.BlockSpec((None, D_MODEL, FF_TILE), lambda s: (layer, 0, w_tile(s))),
            pl.BlockSpec((None, D_MODEL, FF_TILE), lambda s: (layer, 0, w_tile(s))),
            pl.BlockSpec((None, FF_TILE, D_MODEL), lambda s: (layer, w_tile(s), 0)),
        ],
        out_specs=[pl.BlockSpec((TM, D_MODEL), row_blk), _full((ns, D_MODEL))],
        out_shape=[jax.ShapeDtypeStruct((rows, D_MODEL), F32), jax.ShapeDtypeStruct((ns, D_MODEL), F32)],
        scratch_shapes=[
            pltpu.VMEM((D_MODEL, D_FF), BF16),
            pltpu.VMEM((D_MODEL, D_FF), BF16),
            pltpu.VMEM((D_FF, D_MODEL), BF16),
            pltpu.VMEM((TM, D_MODEL), BF16),
            pltpu.VMEM((TM, D_MODEL), F32),
        ],
        compiler_params=pltpu.CompilerParams(
            dimension_semantics=("arbitrary",), vmem_limit_bytes=VMEM_LIMIT),
        name="ffn",
    )(xp, xs, mix_s, w_o, g, w_gate, w_up, w_down)


def _lower_bound(clb):
    m = jnp.max(clb, axis=0, keepdims=True)
    e = jnp.exp(clb - m)
    sm = e / jnp.sum(e, axis=0, keepdims=True)
    return (sm[0:1] + sm[1:2]) - sm[0:1]


def _split3(x):
    hi = x.astype(BF16)
    r = x - hi.astype(F32)
    mid = r.astype(BF16)
    lo = (r - mid.astype(F32)).astype(BF16)
    return hi, mid, lo


def _neg_abs(x):
    return lax.bitcast_convert_type(
        lax.bitcast_convert_type(x, jnp.uint32) | jnp.uint32(0x80000000), F32)


def _pair_level_table():
    t = np.arange(CHUNK)[:, None]
    s = np.arange(CHUNK)[None, :]
    lev = np.floor(np.log2(np.maximum(t ^ s, 1))).astype(np.int32)
    lev = np.where(t == s, -1, lev)
    return np.where(s > t, -2, lev).astype(np.int32)


def _level_operand(p, q, kk, f, b2):
    m = 2 ** p
    if m < VREG_ROWS:
        shape3 = (CHUNK // VREG_ROWS, VREG_ROWS, q.shape[1])
        sub = lax.broadcasted_iota(jnp.int32, (1, VREG_ROWS, q.shape[1]), 1)
        upper = ((sub >> p) & 1) == 1
        q3, k3 = q.reshape(shape3), kk.reshape(shape3)
        if p == 0:
            y = jnp.where(upper, q3 * f.reshape(shape3), k3)
        else:
            b3 = b2.reshape(shape3)
            be = b3[:, m - 1:m, :]
            for k in range(1, VREG_ROWS // (2 * m)):
                be = jnp.where(sub >= 2 * m * k, b3[:, 2 * m * k + m - 1:2 * m * k + m, :], be)
            y = jnp.where(upper, q3, k3) * jnp.exp2(_neg_abs(b3 - be))
        return y.reshape(q.shape).astype(BF16)
    parts = []
    for k in range(CHUNK // (2 * m)):
        lo = slice(2 * m * k, 2 * m * k + m)
        up = slice(2 * m * k + m, 2 * m * (k + 1))
        be = b2[2 * m * k + m - 1:2 * m * k + m, :]
        parts.append(kk[lo] * jnp.exp2(be - b2[lo]))
        parts.append(q[up] * jnp.exp2(b2[up] - be))
    return jnp.concatenate(parts, axis=0).astype(BF16)


def _merge_level(p, att, pm, lev):
    m = 2 ** p
    if m < VREG_ROWS:
        return jnp.where(lev == p, pm, att)
    col = lax.broadcasted_iota(jnp.int32, (1, CHUNK), 1)
    parts = []
    for k in range(CHUNK // (2 * m)):
        lo = slice(2 * m * k, 2 * m * k + m)
        up = slice(2 * m * k + m, 2 * m * (k + 1))
        parts.append(att[lo])
        parts.append(jnp.where((col >= 2 * m * k) & (col < 2 * m * k + m), pm[up], att[up]))
    return jnp.concatenate(parts, axis=0)


def _hgrn_prompt_kernel(x_ref, nm_ref, win_ref, clb_ref, on_ref, wout_ref, lev_ref,
                        ss_ref, fs_ref, qs_ref, is_ref, gs_ref,
                        y_ref, st_ref, sso_ref, os_ref,
                        z_ref, o_ref, stt_ref, k_ref):
    j = pl.program_id(1)
    last_j = pl.num_programs(1) - 1
    n_chunks = TQH // CHUNK
    n_levels = int(math.log2(CHUNK))

    @pl.when(j == 0)
    def _():
        stt_ref[...] = jnp.zeros_like(stt_ref)

    h = _rms(x_ref[0], nm_ref[...]).astype(BF16)
    z_ref[...] = _dot(h, win_ref[...])

    head_cols = lambda ref: [ref[:, hd * C_KEY_DIM:(hd + 1) * C_KEY_DIM].T for hd in range(C_HEADS)]
    f_cols, q_cols = head_cols(fs_ref), head_cols(qs_ref)
    out_rows = []
    for smp in range(ss_ref.shape[0]):
        parts = []
        for hd in range(C_HEADS):
            hs = slice(hd * C_VAL_DIM, (hd + 1) * C_VAL_DIM)
            fb = jnp.broadcast_to(f_cols[hd][:, smp:smp + 1], (C_KEY_DIM, C_VAL_DIM))
            sn = fb * ss_ref[smp, hd] + (1.0 - fb) * is_ref[smp:smp + 1, hs]
            sso_ref[smp, hd] = sn
            o = jnp.sum(q_cols[hd][:, smp:smp + 1] * sn, axis=0, keepdims=True)
            parts.append(_rms(o, on_ref[...]))
        out_rows.append(jnp.concatenate(parts, axis=-1))
    os_ref[...] = jnp.concatenate(out_rows, axis=0) * gs_ref[...]

    lb = _lower_bound(clb_ref[...])

    row = lax.broadcasted_iota(jnp.int32, (CHUNK, CHUNK), 0)
    col = lax.broadcasted_iota(jnp.int32, (CHUNK, CHUNK), 1)
    ltri = (row >= col).astype(BF16)

    def chunk_rows(c):
        return pl.ds(pl.multiple_of(c * CHUNK, CHUNK), CHUNK)

    def prefix(g, worst):
        rows = [chunk_rows(g * PREFIX_GROUP + i) for i in range(PREFIX_GROUP)]
        gates = [z_ref[r, C_F:2 * C_F] for r in rows]
        for r, gate in zip(rows, gates):
            f_all = lb + (1.0 - lb) * jax.nn.sigmoid(gate)
            k_ref[r, :] = 1.0 - f_all
            hi, mid, lo = _split3(jnp.log2(f_all))
            b2 = (_dot(ltri, hi) + _dot(ltri, mid)) + _dot(ltri, lo)
            z_ref[r, C_F:2 * C_F] = b2
            b_mid = b2[CHUNK // 2 - 1:CHUNK // 2, :]
            b_last = b2[CHUNK - 1:CHUNK, :]
            worst = jnp.maximum(worst, jnp.maximum(-b_mid, b_mid - b_last))
        return worst

    worst = lax.fori_loop(0, n_chunks // PREFIX_GROUP, prefix, jnp.zeros((1, C_F), F32))
    bounded = jnp.max(worst) <= SAFE_LOG2_RANGE

    def finish_head(rows, hd, o):
        gt = z_ref[rows, 2 * C_F + C_V + hd * C_VAL_DIM:2 * C_F + C_V + (hd + 1) * C_VAL_DIM]
        o = _rms(o, on_ref[...]) * jax.nn.sigmoid(gt)
        o_ref[rows, hd * C_VAL_DIM:(hd + 1) * C_VAL_DIM] = o.astype(BF16)

    def head_inputs(rows, hd):
        q = z_ref[rows, hd * C_KEY_DIM:(hd + 1) * C_KEY_DIM]
        kk = k_ref[rows, hd * C_KEY_DIM:(hd + 1) * C_KEY_DIM]
        b2 = z_ref[rows, C_F + hd * C_KEY_DIM:C_F + (hd + 1) * C_KEY_DIM]
        ivb = z_ref[rows, 2 * C_F + hd * C_VAL_DIM:2 * C_F + (hd + 1) * C_VAL_DIM].astype(BF16)
        return q, kk, b2, ivb

    def factored_chunk(c, carry):
        rows = chunk_rows(c)
        for hd in range(C_HEADS):
            q, kk, b2, ivb = head_inputs(rows, hd)
            b_mid = b2[CHUNK // 2 - 1:CHUNK // 2, :]
            b_last = b2[CHUNK - 1:CHUNK, :]
            qs = (q * jnp.exp2(b2 - b_mid)).astype(BF16)
            kd = (kk * jnp.exp2(b_mid - b2)).astype(BF16)
            att = jnp.where(row >= col, lax.dot_general(qs, kd, _NT, preferred_element_type=F32), 0.0)
            stt = stt_ref[hd]
            o = lax.dot_general(qs, (stt * jnp.exp2(b_mid)).astype(BF16), _NT,
                                preferred_element_type=F32) + _dot(att.astype(BF16), ivb)
            stt_ref[hd] = stt * jnp.exp2(b_last) + jnp.exp2(b_last - b_mid) * lax.dot_general(
                ivb, kd, _TN, preferred_element_type=F32)
            finish_head(rows, hd, o)
        return carry

    def tree_chunk(c, carry):
        rows = chunk_rows(c)
        lev = lev_ref[...]

        def products(hd):
            q, kk, b2, ivb = head_inputs(rows, hd)
            diag = jnp.sum(q * kk, axis=-1, keepdims=True)
            pms = []
            for p in range(n_levels):
                y = _level_operand(p, q, kk, 1.0 - kk, b2)
                pms.append(lax.dot_general(y, y, _NT, preferred_element_type=F32))
            stt = stt_ref[hd]
            o_prev = lax.dot_general((q * jnp.exp2(b2)).astype(BF16), stt.astype(BF16), _NT,
                                     preferred_element_type=F32)
            b_last = b2[CHUNK - 1:CHUNK, :]
            kd = (kk * jnp.exp2(b_last - b2)).astype(BF16)
            stt_ref[hd] = stt * jnp.exp2(b_last) + lax.dot_general(
                ivb, kd, _TN, preferred_element_type=F32)
            return diag, pms, o_prev, ivb

        def finish(hd, diag, pms, o_prev, ivb):
            att = jnp.where(lev == -1, diag, 0.0)
            for p in range(n_levels):
                att = _merge_level(p, att, pms[p], lev)
            finish_head(rows, hd, o_prev + _dot(att.astype(BF16), ivb))

        pending = [products(hd) for hd in range(HEAD_SKEW)]
        for hd in range(C_HEADS):
            if hd + HEAD_SKEW < C_HEADS:
                pending.append(products(hd + HEAD_SKEW))
            finish(hd, *pending.pop(0))
        return carry

    @pl.when(bounded)
    def _():
        lax.fori_loop(0, n_chunks, factored_chunk, 0, unroll=FACTORED_UNROLL)

    @pl.when(jnp.logical_not(bounded))
    def _():
        lax.fori_loop(0, n_chunks, tree_chunk, 0)

    y_ref[0] = x_ref[0] + _dot(o_ref[...], wout_ref[...])

    @pl.when(j == last_j)
    def _():
        for hd in range(C_HEADS):
            st_ref[0, hd] = stt_ref[hd].T


def _hgrn_prompt(x, nm, w_in, clb, on, w_out, state_s, f_s, q_s, i_s, g_s):
    nb, seq, _ = x.shape
    nj = seq // TQH
    ns = state_s.shape[0]
    sbh = f_s.shape[1]
    assert f_s.shape[0] == nb * nj and sbh * nb * nj == ns
    blk = lambda b, j: (b, j, 0)
    step = lambda b, j: (b * nj + j, 0, 0)
    step4 = lambda b, j: (b * nj + j, 0, 0, 0)
    y, st, st_s, o_s = pl.pallas_call(
        _hgrn_prompt_kernel,
        grid=(nb, nj),
        in_specs=[
            pl.BlockSpec((1, TQH, D_MODEL), blk),
            _full((1, D_MODEL)),
            _resident((D_MODEL, C_IN)),
            _full((DEPTH, C_F)),
            _full((1, C_VAL_DIM)),
            _resident((C_V, D_MODEL)),
            _full((CHUNK, CHUNK)),
            pl.BlockSpec((sbh, C_HEADS, C_KEY_DIM, C_VAL_DIM), step4),
            pl.BlockSpec((None, sbh, C_F), step),
            pl.BlockSpec((None, sbh, C_F), step),
            pl.BlockSpec((None, sbh, C_V), step),
            pl.BlockSpec((None, sbh, C_V), step),
        ],
        out_specs=[
            pl.BlockSpec((1, TQH, D_MODEL), blk),
            pl.BlockSpec((1, C_HEADS, C_KEY_DIM, C_VAL_DIM), lambda b, j: (b, 0, 0, 0)),
            pl.BlockSpec((sbh, C_HEADS, C_KEY_DIM, C_VAL_DIM), step4),
            pl.BlockSpec((None, sbh, C_V), step),
        ],
        out_shape=[
            jax.ShapeDtypeStruct((nb, seq, D_MODEL), F32),
            jax.ShapeDtypeStruct((nb, C_HEADS, C_KEY_DIM, C_VAL_DIM), F32),
            jax.ShapeDtypeStruct(state_s.shape, F32),
            jax.ShapeDtypeStruct((ns // sbh, sbh, C_V), F32),
        ],
        scratch_shapes=[
            pltpu.VMEM((TQH, C_IN), F32),
            pltpu.VMEM((TQH, C_V), BF16),
            pltpu.VMEM((C_HEADS, C_VAL_DIM, C_KEY_DIM), F32),
            pltpu.VMEM((TQH, C_F), F32),
        ],
        compiler_params=pltpu.CompilerParams(
            dimension_semantics=("arbitrary", "arbitrary"), vmem_limit_bytes=VMEM_LIMIT),
        name="hgrn_prompt",
    )(x, nm, w_in, clb, on, w_out, jnp.asarray(_pair_level_table()),
      state_s, f_s, q_s, i_s, g_s)
    return y, st, st_s, o_s.reshape(ns, C_V)


def _ab_sample_proj_kernel(w00_ref, b0_ref, x_ref, nm_ref, win_ref, qn_ref, kn_ref, lng_ref, lnb_ref,
                           qx_ref, knew_ref, vnew_ref, bm_ref, gv_ref, wbf_ref):
    n = x_ref.shape[0]
    wbf_ref[...] = win_ref[...].astype(BF16)
    h = _rms(x_ref[...], nm_ref[...]).astype(BF16)
    z = _dot(h, wbf_ref[...])
    zeros = jnp.zeros((n, A_HEAD_DIM), F32)
    for hh in range(A_HEADS):
        qh = _rms(z[:, hh * A_HEAD_DIM:(hh + 1) * A_HEAD_DIM], qn_ref[...]) * ATTN_SCALE
        qx_ref[:, hh, :] = jnp.concatenate([qh, zeros] if hh // A_GROUP == 0 else [zeros, qh], axis=-1)
    kparts = []
    for g in range(A_KV_HEADS):
        kparts.append(_rms(z[:, A_Q + g * A_HEAD_DIM:A_Q + (g + 1) * A_HEAD_DIM], kn_ref[...]))
    knew_ref[...] = jnp.concatenate(kparts, axis=-1)
    vnew_ref[...] = z[:, A_Q + A_KV:A_Q + 2 * A_KV]

    u = _gelu(z[:, A_Q + 2 * A_KV:A_Q + 2 * A_KV + B_WIDTH])
    vln = _layernorm(_gelu(z[:, A_Q + 2 * A_KV + B_WIDTH:AB_IN]), lng_ref[...], lnb_ref[...])
    grp = lax.broadcasted_iota(jnp.int32, (1, B_WIDTH), 1) // B_GROUP_DIM
    srow = jnp.zeros((1, B_WIDTH), F32)
    brow = jnp.zeros((1, B_WIDTH), F32)
    for g in range(B_GROUPS):
        srow = jnp.where(grp == g, w00_ref[g], srow)
        brow = jnp.where(grp == g, b0_ref[g], brow)
    bm_ref[...] = u * (vln * srow + brow)
    gv_ref[...] = vln


def _ab_sample_proj(x, nm, w_in, qn, kn, lng, lnb, w00, b0):
    n = x.shape[0]
    return pl.pallas_call(
        _ab_sample_proj_kernel,
        in_specs=[_SMEM, _SMEM] + [pl.BlockSpec(memory_space=pltpu.VMEM)] * 7,
        out_shape=[
            jax.ShapeDtypeStruct((n, A_HEADS, A_KV), F32),
            jax.ShapeDtypeStruct((n, A_KV), F32),
            jax.ShapeDtypeStruct((n, A_KV), F32),
            jax.ShapeDtypeStruct((n, B_WIDTH), F32),
            jax.ShapeDtypeStruct((n, B_WIDTH), F32),
            jax.ShapeDtypeStruct(w_in.shape, BF16),
        ],
        compiler_params=pltpu.CompilerParams(vmem_limit_bytes=VMEM_LIMIT),
        name="ab_sample_proj",
    )(w00, b0, x, nm, w_in, qn, kn, lng, lnb)


PROJ_TILE = 512


def _hgrn_sample_proj_kernel(x_ref, nm_ref, win_ref, clb_ref, z_ref, wbf_ref, h_s):
    t = pl.program_id(0)
    tiles_per_kind = C_F // PROJ_TILE
    kind = t // tiles_per_kind

    @pl.when(t == 0)
    def _():
        h_s[...] = _rms(x_ref[...], nm_ref[...]).astype(BF16)

    wt = win_ref[...].astype(BF16)
    wbf_ref[...] = wt
    z = _dot(h_s[...], wt)
    lb_all = _lower_bound(clb_ref[...])
    lb = lb_all[:, 0:PROJ_TILE]
    for i in range(1, tiles_per_kind):
        lb = jnp.where(t % tiles_per_kind == i, lb_all[:, i * PROJ_TILE:(i + 1) * PROJ_TILE], lb)
    sg = jax.nn.sigmoid(z)
    val = jnp.where(kind == 1, lb + (1.0 - lb) * sg, jnp.where(kind == 3, sg, z))
    per = z_ref.shape[1]
    for i in range(z_ref.shape[0]):
        z_ref[i] = val[i * per:(i + 1) * per, :]


def _hgrn_sample_proj(x, nm, w_in, clb, per_step):
    n = x.shape[0]
    tiles_per_kind = C_F // PROJ_TILE
    assert C_F == C_V and C_IN == 4 * C_F
    zz, w_bf = pl.pallas_call(
        _hgrn_sample_proj_kernel,
        grid=(C_IN // PROJ_TILE,),
        in_specs=[
            _full((n, D_MODEL)),
            _full((1, D_MODEL)),
            pl.BlockSpec((D_MODEL, PROJ_TILE), lambda t: (0, t)),
            _full((DEPTH, C_F)),
        ],
        out_specs=[
            pl.BlockSpec((None, n // per_step, per_step, PROJ_TILE),
                         lambda t: (t // tiles_per_kind, 0, 0, t % tiles_per_kind)),
            pl.BlockSpec((D_MODEL, PROJ_TILE), lambda t: (0, t)),
        ],
        out_shape=[
            jax.ShapeDtypeStruct((4, n // per_step, per_step, C_F), F32),
            jax.ShapeDtypeStruct(w_in.shape, BF16),
        ],
        scratch_shapes=[pltpu.VMEM((n, D_MODEL), BF16)],
        compiler_params=pltpu.CompilerParams(dimension_semantics=("arbitrary",)),
        name="hgrn_sample_proj",
    )(x, nm, w_in, clb)
    return zz[0], zz[1], zz[2], zz[3], w_bf


def kernel(x_prompt, x_sample, cache_k, cache_v, state_hgrn, norm_mix, norm_ffn, w_in_ab, w_out_ab,
           q_norm, k_norm, attn_sink, rel_bias, gmlp_ln_g, gmlp_ln_b, gmlp_w_s, gmlp_b_s,
           w_in_c, c_lower_bounds, c_out_norm, w_out_c, w_gate, w_up, w_down):
    assert norm_mix.shape[0] == DEPTH == 2 and w_in_ab.shape[0] == 1 and w_in_c.shape[0] == 1
    nb, seq, _ = x_prompt.shape
    ns = x_sample.shape[0]
    assert x_sample.shape[1] == 1 and cache_k.shape[2] == WINDOW

    row = lambda v: v.reshape(1, -1)
    bf = lambda w: w.astype(BF16)
    w_out_ab0, w_out_c0 = bf(w_out_ab[0]), bf(w_out_c[0])
    nm, nf = norm_mix, norm_ffn
    qn, kn = row(q_norm[0]), row(k_norm[0])
    lng, lnb = row(gmlp_ln_g[0]), row(gmlp_ln_b[0])
    sink = attn_sink[0]

    sb, tabp = _bias_table(rel_bias)

    xs = x_sample.reshape(ns, D_MODEL)
    qx, knew_s, vnew_s, bm_s, gv_s, w_in_ab0 = _ab_sample_proj(
        xs, row(nm[0]), w_in_ab[0], qn, kn, lng, lnb, gmlp_w_s[0, :, 0, 0], gmlp_b_s[0, :, 0])
    to_t = lambda c: c.transpose(0, 2, 3, 1).reshape(ns, A_KV, WINDOW)
    xp, knew_p, vnew_p, gv_p, nk_s, nv_s, om = _ab_prompt(
        x_prompt, row(nm[0]), w_in_ab0, jnp.tile(qn, (1, A_HEADS)), jnp.tile(kn, (1, A_KV_HEADS)),
        sink, tabp, lng, lnb, gmlp_w_s[0], jnp.repeat(gmlp_b_s[0].T, B_GROUP_DIM, axis=1), w_out_ab0,
        to_t(cache_k[0]), to_t(cache_v[0]), qx, knew_s, vnew_s, sb)
    a_s = om.reshape(ns, A_Q)
    xp, xs = _ffn(xp.reshape(nb * seq, D_MODEL), xs, jnp.concatenate([a_s, bm_s], axis=-1), w_out_ab0,
                  row(nf[0]), w_gate, w_up, w_down, 0)

    q_s, f_s, i_s, sg_s, w_in_c0 = _hgrn_sample_proj(xs, row(nm[1]), w_in_c[0], c_lower_bounds,
                                                     ns // (nb * (seq // TQH)))
    xp, st_p, st_s, o_s = _hgrn_prompt(xp.reshape(nb, seq, D_MODEL), row(nm[1]), w_in_c0, c_lower_bounds,
                                       row(c_out_norm[0]), w_out_c0, state_hgrn[0], f_s, q_s, i_s, sg_s)
    xp, xs = _ffn(xp.reshape(nb * seq, D_MODEL), xs, o_s, w_out_c0, row(nf[1]), w_gate, w_up, w_down, 1)

    kv5 = lambda a: a.reshape(1, a.shape[0], WINDOW, A_KV_HEADS, A_HEAD_DIM)
    from_t = lambda a: a.reshape(ns, A_KV_HEADS, A_HEAD_DIM, WINDOW).transpose(0, 3, 1, 2)[None]
    return (xp.reshape(nb, seq, D_MODEL), xs.reshape(ns, 1, D_MODEL),
            kv5(knew_p), kv5(vnew_p), from_t(nk_s), from_t(nv_s),
            gv_p[None], gv_s.reshape(1, ns, 1, B_WIDTH),
            st_p[None], st_s[None])
```

```python
import functools
import math

import jax
import jax.numpy as jnp
import numpy as np
from jax import lax
from jax.experimental import pallas as pl
from jax.experimental.pallas import tpu as pltpu

F32 = jnp.float32
BF16 = jnp.bfloat16

D_MODEL = 1024
DEPTH = 2
A_HEADS = 8
A_KV_HEADS = 2
A_GROUP = A_HEADS // A_KV_HEADS
A_HEAD_DIM = 64
WINDOW = 128
ATTN_SCALE = A_HEAD_DIM ** -0.5
NUM_BUCKETS = 32
MAX_DISTANCE = 128
A_Q = A_HEADS * A_HEAD_DIM
A_KV = A_KV_HEADS * A_HEAD_DIM
B_GROUPS = 8
B_GROUP_DIM = 64
B_WIDTH = B_GROUPS * B_GROUP_DIM
B_CHUNK = 128
AB_IN = A_Q + 2 * A_KV + 2 * B_WIDTH
AB_MIX = A_Q + B_WIDTH
C_HEADS = 8
C_KEY_DIM = 128
C_VAL_DIM = 128
C_F = C_HEADS * C_KEY_DIM
C_V = C_HEADS * C_VAL_DIM
C_IN = 2 * C_F + 2 * C_V
D_FF = 2816
EPS = 1e-6

NEG = -1e30

VMEM_LIMIT = 56 * 1024 * 1024
VREG_ROWS = 8

CHUNK = 128
TQ = 512
TQH = 512
TM = 512
HEAD_SKEW = 2
SAFE_LOG2_RANGE = 64.0
PREFIX_GROUP = 4
FACTORED_UNROLL = 4
AB_UNROLL = 4

_NT = (((1,), (1,)), ((), ()))
_TN = (((0,), (0,)), ((), ()))


def _rms(x, g):
    return x * lax.rsqrt(jnp.mean(x * x, axis=-1, keepdims=True) + EPS) * g


def _gelu(x):
    return 0.5 * x * (1.0 + lax.erf(x * math.sqrt(0.5)))


def _layernorm(x, g, b):
    xc = x - jnp.mean(x, axis=-1, keepdims=True)
    return xc * lax.rsqrt(jnp.mean(xc * xc, axis=-1, keepdims=True) + EPS) * g + b


def _dot(a, b):
    return jnp.dot(a, b, preferred_element_type=F32)


def _full(shape):
    n = len(shape)
    return pl.BlockSpec(shape, lambda *_: (0,) * n)


def _resident(shape):
    n = len(shape)
    return pl.BlockSpec(shape, lambda *_: (0,) * n, pipeline_mode=pl.Buffered(1))


_SMEM = pl.BlockSpec(memory_space=pltpu.SMEM)


def _bias_table_kernel(rel_ref, sb_ref, tabp_ref):
    qi = lax.broadcasted_iota(jnp.int32, (WINDOW, 2 * WINDOW), 0)
    kj = lax.broadcasted_iota(jnp.int32, (WINDOW, 2 * WINDOW), 1)
    dist = qi + WINDOW - kj
    ok = (dist >= 0) & (dist < WINDOW)
    max_exact = NUM_BUCKETS // 2
    d = jnp.maximum(dist, 0)
    dl = jnp.maximum(d, 1).astype(F32)
    v = (jnp.log(dl / max_exact) / math.log(MAX_DISTANCE / max_exact) * (NUM_BUCKETS - max_exact))
    far = d >= max_exact
    hits = []
    for b in range(NUM_BUCKETS):
        if b < max_exact:
            hits.append(d == b)
        elif b < NUM_BUCKETS - 1:
            hits.append(far & (v >= b - max_exact) & (v < b - max_exact + 1))
        else:
            hits.append(far & (v >= b - max_exact))
    for h in range(A_HEADS):
        acc = jnp.zeros((WINDOW, 2 * WINDOW), F32)
        for b in range(NUM_BUCKETS):
            acc = jnp.where(hits[b], rel_ref[b, h], acc)
        t = jnp.where(ok, acc, NEG)
        last = pltpu.roll(t[WINDOW - 1:WINDOW, :], WINDOW + 1, 1)
        sb_ref[h:h + 1, :] = jnp.where(kj[0:1, :] <= WINDOW, last, 0.0)
        cols = slice((h % 2) * 2 * WINDOW, (h % 2 + 1) * 2 * WINDOW)
        tabp_ref[0, h // 2, :, cols] = t
        tabp_ref[1, h // 2, :, cols] = jnp.where(kj < WINDOW, NEG, t)


def _bias_table(rel_bias):
    return pl.pallas_call(
        _bias_table_kernel,
        out_shape=[
            jax.ShapeDtypeStruct((A_HEADS, 2 * WINDOW), F32),
            jax.ShapeDtypeStruct((2, A_HEADS // 2, WINDOW, 4 * WINDOW), F32),
        ],
        in_specs=[_SMEM],
        name="bias_table",
    )(rel_bias)


PAIR = 2 * A_HEAD_DIM
N_PAIRS = A_HEADS // 2


def _sample_cache_attention(ck_ref, cv_ref, qx_ref, kn_ref, vn_ref, sb_ref, sink_ref,
                            nk_ref, nv_ref, om_ref):
    wb = ck_ref.shape[2]
    first_group = lax.broadcasted_iota(jnp.int32, (1, A_HEADS, A_HEAD_DIM), 1) < A_GROUP
    newest = lax.broadcasted_iota(jnp.int32, (1, 1, wb), 2) == wb - 1
    sink = sink_ref[...][None]
    kc, vc = ck_ref[...], cv_ref[...]
    kn, vn = kn_ref[...], vn_ref[...]
    kn_cols, vn_cols = kn_ref[:, 0, :].T, vn_ref[:, 0, :].T
    for i in range(kc.shape[0]):
        nk_ref[i] = jnp.where(newest[0], kn_cols[:, i:i + 1], pltpu.roll(kc[i], wb - 1, 1))
        nv_ref[i] = jnp.where(newest[0], vn_cols[:, i:i + 1], pltpu.roll(vc[i], wb - 1, 1))
    q = qx_ref[...]
    s = jnp.einsum('bhd,bdk->bhk', q.astype(BF16), kc.astype(BF16), preferred_element_type=F32)
    s = s + sb_ref[:, 0:wb][None]
    sn = jnp.sum(q * kn, axis=-1, keepdims=True) + sb_ref[:, wb:wb + 1][None]
    m = jnp.maximum(jnp.maximum(jnp.max(s, axis=-1, keepdims=True), sn), sink)
    e = jnp.exp(s - m)
    en = jnp.exp(sn - m)
    r = 1.0 / (jnp.sum(e, axis=-1, keepdims=True) + en + jnp.exp(sink - m))
    o = jnp.einsum('bhk,bdk->bhd', (e * r).astype(BF16), vc.astype(BF16),
                   preferred_element_type=F32) + (en * r) * vn
    om_ref[...] = jnp.where(first_group, o[:, :, 0:A_HEAD_DIM], o[:, :, A_HEAD_DIM:A_KV])


def _ab_prompt_kernel(sink_ref, x_ref, nm_ref, win_ref, qg_ref, kg_ref, tabp_ref, lng_ref, lnb_ref,
                      ws_ref, bsp_ref, wout_ref,
                      ck_ref, cv_ref, qx_ref, kn_ref, vn_ref, sb_ref, sinkc_ref,
                      y_ref, knew_ref, vnew_ref, gv_ref, nk_ref, nv_ref, om_ref,
                      z_ref, mix_ref, q_ref, k_ref, kr_ref, v_ref, vr_ref, wpair_ref, kl_ref, vl_ref, gl_ref):
    j = pl.program_id(1)
    last_j = pl.num_programs(1) - 1
    n_chunks = TQ // CHUNK

    @pl.when(j == 0)
    def _():
        for ref in (k_ref, kr_ref, v_ref, vr_ref):
            ref[0:CHUNK, :] = jnp.zeros((CHUNK, A_KV), BF16)
        row = lax.broadcasted_iota(jnp.int32, (B_CHUNK, B_CHUNK), 0)
        col = lax.broadcasted_iota(jnp.int32, (B_CHUNK, B_CHUNK), 1)
        for g in range(B_GROUPS):
            wpair_ref[g // 2, :, (g % 2) * B_CHUNK:(g % 2 + 1) * B_CHUNK] = jnp.where(
                row >= col, ws_ref[g], 0.0).astype(BF16)

    h = _rms(x_ref[0], nm_ref[...]).astype(BF16)
    z_ref[...] = _dot(h, win_ref[...])

    _sample_cache_attention(ck_ref, cv_ref, qx_ref, kn_ref, vn_ref, sb_ref, sinkc_ref,
                            nk_ref, nv_ref, om_ref)

    lo_half = lax.broadcasted_iota(jnp.int32, (1, PAIR), 1) < A_HEAD_DIM

    def mean_sq_halves(x):
        x2 = x * x
        lo_sum = jnp.sum(jnp.where(lo_half, x2, 0.0), axis=-1, keepdims=True)
        hi_sum = jnp.sum(jnp.where(lo_half, 0.0, x2), axis=-1, keepdims=True)
        return jnp.where(lo_half, lo_sum, hi_sum) * (1.0 / A_HEAD_DIM)

    def block_diag(top, bot):
        zero = jnp.zeros_like(top)
        return jnp.concatenate([jnp.where(lo_half, top, zero), jnp.where(lo_half, zero, bot)], axis=0)

    kraw = z_ref[:, A_Q:A_Q + A_KV]
    v_all = z_ref[:, A_Q + A_KV:A_Q + 2 * A_KV]
    kn_all = kraw * lax.rsqrt(mean_sq_halves(kraw) + EPS) * kg_ref[...]
    k_ref[CHUNK:, :] = kn_all.astype(BF16)
    kr_ref[CHUNK:, :] = pltpu.roll(kn_all, A_HEAD_DIM, 1).astype(BF16)
    v_ref[CHUNK:, :] = v_all.astype(BF16)
    vr_ref[CHUNK:, :] = pltpu.roll(v_all, A_HEAD_DIM, 1).astype(BF16)
    kl_ref[...] = kn_all[TQ - CHUNK:, :]
    vl_ref[...] = v_all[TQ - CHUNK:, :]
    for i in range(N_PAIRS):
        ps = slice(i * PAIR, (i + 1) * PAIR)
        qraw = z_ref[:, ps]
        qn = qraw * lax.rsqrt(mean_sq_halves(qraw) + EPS) * (qg_ref[:, ps] * ATTN_SCALE)
        q_ref[:, ps] = qn.astype(BF16)

    def chunk(c, carry):
        r0 = pl.multiple_of(c * CHUNK, CHUNK)
        rows = pl.ds(r0, CHUNK)
        first = jnp.where(jnp.logical_and(j == 0, c == 0), 1, 0)

        both = pl.ds(r0, 2 * CHUNK)
        k2, k2r, v2, v2r = k_ref[both, :], kr_ref[both, :], v_ref[both, :], vr_ref[both, :]
        kbd = [block_diag(k2, k2r), block_diag(k2r, k2)]
        vbd = [block_diag(v2, v2r), block_diag(v2r, v2)]

        scores = []
        for i in range(N_PAIRS):
            s = lax.dot_general(q_ref[rows, i * PAIR:(i + 1) * PAIR], kbd[i // (A_GROUP // 2)], _NT,
                                preferred_element_type=F32)
            scores.append(s + tabp_ref[first, i])
        outs = []
        for i in range(N_PAIRS):
            es, rs = [], []
            for hh in range(2):
                sh = scores[i][:, hh * 2 * WINDOW:(hh + 1) * 2 * WINDOW]
                sk = sink_ref[2 * i + hh]
                m = jnp.maximum(jnp.max(sh, axis=-1, keepdims=True), sk)
                e = jnp.exp(sh - m)
                rs.append(1.0 / (jnp.sum(e, axis=-1, keepdims=True) + jnp.exp(sk - m)))
                es.append(e.astype(BF16))
            o = _dot(jnp.concatenate(es, axis=-1), vbd[i // (A_GROUP // 2)])
            outs.append(o * jnp.where(lo_half, rs[0], rs[1]))
        mix_ref[rows, 0:A_Q] = jnp.concatenate(outs, axis=-1).astype(BF16)

        zu = z_ref[rows, A_Q + 2 * A_KV:A_Q + 2 * A_KV + B_WIDTH]
        zv = z_ref[rows, A_Q + 2 * A_KV + B_WIDTH:AB_IN]
        u = _gelu(zu)
        vln = _layernorm(_gelu(zv), lng_ref[...], lnb_ref[...])
        vlb = vln.astype(BF16)
        sparts = []
        for i in range(B_GROUPS // 2):
            vpair = vlb[:, i * PAIR:(i + 1) * PAIR]
            sparts.append(_dot(wpair_ref[i], block_diag(vpair, vpair)))
        bm = u * (jnp.concatenate(sparts, axis=-1) + bsp_ref[...])
        mix_ref[rows, A_Q:AB_MIX] = bm.astype(BF16)

        gl_ref[...] = vln
        return carry

    lax.fori_loop(0, n_chunks, chunk, 0, unroll=AB_UNROLL)
    y_ref[0] = x_ref[0] + _dot(mix_ref[...], wout_ref[...])
    for ref in (k_ref, kr_ref, v_ref, vr_ref):
        ref[0:CHUNK, :] = ref[TQ:TQ + CHUNK, :]

    @pl.when(j == last_j)
    def _():
        knew_ref[0] = kl_ref[...]
        vnew_ref[0] = vl_ref[...]
        gv_ref[0] = gl_ref[...]


def _ab_prompt(x, nm, w_in, qg, kg, sink, tabp, lng, lnb, w_s, bsp, w_out, ck, cv, qx, kn_s, vn_s, sb):
    nb, seq, _ = x.shape
    nj = seq // TQ
    grid = (nb, nj)
    ns, _, wb = ck.shape
    sba = ns // (nb * nj)
    assert sba * nb * nj == ns
    blk = lambda b, j: (b, j, 0)
    per_b = lambda b, j: (b, 0, 0)
    step = lambda b, j: (b * nj + j, 0, 0)
    return pl.pallas_call(
        _ab_prompt_kernel,
        grid=grid,
        in_specs=[
            _SMEM,
            pl.BlockSpec((1, TQ, D_MODEL), blk),
            _full((1, D_MODEL)),
            _resident((D_MODEL, AB_IN)),
            _full((1, A_Q)),
            _full((1, A_KV)),
            _resident((2, N_PAIRS, WINDOW, 4 * WINDOW)),
            _full((1, B_WIDTH)),
            _full((1, B_WIDTH)),
            _resident((B_GROUPS, B_CHUNK, B_CHUNK)),
            _resident((B_CHUNK, B_WIDTH)),
            _resident((AB_MIX, D_MODEL)),
            pl.BlockSpec((sba, A_KV, wb), step),
            pl.BlockSpec((sba, A_KV, wb), step),
            pl.BlockSpec((sba, A_HEADS, A_KV), step),
            pl.BlockSpec((sba, 1, A_KV), step),
            pl.BlockSpec((sba, 1, A_KV), step),
            _full((A_HEADS, 2 * WINDOW)),
            _full((A_HEADS, 1)),
        ],
        out_specs=[
            pl.BlockSpec((1, TQ, D_MODEL), blk),
            pl.BlockSpec((1, WINDOW, A_KV), per_b),
            pl.BlockSpec((1, WINDOW, A_KV), per_b),
            pl.BlockSpec((1, B_CHUNK, B_WIDTH), per_b),
            pl.BlockSpec((sba, A_KV, wb), step),
            pl.BlockSpec((sba, A_KV, wb), step),
            pl.BlockSpec((sba, A_HEADS, A_HEAD_DIM), step),
        ],
        out_shape=[
            jax.ShapeDtypeStruct((nb, seq, D_MODEL), F32),
            jax.ShapeDtypeStruct((nb, WINDOW, A_KV), F32),
            jax.ShapeDtypeStruct((nb, WINDOW, A_KV), F32),
            jax.ShapeDtypeStruct((nb, B_CHUNK, B_WIDTH), F32),
            jax.ShapeDtypeStruct((ns, A_KV, wb), F32),
            jax.ShapeDtypeStruct((ns, A_KV, wb), F32),
            jax.ShapeDtypeStruct((ns, A_HEADS, A_HEAD_DIM), F32),
        ],
        scratch_shapes=[
            pltpu.VMEM((TQ, AB_IN), F32),
            pltpu.VMEM((TQ, AB_MIX), BF16),
            pltpu.VMEM((TQ, A_Q), BF16),
            pltpu.VMEM((CHUNK + TQ, A_KV), BF16),
            pltpu.VMEM((CHUNK + TQ, A_KV), BF16),
            pltpu.VMEM((CHUNK + TQ, A_KV), BF16),
            pltpu.VMEM((CHUNK + TQ, A_KV), BF16),
            pltpu.VMEM((B_GROUPS // 2, B_CHUNK, 2 * B_CHUNK), BF16),
            pltpu.VMEM((WINDOW, A_KV), F32),
            pltpu.VMEM((WINDOW, A_KV), F32),
            pltpu.VMEM((B_CHUNK, B_WIDTH), F32),
        ],
        compiler_params=pltpu.CompilerParams(
            dimension_semantics=("arbitrary", "arbitrary"), vmem_limit_bytes=VMEM_LIMIT),
        name="ab_prompt",
    )(sink, x, nm, w_in, qg, kg, tabp, lng, lnb, w_s, bsp, w_out,
      ck, cv, qx, kn_s[:, None, :], vn_s[:, None, :], sb, sink.reshape(A_HEADS, 1))


FF_TILE = 256


def _ffn_kernel(xp_ref, xs_ref, ms_ref, wo_ref, g_ref, wg_ref, wu_ref, wd_ref, yp_ref, ys_ref,
                wg_s, wu_s, wd_s, h0_s, acc_s, *, n_cast, n_prompt):
    s = pl.program_id(0)

    def gated(h, wg, wu):
        gate = _dot(h, wg)
        return (gate * jax.nn.sigmoid(gate) * _dot(h, wu)).astype(BF16)

    @pl.when(s == 0)
    def _():
        x = xp_ref[...]
        h0_s[...] = _rms(x, g_ref[...]).astype(BF16)
        acc_s[...] = x

    for c in range(n_cast):
        @pl.when(s == c)
        def _(c=c):
            tile = slice(c * FF_TILE, (c + 1) * FF_TILE)
            wg_t, wu_t, wd_t = (r[...].astype(BF16) for r in (wg_ref, wu_ref, wd_ref))
            wg_s[:, tile] = wg_t
            wu_s[:, tile] = wu_t
            wd_s[tile, :] = wd_t
            acc_s[...] += _dot(gated(h0_s[...], wg_t, wu_t), wd_t)

    @pl.when(s == n_cast - 1)
    def _():
        yp_ref[...] = acc_s[...]

    def swiglu(x):
        h = _rms(x, g_ref[...]).astype(BF16)
        return x + _dot(gated(h, wg_s[...], wu_s[...]), wd_s[...])

    @pl.when(jnp.logical_and(s >= n_cast, s < n_cast + n_prompt - 1))
    def _():
        yp_ref[...] = swiglu(xp_ref[...])

    @pl.when(s == n_cast + n_prompt - 1)
    def _():
        ys_ref[...] = swiglu(xs_ref[...] + _dot(ms_ref[...].astype(BF16), wo_ref[...]))


def _ffn(xp, xs, mix_s, w_o, g, w_gate, w_up, w_down, layer):
    rows, ns = xp.shape[0], xs.shape[0]
    n_cast, n_prompt = D_FF // FF_TILE, rows // TM
    w_tile = lambda s: jnp.minimum(s, n_cast - 1)
    row_blk = lambda s: (jnp.clip(s - (n_cast - 1), 0, n_prompt - 1), 0)
    return pl.pallas_call(
        functools.partial(_ffn_kernel, n_cast=n_cast, n_prompt=n_prompt),
        grid=(n_cast + n_prompt,),
        in_specs=[
            pl.BlockSpec((TM, D_MODEL), row_blk),
            _full((ns, D_MODEL)),
            _full((ns, D_MODEL)),
            _resident((D_MODEL, D_MODEL)),
            _full((1, D_MODEL)),
            pl.BlockSpec((None, D_MODEL, FF_TILE), lambda s: (layer, 0, w_tile(s))),
            pl.BlockSpec((None, D_MODEL, FF_TILE), lambda s: (layer, 0, w_tile(s))),
            pl.BlockSpec((None, FF_TILE, D_MODEL), lambda s: (layer, w_tile(s), 0)),
        ],
        out_specs=[pl.BlockSpec((TM, D_MODEL), row_blk), _full((ns, D_MODEL))],
        out_shape=[jax.ShapeDtypeStruct((rows, D_MODEL), F32), jax.ShapeDtypeStruct((ns, D_MODEL), F32)],
        scratch_shapes=[
            pltpu.VMEM((D_MODEL, D_FF), BF16),
            pltpu.VMEM((D_MODEL, D_FF), BF16),
            pltpu.VMEM((D_FF, D_MODEL), BF16),
            pltpu.VMEM((TM, D_MODEL), BF16),
            pltpu.VMEM((TM, D_MODEL), F32),
        ],
        compiler_params=pltpu.CompilerParams(
            dimension_semantics=("arbitrary",), vmem_limit_bytes=VMEM_LIMIT),
        name="ffn",
    )(xp, xs, mix_s, w_o, g, w_gate, w_up, w_down)


def _lower_bound(clb):
    m = jnp.max(clb, axis=0, keepdims=True)
    e = jnp.exp(clb - m)
    sm = e / jnp.sum(e, axis=0, keepdims=True)
    return (sm[0:1] + sm[1:2]) - sm[0:1]


def _split3(x):
    hi = x.astype(BF16)
    r = x - hi.astype(F32)
    mid = r.astype(BF16)
    lo = (r - mid.astype(F32)).astype(BF16)
    return hi, mid, lo


def _neg_abs(x):
    return lax.bitcast_convert_type(
        lax.bitcast_convert_type(x, jnp.uint32) | jnp.uint32(0x80000000), F32)


def _pair_level_table():
    t = np.arange(CHUNK)[:, None]
    s = np.arange(CHUNK)[None, :]
    lev = np.floor(np.log2(np.maximum(t ^ s, 1))).astype(np.int32)
    lev = np.where(t == s, -1, lev)
    return np.where(s > t, -2, lev).astype(np.int32)


def _level_operand(p, q, kk, f, b2):
    m = 2 ** p
    if m < VREG_ROWS:
        shape3 = (CHUNK // VREG_ROWS, VREG_ROWS, q.shape[1])
        sub = lax.broadcasted_iota(jnp.int32, (1, VREG_ROWS, q.shape[1]), 1)
        upper = ((sub >> p) & 1) == 1
        q3, k3 = q.reshape(shape3), kk.reshape(shape3)
        if p == 0:
            y = jnp.where(upper, q3 * f.reshape(shape3), k3)
        else:
            b3 = b2.reshape(shape3)
            be = b3[:, m - 1:m, :]
            for k in range(1, VREG_ROWS // (2 * m)):
                be = jnp.where(sub >= 2 * m * k, b3[:, 2 * m * k + m - 1:2 * m * k + m, :], be)
            y = jnp.where(upper, q3, k3) * jnp.exp2(_neg_abs(b3 - be))
        return y.reshape(q.shape).astype(BF16)
    parts = []
    for k in range(CHUNK // (2 * m)):
        lo = slice(2 * m * k, 2 * m * k + m)
        up = slice(2 * m * k + m, 2 * m * (k + 1))
        be = b2[2 * m * k + m - 1:2 * m * k + m, :]
        parts.append(kk[lo] * jnp.exp2(be - b2[lo]))
        parts.append(q[up] * jnp.exp2(b2[up] - be))
    return jnp.concatenate(parts, axis=0).astype(BF16)


def _merge_level(p, att, pm, lev):
    m = 2 ** p
    if m < VREG_ROWS:
        return jnp.where(lev == p, pm, att)
    col = lax.broadcasted_iota(jnp.int32, (1, CHUNK), 1)
    parts = []
    for k in range(CHUNK // (2 * m)):
        lo = slice(2 * m * k, 2 * m * k + m)
        up = slice(2 * m * k + m, 2 * m * (k + 1))
        parts.append(att[lo])
        parts.append(jnp.where((col >= 2 * m * k) & (col < 2 * m * k + m), pm[up], att[up]))
    return jnp.concatenate(parts, axis=0)


def _hgrn_prompt_kernel(x_ref, nm_ref, win_ref, clb_ref, on_ref, wout_ref, lev_ref,
                        ss_ref, fs_ref, qs_ref, is_ref, gs_ref,
                        y_ref, st_ref, sso_ref, os_ref,
                        z_ref, o_ref, stt_ref, k_ref):
    j = pl.program_id(1)
    last_j = pl.num_programs(1) - 1
    n_chunks = TQH // CHUNK
    n_levels = int(math.log2(CHUNK))

    @pl.when(j == 0)
    def _():
        stt_ref[...] = jnp.zeros_like(stt_ref)

    h = _rms(x_ref[0], nm_ref[...]).astype(BF16)
    z_ref[...] = _dot(h, win_ref[...])

    head_cols = lambda ref: [ref[:, hd * C_KEY_DIM:(hd + 1) * C_KEY_DIM].T for hd in range(C_HEADS)]
    f_cols, q_cols = head_cols(fs_ref), head_cols(qs_ref)
    out_rows = []
    for smp in range(ss_ref.shape[0]):
        parts = []
        for hd in range(C_HEADS):
            hs = slice(hd * C_VAL_DIM, (hd + 1) * C_VAL_DIM)
            fb = jnp.broadcast_to(f_cols[hd][:, smp:smp + 1], (C_KEY_DIM, C_VAL_DIM))
            sn = fb * ss_ref[smp, hd] + (1.0 - fb) * is_ref[smp:smp + 1, hs]
            sso_ref[smp, hd] = sn
            o = jnp.sum(q_cols[hd][:, smp:smp + 1] * sn, axis=0, keepdims=True)
            parts.append(_rms(o, on_ref[...]))
        out_rows.append(jnp.concatenate(parts, axis=-1))
    os_ref[...] = jnp.concatenate(out_rows, axis=0) * gs_ref[...]

    lb = _lower_bound(clb_ref[...])

    row = lax.broadcasted_iota(jnp.int32, (CHUNK, CHUNK), 0)
    col = lax.broadcasted_iota(jnp.int32, (CHUNK, CHUNK), 1)
    ltri = (row >= col).astype(BF16)

    def chunk_rows(c):
        return pl.ds(pl.multiple_of(c * CHUNK, CHUNK), CHUNK)

    def prefix(g, worst):
        rows = [chunk_rows(g * PREFIX_GROUP + i) for i in range(PREFIX_GROUP)]
        gates = [z_ref[r, C_F:2 * C_F] for r in rows]
        for r, gate in zip(rows, gates):
            f_all = lb + (1.0 - lb) * jax.nn.sigmoid(gate)
            k_ref[r, :] = 1.0 - f_all
            hi, mid, lo = _split3(jnp.log2(f_all))
            b2 = (_dot(ltri, hi) + _dot(ltri, mid)) + _dot(ltri, lo)
            z_ref[r, C_F:2 * C_F] = b2
            b_mid = b2[CHUNK // 2 - 1:CHUNK // 2, :]
            b_last = b2[CHUNK - 1:CHUNK, :]
            worst = jnp.maximum(worst, jnp.maximum(-b_mid, b_mid - b_last))
        return worst

    worst = lax.fori_loop(0, n_chunks // PREFIX_GROUP, prefix, jnp.zeros((1, C_F), F32))
    bounded = jnp.max(worst) <= SAFE_LOG2_RANGE

    def finish_head(rows, hd, o):
        gt = z_ref[rows, 2 * C_F + C_V + hd * C_VAL_DIM:2 * C_F + C_V + (hd + 1) * C_VAL_DIM]
        o = _rms(o, on_ref[...]) * jax.nn.sigmoid(gt)
        o_ref[rows, hd * C_VAL_DIM:(hd + 1) * C_VAL_DIM] = o.astype(BF16)

    def head_inputs(rows, hd):
        q = z_ref[rows, hd * C_KEY_DIM:(hd + 1) * C_KEY_DIM]
        kk = k_ref[rows, hd * C_KEY_DIM:(hd + 1) * C_KEY_DIM]
        b2 = z_ref[rows, C_F + hd * C_KEY_DIM:C_F + (hd + 1) * C_KEY_DIM]
        ivb = z_ref[rows, 2 * C_F + hd * C_VAL_DIM:2 * C_F + (hd + 1) * C_VAL_DIM].astype(BF16)
        return q, kk, b2, ivb

    def factored_chunk(c, carry):
        rows = chunk_rows(c)
        for hd in range(C_HEADS):
            q, kk, b2, ivb = head_inputs(rows, hd)
            b_mid = b2[CHUNK // 2 - 1:CHUNK // 2, :]
            b_last = b2[CHUNK - 1:CHUNK, :]
            qs = (q * jnp.exp2(b2 - b_mid)).astype(BF16)
            kd = (kk * jnp.exp2(b_mid - b2)).astype(BF16)
            att = jnp.where(row >= col, lax.dot_general(qs, kd, _NT, preferred_element_type=F32), 0.0)
            stt = stt_ref[hd]
            o = lax.dot_general(qs, (stt * jnp.exp2(b_mid)).astype(BF16), _NT,
                                preferred_element_type=F32) + _dot(att.astype(BF16), ivb)
            stt_ref[hd] = stt * jnp.exp2(b_last) + jnp.exp2(b_last - b_mid) * lax.dot_general(
                ivb, kd, _TN, preferred_element_type=F32)
            finish_head(rows, hd, o)
        return carry

    def tree_chunk(c, carry):
        rows = chunk_rows(c)
        lev = lev_ref[...]

        def products(hd):
            q, kk, b2, ivb = head_inputs(rows, hd)
            diag = jnp.sum(q * kk, axis=-1, keepdims=True)
            pms = []
            for p in range(n_levels):
                y = _level_operand(p, q, kk, 1.0 - kk, b2)
                pms.append(lax.dot_general(y, y, _NT, preferred_element_type=F32))
            stt = stt_ref[hd]
            o_prev = lax.dot_general((q * jnp.exp2(b2)).astype(BF16), stt.astype(BF16), _NT,
                                     preferred_element_type=F32)
            b_last = b2[CHUNK - 1:CHUNK, :]
            kd = (kk * jnp.exp2(b_last - b2)).astype(BF16)
            stt_ref[hd] = stt * jnp.exp2(b_last) + lax.dot_general(
                ivb, kd, _TN, preferred_element_type=F32)
            return diag, pms, o_prev, ivb

        def finish(hd, diag, pms, o_prev, ivb):
            att = jnp.where(lev == -1, diag, 0.0)
            for p in range(n_levels):
                att = _merge_level(p, att, pms[p], lev)
            finish_head(rows, hd, o_prev + _dot(att.astype(BF16), ivb))

        pending = [products(hd) for hd in range(HEAD_SKEW)]
        for hd in range(C_HEADS):
            if hd + HEAD_SKEW < C_HEADS:
                pending.append(products(hd + HEAD_SKEW))
            finish(hd, *pending.pop(0))
        return carry

    @pl.when(bounded)
    def _():
        lax.fori_loop(0, n_chunks, factored_chunk, 0, unroll=FACTORED_UNROLL)

    @pl.when(jnp.logical_not(bounded))
    def _():
        lax.fori_loop(0, n_chunks, tree_chunk, 0)

    y_ref[0] = x_ref[0] + _dot(o_ref[...], wout_ref[...])

    @pl.when(j == last_j)
    def _():
        for hd in range(C_HEADS):
            st_ref[0, hd] = stt_ref[hd].T


def _hgrn_prompt(x, nm, w_in, clb, on, w_out, state_s, f_s, q_s, i_s, g_s):
    nb, seq, _ = x.shape
    nj = seq // TQH
    ns = state_s.shape[0]
    sbh = f_s.shape[1]
    assert f_s.shape[0] == nb * nj and sbh * nb * nj == ns
    blk = lambda b, j: (b, j, 0)
    step = lambda b, j: (b * nj + j, 0, 0)
    step4 = lambda b, j: (b * nj + j, 0, 0, 0)
    y, st, st_s, o_s = pl.pallas_call(
        _hgrn_prompt_kernel,
        grid=(nb, nj),
        in_specs=[
            pl.BlockSpec((1, TQH, D_MODEL), blk),
            _full((1, D_MODEL)),
            _resident((D_MODEL, C_IN)),
            _full((DEPTH, C_F)),
            _full((1, C_VAL_DIM)),
            _resident((C_V, D_MODEL)),
            _full((CHUNK, CHUNK)),
            pl.BlockSpec((sbh, C_HEADS, C_KEY_DIM, C_VAL_DIM), step4),
            pl.BlockSpec((None, sbh, C_F), step),
            pl.BlockSpec((None, sbh, C_F), step),
            pl.BlockSpec((None, sbh, C_V), step),
            pl.BlockSpec((None, sbh, C_V), step),
        ],
        out_specs=[
            pl.BlockSpec((1, TQH, D_MODEL), blk),
            pl.BlockSpec((1, C_HEADS, C_KEY_DIM, C_VAL_DIM), lambda b, j: (b, 0, 0, 0)),
            pl.BlockSpec((sbh, C_HEADS, C_KEY_DIM, C_VAL_DIM), step4),
            pl.BlockSpec((None, sbh, C_V), step),
        ],
        out_shape=[
            jax.ShapeDtypeStruct((nb, seq, D_MODEL), F32),
            jax.ShapeDtypeStruct((nb, C_HEADS, C_KEY_DIM, C_VAL_DIM), F32),
            jax.ShapeDtypeStruct(state_s.shape, F32),
            jax.ShapeDtypeStruct((ns // sbh, sbh, C_V), F32),
        ],
        scratch_shapes=[
            pltpu.VMEM((TQH, C_IN), F32),
            pltpu.VMEM((TQH, C_V), BF16),
            pltpu.VMEM((C_HEADS, C_VAL_DIM, C_KEY_DIM), F32),
            pltpu.VMEM((TQH, C_F), F32),
        ],
        compiler_params=pltpu.CompilerParams(
            dimension_semantics=("arbitrary", "arbitrary"), vmem_limit_bytes=VMEM_LIMIT),
        name="hgrn_prompt",
    )(x, nm, w_in, clb, on, w_out, jnp.asarray(_pair_level_table()),
      state_s, f_s, q_s, i_s, g_s)
    return y, st, st_s, o_s.reshape(ns, C_V)


def _ab_sample_proj_kernel(w00_ref, b0_ref, x_ref, nm_ref, win_ref, qn_ref, kn_ref, lng_ref, lnb_ref,
                           qx_ref, knew_ref, vnew_ref, bm_ref, gv_ref, wbf_ref):
    n = x_ref.shape[0]
    wbf_ref[...] = win_ref[...].astype(BF16)
    h = _rms(x_ref[...], nm_ref[...]).astype(BF16)
    z = _dot(h, wbf_ref[...])
    zeros = jnp.zeros((n, A_HEAD_DIM), F32)
    for hh in range(A_HEADS):
        qh = _rms(z[:, hh * A_HEAD_DIM:(hh + 1) * A_HEAD_DIM], qn_ref[...]) * ATTN_SCALE
        qx_ref[:, hh, :] = jnp.concatenate([qh, zeros] if hh // A_GROUP == 0 else [zeros, qh], axis=-1)
    kparts = []
    for g in range(A_KV_HEADS):
        kparts.append(_rms(z[:, A_Q + g * A_HEAD_DIM:A_Q + (g + 1) * A_HEAD_DIM], kn_ref[...]))
    knew_ref[...] = jnp.concatenate(kparts, axis=-1)
    vnew_ref[...] = z[:, A_Q + A_KV:A_Q + 2 * A_KV]

    u = _gelu(z[:, A_Q + 2 * A_KV:A_Q + 2 * A_KV + B_WIDTH])
    vln = _layernorm(_gelu(z[:, A_Q + 2 * A_KV + B_WIDTH:AB_IN]), lng_ref[...], lnb_ref[...])
    grp = lax.broadcasted_iota(jnp.int32, (1, B_WIDTH), 1) // B_GROUP_DIM
    srow = jnp.zeros((1, B_WIDTH), F32)
    brow = jnp.zeros((1, B_WIDTH), F32)
    for g in range(B_GROUPS):
        srow = jnp.where(grp == g, w00_ref[g], srow)
        brow = jnp.where(grp == g, b0_ref[g], brow)
    bm_ref[...] = u * (vln * srow + brow)
    gv_ref[...] = vln


def _ab_sample_proj(x, nm, w_in, qn, kn, lng, lnb, w00, b0):
    n = x.shape[0]
    return pl.pallas_call(
        _ab_sample_proj_kernel,
        in_specs=[_SMEM, _SMEM] + [pl.BlockSpec(memory_space=pltpu.VMEM)] * 7,
        out_shape=[
            jax.ShapeDtypeStruct((n, A_HEADS, A_KV), F32),
            jax.ShapeDtypeStruct((n, A_KV), F32),
            jax.ShapeDtypeStruct((n, A_KV), F32),
            jax.ShapeDtypeStruct((n, B_WIDTH), F32),
            jax.ShapeDtypeStruct((n, B_WIDTH), F32),
            jax.ShapeDtypeStruct(w_in.shape, BF16),
        ],
        compiler_params=pltpu.CompilerParams(vmem_limit_bytes=VMEM_LIMIT),
        name="ab_sample_proj",
    )(w00, b0, x, nm, w_in, qn, kn, lng, lnb)


def _hgrn_sample_proj_kernel(x_ref, nm_ref, win_ref, clb_ref, q_ref, f_ref, i_ref, sg_ref, wbf_ref, h_s):
    t = pl.program_id(0)

    def put(ref, val):
        per = ref.shape[1]
        for i in range(ref.shape[0]):
            ref[i] = val[i * per:(i + 1) * per, :]

    @pl.when(t == 0)
    def _():
        h_s[...] = _rms(x_ref[...], nm_ref[...]).astype(BF16)

    wbf_ref[...] = win_ref[...].astype(BF16)
    z = _dot(h_s[...], wbf_ref[...])

    @pl.when(t == 0)
    def _():
        lb = _lower_bound(clb_ref[...])
        put(q_ref, z[:, 0:C_F])
        put(f_ref, lb + (1.0 - lb) * jax.nn.sigmoid(z[:, C_F:2 * C_F]))

    @pl.when(t == 1)
    def _():
        put(i_ref, z[:, 0:C_V])
        put(sg_ref, jax.nn.sigmoid(z[:, C_V:2 * C_V]))


def _hgrn_sample_proj(x, nm, w_in, clb, per_step):
    n = x.shape[0]
    assert C_F == C_V and C_IN == 4 * C_F
    grouped = lambda width: jax.ShapeDtypeStruct((n // per_step, per_step, width), F32)
    return pl.pallas_call(
        _hgrn_sample_proj_kernel,
        grid=(2,),
        in_specs=[
            _full((n, D_MODEL)),
            _full((1, D_MODEL)),
            pl.BlockSpec((D_MODEL, C_IN // 2), lambda t: (0, t)),
            _full((DEPTH, C_F)),
        ],
        out_specs=[_full((n // per_step, per_step, C_F))] * 2 + [_full((n // per_step, per_step, C_V))] * 2
        + [pl.BlockSpec((D_MODEL, C_IN // 2), lambda t: (0, t))],
        out_shape=[grouped(C_F)] * 2 + [grouped(C_V)] * 2 + [jax.ShapeDtypeStruct(w_in.shape, BF16)],
        scratch_shapes=[pltpu.VMEM((n, D_MODEL), BF16)],
        compiler_params=pltpu.CompilerParams(
            dimension_semantics=("arbitrary",), vmem_limit_bytes=VMEM_LIMIT),
        name="hgrn_sample_proj",
    )(x, nm, w_in, clb)


def kernel(x_prompt, x_sample, cache_k, cache_v, state_hgrn, norm_mix, norm_ffn, w_in_ab, w_out_ab,
           q_norm, k_norm, attn_sink, rel_bias, gmlp_ln_g, gmlp_ln_b, gmlp_w_s, gmlp_b_s,
           w_in_c, c_lower_bounds, c_out_norm, w_out_c, w_gate, w_up, w_down):
    assert norm_mix.shape[0] == DEPTH == 2 and w_in_ab.shape[0] == 1 and w_in_c.shape[0] == 1
    nb, seq, _ = x_prompt.shape
    ns = x_sample.shape[0]
    assert x_sample.shape[1] == 1 and cache_k.shape[2] == WINDOW

    row = lambda v: v.reshape(1, -1)
    bf = lambda w: w.astype(BF16)
    w_out_ab0, w_out_c0 = bf(w_out_ab[0]), bf(w_out_c[0])
    nm, nf = norm_mix, norm_ffn
    qn, kn = row(q_norm[0]), row(k_norm[0])
    lng, lnb = row(gmlp_ln_g[0]), row(gmlp_ln_b[0])
    sink = attn_sink[0]

    sb, tabp = _bias_table(rel_bias)

    xs = x_sample.reshape(ns, D_MODEL)
    qx, knew_s, vnew_s, bm_s, gv_s, w_in_ab0 = _ab_sample_proj(
        xs, row(nm[0]), w_in_ab[0], qn, kn, lng, lnb, gmlp_w_s[0, :, 0, 0], gmlp_b_s[0, :, 0])
    to_t = lambda c: c.transpose(0, 2, 3, 1).reshape(ns, A_KV, WINDOW)
    xp, knew_p, vnew_p, gv_p, nk_s, nv_s, om = _ab_prompt(
        x_prompt, row(nm[0]), w_in_ab0, jnp.tile(qn, (1, A_HEADS)), jnp.tile(kn, (1, A_KV_HEADS)),
        sink, tabp, lng, lnb, gmlp_w_s[0], jnp.repeat(gmlp_b_s[0].T, B_GROUP_DIM, axis=1), w_out_ab0,
        to_t(cache_k[0]), to_t(cache_v[0]), qx, knew_s, vnew_s, sb)
    a_s = om.reshape(ns, A_Q)
    xp, xs = _ffn(xp.reshape(nb * seq, D_MODEL), xs, jnp.concatenate([a_s, bm_s], axis=-1), w_out_ab0,
                  row(nf[0]), w_gate, w_up, w_down, 0)

    q_s, f_s, i_s, sg_s, w_in_c0 = _hgrn_sample_proj(xs, row(nm[1]), w_in_c[0], c_lower_bounds,
                                                     ns // (nb * (seq // TQH)))
    xp, st_p, st_s, o_s = _hgrn_prompt(xp.reshape(nb, seq, D_MODEL), row(nm[1]), w_in_c0, c_lower_bounds,
                                       row(c_out_norm[0]), w_out_c0, state_hgrn[0], f_s, q_s, i_s, sg_s)
    xp, xs = _ffn(xp.reshape(nb * seq, D_MODEL), xs, o_s, w_out_c0, row(nf[1]), w_gate, w_up, w_down, 1)

    kv5 = lambda a: a.reshape(1, a.shape[0], WINDOW, A_KV_HEADS, A_HEAD_DIM)
    from_t = lambda a: a.reshape(ns, A_KV_HEADS, A_HEAD_DIM, WINDOW).transpose(0, 3, 1, 2)[None]
    return (xp.reshape(nb, seq, D_MODEL), xs.reshape(ns, 1, D_MODEL),
            kv5(knew_p), kv5(vnew_p), from_t(nk_s), from_t(nv_s),
            gv_p[None], gv_s.reshape(1, ns, 1, B_WIDTH),
            st_p[None], st_s[None])
```

```python
import functools
import math

import jax
import jax.numpy as jnp
import numpy as np
from jax import lax
from jax.experimental import pallas as pl
from jax.experimental.pallas import tpu as pltpu

F32 = jnp.float32
BF16 = jnp.bfloat16

D_MODEL = 1024
DEPTH = 2
A_HEADS = 8
A_KV_HEADS = 2
A_GROUP = A_HEADS // A_KV_HEADS
A_HEAD_DIM = 64
WINDOW = 128
ATTN_SCALE = A_HEAD_DIM ** -0.5
NUM_BUCKETS = 32
MAX_DISTANCE = 128
A_Q = A_HEADS * A_HEAD_DIM
A_KV = A_KV_HEADS * A_HEAD_DIM
B_GROUPS = 8
B_GROUP_DIM = 64
B_WIDTH = B_GROUPS * B_GROUP_DIM
B_CHUNK = 128
AB_IN = A_Q + 2 * A_KV + 2 * B_WIDTH
AB_MIX = A_Q + B_WIDTH
C_HEADS = 8
C_KEY_DIM = 128
C_VAL_DIM = 128
C_F = C_HEADS * C_KEY_DIM
C_V = C_HEADS * C_VAL_DIM
C_IN = 2 * C_F + 2 * C_V
D_FF = 2816
EPS = 1e-6
LOG2E = math.log2(math.e)

NEG = -1e30

VMEM_LIMIT = 56 * 1024 * 1024
VREG_ROWS = 8

CHUNK = 128
TQ = 512
TQH = 512
TM = 512
HEAD_SKEW = 2
SAFE_LOG2_RANGE = 64.0
PREFIX_GROUP = 4
FACTORED_UNROLL = 4

_NT = (((1,), (1,)), ((), ()))
_TN = (((0,), (0,)), ((), ()))


def _rms(x, g):
    return x * lax.rsqrt(jnp.mean(x * x, axis=-1, keepdims=True) + EPS) * g


def _gelu(x):
    return 0.5 * x * (1.0 + lax.erf(x * math.sqrt(0.5)))


def _layernorm(x, g, b):
    xc = x - jnp.mean(x, axis=-1, keepdims=True)
    return xc * lax.rsqrt(jnp.mean(xc * xc, axis=-1, keepdims=True) + EPS) * g + b


def _dot(a, b):
    return jnp.dot(a, b, preferred_element_type=F32)


def _full(shape):
    n = len(shape)
    return pl.BlockSpec(shape, lambda *_: (0,) * n)


def _resident(shape):
    n = len(shape)
    return pl.BlockSpec(shape, lambda *_: (0,) * n, pipeline_mode=pl.Buffered(1))


_SMEM = pl.BlockSpec(memory_space=pltpu.SMEM)


def _bias_table_kernel(rel_ref, sb_ref, tabp_ref):
    qi = lax.broadcasted_iota(jnp.int32, (WINDOW, 2 * WINDOW), 0)
    kj = lax.broadcasted_iota(jnp.int32, (WINDOW, 2 * WINDOW), 1)
    dist = qi + WINDOW - kj
    ok = (dist >= 0) & (dist < WINDOW)
    max_exact = NUM_BUCKETS // 2
    d = jnp.maximum(dist, 0)
    dl = jnp.maximum(d, 1).astype(F32)
    v = (jnp.log(dl / max_exact) / math.log(MAX_DISTANCE / max_exact) * (NUM_BUCKETS - max_exact))
    far = d >= max_exact
    hits = []
    for b in range(NUM_BUCKETS):
        if b < max_exact:
            hits.append(d == b)
        elif b < NUM_BUCKETS - 1:
            hits.append(far & (v >= b - max_exact) & (v < b - max_exact + 1))
        else:
            hits.append(far & (v >= b - max_exact))
    for h in range(A_HEADS):
        acc = jnp.zeros((WINDOW, 2 * WINDOW), F32)
        for b in range(NUM_BUCKETS):
            acc = jnp.where(hits[b], rel_ref[b, h], acc)
        t = jnp.where(ok, acc, NEG)
        last = pltpu.roll(t[WINDOW - 1:WINDOW, :], WINDOW + 1, 1)
        sb_ref[h:h + 1, :] = jnp.where(kj[0:1, :] <= WINDOW, last, 0.0)
        cols = slice((h % 2) * 2 * WINDOW, (h % 2 + 1) * 2 * WINDOW)
        tabp_ref[0, h // 2, :, cols] = t * LOG2E
        tabp_ref[1, h // 2, :, cols] = jnp.where(kj < WINDOW, NEG, t * LOG2E)


def _bias_table(rel_bias):
    return pl.pallas_call(
        _bias_table_kernel,
        out_shape=[
            jax.ShapeDtypeStruct((A_HEADS, 2 * WINDOW), F32),
            jax.ShapeDtypeStruct((2, A_HEADS // 2, WINDOW, 4 * WINDOW), F32),
        ],
        in_specs=[_SMEM],
        name="bias_table",
    )(rel_bias)


PAIR = 2 * A_HEAD_DIM
N_PAIRS = A_HEADS // 2


def _sample_cache_attention(ck_ref, cv_ref, qx_ref, kn_ref, vn_ref, sb_ref, sink_ref,
                            nk_ref, nv_ref, om_ref):
    wb = ck_ref.shape[2]
    first_group = lax.broadcasted_iota(jnp.int32, (1, A_HEADS, A_HEAD_DIM), 1) < A_GROUP
    newest = lax.broadcasted_iota(jnp.int32, (1, 1, wb), 2) == wb - 1
    sink = sink_ref[...][None]
    kc, vc = ck_ref[...], cv_ref[...]
    kn, vn = kn_ref[...], vn_ref[...]
    kn_cols, vn_cols = kn_ref[:, 0, :].T, vn_ref[:, 0, :].T
    for i in range(kc.shape[0]):
        nk_ref[i] = jnp.where(newest[0], kn_cols[:, i:i + 1], pltpu.roll(kc[i], wb - 1, 1))
        nv_ref[i] = jnp.where(newest[0], vn_cols[:, i:i + 1], pltpu.roll(vc[i], wb - 1, 1))
    q = qx_ref[...]
    s = jnp.einsum('bhd,bdk->bhk', q.astype(BF16), kc.astype(BF16), preferred_element_type=F32)
    s = s + sb_ref[:, 0:wb][None]
    sn = jnp.sum(q * kn, axis=-1, keepdims=True) + sb_ref[:, wb:wb + 1][None]
    m = jnp.maximum(jnp.maximum(jnp.max(s, axis=-1, keepdims=True), sn), sink)
    e = jnp.exp(s - m)
    en = jnp.exp(sn - m)
    r = 1.0 / (jnp.sum(e, axis=-1, keepdims=True) + en + jnp.exp(sink - m))
    o = jnp.einsum('bhk,bdk->bhd', (e * r).astype(BF16), vc.astype(BF16),
                   preferred_element_type=F32) + (en * r) * vn
    om_ref[...] = jnp.where(first_group, o[:, :, 0:A_HEAD_DIM], o[:, :, A_HEAD_DIM:A_KV])


def _ab_prompt_kernel(sink_ref, x_ref, nm_ref, win_ref, qg_ref, kg_ref, tabp_ref, lng_ref, lnb_ref,
                      ws_ref, bsp_ref, wout_ref,
                      ck_ref, cv_ref, qx_ref, kn_ref, vn_ref, sb_ref, sinkc_ref,
                      y_ref, knew_ref, vnew_ref, gv_ref, nk_ref, nv_ref, om_ref,
                      z_ref, mix_ref, q_ref, k_ref, kr_ref, v_ref, vr_ref, wpair_ref, kl_ref, vl_ref, gl_ref):
    j = pl.program_id(1)
    last_j = pl.num_programs(1) - 1
    n_chunks = TQ // CHUNK

    @pl.when(j == 0)
    def _():
        for ref in (k_ref, kr_ref, v_ref, vr_ref):
            ref[0:CHUNK, :] = jnp.zeros((CHUNK, A_KV), BF16)
        row = lax.broadcasted_iota(jnp.int32, (B_CHUNK, B_CHUNK), 0)
        col = lax.broadcasted_iota(jnp.int32, (B_CHUNK, B_CHUNK), 1)
        for g in range(B_GROUPS):
            wpair_ref[g // 2, :, (g % 2) * B_CHUNK:(g % 2 + 1) * B_CHUNK] = jnp.where(
                row >= col, ws_ref[g], 0.0).astype(BF16)

    h = _rms(x_ref[0], nm_ref[...]).astype(BF16)
    z_ref[...] = _dot(h, win_ref[...])

    _sample_cache_attention(ck_ref, cv_ref, qx_ref, kn_ref, vn_ref, sb_ref, sinkc_ref,
                            nk_ref, nv_ref, om_ref)

    lo_half = lax.broadcasted_iota(jnp.int32, (1, PAIR), 1) < A_HEAD_DIM

    def mean_sq_halves(x):
        x2 = x * x
        lo_sum = jnp.sum(jnp.where(lo_half, x2, 0.0), axis=-1, keepdims=True)
        hi_sum = jnp.sum(jnp.where(lo_half, 0.0, x2), axis=-1, keepdims=True)
        return jnp.where(lo_half, lo_sum, hi_sum) * (1.0 / A_HEAD_DIM)

    def block_diag(top, bot):
        zero = jnp.zeros_like(top)
        return jnp.concatenate([jnp.where(lo_half, top, zero), jnp.where(lo_half, zero, bot)], axis=0)

    kraw = z_ref[:, A_Q:A_Q + A_KV]
    v_all = z_ref[:, A_Q + A_KV:A_Q + 2 * A_KV]
    kn_all = kraw * lax.rsqrt(mean_sq_halves(kraw) + EPS) * kg_ref[...]
    k_ref[CHUNK:, :] = kn_all.astype(BF16)
    kr_ref[CHUNK:, :] = pltpu.roll(kn_all, A_HEAD_DIM, 1).astype(BF16)
    v_ref[CHUNK:, :] = v_all.astype(BF16)
    vr_ref[CHUNK:, :] = pltpu.roll(v_all, A_HEAD_DIM, 1).astype(BF16)
    kl_ref[...] = kn_all[TQ - CHUNK:, :]
    vl_ref[...] = v_all[TQ - CHUNK:, :]
    for i in range(N_PAIRS):
        ps = slice(i * PAIR, (i + 1) * PAIR)
        qraw = z_ref[:, ps]
        qn = qraw * lax.rsqrt(mean_sq_halves(qraw) + EPS) * (qg_ref[:, ps] * (ATTN_SCALE * LOG2E))
        q_ref[:, ps] = qn.astype(BF16)

    def chunk(c, carry):
        r0 = c * CHUNK
        rows = pl.ds(r0, CHUNK)
        first = jnp.where(jnp.logical_and(j == 0, c == 0), 1, 0)

        both = pl.ds(r0, 2 * CHUNK)
        k2, k2r, v2, v2r = k_ref[both, :], kr_ref[both, :], v_ref[both, :], vr_ref[both, :]
        kbd = [block_diag(k2, k2r), block_diag(k2r, k2)]
        vbd = [block_diag(v2, v2r), block_diag(v2r, v2)]

        scores = []
        for i in range(N_PAIRS):
            s = lax.dot_general(q_ref[rows, i * PAIR:(i + 1) * PAIR], kbd[i // (A_GROUP // 2)], _NT,
                                preferred_element_type=F32)
            scores.append(s + tabp_ref[first, i])
        outs = []
        for i in range(N_PAIRS):
            es, rs = [], []
            for hh in range(2):
                sh = scores[i][:, hh * 2 * WINDOW:(hh + 1) * 2 * WINDOW]
                sk = sink_ref[2 * i + hh] * LOG2E
                m = jnp.maximum(jnp.max(sh, axis=-1, keepdims=True), sk)
                e = jnp.exp2(sh - m)
                rs.append(1.0 / (jnp.sum(e, axis=-1, keepdims=True) + jnp.exp2(sk - m)))
                es.append(e.astype(BF16))
            o = _dot(jnp.concatenate(es, axis=-1), vbd[i // (A_GROUP // 2)])
            outs.append(o * jnp.where(lo_half, rs[0], rs[1]))
        mix_ref[rows, 0:A_Q] = jnp.concatenate(outs, axis=-1).astype(BF16)

        zu = z_ref[rows, A_Q + 2 * A_KV:A_Q + 2 * A_KV + B_WIDTH]
        zv = z_ref[rows, A_Q + 2 * A_KV + B_WIDTH:AB_IN]
        u = _gelu(zu)
        vln = _layernorm(_gelu(zv), lng_ref[...], lnb_ref[...])
        vlb = vln.astype(BF16)
        sparts = []
        for i in range(B_GROUPS // 2):
            vpair = vlb[:, i * PAIR:(i + 1) * PAIR]
            sparts.append(_dot(wpair_ref[i], block_diag(vpair, vpair)))
        bm = u * (jnp.concatenate(sparts, axis=-1) + bsp_ref[...])
        mix_ref[rows, A_Q:AB_MIX] = bm.astype(BF16)

        gl_ref[...] = vln
        return carry

    for c in range(n_chunks):
        chunk(c, 0)
        done = slice(c * CHUNK, (c + 1) * CHUNK)
        y_ref[0, done, :] = x_ref[0, done, :] + _dot(mix_ref[done, :], wout_ref[...])
    for ref in (k_ref, kr_ref, v_ref, vr_ref):
        ref[0:CHUNK, :] = ref[TQ:TQ + CHUNK, :]

    @pl.when(j == last_j)
    def _():
        knew_ref[0] = kl_ref[...]
        vnew_ref[0] = vl_ref[...]
        gv_ref[0] = gl_ref[...]


def _ab_prompt(x, nm, w_in, qg, kg, sink, tabp, lng, lnb, w_s, bsp, w_out, ck, cv, qx, kn_s, vn_s, sb):
    nb, seq, _ = x.shape
    nj = seq // TQ
    grid = (nb, nj)
    ns, _, wb = ck.shape
    sba = ns // (nb * nj)
    assert sba * nb * nj == ns
    blk = lambda b, j: (b, j, 0)
    per_b = lambda b, j: (b, 0, 0)
    step = lambda b, j: (b * nj + j, 0, 0)
    return pl.pallas_call(
        _ab_prompt_kernel,
        grid=grid,
        in_specs=[
            _SMEM,
            pl.BlockSpec((1, TQ, D_MODEL), blk),
            _full((1, D_MODEL)),
            _resident((D_MODEL, AB_IN)),
            _full((1, A_Q)),
            _full((1, A_KV)),
            _resident((2, N_PAIRS, WINDOW, 4 * WINDOW)),
            _full((1, B_WIDTH)),
            _full((1, B_WIDTH)),
            _resident((B_GROUPS, B_CHUNK, B_CHUNK)),
            _resident((B_CHUNK, B_WIDTH)),
            _resident((AB_MIX, D_MODEL)),
            pl.BlockSpec((sba, A_KV, wb), step),
            pl.BlockSpec((sba, A_KV, wb), step),
            pl.BlockSpec((sba, A_HEADS, A_KV), step),
            pl.BlockSpec((sba, 1, A_KV), step),
            pl.BlockSpec((sba, 1, A_KV), step),
            _full((A_HEADS, 2 * WINDOW)),
            _full((A_HEADS, 1)),
        ],
        out_specs=[
            pl.BlockSpec((1, TQ, D_MODEL), blk),
            pl.BlockSpec((1, WINDOW, A_KV), per_b),
            pl.BlockSpec((1, WINDOW, A_KV), per_b),
            pl.BlockSpec((1, B_CHUNK, B_WIDTH), per_b),
            pl.BlockSpec((sba, A_KV, wb), step),
            pl.BlockSpec((sba, A_KV, wb), step),
            pl.BlockSpec((sba, A_HEADS, A_HEAD_DIM), step),
        ],
        out_shape=[
            jax.ShapeDtypeStruct((nb, seq, D_MODEL), F32),
            jax.ShapeDtypeStruct((nb, WINDOW, A_KV), F32),
            jax.ShapeDtypeStruct((nb, WINDOW, A_KV), F32),
            jax.ShapeDtypeStruct((nb, B_CHUNK, B_WIDTH), F32),
            jax.ShapeDtypeStruct((ns, A_KV, wb), F32),
            jax.ShapeDtypeStruct((ns, A_KV, wb), F32),
            jax.ShapeDtypeStruct((ns, A_HEADS, A_HEAD_DIM), F32),
        ],
        scratch_shapes=[
            pltpu.VMEM((TQ, AB_IN), F32),
            pltpu.VMEM((TQ, AB_MIX), BF16),
            pltpu.VMEM((TQ, A_Q), BF16),
            pltpu.VMEM((CHUNK + TQ, A_KV), BF16),
            pltpu.VMEM((CHUNK + TQ, A_KV), BF16),
            pltpu.VMEM((CHUNK + TQ, A_KV), BF16),
            pltpu.VMEM((CHUNK + TQ, A_KV), BF16),
            pltpu.VMEM((B_GROUPS // 2, B_CHUNK, 2 * B_CHUNK), BF16),
            pltpu.VMEM((WINDOW, A_KV), F32),
            pltpu.VMEM((WINDOW, A_KV), F32),
            pltpu.VMEM((B_CHUNK, B_WIDTH), F32),
        ],
        compiler_params=pltpu.CompilerParams(
            dimension_semantics=("arbitrary", "arbitrary"), vmem_limit_bytes=VMEM_LIMIT),
        name="ab_prompt",
    )(sink, x, nm, w_in, qg, kg, tabp, lng, lnb, w_s, bsp, w_out,
      ck, cv, qx, kn_s[:, None, :], vn_s[:, None, :], sb, sink.reshape(A_HEADS, 1))


FF_TILE = 256


def _ffn_kernel(xp_ref, xs_ref, ms_ref, wo_ref, g_ref, wg_ref, wu_ref, wd_ref, yp_ref, ys_ref,
                wg_s, wu_s, wd_s, h0_s, acc_s, *, n_cast, n_prompt):
    s = pl.program_id(0)

    def gated(h, wg, wu):
        gate = _dot(h, wg)
        return (gate * jax.nn.sigmoid(gate) * _dot(h, wu)).astype(BF16)

    @pl.when(s == 0)
    def _():
        x = xp_ref[...]
        h0_s[...] = _rms(x, g_ref[...]).astype(BF16)
        acc_s[...] = x

    for c in range(n_cast):
        @pl.when(s == c)
        def _(c=c):
            tile = slice(c * FF_TILE, (c + 1) * FF_TILE)
            wg_t, wu_t, wd_t = (r[...].astype(BF16) for r in (wg_ref, wu_ref, wd_ref))
            wg_s[:, tile] = wg_t
            wu_s[:, tile] = wu_t
            wd_s[tile, :] = wd_t
            acc_s[...] += _dot(gated(h0_s[...], wg_t, wu_t), wd_t)

    @pl.when(s == n_cast - 1)
    def _():
        yp_ref[...] = acc_s[...]

    def swiglu(x):
        h = _rms(x, g_ref[...]).astype(BF16)
        return x + _dot(gated(h, wg_s[...], wu_s[...]), wd_s[...])

    @pl.when(jnp.logical_and(s >= n_cast, s < n_cast + n_prompt - 1))
    def _():
        yp_ref[...] = swiglu(xp_ref[...])

    @pl.when(s == n_cast + n_prompt - 1)
    def _():
        ys_ref[...] = swiglu(xs_ref[...] + _dot(ms_ref[...].astype(BF16), wo_ref[...]))


def _ffn(xp, xs, mix_s, w_o, g, w_gate, w_up, w_down, layer):
    rows, ns = xp.shape[0], xs.shape[0]
    n_cast, n_prompt = D_FF // FF_TILE, rows // TM
    w_tile = lambda s: jnp.minimum(s, n_cast - 1)
    row_blk = lambda s: (jnp.clip(s - (n_cast - 1), 0, n_prompt - 1), 0)
    return pl.pallas_call(
        functools.partial(_ffn_kernel, n_cast=n_cast, n_prompt=n_prompt),
        grid=(n_cast + n_prompt,),
        in_specs=[
            pl.BlockSpec((TM, D_MODEL), row_blk),
            _full((ns, D_MODEL)),
            _full((ns, D_MODEL)),
            _resident((D_MODEL, D_MODEL)),
            _full((1, D_MODEL)),
            pl.BlockSpec((None, D_MODEL, FF_TILE), lambda s: (layer, 0, w_tile(s))),
            pl.BlockSpec((None, D_MODEL, FF_TILE), lambda s: (layer, 0, w_tile(s))),
            pl.BlockSpec((None, FF_TILE, D_MODEL), lambda s: (layer, w_tile(s), 0)),
        ],
        out_specs=[pl.BlockSpec((TM, D_MODEL), row_blk), _full((ns, D_MODEL))],
        out_shape=[jax.ShapeDtypeStruct((rows, D_MODEL), F32), jax.ShapeDtypeStruct((ns, D_MODEL), F32)],
        scratch_shapes=[
            pltpu.VMEM((D_MODEL, D_FF), BF16),
            pltpu.VMEM((D_MODEL, D_FF), BF16),
            pltpu.VMEM((D_FF, D_MODEL), BF16),
            pltpu.VMEM((TM, D_MODEL), BF16),
            pltpu.VMEM((TM, D_MODEL), F32),
        ],
        compiler_params=pltpu.CompilerParams(
            dimension_semantics=("arbitrary",), vmem_limit_bytes=VMEM_LIMIT),
        name="ffn",
    )(xp, xs, mix_s, w_o, g, w_gate, w_up, w_down)


def _lower_bound(clb):
    m = jnp.max(clb, axis=0, keepdims=True)
    e = jnp.exp(clb - m)
    sm = e / jnp.sum(e, axis=0, keepdims=True)
    return (sm[0:1] + sm[1:2]) - sm[0:1]


def _split3(x):
    hi = x.astype(BF16)
    r = x - hi.astype(F32)
    mid = r.astype(BF16)
    lo = (r - mid.astype(F32)).astype(BF16)
    return hi, mid, lo


def _neg_abs(x):
    return lax.bitcast_convert_type(
        lax.bitcast_convert_type(x, jnp.uint32) | jnp.uint32(0x80000000), F32)


def _pair_level_table():
    t = np.arange(CHUNK)[:, None]
    s = np.arange(CHUNK)[None, :]
    lev = np.floor(np.log2(np.maximum(t ^ s, 1))).astype(np.int32)
    lev = np.where(t == s, -1, lev)
    return np.where(s > t, -2, lev).astype(np.int32)


def _level_operand(p, q, kk, f, b2):
    m = 2 ** p
    if m < VREG_ROWS:
        shape3 = (CHUNK // VREG_ROWS, VREG_ROWS, q.shape[1])
        sub = lax.broadcasted_iota(jnp.int32, (1, VREG_ROWS, q.shape[1]), 1)
        upper = ((sub >> p) & 1) == 1
        q3, k3 = q.reshape(shape3), kk.reshape(shape3)
        if p == 0:
            y = jnp.where(upper, q3 * f.reshape(shape3), k3)
        else:
            b3 = b2.reshape(shape3)
            be = b3[:, m - 1:m, :]
            for k in range(1, VREG_ROWS // (2 * m)):
                be = jnp.where(sub >= 2 * m * k, b3[:, 2 * m * k + m - 1:2 * m * k + m, :], be)
            y = jnp.where(upper, q3, k3) * jnp.exp2(_neg_abs(b3 - be))
        return y.reshape(q.shape).astype(BF16)
    parts = []
    for k in range(CHUNK // (2 * m)):
        lo = slice(2 * m * k, 2 * m * k + m)
        up = slice(2 * m * k + m, 2 * m * (k + 1))
        be = b2[2 * m * k + m - 1:2 * m * k + m, :]
        parts.append(kk[lo] * jnp.exp2(be - b2[lo]))
        parts.append(q[up] * jnp.exp2(b2[up] - be))
    return jnp.concatenate(parts, axis=0).astype(BF16)


def _merge_level(p, att, pm, lev):
    m = 2 ** p
    if m < VREG_ROWS:
        return jnp.where(lev == p, pm, att)
    col = lax.broadcasted_iota(jnp.int32, (1, CHUNK), 1)
    parts = []
    for k in range(CHUNK // (2 * m)):
        lo = slice(2 * m * k, 2 * m * k + m)
        up = slice(2 * m * k + m, 2 * m * (k + 1))
        parts.append(att[lo])
        parts.append(jnp.where((col >= 2 * m * k) & (col < 2 * m * k + m), pm[up], att[up]))
    return jnp.concatenate(parts, axis=0)


def _hgrn_prompt_kernel(x_ref, nm_ref, win_ref, clb_ref, on_ref, wout_ref, lev_ref,
                        ss_ref, fs_ref, qs_ref, is_ref, gs_ref,
                        y_ref, st_ref, sso_ref, os_ref,
                        z_ref, o_ref, stt_ref, k_ref):
    j = pl.program_id(1)
    last_j = pl.num_programs(1) - 1
    n_chunks = TQH // CHUNK
    n_levels = int(math.log2(CHUNK))

    @pl.when(j == 0)
    def _():
        stt_ref[...] = jnp.zeros_like(stt_ref)

    h = _rms(x_ref[0], nm_ref[...]).astype(BF16)
    z_ref[...] = _dot(h, win_ref[...])

    head_cols = lambda ref: [ref[:, hd * C_KEY_DIM:(hd + 1) * C_KEY_DIM].T for hd in range(C_HEADS)]
    f_cols, q_cols = head_cols(fs_ref), head_cols(qs_ref)
    out_rows = []
    for smp in range(ss_ref.shape[0]):
        parts = []
        for hd in range(C_HEADS):
            hs = slice(hd * C_VAL_DIM, (hd + 1) * C_VAL_DIM)
            fb = jnp.broadcast_to(f_cols[hd][:, smp:smp + 1], (C_KEY_DIM, C_VAL_DIM))
            sn = fb * ss_ref[smp, hd] + (1.0 - fb) * is_ref[smp:smp + 1, hs]
            sso_ref[smp, hd] = sn
            o = jnp.sum(q_cols[hd][:, smp:smp + 1] * sn, axis=0, keepdims=True)
            parts.append(_rms(o, on_ref[...]))
        out_rows.append(jnp.concatenate(parts, axis=-1))
    os_ref[...] = jnp.concatenate(out_rows, axis=0) * gs_ref[...]

    lb = _lower_bound(clb_ref[...])

    row = lax.broadcasted_iota(jnp.int32, (CHUNK, CHUNK), 0)
    col = lax.broadcasted_iota(jnp.int32, (CHUNK, CHUNK), 1)
    ltri = (row >= col).astype(BF16)

    def chunk_rows(c):
        return pl.ds(pl.multiple_of(c * CHUNK, CHUNK), CHUNK)

    def prefix(g, worst):
        rows = [chunk_rows(g * PREFIX_GROUP + i) for i in range(PREFIX_GROUP)]
        gates = [z_ref[r, C_F:2 * C_F] for r in rows]
        for r, gate in zip(rows, gates):
            f_all = lb + (1.0 - lb) * jax.nn.sigmoid(gate)
            k_ref[r, :] = 1.0 - f_all
            hi, mid, lo = _split3(jnp.log2(f_all))
            b2 = (_dot(ltri, hi) + _dot(ltri, mid)) + _dot(ltri, lo)
            z_ref[r, C_F:2 * C_F] = b2
            b_mid = b2[CHUNK // 2 - 1:CHUNK // 2, :]
            b_last = b2[CHUNK - 1:CHUNK, :]
            worst = jnp.maximum(worst, jnp.maximum(-b_mid, b_mid - b_last))
        return worst

    worst = lax.fori_loop(0, n_chunks // PREFIX_GROUP, prefix, jnp.zeros((1, C_F), F32))
    bounded = jnp.max(worst) <= SAFE_LOG2_RANGE

    def finish_head(rows, hd, o):
        gt = z_ref[rows, 2 * C_F + C_V + hd * C_VAL_DIM:2 * C_F + C_V + (hd + 1) * C_VAL_DIM]
        o = _rms(o, on_ref[...]) * jax.nn.sigmoid(gt)
        o_ref[rows, hd * C_VAL_DIM:(hd + 1) * C_VAL_DIM] = o.astype(BF16)

    def head_inputs(rows, hd):
        q = z_ref[rows, hd * C_KEY_DIM:(hd + 1) * C_KEY_DIM]
        kk = k_ref[rows, hd * C_KEY_DIM:(hd + 1) * C_KEY_DIM]
        b2 = z_ref[rows, C_F + hd * C_KEY_DIM:C_F + (hd + 1) * C_KEY_DIM]
        ivb = z_ref[rows, 2 * C_F + hd * C_VAL_DIM:2 * C_F + (hd + 1) * C_VAL_DIM].astype(BF16)
        return q, kk, b2, ivb

    def factored_chunk(c, carry):
        rows = chunk_rows(c)
        for hd in range(C_HEADS):
            q, kk, b2, ivb = head_inputs(rows, hd)
            b_mid = b2[CHUNK // 2 - 1:CHUNK // 2, :]
            b_last = b2[CHUNK - 1:CHUNK, :]
            qs = (q * jnp.exp2(b2 - b_mid)).astype(BF16)
            kd = (kk * jnp.exp2(b_mid - b2)).astype(BF16)
            att = jnp.where(row >= col, lax.dot_general(qs, kd, _NT, preferred_element_type=F32), 0.0)
            stt = stt_ref[hd]
            o = lax.dot_general(qs, (stt * jnp.exp2(b_mid)).astype(BF16), _NT,
                                preferred_element_type=F32) + _dot(att.astype(BF16), ivb)
            stt_ref[hd] = stt * jnp.exp2(b_last) + jnp.exp2(b_last - b_mid) * lax.dot_general(
                ivb, kd, _TN, preferred_element_type=F32)
            finish_head(rows, hd, o)
        return carry

    def tree_chunk(c, carry):
        rows = chunk_rows(c)
        lev = lev_ref[...]

        def products(hd):
            q, kk, b2, ivb = head_inputs(rows, hd)
            diag = jnp.sum(q * kk, axis=-1, keepdims=True)
            pms = []
            for p in range(n_levels):
                y = _level_operand(p, q, kk, 1.0 - kk, b2)
                pms.append(lax.dot_general(y, y, _NT, preferred_element_type=F32))
            stt = stt_ref[hd]
            o_prev = lax.dot_general((q * jnp.exp2(b2)).astype(BF16), stt.astype(BF16), _NT,
                                     preferred_element_type=F32)
            b_last = b2[CHUNK - 1:CHUNK, :]
            kd = (kk * jnp.exp2(b_last - b2)).astype(BF16)
            stt_ref[hd] = stt * jnp.exp2(b_last) + lax.dot_general(
                ivb, kd, _TN, preferred_element_type=F32)
            return diag, pms, o_prev, ivb

        def finish(hd, diag, pms, o_prev, ivb):
            att = jnp.where(lev == -1, diag, 0.0)
            for p in range(n_levels):
                att = _merge_level(p, att, pms[p], lev)
            finish_head(rows, hd, o_prev + _dot(att.astype(BF16), ivb))

        pending = [products(hd) for hd in range(HEAD_SKEW)]
        for hd in range(C_HEADS):
            if hd + HEAD_SKEW < C_HEADS:
                pending.append(products(hd + HEAD_SKEW))
            finish(hd, *pending.pop(0))
        return carry

    @pl.when(bounded)
    def _():
        lax.fori_loop(0, n_chunks, factored_chunk, 0, unroll=FACTORED_UNROLL)

    @pl.when(jnp.logical_not(bounded))
    def _():
        lax.fori_loop(0, n_chunks, tree_chunk, 0)

    y_ref[0] = x_ref[0] + _dot(o_ref[...], wout_ref[...])

    @pl.when(j == last_j)
    def _():
        for hd in range(C_HEADS):
            st_ref[0, hd] = stt_ref[hd].T


def _hgrn_prompt(x, nm, w_in, clb, on, w_out, state_s, f_s, q_s, i_s, g_s):
    nb, seq, _ = x.shape
    nj = seq // TQH
    ns = state_s.shape[0]
    sbh = f_s.shape[1]
    assert f_s.shape[0] == nb * nj and sbh * nb * nj == ns
    blk = lambda b, j: (b, j, 0)
    step = lambda b, j: (b * nj + j, 0, 0)
    step4 = lambda b, j: (b * nj + j, 0, 0, 0)
    y, st, st_s, o_s = pl.pallas_call(
        _hgrn_prompt_kernel,
        grid=(nb, nj),
        in_specs=[
            pl.BlockSpec((1, TQH, D_MODEL), blk),
            _full((1, D_MODEL)),
            _resident((D_MODEL, C_IN)),
            _full((DEPTH, C_F)),
            _full((1, C_VAL_DIM)),
            _resident((C_V, D_MODEL)),
            _full((CHUNK, CHUNK)),
            pl.BlockSpec((sbh, C_HEADS, C_KEY_DIM, C_VAL_DIM), step4),
            pl.BlockSpec((None, sbh, C_F), step),
            pl.BlockSpec((None, sbh, C_F), step),
            pl.BlockSpec((None, sbh, C_V), step),
            pl.BlockSpec((None, sbh, C_V), step),
        ],
        out_specs=[
            pl.BlockSpec((1, TQH, D_MODEL), blk),
            pl.BlockSpec((1, C_HEADS, C_KEY_DIM, C_VAL_DIM), lambda b, j: (b, 0, 0, 0)),
            pl.BlockSpec((sbh, C_HEADS, C_KEY_DIM, C_VAL_DIM), step4),
            pl.BlockSpec((None, sbh, C_V), step),
        ],
        out_shape=[
            jax.ShapeDtypeStruct((nb, seq, D_MODEL), F32),
            jax.ShapeDtypeStruct((nb, C_HEADS, C_KEY_DIM, C_VAL_DIM), F32),
            jax.ShapeDtypeStruct(state_s.shape, F32),
            jax.ShapeDtypeStruct((ns // sbh, sbh, C_V), F32),
        ],
        scratch_shapes=[
            pltpu.VMEM((TQH, C_IN), F32),
            pltpu.VMEM((TQH, C_V), BF16),
            pltpu.VMEM((C_HEADS, C_VAL_DIM, C_KEY_DIM), F32),
            pltpu.VMEM((TQH, C_F), F32),
        ],
        compiler_params=pltpu.CompilerParams(
            dimension_semantics=("arbitrary", "arbitrary"), vmem_limit_bytes=VMEM_LIMIT),
        name="hgrn_prompt",
    )(x, nm, w_in, clb, on, w_out, jnp.asarray(_pair_level_table()),
      state_s, f_s, q_s, i_s, g_s)
    return y, st, st_s, o_s.reshape(ns, C_V)


def _ab_sample_proj_kernel(w00_ref, b0_ref, x_ref, nm_ref, win_ref, qn_ref, kn_ref, lng_ref, lnb_ref,
                           qx_ref, knew_ref, vnew_ref, bm_ref, gv_ref, wbf_ref):
    n = x_ref.shape[0]
    wbf_ref[...] = win_ref[...].astype(BF16)
    h = _rms(x_ref[...], nm_ref[...]).astype(BF16)
    z = _dot(h, wbf_ref[...])
    zeros = jnp.zeros((n, A_HEAD_DIM), F32)
    for hh in range(A_HEADS):
        qh = _rms(z[:, hh * A_HEAD_DIM:(hh + 1) * A_HEAD_DIM], qn_ref[...]) * ATTN_SCALE
        qx_ref[:, hh, :] = jnp.concatenate([qh, zeros] if hh // A_GROUP == 0 else [zeros, qh], axis=-1)
    kparts = []
    for g in range(A_KV_HEADS):
        kparts.append(_rms(z[:, A_Q + g * A_HEAD_DIM:A_Q + (g + 1) * A_HEAD_DIM], kn_ref[...]))
    knew_ref[...] = jnp.concatenate(kparts, axis=-1)
    vnew_ref[...] = z[:, A_Q + A_KV:A_Q + 2 * A_KV]

    u = _gelu(z[:, A_Q + 2 * A_KV:A_Q + 2 * A_KV + B_WIDTH])
    vln = _layernorm(_gelu(z[:, A_Q + 2 * A_KV + B_WIDTH:AB_IN]), lng_ref[...], lnb_ref[...])
    grp = lax.broadcasted_iota(jnp.int32, (1, B_WIDTH), 1) // B_GROUP_DIM
    srow = jnp.zeros((1, B_WIDTH), F32)
    brow = jnp.zeros((1, B_WIDTH), F32)
    for g in range(B_GROUPS):
        srow = jnp.where(grp == g, w00_ref[g], srow)
        brow = jnp.where(grp == g, b0_ref[g], brow)
    bm_ref[...] = u * (vln * srow + brow)
    gv_ref[...] = vln


def _ab_sample_proj(x, nm, w_in, qn, kn, lng, lnb, w00, b0):
    n = x.shape[0]
    return pl.pallas_call(
        _ab_sample_proj_kernel,
        in_specs=[_SMEM, _SMEM] + [pl.BlockSpec(memory_space=pltpu.VMEM)] * 7,
        out_shape=[
            jax.ShapeDtypeStruct((n, A_HEADS, A_KV), F32),
            jax.ShapeDtypeStruct((n, A_KV), F32),
            jax.ShapeDtypeStruct((n, A_KV), F32),
            jax.ShapeDtypeStruct((n, B_WIDTH), F32),
            jax.ShapeDtypeStruct((n, B_WIDTH), F32),
            jax.ShapeDtypeStruct(w_in.shape, BF16),
        ],
        compiler_params=pltpu.CompilerParams(vmem_limit_bytes=VMEM_LIMIT),
        name="ab_sample_proj",
    )(w00, b0, x, nm, w_in, qn, kn, lng, lnb)


def _hgrn_sample_proj_kernel(x_ref, nm_ref, win_ref, clb_ref, q_ref, f_ref, i_ref, sg_ref, wbf_ref, h_s):
    t = pl.program_id(0)

    def put(ref, val):
        per = ref.shape[1]
        for i in range(ref.shape[0]):
            ref[i] = val[i * per:(i + 1) * per, :]

    @pl.when(t == 0)
    def _():
        h_s[...] = _rms(x_ref[...], nm_ref[...]).astype(BF16)

    wbf_ref[...] = win_ref[...].astype(BF16)
    z = _dot(h_s[...], wbf_ref[...])

    @pl.when(t == 0)
    def _():
        lb = _lower_bound(clb_ref[...])
        put(q_ref, z[:, 0:C_F])
        put(f_ref, lb + (1.0 - lb) * jax.nn.sigmoid(z[:, C_F:2 * C_F]))

    @pl.when(t == 1)
    def _():
        put(i_ref, z[:, 0:C_V])
        put(sg_ref, jax.nn.sigmoid(z[:, C_V:2 * C_V]))


def _hgrn_sample_proj(x, nm, w_in, clb, per_step):
    n = x.shape[0]
    assert C_F == C_V and C_IN == 4 * C_F
    grouped = lambda width: jax.ShapeDtypeStruct((n // per_step, per_step, width), F32)
    return pl.pallas_call(
        _hgrn_sample_proj_kernel,
        grid=(2,),
        in_specs=[
            _full((n, D_MODEL)),
            _full((1, D_MODEL)),
            pl.BlockSpec((D_MODEL, C_IN // 2), lambda t: (0, t)),
            _full((DEPTH, C_F)),
        ],
        out_specs=[_full((n // per_step, per_step, C_F))] * 2 + [_full((n // per_step, per_step, C_V))] * 2
        + [pl.BlockSpec((D_MODEL, C_IN // 2), lambda t: (0, t))],
        out_shape=[grouped(C_F)] * 2 + [grouped(C_V)] * 2 + [jax.ShapeDtypeStruct(w_in.shape, BF16)],
        scratch_shapes=[pltpu.VMEM((n, D_MODEL), BF16)],
        compiler_params=pltpu.CompilerParams(
            dimension_semantics=("arbitrary",), vmem_limit_bytes=VMEM_LIMIT),
        name="hgrn_sample_proj",
    )(x, nm, w_in, clb)


def kernel(x_prompt, x_sample, cache_k, cache_v, state_hgrn, norm_mix, norm_ffn, w_in_ab, w_out_ab,
           q_norm, k_norm, attn_sink, rel_bias, gmlp_ln_g, gmlp_ln_b, gmlp_w_s, gmlp_b_s,
           w_in_c, c_lower_bounds, c_out_norm, w_out_c, w_gate, w_up, w_down):
    assert norm_mix.shape[0] == DEPTH == 2 and w_in_ab.shape[0] == 1 and w_in_c.shape[0] == 1
    nb, seq, _ = x_prompt.shape
    ns = x_sample.shape[0]
    assert x_sample.shape[1] == 1 and cache_k.shape[2] == WINDOW

    row = lambda v: v.reshape(1, -1)
    bf = lambda w: w.astype(BF16)
    w_out_ab0, w_out_c0 = bf(w_out_ab[0]), bf(w_out_c[0])
    nm, nf = norm_mix, norm_ffn
    qn, kn = row(q_norm[0]), row(k_norm[0])
    lng, lnb = row(gmlp_ln_g[0]), row(gmlp_ln_b[0])
    sink = attn_sink[0]

    sb, tabp = _bias_table(rel_bias)

    xs = x_sample.reshape(ns, D_MODEL)
    qx, knew_s, vnew_s, bm_s, gv_s, w_in_ab0 = _ab_sample_proj(
        xs, row(nm[0]), w_in_ab[0], qn, kn, lng, lnb, gmlp_w_s[0, :, 0, 0], gmlp_b_s[0, :, 0])
    to_t = lambda c: c.transpose(0, 2, 3, 1).reshape(ns, A_KV, WINDOW)
    xp, knew_p, vnew_p, gv_p, nk_s, nv_s, om = _ab_prompt(
        x_prompt, row(nm[0]), w_in_ab0, jnp.tile(qn, (1, A_HEADS)), jnp.tile(kn, (1, A_KV_HEADS)),
        sink, tabp, lng, lnb, gmlp_w_s[0], jnp.repeat(gmlp_b_s[0].T, B_GROUP_DIM, axis=1), w_out_ab0,
        to_t(cache_k[0]), to_t(cache_v[0]), qx, knew_s, vnew_s, sb)
    a_s = om.reshape(ns, A_Q)
    xp, xs = _ffn(xp.reshape(nb * seq, D_MODEL), xs, jnp.concatenate([a_s, bm_s], axis=-1), w_out_ab0,
                  row(nf[0]), w_gate, w_up, w_down, 0)

    q_s, f_s, i_s, sg_s, w_in_c0 = _hgrn_sample_proj(xs, row(nm[1]), w_in_c[0], c_lower_bounds,
                                                     ns // (nb * (seq // TQH)))
    xp, st_p, st_s, o_s = _hgrn_prompt(xp.reshape(nb, seq, D_MODEL), row(nm[1]), w_in_c0, c_lower_bounds,
                                       row(c_out_norm[0]), w_out_c0, state_hgrn[0], f_s, q_s, i_s, sg_s)
    xp, xs = _ffn(xp.reshape(nb * seq, D_MODEL), xs, o_s, w_out_c0, row(nf[1]), w_gate, w_up, w_down, 1)

    kv5 = lambda a: a.reshape(1, a.shape[0], WINDOW, A_KV_HEADS, A_HEAD_DIM)
    from_t = lambda a: a.reshape(ns, A_KV_HEADS, A_HEAD_DIM, WINDOW).transpose(0, 3, 1, 2)[None]
    return (xp.reshape(nb, seq, D_MODEL), xs.reshape(ns, 1, D_MODEL),
            kv5(knew_p), kv5(vnew_p), from_t(nk_s), from_t(nv_s),
            gv_p[None], gv_s.reshape(1, ns, 1, B_WIDTH),
            st_p[None], st_s[None])
```

```python
import functools
import math

import jax
import jax.numpy as jnp
import numpy as np
from jax import lax
from jax.experimental import pallas as pl
from jax.experimental.pallas import tpu as pltpu

F32 = jnp.float32
BF16 = jnp.bfloat16

D_MODEL = 1024
DEPTH = 2
A_HEADS = 8
A_KV_HEADS = 2
A_GROUP = A_HEADS // A_KV_HEADS
A_HEAD_DIM = 64
WINDOW = 128
ATTN_SCALE = A_HEAD_DIM ** -0.5
NUM_BUCKETS = 32
MAX_DISTANCE = 128
A_Q = A_HEADS * A_HEAD_DIM
A_KV = A_KV_HEADS * A_HEAD_DIM
B_GROUPS = 8
B_GROUP_DIM = 64
B_WIDTH = B_GROUPS * B_GROUP_DIM
B_CHUNK = 128
AB_IN = A_Q + 2 * A_KV + 2 * B_WIDTH
AB_MIX = A_Q + B_WIDTH
C_HEADS = 8
C_KEY_DIM = 128
C_VAL_DIM = 128
C_F = C_HEADS * C_KEY_DIM
C_V = C_HEADS * C_VAL_DIM
C_IN = 2 * C_F + 2 * C_V
D_FF = 2816
EPS = 1e-6
LOG2E = math.log2(math.e)

NEG = -1e30

VMEM_LIMIT = 56 * 1024 * 1024
VREG_ROWS = 8

CHUNK = 128
TQ = 512
TQH = 512
TM = 512
HEAD_SKEW = 2
SAFE_LOG2_RANGE = 64.0
PREFIX_GROUP = 4
OUT_GROUP = 2

_NT = (((1,), (1,)), ((), ()))
_TN = (((0,), (0,)), ((), ()))


def _rms(x, g):
    return x * lax.rsqrt(jnp.mean(x * x, axis=-1, keepdims=True) + EPS) * g


def _gelu(x):
    return 0.5 * x * (1.0 + lax.erf(x * math.sqrt(0.5)))


def _layernorm(x, g, b):
    xc = x - jnp.mean(x, axis=-1, keepdims=True)
    return xc * lax.rsqrt(jnp.mean(xc * xc, axis=-1, keepdims=True) + EPS) * g + b


def _dot(a, b):
    return jnp.dot(a, b, preferred_element_type=F32)


def _full(shape):
    n = len(shape)
    return pl.BlockSpec(shape, lambda *_: (0,) * n)


def _resident(shape):
    n = len(shape)
    return pl.BlockSpec(shape, lambda *_: (0,) * n, pipeline_mode=pl.Buffered(1))


_SMEM = pl.BlockSpec(memory_space=pltpu.SMEM)


def _bias_table_kernel(rel_ref, sb_ref, tabp_ref):
    qi = lax.broadcasted_iota(jnp.int32, (WINDOW, 2 * WINDOW), 0)
    kj = lax.broadcasted_iota(jnp.int32, (WINDOW, 2 * WINDOW), 1)
    dist = qi + WINDOW - kj
    ok = (dist >= 0) & (dist < WINDOW)
    max_exact = NUM_BUCKETS // 2
    d = jnp.maximum(dist, 0)
    dl = jnp.maximum(d, 1).astype(F32)
    v = (jnp.log(dl / max_exact) / math.log(MAX_DISTANCE / max_exact) * (NUM_BUCKETS - max_exact))
    far = d >= max_exact
    hits = []
    for b in range(NUM_BUCKETS):
        if b < max_exact:
            hits.append(d == b)
        elif b < NUM_BUCKETS - 1:
            hits.append(far & (v >= b - max_exact) & (v < b - max_exact + 1))
        else:
            hits.append(far & (v >= b - max_exact))
    for h in range(A_HEADS):
        acc = jnp.zeros((WINDOW, 2 * WINDOW), F32)
        for b in range(NUM_BUCKETS):
            acc = jnp.where(hits[b], rel_ref[b, h], acc)
        t = jnp.where(ok, acc, NEG)
        last = pltpu.roll(t[WINDOW - 1:WINDOW, :], WINDOW + 1, 1)
        sb_ref[h:h + 1, :] = jnp.where(kj[0:1, :] <= WINDOW, last, 0.0)
        cols = slice((h % 2) * 2 * WINDOW, (h % 2 + 1) * 2 * WINDOW)
        tabp_ref[0, h // 2, :, cols] = t * LOG2E
        tabp_ref[1, h // 2, :, cols] = jnp.where(kj < WINDOW, NEG, t * LOG2E)


def _bias_table(rel_bias):
    return pl.pallas_call(
        _bias_table_kernel,
        out_shape=[
            jax.ShapeDtypeStruct((A_HEADS, 2 * WINDOW), F32),
            jax.ShapeDtypeStruct((2, A_HEADS // 2, WINDOW, 4 * WINDOW), F32),
        ],
        in_specs=[_SMEM],
        name="bias_table",
    )(rel_bias)


PAIR = 2 * A_HEAD_DIM
N_PAIRS = A_HEADS // 2


def _sample_cache_attention(ck_ref, cv_ref, qx_ref, kn_ref, vn_ref, sb_ref, sink_ref,
                            nk_ref, nv_ref, om_ref):
    wb = ck_ref.shape[2]
    first_group = lax.broadcasted_iota(jnp.int32, (1, A_HEADS, A_HEAD_DIM), 1) < A_GROUP
    newest = lax.broadcasted_iota(jnp.int32, (1, 1, wb), 2) == wb - 1
    sink = sink_ref[...][None]
    kc, vc = ck_ref[...], cv_ref[...]
    kn, vn = kn_ref[...], vn_ref[...]
    kn_cols, vn_cols = kn_ref[:, 0, :].T, vn_ref[:, 0, :].T
    for i in range(kc.shape[0]):
        nk_ref[i] = jnp.where(newest[0], kn_cols[:, i:i + 1], pltpu.roll(kc[i], wb - 1, 1))
        nv_ref[i] = jnp.where(newest[0], vn_cols[:, i:i + 1], pltpu.roll(vc[i], wb - 1, 1))
    q = qx_ref[...]
    s = jnp.einsum('bhd,bdk->bhk', q.astype(BF16), kc.astype(BF16), preferred_element_type=F32)
    s = s + sb_ref[:, 0:wb][None]
    sn = jnp.sum(q * kn, axis=-1, keepdims=True) + sb_ref[:, wb:wb + 1][None]
    m = jnp.maximum(jnp.maximum(jnp.max(s, axis=-1, keepdims=True), sn), sink)
    e = jnp.exp(s - m)
    en = jnp.exp(sn - m)
    r = 1.0 / (jnp.sum(e, axis=-1, keepdims=True) + en + jnp.exp(sink - m))
    o = jnp.einsum('bhk,bdk->bhd', (e * r).astype(BF16), vc.astype(BF16),
                   preferred_element_type=F32) + (en * r) * vn
    om_ref[...] = jnp.where(first_group, o[:, :, 0:A_HEAD_DIM], o[:, :, A_HEAD_DIM:A_KV])


def _ab_prompt_kernel(sink_ref, x_ref, nm_ref, win_ref, qg_ref, kg_ref, tabp_ref, lng_ref, lnb_ref,
                      ws_ref, bsp_ref, wout_ref,
                      ck_ref, cv_ref, qx_ref, kn_ref, vn_ref, sb_ref, sinkc_ref,
                      y_ref, knew_ref, vnew_ref, gv_ref, nk_ref, nv_ref, om_ref,
                      z_ref, mix_ref, q_ref, k_ref, kr_ref, v_ref, vr_ref, wpair_ref, kl_ref, vl_ref, gl_ref):
    j = pl.program_id(1)
    last_j = pl.num_programs(1) - 1
    n_chunks = TQ // CHUNK

    @pl.when(j == 0)
    def _():
        for ref in (k_ref, kr_ref, v_ref, vr_ref):
            ref[0:CHUNK, :] = jnp.zeros((CHUNK, A_KV), BF16)
        row = lax.broadcasted_iota(jnp.int32, (B_CHUNK, B_CHUNK), 0)
        col = lax.broadcasted_iota(jnp.int32, (B_CHUNK, B_CHUNK), 1)
        for g in range(B_GROUPS):
            wpair_ref[g // 2, :, (g % 2) * B_CHUNK:(g % 2 + 1) * B_CHUNK] = jnp.where(
                row >= col, ws_ref[g], 0.0).astype(BF16)

    h = _rms(x_ref[0], nm_ref[...]).astype(BF16)
    z_ref[...] = _dot(h, win_ref[...])

    _sample_cache_attention(ck_ref, cv_ref, qx_ref, kn_ref, vn_ref, sb_ref, sinkc_ref,
                            nk_ref, nv_ref, om_ref)

    lo_half = lax.broadcasted_iota(jnp.int32, (1, PAIR), 1) < A_HEAD_DIM

    def mean_sq_halves(x):
        x2 = x * x
        lo_sum = jnp.sum(jnp.where(lo_half, x2, 0.0), axis=-1, keepdims=True)
        hi_sum = jnp.sum(jnp.where(lo_half, 0.0, x2), axis=-1, keepdims=True)
        return jnp.where(lo_half, lo_sum, hi_sum) * (1.0 / A_HEAD_DIM)

    def block_diag(top, bot):
        zero = jnp.zeros_like(top)
        return jnp.concatenate([jnp.where(lo_half, top, zero), jnp.where(lo_half, zero, bot)], axis=0)

    kraw = z_ref[:, A_Q:A_Q + A_KV]
    v_all = z_ref[:, A_Q + A_KV:A_Q + 2 * A_KV]
    kn_all = kraw * lax.rsqrt(mean_sq_halves(kraw) + EPS) * kg_ref[...]
    k_ref[CHUNK:, :] = kn_all.astype(BF16)
    kr_ref[CHUNK:, :] = pltpu.roll(kn_all, A_HEAD_DIM, 1).astype(BF16)
    v_ref[CHUNK:, :] = v_all.astype(BF16)
    vr_ref[CHUNK:, :] = pltpu.roll(v_all, A_HEAD_DIM, 1).astype(BF16)
    kl_ref[...] = kn_all[TQ - CHUNK:, :]
    vl_ref[...] = v_all[TQ - CHUNK:, :]
    for i in range(N_PAIRS):
        ps = slice(i * PAIR, (i + 1) * PAIR)
        qraw = z_ref[:, ps]
        qn = qraw * lax.rsqrt(mean_sq_halves(qraw) + EPS) * (qg_ref[:, ps] * (ATTN_SCALE * LOG2E))
        q_ref[:, ps] = qn.astype(BF16)

    def chunk(c, carry):
        r0 = c * CHUNK
        rows = pl.ds(r0, CHUNK)
        first = jnp.where(jnp.logical_and(j == 0, c == 0), 1, 0)

        both = pl.ds(r0, 2 * CHUNK)
        k2, k2r, v2, v2r = k_ref[both, :], kr_ref[both, :], v_ref[both, :], vr_ref[both, :]
        kbd = [block_diag(k2, k2r), block_diag(k2r, k2)]
        vbd = [block_diag(v2, v2r), block_diag(v2r, v2)]

        scores = []
        for i in range(N_PAIRS):
            s = lax.dot_general(q_ref[rows, i * PAIR:(i + 1) * PAIR], kbd[i // (A_GROUP // 2)], _NT,
                                preferred_element_type=F32)
            scores.append(s + tabp_ref[first, i])
        outs = []
        for i in range(N_PAIRS):
            es, rs = [], []
            for hh in range(2):
                sh = scores[i][:, hh * 2 * WINDOW:(hh + 1) * 2 * WINDOW]
                sk = sink_ref[2 * i + hh] * LOG2E
                m = jnp.maximum(jnp.max(sh, axis=-1, keepdims=True), sk)
                e = jnp.exp2(sh - m)
                rs.append(1.0 / (jnp.sum(e, axis=-1, keepdims=True) + jnp.exp2(sk - m)))
                es.append(e.astype(BF16))
            o = _dot(jnp.concatenate(es, axis=-1), vbd[i // (A_GROUP // 2)])
            outs.append(o * jnp.where(lo_half, rs[0], rs[1]))
        mix_ref[rows, 0:A_Q] = jnp.concatenate(outs, axis=-1).astype(BF16)

        zu = z_ref[rows, A_Q + 2 * A_KV:A_Q + 2 * A_KV + B_WIDTH]
        zv = z_ref[rows, A_Q + 2 * A_KV + B_WIDTH:AB_IN]
        u = _gelu(zu)
        vln = _layernorm(_gelu(zv), lng_ref[...], lnb_ref[...])
        vlb = vln.astype(BF16)
        sparts = []
        for i in range(B_GROUPS // 2):
            vpair = vlb[:, i * PAIR:(i + 1) * PAIR]
            sparts.append(_dot(wpair_ref[i], block_diag(vpair, vpair)))
        bm = u * (jnp.concatenate(sparts, axis=-1) + bsp_ref[...])
        mix_ref[rows, A_Q:AB_MIX] = bm.astype(BF16)

        gl_ref[...] = vln
        return carry

    for c in range(n_chunks):
        chunk(c, 0)
        done = slice(c * CHUNK, (c + 1) * CHUNK)
        y_ref[0, done, :] = x_ref[0, done, :] + _dot(mix_ref[done, :], wout_ref[...])
    for ref in (k_ref, kr_ref, v_ref, vr_ref):
        ref[0:CHUNK, :] = ref[TQ:TQ + CHUNK, :]

    @pl.when(j == last_j)
    def _():
        knew_ref[0] = kl_ref[...]
        vnew_ref[0] = vl_ref[...]
        gv_ref[0] = gl_ref[...]


def _ab_prompt(x, nm, w_in, qg, kg, sink, tabp, lng, lnb, w_s, bsp, w_out, ck, cv, qx, kn_s, vn_s, sb):
    nb, seq, _ = x.shape
    nj = seq // TQ
    grid = (nb, nj)
    ns, _, wb = ck.shape
    sba = ns // (nb * nj)
    assert sba * nb * nj == ns
    blk = lambda b, j: (b, j, 0)
    per_b = lambda b, j: (b, 0, 0)
    step = lambda b, j: (b * nj + j, 0, 0)
    return pl.pallas_call(
        _ab_prompt_kernel,
        grid=grid,
        in_specs=[
            _SMEM,
            pl.BlockSpec((1, TQ, D_MODEL), blk),
            _full((1, D_MODEL)),
            _resident((D_MODEL, AB_IN)),
            _full((1, A_Q)),
            _full((1, A_KV)),
            _resident((2, N_PAIRS, WINDOW, 4 * WINDOW)),
            _full((1, B_WIDTH)),
            _full((1, B_WIDTH)),
            _resident((B_GROUPS, B_CHUNK, B_CHUNK)),
            _resident((B_CHUNK, B_WIDTH)),
            _resident((AB_MIX, D_MODEL)),
            pl.BlockSpec((sba, A_KV, wb), step),
            pl.BlockSpec((sba, A_KV, wb), step),
            pl.BlockSpec((sba, A_HEADS, A_KV), step),
            pl.BlockSpec((sba, 1, A_KV), step),
            pl.BlockSpec((sba, 1, A_KV), step),
            _full((A_HEADS, 2 * WINDOW)),
            _full((A_HEADS, 1)),
        ],
        out_specs=[
            pl.BlockSpec((1, TQ, D_MODEL), blk),
            pl.BlockSpec((1, WINDOW, A_KV), per_b),
            pl.BlockSpec((1, WINDOW, A_KV), per_b),
            pl.BlockSpec((1, B_CHUNK, B_WIDTH), per_b),
            pl.BlockSpec((sba, A_KV, wb), step),
            pl.BlockSpec((sba, A_KV, wb), step),
            pl.BlockSpec((sba, A_HEADS, A_HEAD_DIM), step),
        ],
        out_shape=[
            jax.ShapeDtypeStruct((nb, seq, D_MODEL), F32),
            jax.ShapeDtypeStruct((nb, WINDOW, A_KV), F32),
            jax.ShapeDtypeStruct((nb, WINDOW, A_KV), F32),
            jax.ShapeDtypeStruct((nb, B_CHUNK, B_WIDTH), F32),
            jax.ShapeDtypeStruct((ns, A_KV, wb), F32),
            jax.ShapeDtypeStruct((ns, A_KV, wb), F32),
            jax.ShapeDtypeStruct((ns, A_HEADS, A_HEAD_DIM), F32),
        ],
        scratch_shapes=[
            pltpu.VMEM((TQ, AB_IN), F32),
            pltpu.VMEM((TQ, AB_MIX), BF16),
            pltpu.VMEM((TQ, A_Q), BF16),
            pltpu.VMEM((CHUNK + TQ, A_KV), BF16),
            pltpu.VMEM((CHUNK + TQ, A_KV), BF16),
            pltpu.VMEM((CHUNK + TQ, A_KV), BF16),
            pltpu.VMEM((CHUNK + TQ, A_KV), BF16),
            pltpu.VMEM((B_GROUPS // 2, B_CHUNK, 2 * B_CHUNK), BF16),
            pltpu.VMEM((WINDOW, A_KV), F32),
            pltpu.VMEM((WINDOW, A_KV), F32),
            pltpu.VMEM((B_CHUNK, B_WIDTH), F32),
        ],
        compiler_params=pltpu.CompilerParams(
            dimension_semantics=("arbitrary", "arbitrary"), vmem_limit_bytes=VMEM_LIMIT),
        name="ab_prompt",
    )(sink, x, nm, w_in, qg, kg, tabp, lng, lnb, w_s, bsp, w_out,
      ck, cv, qx, kn_s[:, None, :], vn_s[:, None, :], sb, sink.reshape(A_HEADS, 1))


FF_TILE = 256


def _ffn_kernel(xp_ref, xs_ref, ms_ref, wo_ref, g_ref, wg_ref, wu_ref, wd_ref, yp_ref, ys_ref,
                wg_s, wu_s, wd_s, h0_s, acc_s, *, n_cast, n_prompt):
    s = pl.program_id(0)

    def gated(h, wg, wu):
        gate = _dot(h, wg)
        return (gate * jax.nn.sigmoid(gate) * _dot(h, wu)).astype(BF16)

    @pl.when(s == 0)
    def _():
        x = xp_ref[...]
        h0_s[...] = _rms(x, g_ref[...]).astype(BF16)
        acc_s[...] = x

    for c in range(n_cast):
        @pl.when(s == c)
        def _(c=c):
            tile = slice(c * FF_TILE, (c + 1) * FF_TILE)
            wg_t, wu_t, wd_t = (r[...].astype(BF16) for r in (wg_ref, wu_ref, wd_ref))
            wg_s[:, tile] = wg_t
            wu_s[:, tile] = wu_t
            wd_s[tile, :] = wd_t
            acc_s[...] += _dot(gated(h0_s[...], wg_t, wu_t), wd_t)

    @pl.when(s == n_cast - 1)
    def _():
        yp_ref[...] = acc_s[...]

    def swiglu(x):
        h = _rms(x, g_ref[...]).astype(BF16)
        return x + _dot(gated(h, wg_s[...], wu_s[...]), wd_s[...])

    @pl.when(jnp.logical_and(s >= n_cast, s < n_cast + n_prompt - 1))
    def _():
        yp_ref[...] = swiglu(xp_ref[...])

    @pl.when(s == n_cast + n_prompt - 1)
    def _():
        ys_ref[...] = swiglu(xs_ref[...] + _dot(ms_ref[...].astype(BF16), wo_ref[...]))


def _ffn(xp, xs, mix_s, w_o, g, w_gate, w_up, w_down, layer):
    rows, ns = xp.shape[0], xs.shape[0]
    n_cast, n_prompt = D_FF // FF_TILE, rows // TM
    w_tile = lambda s: jnp.minimum(s, n_cast - 1)
    row_blk = lambda s: (jnp.clip(s - (n_cast - 1), 0, n_prompt - 1), 0)
    return pl.pallas_call(
        functools.partial(_ffn_kernel, n_cast=n_cast, n_prompt=n_prompt),
        grid=(n_cast + n_prompt,),
        in_specs=[
            pl.BlockSpec((TM, D_MODEL), row_blk),
            _full((ns, D_MODEL)),
            _full((ns, D_MODEL)),
            _resident((D_MODEL, D_MODEL)),
            _full((1, D_MODEL)),
            pl.BlockSpec((None, D_MODEL, FF_TILE), lambda s: (layer, 0, w_tile(s))),
            pl.BlockSpec((None, D_MODEL, FF_TILE), lambda s: (layer, 0, w_tile(s))),
            pl.BlockSpec((None, FF_TILE, D_MODEL), lambda s: (layer, w_tile(s), 0)),
        ],
        out_specs=[pl.BlockSpec((TM, D_MODEL), row_blk), _full((ns, D_MODEL))],
        out_shape=[jax.ShapeDtypeStruct((rows, D_MODEL), F32), jax.ShapeDtypeStruct((ns, D_MODEL), F32)],
        scratch_shapes=[
            pltpu.VMEM((D_MODEL, D_FF), BF16),
            pltpu.VMEM((D_MODEL, D_FF), BF16),
            pltpu.VMEM((D_FF, D_MODEL), BF16),
            pltpu.VMEM((TM, D_MODEL), BF16),
            pltpu.VMEM((TM, D_MODEL), F32),
        ],
        compiler_params=pltpu.CompilerParams(
            dimension_semantics=("arbitrary",), vmem_limit_bytes=VMEM_LIMIT),
        name="ffn",
    )(xp, xs, mix_s, w_o, g, w_gate, w_up, w_down)


def _lower_bound(clb):
    m = jnp.max(clb, axis=0, keepdims=True)
    e = jnp.exp(clb - m)
    sm = e / jnp.sum(e, axis=0, keepdims=True)
    return (sm[0:1] + sm[1:2]) - sm[0:1]


def _split3(x):
    hi = x.astype(BF16)
    r = x - hi.astype(F32)
    mid = r.astype(BF16)
    lo = (r - mid.astype(F32)).astype(BF16)
    return hi, mid, lo


def _neg_abs(x):
    return lax.bitcast_convert_type(
        lax.bitcast_convert_type(x, jnp.uint32) | jnp.uint32(0x80000000), F32)


def _pair_level_table():
    t = np.arange(CHUNK)[:, None]
    s = np.arange(CHUNK)[None, :]
    lev = np.floor(np.log2(np.maximum(t ^ s, 1))).astype(np.int32)
    lev = np.where(t == s, -1, lev)
    return np.where(s > t, -2, lev).astype(np.int32)


def _level_operand(p, q, kk, f, b2):
    m = 2 ** p
    if m < VREG_ROWS:
        shape3 = (CHUNK // VREG_ROWS, VREG_ROWS, q.shape[1])
        sub = lax.broadcasted_iota(jnp.int32, (1, VREG_ROWS, q.shape[1]), 1)
        upper = ((sub >> p) & 1) == 1
        q3, k3 = q.reshape(shape3), kk.reshape(shape3)
        if p == 0:
            y = jnp.where(upper, q3 * f.reshape(shape3), k3)
        else:
            b3 = b2.reshape(shape3)
            be = b3[:, m - 1:m, :]
            for k in range(1, VREG_ROWS // (2 * m)):
                be = jnp.where(sub >= 2 * m * k, b3[:, 2 * m * k + m - 1:2 * m * k + m, :], be)
            y = jnp.where(upper, q3, k3) * jnp.exp2(_neg_abs(b3 - be))
        return y.reshape(q.shape).astype(BF16)
    parts = []
    for k in range(CHUNK // (2 * m)):
        lo = slice(2 * m * k, 2 * m * k + m)
        up = slice(2 * m * k + m, 2 * m * (k + 1))
        be = b2[2 * m * k + m - 1:2 * m * k + m, :]
        parts.append(kk[lo] * jnp.exp2(be - b2[lo]))
        parts.append(q[up] * jnp.exp2(b2[up] - be))
    return jnp.concatenate(parts, axis=0).astype(BF16)


def _merge_level(p, att, pm, lev):
    m = 2 ** p
    if m < VREG_ROWS:
        return jnp.where(lev == p, pm, att)
    col = lax.broadcasted_iota(jnp.int32, (1, CHUNK), 1)
    parts = []
    for k in range(CHUNK // (2 * m)):
        lo = slice(2 * m * k, 2 * m * k + m)
        up = slice(2 * m * k + m, 2 * m * (k + 1))
        parts.append(att[lo])
        parts.append(jnp.where((col >= 2 * m * k) & (col < 2 * m * k + m), pm[up], att[up]))
    return jnp.concatenate(parts, axis=0)


def _hgrn_prompt_kernel(x_ref, nm_ref, win_ref, clb_ref, on_ref, wout_ref, lev_ref,
                        ss_ref, fs_ref, qs_ref, is_ref, gs_ref,
                        y_ref, st_ref, sso_ref, os_ref,
                        z_ref, o_ref, stt_ref, k_ref):
    j = pl.program_id(1)
    last_j = pl.num_programs(1) - 1
    n_chunks = TQH // CHUNK
    n_levels = int(math.log2(CHUNK))

    @pl.when(j == 0)
    def _():
        stt_ref[...] = jnp.zeros_like(stt_ref)

    h = _rms(x_ref[0], nm_ref[...]).astype(BF16)
    z_ref[...] = _dot(h, win_ref[...])

    head_cols = lambda ref: [ref[:, hd * C_KEY_DIM:(hd + 1) * C_KEY_DIM].T for hd in range(C_HEADS)]
    f_cols, q_cols = head_cols(fs_ref), head_cols(qs_ref)
    out_rows = []
    for smp in range(ss_ref.shape[0]):
        parts = []
        for hd in range(C_HEADS):
            hs = slice(hd * C_VAL_DIM, (hd + 1) * C_VAL_DIM)
            fb = jnp.broadcast_to(f_cols[hd][:, smp:smp + 1], (C_KEY_DIM, C_VAL_DIM))
            sn = fb * ss_ref[smp, hd] + (1.0 - fb) * is_ref[smp:smp + 1, hs]
            sso_ref[smp, hd] = sn
            o = jnp.sum(q_cols[hd][:, smp:smp + 1] * sn, axis=0, keepdims=True)
            parts.append(_rms(o, on_ref[...]))
        out_rows.append(jnp.concatenate(parts, axis=-1))
    os_ref[...] = jnp.concatenate(out_rows, axis=0) * gs_ref[...]

    lb = _lower_bound(clb_ref[...])

    row = lax.broadcasted_iota(jnp.int32, (CHUNK, CHUNK), 0)
    col = lax.broadcasted_iota(jnp.int32, (CHUNK, CHUNK), 1)
    ltri = (row >= col).astype(BF16)

    def chunk_rows(c):
        return pl.ds(pl.multiple_of(c * CHUNK, CHUNK), CHUNK)

    def prefix(g, worst):
        rows = [chunk_rows(g * PREFIX_GROUP + i) for i in range(PREFIX_GROUP)]
        gates = [z_ref[r, C_F:2 * C_F] for r in rows]
        for r, gate in zip(rows, gates):
            f_all = lb + (1.0 - lb) * jax.nn.sigmoid(gate)
            k_ref[r, :] = 1.0 - f_all
            hi, mid, lo = _split3(jnp.log2(f_all))
            b2 = (_dot(ltri, hi) + _dot(ltri, mid)) + _dot(ltri, lo)
            z_ref[r, C_F:2 * C_F] = b2
            b_mid = b2[CHUNK // 2 - 1:CHUNK // 2, :]
            b_last = b2[CHUNK - 1:CHUNK, :]
            worst = jnp.maximum(worst, jnp.maximum(-b_mid, b_mid - b_last))
        return worst

    worst = lax.fori_loop(0, n_chunks // PREFIX_GROUP, prefix, jnp.zeros((1, C_F), F32))
    bounded = jnp.max(worst) <= SAFE_LOG2_RANGE

    def finish_head(rows, hd, o):
        gt = z_ref[rows, 2 * C_F + C_V + hd * C_VAL_DIM:2 * C_F + C_V + (hd + 1) * C_VAL_DIM]
        o = _rms(o, on_ref[...]) * jax.nn.sigmoid(gt)
        o_ref[rows, hd * C_VAL_DIM:(hd + 1) * C_VAL_DIM] = o.astype(BF16)

    def head_inputs(rows, hd):
        q = z_ref[rows, hd * C_KEY_DIM:(hd + 1) * C_KEY_DIM]
        kk = k_ref[rows, hd * C_KEY_DIM:(hd + 1) * C_KEY_DIM]
        b2 = z_ref[rows, C_F + hd * C_KEY_DIM:C_F + (hd + 1) * C_KEY_DIM]
        ivb = z_ref[rows, 2 * C_F + hd * C_VAL_DIM:2 * C_F + (hd + 1) * C_VAL_DIM].astype(BF16)
        return q, kk, b2, ivb

    def factored_chunk(c, carry):
        rows = chunk_rows(c)
        for hd in range(C_HEADS):
            q, kk, b2, ivb = head_inputs(rows, hd)
            b_mid = b2[CHUNK // 2 - 1:CHUNK // 2, :]
            b_last = b2[CHUNK - 1:CHUNK, :]
            qs = (q * jnp.exp2(b2 - b_mid)).astype(BF16)
            kd = (kk * jnp.exp2(b_mid - b2)).astype(BF16)
            att = jnp.where(row >= col, lax.dot_general(qs, kd, _NT, preferred_element_type=F32), 0.0)
            stt = stt_ref[hd]
            o = lax.dot_general(qs, (stt * jnp.exp2(b_mid)).astype(BF16), _NT,
                                preferred_element_type=F32) + _dot(att.astype(BF16), ivb)
            stt_ref[hd] = stt * jnp.exp2(b_last) + jnp.exp2(b_last - b_mid) * lax.dot_general(
                ivb, kd, _TN, preferred_element_type=F32)
            finish_head(rows, hd, o)
        return carry

    def tree_chunk(c, carry):
        rows = chunk_rows(c)
        lev = lev_ref[...]

        def products(hd):
            q, kk, b2, ivb = head_inputs(rows, hd)
            diag = jnp.sum(q * kk, axis=-1, keepdims=True)
            pms = []
            for p in range(n_levels):
                y = _level_operand(p, q, kk, 1.0 - kk, b2)
                pms.append(lax.dot_general(y, y, _NT, preferred_element_type=F32))
            stt = stt_ref[hd]
            o_prev = lax.dot_general((q * jnp.exp2(b2)).astype(BF16), stt.astype(BF16), _NT,
                                     preferred_element_type=F32)
            b_last = b2[CHUNK - 1:CHUNK, :]
            kd = (kk * jnp.exp2(b_last - b2)).astype(BF16)
            stt_ref[hd] = stt * jnp.exp2(b_last) + lax.dot_general(
                ivb, kd, _TN, preferred_element_type=F32)
            return diag, pms, o_prev, ivb

        def finish(hd, diag, pms, o_prev, ivb):
            att = jnp.where(lev == -1, diag, 0.0)
            for p in range(n_levels):
                att = _merge_level(p, att, pms[p], lev)
            finish_head(rows, hd, o_prev + _dot(att.astype(BF16), ivb))

        pending = [products(hd) for hd in range(HEAD_SKEW)]
        for hd in range(C_HEADS):
            if hd + HEAD_SKEW < C_HEADS:
                pending.append(products(hd + HEAD_SKEW))
            finish(hd, *pending.pop(0))
        return carry

    @pl.when(bounded)
    def _():
        for c in range(n_chunks):
            factored_chunk(c, 0)
            if c % OUT_GROUP == OUT_GROUP - 1:
                done = slice((c + 1 - OUT_GROUP) * CHUNK, (c + 1) * CHUNK)
                y_ref[0, done, :] = x_ref[0, done, :] + _dot(o_ref[done, :], wout_ref[...])

    @pl.when(jnp.logical_not(bounded))
    def _():
        lax.fori_loop(0, n_chunks, tree_chunk, 0)
        y_ref[0] = x_ref[0] + _dot(o_ref[...], wout_ref[...])

    @pl.when(j == last_j)
    def _():
        for hd in range(C_HEADS):
            st_ref[0, hd] = stt_ref[hd].T


def _hgrn_prompt(x, nm, w_in, clb, on, w_out, state_s, f_s, q_s, i_s, g_s):
    nb, seq, _ = x.shape
    nj = seq // TQH
    ns = state_s.shape[0]
    sbh = f_s.shape[1]
    assert f_s.shape[0] == nb * nj and sbh * nb * nj == ns
    blk = lambda b, j: (b, j, 0)
    step = lambda b, j: (b * nj + j, 0, 0)
    step4 = lambda b, j: (b * nj + j, 0, 0, 0)
    y, st, st_s, o_s = pl.pallas_call(
        _hgrn_prompt_kernel,
        grid=(nb, nj),
        in_specs=[
            pl.BlockSpec((1, TQH, D_MODEL), blk),
            _full((1, D_MODEL)),
            _resident((D_MODEL, C_IN)),
            _full((DEPTH, C_F)),
            _full((1, C_VAL_DIM)),
            _resident((C_V, D_MODEL)),
            _full((CHUNK, CHUNK)),
            pl.BlockSpec((sbh, C_HEADS, C_KEY_DIM, C_VAL_DIM), step4),
            pl.BlockSpec((None, sbh, C_F), step),
            pl.BlockSpec((None, sbh, C_F), step),
            pl.BlockSpec((None, sbh, C_V), step),
            pl.BlockSpec((None, sbh, C_V), step),
        ],
        out_specs=[
            pl.BlockSpec((1, TQH, D_MODEL), blk),
            pl.BlockSpec((1, C_HEADS, C_KEY_DIM, C_VAL_DIM), lambda b, j: (b, 0, 0, 0)),
            pl.BlockSpec((sbh, C_HEADS, C_KEY_DIM, C_VAL_DIM), step4),
            pl.BlockSpec((None, sbh, C_V), step),
        ],
        out_shape=[
            jax.ShapeDtypeStruct((nb, seq, D_MODEL), F32),
            jax.ShapeDtypeStruct((nb, C_HEADS, C_KEY_DIM, C_VAL_DIM), F32),
            jax.ShapeDtypeStruct(state_s.shape, F32),
            jax.ShapeDtypeStruct((ns // sbh, sbh, C_V), F32),
        ],
        scratch_shapes=[
            pltpu.VMEM((TQH, C_IN), F32),
            pltpu.VMEM((TQH, C_V), BF16),
            pltpu.VMEM((C_HEADS, C_VAL_DIM, C_KEY_DIM), F32),
            pltpu.VMEM((TQH, C_F), F32),
        ],
        compiler_params=pltpu.CompilerParams(
            dimension_semantics=("arbitrary", "arbitrary"), vmem_limit_bytes=VMEM_LIMIT),
        name="hgrn_prompt",
    )(x, nm, w_in, clb, on, w_out, jnp.asarray(_pair_level_table()),
      state_s, f_s, q_s, i_s, g_s)
    return y, st, st_s, o_s.reshape(ns, C_V)


def _ab_sample_proj_kernel(w00_ref, b0_ref, x_ref, nm_ref, win_ref, qn_ref, kn_ref, lng_ref, lnb_ref,
                           qx_ref, knew_ref, vnew_ref, bm_ref, gv_ref, wbf_ref):
    n = x_ref.shape[0]
    wbf_ref[...] = win_ref[...].astype(BF16)
    h = _rms(x_ref[...], nm_ref[...]).astype(BF16)
    z = _dot(h, wbf_ref[...])
    zeros = jnp.zeros((n, A_HEAD_DIM), F32)
    for hh in range(A_HEADS):
        qh = _rms(z[:, hh * A_HEAD_DIM:(hh + 1) * A_HEAD_DIM], qn_ref[...]) * ATTN_SCALE
        qx_ref[:, hh, :] = jnp.concatenate([qh, zeros] if hh // A_GROUP == 0 else [zeros, qh], axis=-1)
    kparts = []
    for g in range(A_KV_HEADS):
        kparts.append(_rms(z[:, A_Q + g * A_HEAD_DIM:A_Q + (g + 1) * A_HEAD_DIM], kn_ref[...]))
    knew_ref[...] = jnp.concatenate(kparts, axis=-1)
    vnew_ref[...] = z[:, A_Q + A_KV:A_Q + 2 * A_KV]

    u = _gelu(z[:, A_Q + 2 * A_KV:A_Q + 2 * A_KV + B_WIDTH])
    vln = _layernorm(_gelu(z[:, A_Q + 2 * A_KV + B_WIDTH:AB_IN]), lng_ref[...], lnb_ref[...])
    grp = lax.broadcasted_iota(jnp.int32, (1, B_WIDTH), 1) // B_GROUP_DIM
    srow = jnp.zeros((1, B_WIDTH), F32)
    brow = jnp.zeros((1, B_WIDTH), F32)
    for g in range(B_GROUPS):
        srow = jnp.where(grp == g, w00_ref[g], srow)
        brow = jnp.where(grp == g, b0_ref[g], brow)
    bm_ref[...] = u * (vln * srow + brow)
    gv_ref[...] = vln


def _ab_sample_proj(x, nm, w_in, qn, kn, lng, lnb, w00, b0):
    n = x.shape[0]
    return pl.pallas_call(
        _ab_sample_proj_kernel,
        in_specs=[_SMEM, _SMEM] + [pl.BlockSpec(memory_space=pltpu.VMEM)] * 7,
        out_shape=[
            jax.ShapeDtypeStruct((n, A_HEADS, A_KV), F32),
            jax.ShapeDtypeStruct((n, A_KV), F32),
            jax.ShapeDtypeStruct((n, A_KV), F32),
            jax.ShapeDtypeStruct((n, B_WIDTH), F32),
            jax.ShapeDtypeStruct((n, B_WIDTH), F32),
            jax.ShapeDtypeStruct(w_in.shape, BF16),
        ],
        compiler_params=pltpu.CompilerParams(vmem_limit_bytes=VMEM_LIMIT),
        name="ab_sample_proj",
    )(w00, b0, x, nm, w_in, qn, kn, lng, lnb)


def _hgrn_sample_proj_kernel(x_ref, nm_ref, win_ref, clb_ref, q_ref, f_ref, i_ref, sg_ref, wbf_ref, h_s):
    t = pl.program_id(0)

    def put(ref, val):
        per = ref.shape[1]
        for i in range(ref.shape[0]):
            ref[i] = val[i * per:(i + 1) * per, :]

    @pl.when(t == 0)
    def _():
        h_s[...] = _rms(x_ref[...], nm_ref[...]).astype(BF16)

    wbf_ref[...] = win_ref[...].astype(BF16)
    z = _dot(h_s[...], wbf_ref[...])

    @pl.when(t == 0)
    def _():
        lb = _lower_bound(clb_ref[...])
        put(q_ref, z[:, 0:C_F])
        put(f_ref, lb + (1.0 - lb) * jax.nn.sigmoid(z[:, C_F:2 * C_F]))

    @pl.when(t == 1)
    def _():
        put(i_ref, z[:, 0:C_V])
        put(sg_ref, jax.nn.sigmoid(z[:, C_V:2 * C_V]))


def _hgrn_sample_proj(x, nm, w_in, clb, per_step):
    n = x.shape[0]
    assert C_F == C_V and C_IN == 4 * C_F
    grouped = lambda width: jax.ShapeDtypeStruct((n // per_step, per_step, width), F32)
    return pl.pallas_call(
        _hgrn_sample_proj_kernel,
        grid=(2,),
        in_specs=[
            _full((n, D_MODEL)),
            _full((1, D_MODEL)),
            pl.BlockSpec((D_MODEL, C_IN // 2), lambda t: (0, t)),
            _full((DEPTH, C_F)),
        ],
        out_specs=[_full((n // per_step, per_step, C_F))] * 2 + [_full((n // per_step, per_step, C_V))] * 2
        + [pl.BlockSpec((D_MODEL, C_IN // 2), lambda t: (0, t))],
        out_shape=[grouped(C_F)] * 2 + [grouped(C_V)] * 2 + [jax.ShapeDtypeStruct(w_in.shape, BF16)],
        scratch_shapes=[pltpu.VMEM((n, D_MODEL), BF16)],
        compiler_params=pltpu.CompilerParams(
            dimension_semantics=("arbitrary",), vmem_limit_bytes=VMEM_LIMIT),
        name="hgrn_sample_proj",
    )(x, nm, w_in, clb)


def kernel(x_prompt, x_sample, cache_k, cache_v, state_hgrn, norm_mix, norm_ffn, w_in_ab, w_out_ab,
           q_norm, k_norm, attn_sink, rel_bias, gmlp_ln_g, gmlp_ln_b, gmlp_w_s, gmlp_b_s,
           w_in_c, c_lower_bounds, c_out_norm, w_out_c, w_gate, w_up, w_down):
    assert norm_mix.shape[0] == DEPTH == 2 and w_in_ab.shape[0] == 1 and w_in_c.shape[0] == 1
    nb, seq, _ = x_prompt.shape
    ns = x_sample.shape[0]
    assert x_sample.shape[1] == 1 and cache_k.shape[2] == WINDOW

    row = lambda v: v.reshape(1, -1)
    bf = lambda w: w.astype(BF16)
    w_out_ab0, w_out_c0 = bf(w_out_ab[0]), bf(w_out_c[0])
    nm, nf = norm_mix, norm_ffn
    qn, kn = row(q_norm[0]), row(k_norm[0])
    lng, lnb = row(gmlp_ln_g[0]), row(gmlp_ln_b[0])
    sink = attn_sink[0]

    sb, tabp = _bias_table(rel_bias)

    xs = x_sample.reshape(ns, D_MODEL)
    qx, knew_s, vnew_s, bm_s, gv_s, w_in_ab0 = _ab_sample_proj(
        xs, row(nm[0]), w_in_ab[0], qn, kn, lng, lnb, gmlp_w_s[0, :, 0, 0], gmlp_b_s[0, :, 0])
    to_t = lambda c: c.transpose(0, 2, 3, 1).reshape(ns, A_KV, WINDOW)
    xp, knew_p, vnew_p, gv_p, nk_s, nv_s, om = _ab_prompt(
        x_prompt, row(nm[0]), w_in_ab0, jnp.tile(qn, (1, A_HEADS)), jnp.tile(kn, (1, A_KV_HEADS)),
        sink, tabp, lng, lnb, gmlp_w_s[0], jnp.repeat(gmlp_b_s[0].T, B_GROUP_DIM, axis=1), w_out_ab0,
        to_t(cache_k[0]), to_t(cache_v[0]), qx, knew_s, vnew_s, sb)
    a_s = om.reshape(ns, A_Q)
    xp, xs = _ffn(xp.reshape(nb * seq, D_MODEL), xs, jnp.concatenate([a_s, bm_s], axis=-1), w_out_ab0,
                  row(nf[0]), w_gate, w_up, w_down, 0)

    q_s, f_s, i_s, sg_s, w_in_c0 = _hgrn_sample_proj(xs, row(nm[1]), w_in_c[0], c_lower_bounds,
                                                     ns // (nb * (seq // TQH)))
    xp, st_p, st_s, o_s = _hgrn_prompt(xp.reshape(nb, seq, D_MODEL), row(nm[1]), w_in_c0, c_lower_bounds,
                                       row(c_out_norm[0]), w_out_c0, state_hgrn[0], f_s, q_s, i_s, sg_s)
    xp, xs = _ffn(xp.reshape(nb * seq, D_MODEL), xs, o_s, w_out_c0, row(nf[1]), w_gate, w_up, w_down, 1)

    kv5 = lambda a: a.reshape(1, a.shape[0], WINDOW, A_KV_HEADS, A_HEAD_DIM)
    from_t = lambda a: a.reshape(ns, A_KV_HEADS, A_HEAD_DIM, WINDOW).transpose(0, 3, 1, 2)[None]
    return (xp.reshape(nb, seq, D_MODEL), xs.reshape(ns, 1, D_MODEL),
            kv5(knew_p), kv5(vnew_p), from_t(nk_s), from_t(nv_s),
            gv_p[None], gv_s.reshape(1, ns, 1, B_WIDTH),
            st_p[None], st_s[None])
```

```python
import functools
import math

import jax
import jax.numpy as jnp
import numpy as np
from jax import lax
from jax.experimental import pallas as pl
from jax.experimental.pallas import tpu as pltpu

F32 = jnp.float32
BF16 = jnp.bfloat16

D_MODEL = 1024
DEPTH = 2
A_HEADS = 8
A_KV_HEADS = 2
A_GROUP = A_HEADS // A_KV_HEADS
A_HEAD_DIM = 64
WINDOW = 128
ATTN_SCALE = A_HEAD_DIM ** -0.5
NUM_BUCKETS = 32
MAX_DISTANCE = 128
A_Q = A_HEADS * A_HEAD_DIM
A_KV = A_KV_HEADS * A_HEAD_DIM
B_GROUPS = 8
B_GROUP_DIM = 64
B_WIDTH = B_GROUPS * B_GROUP_DIM
B_CHUNK = 128
AB_IN = A_Q + 2 * A_KV + 2 * B_WIDTH
AB_MIX = A_Q + B_WIDTH
C_HEADS = 8
C_KEY_DIM = 128
C_VAL_DIM = 128
C_F = C_HEADS * C_KEY_DIM
C_V = C_HEADS * C_VAL_DIM
C_IN = 2 * C_F + 2 * C_V
D_FF = 2816
EPS = 1e-6
LOG2E = math.log2(math.e)

NEG = -1e30

VMEM_LIMIT = 56 * 1024 * 1024
VREG_ROWS = 8

CHUNK = 128
TQ = 512
TQH = 512
TM = 512
HEAD_SKEW = 2
SAFE_LOG2_RANGE = 64.0
PREFIX_GROUP = 4
FACTORED_UNROLL = 4

_NT = (((1,), (1,)), ((), ()))
_TN = (((0,), (0,)), ((), ()))


def _rms(x, g):
    return x * lax.rsqrt(jnp.mean(x * x, axis=-1, keepdims=True) + EPS) * g


def _gelu(x):
    return 0.5 * x * (1.0 + lax.erf(x * math.sqrt(0.5)))


def _layernorm(x, g, b):
    xc = x - jnp.mean(x, axis=-1, keepdims=True)
    return xc * lax.rsqrt(jnp.mean(xc * xc, axis=-1, keepdims=True) + EPS) * g + b


def _dot(a, b):
    return jnp.dot(a, b, preferred_element_type=F32)


def _full(shape):
    n = len(shape)
    return pl.BlockSpec(shape, lambda *_: (0,) * n)


def _resident(shape):
    n = len(shape)
    return pl.BlockSpec(shape, lambda *_: (0,) * n, pipeline_mode=pl.Buffered(1))


_SMEM = pl.BlockSpec(memory_space=pltpu.SMEM)


def _bias_table_kernel(rel_ref, sb_ref, tabp_ref):
    qi = lax.broadcasted_iota(jnp.int32, (WINDOW, 2 * WINDOW), 0)
    kj = lax.broadcasted_iota(jnp.int32, (WINDOW, 2 * WINDOW), 1)
    dist = qi + WINDOW - kj
    ok = (dist >= 0) & (dist < WINDOW)
    max_exact = NUM_BUCKETS // 2
    d = jnp.maximum(dist, 0)
    dl = jnp.maximum(d, 1).astype(F32)
    v = (jnp.log(dl / max_exact) / math.log(MAX_DISTANCE / max_exact) * (NUM_BUCKETS - max_exact))
    far = d >= max_exact
    hits = []
    for b in range(NUM_BUCKETS):
        if b < max_exact:
            hits.append(d == b)
        elif b < NUM_BUCKETS - 1:
            hits.append(far & (v >= b - max_exact) & (v < b - max_exact + 1))
        else:
            hits.append(far & (v >= b - max_exact))
    for h in range(A_HEADS):
        acc = jnp.zeros((WINDOW, 2 * WINDOW), F32)
        for b in range(NUM_BUCKETS):
            acc = jnp.where(hits[b], rel_ref[b, h], acc)
        t = jnp.where(ok, acc, NEG)
        last = pltpu.roll(t[WINDOW - 1:WINDOW, :], WINDOW + 1, 1)
        sb_ref[h:h + 1, :] = jnp.where(kj[0:1, :] <= WINDOW, last, 0.0)
        cols = slice((h % 2) * 2 * WINDOW, (h % 2 + 1) * 2 * WINDOW)
        tabp_ref[0, h // 2, :, cols] = t * LOG2E
        tabp_ref[1, h // 2, :, cols] = jnp.where(kj < WINDOW, NEG, t * LOG2E)


PAIR = 2 * A_HEAD_DIM
N_PAIRS = A_HEADS // 2


def _sample_cache_attention(ck_ref, cv_ref, qx_ref, kn_ref, vn_ref, sb_ref, sink_ref,
                            nk_ref, nv_ref, om_ref):
    wb = ck_ref.shape[2]
    first_group = lax.broadcasted_iota(jnp.int32, (1, A_HEADS, A_HEAD_DIM), 1) < A_GROUP
    newest = lax.broadcasted_iota(jnp.int32, (1, 1, wb), 2) == wb - 1
    sink = sink_ref[...][None]
    kc, vc = ck_ref[...], cv_ref[...]
    kn, vn = kn_ref[...], vn_ref[...]
    kn_cols, vn_cols = kn_ref[:, 0, :].T, vn_ref[:, 0, :].T
    for i in range(kc.shape[0]):
        nk_ref[i] = jnp.where(newest[0], kn_cols[:, i:i + 1], pltpu.roll(kc[i], wb - 1, 1))
        nv_ref[i] = jnp.where(newest[0], vn_cols[:, i:i + 1], pltpu.roll(vc[i], wb - 1, 1))
    q = qx_ref[...]
    s = jnp.einsum('bhd,bdk->bhk', q.astype(BF16), kc.astype(BF16), preferred_element_type=F32)
    s = s + sb_ref[:, 0:wb][None]
    sn = jnp.sum(q * kn, axis=-1, keepdims=True) + sb_ref[:, wb:wb + 1][None]
    m = jnp.maximum(jnp.maximum(jnp.max(s, axis=-1, keepdims=True), sn), sink)
    e = jnp.exp(s - m)
    en = jnp.exp(sn - m)
    r = 1.0 / (jnp.sum(e, axis=-1, keepdims=True) + en + jnp.exp(sink - m))
    o = jnp.einsum('bhk,bdk->bhd', (e * r).astype(BF16), vc.astype(BF16),
                   preferred_element_type=F32) + (en * r) * vn
    om_ref[...] = jnp.where(first_group, o[:, :, 0:A_HEAD_DIM], o[:, :, A_HEAD_DIM:A_KV])


def _ab_prompt_kernel(sink_ref, x_ref, nm_ref, win_ref, qg_ref, kg_ref, tabp_ref, lng_ref, lnb_ref,
                      ws_ref, bsp_ref, wout_ref,
                      ck_ref, cv_ref, qx_ref, kn_ref, vn_ref, sb_ref, sinkc_ref,
                      y_ref, knew_ref, vnew_ref, gv_ref, nk_ref, nv_ref, om_ref,
                      z_ref, mix_ref, q_ref, k_ref, kr_ref, v_ref, vr_ref, wpair_ref, kl_ref, vl_ref, gl_ref):
    j = pl.program_id(1)
    last_j = pl.num_programs(1) - 1
    n_chunks = TQ // CHUNK

    @pl.when(j == 0)
    def _():
        for ref in (k_ref, kr_ref, v_ref, vr_ref):
            ref[0:CHUNK, :] = jnp.zeros((CHUNK, A_KV), BF16)
        row = lax.broadcasted_iota(jnp.int32, (B_CHUNK, B_CHUNK), 0)
        col = lax.broadcasted_iota(jnp.int32, (B_CHUNK, B_CHUNK), 1)
        for g in range(B_GROUPS):
            wpair_ref[g // 2, :, (g % 2) * B_CHUNK:(g % 2 + 1) * B_CHUNK] = jnp.where(
                row >= col, ws_ref[g], 0.0).astype(BF16)

    h = _rms(x_ref[0], nm_ref[...]).astype(BF16)
    z_ref[...] = _dot(h, win_ref[...])

    _sample_cache_attention(ck_ref, cv_ref, qx_ref, kn_ref, vn_ref, sb_ref, sinkc_ref,
                            nk_ref, nv_ref, om_ref)

    lo_half = lax.broadcasted_iota(jnp.int32, (1, PAIR), 1) < A_HEAD_DIM

    def mean_sq_halves(x):
        x2 = x * x
        lo_sum = jnp.sum(jnp.where(lo_half, x2, 0.0), axis=-1, keepdims=True)
        hi_sum = jnp.sum(jnp.where(lo_half, 0.0, x2), axis=-1, keepdims=True)
        return jnp.where(lo_half, lo_sum, hi_sum) * (1.0 / A_HEAD_DIM)

    def block_diag(top, bot):
        zero = jnp.zeros_like(top)
        return jnp.concatenate([jnp.where(lo_half, top, zero), jnp.where(lo_half, zero, bot)], axis=0)

    kraw = z_ref[:, A_Q:A_Q + A_KV]
    v_all = z_ref[:, A_Q + A_KV:A_Q + 2 * A_KV]
    kn_all = kraw * lax.rsqrt(mean_sq_halves(kraw) + EPS) * kg_ref[...]
    k_ref[CHUNK:, :] = kn_all.astype(BF16)
    kr_ref[CHUNK:, :] = pltpu.roll(kn_all, A_HEAD_DIM, 1).astype(BF16)
    v_ref[CHUNK:, :] = v_all.astype(BF16)
    vr_ref[CHUNK:, :] = pltpu.roll(v_all, A_HEAD_DIM, 1).astype(BF16)
    kl_ref[...] = kn_all[TQ - CHUNK:, :]
    vl_ref[...] = v_all[TQ - CHUNK:, :]
    for i in range(N_PAIRS):
        ps = slice(i * PAIR, (i + 1) * PAIR)
        qraw = z_ref[:, ps]
        qn = qraw * lax.rsqrt(mean_sq_halves(qraw) + EPS) * (qg_ref[:, ps] * (ATTN_SCALE * LOG2E))
        q_ref[:, ps] = qn.astype(BF16)

    def chunk(c, carry):
        r0 = c * CHUNK
        rows = pl.ds(r0, CHUNK)
        first = jnp.where(jnp.logical_and(j == 0, c == 0), 1, 0)

        both = pl.ds(r0, 2 * CHUNK)
        k2, k2r, v2, v2r = k_ref[both, :], kr_ref[both, :], v_ref[both, :], vr_ref[both, :]
        kbd = [block_diag(k2, k2r), block_diag(k2r, k2)]
        vbd = [block_diag(v2, v2r), block_diag(v2r, v2)]

        scores = []
        for i in range(N_PAIRS):
            s = lax.dot_general(q_ref[rows, i * PAIR:(i + 1) * PAIR], kbd[i // (A_GROUP // 2)], _NT,
                                preferred_element_type=F32)
            scores.append(s + tabp_ref[first, i])
        outs = []
        for i in range(N_PAIRS):
            es, rs = [], []
            for hh in range(2):
                sh = scores[i][:, hh * 2 * WINDOW:(hh + 1) * 2 * WINDOW]
                sk = sink_ref[2 * i + hh] * LOG2E
                m = jnp.maximum(jnp.max(sh, axis=-1, keepdims=True), sk)
                e = jnp.exp2(sh - m)
                rs.append(1.0 / (jnp.sum(e, axis=-1, keepdims=True) + jnp.exp2(sk - m)))
                es.append(e.astype(BF16))
            o = _dot(jnp.concatenate(es, axis=-1), vbd[i // (A_GROUP // 2)])
            outs.append(o * jnp.where(lo_half, rs[0], rs[1]))
        mix_ref[rows, 0:A_Q] = jnp.concatenate(outs, axis=-1).astype(BF16)

        zu = z_ref[rows, A_Q + 2 * A_KV:A_Q + 2 * A_KV + B_WIDTH]
        zv = z_ref[rows, A_Q + 2 * A_KV + B_WIDTH:AB_IN]
        u = _gelu(zu)
        vln = _layernorm(_gelu(zv), lng_ref[...], lnb_ref[...])
        vlb = vln.astype(BF16)
        sparts = []
        for i in range(B_GROUPS // 2):
            vpair = vlb[:, i * PAIR:(i + 1) * PAIR]
            sparts.append(_dot(wpair_ref[i], block_diag(vpair, vpair)))
        bm = u * (jnp.concatenate(sparts, axis=-1) + bsp_ref[...])
        mix_ref[rows, A_Q:AB_MIX] = bm.astype(BF16)

        gl_ref[...] = vln
        return carry

    for c in range(n_chunks):
        chunk(c, 0)
        done = slice(c * CHUNK, (c + 1) * CHUNK)
        y_ref[0, done, :] = x_ref[0, done, :] + _dot(mix_ref[done, :], wout_ref[...])
    for ref in (k_ref, kr_ref, v_ref, vr_ref):
        ref[0:CHUNK, :] = ref[TQ:TQ + CHUNK, :]

    @pl.when(j == last_j)
    def _():
        knew_ref[0] = kl_ref[...]
        vnew_ref[0] = vl_ref[...]
        gv_ref[0] = gl_ref[...]


def _ab_prompt(x, nm, w_in, qg, kg, sink, tabp, lng, lnb, w_s, bsp, w_out, ck, cv, qx, kn_s, vn_s, sb):
    nb, seq, _ = x.shape
    nj = seq // TQ
    grid = (nb, nj)
    ns, _, wb = ck.shape
    sba = ns // (nb * nj)
    assert sba * nb * nj == ns
    blk = lambda b, j: (b, j, 0)
    per_b = lambda b, j: (b, 0, 0)
    step = lambda b, j: (b * nj + j, 0, 0)
    return pl.pallas_call(
        _ab_prompt_kernel,
        grid=grid,
        in_specs=[
            _SMEM,
            pl.BlockSpec((1, TQ, D_MODEL), blk),
            _full((1, D_MODEL)),
            _resident((D_MODEL, AB_IN)),
            _full((1, A_Q)),
            _full((1, A_KV)),
            _resident((2, N_PAIRS, WINDOW, 4 * WINDOW)),
            _full((1, B_WIDTH)),
            _full((1, B_WIDTH)),
            _resident((B_GROUPS, B_CHUNK, B_CHUNK)),
            _resident((B_CHUNK, B_WIDTH)),
            _resident((AB_MIX, D_MODEL)),
            pl.BlockSpec((sba, A_KV, wb), step),
            pl.BlockSpec((sba, A_KV, wb), step),
            pl.BlockSpec((sba, A_HEADS, A_KV), step),
            pl.BlockSpec((sba, 1, A_KV), step),
            pl.BlockSpec((sba, 1, A_KV), step),
            _full((A_HEADS, 2 * WINDOW)),
            _full((A_HEADS, 1)),
        ],
        out_specs=[
            pl.BlockSpec((1, TQ, D_MODEL), blk),
            pl.BlockSpec((1, WINDOW, A_KV), per_b),
            pl.BlockSpec((1, WINDOW, A_KV), per_b),
            pl.BlockSpec((1, B_CHUNK, B_WIDTH), per_b),
            pl.BlockSpec((sba, A_KV, wb), step),
            pl.BlockSpec((sba, A_KV, wb), step),
            pl.BlockSpec((sba, A_HEADS, A_HEAD_DIM), step),
        ],
        out_shape=[
            jax.ShapeDtypeStruct((nb, seq, D_MODEL), F32),
            jax.ShapeDtypeStruct((nb, WINDOW, A_KV), F32),
            jax.ShapeDtypeStruct((nb, WINDOW, A_KV), F32),
            jax.ShapeDtypeStruct((nb, B_CHUNK, B_WIDTH), F32),
            jax.ShapeDtypeStruct((ns, A_KV, wb), F32),
            jax.ShapeDtypeStruct((ns, A_KV, wb), F32),
            jax.ShapeDtypeStruct((ns, A_HEADS, A_HEAD_DIM), F32),
        ],
        scratch_shapes=[
            pltpu.VMEM((TQ, AB_IN), F32),
            pltpu.VMEM((TQ, AB_MIX), BF16),
            pltpu.VMEM((TQ, A_Q), BF16),
            pltpu.VMEM((CHUNK + TQ, A_KV), BF16),
            pltpu.VMEM((CHUNK + TQ, A_KV), BF16),
            pltpu.VMEM((CHUNK + TQ, A_KV), BF16),
            pltpu.VMEM((CHUNK + TQ, A_KV), BF16),
            pltpu.VMEM((B_GROUPS // 2, B_CHUNK, 2 * B_CHUNK), BF16),
            pltpu.VMEM((WINDOW, A_KV), F32),
            pltpu.VMEM((WINDOW, A_KV), F32),
            pltpu.VMEM((B_CHUNK, B_WIDTH), F32),
        ],
        compiler_params=pltpu.CompilerParams(
            dimension_semantics=("arbitrary", "arbitrary"), vmem_limit_bytes=VMEM_LIMIT),
        name="ab_prompt",
    )(sink, x, nm, w_in, qg, kg, tabp, lng, lnb, w_s, bsp, w_out,
      ck, cv, qx, kn_s[:, None, :], vn_s[:, None, :], sb, sink.reshape(A_HEADS, 1))


FF_TILE = 256


def _ffn_kernel(xp_ref, xs_ref, ms_ref, wo_ref, g_ref, wg_ref, wu_ref, wd_ref, yp_ref, ys_ref,
                wg_s, wu_s, wd_s, h0_s, acc_s, *, n_cast, n_prompt):
    s = pl.program_id(0)

    def gated(h, wg, wu):
        gate = _dot(h, wg)
        return (gate * jax.nn.sigmoid(gate) * _dot(h, wu)).astype(BF16)

    @pl.when(s == 0)
    def _():
        x = xp_ref[...]
        h0_s[...] = _rms(x, g_ref[...]).astype(BF16)
        acc_s[...] = x

    for c in range(n_cast):
        @pl.when(s == c)
        def _(c=c):
            tile = slice(c * FF_TILE, (c + 1) * FF_TILE)
            wg_t, wu_t, wd_t = (r[...].astype(BF16) for r in (wg_ref, wu_ref, wd_ref))
            wg_s[:, tile] = wg_t
            wu_s[:, tile] = wu_t
            wd_s[tile, :] = wd_t
            acc_s[...] += _dot(gated(h0_s[...], wg_t, wu_t), wd_t)

    @pl.when(s == n_cast - 1)
    def _():
        yp_ref[...] = acc_s[...]

    def swiglu(x):
        h = _rms(x, g_ref[...]).astype(BF16)
        return x + _dot(gated(h, wg_s[...], wu_s[...]), wd_s[...])

    @pl.when(jnp.logical_and(s >= n_cast, s < n_cast + n_prompt - 1))
    def _():
        yp_ref[...] = swiglu(xp_ref[...])

    @pl.when(s == n_cast + n_prompt - 1)
    def _():
        ys_ref[...] = swiglu(xs_ref[...] + _dot(ms_ref[...].astype(BF16), wo_ref[...]))


def _ffn(xp, xs, mix_s, w_o, g, w_gate, w_up, w_down, layer):
    rows, ns = xp.shape[0], xs.shape[0]
    n_cast, n_prompt = D_FF // FF_TILE, rows // TM
    w_tile = lambda s: jnp.minimum(s, n_cast - 1)
    row_blk = lambda s: (jnp.clip(s - (n_cast - 1), 0, n_prompt - 1), 0)
    return pl.pallas_call(
        functools.partial(_ffn_kernel, n_cast=n_cast, n_prompt=n_prompt),
        grid=(n_cast + n_prompt,),
        in_specs=[
            pl.BlockSpec((TM, D_MODEL), row_blk),
            _full((ns, D_MODEL)),
            _full((ns, D_MODEL)),
            _resident((D_MODEL, D_MODEL)),
            _full((1, D_MODEL)),
            pl.BlockSpec((None, D_MODEL, FF_TILE), lambda s: (layer, 0, w_tile(s))),
            pl.BlockSpec((None, D_MODEL, FF_TILE), lambda s: (layer, 0, w_tile(s))),
            pl.BlockSpec((None, FF_TILE, D_MODEL), lambda s: (layer, w_tile(s), 0)),
        ],
        out_specs=[pl.BlockSpec((TM, D_MODEL), row_blk), _full((ns, D_MODEL))],
        out_shape=[jax.ShapeDtypeStruct((rows, D_MODEL), F32), jax.ShapeDtypeStruct((ns, D_MODEL), F32)],
        scratch_shapes=[
            pltpu.VMEM((D_MODEL, D_FF), BF16),
            pltpu.VMEM((D_MODEL, D_FF), BF16),
            pltpu.VMEM((D_FF, D_MODEL), BF16),
            pltpu.VMEM((TM, D_MODEL), BF16),
            pltpu.VMEM((TM, D_MODEL), F32),
        ],
        compiler_params=pltpu.CompilerParams(
            dimension_semantics=("arbitrary",), vmem_limit_bytes=VMEM_LIMIT),
        name="ffn",
    )(xp, xs, mix_s, w_o, g, w_gate, w_up, w_down)


def _lower_bound(clb):
    m = jnp.max(clb, axis=0, keepdims=True)
    e = jnp.exp(clb - m)
    sm = e / jnp.sum(e, axis=0, keepdims=True)
    return (sm[0:1] + sm[1:2]) - sm[0:1]


def _split3(x):
    hi = x.astype(BF16)
    r = x - hi.astype(F32)
    mid = r.astype(BF16)
    lo = (r - mid.astype(F32)).astype(BF16)
    return hi, mid, lo


def _neg_abs(x):
    return lax.bitcast_convert_type(
        lax.bitcast_convert_type(x, jnp.uint32) | jnp.uint32(0x80000000), F32)


def _pair_level_table():
    t = np.arange(CHUNK)[:, None]
    s = np.arange(CHUNK)[None, :]
    lev = np.floor(np.log2(np.maximum(t ^ s, 1))).astype(np.int32)
    lev = np.where(t == s, -1, lev)
    return np.where(s > t, -2, lev).astype(np.int32)


def _level_operand(p, q, kk, f, b2):
    m = 2 ** p
    if m < VREG_ROWS:
        shape3 = (CHUNK // VREG_ROWS, VREG_ROWS, q.shape[1])
        sub = lax.broadcasted_iota(jnp.int32, (1, VREG_ROWS, q.shape[1]), 1)
        upper = ((sub >> p) & 1) == 1
        q3, k3 = q.reshape(shape3), kk.reshape(shape3)
        if p == 0:
            y = jnp.where(upper, q3 * f.reshape(shape3), k3)
        else:
            b3 = b2.reshape(shape3)
            be = b3[:, m - 1:m, :]
            for k in range(1, VREG_ROWS // (2 * m)):
                be = jnp.where(sub >= 2 * m * k, b3[:, 2 * m * k + m - 1:2 * m * k + m, :], be)
            y = jnp.where(upper, q3, k3) * jnp.exp2(_neg_abs(b3 - be))
        return y.reshape(q.shape).astype(BF16)
    parts = []
    for k in range(CHUNK // (2 * m)):
        lo = slice(2 * m * k, 2 * m * k + m)
        up = slice(2 * m * k + m, 2 * m * (k + 1))
        be = b2[2 * m * k + m - 1:2 * m * k + m, :]
        parts.append(kk[lo] * jnp.exp2(be - b2[lo]))
        parts.append(q[up] * jnp.exp2(b2[up] - be))
    return jnp.concatenate(parts, axis=0).astype(BF16)


def _merge_level(p, att, pm, lev):
    m = 2 ** p
    if m < VREG_ROWS:
        return jnp.where(lev == p, pm, att)
    col = lax.broadcasted_iota(jnp.int32, (1, CHUNK), 1)
    parts = []
    for k in range(CHUNK // (2 * m)):
        lo = slice(2 * m * k, 2 * m * k + m)
        up = slice(2 * m * k + m, 2 * m * (k + 1))
        parts.append(att[lo])
        parts.append(jnp.where((col >= 2 * m * k) & (col < 2 * m * k + m), pm[up], att[up]))
    return jnp.concatenate(parts, axis=0)


def _hgrn_prompt_kernel(x_ref, nm_ref, win_ref, clb_ref, on_ref, wout_ref, lev_ref,
                        ss_ref, fs_ref, qs_ref, is_ref, gs_ref,
                        y_ref, st_ref, sso_ref, os_ref,
                        z_ref, o_ref, stt_ref, k_ref):
    j = pl.program_id(1)
    last_j = pl.num_programs(1) - 1
    n_chunks = TQH // CHUNK
    n_levels = int(math.log2(CHUNK))

    @pl.when(j == 0)
    def _():
        stt_ref[...] = jnp.zeros_like(stt_ref)

    h = _rms(x_ref[0], nm_ref[...]).astype(BF16)
    z_ref[...] = _dot(h, win_ref[...])

    head_cols = lambda ref: [ref[:, hd * C_KEY_DIM:(hd + 1) * C_KEY_DIM].T for hd in range(C_HEADS)]
    f_cols, q_cols = head_cols(fs_ref), head_cols(qs_ref)
    out_rows = []
    for smp in range(ss_ref.shape[0]):
        parts = []
        for hd in range(C_HEADS):
            hs = slice(hd * C_VAL_DIM, (hd + 1) * C_VAL_DIM)
            fb = jnp.broadcast_to(f_cols[hd][:, smp:smp + 1], (C_KEY_DIM, C_VAL_DIM))
            sn = fb * ss_ref[smp, hd] + (1.0 - fb) * is_ref[smp:smp + 1, hs]
            sso_ref[smp, hd] = sn
            o = jnp.sum(q_cols[hd][:, smp:smp + 1] * sn, axis=0, keepdims=True)
            parts.append(_rms(o, on_ref[...]))
        out_rows.append(jnp.concatenate(parts, axis=-1))
    os_ref[...] = jnp.concatenate(out_rows, axis=0) * gs_ref[...]

    lb = _lower_bound(clb_ref[...])

    row = lax.broadcasted_iota(jnp.int32, (CHUNK, CHUNK), 0)
    col = lax.broadcasted_iota(jnp.int32, (CHUNK, CHUNK), 1)
    ltri = (row >= col).astype(BF16)

    def chunk_rows(c):
        return pl.ds(pl.multiple_of(c * CHUNK, CHUNK), CHUNK)

    def prefix(g, worst):
        rows = [chunk_rows(g * PREFIX_GROUP + i) for i in range(PREFIX_GROUP)]
        gates = [z_ref[r, C_F:2 * C_F] for r in rows]
        for r, gate in zip(rows, gates):
            f_all = lb + (1.0 - lb) * jax.nn.sigmoid(gate)
            k_ref[r, :] = 1.0 - f_all
            hi, mid, lo = _split3(jnp.log2(f_all))
            b2 = (_dot(ltri, hi) + _dot(ltri, mid)) + _dot(ltri, lo)
            z_ref[r, C_F:2 * C_F] = b2
            b_mid = b2[CHUNK // 2 - 1:CHUNK // 2, :]
            b_last = b2[CHUNK - 1:CHUNK, :]
            worst = jnp.maximum(worst, jnp.maximum(-b_mid, b_mid - b_last))
        return worst

    worst = lax.fori_loop(0, n_chunks // PREFIX_GROUP, prefix, jnp.zeros((1, C_F), F32))
    bounded = jnp.max(worst) <= SAFE_LOG2_RANGE

    def finish_head(rows, hd, o):
        gt = z_ref[rows, 2 * C_F + C_V + hd * C_VAL_DIM:2 * C_F + C_V + (hd + 1) * C_VAL_DIM]
        o = _rms(o, on_ref[...]) * jax.nn.sigmoid(gt)
        o_ref[rows, hd * C_VAL_DIM:(hd + 1) * C_VAL_DIM] = o.astype(BF16)

    def head_inputs(rows, hd):
        q = z_ref[rows, hd * C_KEY_DIM:(hd + 1) * C_KEY_DIM]
        kk = k_ref[rows, hd * C_KEY_DIM:(hd + 1) * C_KEY_DIM]
        b2 = z_ref[rows, C_F + hd * C_KEY_DIM:C_F + (hd + 1) * C_KEY_DIM]
        ivb = z_ref[rows, 2 * C_F + hd * C_VAL_DIM:2 * C_F + (hd + 1) * C_VAL_DIM].astype(BF16)
        return q, kk, b2, ivb

    def factored_chunk(c, carry):
        rows = chunk_rows(c)
        for hd in range(C_HEADS):
            q, kk, b2, ivb = head_inputs(rows, hd)
            b_mid = b2[CHUNK // 2 - 1:CHUNK // 2, :]
            b_last = b2[CHUNK - 1:CHUNK, :]
            qs = (q * jnp.exp2(b2 - b_mid)).astype(BF16)
            kd = (kk * jnp.exp2(b_mid - b2)).astype(BF16)
            att = jnp.where(row >= col, lax.dot_general(qs, kd, _NT, preferred_element_type=F32), 0.0)
            stt = stt_ref[hd]
            o = lax.dot_general(qs, (stt * jnp.exp2(b_mid)).astype(BF16), _NT,
                                preferred_element_type=F32) + _dot(att.astype(BF16), ivb)
            stt_ref[hd] = stt * jnp.exp2(b_last) + jnp.exp2(b_last - b_mid) * lax.dot_general(
                ivb, kd, _TN, preferred_element_type=F32)
            finish_head(rows, hd, o)
        return carry

    def tree_chunk(c, carry):
        rows = chunk_rows(c)
        lev = lev_ref[...]

        def products(hd):
            q, kk, b2, ivb = head_inputs(rows, hd)
            diag = jnp.sum(q * kk, axis=-1, keepdims=True)
            pms = []
            for p in range(n_levels):
                y = _level_operand(p, q, kk, 1.0 - kk, b2)
                pms.append(lax.dot_general(y, y, _NT, preferred_element_type=F32))
            stt = stt_ref[hd]
            o_prev = lax.dot_general((q * jnp.exp2(b2)).astype(BF16), stt.astype(BF16), _NT,
                                     preferred_element_type=F32)
            b_last = b2[CHUNK - 1:CHUNK, :]
            kd = (kk * jnp.exp2(b_last - b2)).astype(BF16)
            stt_ref[hd] = stt * jnp.exp2(b_last) + lax.dot_general(
                ivb, kd, _TN, preferred_element_type=F32)
            return diag, pms, o_prev, ivb

        def finish(hd, diag, pms, o_prev, ivb):
            att = jnp.where(lev == -1, diag, 0.0)
            for p in range(n_levels):
                att = _merge_level(p, att, pms[p], lev)
            finish_head(rows, hd, o_prev + _dot(att.astype(BF16), ivb))

        pending = [products(hd) for hd in range(HEAD_SKEW)]
        for hd in range(C_HEADS):
            if hd + HEAD_SKEW < C_HEADS:
                pending.append(products(hd + HEAD_SKEW))
            finish(hd, *pending.pop(0))
        return carry

    @pl.when(bounded)
    def _():
        lax.fori_loop(0, n_chunks, factored_chunk, 0, unroll=FACTORED_UNROLL)

    @pl.when(jnp.logical_not(bounded))
    def _():
        lax.fori_loop(0, n_chunks, tree_chunk, 0)

    y_ref[0] = x_ref[0] + _dot(o_ref[...], wout_ref[...])

    @pl.when(j == last_j)
    def _():
        for hd in range(C_HEADS):
            st_ref[0, hd] = stt_ref[hd].T


def _hgrn_prompt(x, nm, w_in, clb, on, w_out, state_s, f_s, q_s, i_s, g_s):
    nb, seq, _ = x.shape
    nj = seq // TQH
    ns = state_s.shape[0]
    sbh = f_s.shape[1]
    assert f_s.shape[0] == nb * nj and sbh * nb * nj == ns
    blk = lambda b, j: (b, j, 0)
    step = lambda b, j: (b * nj + j, 0, 0)
    step4 = lambda b, j: (b * nj + j, 0, 0, 0)
    y, st, st_s, o_s = pl.pallas_call(
        _hgrn_prompt_kernel,
        grid=(nb, nj),
        in_specs=[
            pl.BlockSpec((1, TQH, D_MODEL), blk),
            _full((1, D_MODEL)),
            _resident((D_MODEL, C_IN)),
            _full((DEPTH, C_F)),
            _full((1, C_VAL_DIM)),
            _resident((C_V, D_MODEL)),
            _full((CHUNK, CHUNK)),
            pl.BlockSpec((sbh, C_HEADS, C_KEY_DIM, C_VAL_DIM), step4),
            pl.BlockSpec((None, sbh, C_F), step),
            pl.BlockSpec((None, sbh, C_F), step),
            pl.BlockSpec((None, sbh, C_V), step),
            pl.BlockSpec((None, sbh, C_V), step),
        ],
        out_specs=[
            pl.BlockSpec((1, TQH, D_MODEL), blk),
            pl.BlockSpec((1, C_HEADS, C_KEY_DIM, C_VAL_DIM), lambda b, j: (b, 0, 0, 0)),
            pl.BlockSpec((sbh, C_HEADS, C_KEY_DIM, C_VAL_DIM), step4),
            pl.BlockSpec((None, sbh, C_V), step),
        ],
        out_shape=[
            jax.ShapeDtypeStruct((nb, seq, D_MODEL), F32),
            jax.ShapeDtypeStruct((nb, C_HEADS, C_KEY_DIM, C_VAL_DIM), F32),
            jax.ShapeDtypeStruct(state_s.shape, F32),
            jax.ShapeDtypeStruct((ns // sbh, sbh, C_V), F32),
        ],
        scratch_shapes=[
            pltpu.VMEM((TQH, C_IN), F32),
            pltpu.VMEM((TQH, C_V), BF16),
            pltpu.VMEM((C_HEADS, C_VAL_DIM, C_KEY_DIM), F32),
            pltpu.VMEM((TQH, C_F), F32),
        ],
        compiler_params=pltpu.CompilerParams(
            dimension_semantics=("arbitrary", "arbitrary"), vmem_limit_bytes=VMEM_LIMIT),
        name="hgrn_prompt",
    )(x, nm, w_in, clb, on, w_out, jnp.asarray(_pair_level_table()),
      state_s, f_s, q_s, i_s, g_s)
    return y, st, st_s, o_s.reshape(ns, C_V)


def _ab_sample_proj_kernel(w00_ref, b0_ref, rel_ref, x_ref, nm_ref, win_ref, qn_ref, kn_ref, lng_ref, lnb_ref,
                           wout_ref, qx_ref, knew_ref, vnew_ref, bm_ref, gv_ref, wbf_ref, sb_ref, tabp_ref,
                           woutbf_ref, h_s, z_s):
    t = pl.program_id(0)
    half = AB_IN // 2

    @pl.when(t == 0)
    def _():
        h_s[...] = _rms(x_ref[...], nm_ref[...]).astype(BF16)

    wbf_ref[...] = win_ref[...].astype(BF16)
    woutbf_ref[...] = wout_ref[...].astype(BF16)
    zt = _dot(h_s[...], wbf_ref[...])

    @pl.when(t == 0)
    def _():
        z_s[:, 0:half] = zt
        _bias_table_kernel(rel_ref, sb_ref, tabp_ref)

    @pl.when(t == 1)
    def _():
        z_s[:, half:AB_IN] = zt
        _ab_sample_post(w00_ref, b0_ref, z_s[...], qn_ref, kn_ref, lng_ref, lnb_ref,
                        qx_ref, knew_ref, vnew_ref, bm_ref, gv_ref)


def _ab_sample_post(w00_ref, b0_ref, z, qn_ref, kn_ref, lng_ref, lnb_ref,
                    qx_ref, knew_ref, vnew_ref, bm_ref, gv_ref):
    n = z.shape[0]
    zeros = jnp.zeros((n, A_HEAD_DIM), F32)
    for hh in range(A_HEADS):
        qh = _rms(z[:, hh * A_HEAD_DIM:(hh + 1) * A_HEAD_DIM], qn_ref[...]) * ATTN_SCALE
        qx_ref[:, hh, :] = jnp.concatenate([qh, zeros] if hh // A_GROUP == 0 else [zeros, qh], axis=-1)
    kparts = []
    for g in range(A_KV_HEADS):
        kparts.append(_rms(z[:, A_Q + g * A_HEAD_DIM:A_Q + (g + 1) * A_HEAD_DIM], kn_ref[...]))
    knew_ref[...] = jnp.concatenate(kparts, axis=-1)
    vnew_ref[...] = z[:, A_Q + A_KV:A_Q + 2 * A_KV]

    u = _gelu(z[:, A_Q + 2 * A_KV:A_Q + 2 * A_KV + B_WIDTH])
    vln = _layernorm(_gelu(z[:, A_Q + 2 * A_KV + B_WIDTH:AB_IN]), lng_ref[...], lnb_ref[...])
    grp = lax.broadcasted_iota(jnp.int32, (1, B_WIDTH), 1) // B_GROUP_DIM
    srow = jnp.zeros((1, B_WIDTH), F32)
    brow = jnp.zeros((1, B_WIDTH), F32)
    for g in range(B_GROUPS):
        srow = jnp.where(grp == g, w00_ref[g], srow)
        brow = jnp.where(grp == g, b0_ref[g], brow)
    bm_ref[...] = u * (vln * srow + brow)
    gv_ref[...] = vln


def _ab_sample_proj(x, nm, w_in, qn, kn, lng, lnb, w00, b0, rel_bias, w_out):
    n = x.shape[0]
    assert AB_IN % (2 * PAIR) == 0 and w_out.shape[0] % 2 == 0
    win_spec = pl.BlockSpec((D_MODEL, AB_IN // 2), lambda t: (0, t))
    wout_spec = pl.BlockSpec((w_out.shape[0] // 2, D_MODEL), lambda t: (t, 0))
    shapes = [(n, A_HEADS, A_KV), (n, A_KV), (n, A_KV), (n, B_WIDTH), (n, B_WIDTH)]
    tables = [(A_HEADS, 2 * WINDOW), (2, A_HEADS // 2, WINDOW, 4 * WINDOW)]
    return pl.pallas_call(
        _ab_sample_proj_kernel,
        grid=(2,),
        in_specs=[_SMEM, _SMEM, _SMEM, _full((n, D_MODEL)), _full((1, D_MODEL)), win_spec,
                  _full(qn.shape), _full(kn.shape), _full(lng.shape), _full(lnb.shape), wout_spec],
        out_specs=[_full(s) for s in shapes] + [win_spec] + [_full(s) for s in tables] + [wout_spec],
        out_shape=[jax.ShapeDtypeStruct(s, F32) for s in shapes]
        + [jax.ShapeDtypeStruct(w_in.shape, BF16)]
        + [jax.ShapeDtypeStruct(s, F32) for s in tables]
        + [jax.ShapeDtypeStruct(w_out.shape, BF16)],
        scratch_shapes=[pltpu.VMEM((n, D_MODEL), BF16), pltpu.VMEM((n, AB_IN), F32)],
        compiler_params=pltpu.CompilerParams(
            dimension_semantics=("arbitrary",), vmem_limit_bytes=VMEM_LIMIT),
        name="ab_sample_proj",
    )(w00, b0, rel_bias, x, nm, w_in, qn, kn, lng, lnb, w_out)


def _hgrn_sample_proj_kernel(x_ref, nm_ref, win_ref, clb_ref, wout_ref, q_ref, f_ref, i_ref, sg_ref, wbf_ref,
                             woutbf_ref, h_s):
    t = pl.program_id(0)

    def put(ref, val):
        per = ref.shape[1]
        for i in range(ref.shape[0]):
            ref[i] = val[i * per:(i + 1) * per, :]

    @pl.when(t == 0)
    def _():
        h_s[...] = _rms(x_ref[...], nm_ref[...]).astype(BF16)

    wbf_ref[...] = win_ref[...].astype(BF16)
    woutbf_ref[...] = wout_ref[...].astype(BF16)
    z = _dot(h_s[...], wbf_ref[...])

    @pl.when(t == 0)
    def _():
        lb = _lower_bound(clb_ref[...])
        put(q_ref, z[:, 0:C_F])
        put(f_ref, lb + (1.0 - lb) * jax.nn.sigmoid(z[:, C_F:2 * C_F]))

    @pl.when(t == 1)
    def _():
        put(i_ref, z[:, 0:C_V])
        put(sg_ref, jax.nn.sigmoid(z[:, C_V:2 * C_V]))


def _hgrn_sample_proj(x, nm, w_in, clb, w_out, per_step):
    n = x.shape[0]
    assert C_F == C_V and C_IN == 4 * C_F and w_out.shape[0] % 2 == 0
    grouped = lambda width: jax.ShapeDtypeStruct((n // per_step, per_step, width), F32)
    wout_spec = pl.BlockSpec((w_out.shape[0] // 2, D_MODEL), lambda t: (t, 0))
    return pl.pallas_call(
        _hgrn_sample_proj_kernel,
        grid=(2,),
        in_specs=[
            _full((n, D_MODEL)),
            _full((1, D_MODEL)),
            pl.BlockSpec((D_MODEL, C_IN // 2), lambda t: (0, t)),
            _full((DEPTH, C_F)),
            wout_spec,
        ],
        out_specs=[_full((n // per_step, per_step, C_F))] * 2 + [_full((n // per_step, per_step, C_V))] * 2
        + [pl.BlockSpec((D_MODEL, C_IN // 2), lambda t: (0, t)), wout_spec],
        out_shape=[grouped(C_F)] * 2 + [grouped(C_V)] * 2
        + [jax.ShapeDtypeStruct(w_in.shape, BF16), jax.ShapeDtypeStruct(w_out.shape, BF16)],
        scratch_shapes=[pltpu.VMEM((n, D_MODEL), BF16)],
        compiler_params=pltpu.CompilerParams(
            dimension_semantics=("arbitrary",), vmem_limit_bytes=VMEM_LIMIT),
        name="hgrn_sample_proj",
    )(x, nm, w_in, clb, w_out)


def kernel(x_prompt, x_sample, cache_k, cache_v, state_hgrn, norm_mix, norm_ffn, w_in_ab, w_out_ab,
           q_norm, k_norm, attn_sink, rel_bias, gmlp_ln_g, gmlp_ln_b, gmlp_w_s, gmlp_b_s,
           w_in_c, c_lower_bounds, c_out_norm, w_out_c, w_gate, w_up, w_down):
    assert norm_mix.shape[0] == DEPTH == 2 and w_in_ab.shape[0] == 1 and w_in_c.shape[0] == 1
    nb, seq, _ = x_prompt.shape
    ns = x_sample.shape[0]
    assert x_sample.shape[1] == 1 and cache_k.shape[2] == WINDOW

    row = lambda v: v.reshape(1, -1)
    nm, nf = norm_mix, norm_ffn
    qn, kn = row(q_norm[0]), row(k_norm[0])
    lng, lnb = row(gmlp_ln_g[0]), row(gmlp_ln_b[0])
    sink = attn_sink[0]

    xs = x_sample.reshape(ns, D_MODEL)
    qx, knew_s, vnew_s, bm_s, gv_s, w_in_ab0, sb, tabp, w_out_ab0 = _ab_sample_proj(
        xs, row(nm[0]), w_in_ab[0], qn, kn, lng, lnb, gmlp_w_s[0, :, 0, 0], gmlp_b_s[0, :, 0],
        rel_bias, w_out_ab[0])
    to_t = lambda c: c.transpose(0, 2, 3, 1).reshape(ns, A_KV, WINDOW)
    xp, knew_p, vnew_p, gv_p, nk_s, nv_s, om = _ab_prompt(
        x_prompt, row(nm[0]), w_in_ab0, jnp.tile(qn, (1, A_HEADS)), jnp.tile(kn, (1, A_KV_HEADS)),
        sink, tabp, lng, lnb, gmlp_w_s[0], jnp.repeat(gmlp_b_s[0].T, B_GROUP_DIM, axis=1), w_out_ab0,
        to_t(cache_k[0]), to_t(cache_v[0]), qx, knew_s, vnew_s, sb)
    a_s = om.reshape(ns, A_Q)
    xp, xs = _ffn(xp.reshape(nb * seq, D_MODEL), xs, jnp.concatenate([a_s, bm_s], axis=-1), w_out_ab0,
                  row(nf[0]), w_gate, w_up, w_down, 0)

    q_s, f_s, i_s, sg_s, w_in_c0, w_out_c0 = _hgrn_sample_proj(
        xs, row(nm[1]), w_in_c[0], c_lower_bounds, w_out_c[0], ns // (nb * (seq // TQH)))
    xp, st_p, st_s, o_s = _hgrn_prompt(xp.reshape(nb, seq, D_MODEL), row(nm[1]), w_in_c0, c_lower_bounds,
                                       row(c_out_norm[0]), w_out_c0, state_hgrn[0], f_s, q_s, i_s, sg_s)
    xp, xs = _ffn(xp.reshape(nb * seq, D_MODEL), xs, o_s, w_out_c0, row(nf[1]), w_gate, w_up, w_down, 1)

    kv5 = lambda a: a.reshape(1, a.shape[0], WINDOW, A_KV_HEADS, A_HEAD_DIM)
    from_t = lambda a: a.reshape(ns, A_KV_HEADS, A_HEAD_DIM, WINDOW).transpose(0, 3, 1, 2)[None]
    return (xp.reshape(nb, seq, D_MODEL), xs.reshape(ns, 1, D_MODEL),
            kv5(knew_p), kv5(vnew_p), from_t(nk_s), from_t(nv_s),
            gv_p[None], gv_s.reshape(1, ns, 1, B_WIDTH),
            st_p[None], st_s[None])
```

```python
import functools
import math

import jax
import jax.numpy as jnp
import numpy as np
from jax import lax
from jax.experimental import pallas as pl
from jax.experimental.pallas import tpu as pltpu

F32 = jnp.float32
BF16 = jnp.bfloat16

D_MODEL = 1024
DEPTH = 2
A_HEADS = 8
A_KV_HEADS = 2
A_GROUP = A_HEADS // A_KV_HEADS
A_HEAD_DIM = 64
WINDOW = 128
ATTN_SCALE = A_HEAD_DIM ** -0.5
NUM_BUCKETS = 32
MAX_DISTANCE = 128
A_Q = A_HEADS * A_HEAD_DIM
A_KV = A_KV_HEADS * A_HEAD_DIM
B_GROUPS = 8
B_GROUP_DIM = 64
B_WIDTH = B_GROUPS * B_GROUP_DIM
B_CHUNK = 128
AB_IN = A_Q + 2 * A_KV + 2 * B_WIDTH
AB_MIX = A_Q + B_WIDTH
C_HEADS = 8
C_KEY_DIM = 128
C_VAL_DIM = 128
C_F = C_HEADS * C_KEY_DIM
C_V = C_HEADS * C_VAL_DIM
C_IN = 2 * C_F + 2 * C_V
D_FF = 2816
EPS = 1e-6
LOG2E = math.log2(math.e)

NEG = -1e30

VMEM_LIMIT = 56 * 1024 * 1024
VREG_ROWS = 8

CHUNK = 128
TQ = 512
TQH = 512
TM = 512
HEAD_SKEW = 2
SAFE_LOG2_RANGE = 64.0
PREFIX_GROUP = 4
FACTORED_UNROLL = 4

_NT = (((1,), (1,)), ((), ()))
_TN = (((0,), (0,)), ((), ()))


def _rms(x, g):
    return x * lax.rsqrt(jnp.mean(x * x, axis=-1, keepdims=True) + EPS) * g


def _gelu(x):
    return 0.5 * x * (1.0 + lax.erf(x * math.sqrt(0.5)))


def _layernorm(x, g, b):
    xc = x - jnp.mean(x, axis=-1, keepdims=True)
    return xc * lax.rsqrt(jnp.mean(xc * xc, axis=-1, keepdims=True) + EPS) * g + b


def _dot(a, b):
    return jnp.dot(a, b, preferred_element_type=F32)


def _full(shape):
    n = len(shape)
    return pl.BlockSpec(shape, lambda *_: (0,) * n)


def _resident(shape):
    n = len(shape)
    return pl.BlockSpec(shape, lambda *_: (0,) * n, pipeline_mode=pl.Buffered(1))


_SMEM = pl.BlockSpec(memory_space=pltpu.SMEM)


def _bias_table_kernel(rel_ref, sb_ref, tabp_ref):
    qi = lax.broadcasted_iota(jnp.int32, (WINDOW, 2 * WINDOW), 0)
    kj = lax.broadcasted_iota(jnp.int32, (WINDOW, 2 * WINDOW), 1)
    dist = qi + WINDOW - kj
    ok = (dist >= 0) & (dist < WINDOW)
    max_exact = NUM_BUCKETS // 2
    d = jnp.maximum(dist, 0)
    dl = jnp.maximum(d, 1).astype(F32)
    v = (jnp.log(dl / max_exact) / math.log(MAX_DISTANCE / max_exact) * (NUM_BUCKETS - max_exact))
    far = d >= max_exact
    hits = []
    for b in range(NUM_BUCKETS):
        if b < max_exact:
            hits.append(d == b)
        elif b < NUM_BUCKETS - 1:
            hits.append(far & (v >= b - max_exact) & (v < b - max_exact + 1))
        else:
            hits.append(far & (v >= b - max_exact))
    for h in range(A_HEADS):
        acc = jnp.zeros((WINDOW, 2 * WINDOW), F32)
        for b in range(NUM_BUCKETS):
            acc = jnp.where(hits[b], rel_ref[b, h], acc)
        t = jnp.where(ok, acc, NEG)
        last = pltpu.roll(t[WINDOW - 1:WINDOW, :], WINDOW + 1, 1)
        sb_ref[h:h + 1, :] = jnp.where(kj[0:1, :] <= WINDOW, last, 0.0)
        cols = slice((h % 2) * 2 * WINDOW, (h % 2 + 1) * 2 * WINDOW)
        tabp_ref[0, h // 2, :, cols] = t * LOG2E
        tabp_ref[1, h // 2, :, cols] = jnp.where(kj < WINDOW, NEG, t * LOG2E)


PAIR = 2 * A_HEAD_DIM
N_PAIRS = A_HEADS // 2


def _sample_cache_attention(ck_ref, cv_ref, qx_ref, kn_ref, vn_ref, sb_ref, sink_ref,
                            nk_ref, nv_ref, om_ref):
    wb = ck_ref.shape[2]
    first_group = lax.broadcasted_iota(jnp.int32, (1, A_HEADS, A_HEAD_DIM), 1) < A_GROUP
    newest = lax.broadcasted_iota(jnp.int32, (1, 1, wb), 2) == wb - 1
    sink = sink_ref[...][None]
    kc, vc = ck_ref[...], cv_ref[...]
    kn, vn = kn_ref[...], vn_ref[...]
    kn_cols, vn_cols = kn_ref[:, 0, :].T, vn_ref[:, 0, :].T
    for i in range(kc.shape[0]):
        nk_ref[i] = jnp.where(newest[0], kn_cols[:, i:i + 1], pltpu.roll(kc[i], wb - 1, 1))
        nv_ref[i] = jnp.where(newest[0], vn_cols[:, i:i + 1], pltpu.roll(vc[i], wb - 1, 1))
    q = qx_ref[...]
    s = jnp.einsum('bhd,bdk->bhk', q.astype(BF16), kc.astype(BF16), preferred_element_type=F32)
    s = s + sb_ref[:, 0:wb][None]
    sn = jnp.sum(q * kn, axis=-1, keepdims=True) + sb_ref[:, wb:wb + 1][None]
    m = jnp.maximum(jnp.maximum(jnp.max(s, axis=-1, keepdims=True), sn), sink)
    e = jnp.exp(s - m)
    en = jnp.exp(sn - m)
    r = 1.0 / (jnp.sum(e, axis=-1, keepdims=True) + en + jnp.exp(sink - m))
    o = jnp.einsum('bhk,bdk->bhd', (e * r).astype(BF16), vc.astype(BF16),
                   preferred_element_type=F32) + (en * r) * vn
    om_ref[...] = jnp.where(first_group, o[:, :, 0:A_HEAD_DIM], o[:, :, A_HEAD_DIM:A_KV])


def _ab_prompt_kernel(sink_ref, x_ref, nm_ref, win_ref, qg_ref, kg_ref, tabp_ref, lng_ref, lnb_ref,
                      ws_ref, bsp_ref, wout_ref,
                      ck_ref, cv_ref, qx_ref, kn_ref, vn_ref, sb_ref, sinkc_ref,
                      y_ref, knew_ref, vnew_ref, gv_ref, nk_ref, nv_ref, om_ref,
                      z_ref, mix_ref, q_ref, k_ref, kr_ref, v_ref, vr_ref, wpair_ref, kl_ref, vl_ref, gl_ref):
    j = pl.program_id(1)
    last_j = pl.num_programs(1) - 1
    n_chunks = TQ // CHUNK

    @pl.when(j == 0)
    def _():
        for ref in (k_ref, kr_ref, v_ref, vr_ref):
            ref[0:CHUNK, :] = jnp.zeros((CHUNK, A_KV), BF16)
        row = lax.broadcasted_iota(jnp.int32, (B_CHUNK, B_CHUNK), 0)
        col = lax.broadcasted_iota(jnp.int32, (B_CHUNK, B_CHUNK), 1)
        for g in range(B_GROUPS):
            wpair_ref[g // 2, :, (g % 2) * B_CHUNK:(g % 2 + 1) * B_CHUNK] = jnp.where(
                row >= col, ws_ref[g], 0.0).astype(BF16)

    h = _rms(x_ref[0], nm_ref[...]).astype(BF16)
    z_ref[...] = _dot(h, win_ref[...])

    _sample_cache_attention(ck_ref, cv_ref, qx_ref, kn_ref, vn_ref, sb_ref, sinkc_ref,
                            nk_ref, nv_ref, om_ref)

    lo_half = lax.broadcasted_iota(jnp.int32, (1, PAIR), 1) < A_HEAD_DIM

    def mean_sq_halves(x):
        x2 = x * x
        lo_sum = jnp.sum(jnp.where(lo_half, x2, 0.0), axis=-1, keepdims=True)
        hi_sum = jnp.sum(jnp.where(lo_half, 0.0, x2), axis=-1, keepdims=True)
        return jnp.where(lo_half, lo_sum, hi_sum) * (1.0 / A_HEAD_DIM)

    def block_diag(top, bot):
        zero = jnp.zeros_like(top)
        return jnp.concatenate([jnp.where(lo_half, top, zero), jnp.where(lo_half, zero, bot)], axis=0)

    kraw = z_ref[:, A_Q:A_Q + A_KV]
    v_all = z_ref[:, A_Q + A_KV:A_Q + 2 * A_KV]
    kn_all = kraw * lax.rsqrt(mean_sq_halves(kraw) + EPS) * kg_ref[...]
    k_ref[CHUNK:, :] = kn_all.astype(BF16)
    kr_ref[CHUNK:, :] = pltpu.roll(kn_all, A_HEAD_DIM, 1).astype(BF16)
    v_ref[CHUNK:, :] = v_all.astype(BF16)
    vr_ref[CHUNK:, :] = pltpu.roll(v_all, A_HEAD_DIM, 1).astype(BF16)
    kl_ref[...] = kn_all[TQ - CHUNK:, :]
    vl_ref[...] = v_all[TQ - CHUNK:, :]
    for i in range(N_PAIRS):
        ps = slice(i * PAIR, (i + 1) * PAIR)
        qraw = z_ref[:, ps]
        qn = qraw * lax.rsqrt(mean_sq_halves(qraw) + EPS) * (qg_ref[:, ps] * (ATTN_SCALE * LOG2E))
        q_ref[:, ps] = qn.astype(BF16)

    def chunk(c, carry):
        r0 = c * CHUNK
        rows = pl.ds(r0, CHUNK)
        first = jnp.where(jnp.logical_and(j == 0, c == 0), 1, 0)

        both = pl.ds(r0, 2 * CHUNK)
        k2, k2r, v2, v2r = k_ref[both, :], kr_ref[both, :], v_ref[both, :], vr_ref[both, :]
        kbd = [block_diag(k2, k2r), block_diag(k2r, k2)]
        vbd = [block_diag(v2, v2r), block_diag(v2r, v2)]

        scores = []
        for i in range(N_PAIRS):
            s = lax.dot_general(q_ref[rows, i * PAIR:(i + 1) * PAIR], kbd[i // (A_GROUP // 2)], _NT,
                                preferred_element_type=F32)
            scores.append(s + tabp_ref[first, i])
        outs = []
        for i in range(N_PAIRS):
            es, rs = [], []
            for hh in range(2):
                sh = scores[i][:, hh * 2 * WINDOW:(hh + 1) * 2 * WINDOW]
                sk = sink_ref[2 * i + hh] * LOG2E
                m = jnp.maximum(jnp.max(sh, axis=-1, keepdims=True), sk)
                e = jnp.exp2(sh - m)
                rs.append(1.0 / (jnp.sum(e, axis=-1, keepdims=True) + jnp.exp2(sk - m)))
                es.append(e.astype(BF16))
            o = _dot(jnp.concatenate(es, axis=-1), vbd[i // (A_GROUP // 2)])
            outs.append(o * jnp.where(lo_half, rs[0], rs[1]))
        mix_ref[rows, 0:A_Q] = jnp.concatenate(outs, axis=-1).astype(BF16)

        zu = z_ref[rows, A_Q + 2 * A_KV:A_Q + 2 * A_KV + B_WIDTH]
        zv = z_ref[rows, A_Q + 2 * A_KV + B_WIDTH:AB_IN]
        u = _gelu(zu)
        vln = _layernorm(_gelu(zv), lng_ref[...], lnb_ref[...])
        vlb = vln.astype(BF16)
        sparts = []
        for i in range(B_GROUPS // 2):
            vpair = vlb[:, i * PAIR:(i + 1) * PAIR]
            sparts.append(_dot(wpair_ref[i], block_diag(vpair, vpair)))
        bm = u * (jnp.concatenate(sparts, axis=-1) + bsp_ref[...])
        mix_ref[rows, A_Q:AB_MIX] = bm.astype(BF16)

        gl_ref[...] = vln
        return carry

    for c in range(n_chunks):
        chunk(c, 0)
        done = slice(c * CHUNK, (c + 1) * CHUNK)
        y_ref[0, done, :] = x_ref[0, done, :] + _dot(mix_ref[done, :], wout_ref[...])
    for ref in (k_ref, kr_ref, v_ref, vr_ref):
        ref[0:CHUNK, :] = ref[TQ:TQ + CHUNK, :]

    @pl.when(j == last_j)
    def _():
        knew_ref[0] = kl_ref[...]
        vnew_ref[0] = vl_ref[...]
        gv_ref[0] = gl_ref[...]


def _ab_prompt(x, nm, w_in, qg, kg, sink, tabp, lng, lnb, w_s, bsp, w_out, ck, cv, qx, kn_s, vn_s, sb):
    nb, seq, _ = x.shape
    nj = seq // TQ
    grid = (nb, nj)
    ns, _, wb = ck.shape
    sba = ns // (nb * nj)
    assert sba * nb * nj == ns
    blk = lambda b, j: (b, j, 0)
    per_b = lambda b, j: (b, 0, 0)
    step = lambda b, j: (b * nj + j, 0, 0)
    return pl.pallas_call(
        _ab_prompt_kernel,
        grid=grid,
        in_specs=[
            _SMEM,
            pl.BlockSpec((1, TQ, D_MODEL), blk),
            _full((1, D_MODEL)),
            _resident((D_MODEL, AB_IN)),
            _full((1, A_Q)),
            _full((1, A_KV)),
            _resident((2, N_PAIRS, WINDOW, 4 * WINDOW)),
            _full((1, B_WIDTH)),
            _full((1, B_WIDTH)),
            _resident((B_GROUPS, B_CHUNK, B_CHUNK)),
            _resident((B_CHUNK, B_WIDTH)),
            _resident((AB_MIX, D_MODEL)),
            pl.BlockSpec((sba, A_KV, wb), step),
            pl.BlockSpec((sba, A_KV, wb), step),
            pl.BlockSpec((sba, A_HEADS, A_KV), step),
            pl.BlockSpec((sba, 1, A_KV), step),
            pl.BlockSpec((sba, 1, A_KV), step),
            _full((A_HEADS, 2 * WINDOW)),
            _full((A_HEADS, 1)),
        ],
        out_specs=[
            pl.BlockSpec((1, TQ, D_MODEL), blk),
            pl.BlockSpec((1, WINDOW, A_KV), per_b),
            pl.BlockSpec((1, WINDOW, A_KV), per_b),
            pl.BlockSpec((1, B_CHUNK, B_WIDTH), per_b),
            pl.BlockSpec((sba, A_KV, wb), step),
            pl.BlockSpec((sba, A_KV, wb), step),
            pl.BlockSpec((sba, A_HEADS, A_HEAD_DIM), step),
        ],
        out_shape=[
            jax.ShapeDtypeStruct((nb, seq, D_MODEL), F32),
            jax.ShapeDtypeStruct((nb, WINDOW, A_KV), F32),
            jax.ShapeDtypeStruct((nb, WINDOW, A_KV), F32),
            jax.ShapeDtypeStruct((nb, B_CHUNK, B_WIDTH), F32),
            jax.ShapeDtypeStruct((ns, A_KV, wb), F32),
            jax.ShapeDtypeStruct((ns, A_KV, wb), F32),
            jax.ShapeDtypeStruct((ns, A_HEADS, A_HEAD_DIM), F32),
        ],
        scratch_shapes=[
            pltpu.VMEM((TQ, AB_IN), F32),
            pltpu.VMEM((TQ, AB_MIX), BF16),
            pltpu.VMEM((TQ, A_Q), BF16),
            pltpu.VMEM((CHUNK + TQ, A_KV), BF16),
            pltpu.VMEM((CHUNK + TQ, A_KV), BF16),
            pltpu.VMEM((CHUNK + TQ, A_KV), BF16),
            pltpu.VMEM((CHUNK + TQ, A_KV), BF16),
            pltpu.VMEM((B_GROUPS // 2, B_CHUNK, 2 * B_CHUNK), BF16),
            pltpu.VMEM((WINDOW, A_KV), F32),
            pltpu.VMEM((WINDOW, A_KV), F32),
            pltpu.VMEM((B_CHUNK, B_WIDTH), F32),
        ],
        compiler_params=pltpu.CompilerParams(
            dimension_semantics=("arbitrary", "arbitrary"), vmem_limit_bytes=VMEM_LIMIT),
        name="ab_prompt",
    )(sink, x, nm, w_in, qg, kg, tabp, lng, lnb, w_s, bsp, w_out,
      ck, cv, qx, kn_s[:, None, :], vn_s[:, None, :], sb, sink.reshape(A_HEADS, 1))


FF_TILE = 256


def _ffn_kernel(xp_ref, xs_ref, ms_ref, wo_ref, g_ref, wg_ref, wu_ref, wd_ref, yp_ref, ys_ref,
                wg_s, wu_s, wd_s, h0_s, acc_s, *, n_cast, n_prompt):
    s = pl.program_id(0)

    def gated(h, wg, wu):
        gate = _dot(h, wg)
        return (gate * jax.nn.sigmoid(gate) * _dot(h, wu)).astype(BF16)

    @pl.when(s == 0)
    def _():
        x = xp_ref[...]
        h0_s[...] = _rms(x, g_ref[...]).astype(BF16)
        acc_s[...] = x

    for c in range(n_cast):
        @pl.when(s == c)
        def _(c=c):
            tile = slice(c * FF_TILE, (c + 1) * FF_TILE)
            wg_t, wu_t, wd_t = (r[...].astype(BF16) for r in (wg_ref, wu_ref, wd_ref))
            wg_s[:, tile] = wg_t
            wu_s[:, tile] = wu_t
            wd_s[tile, :] = wd_t
            acc_s[...] += _dot(gated(h0_s[...], wg_t, wu_t), wd_t)

    @pl.when(s == n_cast - 1)
    def _():
        yp_ref[...] = acc_s[...]

    def swiglu(x):
        h = _rms(x, g_ref[...]).astype(BF16)
        return x + _dot(gated(h, wg_s[...], wu_s[...]), wd_s[...])

    @pl.when(jnp.logical_and(s >= n_cast, s < n_cast + n_prompt - 1))
    def _():
        yp_ref[...] = swiglu(xp_ref[...])

    @pl.when(s == n_cast + n_prompt - 1)
    def _():
        ys_ref[...] = swiglu(xs_ref[...] + _dot(ms_ref[...].astype(BF16), wo_ref[...]))


def _ffn(xp, xs, mix_s, w_o, g, w_gate, w_up, w_down, layer):
    rows, ns = xp.shape[0], xs.shape[0]
    n_cast, n_prompt = D_FF // FF_TILE, rows // TM
    w_tile = lambda s: jnp.minimum(s, n_cast - 1)
    row_blk = lambda s: (jnp.clip(s - (n_cast - 1), 0, n_prompt - 1), 0)
    return pl.pallas_call(
        functools.partial(_ffn_kernel, n_cast=n_cast, n_prompt=n_prompt),
        grid=(n_cast + n_prompt,),
        in_specs=[
            pl.BlockSpec((TM, D_MODEL), row_blk),
            _full((ns, D_MODEL)),
            _full((ns, D_MODEL)),
            _resident((D_MODEL, D_MODEL)),
            _full((1, D_MODEL)),
            pl.BlockSpec((None, D_MODEL, FF_TILE), lambda s: (layer, 0, w_tile(s))),
            pl.BlockSpec((None, D_MODEL, FF_TILE), lambda s: (layer, 0, w_tile(s))),
            pl.BlockSpec((None, FF_TILE, D_MODEL), lambda s: (layer, w_tile(s), 0)),
        ],
        out_specs=[pl.BlockSpec((TM, D_MODEL), row_blk), _full((ns, D_MODEL))],
        out_shape=[jax.ShapeDtypeStruct((rows, D_MODEL), F32), jax.ShapeDtypeStruct((ns, D_MODEL), F32)],
        scratch_shapes=[
            pltpu.VMEM((D_MODEL, D_FF), BF16),
            pltpu.VMEM((D_MODEL, D_FF), BF16),
            pltpu.VMEM((D_FF, D_MODEL), BF16),
            pltpu.VMEM((TM, D_MODEL), BF16),
            pltpu.VMEM((TM, D_MODEL), F32),
        ],
        compiler_params=pltpu.CompilerParams(
            dimension_semantics=("arbitrary",), vmem_limit_bytes=VMEM_LIMIT),
        name="ffn",
    )(xp, xs, mix_s, w_o, g, w_gate, w_up, w_down)


def _lower_bound(clb):
    m = jnp.max(clb, axis=0, keepdims=True)
    e = jnp.exp(clb - m)
    sm = e / jnp.sum(e, axis=0, keepdims=True)
    return (sm[0:1] + sm[1:2]) - sm[0:1]


def _split3(x):
    hi = x.astype(BF16)
    r = x - hi.astype(F32)
    mid = r.astype(BF16)
    lo = (r - mid.astype(F32)).astype(BF16)
    return hi, mid, lo


def _neg_abs(x):
    return lax.bitcast_convert_type(
        lax.bitcast_convert_type(x, jnp.uint32) | jnp.uint32(0x80000000), F32)


def _pair_level_table():
    t = np.arange(CHUNK)[:, None]
    s = np.arange(CHUNK)[None, :]
    lev = np.floor(np.log2(np.maximum(t ^ s, 1))).astype(np.int32)
    lev = np.where(t == s, -1, lev)
    return np.where(s > t, -2, lev).astype(np.int32)


def _level_operand(p, q, kk, f, b2):
    m = 2 ** p
    if m < VREG_ROWS:
        shape3 = (CHUNK // VREG_ROWS, VREG_ROWS, q.shape[1])
        sub = lax.broadcasted_iota(jnp.int32, (1, VREG_ROWS, q.shape[1]), 1)
        upper = ((sub >> p) & 1) == 1
        q3, k3 = q.reshape(shape3), kk.reshape(shape3)
        if p == 0:
            y = jnp.where(upper, q3 * f.reshape(shape3), k3)
        else:
            b3 = b2.reshape(shape3)
            be = b3[:, m - 1:m, :]
            for k in range(1, VREG_ROWS // (2 * m)):
                be = jnp.where(sub >= 2 * m * k, b3[:, 2 * m * k + m - 1:2 * m * k + m, :], be)
            y = jnp.where(upper, q3, k3) * jnp.exp2(_neg_abs(b3 - be))
        return y.reshape(q.shape).astype(BF16)
    parts = []
    for k in range(CHUNK // (2 * m)):
        lo = slice(2 * m * k, 2 * m * k + m)
        up = slice(2 * m * k + m, 2 * m * (k + 1))
        be = b2[2 * m * k + m - 1:2 * m * k + m, :]
        parts.append(kk[lo] * jnp.exp2(be - b2[lo]))
        parts.append(q[up] * jnp.exp2(b2[up] - be))
    return jnp.concatenate(parts, axis=0).astype(BF16)


def _merge_level(p, att, pm, lev):
    m = 2 ** p
    if m < VREG_ROWS:
        return jnp.where(lev == p, pm, att)
    col = lax.broadcasted_iota(jnp.int32, (1, CHUNK), 1)
    parts = []
    for k in range(CHUNK // (2 * m)):
        lo = slice(2 * m * k, 2 * m * k + m)
        up = slice(2 * m * k + m, 2 * m * (k + 1))
        parts.append(att[lo])
        parts.append(jnp.where((col >= 2 * m * k) & (col < 2 * m * k + m), pm[up], att[up]))
    return jnp.concatenate(parts, axis=0)


def _hgrn_prompt_kernel(x_ref, nm_ref, win_ref, clb_ref, on_ref, wout_ref, lev_ref,
                        ss_ref, fs_ref, qs_ref, is_ref, gs_ref,
                        y_ref, st_ref, sso_ref, os_ref,
                        z_ref, o_ref, stt_ref, k_ref):
    j = pl.program_id(1)
    last_j = pl.num_programs(1) - 1
    n_chunks = TQH // CHUNK
    n_levels = int(math.log2(CHUNK))

    @pl.when(j == 0)
    def _():
        stt_ref[...] = jnp.zeros_like(stt_ref)

    h = _rms(x_ref[0], nm_ref[...]).astype(BF16)
    z_ref[...] = _dot(h, win_ref[...])

    head_cols = lambda ref: [ref[:, hd * C_KEY_DIM:(hd + 1) * C_KEY_DIM].T for hd in range(C_HEADS)]
    f_cols, q_cols = head_cols(fs_ref), head_cols(qs_ref)
    out_rows = []
    for smp in range(ss_ref.shape[0]):
        parts = []
        for hd in range(C_HEADS):
            hs = slice(hd * C_VAL_DIM, (hd + 1) * C_VAL_DIM)
            fb = jnp.broadcast_to(f_cols[hd][:, smp:smp + 1], (C_KEY_DIM, C_VAL_DIM))
            sn = fb * ss_ref[smp, hd] + (1.0 - fb) * is_ref[smp:smp + 1, hs]
            sso_ref[smp, hd] = sn
            o = jnp.sum(q_cols[hd][:, smp:smp + 1] * sn, axis=0, keepdims=True)
            parts.append(_rms(o, on_ref[...]))
        out_rows.append(jnp.concatenate(parts, axis=-1))
    os_ref[...] = jnp.concatenate(out_rows, axis=0) * gs_ref[...]

    lb = _lower_bound(clb_ref[...])

    row = lax.broadcasted_iota(jnp.int32, (CHUNK, CHUNK), 0)
    col = lax.broadcasted_iota(jnp.int32, (CHUNK, CHUNK), 1)
    ltri = (row >= col).astype(BF16)

    def chunk_rows(c):
        return pl.ds(pl.multiple_of(c * CHUNK, CHUNK), CHUNK)

    def prefix(g, worst):
        rows = [chunk_rows(g * PREFIX_GROUP + i) for i in range(PREFIX_GROUP)]
        gates = [z_ref[r, C_F:2 * C_F] for r in rows]
        for r, gate in zip(rows, gates):
            f_all = lb + (1.0 - lb) * jax.nn.sigmoid(gate)
            k_ref[r, :] = 1.0 - f_all
            hi, mid, lo = _split3(jnp.log2(f_all))
            b2 = (_dot(ltri, hi) + _dot(ltri, mid)) + _dot(ltri, lo)
            z_ref[r, C_F:2 * C_F] = b2
            b_mid = b2[CHUNK // 2 - 1:CHUNK // 2, :]
            b_last = b2[CHUNK - 1:CHUNK, :]
            worst = jnp.maximum(worst, jnp.maximum(-b_mid, b_mid - b_last))
        return worst

    worst = lax.fori_loop(0, n_chunks // PREFIX_GROUP, prefix, jnp.zeros((1, C_F), F32))
    bounded = jnp.max(worst) <= SAFE_LOG2_RANGE

    def finish_head(rows, hd, o):
        gt = z_ref[rows, 2 * C_F + C_V + hd * C_VAL_DIM:2 * C_F + C_V + (hd + 1) * C_VAL_DIM]
        o = _rms(o, on_ref[...]) * jax.nn.sigmoid(gt)
        o_ref[rows, hd * C_VAL_DIM:(hd + 1) * C_VAL_DIM] = o.astype(BF16)

    def head_inputs(rows, hd):
        q = z_ref[rows, hd * C_KEY_DIM:(hd + 1) * C_KEY_DIM]
        kk = k_ref[rows, hd * C_KEY_DIM:(hd + 1) * C_KEY_DIM]
        b2 = z_ref[rows, C_F + hd * C_KEY_DIM:C_F + (hd + 1) * C_KEY_DIM]
        ivb = z_ref[rows, 2 * C_F + hd * C_VAL_DIM:2 * C_F + (hd + 1) * C_VAL_DIM].astype(BF16)
        return q, kk, b2, ivb

    def factored_chunk(c, carry):
        rows = chunk_rows(c)
        for hd in range(C_HEADS):
            q, kk, b2, ivb = head_inputs(rows, hd)
            b_mid = b2[CHUNK // 2 - 1:CHUNK // 2, :]
            b_last = b2[CHUNK - 1:CHUNK, :]
            qs = (q * jnp.exp2(b2 - b_mid)).astype(BF16)
            kd = (kk * jnp.exp2(b_mid - b2)).astype(BF16)
            att = jnp.where(row >= col, lax.dot_general(qs, kd, _NT, preferred_element_type=F32), 0.0)
            stt = stt_ref[hd]
            o = lax.dot_general(qs, (stt * jnp.exp2(b_mid)).astype(BF16), _NT,
                                preferred_element_type=F32) + _dot(att.astype(BF16), ivb)
            stt_ref[hd] = stt * jnp.exp2(b_last) + jnp.exp2(b_last - b_mid) * lax.dot_general(
                ivb, kd, _TN, preferred_element_type=F32)
            finish_head(rows, hd, o)
        return carry

    def tree_chunk(c, carry):
        rows = chunk_rows(c)
        lev = lev_ref[...]

        def products(hd):
            q, kk, b2, ivb = head_inputs(rows, hd)
            diag = jnp.sum(q * kk, axis=-1, keepdims=True)
            pms = []
            for p in range(n_levels):
                y = _level_operand(p, q, kk, 1.0 - kk, b2)
                pms.append(lax.dot_general(y, y, _NT, preferred_element_type=F32))
            stt = stt_ref[hd]
            o_prev = lax.dot_general((q * jnp.exp2(b2)).astype(BF16), stt.astype(BF16), _NT,
                                     preferred_element_type=F32)
            b_last = b2[CHUNK - 1:CHUNK, :]
            kd = (kk * jnp.exp2(b_last - b2)).astype(BF16)
            stt_ref[hd] = stt * jnp.exp2(b_last) + lax.dot_general(
                ivb, kd, _TN, preferred_element_type=F32)
            return diag, pms, o_prev, ivb

        def finish(hd, diag, pms, o_prev, ivb):
            att = jnp.where(lev == -1, diag, 0.0)
            for p in range(n_levels):
                att = _merge_level(p, att, pms[p], lev)
            finish_head(rows, hd, o_prev + _dot(att.astype(BF16), ivb))

        pending = [products(hd) for hd in range(HEAD_SKEW)]
        for hd in range(C_HEADS):
            if hd + HEAD_SKEW < C_HEADS:
                pending.append(products(hd + HEAD_SKEW))
            finish(hd, *pending.pop(0))
        return carry

    @pl.when(bounded)
    def _():
        lax.fori_loop(0, n_chunks, factored_chunk, 0, unroll=FACTORED_UNROLL)

    @pl.when(jnp.logical_not(bounded))
    def _():
        lax.fori_loop(0, n_chunks, tree_chunk, 0)

    y_ref[0] = x_ref[0] + _dot(o_ref[...], wout_ref[...])

    @pl.when(j == last_j)
    def _():
        for hd in range(C_HEADS):
            st_ref[0, hd] = stt_ref[hd].T


def _hgrn_prompt(x, nm, w_in, clb, on, w_out, state_s, f_s, q_s, i_s, g_s):
    nb, seq, _ = x.shape
    nj = seq // TQH
    ns = state_s.shape[0]
    sbh = f_s.shape[1]
    assert f_s.shape[0] == nb * nj and sbh * nb * nj == ns
    blk = lambda b, j: (b, j, 0)
    step = lambda b, j: (b * nj + j, 0, 0)
    step4 = lambda b, j: (b * nj + j, 0, 0, 0)
    y, st, st_s, o_s = pl.pallas_call(
        _hgrn_prompt_kernel,
        grid=(nb, nj),
        in_specs=[
            pl.BlockSpec((1, TQH, D_MODEL), blk),
            _full((1, D_MODEL)),
            _resident((D_MODEL, C_IN)),
            _full((DEPTH, C_F)),
            _full((1, C_VAL_DIM)),
            _resident((C_V, D_MODEL)),
            _full((CHUNK, CHUNK)),
            pl.BlockSpec((sbh, C_HEADS, C_KEY_DIM, C_VAL_DIM), step4),
            pl.BlockSpec((None, sbh, C_F), step),
            pl.BlockSpec((None, sbh, C_F), step),
            pl.BlockSpec((None, sbh, C_V), step),
            pl.BlockSpec((None, sbh, C_V), step),
        ],
        out_specs=[
            pl.BlockSpec((1, TQH, D_MODEL), blk),
            pl.BlockSpec((1, C_HEADS, C_KEY_DIM, C_VAL_DIM), lambda b, j: (b, 0, 0, 0)),
            pl.BlockSpec((sbh, C_HEADS, C_KEY_DIM, C_VAL_DIM), step4),
            pl.BlockSpec((None, sbh, C_V), step),
        ],
        out_shape=[
            jax.ShapeDtypeStruct((nb, seq, D_MODEL), F32),
            jax.ShapeDtypeStruct((nb, C_HEADS, C_KEY_DIM, C_VAL_DIM), F32),
            jax.ShapeDtypeStruct(state_s.shape, F32),
            jax.ShapeDtypeStruct((ns // sbh, sbh, C_V), F32),
        ],
        scratch_shapes=[
            pltpu.VMEM((TQH, C_IN), F32),
            pltpu.VMEM((TQH, C_V), BF16),
            pltpu.VMEM((C_HEADS, C_VAL_DIM, C_KEY_DIM), F32),
            pltpu.VMEM((TQH, C_F), F32),
        ],
        compiler_params=pltpu.CompilerParams(
            dimension_semantics=("arbitrary", "arbitrary"), vmem_limit_bytes=VMEM_LIMIT),
        name="hgrn_prompt",
    )(x, nm, w_in, clb, on, w_out, jnp.asarray(_pair_level_table()),
      state_s, f_s, q_s, i_s, g_s)
    return y, st, st_s, o_s.reshape(ns, C_V)


def _ab_sample_proj_kernel(w00_ref, b0_ref, rel_ref, x_ref, nm_ref, win_ref, qn_ref, kn_ref, lng_ref, lnb_ref,
                           wout_ref, qx_ref, knew_ref, vnew_ref, bm_ref, gv_ref, wbf_ref, sb_ref, tabp_ref,
                           woutbf_ref, h_s, z_s):
    t = pl.program_id(0)
    half = AB_IN // 2

    @pl.when(t == 0)
    def _():
        h_s[...] = _rms(x_ref[...], nm_ref[...]).astype(BF16)

    wbf_ref[...] = win_ref[...].astype(BF16)
    woutbf_ref[...] = wout_ref[...].astype(BF16)
    zt = _dot(h_s[...], wbf_ref[...])

    @pl.when(t == 0)
    def _():
        z_s[:, 0:half] = zt
        _bias_table_kernel(rel_ref, sb_ref, tabp_ref)

    @pl.when(t == 1)
    def _():
        z_s[:, half:AB_IN] = zt
        _ab_sample_post(w00_ref, b0_ref, z_s[...], qn_ref, kn_ref, lng_ref, lnb_ref,
                        qx_ref, knew_ref, vnew_ref, bm_ref, gv_ref)


def _ab_sample_post(w00_ref, b0_ref, z, qn_ref, kn_ref, lng_ref, lnb_ref,
                    qx_ref, knew_ref, vnew_ref, bm_ref, gv_ref):
    n = z.shape[0]
    zeros = jnp.zeros((n, A_HEAD_DIM), F32)
    for hh in range(A_HEADS):
        qh = _rms(z[:, hh * A_HEAD_DIM:(hh + 1) * A_HEAD_DIM], qn_ref[...]) * ATTN_SCALE
        qx_ref[:, hh, :] = jnp.concatenate([qh, zeros] if hh // A_GROUP == 0 else [zeros, qh], axis=-1)
    kparts = []
    for g in range(A_KV_HEADS):
        kparts.append(_rms(z[:, A_Q + g * A_HEAD_DIM:A_Q + (g + 1) * A_HEAD_DIM], kn_ref[...]))
    knew_ref[...] = jnp.concatenate(kparts, axis=-1)
    vnew_ref[...] = z[:, A_Q + A_KV:A_Q + 2 * A_KV]

    u = _gelu(z[:, A_Q + 2 * A_KV:A_Q + 2 * A_KV + B_WIDTH])
    vln = _layernorm(_gelu(z[:, A_Q + 2 * A_KV + B_WIDTH:AB_IN]), lng_ref[...], lnb_ref[...])
    grp = lax.broadcasted_iota(jnp.int32, (1, B_WIDTH), 1) // B_GROUP_DIM
    srow = jnp.zeros((1, B_WIDTH), F32)
    brow = b0_ref[0:1, :]
    for g in range(B_GROUPS):
        srow = jnp.where(grp == g, w00_ref[g, 0:1, 0:1], srow)
    bm_ref[...] = u * (vln * srow + brow)
    gv_ref[...] = vln


def _ab_sample_proj(x, nm, w_in, qn, kn, lng, lnb, w00, b0, rel_bias, w_out):
    n = x.shape[0]
    assert AB_IN % (2 * PAIR) == 0 and w_out.shape[0] % 2 == 0
    win_spec = pl.BlockSpec((D_MODEL, AB_IN // 2), lambda t: (0, t))
    wout_spec = pl.BlockSpec((w_out.shape[0] // 2, D_MODEL), lambda t: (t, 0))
    shapes = [(n, A_HEADS, A_KV), (n, A_KV), (n, A_KV), (n, B_WIDTH), (n, B_WIDTH)]
    tables = [(A_HEADS, 2 * WINDOW), (2, A_HEADS // 2, WINDOW, 4 * WINDOW)]
    return pl.pallas_call(
        _ab_sample_proj_kernel,
        grid=(2,),
        in_specs=[_full((B_GROUPS, VREG_ROWS, CHUNK)), _full((VREG_ROWS, B_WIDTH)), _SMEM,
                  _full((n, D_MODEL)), _full((1, D_MODEL)), win_spec,
                  _full(qn.shape), _full(kn.shape), _full(lng.shape), _full(lnb.shape), wout_spec],
        out_specs=[_full(s) for s in shapes] + [win_spec] + [_full(s) for s in tables] + [wout_spec],
        out_shape=[jax.ShapeDtypeStruct(s, F32) for s in shapes]
        + [jax.ShapeDtypeStruct(w_in.shape, BF16)]
        + [jax.ShapeDtypeStruct(s, F32) for s in tables]
        + [jax.ShapeDtypeStruct(w_out.shape, BF16)],
        scratch_shapes=[pltpu.VMEM((n, D_MODEL), BF16), pltpu.VMEM((n, AB_IN), F32)],
        compiler_params=pltpu.CompilerParams(
            dimension_semantics=("arbitrary",), vmem_limit_bytes=VMEM_LIMIT),
        name="ab_sample_proj",
    )(w00, b0, rel_bias, x, nm, w_in, qn, kn, lng, lnb, w_out)


def _hgrn_sample_proj_kernel(x_ref, nm_ref, win_ref, clb_ref, wout_ref, q_ref, f_ref, i_ref, sg_ref, wbf_ref,
                             woutbf_ref, h_s):
    t = pl.program_id(0)

    def put(ref, val):
        per = ref.shape[1]
        for i in range(ref.shape[0]):
            ref[i] = val[i * per:(i + 1) * per, :]

    @pl.when(t == 0)
    def _():
        h_s[...] = _rms(x_ref[...], nm_ref[...]).astype(BF16)

    wbf_ref[...] = win_ref[...].astype(BF16)
    woutbf_ref[...] = wout_ref[...].astype(BF16)
    z = _dot(h_s[...], wbf_ref[...])

    @pl.when(t == 0)
    def _():
        lb = _lower_bound(clb_ref[...])
        put(q_ref, z[:, 0:C_F])
        put(f_ref, lb + (1.0 - lb) * jax.nn.sigmoid(z[:, C_F:2 * C_F]))

    @pl.when(t == 1)
    def _():
        put(i_ref, z[:, 0:C_V])
        put(sg_ref, jax.nn.sigmoid(z[:, C_V:2 * C_V]))


def _hgrn_sample_proj(x, nm, w_in, clb, w_out, per_step):
    n = x.shape[0]
    assert C_F == C_V and C_IN == 4 * C_F and w_out.shape[0] % 2 == 0
    grouped = lambda width: jax.ShapeDtypeStruct((n // per_step, per_step, width), F32)
    wout_spec = pl.BlockSpec((w_out.shape[0] // 2, D_MODEL), lambda t: (t, 0))
    return pl.pallas_call(
        _hgrn_sample_proj_kernel,
        grid=(2,),
        in_specs=[
            _full((n, D_MODEL)),
            _full((1, D_MODEL)),
            pl.BlockSpec((D_MODEL, C_IN // 2), lambda t: (0, t)),
            _full((DEPTH, C_F)),
            wout_spec,
        ],
        out_specs=[_full((n // per_step, per_step, C_F))] * 2 + [_full((n // per_step, per_step, C_V))] * 2
        + [pl.BlockSpec((D_MODEL, C_IN // 2), lambda t: (0, t)), wout_spec],
        out_shape=[grouped(C_F)] * 2 + [grouped(C_V)] * 2
        + [jax.ShapeDtypeStruct(w_in.shape, BF16), jax.ShapeDtypeStruct(w_out.shape, BF16)],
        scratch_shapes=[pltpu.VMEM((n, D_MODEL), BF16)],
        compiler_params=pltpu.CompilerParams(
            dimension_semantics=("arbitrary",), vmem_limit_bytes=VMEM_LIMIT),
        name="hgrn_sample_proj",
    )(x, nm, w_in, clb, w_out)


def kernel(x_prompt, x_sample, cache_k, cache_v, state_hgrn, norm_mix, norm_ffn, w_in_ab, w_out_ab,
           q_norm, k_norm, attn_sink, rel_bias, gmlp_ln_g, gmlp_ln_b, gmlp_w_s, gmlp_b_s,
           w_in_c, c_lower_bounds, c_out_norm, w_out_c, w_gate, w_up, w_down):
    assert norm_mix.shape[0] == DEPTH == 2 and w_in_ab.shape[0] == 1 and w_in_c.shape[0] == 1
    nb, seq, _ = x_prompt.shape
    ns = x_sample.shape[0]
    assert x_sample.shape[1] == 1 and cache_k.shape[2] == WINDOW

    row = lambda v: v.reshape(1, -1)
    nm, nf = norm_mix, norm_ffn
    qn, kn = row(q_norm[0]), row(k_norm[0])
    lng, lnb = row(gmlp_ln_g[0]), row(gmlp_ln_b[0])
    sink = attn_sink[0]

    xs = x_sample.reshape(ns, D_MODEL)
    bsp = jnp.repeat(gmlp_b_s[0].T, B_GROUP_DIM, axis=1)
    qx, knew_s, vnew_s, bm_s, gv_s, w_in_ab0, sb, tabp, w_out_ab0 = _ab_sample_proj(
        xs, row(nm[0]), w_in_ab[0], qn, kn, lng, lnb, gmlp_w_s[0], bsp, rel_bias, w_out_ab[0])
    to_t = lambda c: c.transpose(0, 2, 3, 1).reshape(ns, A_KV, WINDOW)
    xp, knew_p, vnew_p, gv_p, nk_s, nv_s, om = _ab_prompt(
        x_prompt, row(nm[0]), w_in_ab0, jnp.tile(qn, (1, A_HEADS)), jnp.tile(kn, (1, A_KV_HEADS)),
        sink, tabp, lng, lnb, gmlp_w_s[0], bsp, w_out_ab0,
        to_t(cache_k[0]), to_t(cache_v[0]), qx, knew_s, vnew_s, sb)
    a_s = om.reshape(ns, A_Q)
    xp, xs = _ffn(xp.reshape(nb * seq, D_MODEL), xs, jnp.concatenate([a_s, bm_s], axis=-1), w_out_ab0,
                  row(nf[0]), w_gate, w_up, w_down, 0)

    q_s, f_s, i_s, sg_s, w_in_c0, w_out_c0 = _hgrn_sample_proj(
        xs, row(nm[1]), w_in_c[0], c_lower_bounds, w_out_c[0], ns // (nb * (seq // TQH)))
    xp, st_p, st_s, o_s = _hgrn_prompt(xp.reshape(nb, seq, D_MODEL), row(nm[1]), w_in_c0, c_lower_bounds,
                                       row(c_out_norm[0]), w_out_c0, state_hgrn[0], f_s, q_s, i_s, sg_s)
    xp, xs = _ffn(xp.reshape(nb * seq, D_MODEL), xs, o_s, w_out_c0, row(nf[1]), w_gate, w_up, w_down, 1)

    kv5 = lambda a: a.reshape(1, a.shape[0], WINDOW, A_KV_HEADS, A_HEAD_DIM)
    from_t = lambda a: a.reshape(ns, A_KV_HEADS, A_HEAD_DIM, WINDOW).transpose(0, 3, 1, 2)[None]
    return (xp.reshape(nb, seq, D_MODEL), xs.reshape(ns, 1, D_MODEL),
            kv5(knew_p), kv5(vnew_p), from_t(nk_s), from_t(nv_s),
            gv_p[None], gv_s.reshape(1, ns, 1, B_WIDTH),
            st_p[None], st_s[None])
```
